```python
import jax, jax.numpy as jnp
from jax import lax
import numpy as np

D_MODEL = 1024
BATCH = 8
SEQ = 2048
DEPTH = 4

GRID_W = 64
N_HEADS = 8
HEAD_DIM = 64
D_ATTN = N_HEADS * HEAD_DIM
KH_MAX = 8
KW = 16
Q_BLOCK = KW
K_BLOCK = 2 * KW
POOL_WINDOWS = (2, 4, 8, 16)
POOL_GROUPS = len(POOL_WINDOWS)
D_POOL = 512
POOL_GROUP_DIM = D_POOL // POOL_GROUPS
D_FF = 2816
CONV_W = 3
PLE_DIM = 256
N_PROJ = 3 * D_ATTN + D_POOL + 2 * D_MODEL
ALPHA = (2 * DEPTH) ** 0.25
BETA = (8 * DEPTH) ** -0.25
LN_EPS = 1e-5
NEG_INF = -1e30

kernel_name = "hybrid_natten_pool_convffn_encoder"


def layer_norm(x, g, b):
    xf = x.astype(jnp.float32)
    mu = jnp.mean(xf, axis=-1, keepdims=True)
    xc = xf - mu
    var = jnp.mean(xc * xc, axis=-1, keepdims=True)
    y = xc * lax.rsqrt(var + LN_EPS)
    return (y * g.astype(jnp.float32) + b.astype(jnp.float32)).astype(x.dtype)


def neighbourhood_attention(q, k, v, rpb):
    b, s, _ = q.shape
    rows = s // GRID_W
    kh = min(KH_MAX, rows)

    def to_grid(t):
        return t.reshape(b, rows, GRID_W, N_HEADS, HEAD_DIM).transpose(0, 3, 1, 2, 4)

    q, k, v = to_grid(q), to_grid(k), to_grid(v)
    r = np.arange(rows)
    row_start = np.clip(r - kh // 2, 0, rows - kh)
    row_idx = row_start[:, None] + np.arange(kh)
    k_rows = k[:, :, row_idx]
    v_rows = v[:, :, row_idx]
    row_off = row_idx - r[:, None] + KH_MAX - 1
    scale = HEAD_DIM ** -0.5
    outs = []
    for c0q in range(0, GRID_W, Q_BLOCK):
        c0k = int(np.clip(c0q - KW // 2, 0, GRID_W - K_BLOCK))
        qc = c0q + np.arange(Q_BLOCK)
        kc = c0k + np.arange(K_BLOCK)
        col_start = np.clip(qc - KW // 2, 0, GRID_W - KW)
        valid = (kc[None, :] >= col_start[:, None]) & (kc[None, :] < col_start[:, None] + KW)
        col_off = np.clip(kc[None, :] - qc[:, None], -(KW - 1), KW - 1) + KW - 1
        bias = rpb[:, row_off[:, None, :, None], col_off[None, :, None, :]]
        qb = q[:, :, :, c0q:c0q + Q_BLOCK]
        kb = k_rows[:, :, :, :, c0k:c0k + K_BLOCK]
        vb = v_rows[:, :, :, :, c0k:c0k + K_BLOCK]
        sc = jnp.einsum('bhrqd,bhrikd->bhrqik', qb, kb).astype(jnp.float32) * scale + bias.astype(jnp.float32)
        sc = jnp.where(valid[:, None, :], sc, NEG_INF)
        pr = jax.nn.softmax(sc.reshape(b, N_HEADS, rows, Q_BLOCK, kh * K_BLOCK), axis=-1)
        pr = pr.reshape(sc.shape).astype(v.dtype)
        outs.append(jnp.einsum('bhrqik,bhrikd->bhrqd', pr, vb))
    o = jnp.concatenate(outs, axis=3)
    return o.transpose(0, 2, 3, 1, 4).reshape(b, s, D_ATTN)


def multiscale_pool(u):
    b, s, _ = u.shape
    uf = u.astype(jnp.float32)
    cs = jnp.concatenate([jnp.zeros((b, 1, D_POOL), jnp.float32), jnp.cumsum(uf, axis=1)], axis=1)
    t = np.arange(s)
    outs = []
    for g, w in enumerate(POOL_WINDOWS):
        lo = np.clip(t - w // 2, 0, s)
        hi = np.clip(t + w // 2, 0, s)
        cnt = (hi - lo).astype(np.float32)
        sl = slice(g * POOL_GROUP_DIM, (g + 1) * POOL_GROUP_DIM)
        csg = cs[:, :, sl]
        mean = (csg[:, hi] - csg[:, lo]) / cnt[None, :, None]
        outs.append(mean - uf[:, :, sl])
    return jnp.concatenate(outs, axis=-1).astype(u.dtype)


def dwconv_centred(h, w, bias):
    s = h.shape[1]
    pad = CONV_W // 2
    hp = jnp.pad(h, ((0, 0), (pad, CONV_W - 1 - pad), (0, 0)))
    y = hp[:, 0:s] * w[0]
    for j in range(1, CONV_W):
        y = y + hp[:, j:j + s] * w[j]
    return y + bias


def _fwd_setup_inputs(seed: int = 0) -> dict:
    key = jax.random.key(seed)
    ks = jax.random.split(key, 24)
    f32 = jnp.float32
    nrm = lambda k, shape, sc: jax.random.normal(k, shape, f32) * sc
    L, D = DEPTH, D_MODEL
    w_in = jnp.concatenate([
        nrm(ks[0], (L, D, D_ATTN), D ** -0.5),
        nrm(ks[1], (L, D, D_ATTN), D ** -0.5),
        nrm(ks[2], (L, D, D_ATTN), D ** -0.5 * BETA),
        nrm(ks[3], (L, D, D_POOL), D ** -0.5),
        nrm(ks[4], (L, D, 2 * D), D ** -0.5),
    ], axis=-1)
    return {
        "x": jax.random.normal(ks[5], (BATCH, SEQ, D), f32),
        "p": jax.random.normal(ks[6], (DEPTH, BATCH, SEQ, PLE_DIM), f32),
        "ln_in_g": 1.0 + nrm(ks[7], (D,), 0.02),
        "ln_in_b": nrm(ks[8], (D,), 0.02),
        "w_in": w_in,
        "b_in": nrm(ks[9], (L, N_PROJ), 0.01),
        "rpb": nrm(ks[10], (L, N_HEADS, 2 * KH_MAX - 1, 2 * KW - 1), 0.02),
        "w_attn_out": nrm(ks[11], (L, D_ATTN, D), D_ATTN ** -0.5 * BETA),
        "pool_w": nrm(ks[12], (L, POOL_GROUPS, POOL_GROUP_DIM, POOL_GROUP_DIM), POOL_GROUP_DIM ** -0.5),
        "pool_scale": 1.0 + nrm(ks[13], (L, D_POOL), 0.02),
        "w_pool_out": nrm(ks[14], (L, D_POOL, D), D_POOL ** -0.5 * BETA),
        "w_mix_out": nrm(ks[15], (L, D, D), D ** -0.5 * BETA),
        "ln1_g": 1.0 + nrm(ks[16], (L, D), 0.02),
        "ln1_b": nrm(ks[17], (L, D), 0.02),
        "w_up": nrm(ks[18], (L, D, 2 * D_FF), D ** -0.5),
        "conv_w": nrm(ks[19], (L, CONV_W, D_FF), CONV_W ** -0.5),
        "conv_b": nrm(ks[20], (L, D_FF), 0.01),
        "w_down": nrm(ks[21], (L, D_FF, D), D_FF ** -0.5 * BETA),
        "w_ple_gate": nrm(ks[22], (L, D, D), D ** -0.5),
        "w_ple_proj": nrm(ks[23], (L, PLE_DIM, D), PLE_DIM ** -0.5 * BETA),
        "ln2_g": 1.0 + nrm(jax.random.fold_in(key, 100), (L, D), 0.02),
        "ln2_b": nrm(jax.random.fold_in(key, 101), (L, D), 0.02),
    }


def _fwd_reference(x, p, ln_in_g, ln_in_b, w_in, b_in, rpb, w_attn_out, pool_w, pool_scale,
              w_pool_out, w_mix_out, ln1_g, ln1_b, w_up, conv_w, conv_b, w_down,
              w_ple_gate, w_ple_proj, ln2_g, ln2_b):
    b, s, _ = x.shape
    splits = [D_ATTN, 2 * D_ATTN, 3 * D_ATTN, 3 * D_ATTN + D_POOL, 3 * D_ATTN + D_POOL + D_MODEL]
    h = layer_norm(x, ln_in_g, ln_in_b)
    for i in range(DEPTH):
        proj = h @ w_in[i] + b_in[i]
        q, k, v, u_pool, g_a, g_b = jnp.split(proj, splits, axis=-1)
        y_attn = neighbourhood_attention(q, k, v, rpb[i]) @ w_attn_out[i]
        pooled = multiscale_pool(u_pool).reshape(b, s, POOL_GROUPS, POOL_GROUP_DIM)
        pooled = jnp.einsum('bsgc,gcd->bsgd', pooled, pool_w[i]).reshape(b, s, D_POOL) * pool_scale[i]
        y_pool = pooled @ w_pool_out[i]
        merged = jax.nn.sigmoid(g_a) * y_attn + jax.nn.sigmoid(g_b) * y_pool
        h = layer_norm(ALPHA * h + merged @ w_mix_out[i], ln1_g[i], ln1_b[i])
        h_val, h_gate = jnp.split(h @ w_up[i], 2, axis=-1)
        act = jax.nn.gelu(dwconv_centred(h_gate, conv_w[i], conv_b[i]), approximate=False)
        ffn = (act * h_val) @ w_down[i]
        ple = jax.nn.sigmoid(h @ w_ple_gate[i]) * (p[i] @ w_ple_proj[i])
        h = layer_norm(ALPHA * h + ffn + ple, ln2_g[i], ln2_b[i])
    return h


import jax as _jax
import jax.numpy as _jnp

TWIN_FORMAT = 'train_step'
FWD_PARAMS = ['x', 'p', 'ln_in_g', 'ln_in_b', 'w_in', 'b_in', 'rpb', 'w_attn_out', 'pool_w', 'pool_scale', 'w_pool_out', 'w_mix_out', 'ln1_g', 'ln1_b', 'w_up', 'conv_w', 'conv_b', 'w_down', 'w_ple_gate', 'w_ple_proj', 'ln2_g', 'ln2_b']
TWIN_WEIGHTS = ['ln_in_g', 'ln_in_b', 'w_in', 'b_in', 'rpb', 'w_attn_out', 'pool_w', 'pool_scale', 'w_pool_out', 'w_mix_out', 'ln1_g', 'ln1_b', 'w_up', 'conv_w', 'conv_b', 'w_down', 'w_ple_gate', 'w_ple_proj', 'ln2_g', 'ln2_b']
TWIN_DIFF_INPUT = 'x'
TWIN_INPUTS = ['x', 'p', 'ln_in_g', 'ln_in_b', 'w_in', 'b_in', 'rpb', 'w_attn_out', 'pool_w', 'pool_scale', 'w_pool_out', 'w_mix_out', 'ln1_g', 'ln1_b', 'w_up', 'conv_w', 'conv_b', 'w_down', 'w_ple_gate', 'w_ple_proj', 'ln2_g', 'ln2_b', 'loss_target', 'm_ln_in_g', 'm_ln_in_b', 'm_w_in', 'm_b_in', 'm_rpb', 'm_w_attn_out', 'm_pool_w', 'm_pool_scale', 'm_w_pool_out', 'm_w_mix_out', 'm_ln1_g', 'm_ln1_b', 'm_w_up', 'm_conv_w', 'm_conv_b', 'm_w_down', 'm_w_ple_gate', 'm_w_ple_proj', 'm_ln2_g', 'm_ln2_b', 'v_ln_in_g', 'v_ln_in_b', 'v_w_in', 'v_b_in', 'v_rpb', 'v_w_attn_out', 'v_pool_w', 'v_pool_scale', 'v_w_pool_out', 'v_w_mix_out', 'v_ln1_g', 'v_ln1_b', 'v_w_up', 'v_conv_w', 'v_conv_b', 'v_w_down', 'v_w_ple_gate', 'v_w_ple_proj', 'v_ln2_g', 'v_ln2_b']
TWIN_OUTPUTS = ['loss', 'grad_x', 'grad_ln_in_g', 'grad_ln_in_b', 'grad_w_in', 'grad_b_in', 'grad_rpb', 'grad_w_attn_out', 'grad_pool_w', 'grad_pool_scale', 'grad_w_pool_out', 'grad_w_mix_out', 'grad_ln1_g', 'grad_ln1_b', 'grad_w_up', 'grad_conv_w', 'grad_conv_b', 'grad_w_down', 'grad_w_ple_gate', 'grad_w_ple_proj', 'grad_ln2_g', 'grad_ln2_b', 'delta_ln_in_g', 'delta_ln_in_b', 'delta_w_in', 'delta_b_in', 'delta_rpb', 'delta_w_attn_out', 'delta_pool_w', 'delta_pool_scale', 'delta_w_pool_out', 'delta_w_mix_out', 'delta_ln1_g', 'delta_ln1_b', 'delta_w_up', 'delta_conv_w', 'delta_conv_b', 'delta_w_down', 'delta_w_ple_gate', 'delta_w_ple_proj', 'delta_ln2_g', 'delta_ln2_b', 'new_m_ln_in_g', 'new_m_ln_in_b', 'new_m_w_in', 'new_m_b_in', 'new_m_rpb', 'new_m_w_attn_out', 'new_m_pool_w', 'new_m_pool_scale', 'new_m_w_pool_out', 'new_m_w_mix_out', 'new_m_ln1_g', 'new_m_ln1_b', 'new_m_w_up', 'new_m_conv_w', 'new_m_conv_b', 'new_m_w_down', 'new_m_w_ple_gate', 'new_m_w_ple_proj', 'new_m_ln2_g', 'new_m_ln2_b', 'new_v_ln_in_g', 'new_v_ln_in_b', 'new_v_w_in', 'new_v_b_in', 'new_v_rpb', 'new_v_w_attn_out', 'new_v_pool_w', 'new_v_pool_scale', 'new_v_w_pool_out', 'new_v_w_mix_out', 'new_v_ln1_g', 'new_v_ln1_b', 'new_v_w_up', 'new_v_conv_w', 'new_v_conv_b', 'new_v_w_down', 'new_v_w_ple_gate', 'new_v_w_ple_proj', 'new_v_ln2_g', 'new_v_ln2_b']
TWIN_LEAF_KINDS = {'loss': 'loss', 'grad_x': 'grad_x', 'grad_ln_in_g': 'grad_w', 'grad_ln_in_b': 'grad_w', 'grad_w_in': 'grad_w', 'grad_b_in': 'grad_w', 'grad_rpb': 'grad_w', 'grad_w_attn_out': 'grad_w', 'grad_pool_w': 'grad_w', 'grad_pool_scale': 'grad_w', 'grad_w_pool_out': 'grad_w', 'grad_w_mix_out': 'grad_w', 'grad_ln1_g': 'grad_w', 'grad_ln1_b': 'grad_w', 'grad_w_up': 'grad_w', 'grad_conv_w': 'grad_w', 'grad_conv_b': 'grad_w', 'grad_w_down': 'grad_w', 'grad_w_ple_gate': 'grad_w', 'grad_w_ple_proj': 'grad_w', 'grad_ln2_g': 'grad_w', 'grad_ln2_b': 'grad_w', 'delta_ln_in_g': 'delta_w', 'delta_ln_in_b': 'delta_w', 'delta_w_in': 'delta_w', 'delta_b_in': 'delta_w', 'delta_rpb': 'delta_w', 'delta_w_attn_out': 'delta_w', 'delta_pool_w': 'delta_w', 'delta_pool_scale': 'delta_w', 'delta_w_pool_out': 'delta_w', 'delta_w_mix_out': 'delta_w', 'delta_ln1_g': 'delta_w', 'delta_ln1_b': 'delta_w', 'delta_w_up': 'delta_w', 'delta_conv_w': 'delta_w', 'delta_conv_b': 'delta_w', 'delta_w_down': 'delta_w', 'delta_w_ple_gate': 'delta_w', 'delta_w_ple_proj': 'delta_w', 'delta_ln2_g': 'delta_w', 'delta_ln2_b': 'delta_w', 'new_m_ln_in_g': 'new_m', 'new_m_ln_in_b': 'new_m', 'new_m_w_in': 'new_m', 'new_m_b_in': 'new_m', 'new_m_rpb': 'new_m', 'new_m_w_attn_out': 'new_m', 'new_m_pool_w': 'new_m', 'new_m_pool_scale': 'new_m', 'new_m_w_pool_out': 'new_m', 'new_m_w_mix_out': 'new_m', 'new_m_ln1_g': 'new_m', 'new_m_ln1_b': 'new_m', 'new_m_w_up': 'new_m', 'new_m_conv_w': 'new_m', 'new_m_conv_b': 'new_m', 'new_m_w_down': 'new_m', 'new_m_w_ple_gate': 'new_m', 'new_m_w_ple_proj': 'new_m', 'new_m_ln2_g': 'new_m', 'new_m_ln2_b': 'new_m', 'new_v_ln_in_g': 'new_v', 'new_v_ln_in_b': 'new_v', 'new_v_w_in': 'new_v', 'new_v_b_in': 'new_v', 'new_v_rpb': 'new_v', 'new_v_w_attn_out': 'new_v', 'new_v_pool_w': 'new_v', 'new_v_pool_scale': 'new_v', 'new_v_w_pool_out': 'new_v', 'new_v_w_mix_out': 'new_v', 'new_v_ln1_g': 'new_v', 'new_v_ln1_b': 'new_v', 'new_v_w_up': 'new_v', 'new_v_conv_w': 'new_v', 'new_v_conv_b': 'new_v', 'new_v_w_down': 'new_v', 'new_v_w_ple_gate': 'new_v', 'new_v_w_ple_proj': 'new_v', 'new_v_ln2_g': 'new_v', 'new_v_ln2_b': 'new_v'}


def _forward(args):
    return _fwd_reference(*[args[k] for k in FWD_PARAMS])


def _output_shape():
    out = _jax.eval_shape(lambda: _forward(_fwd_setup_inputs(0)))
    return out.shape, out.dtype

N_MICROBATCH = 1
ADAM_LR = 0.001
ADAM_B1 = 0.9
ADAM_B2 = 0.999
ADAM_EPS = 1e-08
ADAM_WD = 0.01
ADAM_STEP = 10
PER_EXAMPLE_BATCH_AXIS = {'x': 0, 'p': 1, 'loss_target': 0}
SHARED_INPUTS = []
_WEIGHT_DTYPES = {'ln_in_g': _jnp.float32, 'ln_in_b': _jnp.float32, 'w_in': _jnp.float32, 'b_in': _jnp.float32, 'rpb': _jnp.float32, 'w_attn_out': _jnp.float32, 'pool_w': _jnp.float32, 'pool_scale': _jnp.float32, 'w_pool_out': _jnp.float32, 'w_mix_out': _jnp.float32, 'ln1_g': _jnp.float32, 'ln1_b': _jnp.float32, 'w_up': _jnp.float32, 'conv_w': _jnp.float32, 'conv_b': _jnp.float32, 'w_down': _jnp.float32, 'w_ple_gate': _jnp.float32, 'w_ple_proj': _jnp.float32, 'ln2_g': _jnp.float32, 'ln2_b': _jnp.float32}
MOMENT_SCALE = {'ln_in_g': 4.841482e-01, 'ln_in_b': 2.522584e-01, 'w_in': 3.491633e-03, 'b_in': 6.653994e-03, 'rpb': 2.972961e-04, 'w_attn_out': 1.493784e-03, 'pool_w': 8.979998e-03, 'pool_scale': 8.888024e-03, 'w_pool_out': 1.509461e-02, 'w_mix_out': 1.509521e-02, 'ln1_g': 5.246261e-01, 'ln1_b': 2.622073e-01, 'w_up': 1.240291e-02, 'conv_w': 1.250496e-02, 'conv_b': 1.212484e-02, 'w_down': 4.809506e-02, 'w_ple_gate': 6.530707e-03, 'w_ple_proj': 3.972346e-02, 'ln2_g': 8.047072e+00, 'ln2_b': 4.424506e-01}


def _to_microbatches(a, axis):
    t = _jnp.moveaxis(a, axis, 0)
    t = t.reshape((N_MICROBATCH, t.shape[0] // N_MICROBATCH) + t.shape[1:])
    return _jnp.moveaxis(t, 1, axis + 1)


def setup_inputs(seed: int = 0) -> dict:
    inp = _fwd_setup_inputs(seed)
    key = _jax.random.fold_in(_jax.random.key(seed), 7919)
    shape, _ = _output_shape()
    out = dict(inp)
    out["loss_target"] = _jax.random.normal(_jax.random.fold_in(key, 0), shape, _jnp.float32)
    for i, name in enumerate(TWIN_WEIGHTS):
        w = inp[name].astype(_jnp.float32)
        if MOMENT_SCALE is None:
            s = _jnp.sqrt(_jnp.mean(_jnp.square(w)) + 1e-30)
        else:
            s = MOMENT_SCALE[name]
        km, kv = _jax.random.split(_jax.random.fold_in(key, i + 1))
        out[name] = w
        out["m_" + name] = s * _jax.random.normal(km, w.shape, _jnp.float32)
        out["v_" + name] = (s * s) * _jax.random.uniform(kv, w.shape, _jnp.float32, 0.5, 1.5)
    if N_MICROBATCH > 1:
        for name, axis in PER_EXAMPLE_BATCH_AXIS.items():
            out[name] = _to_microbatches(out[name], axis)
    return {'x': out['x'], 'p': out['p'], 'ln_in_g': out['ln_in_g'], 'ln_in_b': out['ln_in_b'], 'w_in': out['w_in'], 'b_in': out['b_in'], 'rpb': out['rpb'], 'w_attn_out': out['w_attn_out'], 'pool_w': out['pool_w'], 'pool_scale': out['pool_scale'], 'w_pool_out': out['w_pool_out'], 'w_mix_out': out['w_mix_out'], 'ln1_g': out['ln1_g'], 'ln1_b': out['ln1_b'], 'w_up': out['w_up'], 'conv_w': out['conv_w'], 'conv_b': out['conv_b'], 'w_down': out['w_down'], 'w_ple_gate': out['w_ple_gate'], 'w_ple_proj': out['w_ple_proj'], 'ln2_g': out['ln2_g'], 'ln2_b': out['ln2_b'], 'loss_target': out['loss_target'], 'm_ln_in_g': out['m_ln_in_g'], 'm_ln_in_b': out['m_ln_in_b'], 'm_w_in': out['m_w_in'], 'm_b_in': out['m_b_in'], 'm_rpb': out['m_rpb'], 'm_w_attn_out': out['m_w_attn_out'], 'm_pool_w': out['m_pool_w'], 'm_pool_scale': out['m_pool_scale'], 'm_w_pool_out': out['m_w_pool_out'], 'm_w_mix_out': out['m_w_mix_out'], 'm_ln1_g': out['m_ln1_g'], 'm_ln1_b': out['m_ln1_b'], 'm_w_up': out['m_w_up'], 'm_conv_w': out['m_conv_w'], 'm_conv_b': out['m_conv_b'], 'm_w_down': out['m_w_down'], 'm_w_ple_gate': out['m_w_ple_gate'], 'm_w_ple_proj': out['m_w_ple_proj'], 'm_ln2_g': out['m_ln2_g'], 'm_ln2_b': out['m_ln2_b'], 'v_ln_in_g': out['v_ln_in_g'], 'v_ln_in_b': out['v_ln_in_b'], 'v_w_in': out['v_w_in'], 'v_b_in': out['v_b_in'], 'v_rpb': out['v_rpb'], 'v_w_attn_out': out['v_w_attn_out'], 'v_pool_w': out['v_pool_w'], 'v_pool_scale': out['v_pool_scale'], 'v_w_pool_out': out['v_w_pool_out'], 'v_w_mix_out': out['v_w_mix_out'], 'v_ln1_g': out['v_ln1_g'], 'v_ln1_b': out['v_ln1_b'], 'v_w_up': out['v_w_up'], 'v_conv_w': out['v_conv_w'], 'v_conv_b': out['v_conv_b'], 'v_w_down': out['v_w_down'], 'v_w_ple_gate': out['v_w_ple_gate'], 'v_w_ple_proj': out['v_w_ple_proj'], 'v_ln2_g': out['v_ln2_g'], 'v_ln2_b': out['v_ln2_b']}


def _loss(weights, diff, rest, loss_target):
    with _jax.named_scope("forward"):
        args = {**rest, TWIN_DIFF_INPUT: diff, **{k: w.astype(_WEIGHT_DTYPES[k]) for k, w in weights.items()}}
        y = _forward(args)
    with _jax.named_scope("loss_head"):
        err = _jnp.square(y.astype(_jnp.float32) - loss_target)
        return 0.5 * _jnp.sum(_jnp.mean(err, axis=-1)) if err.ndim else 0.5 * err


def _adamw(w, g, m, v):
    m = ADAM_B1 * m + (1.0 - ADAM_B1) * g
    v = ADAM_B2 * v + (1.0 - ADAM_B2) * _jnp.square(g)
    m_hat = m / (1.0 - ADAM_B1 ** ADAM_STEP)
    v_hat = v / (1.0 - ADAM_B2 ** ADAM_STEP)
    delta = -ADAM_LR * (m_hat / (_jnp.sqrt(v_hat) + ADAM_EPS) + ADAM_WD * w)
    return delta, m, v


def reference(x, p, ln_in_g, ln_in_b, w_in, b_in, rpb, w_attn_out, pool_w, pool_scale, w_pool_out, w_mix_out, ln1_g, ln1_b, w_up, conv_w, conv_b, w_down, w_ple_gate, w_ple_proj, ln2_g, ln2_b, loss_target, m_ln_in_g, m_ln_in_b, m_w_in, m_b_in, m_rpb, m_w_attn_out, m_pool_w, m_pool_scale, m_w_pool_out, m_w_mix_out, m_ln1_g, m_ln1_b, m_w_up, m_conv_w, m_conv_b, m_w_down, m_w_ple_gate, m_w_ple_proj, m_ln2_g, m_ln2_b, v_ln_in_g, v_ln_in_b, v_w_in, v_b_in, v_rpb, v_w_attn_out, v_pool_w, v_pool_scale, v_w_pool_out, v_w_mix_out, v_ln1_g, v_ln1_b, v_w_up, v_conv_w, v_conv_b, v_w_down, v_w_ple_gate, v_w_ple_proj, v_ln2_g, v_ln2_b):
    given = dict(x=x, p=p, ln_in_g=ln_in_g, ln_in_b=ln_in_b, w_in=w_in, b_in=b_in, rpb=rpb, w_attn_out=w_attn_out, pool_w=pool_w, pool_scale=pool_scale, w_pool_out=w_pool_out, w_mix_out=w_mix_out, ln1_g=ln1_g, ln1_b=ln1_b, w_up=w_up, conv_w=conv_w, conv_b=conv_b, w_down=w_down, w_ple_gate=w_ple_gate, w_ple_proj=w_ple_proj, ln2_g=ln2_g, ln2_b=ln2_b, loss_target=loss_target, m_ln_in_g=m_ln_in_g, m_ln_in_b=m_ln_in_b, m_w_in=m_w_in, m_b_in=m_b_in, m_rpb=m_rpb, m_w_attn_out=m_w_attn_out, m_pool_w=m_pool_w, m_pool_scale=m_pool_scale, m_w_pool_out=m_w_pool_out, m_w_mix_out=m_w_mix_out, m_ln1_g=m_ln1_g, m_ln1_b=m_ln1_b, m_w_up=m_w_up, m_conv_w=m_conv_w, m_conv_b=m_conv_b, m_w_down=m_w_down, m_w_ple_gate=m_w_ple_gate, m_w_ple_proj=m_w_ple_proj, m_ln2_g=m_ln2_g, m_ln2_b=m_ln2_b, v_ln_in_g=v_ln_in_g, v_ln_in_b=v_ln_in_b, v_w_in=v_w_in, v_b_in=v_b_in, v_rpb=v_rpb, v_w_attn_out=v_w_attn_out, v_pool_w=v_pool_w, v_pool_scale=v_pool_scale, v_w_pool_out=v_w_pool_out, v_w_mix_out=v_w_mix_out, v_ln1_g=v_ln1_g, v_ln1_b=v_ln1_b, v_w_up=v_w_up, v_conv_w=v_conv_w, v_conv_b=v_conv_b, v_w_down=v_w_down, v_w_ple_gate=v_w_ple_gate, v_w_ple_proj=v_w_ple_proj, v_ln2_g=v_ln2_g, v_ln2_b=v_ln2_b)
    weights = {n: given[n] for n in TWIN_WEIGHTS}
    shared = {n: given[n] for n in SHARED_INPUTS}
    per_example = {n: given[n] for n in ['x', 'p']}
    grad_fn = _jax.value_and_grad(_loss, argnums=(0, 1))

    def one_microbatch(ex, loss_target):
        ex = dict(ex)
        diff = ex.pop(TWIN_DIFF_INPUT)
        return grad_fn(weights, diff, {**shared, **ex}, loss_target)

    if N_MICROBATCH == 1:
        loss, (grad_w, grad_x) = one_microbatch(per_example, given["loss_target"])
    else:
        def body(carry, xs):
            loss_sum, grad_sum = carry
            l_k, (gw_k, gx_k) = one_microbatch(xs[0], xs[1])
            with _jax.named_scope("update"):
                return (loss_sum + l_k, _jax.tree.map(_jnp.add, grad_sum, gw_k)), gx_k

        init = (_jnp.zeros((), _jnp.float32), _jax.tree.map(_jnp.zeros_like, weights))
        (loss, grad_w), grad_x = _jax.lax.scan(body, init, (per_example, given["loss_target"]))
    with _jax.named_scope("update"):
        delta_w, new_m, new_v = {}, {}, {}
        for n in TWIN_WEIGHTS:
            delta_w[n], new_m[n], new_v[n] = _adamw(weights[n], grad_w[n], given["m_" + n], given["v_" + n])
    return (loss, grad_x, *[grad_w[n] for n in TWIN_WEIGHTS], *[delta_w[n] for n in TWIN_WEIGHTS],
            *[new_m[n] for n in TWIN_WEIGHTS], *[new_v[n] for n in TWIN_WEIGHTS])
```

```python
import numpy as np
import jax
import jax.numpy as jnp
from jax import lax
from jax.experimental import pallas as pl
from jax.experimental.pallas import tpu as pltpu

F32 = jnp.float32
BF16 = jnp.bfloat16
I32 = jnp.int32

D = 1024
DEPTH = 4
GRID_W = 64
N_HEADS = 8
HEAD_DIM = 64
D_ATTN = 512
KH = 8
KW = 16
POOL_WINDOWS = (2, 4, 8, 16)
D_POOL = 512
PGD = 128
D_FF = 2816
PLE_DIM = 256
N_PROJ = 4096
ALPHA = (2 * DEPTH) ** 0.25
LN_EPS = 1e-5
NEG_INF = -1e30
ATT_SCALE = HEAD_DIM ** -0.5
ADAM_LR = 0.001
ADAM_B1 = 0.9
ADAM_B2 = 0.999
ADAM_EPS = 1e-08
ADAM_WD = 0.01
ADAM_STEP = 10

N_DEV = 8
AXES = ("x", "y", "c")
FF_BLK = D_FF // 4
FF_SHARD = D_FF // N_DEV
QROWS = 8
KROWS = 16
QB = QROWS * GRID_W
KB = KROWS * GRID_W
V7X_VMEM_LIMIT = 56 * 2 ** 20
MESH = pl.DeviceIdType.MESH
ANY = pl.BlockSpec(memory_space=pl.ANY)


def _cparams(n_grid):
    return pltpu.CompilerParams(dimension_semantics=("arbitrary",) * n_grid, vmem_limit_bytes=V7X_VMEM_LIMIT)


def _nn(a, b):
    return lax.dot_general(a, b, (((1,), (0,)), ((), ())), preferred_element_type=F32)


def _nt(a, b):
    return lax.dot_general(a, b, (((1,), (1,)), ((), ())), preferred_element_type=F32)


def _tn(a, b):
    return lax.dot_general(a, b, (((0,), (0,)), ((), ())), preferred_element_type=F32)


def _sigmoid(x):
    return 1.0 / (1.0 + jnp.exp(-x))


def _ln_fwd(z, g, b):
    mu = jnp.mean(z, axis=-1, keepdims=True)
    xc = z - mu
    var = jnp.mean(xc * xc, axis=-1, keepdims=True)
    return xc * lax.rsqrt(var + LN_EPS) * g + b


def _ln_bwd(dh, z, g):
    mu = jnp.mean(z, axis=-1, keepdims=True)
    xc = z - mu
    var = jnp.mean(xc * xc, axis=-1, keepdims=True)
    rstd = lax.rsqrt(var + LN_EPS)
    xhat = xc * rstd
    dxh = dh * g
    m1 = jnp.mean(dxh, axis=-1, keepdims=True)
    m2 = jnp.mean(dxh * xhat, axis=-1, keepdims=True)
    return rstd * (dxh - m1 - xhat * m2), dh * xhat


def _colsum(x):
    return jnp.sum(x, axis=0, keepdims=True)


def _lane_cat(ref):
    return jnp.concatenate([ref[j] for j in range(ref.shape[0])], axis=1)


def _row_cat(ref):
    n, r, c = ref.shape
    return ref[...].reshape(n * r, c)


def _shards(n, r, c, li, j_of=None):
    if j_of is None:
        return pl.BlockSpec((n, None, r, c), lambda *_: (0, li, 0, 0))
    return pl.BlockSpec((n, None, r, c), lambda *g: (j_of(*g), li, 0, 0))


def _shard(r, c, li, j_of):
    return pl.BlockSpec((None, None, r, c), lambda *g: (j_of(*g), li, 0, 0))


def ln_fwd(x, g, b, name):
    s = x.shape[0]
    tm = 512

    def body(x_ref, g_ref, b_ref, h_ref, hb_ref):
        h = _ln_fwd(x_ref[...], g_ref[...], b_ref[...])
        h_ref[...] = h
        hb_ref[...] = h.astype(BF16)

    row = pl.BlockSpec((tm, D), lambda i: (i, 0))
    vec = pl.BlockSpec((1, D), lambda i: (0, 0))
    return pl.pallas_call(
        body, name=name, grid=(s // tm,), in_specs=[row, vec, vec], out_specs=[row, row],
        out_shape=[jax.ShapeDtypeStruct((s, D), F32), jax.ShapeDtypeStruct((s, D), BF16)],
        compiler_params=_cparams(1))(x, g, b)


def proj_fwd(hb, win, bias, li, j0, nj, out_dtype, name):
    s = hb.shape[0]
    bn = N_PROJ // N_DEV
    tm = 1024

    def body(a_ref, w_ref, b_ref, o_ref):
        o_ref[...] = (_nn(a_ref[...], w_ref[...]) + b_ref[...]).astype(out_dtype)

    return pl.pallas_call(
        body, name=name, grid=(s // tm, nj),
        in_specs=[pl.BlockSpec((tm, D), lambda i, j: (i, 0)),
                  _shard(D, bn, li, lambda i, j: j0 + j),
                  pl.BlockSpec((1, bn), lambda i, j: (0, j0 + j))],
        out_specs=pl.BlockSpec((tm, bn), lambda i, j: (i, j)),
        out_shape=jax.ShapeDtypeStruct((s, nj * bn), out_dtype),
        compiler_params=_cparams(2))(hb, win, bias)


def _attn_block_scalars(b, n_rows):
    ks = jnp.clip(QROWS * b - KH // 2, 0, n_rows - KROWS)
    delta = ks - QROWS * b
    k0 = pl.multiple_of(ks * GRID_W, 256)
    return ks, delta, k0


def _attn_band_logits(sb, e, b, qr, ks, delta, n_rows, klane):
    shift = ((qr - delta - (KH - 1)) * GRID_W) % KB
    bias = pltpu.roll(e, shift, 1)
    rs = jnp.clip(QROWS * b + qr - KH // 2, 0, n_rows - KH)
    lo = (rs - ks) * GRID_W
    ok = (klane >= lo) & (klane < lo + KH * GRID_W)
    return jnp.where(ok, sb + bias, NEG_INF)


def attn_fwd(proj, e_tab, name):
    s = proj.shape[0]
    n_rows = s // GRID_W
    nb = n_rows // QROWS

    def body(q_ref, k_ref, v_ref, e_ref, o_ref, s_ref, p_ref):
        b = pl.program_id(1)
        ks, delta, k0 = _attn_block_scalars(b, n_rows)
        kwin = k_ref[pl.ds(k0, KB), :]
        vwin = v_ref[pl.ds(k0, KB), :]
        q = q_ref[...]
        lane = lax.broadcasted_iota(I32, (1, 128), 1)
        klane = lax.broadcasted_iota(I32, (1, KB), 1)
        acc = jnp.zeros((QB, 128), F32)
        for hh in range(2):
            lm = (lane // HEAD_DIM) == hh
            qh = jnp.where(lm, q, jnp.zeros_like(q))
            vh = jnp.where(lm, vwin, jnp.zeros_like(vwin))
            s_ref[...] = _nt(qh, kwin) * ATT_SCALE
            e = e_ref[hh]

            def band(qr, carry):
                r0 = pl.multiple_of(qr * GRID_W, GRID_W)
                sb = _attn_band_logits(s_ref[pl.ds(r0, GRID_W), :], e, b, qr, ks, delta, n_rows, klane)
                m = jnp.max(sb, axis=1, keepdims=True)
                p = jnp.exp(sb - m)
                l = jnp.sum(p, axis=1, keepdims=True)
                p_ref[pl.ds(r0, GRID_W), :] = (p * (1.0 / l)).astype(BF16)
                return carry

            lax.fori_loop(0, QROWS, band, 0)
            acc = acc + _nn(p_ref[...], vh)
        o_ref[...] = acc.astype(BF16)

    return pl.pallas_call(
        body, name=name, grid=(4, nb),
        in_specs=[pl.BlockSpec((QB, 128), lambda j, b: (b, j)),
                  pl.BlockSpec((s, 128), lambda j, b: (0, 4 + j)),
                  pl.BlockSpec((s, 128), lambda j, b: (0, 8 + j)),
                  pl.BlockSpec((2, GRID_W, KB), lambda j, b: (j, 0, 0))],
        out_specs=pl.BlockSpec((QB, 128), lambda j, b: (b, j)),
        out_shape=jax.ShapeDtypeStruct((s, D_ATTN), BF16),
        scratch_shapes=[pltpu.VMEM((QB, KB), F32), pltpu.VMEM((QB, KB), BF16)],
        compiler_params=_cparams(2))(proj, proj, proj, e_tab)


_POOL_PAD = 8


def _pool_counts(s, w):
    t = lax.broadcasted_iota(I32, (s, 1), 0)
    return (jnp.minimum(t + w // 2, s) - jnp.maximum(t - w // 2, 0)).astype(F32)


def _window_sum(x, w, back_first):
    s = x.shape[0]
    z = jnp.zeros((_POOL_PAD, x.shape[1]), F32)
    xe = jnp.concatenate([z, x, z], axis=0)
    n = s + 2 * _POOL_PAD
    acc = xe + pltpu.roll(xe, 1 if back_first else n - 1, 0)
    k = 1
    while 2 * k < w:
        acc = pltpu.roll(acc, k, 0) + pltpu.roll(acc, n - k, 0)
        k *= 2
    return acc[_POOL_PAD:_POOL_PAD + s, :]


def pool_fwd(u, pool_w, pool_scale, name):
    s = u.shape[0]

    def body(u_ref, w_ref, sc_ref, pm_ref, pw_ref):
        for g, w in enumerate(POOL_WINDOWS):
            cols = slice(g * PGD, (g + 1) * PGD)
            ug = u_ref[:, cols]
            pm = (_window_sum(ug, w, True) / _pool_counts(s, w) - ug).astype(BF16)
            pm_ref[:, cols] = pm
            pw_ref[:, cols] = (_nn(pm, w_ref[g]) * sc_ref[:, cols]).astype(BF16)

    full = lambda shape: pl.BlockSpec(shape, lambda i: (0,) * len(shape))
    return pl.pallas_call(
        body, name=name, grid=(1,),
        in_specs=[full((s, D_POOL)), full((4, PGD, PGD)), full((1, D_POOL))],
        out_specs=[full((s, D_POOL)), full((s, D_POOL))],
        out_shape=[jax.ShapeDtypeStruct((s, D_POOL), BF16)] * 2,
        compiler_params=_cparams(1))(u, pool_w, pool_scale)


def merge_fwd(a, pw, wao, wpo, proj, li, name):
    s = a.shape[0]
    tm, tn = 512, 512
    nt = D // tn
    per = tn // 128

    def body(a_ref, pw_ref, wa_ref, wp_ref, ga_ref, gb_ref, mg_ref, ya_ref, yp_ref):
        ya = _nn(a_ref[...], _lane_cat(wa_ref))
        yp = _nn(pw_ref[...], _lane_cat(wp_ref))
        mg = _sigmoid(ga_ref[...].astype(F32)) * ya + _sigmoid(gb_ref[...].astype(F32)) * yp
        mg_ref[...] = mg.astype(BF16)
        ya_ref[...] = ya.astype(BF16)
        yp_ref[...] = yp.astype(BF16)

    act = pl.BlockSpec((tm, D_ATTN), lambda i, j: (i, 0))
    wsp = _shards(per, D_ATTN, 128, li, lambda i, j: j)
    out = pl.BlockSpec((tm, tn), lambda i, j: (i, j))
    ga0 = (3 * D_ATTN + D_POOL) // tn
    return pl.pallas_call(
        body, name=name, grid=(s // tm, nt),
        in_specs=[act, act, wsp, wsp,
                  pl.BlockSpec((tm, tn), lambda i, j: (i, ga0 + j)),
                  pl.BlockSpec((tm, tn), lambda i, j: (i, ga0 + nt + j))],
        out_specs=[out, out, out],
        out_shape=[jax.ShapeDtypeStruct((s, D), BF16)] * 3,
        compiler_params=_cparams(2))(a, pw, wao, wpo, proj, proj)


def mix_ln_fwd(mg, wmix, h0, g, b, li, name):
    s = mg.shape[0]
    tm = 256

    def body(mg_ref, w_ref, h0_ref, g_ref, b_ref, z_ref, h_ref, hb_ref):
        z = ALPHA * h0_ref[...] + _nn(mg_ref[...], _row_cat(w_ref))
        h = _ln_fwd(z, g_ref[...], b_ref[...])
        z_ref[...] = z
        h_ref[...] = h
        hb_ref[...] = h.astype(BF16)

    row = pl.BlockSpec((tm, D), lambda i: (i, 0))
    vec = pl.BlockSpec((1, D), lambda i: (0, 0))
    return pl.pallas_call(
        body, name=name, grid=(s // tm,),
        in_specs=[row, _shards(N_DEV, D // N_DEV, D, li), row, vec, vec],
        out_specs=[row, row, row],
        out_shape=[jax.ShapeDtypeStruct((s, D), F32), jax.ShapeDtypeStruct((s, D), F32),
                   jax.ShapeDtypeStruct((s, D), BF16)],
        compiler_params=_cparams(1))(mg, wmix, h0, g, b)


def up_fwd(hb, wup, li, name):
    s = hb.shape[0]
    tm = 1024

    def body(a_ref, w_ref, o_ref):
        o_ref[...] = _nn(a_ref[...], w_ref[...]).astype(BF16)

    return pl.pallas_call(
        body, name=name, grid=(s // tm, N_DEV),
        in_specs=[pl.BlockSpec((tm, D), lambda i, j: (i, 0)), _shard(D, FF_BLK, li, lambda i, j: j)],
        out_specs=pl.BlockSpec((None, tm, FF_BLK), lambda i, j: (j, i, 0)),
        out_shape=jax.ShapeDtypeStruct((N_DEV, s, FF_BLK), BF16),
        compiler_params=_cparams(2))(hb, wup)


_SQRT_HALF = 0.7071067811865476
_INV_SQRT_2PI = 0.3989422804014327


def _shift_rows(x, prev_row, next_row):
    n = x.shape[0]
    r = lax.broadcasted_iota(I32, (n, 1), 0)
    back = jnp.where(r == 0, prev_row, pltpu.roll(x, 1, 0))
    fwd = jnp.where(r == n - 1, next_row, pltpu.roll(x, n - 1, 0))
    return back, fwd


HALO = 16


def _halo_maps(tm, s):
    th = tm // HALO
    return (lambda i: jnp.maximum(i * th - 1, 0)), (lambda i: jnp.minimum((i + 1) * th, s // HALO - 1))


def _slab_specs(tm, s, blk_of):
    before, after = _halo_maps(tm, s)
    main = pl.BlockSpec((None, tm, FF_BLK), lambda c, i: (blk_of(c), i, 0))
    prev = pl.BlockSpec((None, HALO, FF_BLK), lambda c, i: (blk_of(c), before(i), 0))
    nxt = pl.BlockSpec((None, HALO, FF_BLK), lambda c, i: (blk_of(c), after(i), 0))
    return main, prev, nxt


def ffn_act_fwd(up, conv_w, conv_b, name):
    s = up.shape[1]
    tm = 512
    nt = s // tm
    hv_main, _, _ = _slab_specs(tm, s, lambda c: c)
    hg_main, hg_prev, hg_next = _slab_specs(tm, s, lambda c: 4 + c)

    def body(hv_ref, hg_ref, hp_ref, hn_ref, cw_ref, cb_ref, t_ref):
        i = pl.program_id(1)
        hg = hg_ref[...].astype(F32)
        prow = jnp.where(i == 0, 0.0, hp_ref[...].astype(F32)[HALO - 1:HALO, :])
        nrow = jnp.where(i == nt - 1, 0.0, hn_ref[...].astype(F32)[0:1, :])
        back, fwd = _shift_rows(hg, prow, nrow)
        c = back * cw_ref[0:1, :] + hg * cw_ref[1:2, :] + fwd * cw_ref[2:3, :] + cb_ref[...]
        act = 0.5 * c * (1.0 + lax.erf(c * _SQRT_HALF))
        t_ref[...] = (act * hv_ref[...].astype(F32)).astype(BF16)

    return pl.pallas_call(
        body, name=name, grid=(4, nt),
        in_specs=[hv_main, hg_main, hg_prev, hg_next,
                  pl.BlockSpec((None, 3, FF_BLK), lambda c, i: (c, 0, 0)),
                  pl.BlockSpec((None, 1, FF_BLK), lambda c, i: (c, 0, 0))],
        out_specs=pl.BlockSpec((None, tm, FF_BLK), lambda c, i: (c, i, 0)),
        out_shape=jax.ShapeDtypeStruct((4, s, FF_BLK), BF16),
        compiler_params=_cparams(2))(up, up, up, up, conv_w, conv_b)


def down_ple_ln_fwd(t, wdown, hb, wpg, pb, wpp, h1, g, b, li, name):
    s = hb.shape[0]
    tm = 256

    def body(t_ref, wd_ref, hb_ref, wpg_ref, p_ref, wpp_ref, h1_ref, g_ref, b_ref,
             z_ref, h_ref, hbo_ref, pg_ref, pp_ref):
        wd = _row_cat(wd_ref)
        ffn = _nn(t_ref[0], wd[0:FF_BLK, :])
        for c in range(1, 4):
            ffn = ffn + _nn(t_ref[c], wd[c * FF_BLK:(c + 1) * FF_BLK, :])
        pg = _nn(hb_ref[...], _row_cat(wpg_ref))
        pp = _nn(p_ref[...], _lane_cat(wpp_ref))
        z = ALPHA * h1_ref[...] + ffn + _sigmoid(pg) * pp
        h = _ln_fwd(z, g_ref[...], b_ref[...])
        z_ref[...] = z
        h_ref[...] = h
        hbo_ref[...] = h.astype(BF16)
        pg_ref[...] = pg.astype(BF16)
        pp_ref[...] = pp.astype(BF16)

    row = pl.BlockSpec((tm, D), lambda i: (i, 0))
    vec = pl.BlockSpec((1, D), lambda i: (0, 0))
    return pl.pallas_call(
        body, name=name, grid=(s // tm,),
        in_specs=[pl.BlockSpec((4, tm, FF_BLK), lambda i: (0, i, 0)),
                  _shards(N_DEV, FF_SHARD, D, li),
                  row, _shards(N_DEV, D // N_DEV, D, li),
                  pl.BlockSpec((tm, PLE_DIM), lambda i: (i, 0)),
                  _shards(N_DEV, PLE_DIM, 128, li),
                  row, vec, vec],
        out_specs=[row] * 5,
        out_shape=[jax.ShapeDtypeStruct((s, D), F32), jax.ShapeDtypeStruct((s, D), F32),
                   jax.ShapeDtypeStruct((s, D), BF16), jax.ShapeDtypeStruct((s, D), BF16),
                   jax.ShapeDtypeStruct((s, D), BF16)],
        compiler_params=_cparams(1))(t, wdown, hb, wpg, pb, wpp, h1, g, b)


def loss_bwd(h, target, name):
    s = h.shape[0]
    tm = 512

    def body(h_ref, t_ref, dh_ref, l_ref):
        @pl.when(pl.program_id(0) == 0)
        def _():
            l_ref[...] = jnp.zeros_like(l_ref)
        e = h_ref[...] - t_ref[...]
        dh_ref[...] = e * (1.0 / D)
        l_ref[...] += 0.5 * jnp.sum(jnp.mean(e * e, axis=-1, keepdims=True), axis=0, keepdims=True)

    row = pl.BlockSpec((tm, D), lambda i: (i, 0))
    return pl.pallas_call(
        body, name=name, grid=(s // tm,), in_specs=[row, row],
        out_specs=[row, pl.BlockSpec((1, 1), lambda i: (0, 0))],
        out_shape=[jax.ShapeDtypeStruct((s, D), F32), jax.ShapeDtypeStruct((1, 1), F32)],
        compiler_params=_cparams(1))(h, target)


def ln_bwd(dh, z, g, name):
    s = dh.shape[0]
    tm = 512

    def body(dh_ref, z_ref, g_ref, dz_ref, dg_ref, db_ref):
        @pl.when(pl.program_id(0) == 0)
        def _():
            dg_ref[...] = jnp.zeros_like(dg_ref)
            db_ref[...] = jnp.zeros_like(db_ref)
        dh = dh_ref[...]
        dz, dgx = _ln_bwd(dh, z_ref[...], g_ref[...])
        dz_ref[...] = dz
        dg_ref[...] += _colsum(dgx)
        db_ref[...] += _colsum(dh)

    row = pl.BlockSpec((tm, D), lambda i: (i, 0))
    vec = pl.BlockSpec((1, D), lambda i: (0, 0))
    return pl.pallas_call(
        body, name=name, grid=(s // tm,), in_specs=[row, row, vec], out_specs=[row, vec, vec],
        out_shape=[jax.ShapeDtypeStruct((s, D), F32), jax.ShapeDtypeStruct((1, D), F32),
                   jax.ShapeDtypeStruct((1, D), F32)],
        compiler_params=_cparams(1))(dh, z, g)


def ln2_ple_bwd(dh, z, g, pg, pp, name):
    s = dh.shape[0]
    tm = 512

    def body(dh_ref, z_ref, g_ref, pg_ref, pp_ref, dz_ref, dzb_ref, dpg_ref, dpp_ref, dg_ref, db_ref):
        @pl.when(pl.program_id(0) == 0)
        def _():
            dg_ref[...] = jnp.zeros_like(dg_ref)
            db_ref[...] = jnp.zeros_like(db_ref)
        dh = dh_ref[...]
        dz, dgx = _ln_bwd(dh, z_ref[...], g_ref[...])
        sg = _sigmoid(pg_ref[...].astype(F32))
        dz_ref[...] = dz
        dzb_ref[...] = dz.astype(BF16)
        dpg_ref[...] = (dz * pp_ref[...].astype(F32) * sg * (1.0 - sg)).astype(BF16)
        dpp_ref[...] = (dz * sg).astype(BF16)
        dg_ref[...] += _colsum(dgx)
        db_ref[...] += _colsum(dh)

    row = pl.BlockSpec((tm, D), lambda i: (i, 0))
    vec = pl.BlockSpec((1, D), lambda i: (0, 0))
    return pl.pallas_call(
        body, name=name, grid=(s // tm,), in_specs=[row, row, vec, row, row],
        out_specs=[row, row, row, row, vec, vec],
        out_shape=[jax.ShapeDtypeStruct((s, D), F32)] + [jax.ShapeDtypeStruct((s, D), BF16)] * 3
        + [jax.ShapeDtypeStruct((1, D), F32)] * 2,
        compiler_params=_cparams(1))(dh, z, g, pg, pp)


def wgrad_rows(a, dy, name):
    s, k = a.shape
    n = dy.shape[1]
    kb = k // N_DEV

    def body(a_ref, dy_ref, o_ref):
        o_ref[...] = _tn(a_ref[...], dy_ref[...]).astype(BF16)

    return pl.pallas_call(
        body, name=name, grid=(N_DEV,),
        in_specs=[pl.BlockSpec((s, kb), lambda j: (0, j)), pl.BlockSpec((s, n), lambda j: (0, 0))],
        out_specs=pl.BlockSpec((None, kb, n), lambda j: (j, 0, 0)),
        out_shape=jax.ShapeDtypeStruct((N_DEV, kb, n), BF16),
        compiler_params=_cparams(1))(a, dy)


def wgrad_cols(a, dy, name, with_colsum=False):
    s, k = a.shape
    n = dy.shape[1]
    nb = n // N_DEV

    def body(a_ref, dy_ref, o_ref, *cs_ref):
        dy = dy_ref[...]
        o_ref[...] = _tn(a_ref[...], dy).astype(BF16)
        if with_colsum:
            cs_ref[0][...] = _colsum(dy.astype(F32))

    out_specs = [pl.BlockSpec((None, k, nb), lambda j: (j, 0, 0))]
    out_shape = [jax.ShapeDtypeStruct((N_DEV, k, nb), BF16)]
    if with_colsum:
        out_specs.append(pl.BlockSpec((1, nb), lambda j: (0, j)))
        out_shape.append(jax.ShapeDtypeStruct((1, n), F32))
    res = pl.pallas_call(
        body, name=name, grid=(N_DEV,),
        in_specs=[pl.BlockSpec((s, k), lambda j: (0, 0)), pl.BlockSpec((s, nb), lambda j: (0, j))],
        out_specs=out_specs, out_shape=out_shape,
        compiler_params=_cparams(1))(a, dy)
    return res if with_colsum else res[0]


def wgrad_down(t, dy, name):
    _, s, k = t.shape
    n = dy.shape[1]

    def body(a_ref, dy_ref, o_ref):
        o_ref[...] = _tn(a_ref[...], dy_ref[...]).astype(BF16)

    return pl.pallas_call(
        body, name=name, grid=(4,),
        in_specs=[pl.BlockSpec((None, s, k), lambda j: (j, 0, 0)), pl.BlockSpec((s, n), lambda j: (0, 0))],
        out_specs=pl.BlockSpec((None, k, n), lambda j: (j, 0, 0)),
        out_shape=jax.ShapeDtypeStruct((4, k, n), BF16),
        compiler_params=_cparams(1))(t, dy)


def wgrad_up(a, dhv, dhg, name):
    s, k = a.shape

    def body(a_ref, dv_ref, dg_ref, o_ref):
        j = pl.program_id(0)

        @pl.when(j < 4)
        def _():
            o_ref[...] = _tn(a_ref[...], dv_ref[...]).astype(BF16)

        @pl.when(j >= 4)
        def _():
            o_ref[...] = _tn(a_ref[...], dg_ref[...]).astype(BF16)

    return pl.pallas_call(
        body, name=name, grid=(N_DEV,),
        in_specs=[pl.BlockSpec((s, k), lambda j: (0, 0)),
                  pl.BlockSpec((None, s, FF_BLK), lambda j: (jnp.minimum(j, 3), 0, 0)),
                  pl.BlockSpec((None, s, FF_BLK), lambda j: (jnp.maximum(j - 4, 0), 0, 0))],
        out_specs=pl.BlockSpec((None, k, FF_BLK), lambda j: (j, 0, 0)),
        out_shape=jax.ShapeDtypeStruct((N_DEV, k, FF_BLK), BF16),
        compiler_params=_cparams(1))(a, dhv, dhg)


def ffn_act_bwd(dzb, wdown, up, conv_w, conv_b, li, name):
    s = up.shape[1]
    tm = 512
    nt = s // tm
    before, after = _halo_maps(tm, s)
    hv_main, hv_prev, hv_next = _slab_specs(tm, s, lambda c: c)
    hg_main, hg_prev, hg_next = _slab_specs(tm, s, lambda c: 4 + c)

    def dc_of(dz, wd, hv, hg, back, fwd, cw_ref, cb_ref):
        dt = _nt(dz, wd)
        c = back * cw_ref[0:1, :] + hg * cw_ref[1:2, :] + fwd * cw_ref[2:3, :] + cb_ref[...]
        cdf = 0.5 * (1.0 + lax.erf(c * _SQRT_HALF))
        pdf = jnp.exp(-0.5 * c * c) * _INV_SQRT_2PI
        return dt, c * cdf, dt * hv * (cdf + c * pdf)

    def body(dz_ref, dzp_ref, dzn_ref, wd_ref, hv_ref, hvp_ref, hvn_ref, hg_ref, hgp_ref, hgn_ref, cw_ref, cb_ref,
             dhv_ref, dhg_ref, dcw_ref, dcb_ref):
        i = pl.program_id(1)

        @pl.when(i == 0)
        def _():
            dcw_ref[...] = jnp.zeros_like(dcw_ref)
            dcb_ref[...] = jnp.zeros_like(dcb_ref)

        wd = _row_cat(wd_ref)
        hg = hg_ref[...].astype(F32)
        hgp = hgp_ref[...].astype(F32)
        hgn = hgn_ref[...].astype(F32)
        first, last = i == 0, i == nt - 1
        e = HALO - 1
        back, fwd = _shift_rows(hg, jnp.where(first, 0.0, hgp[e:e + 1, :]), jnp.where(last, 0.0, hgn[0:1, :]))
        dt, act, dc = dc_of(dz_ref[...], wd, hv_ref[...].astype(F32), hg, back, fwd, cw_ref, cb_ref)
        dhv_ref[...] = (dt * act).astype(BF16)
        bp, fp = _shift_rows(hgp, hgp[0:1, :], hg[0:1, :])
        _, _, dcp = dc_of(dzp_ref[...], wd, hvp_ref[...].astype(F32), hgp, bp, fp, cw_ref, cb_ref)
        bn, fn = _shift_rows(hgn, hg[tm - 1:tm, :], hgn[e:e + 1, :])
        _, _, dcn = dc_of(dzn_ref[...], wd, hvn_ref[...].astype(F32), hgn, bn, fn, cw_ref, cb_ref)
        dc_back, dc_fwd = _shift_rows(dc, jnp.where(first, 0.0, dcp[e:e + 1, :]), jnp.where(last, 0.0, dcn[0:1, :]))
        dhg_ref[...] = (dc_fwd * cw_ref[0:1, :] + dc * cw_ref[1:2, :] + dc_back * cw_ref[2:3, :]).astype(BF16)
        dcw_ref[0:1, :] += _colsum(dc * back)
        dcw_ref[1:2, :] += _colsum(dc * hg)
        dcw_ref[2:3, :] += _colsum(dc * fwd)
        dcb_ref[...] += _colsum(dc)

    out_slab = pl.BlockSpec((None, tm, FF_BLK), lambda c, i: (c, i, 0))
    cw_spec = pl.BlockSpec((None, 3, FF_BLK), lambda c, i: (c, 0, 0))
    cb_spec = pl.BlockSpec((None, 1, FF_BLK), lambda c, i: (c, 0, 0))
    return pl.pallas_call(
        body, name=name, grid=(4, nt),
        in_specs=[pl.BlockSpec((tm, D), lambda c, i: (i, 0)),
                  pl.BlockSpec((HALO, D), lambda c, i: (before(i), 0)),
                  pl.BlockSpec((HALO, D), lambda c, i: (after(i), 0)),
                  _shards(2, FF_SHARD, D, li, lambda c, i: c),
                  hv_main, hv_prev, hv_next, hg_main, hg_prev, hg_next, cw_spec, cb_spec],
        out_specs=[out_slab, out_slab, cw_spec, cb_spec],
        out_shape=[jax.ShapeDtypeStruct((4, s, FF_BLK), BF16), jax.ShapeDtypeStruct((4, s, FF_BLK), BF16),
                   jax.ShapeDtypeStruct((4, 3, FF_BLK), F32), jax.ShapeDtypeStruct((4, 1, FF_BLK), F32)],
        compiler_params=_cparams(2))(dzb, dzb, dzb, wdown, up, up, up, up, up, up, conv_w, conv_b)


def dh1_ln1_bwd(dz2, dpg, wpg, dhv, dhg, wup, z1, g1, li, name):
    s = dz2.shape[0]
    tm = 256

    def body(dz2_ref, dpg_ref, wpg_ref, dhv_ref, dhg_ref, wup_ref, z1_ref, g_ref, dz_ref, dzb_ref, dg_ref, db_ref):
        @pl.when(pl.program_id(0) == 0)
        def _():
            dg_ref[...] = jnp.zeros_like(dg_ref)
            db_ref[...] = jnp.zeros_like(db_ref)
        dh = ALPHA * dz2_ref[...] + _nt(dpg_ref[...], _row_cat(wpg_ref))
        for c in range(4):
            dh = dh + _nt(dhv_ref[c], wup_ref[c]) + _nt(dhg_ref[c], wup_ref[4 + c])
        dz, dgx = _ln_bwd(dh, z1_ref[...], g_ref[...])
        dz_ref[...] = dz
        dzb_ref[...] = dz.astype(BF16)
        dg_ref[...] += _colsum(dgx)
        db_ref[...] += _colsum(dh)

    row = pl.BlockSpec((tm, D), lambda i: (i, 0))
    vec = pl.BlockSpec((1, D), lambda i: (0, 0))
    slab = pl.BlockSpec((4, tm, FF_BLK), lambda i: (0, i, 0))
    return pl.pallas_call(
        body, name=name, grid=(s // tm,),
        in_specs=[row, row, _shards(N_DEV, D // N_DEV, D, li), slab, slab, _shards(N_DEV, D, FF_BLK, li), row, vec],
        out_specs=[row, row, vec, vec],
        out_shape=[jax.ShapeDtypeStruct((s, D), F32), jax.ShapeDtypeStruct((s, D), BF16),
                   jax.ShapeDtypeStruct((1, D), F32), jax.ShapeDtypeStruct((1, D), F32)],
        compiler_params=_cparams(1))(dz2, dpg, wpg, dhv, dhg, wup, z1, g1)


def merge_bwd(dz1b, wmix, proj, ya, yp, li, name):
    s = dz1b.shape[0]
    tm, tn = 512, 512
    nt = D // tn
    per = tn // (D // N_DEV)
    ga0 = (3 * D_ATTN + D_POOL) // tn

    def body(dz_ref, w_ref, ga_ref, gb_ref, ya_ref, yp_ref, dya_ref, dyp_ref, dga_ref, dgb_ref):
        dm = _nt(dz_ref[...], _row_cat(w_ref))
        sa = _sigmoid(ga_ref[...].astype(F32))
        sb = _sigmoid(gb_ref[...].astype(F32))
        dya_ref[...] = (dm * sa).astype(BF16)
        dyp_ref[...] = (dm * sb).astype(BF16)
        dga_ref[...] = (dm * ya_ref[...].astype(F32) * sa * (1.0 - sa)).astype(BF16)
        dgb_ref[...] = (dm * yp_ref[...].astype(F32) * sb * (1.0 - sb)).astype(BF16)

    tile = pl.BlockSpec((tm, tn), lambda i, j: (i, j))
    return pl.pallas_call(
        body, name=name, grid=(s // tm, nt),
        in_specs=[pl.BlockSpec((tm, D), lambda i, j: (i, 0)),
                  _shards(per, D // N_DEV, D, li, lambda i, j: j),
                  pl.BlockSpec((tm, tn), lambda i, j: (i, ga0 + j)),
                  pl.BlockSpec((tm, tn), lambda i, j: (i, ga0 + nt + j)),
                  tile, tile],
        out_specs=[tile] * 4,
        out_shape=[jax.ShapeDtypeStruct((s, D), BF16)] * 4,
        compiler_params=_cparams(2))(dz1b, wmix, proj, proj, ya, yp)


def attn_out_bwd(dya, wao, li, name):
    s = dya.shape[0]
    tm = 512

    def body(d_ref, w_ref, o_ref):
        o_ref[...] = _nt(d_ref[...], _lane_cat(w_ref)).astype(BF16)

    return pl.pallas_call(
        body, name=name, grid=(s // tm,),
        in_specs=[pl.BlockSpec((tm, D), lambda i: (i, 0)), _shards(N_DEV, D_ATTN, 128, li)],
        out_specs=pl.BlockSpec((tm, D_ATTN), lambda i: (i, 0)),
        out_shape=jax.ShapeDtypeStruct((s, D_ATTN), BF16),
        compiler_params=_cparams(1))(dya, wao)


def pool_bwd(dyp, wpo, pm, pool_w, pool_scale, li, name):
    s = dyp.shape[0]

    def body(dyp_ref, wpo_ref, pm_ref, w_ref, sc_ref, du_ref, dw_ref, dsc_ref):
        wpo = _lane_cat(wpo_ref)
        dyp = dyp_ref[...]
        for g, w in enumerate(POOL_WINDOWS):
            cols = slice(g * PGD, (g + 1) * PGD)
            dpw = _nt(dyp, wpo[g * PGD:(g + 1) * PGD, :])
            pmg = pm_ref[:, cols]
            dsc_ref[:, cols] = _colsum(dpw * _nn(pmg, w_ref[g]))
            dpmw = (dpw * sc_ref[:, cols]).astype(BF16)
            dw_ref[g] = _tn(pmg, dpmw)
            dpm = _nt(dpmw, w_ref[g])
            du_ref[:, cols] = (_window_sum(dpm / _pool_counts(s, w), w, False) - dpm).astype(BF16)

    full = lambda shape: pl.BlockSpec(shape, lambda i: (0,) * len(shape))
    return pl.pallas_call(
        body, name=name, grid=(1,),
        in_specs=[full((s, D)), _shards(N_DEV, D_POOL, 128, li), full((s, D_POOL)), full((4, PGD, PGD)),
                  full((1, D_POOL))],
        out_specs=[full((s, D_POOL)), full((4, PGD, PGD)), full((1, D_POOL))],
        out_shape=[jax.ShapeDtypeStruct((s, D_POOL), BF16), jax.ShapeDtypeStruct((4, PGD, PGD), F32),
                   jax.ShapeDtypeStruct((1, D_POOL), F32)],
        compiler_params=_cparams(1))(dyp, wpo, pm, pool_w, pool_scale)


_SKEW_BASE = KB - (GRID_W - KW) - GRID_W


def attn_bwd(proj, da, e_rev, name):
    s = proj.shape[0]
    n_rows = s // GRID_W
    nb = n_rows // QROWS

    def body(q_ref, k_ref, v_ref, do_ref, e_ref, dq_ref, dk_ref, dv_ref, g_ref,
             s_ref, dp_ref, ds_ref, p_ref, dk_acc, dv_acc):
        b = pl.program_id(1)
        ks, delta, k0 = _attn_block_scalars(b, n_rows)

        @pl.when(b == 0)
        def _():
            dk_acc[...] = jnp.zeros_like(dk_acc)
            dv_acc[...] = jnp.zeros_like(dv_acc)
            g_ref[...] = jnp.zeros_like(g_ref)

        kwin = k_ref[pl.ds(k0, KB), :]
        vwin = v_ref[pl.ds(k0, KB), :]
        ri = lax.broadcasted_iota(I32, (QB, QB), 0)
        ci = lax.broadcasted_iota(I32, (QB, QB), 1)
        rev = jnp.where(ri + ci == QB - 1, 1.0, 0.0).astype(BF16)
        q = _nn(rev, q_ref[...]).astype(BF16)
        do = _nn(rev, do_ref[...]).astype(BF16)
        lane = lax.broadcasted_iota(I32, (1, 128), 1)
        klane = lax.broadcasted_iota(I32, (1, KB), 1)
        dq = jnp.zeros((QB, 128), F32)
        for hh in range(2):
            lm = (lane // HEAD_DIM) == hh
            qh = jnp.where(lm, q, jnp.zeros_like(q))
            doh = jnp.where(lm, do, jnp.zeros_like(do))
            kh = jnp.where(lm, kwin, jnp.zeros_like(kwin))
            s_ref[...] = _nt(qh, kwin) * ATT_SCALE
            dp_ref[...] = _nt(doh, vwin)
            e = e_ref[hh]

            def band(ib, carry):
                r0 = pl.multiple_of(ib * GRID_W, GRID_W)
                rows = pl.ds(r0, GRID_W)
                sb = _attn_band_logits(s_ref[rows, :], e, b, QROWS - 1 - ib, ks, delta, n_rows, klane)
                m = jnp.max(sb, axis=1, keepdims=True)
                p = jnp.exp(sb - m)
                p = p * (1.0 / jnp.sum(p, axis=1, keepdims=True))
                dp = dp_ref[rows, :]
                ds_ref[rows, :] = p * (dp - jnp.sum(p * dp, axis=1, keepdims=True))
                p_ref[rows, :] = p.astype(BF16)
                return carry

            lax.fori_loop(0, QROWS, band, 0)
            ds = ds_ref[...]
            dsb = ds.astype(BF16)
            dq = dq + _nn(dsb, kh) * ATT_SCALE
            dk_acc[pl.ds(k0, KB), :] += _tn(dsb, qh) * ATT_SCALE
            dv_acc[pl.ds(k0, KB), :] += _tn(p_ref[...], doh)
            t = pltpu.roll(ds.reshape(QROWS, GRID_W, KB), _SKEW_BASE, 2, stride=1, stride_axis=1).sum(axis=1)
            g = jnp.zeros((1, KB), F32)
            for ib in range(QROWS):
                shift = ((1 + ib + delta) * GRID_W) % KB
                g = g + pltpu.roll(t[ib:ib + 1, :], shift, 1)
            g_ref[hh] += g
        dq_ref[...] = _nn(rev, dq.astype(BF16)).astype(BF16)

        @pl.when(b == nb - 1)
        def _():
            dk_ref[...] = dk_acc[...].astype(BF16)
            dv_ref[...] = dv_acc[...].astype(BF16)

    col = pl.BlockSpec((s, 128), lambda j, b: (0, j))
    return pl.pallas_call(
        body, name=name, grid=(4, nb),
        in_specs=[pl.BlockSpec((QB, 128), lambda j, b: (b, j)),
                  pl.BlockSpec((s, 128), lambda j, b: (0, 4 + j)),
                  pl.BlockSpec((s, 128), lambda j, b: (0, 8 + j)),
                  pl.BlockSpec((QB, 128), lambda j, b: (b, j)),
                  pl.BlockSpec((2, GRID_W, KB), lambda j, b: (j, 0, 0))],
        out_specs=[pl.BlockSpec((QB, 128), lambda j, b: (b, j)), col, col,
                   pl.BlockSpec((2, 1, KB), lambda j, b: (j, 0, 0))],
        out_shape=[jax.ShapeDtypeStruct((s, D_ATTN), BF16)] * 3 + [jax.ShapeDtypeStruct((N_HEADS, 1, KB), F32)],
        scratch_shapes=[pltpu.VMEM((QB, KB), F32), pltpu.VMEM((QB, KB), F32), pltpu.VMEM((QB, KB), F32),
                        pltpu.VMEM((QB, KB), BF16), pltpu.VMEM((s, 128), F32), pltpu.VMEM((s, 128), F32)],
        compiler_params=_cparams(2))(proj, proj, proj, da, e_rev)


def dh0_bwd(dz1, dproj, win, li, name):
    s = dz1.shape[0]
    tm = 256
    bn = N_PROJ // N_DEV

    def body(dz_ref, dp_ref, w_ref, o_ref):
        acc = ALPHA * dz_ref[...]
        for j in range(N_DEV):
            acc = acc + _nt(dp_ref[:, j * bn:(j + 1) * bn], w_ref[j])
        o_ref[...] = acc

    row = pl.BlockSpec((tm, D), lambda i: (i, 0))
    return pl.pallas_call(
        body, name=name, grid=(s // tm,),
        in_specs=[row, pl.BlockSpec((tm, N_PROJ), lambda i: (i, 0)), _shards(N_DEV, D, bn, li)],
        out_specs=row, out_shape=jax.ShapeDtypeStruct((s, D), F32),
        compiler_params=_cparams(1))(dz1, dproj, win)


def _coords():
    return lax.axis_index("x"), lax.axis_index("y"), lax.axis_index("c")


def _dev_index(px, py, pc):
    return 4 * px + 2 * py + pc


def all_gather(arrs, name):
    n = len(arrs)

    def body(*refs):
        ins, outs = refs[:n], refs[n:2 * n]
        send_sems, recv_sems, local_sems = refs[2 * n:]
        x, y, c = _coords()
        me, sibling = (x, y, c), (x, y, 1 - c)
        chips = [(1 - x, y), (x, 1 - y), (1 - x, 1 - y)]

        def copy(a, k, block, to, src=None):
            dst = outs[a].at[_dev_index(*block)]
            return pltpu.make_async_remote_copy(
                src_ref=dst if src is None else src, dst_ref=dst,
                send_sem=send_sems.at[a, k], recv_sem=recv_sems.at[a, k], device_id=to, device_id_type=MESH)

        mine = [pltpu.make_async_copy(ins[a], outs[a].at[_dev_index(*me)], local_sems.at[a]) for a in range(n)]
        for cp in mine:
            cp.start()
        first = []
        for a in range(n):
            first.append(copy(a, 0, me, sibling, src=ins[a]))
            first += [copy(a, 1 + j, me, (*chip, c), src=ins[a]) for j, chip in enumerate(chips)]
        for cp in first:
            cp.start()
        passed = []
        for j, chip in enumerate(chips):
            for a in range(n):
                copy(a, 1 + j, (*chip, c), me).wait_recv()
                cp = copy(a, 4 + j, (*chip, c), sibling)
                cp.start()
                passed.append(cp)
        for a in range(n):
            copy(a, 0, sibling, me).wait_recv()
            for j, chip in enumerate(chips):
                copy(a, 4 + j, (*chip, 1 - c), me).wait_recv()
        for cp in first + passed:
            cp.wait_send()
        for cp in mine:
            cp.wait()

    return pl.pallas_call(
        body, name=name,
        out_shape=[jax.ShapeDtypeStruct((N_DEV,) + a.shape, a.dtype) for a in arrs],
        in_specs=[ANY] * n, out_specs=[ANY] * n,
        scratch_shapes=[pltpu.SemaphoreType.DMA((n, 7)), pltpu.SemaphoreType.DMA((n, 7)),
                        pltpu.SemaphoreType.DMA((n,))],
    )(*arrs)


def exchange_sibling(grads, name):
    n = len(grads)

    def body(*refs):
        ins, outs = refs[:n], refs[n:2 * n]
        send_sems, recv_sems = refs[2 * n:]
        x, y, c = _coords()
        sibling = (x, y, 1 - c)
        copies = []
        for a in range(n):
            for k in range(4):
                blk = _dev_index(x ^ (k & 1), y ^ (k >> 1), 1 - c)
                copies.append(pltpu.make_async_remote_copy(
                    src_ref=ins[a].at[blk], dst_ref=outs[a].at[k],
                    send_sem=send_sems.at[a, k], recv_sem=recv_sems.at[a, k], device_id=sibling, device_id_type=MESH))
        for cp in copies:
            cp.start()
        for cp in copies:
            cp.wait()

    return pl.pallas_call(
        body, name=name,
        out_shape=[jax.ShapeDtypeStruct((4,) + g.shape[1:], g.dtype) for g in grads],
        in_specs=[ANY] * n, out_specs=[ANY] * n,
        scratch_shapes=[pltpu.SemaphoreType.DMA((n, 4)), pltpu.SemaphoreType.DMA((n, 4))],
    )(*grads)


def exchange_chips(psums, name):
    n = len(psums)

    def body(*refs):
        ins, outs = refs[:n], refs[n:2 * n]
        send_sems, recv_sems = refs[2 * n:]
        x, y, c = _coords()
        copies = []
        for a in range(n):
            for k in range(3):
                peer = (x ^ ((k + 1) & 1), y ^ ((k + 1) >> 1), c)
                copies.append(pltpu.make_async_remote_copy(
                    src_ref=ins[a].at[k], dst_ref=outs[a].at[k],
                    send_sem=send_sems.at[a, k], recv_sem=recv_sems.at[a, k], device_id=peer, device_id_type=MESH))
        for cp in copies:
            cp.start()
        for cp in copies:
            cp.wait()

    return pl.pallas_call(
        body, name=name,
        out_shape=[jax.ShapeDtypeStruct(p.shape, p.dtype) for p in psums],
        in_specs=[ANY] * n, out_specs=[ANY] * n,
        scratch_shapes=[pltpu.SemaphoreType.DMA((n, 3)), pltpu.SemaphoreType.DMA((n, 3))],
    )(*psums)


def _row_tile(r):
    return r if r <= 512 else 512


def pair_add(blk_idx, g, recv, name):
    _, r, c = g.shape
    tr = _row_tile(r)

    def body(idx_ref, g_ref, r_ref, own_ref, oth_ref):
        k = pl.program_id(1)
        sm = g_ref[...].astype(F32) + r_ref[...].astype(F32)

        @pl.when(k == 0)
        def _():
            own_ref[...] = sm

        @pl.when(k > 0)
        def _():
            oth_ref[...] = sm.astype(BF16)

    grid_spec = pltpu.PrefetchScalarGridSpec(
        num_scalar_prefetch=1, grid=(r // tr, 4),
        in_specs=[pl.BlockSpec((None, tr, c), lambda t, k, idx: (idx[k], t, 0)),
                  pl.BlockSpec((None, tr, c), lambda t, k, idx: (k, t, 0))],
        out_specs=[pl.BlockSpec((tr, c), lambda t, k, idx: (t, 0)),
                   pl.BlockSpec((None, tr, c), lambda t, k, idx: (jnp.maximum(k - 1, 0), t, 0))])
    return pl.pallas_call(
        body, name=name, grid_spec=grid_spec,
        out_shape=[jax.ShapeDtypeStruct((r, c), F32), jax.ShapeDtypeStruct((3, r, c), BF16)],
        compiler_params=_cparams(2))(blk_idx, g, recv)


def _adamw(w, g, m, v):
    m = ADAM_B1 * m + (1.0 - ADAM_B1) * g
    v = ADAM_B2 * v + (1.0 - ADAM_B2) * (g * g)
    m_hat = m / (1.0 - ADAM_B1 ** ADAM_STEP)
    v_hat = v / (1.0 - ADAM_B2 ** ADAM_STEP)
    delta = -ADAM_LR * (m_hat / (jnp.sqrt(v_hat) + ADAM_EPS) + ADAM_WD * w)
    return delta, m, v


def adamw_shard(own, recv, w, m, v, li, prev, name):
    r, c = own.shape
    tr = _row_tile(r)

    def body(own_ref, recv_ref, w_ref, m_ref, v_ref, p0, p1, p2, p3, g_ref, d_ref, nm_ref, nv_ref):
        g = own_ref[...] + recv_ref[0].astype(F32) + recv_ref[1].astype(F32) + recv_ref[2].astype(F32)
        delta, nm, nv = _adamw(w_ref[...], g, m_ref[...], v_ref[...])
        g_ref[...] = g
        d_ref[...] = delta
        nm_ref[...] = nm
        nv_ref[...] = nv

    lay = pl.BlockSpec((None, tr, c), lambda t: (li, t, 0))
    stack = jax.ShapeDtypeStruct((DEPTH, r, c), F32)
    return pl.pallas_call(
        body, name=name, grid=(r // tr,),
        in_specs=[pl.BlockSpec((tr, c), lambda t: (t, 0)), pl.BlockSpec((3, tr, c), lambda t: (0, t, 0)),
                  lay, lay, lay, ANY, ANY, ANY, ANY],
        out_specs=[lay] * 4, out_shape=[stack] * 4,
        input_output_aliases={5: 0, 6: 1, 7: 2, 8: 3},
        compiler_params=_cparams(1))(own, recv, w, m, v, *prev)


def adamw_replicated(gathered, w, m, v, name):
    _, r, c = gathered.shape
    tr = 88

    def body(gs_ref, w_ref, m_ref, v_ref, g_ref, d_ref, nm_ref, nv_ref):
        g = gs_ref[0]
        for d in range(1, N_DEV):
            g = g + gs_ref[d]
        delta, nm, nv = _adamw(w_ref[...], g, m_ref[...], v_ref[...])
        g_ref[...] = g
        d_ref[...] = delta
        nm_ref[...] = nm
        nv_ref[...] = nv

    row = pl.BlockSpec((tr, c), lambda t: (t, 0))
    return pl.pallas_call(
        body, name=name, grid=(r // tr,),
        in_specs=[pl.BlockSpec((N_DEV, tr, c), lambda t: (0, t, 0)), row, row, row],
        out_specs=[row] * 4, out_shape=[jax.ShapeDtypeStruct((r, c), F32)] * 4,
        compiler_params=_cparams(1))(gathered, w, m, v)


def adamw_plain(g, w, m, v, name):
    def body(g_ref, w_ref, m_ref, v_ref, d_ref, nm_ref, nv_ref):
        delta, nm, nv = _adamw(w_ref[...], g_ref[...], m_ref[...], v_ref[...])
        d_ref[...] = delta
        nm_ref[...] = nm
        nv_ref[...] = nv

    return pl.pallas_call(body, name=name, out_shape=[jax.ShapeDtypeStruct(w.shape, F32)] * 3)(g, w, m, v)


_PACK = (("ln_in_g", (D,)), ("ln_in_b", (D,)), ("b_in", (DEPTH, N_PROJ)), ("rpb", (DEPTH, N_HEADS, 2 * KH - 1, 2 * KW - 1)),
         ("pool_w", (DEPTH, 4, PGD, PGD)), ("pool_scale", (DEPTH, D_POOL)), ("ln1_g", (DEPTH, D)), ("ln1_b", (DEPTH, D)),
         ("conv_b", (DEPTH, D_FF)), ("ln2_g", (DEPTH, D)), ("ln2_b", (DEPTH, D)), ("conv_w", (DEPTH, 3, D_FF)))
_PACK_LANES = 1024


def _pack_rows(shape):
    return -(-int(np.prod(shape)) // _PACK_LANES)


_PACK_ROWS = -(-sum(_pack_rows(s) for _, s in _PACK) // 88) * 88


def _pack(parts):
    rows = []
    for name, shape in _PACK:
        flat = parts[name].reshape(-1).astype(F32)
        rows.append(jnp.pad(flat, (0, _pack_rows(shape) * _PACK_LANES - flat.shape[0])))
    used = sum(_pack_rows(s) for _, s in _PACK)
    rows.append(jnp.zeros(((_PACK_ROWS - used) * _PACK_LANES,), F32))
    return jnp.concatenate(rows).reshape(_PACK_ROWS, _PACK_LANES)


def _unpack(packed):
    out, r0 = {}, 0
    for name, shape in _PACK:
        n = int(np.prod(shape))
        nr = _pack_rows(shape)
        out[name] = packed[r0:r0 + nr].reshape(-1)[:n].reshape(shape)
        r0 += nr
    return out


def _bias_tables(rpb_l):
    qc = np.arange(GRID_W)[:, None]
    kc = np.arange(GRID_W)[None, :]
    start = np.clip(qc - KW // 2, 0, GRID_W - KW)
    valid = (kc >= start) & (kc < start + KW)
    col = np.clip(kc - qc, -(KW - 1), KW - 1) + KW - 1
    onehot = (col.reshape(-1)[None, :] == np.arange(2 * KW - 1)[:, None]).astype(np.float32)
    rows = jnp.pad(rpb_l, ((0, 0), (0, 1), (0, 0)))
    tab = jnp.einsum("hij,jm->him", rows, jnp.asarray(onehot), precision=lax.Precision.HIGHEST)
    tab = tab.reshape(N_HEADS, KROWS, GRID_W, GRID_W).transpose(0, 2, 1, 3)
    ok = valid[None, :, None, :] & (np.arange(KROWS) < 2 * KH - 1)[None, None, :, None]
    tab = jnp.where(jnp.asarray(ok), tab, NEG_INF).reshape(N_HEADS, GRID_W, KB)
    return tab, tab[:, ::-1, :]


_SHARDED = ("w_in", "w_attn_out", "w_pool_out", "w_mix_out", "w_up", "w_down", "w_ple_gate", "w_ple_proj")
_NAMES = ("ln_in_g", "ln_in_b", "w_in", "b_in", "rpb", "w_attn_out", "pool_w", "pool_scale", "w_pool_out", "w_mix_out",
          "ln1_g", "ln1_b", "w_up", "conv_w", "conv_b", "w_down", "w_ple_gate", "w_ple_proj", "ln2_g", "ln2_b")


def kernel(x, p, ln_in_g, ln_in_b, w_in, b_in, rpb, w_attn_out, pool_w, pool_scale, w_pool_out, w_mix_out, ln1_g, ln1_b, w_up, conv_w, conv_b, w_down, w_ple_gate, w_ple_proj, ln2_g, ln2_b, loss_target, m_ln_in_g, m_ln_in_b, m_w_in, m_b_in, m_rpb, m_w_attn_out, m_pool_w, m_pool_scale, m_w_pool_out, m_w_mix_out, m_ln1_g, m_ln1_b, m_w_up, m_conv_w, m_conv_b, m_w_down, m_w_ple_gate, m_w_ple_proj, m_ln2_g, m_ln2_b, v_ln_in_g, v_ln_in_b, v_w_in, v_b_in, v_rpb, v_w_attn_out, v_pool_w, v_pool_scale, v_w_pool_out, v_w_mix_out, v_ln1_g, v_ln1_b, v_w_up, v_conv_w, v_conv_b, v_w_down, v_w_ple_gate, v_w_ple_proj, v_ln2_g, v_ln2_b):
    a = dict(locals())
    W = {n: a[n] for n in _NAMES}
    M = {n: a["m_" + n] for n in _NAMES}
    V = {n: a["v_" + n] for n in _NAMES}
    xi, yi, ci = _coords()
    me = _dev_index(xi, yi, ci)
    x2, tgt = x[0], loss_target[0]
    pb = p[:, 0].astype(BF16)

    gathered = all_gather([W[n].astype(BF16) for n in _SHARDED] + [conv_w], "ag_weights")
    G = dict(zip(_SHARDED, gathered[:-1]))
    cw_full = gathered[-1].transpose(1, 2, 0, 3).reshape(DEPTH, 3, 4, FF_BLK).transpose(0, 2, 1, 3)
    cb_full = conv_b.reshape(DEPTH, 4, 1, FF_BLK)
    pool_w_b = pool_w.astype(BF16)
    vec = lambda t: t.reshape(1, -1)

    loss_part, dx, gws, parts = _local_step(x2, tgt, pb, W, G, cw_full)
    loss = lax.psum(loss_part[0, 0], AXES)

    shapes = {n: W[n].shape[1:] for n in _SHARDED}
    stacks = {n: [lax.empty((DEPTH,) + shapes[n], F32) for _ in range(4)] for n in _SHARDED}
    rel_idx = jnp.stack([_dev_index(xi ^ (k & 1), yi ^ (k >> 1), ci) for k in range(4)]).astype(I32)
    for li in reversed(range(DEPTH)):
        glist = [gws[li][n] for n in _SHARDED]
        recv1 = exchange_sibling(glist, f"rs_d2d{li}")
        sums = [pair_add(rel_idx, g, r1, f"pair_add_{n}{li}") for n, g, r1 in zip(_SHARDED, glist, recv1)]
        recv2 = exchange_chips([s_[1] for s_ in sums], f"rs_ici{li}")
        for n, s_, r2 in zip(_SHARDED, sums, recv2):
            stacks[n] = adamw_shard(s_[0], r2, W[n], M[n], V[n], li, stacks[n], f"adamw_{n}{li}")

    (gath,) = all_gather([_pack(parts)], "ag_small_grads")
    zero_cw = jnp.zeros((DEPTH, 3, D_FF), F32)
    packs = [_pack({**{n: src[n] for n, _ in _PACK if n != "conv_w"}, "conv_w": zero_cw}) for src in (W, M, V)]
    outs = [_unpack(o) for o in adamw_replicated(gath, *packs, "adamw_replicated")]
    g_cw = lax.dynamic_slice_in_dim(outs[0]["conv_w"], me * FF_SHARD, FF_SHARD, axis=2)
    flat = lambda t: t.reshape(DEPTH * 3, FF_SHARD)
    cw_out = [o.reshape(DEPTH, 3, FF_SHARD) for o in
              adamw_plain(flat(g_cw), flat(conv_w), flat(m_conv_w), flat(v_conv_w), "adamw_conv_w")]
    res = []
    for k in range(4):
        d = {n: stacks[n][k] for n in _SHARDED}
        d.update({n: outs[k][n] for n, _ in _PACK if n != "conv_w"})
        d["conv_w"] = g_cw if k == 0 else cw_out[k - 1]
        res.append(d)
    return (loss, dx[None], *[res[k][n] for k in range(4) for n in _NAMES])


def _local_step(x2, tgt, pb, W, G, cw_full):
    depth = W["rpb"].shape[0]
    vec = lambda t: t.reshape(1, -1)
    ln1_g, ln1_b, ln2_g, ln2_b = W["ln1_g"], W["ln1_b"], W["ln2_g"], W["ln2_b"]
    b_in, rpb, pool_scale = W["b_in"], W["rpb"], W["pool_scale"]
    cb_full = W["conv_b"].reshape(depth, 4, 1, FF_BLK)
    pool_w_b = W["pool_w"].astype(BF16)

    h, hb = ln_fwd(x2, vec(W["ln_in_g"]), vec(W["ln_in_b"]), "ln_in")
    saved = []
    for li in range(depth):
        e_tab, e_rev = _bias_tables(rpb[li])
        bias = vec(b_in[li])
        proj = proj_fwd(hb, G["w_in"], bias, li, 0, N_DEV, BF16, f"proj{li}")
        u = proj_fwd(hb, G["w_in"], bias, li, 3, 1, F32, f"proj_u{li}")
        att = attn_fwd(proj, e_tab, f"attn{li}")
        pm, pw = pool_fwd(u, pool_w_b[li], vec(pool_scale[li]), f"pool{li}")
        mg, ya, yp = merge_fwd(att, pw, G["w_attn_out"], G["w_pool_out"], proj, li, f"merge{li}")
        z1, h1, h1b = mix_ln_fwd(mg, G["w_mix_out"], h, vec(ln1_g[li]), vec(ln1_b[li]), li, f"mix_ln{li}")
        up = up_fwd(h1b, G["w_up"], li, f"up{li}")
        t = ffn_act_fwd(up, cw_full[li], cb_full[li], f"ffn_act{li}")
        z2, h2, h2b, pg, pp = down_ple_ln_fwd(t, G["w_down"], h1b, G["w_ple_gate"], pb[li], G["w_ple_proj"], h1,
                                              vec(ln2_g[li]), vec(ln2_b[li]), li, f"down_ln{li}")
        saved.append(dict(hb=hb, proj=proj, att=att, pm=pm, pw=pw, mg=mg, ya=ya, yp=yp, z1=z1, h1b=h1b, up=up, t=t,
                          z2=z2, pg=pg, pp=pp, e_rev=e_rev))
        h, hb = h2, h2b

    dh, loss_part = loss_bwd(h, tgt, "loss")
    small = {n: [None] * depth for n in ("b_in", "rpb", "pool_w", "pool_scale", "ln1_g", "ln1_b", "conv_b", "ln2_g",
                                         "ln2_b", "conv_w")}
    gws = [None] * depth
    for li in reversed(range(depth)):
        sv = saved[li]
        dz2, dz2b, dpg, dpp, dg2, db2 = ln2_ple_bwd(dh, sv["z2"], vec(ln2_g[li]), sv["pg"], sv["pp"], f"ln2_bwd{li}")
        gw = {}
        gw["w_ple_gate"] = wgrad_rows(sv["h1b"], dpg, f"dw_pg{li}")
        gw["w_ple_proj"] = wgrad_cols(pb[li], dpp, f"dw_pp{li}")
        gw["w_down"] = wgrad_down(sv["t"], dz2b, f"dw_down{li}").reshape(N_DEV, FF_SHARD, D)
        dhv, dhg, dcw, dcb = ffn_act_bwd(dz2b, G["w_down"], sv["up"], cw_full[li], cb_full[li], li, f"ffn_bwd{li}")
        gw["w_up"] = wgrad_up(sv["h1b"], dhv, dhg, f"dw_up{li}")
        dz1, dz1b, dg1, db1 = dh1_ln1_bwd(dz2, dpg, G["w_ple_gate"], dhv, dhg, G["w_up"], sv["z1"], vec(ln1_g[li]), li,
                                          f"ln1_bwd{li}")
        gw["w_mix_out"] = wgrad_rows(sv["mg"], dz1b, f"dw_mix{li}")
        dya, dyp, dga, dgb = merge_bwd(dz1b, G["w_mix_out"], sv["proj"], sv["ya"], sv["yp"], li, f"merge_bwd{li}")
        gw["w_attn_out"] = wgrad_cols(sv["att"], dya, f"dw_ao{li}")
        gw["w_pool_out"] = wgrad_cols(sv["pw"], dyp, f"dw_po{li}")
        da = attn_out_bwd(dya, G["w_attn_out"], li, f"da{li}")
        du, dpool_w, dpool_sc = pool_bwd(dyp, G["w_pool_out"], sv["pm"], pool_w_b[li], vec(pool_scale[li]), li,
                                         f"pool_bwd{li}")
        dq, dk, dv, drpb = attn_bwd(sv["proj"], da, sv["e_rev"], f"attn_bwd{li}")
        dproj = jnp.concatenate([dq, dk, dv, du, dga, dgb], axis=1)
        gw["w_in"], db_in = wgrad_cols(sv["hb"], dproj, f"dw_in{li}", with_colsum=True)
        dh = dh0_bwd(dz1, dproj, G["w_in"], li, f"dh0{li}")
        small["b_in"][li] = db_in.reshape(N_PROJ)
        small["rpb"][li] = drpb.reshape(N_HEADS, KROWS, GRID_W)[:, :2 * KH - 1, :2 * KW - 1]
        small["pool_w"][li] = dpool_w
        small["pool_scale"][li] = dpool_sc.reshape(D_POOL)
        small["ln1_g"][li], small["ln1_b"][li] = dg1.reshape(D), db1.reshape(D)
        small["ln2_g"][li], small["ln2_b"][li] = dg2.reshape(D), db2.reshape(D)
        small["conv_b"][li] = dcb.reshape(D_FF)
        small["conv_w"][li] = dcw.transpose(1, 0, 2).reshape(3, D_FF)
        gws[li] = gw
    dx, dg_in, db_in0 = ln_bwd(dh, x2, vec(W["ln_in_g"]), "ln_in_bwd")
    parts = {n: jnp.stack(v_) for n, v_ in small.items()}
    parts["ln_in_g"], parts["ln_in_b"] = dg_in.reshape(D), db_in0.reshape(D)
    return loss_part, dx, gws, parts
```

```python
import numpy as np
import jax
import jax.numpy as jnp
from jax import lax
from jax.experimental import pallas as pl
from jax.experimental.pallas import tpu as pltpu

F32 = jnp.float32
BF16 = jnp.bfloat16
I32 = jnp.int32

D = 1024
DEPTH = 4
GRID_W = 64
N_HEADS = 8
HEAD_DIM = 64
D_ATTN = 512
KH = 8
KW = 16
POOL_WINDOWS = (2, 4, 8, 16)
D_POOL = 512
PGD = 128
D_FF = 2816
PLE_DIM = 256
N_PROJ = 4096
ALPHA = (2 * DEPTH) ** 0.25
LN_EPS = 1e-5
NEG_INF = -1e30
ATT_SCALE = HEAD_DIM ** -0.5
ADAM_LR = 0.001
ADAM_B1 = 0.9
ADAM_B2 = 0.999
ADAM_EPS = 1e-08
ADAM_WD = 0.01
ADAM_STEP = 10

N_DEV = 8
AXES = ("x", "y", "c")
FF_BLK = D_FF // 4
FF_SHARD = D_FF // N_DEV
QROWS = 8
KROWS = 16
QB = QROWS * GRID_W
KB = KROWS * GRID_W
V7X_VMEM_LIMIT = 56 * 2 ** 20
MESH = pl.DeviceIdType.MESH
ANY = pl.BlockSpec(memory_space=pl.ANY)


def _cparams(n_grid):
    return pltpu.CompilerParams(dimension_semantics=("arbitrary",) * n_grid, vmem_limit_bytes=V7X_VMEM_LIMIT)


def _nn(a, b):
    return lax.dot_general(a, b, (((1,), (0,)), ((), ())), preferred_element_type=F32)


def _nt(a, b):
    return lax.dot_general(a, b, (((1,), (1,)), ((), ())), preferred_element_type=F32)


def _tn(a, b):
    return lax.dot_general(a, b, (((0,), (0,)), ((), ())), preferred_element_type=F32)


def _sigmoid(x):
    return 1.0 / (1.0 + jnp.exp(-x))


def _ln_fwd(z, g, b):
    mu = jnp.mean(z, axis=-1, keepdims=True)
    xc = z - mu
    var = jnp.mean(xc * xc, axis=-1, keepdims=True)
    return xc * lax.rsqrt(var + LN_EPS) * g + b


def _ln_bwd(dh, z, g):
    mu = jnp.mean(z, axis=-1, keepdims=True)
    xc = z - mu
    var = jnp.mean(xc * xc, axis=-1, keepdims=True)
    rstd = lax.rsqrt(var + LN_EPS)
    xhat = xc * rstd
    dxh = dh * g
    m1 = jnp.mean(dxh, axis=-1, keepdims=True)
    m2 = jnp.mean(dxh * xhat, axis=-1, keepdims=True)
    return rstd * (dxh - m1 - xhat * m2), dh * xhat


def _colsum(x):
    return jnp.sum(x, axis=0, keepdims=True)


def _lane_cat(ref):
    return jnp.concatenate([ref[j] for j in range(ref.shape[0])], axis=1)


def _row_cat(ref):
    n, r, c = ref.shape
    return ref[...].reshape(n * r, c)


def _shards(n, r, c, li, j_of=None):
    del li
    if j_of is None:
        return pl.BlockSpec((n, r, c), lambda *_: (0, 0, 0))
    return pl.BlockSpec((n, r, c), lambda *g: (j_of(*g), 0, 0))


def _shard(r, c, li, j_of):
    del li
    return pl.BlockSpec((None, r, c), lambda *g: (j_of(*g), 0, 0))


def ln_fwd(x, g, b, name, after=()):
    s = x.shape[0]
    tm = 512
    na = len(after)

    def body(x_ref, g_ref, b_ref, *rest):
        h_ref, hb_ref = rest[na:]
        h = _ln_fwd(x_ref[...], g_ref[...], b_ref[...])
        h_ref[...] = h
        hb_ref[...] = h.astype(BF16)

    row = pl.BlockSpec((tm, D), lambda i: (i, 0))
    vec = pl.BlockSpec((1, D), lambda i: (0, 0))
    return pl.pallas_call(
        body, name=name, grid=(s // tm,), in_specs=[row, vec, vec] + [ANY] * na, out_specs=[row, row],
        out_shape=[jax.ShapeDtypeStruct((s, D), F32), jax.ShapeDtypeStruct((s, D), BF16)],
        compiler_params=_cparams(1))(x, g, b, *after)


def proj_fwd(hb, win, bias, li, j0, nj, out_dtype, name):
    s = hb.shape[0]
    bn = N_PROJ // N_DEV
    tm = 1024

    def body(a_ref, w_ref, b_ref, o_ref):
        o_ref[...] = (_nn(a_ref[...], w_ref[...]) + b_ref[...]).astype(out_dtype)

    return pl.pallas_call(
        body, name=name, grid=(s // tm, nj),
        in_specs=[pl.BlockSpec((tm, D), lambda i, j: (i, 0)),
                  _shard(D, bn, li, lambda i, j: j0 + j),
                  pl.BlockSpec((1, bn), lambda i, j: (0, j0 + j))],
        out_specs=pl.BlockSpec((tm, bn), lambda i, j: (i, j)),
        out_shape=jax.ShapeDtypeStruct((s, nj * bn), out_dtype),
        compiler_params=_cparams(2))(hb, win, bias)


def _attn_block_scalars(b, n_rows):
    ks = jnp.clip(QROWS * b - KH // 2, 0, n_rows - KROWS)
    delta = ks - QROWS * b
    k0 = pl.multiple_of(ks * GRID_W, 256)
    return ks, delta, k0


def _attn_band_logits(sb, e, b, qr, ks, delta, n_rows, klane):
    shift = ((qr - delta - (KH - 1)) * GRID_W) % KB
    bias = pltpu.roll(e, shift, 1)
    rs = jnp.clip(QROWS * b + qr - KH // 2, 0, n_rows - KH)
    lo = (rs - ks) * GRID_W
    ok = (klane >= lo) & (klane < lo + KH * GRID_W)
    return jnp.where(ok, sb + bias, NEG_INF)


def attn_fwd(proj, e_tab, name):
    s = proj.shape[0]
    n_rows = s // GRID_W
    nb = n_rows // QROWS

    def body(q_ref, k_ref, v_ref, e_ref, o_ref, s_ref, p_ref):
        b = pl.program_id(1)
        ks, delta, k0 = _attn_block_scalars(b, n_rows)
        kwin = k_ref[pl.ds(k0, KB), :]
        vwin = v_ref[pl.ds(k0, KB), :]
        q = q_ref[...] * ATT_SCALE
        lane = lax.broadcasted_iota(I32, (1, 128), 1)
        klane = lax.broadcasted_iota(I32, (1, KB), 1)
        acc = jnp.zeros((QB, 128), F32)
        for hh in range(2):
            lm = (lane // HEAD_DIM) == hh
            qh = jnp.where(lm, q, jnp.zeros_like(q))
            vh = jnp.where(lm, vwin, jnp.zeros_like(vwin))
            s_ref[...] = _nt(qh, kwin)
            e = e_ref[hh]

            def band(qr, carry):
                r0 = pl.multiple_of(qr * GRID_W, GRID_W)
                sb = _attn_band_logits(s_ref[pl.ds(r0, GRID_W), :], e, b, qr, ks, delta, n_rows, klane)
                m = jnp.max(sb, axis=1, keepdims=True)
                p = jnp.exp(sb - m)
                l = jnp.sum(p, axis=1, keepdims=True)
                p_ref[pl.ds(r0, GRID_W), :] = (p * (1.0 / l)).astype(BF16)
                return carry

            lax.fori_loop(0, QROWS, band, 0, unroll=True)
            acc = acc + _nn(p_ref[...], vh)
        o_ref[...] = acc.astype(BF16)

    return pl.pallas_call(
        body, name=name, grid=(4, nb),
        in_specs=[pl.BlockSpec((QB, 128), lambda j, b: (b, j)),
                  pl.BlockSpec((s, 128), lambda j, b: (0, 4 + j)),
                  pl.BlockSpec((s, 128), lambda j, b: (0, 8 + j)),
                  pl.BlockSpec((2, GRID_W, KB), lambda j, b: (j, 0, 0))],
        out_specs=pl.BlockSpec((QB, 128), lambda j, b: (b, j)),
        out_shape=jax.ShapeDtypeStruct((s, D_ATTN), BF16),
        scratch_shapes=[pltpu.VMEM((QB, KB), F32), pltpu.VMEM((QB, KB), BF16)],
        compiler_params=_cparams(2))(proj, proj, proj, e_tab)


_POOL_PAD = 8


def _pool_counts(s, w):
    t = lax.broadcasted_iota(I32, (s, 1), 0)
    return (jnp.minimum(t + w // 2, s) - jnp.maximum(t - w // 2, 0)).astype(F32)


def _window_sum(x, w, back_first):
    s = x.shape[0]
    z = jnp.zeros((_POOL_PAD, x.shape[1]), F32)
    xe = jnp.concatenate([z, x, z], axis=0)
    n = s + 2 * _POOL_PAD
    acc = xe + pltpu.roll(xe, 1 if back_first else n - 1, 0)
    k = 1
    while 2 * k < w:
        acc = pltpu.roll(acc, k, 0) + pltpu.roll(acc, n - k, 0)
        k *= 2
    return acc[_POOL_PAD:_POOL_PAD + s, :]


def pool_fwd(u, pool_w, pool_scale, name):
    s = u.shape[0]

    def body(u_ref, w_ref, sc_ref, pm_ref, pw_ref):
        for g, w in enumerate(POOL_WINDOWS):
            cols = slice(g * PGD, (g + 1) * PGD)
            ug = u_ref[:, cols]
            pm = (_window_sum(ug, w, True) / _pool_counts(s, w) - ug).astype(BF16)
            pm_ref[:, cols] = pm
            pw_ref[:, cols] = (_nn(pm, w_ref[g]) * sc_ref[:, cols]).astype(BF16)

    full = lambda shape: pl.BlockSpec(shape, lambda i: (0,) * len(shape))
    return pl.pallas_call(
        body, name=name, grid=(1,),
        in_specs=[full((s, D_POOL)), full((4, PGD, PGD)), full((1, D_POOL))],
        out_specs=[full((s, D_POOL)), full((s, D_POOL))],
        out_shape=[jax.ShapeDtypeStruct((s, D_POOL), BF16)] * 2,
        compiler_params=_cparams(1))(u, pool_w, pool_scale)


def merge_fwd(a, pw, wao, wpo, proj, li, name):
    s = a.shape[0]
    tm, tn = 512, 512
    nt = D // tn
    per = tn // 128

    def body(a_ref, pw_ref, wa_ref, wp_ref, ga_ref, gb_ref, mg_ref, ya_ref, yp_ref):
        ya = _nn(a_ref[...], _lane_cat(wa_ref))
        yp = _nn(pw_ref[...], _lane_cat(wp_ref))
        mg = _sigmoid(ga_ref[...].astype(F32)) * ya + _sigmoid(gb_ref[...].astype(F32)) * yp
        mg_ref[...] = mg.astype(BF16)
        ya_ref[...] = ya.astype(BF16)
        yp_ref[...] = yp.astype(BF16)

    act = pl.BlockSpec((tm, D_ATTN), lambda i, j: (i, 0))
    wsp = _shards(per, D_ATTN, 128, li, lambda i, j: j)
    out = pl.BlockSpec((tm, tn), lambda i, j: (i, j))
    ga0 = (3 * D_ATTN + D_POOL) // tn
    return pl.pallas_call(
        body, name=name, grid=(s // tm, nt),
        in_specs=[act, act, wsp, wsp,
                  pl.BlockSpec((tm, tn), lambda i, j: (i, ga0 + j)),
                  pl.BlockSpec((tm, tn), lambda i, j: (i, ga0 + nt + j))],
        out_specs=[out, out, out],
        out_shape=[jax.ShapeDtypeStruct((s, D), BF16)] * 3,
        compiler_params=_cparams(2))(a, pw, wao, wpo, proj, proj)


def mix_ln_fwd(mg, wmix, h0, g, b, li, name):
    s = mg.shape[0]
    tm = 256

    def body(mg_ref, w_ref, h0_ref, g_ref, b_ref, z_ref, h_ref, hb_ref):
        z = ALPHA * h0_ref[...] + _nn(mg_ref[...], _row_cat(w_ref))
        h = _ln_fwd(z, g_ref[...], b_ref[...])
        z_ref[...] = z
        h_ref[...] = h
        hb_ref[...] = h.astype(BF16)

    row = pl.BlockSpec((tm, D), lambda i: (i, 0))
    vec = pl.BlockSpec((1, D), lambda i: (0, 0))
    return pl.pallas_call(
        body, name=name, grid=(s // tm,),
        in_specs=[row, _shards(N_DEV, D // N_DEV, D, li), row, vec, vec],
        out_specs=[row, row, row],
        out_shape=[jax.ShapeDtypeStruct((s, D), F32), jax.ShapeDtypeStruct((s, D), F32),
                   jax.ShapeDtypeStruct((s, D), BF16)],
        compiler_params=_cparams(1))(mg, wmix, h0, g, b)


def up_fwd(hb, wup, li, name):
    s = hb.shape[0]
    tm = 1024

    def body(a_ref, w_ref, o_ref):
        o_ref[...] = _nn(a_ref[...], w_ref[...]).astype(BF16)

    return pl.pallas_call(
        body, name=name, grid=(s // tm, N_DEV),
        in_specs=[pl.BlockSpec((tm, D), lambda i, j: (i, 0)), _shard(D, FF_BLK, li, lambda i, j: j)],
        out_specs=pl.BlockSpec((None, tm, FF_BLK), lambda i, j: (j, i, 0)),
        out_shape=jax.ShapeDtypeStruct((N_DEV, s, FF_BLK), BF16),
        compiler_params=_cparams(2))(hb, wup)


_SQRT_HALF = 0.7071067811865476
_INV_SQRT_2PI = 0.3989422804014327


def _shift_rows(x, prev_row, next_row):
    n = x.shape[0]
    r = lax.broadcasted_iota(I32, (n, 1), 0)
    back = jnp.where(r == 0, prev_row, pltpu.roll(x, 1, 0))
    fwd = jnp.where(r == n - 1, next_row, pltpu.roll(x, n - 1, 0))
    return back, fwd


HALO = 16


def _halo_maps(tm, s):
    th = tm // HALO
    return (lambda i: jnp.maximum(i * th - 1, 0)), (lambda i: jnp.minimum((i + 1) * th, s // HALO - 1))


def _slab_specs(tm, s, blk_of):
    before, after = _halo_maps(tm, s)
    main = pl.BlockSpec((None, tm, FF_BLK), lambda c, i: (blk_of(c), i, 0))
    prev = pl.BlockSpec((None, HALO, FF_BLK), lambda c, i: (blk_of(c), before(i), 0))
    nxt = pl.BlockSpec((None, HALO, FF_BLK), lambda c, i: (blk_of(c), after(i), 0))
    return main, prev, nxt


def ffn_act_fwd(up, conv_w, conv_b, name):
    s = up.shape[1]
    tm = 512
    nt = s // tm
    hv_main, _, _ = _slab_specs(tm, s, lambda c: c)
    hg_main, hg_prev, hg_next = _slab_specs(tm, s, lambda c: 4 + c)

    def body(hv_ref, hg_ref, hp_ref, hn_ref, cw_ref, cb_ref, t_ref):
        i = pl.program_id(1)
        hg = hg_ref[...].astype(F32)
        prow = jnp.where(i == 0, 0.0, hp_ref[...].astype(F32)[HALO - 1:HALO, :])
        nrow = jnp.where(i == nt - 1, 0.0, hn_ref[...].astype(F32)[0:1, :])
        back, fwd = _shift_rows(hg, prow, nrow)
        c = back * cw_ref[0:1, :] + hg * cw_ref[1:2, :] + fwd * cw_ref[2:3, :] + cb_ref[...]
        act = 0.5 * c * (1.0 + lax.erf(c * _SQRT_HALF))
        t_ref[...] = (act * hv_ref[...].astype(F32)).astype(BF16)

    return pl.pallas_call(
        body, name=name, grid=(4, nt),
        in_specs=[hv_main, hg_main, hg_prev, hg_next,
                  pl.BlockSpec((None, 3, FF_BLK), lambda c, i: (c, 0, 0)),
                  pl.BlockSpec((None, 1, FF_BLK), lambda c, i: (c, 0, 0))],
        out_specs=pl.BlockSpec((None, tm, FF_BLK), lambda c, i: (c, i, 0)),
        out_shape=jax.ShapeDtypeStruct((4, s, FF_BLK), BF16),
        compiler_params=_cparams(2))(up, up, up, up, conv_w, conv_b)


def down_ple_ln_fwd(t, wdown, hb, wpg, pb, wpp, h1, g, b, li, name):
    s = hb.shape[0]
    tm = 256

    def body(t_ref, wd_ref, hb_ref, wpg_ref, p_ref, wpp_ref, h1_ref, g_ref, b_ref,
             z_ref, h_ref, hbo_ref, pg_ref, pp_ref):
        wd = _row_cat(wd_ref)
        ffn = _nn(t_ref[0], wd[0:FF_BLK, :])
        for c in range(1, 4):
            ffn = ffn + _nn(t_ref[c], wd[c * FF_BLK:(c + 1) * FF_BLK, :])
        pg = _nn(hb_ref[...], _row_cat(wpg_ref))
        pp = _nn(p_ref[...], _lane_cat(wpp_ref))
        z = ALPHA * h1_ref[...] + ffn + _sigmoid(pg) * pp
        h = _ln_fwd(z, g_ref[...], b_ref[...])
        z_ref[...] = z
        h_ref[...] = h
        hbo_ref[...] = h.astype(BF16)
        pg_ref[...] = pg.astype(BF16)
        pp_ref[...] = pp.astype(BF16)

    row = pl.BlockSpec((tm, D), lambda i: (i, 0))
    vec = pl.BlockSpec((1, D), lambda i: (0, 0))
    return pl.pallas_call(
        body, name=name, grid=(s // tm,),
        in_specs=[pl.BlockSpec((4, tm, FF_BLK), lambda i: (0, i, 0)),
                  _shards(N_DEV, FF_SHARD, D, li),
                  row, _shards(N_DEV, D // N_DEV, D, li),
                  pl.BlockSpec((tm, PLE_DIM), lambda i: (i, 0)),
                  _shards(N_DEV, PLE_DIM, 128, li),
                  row, vec, vec],
        out_specs=[row] * 5,
        out_shape=[jax.ShapeDtypeStruct((s, D), F32), jax.ShapeDtypeStruct((s, D), F32),
                   jax.ShapeDtypeStruct((s, D), BF16), jax.ShapeDtypeStruct((s, D), BF16),
                   jax.ShapeDtypeStruct((s, D), BF16)],
        compiler_params=_cparams(1))(t, wdown, hb, wpg, pb, wpp, h1, g, b)


def loss_bwd(h, target, name):
    s = h.shape[0]
    tm = 512

    def body(h_ref, t_ref, dh_ref, l_ref):
        @pl.when(pl.program_id(0) == 0)
        def _():
            l_ref[...] = jnp.zeros_like(l_ref)
        e = h_ref[...] - t_ref[...]
        dh_ref[...] = e * (1.0 / D)
        l_ref[...] += 0.5 * jnp.sum(jnp.mean(e * e, axis=-1, keepdims=True), axis=0, keepdims=True)

    row = pl.BlockSpec((tm, D), lambda i: (i, 0))
    return pl.pallas_call(
        body, name=name, grid=(s // tm,), in_specs=[row, row],
        out_specs=[row, pl.BlockSpec((1, 1), lambda i: (0, 0))],
        out_shape=[jax.ShapeDtypeStruct((s, D), F32), jax.ShapeDtypeStruct((1, 1), F32)],
        compiler_params=_cparams(1))(h, target)


def ln_bwd(dh, z, g, name, after=()):
    s = dh.shape[0]
    tm = 512
    na = len(after)

    def body(dh_ref, z_ref, g_ref, *rest):
        dz_ref, dg_ref, db_ref = rest[na:]

        @pl.when(pl.program_id(0) == 0)
        def _():
            dg_ref[...] = jnp.zeros_like(dg_ref)
            db_ref[...] = jnp.zeros_like(db_ref)
        dh = dh_ref[...]
        dz, dgx = _ln_bwd(dh, z_ref[...], g_ref[...])
        dz_ref[...] = dz
        dg_ref[...] += _colsum(dgx)
        db_ref[...] += _colsum(dh)

    row = pl.BlockSpec((tm, D), lambda i: (i, 0))
    vec = pl.BlockSpec((1, D), lambda i: (0, 0))
    return pl.pallas_call(
        body, name=name, grid=(s // tm,), in_specs=[row, row, vec] + [ANY] * na, out_specs=[row, vec, vec],
        out_shape=[jax.ShapeDtypeStruct((s, D), F32), jax.ShapeDtypeStruct((1, D), F32),
                   jax.ShapeDtypeStruct((1, D), F32)],
        compiler_params=_cparams(1))(dh, z, g, *after)


def ln2_ple_bwd(dh, z, g, pg, pp, name, after=()):
    s = dh.shape[0]
    tm = 512
    na = len(after)

    def body(dh_ref, z_ref, g_ref, pg_ref, pp_ref, *rest):
        dz_ref, dzb_ref, dpg_ref, dpp_ref, dg_ref, db_ref = rest[na:]

        @pl.when(pl.program_id(0) == 0)
        def _():
            dg_ref[...] = jnp.zeros_like(dg_ref)
            db_ref[...] = jnp.zeros_like(db_ref)
        dh = dh_ref[...]
        dz, dgx = _ln_bwd(dh, z_ref[...], g_ref[...])
        sg = _sigmoid(pg_ref[...].astype(F32))
        dz_ref[...] = dz
        dzb_ref[...] = dz.astype(BF16)
        dpg_ref[...] = (dz * pp_ref[...].astype(F32) * sg * (1.0 - sg)).astype(BF16)
        dpp_ref[...] = (dz * sg).astype(BF16)
        dg_ref[...] += _colsum(dgx)
        db_ref[...] += _colsum(dh)

    row = pl.BlockSpec((tm, D), lambda i: (i, 0))
    vec = pl.BlockSpec((1, D), lambda i: (0, 0))
    return pl.pallas_call(
        body, name=name, grid=(s // tm,), in_specs=[row, row, vec, row, row] + [ANY] * na,
        out_specs=[row, row, row, row, vec, vec],
        out_shape=[jax.ShapeDtypeStruct((s, D), F32)] + [jax.ShapeDtypeStruct((s, D), BF16)] * 3
        + [jax.ShapeDtypeStruct((1, D), F32)] * 2,
        compiler_params=_cparams(1))(dh, z, g, pg, pp, *after)


def wgrad_rows(a, dy, name):
    s, k = a.shape
    n = dy.shape[1]
    kb = k // N_DEV

    def body(a_ref, dy_ref, o_ref):
        o_ref[...] = _tn(a_ref[...], dy_ref[...]).astype(BF16)

    return pl.pallas_call(
        body, name=name, grid=(N_DEV,),
        in_specs=[pl.BlockSpec((s, kb), lambda j: (0, j)), pl.BlockSpec((s, n), lambda j: (0, 0))],
        out_specs=pl.BlockSpec((None, kb, n), lambda j: (j, 0, 0)),
        out_shape=jax.ShapeDtypeStruct((N_DEV, kb, n), BF16),
        compiler_params=_cparams(1))(a, dy)


def wgrad_cols(a, dy, name, with_colsum=False):
    s, k = a.shape
    n = dy.shape[1]
    nb = n // N_DEV

    def body(a_ref, dy_ref, o_ref, *cs_ref):
        dy = dy_ref[...]
        o_ref[...] = _tn(a_ref[...], dy).astype(BF16)
        if with_colsum:
            cs_ref[0][...] = _colsum(dy.astype(F32))

    out_specs = [pl.BlockSpec((None, k, nb), lambda j: (j, 0, 0))]
    out_shape = [jax.ShapeDtypeStruct((N_DEV, k, nb), BF16)]
    if with_colsum:
        out_specs.append(pl.BlockSpec((1, nb), lambda j: (0, j)))
        out_shape.append(jax.ShapeDtypeStruct((1, n), F32))
    res = pl.pallas_call(
        body, name=name, grid=(N_DEV,),
        in_specs=[pl.BlockSpec((s, k), lambda j: (0, 0)), pl.BlockSpec((s, nb), lambda j: (0, j))],
        out_specs=out_specs, out_shape=out_shape,
        compiler_params=_cparams(1))(a, dy)
    return res if with_colsum else res[0]


def wgrad_down(t, dy, name):
    _, s, k = t.shape
    n = dy.shape[1]

    def body(a_ref, dy_ref, o_ref):
        o_ref[...] = _tn(a_ref[...], dy_ref[...]).astype(BF16)

    return pl.pallas_call(
        body, name=name, grid=(4,),
        in_specs=[pl.BlockSpec((None, s, k), lambda j: (j, 0, 0)), pl.BlockSpec((s, n), lambda j: (0, 0))],
        out_specs=pl.BlockSpec((None, k, n), lambda j: (j, 0, 0)),
        out_shape=jax.ShapeDtypeStruct((4, k, n), BF16),
        compiler_params=_cparams(1))(t, dy)


def wgrad_up(a, dhv, dhg, name):
    s, k = a.shape

    def body(a_ref, dv_ref, dg_ref, o_ref):
        j = pl.program_id(0)

        @pl.when(j < 4)
        def _():
            o_ref[...] = _tn(a_ref[...], dv_ref[...]).astype(BF16)

        @pl.when(j >= 4)
        def _():
            o_ref[...] = _tn(a_ref[...], dg_ref[...]).astype(BF16)

    return pl.pallas_call(
        body, name=name, grid=(N_DEV,),
        in_specs=[pl.BlockSpec((s, k), lambda j: (0, 0)),
                  pl.BlockSpec((None, s, FF_BLK), lambda j: (jnp.minimum(j, 3), 0, 0)),
                  pl.BlockSpec((None, s, FF_BLK), lambda j: (jnp.maximum(j - 4, 0), 0, 0))],
        out_specs=pl.BlockSpec((None, k, FF_BLK), lambda j: (j, 0, 0)),
        out_shape=jax.ShapeDtypeStruct((N_DEV, k, FF_BLK), BF16),
        compiler_params=_cparams(1))(a, dhv, dhg)


def ffn_act_bwd(dzb, wdown, up, conv_w, conv_b, li, name):
    s = up.shape[1]
    tm = 512
    nt = s // tm
    before, after = _halo_maps(tm, s)
    hv_main, hv_prev, hv_next = _slab_specs(tm, s, lambda c: c)
    hg_main, hg_prev, hg_next = _slab_specs(tm, s, lambda c: 4 + c)

    def dc_of(dz, wd, hv, hg, back, fwd, cw_ref, cb_ref):
        dt = _nt(dz, wd)
        c = back * cw_ref[0:1, :] + hg * cw_ref[1:2, :] + fwd * cw_ref[2:3, :] + cb_ref[...]
        cdf = 0.5 * (1.0 + lax.erf(c * _SQRT_HALF))
        pdf = jnp.exp(-0.5 * c * c) * _INV_SQRT_2PI
        return dt, c * cdf, dt * hv * (cdf + c * pdf)

    def body(dz_ref, dzp_ref, dzn_ref, wd_ref, hv_ref, hvp_ref, hvn_ref, hg_ref, hgp_ref, hgn_ref, cw_ref, cb_ref,
             dhv_ref, dhg_ref, dcw_ref, dcb_ref):
        i = pl.program_id(1)

        @pl.when(i == 0)
        def _():
            dcw_ref[...] = jnp.zeros_like(dcw_ref)
            dcb_ref[...] = jnp.zeros_like(dcb_ref)

        wd = _row_cat(wd_ref)
        hg = hg_ref[...].astype(F32)
        hgp = hgp_ref[...].astype(F32)
        hgn = hgn_ref[...].astype(F32)
        first, last = i == 0, i == nt - 1
        e = HALO - 1
        back, fwd = _shift_rows(hg, jnp.where(first, 0.0, hgp[e:e + 1, :]), jnp.where(last, 0.0, hgn[0:1, :]))
        dt, act, dc = dc_of(dz_ref[...], wd, hv_ref[...].astype(F32), hg, back, fwd, cw_ref, cb_ref)
        dhv_ref[...] = (dt * act).astype(BF16)
        bp, fp = _shift_rows(hgp, hgp[0:1, :], hg[0:1, :])
        _, _, dcp = dc_of(dzp_ref[...], wd, hvp_ref[...].astype(F32), hgp, bp, fp, cw_ref, cb_ref)
        bn, fn = _shift_rows(hgn, hg[tm - 1:tm, :], hgn[e:e + 1, :])
        _, _, dcn = dc_of(dzn_ref[...], wd, hvn_ref[...].astype(F32), hgn, bn, fn, cw_ref, cb_ref)
        dc_back, dc_fwd = _shift_rows(dc, jnp.where(first, 0.0, dcp[e:e + 1, :]), jnp.where(last, 0.0, dcn[0:1, :]))
        dhg_ref[...] = (dc_fwd * cw_ref[0:1, :] + dc * cw_ref[1:2, :] + dc_back * cw_ref[2:3, :]).astype(BF16)
        dcw_ref[0:1, :] += _colsum(dc * back)
        dcw_ref[1:2, :] += _colsum(dc * hg)
        dcw_ref[2:3, :] += _colsum(dc * fwd)
        dcb_ref[...] += _colsum(dc)

    out_slab = pl.BlockSpec((None, tm, FF_BLK), lambda c, i: (c, i, 0))
    cw_spec = pl.BlockSpec((None, 3, FF_BLK), lambda c, i: (c, 0, 0))
    cb_spec = pl.BlockSpec((None, 1, FF_BLK), lambda c, i: (c, 0, 0))
    return pl.pallas_call(
        body, name=name, grid=(4, nt),
        in_specs=[pl.BlockSpec((tm, D), lambda c, i: (i, 0)),
                  pl.BlockSpec((HALO, D), lambda c, i: (before(i), 0)),
                  pl.BlockSpec((HALO, D), lambda c, i: (after(i), 0)),
                  _shards(2, FF_SHARD, D, li, lambda c, i: c),
                  hv_main, hv_prev, hv_next, hg_main, hg_prev, hg_next, cw_spec, cb_spec],
        out_specs=[out_slab, out_slab, cw_spec, cb_spec],
        out_shape=[jax.ShapeDtypeStruct((4, s, FF_BLK), BF16), jax.ShapeDtypeStruct((4, s, FF_BLK), BF16),
                   jax.ShapeDtypeStruct((4, 3, FF_BLK), F32), jax.ShapeDtypeStruct((4, 1, FF_BLK), F32)],
        compiler_params=_cparams(2))(dzb, dzb, dzb, wdown, up, up, up, up, up, up, conv_w, conv_b)


def dh1_ln1_bwd(dz2, dpg, wpg, dhv, dhg, wup, z1, g1, li, name):
    s = dz2.shape[0]
    tm = 256

    def body(dz2_ref, dpg_ref, wpg_ref, dhv_ref, dhg_ref, wup_ref, z1_ref, g_ref, dz_ref, dzb_ref, dg_ref, db_ref):
        @pl.when(pl.program_id(0) == 0)
        def _():
            dg_ref[...] = jnp.zeros_like(dg_ref)
            db_ref[...] = jnp.zeros_like(db_ref)
        dh = ALPHA * dz2_ref[...] + _nt(dpg_ref[...], _row_cat(wpg_ref))
        for c in range(4):
            dh = dh + _nt(dhv_ref[c], wup_ref[c]) + _nt(dhg_ref[c], wup_ref[4 + c])
        dz, dgx = _ln_bwd(dh, z1_ref[...], g_ref[...])
        dz_ref[...] = dz
        dzb_ref[...] = dz.astype(BF16)
        dg_ref[...] += _colsum(dgx)
        db_ref[...] += _colsum(dh)

    row = pl.BlockSpec((tm, D), lambda i: (i, 0))
    vec = pl.BlockSpec((1, D), lambda i: (0, 0))
    slab = pl.BlockSpec((4, tm, FF_BLK), lambda i: (0, i, 0))
    return pl.pallas_call(
        body, name=name, grid=(s // tm,),
        in_specs=[row, row, _shards(N_DEV, D // N_DEV, D, li), slab, slab, _shards(N_DEV, D, FF_BLK, li), row, vec],
        out_specs=[row, row, vec, vec],
        out_shape=[jax.ShapeDtypeStruct((s, D), F32), jax.ShapeDtypeStruct((s, D), BF16),
                   jax.ShapeDtypeStruct((1, D), F32), jax.ShapeDtypeStruct((1, D), F32)],
        compiler_params=_cparams(1))(dz2, dpg, wpg, dhv, dhg, wup, z1, g1)


def merge_bwd(dz1b, wmix, proj, ya, yp, li, name):
    s = dz1b.shape[0]
    tm, tn = 512, 512
    nt = D // tn
    per = tn // (D // N_DEV)
    ga0 = (3 * D_ATTN + D_POOL) // tn

    def body(dz_ref, w_ref, ga_ref, gb_ref, ya_ref, yp_ref, dya_ref, dyp_ref, dga_ref, dgb_ref):
        dm = _nt(dz_ref[...], _row_cat(w_ref))
        sa = _sigmoid(ga_ref[...].astype(F32))
        sb = _sigmoid(gb_ref[...].astype(F32))
        dya_ref[...] = (dm * sa).astype(BF16)
        dyp_ref[...] = (dm * sb).astype(BF16)
        dga_ref[...] = (dm * ya_ref[...].astype(F32) * sa * (1.0 - sa)).astype(BF16)
        dgb_ref[...] = (dm * yp_ref[...].astype(F32) * sb * (1.0 - sb)).astype(BF16)

    tile = pl.BlockSpec((tm, tn), lambda i, j: (i, j))
    return pl.pallas_call(
        body, name=name, grid=(s // tm, nt),
        in_specs=[pl.BlockSpec((tm, D), lambda i, j: (i, 0)),
                  _shards(per, D // N_DEV, D, li, lambda i, j: j),
                  pl.BlockSpec((tm, tn), lambda i, j: (i, ga0 + j)),
                  pl.BlockSpec((tm, tn), lambda i, j: (i, ga0 + nt + j)),
                  tile, tile],
        out_specs=[tile] * 4,
        out_shape=[jax.ShapeDtypeStruct((s, D), BF16)] * 4,
        compiler_params=_cparams(2))(dz1b, wmix, proj, proj, ya, yp)


def attn_out_bwd(dya, wao, li, name):
    s = dya.shape[0]
    tm = 512

    def body(d_ref, w_ref, o_ref):
        o_ref[...] = _nt(d_ref[...], _lane_cat(w_ref)).astype(BF16)

    return pl.pallas_call(
        body, name=name, grid=(s // tm,),
        in_specs=[pl.BlockSpec((tm, D), lambda i: (i, 0)), _shards(N_DEV, D_ATTN, 128, li)],
        out_specs=pl.BlockSpec((tm, D_ATTN), lambda i: (i, 0)),
        out_shape=jax.ShapeDtypeStruct((s, D_ATTN), BF16),
        compiler_params=_cparams(1))(dya, wao)


def pool_bwd(dyp, wpo, pm, pool_w, pool_scale, li, name):
    s = dyp.shape[0]

    def body(dyp_ref, wpo_ref, pm_ref, w_ref, sc_ref, du_ref, dw_ref, dsc_ref):
        wpo = _lane_cat(wpo_ref)
        dyp = dyp_ref[...]
        for g, w in enumerate(POOL_WINDOWS):
            cols = slice(g * PGD, (g + 1) * PGD)
            dpw = _nt(dyp, wpo[g * PGD:(g + 1) * PGD, :])
            pmg = pm_ref[:, cols]
            dsc_ref[:, cols] = _colsum(dpw * _nn(pmg, w_ref[g]))
            dpmw = (dpw * sc_ref[:, cols]).astype(BF16)
            dw_ref[g] = _tn(pmg, dpmw)
            dpm = _nt(dpmw, w_ref[g])
            du_ref[:, cols] = (_window_sum(dpm / _pool_counts(s, w), w, False) - dpm).astype(BF16)

    full = lambda shape: pl.BlockSpec(shape, lambda i: (0,) * len(shape))
    return pl.pallas_call(
        body, name=name, grid=(1,),
        in_specs=[full((s, D)), _shards(N_DEV, D_POOL, 128, li), full((s, D_POOL)), full((4, PGD, PGD)),
                  full((1, D_POOL))],
        out_specs=[full((s, D_POOL)), full((4, PGD, PGD)), full((1, D_POOL))],
        out_shape=[jax.ShapeDtypeStruct((s, D_POOL), BF16), jax.ShapeDtypeStruct((4, PGD, PGD), F32),
                   jax.ShapeDtypeStruct((1, D_POOL), F32)],
        compiler_params=_cparams(1))(dyp, wpo, pm, pool_w, pool_scale)


_SKEW_BASE = KB - (GRID_W - KW) - GRID_W


def attn_bwd(proj, da, e_rev, name):
    s = proj.shape[0]
    n_rows = s // GRID_W
    nb = n_rows // QROWS

    def body(q_ref, k_ref, v_ref, do_ref, e_ref, dq_ref, dk_ref, dv_ref, g_ref,
             s_ref, dp_ref, ds_ref, p_ref, dk_acc, dv_acc):
        b = pl.program_id(1)
        ks, delta, k0 = _attn_block_scalars(b, n_rows)

        @pl.when(b == 0)
        def _():
            dk_acc[...] = jnp.zeros_like(dk_acc)
            dv_acc[...] = jnp.zeros_like(dv_acc)
            g_ref[...] = jnp.zeros_like(g_ref)

        kwin = k_ref[pl.ds(k0, KB), :]
        vwin = v_ref[pl.ds(k0, KB), :]
        ri = lax.broadcasted_iota(I32, (QB, QB), 0)
        ci = lax.broadcasted_iota(I32, (QB, QB), 1)
        rev = jnp.where(ri + ci == QB - 1, 1.0, 0.0).astype(BF16)
        q = _nn(rev, q_ref[...]).astype(BF16) * ATT_SCALE
        do = _nn(rev, do_ref[...]).astype(BF16)
        lane = lax.broadcasted_iota(I32, (1, 128), 1)
        klane = lax.broadcasted_iota(I32, (1, KB), 1)
        dq = jnp.zeros((QB, 128), F32)
        for hh in range(2):
            lm = (lane // HEAD_DIM) == hh
            qh = jnp.where(lm, q, jnp.zeros_like(q))
            doh = jnp.where(lm, do, jnp.zeros_like(do))
            kh = jnp.where(lm, kwin, jnp.zeros_like(kwin))
            s_ref[...] = _nt(qh, kwin)
            dp_ref[...] = _nt(doh, vwin)
            e = e_ref[hh]

            def band(ib, carry):
                r0 = pl.multiple_of(ib * GRID_W, GRID_W)
                rows = pl.ds(r0, GRID_W)
                sb = _attn_band_logits(s_ref[rows, :], e, b, QROWS - 1 - ib, ks, delta, n_rows, klane)
                m = jnp.max(sb, axis=1, keepdims=True)
                p = jnp.exp(sb - m)
                p = p * (1.0 / jnp.sum(p, axis=1, keepdims=True))
                dp = dp_ref[rows, :]
                ds_ref[rows, :] = p * (dp - jnp.sum(p * dp, axis=1, keepdims=True))
                p_ref[rows, :] = p.astype(BF16)
                return carry

            lax.fori_loop(0, QROWS, band, 0, unroll=True)
            ds = ds_ref[...]
            dsb = ds.astype(BF16)
            dq = dq + _nn(dsb, kh) * ATT_SCALE
            dk_acc[pl.ds(k0, KB), :] += _tn(dsb, qh)
            dv_acc[pl.ds(k0, KB), :] += _tn(p_ref[...], doh)
            t = pltpu.roll(ds.reshape(QROWS, GRID_W, KB), _SKEW_BASE, 2, stride=1, stride_axis=1).sum(axis=1)
            g = jnp.zeros((1, KB), F32)
            for ib in range(QROWS):
                shift = ((1 + ib + delta) * GRID_W) % KB
                g = g + pltpu.roll(t[ib:ib + 1, :], shift, 1)
            g_ref[hh] += g
        dq_ref[...] = _nn(rev, dq.astype(BF16)).astype(BF16)

        @pl.when(b == nb - 1)
        def _():
            dk_ref[...] = dk_acc[...].astype(BF16)
            dv_ref[...] = dv_acc[...].astype(BF16)

    col = pl.BlockSpec((s, 128), lambda j, b: (0, j))
    return pl.pallas_call(
        body, name=name, grid=(4, nb),
        in_specs=[pl.BlockSpec((QB, 128), lambda j, b: (b, j)),
                  pl.BlockSpec((s, 128), lambda j, b: (0, 4 + j)),
                  pl.BlockSpec((s, 128), lambda j, b: (0, 8 + j)),
                  pl.BlockSpec((QB, 128), lambda j, b: (b, j)),
                  pl.BlockSpec((2, GRID_W, KB), lambda j, b: (j, 0, 0))],
        out_specs=[pl.BlockSpec((QB, 128), lambda j, b: (b, j)), col, col,
                   pl.BlockSpec((2, 1, KB), lambda j, b: (j, 0, 0))],
        out_shape=[jax.ShapeDtypeStruct((s, D_ATTN), BF16)] * 3 + [jax.ShapeDtypeStruct((N_HEADS, 1, KB), F32)],
        scratch_shapes=[pltpu.VMEM((QB, KB), F32), pltpu.VMEM((QB, KB), F32), pltpu.VMEM((QB, KB), F32),
                        pltpu.VMEM((QB, KB), BF16), pltpu.VMEM((s, 128), F32), pltpu.VMEM((s, 128), F32)],
        compiler_params=_cparams(2))(proj, proj, proj, da, e_rev)


def dh0_bwd(dz1, dproj, win, li, name):
    s = dz1.shape[0]
    tm = 256
    bn = N_PROJ // N_DEV

    def body(dz_ref, dp_ref, w_ref, o_ref):
        acc = ALPHA * dz_ref[...]
        for j in range(N_DEV):
            acc = acc + _nt(dp_ref[:, j * bn:(j + 1) * bn], w_ref[j])
        o_ref[...] = acc

    row = pl.BlockSpec((tm, D), lambda i: (i, 0))
    return pl.pallas_call(
        body, name=name, grid=(s // tm,),
        in_specs=[row, pl.BlockSpec((tm, N_PROJ), lambda i: (i, 0)), _shards(N_DEV, D, bn, li)],
        out_specs=row, out_shape=jax.ShapeDtypeStruct((s, D), F32),
        compiler_params=_cparams(1))(dz1, dproj, win)


def _coords():
    return lax.axis_index("x"), lax.axis_index("y"), lax.axis_index("c")


def _dev_index(px, py, pc):
    return 4 * px + 2 * py + pc


def all_gather(arrs, name):
    n = len(arrs)

    def body(*refs):
        ins, outs = refs[:n], refs[n:2 * n]
        send_sems, recv_sems, local_sems = refs[2 * n:]
        x, y, c = _coords()
        me, sibling = (x, y, c), (x, y, 1 - c)
        chips = [(1 - x, y), (x, 1 - y), (1 - x, 1 - y)]

        def copy(a, k, block, to, src=None):
            dst = outs[a].at[_dev_index(*block)]
            return pltpu.make_async_remote_copy(
                src_ref=dst if src is None else src, dst_ref=dst,
                send_sem=send_sems.at[a, k], recv_sem=recv_sems.at[a, k], device_id=to, device_id_type=MESH)

        mine = [pltpu.make_async_copy(ins[a], outs[a].at[_dev_index(*me)], local_sems.at[a]) for a in range(n)]
        for cp in mine:
            cp.start()
        first = []
        for a in range(n):
            first.append(copy(a, 0, me, sibling, src=ins[a]))
            first += [copy(a, 1 + j, me, (*chip, c), src=ins[a]) for j, chip in enumerate(chips)]
        for cp in first:
            cp.start()
        passed = []
        for j, chip in enumerate(chips):
            for a in range(n):
                copy(a, 1 + j, (*chip, c), me).wait_recv()
                cp = copy(a, 4 + j, (*chip, c), sibling)
                cp.start()
                passed.append(cp)
        for a in range(n):
            copy(a, 0, sibling, me).wait_recv()
            for j, chip in enumerate(chips):
                copy(a, 4 + j, (*chip, 1 - c), me).wait_recv()
        for cp in first + passed:
            cp.wait_send()
        for cp in mine:
            cp.wait()

    return pl.pallas_call(
        body, name=name,
        out_shape=[jax.ShapeDtypeStruct((N_DEV,) + a.shape, a.dtype) for a in arrs],
        in_specs=[ANY] * n, out_specs=[ANY] * n,
        scratch_shapes=[pltpu.SemaphoreType.DMA((n, 7)), pltpu.SemaphoreType.DMA((n, 7)),
                        pltpu.SemaphoreType.DMA((n,))],
    )(*arrs)


def exchange_sibling(grads, name):
    n = len(grads)

    def body(*refs):
        ins, outs = refs[:n], refs[n:2 * n]
        send_sems, recv_sems = refs[2 * n:]
        x, y, c = _coords()
        sibling = (x, y, 1 - c)
        copies = []
        for a in range(n):
            for k in range(4):
                blk = _dev_index(x ^ (k & 1), y ^ (k >> 1), 1 - c)
                copies.append(pltpu.make_async_remote_copy(
                    src_ref=ins[a].at[blk], dst_ref=outs[a].at[k],
                    send_sem=send_sems.at[a, k], recv_sem=recv_sems.at[a, k], device_id=sibling, device_id_type=MESH))
        for cp in copies:
            cp.start()
        for cp in copies:
            cp.wait()

    return pl.pallas_call(
        body, name=name,
        out_shape=[jax.ShapeDtypeStruct((4,) + g.shape[1:], g.dtype) for g in grads],
        in_specs=[ANY] * n, out_specs=[ANY] * n,
        scratch_shapes=[pltpu.SemaphoreType.DMA((n, 4)), pltpu.SemaphoreType.DMA((n, 4))],
    )(*grads)


HBM = pl.BlockSpec(memory_space=pltpu.HBM)
SEM = pl.BlockSpec(memory_space=pltpu.SEMAPHORE)
_EFFECT = pltpu.SideEffectType.DATAFLOW_SIDE_EFFECTING
_TOKEN = jax.ShapeDtypeStruct((8, 128), F32)


def _in_hbm(a):
    return pltpu.with_memory_space_constraint(a, pltpu.HBM)


def _hbm_like(a):
    return pltpu.HBM(a.shape, a.dtype)


def _peers(x, y, c):
    return [(x, y, 1 - c), (1 - x, y, c), (x, 1 - y, c), (1 - x, 1 - y, c)]


def ag_own(srcs, li, name):
    n = len(srcs)

    def body(*refs):
        src, land, sems = refs[:n], refs[n:2 * n], refs[2 * n]
        me = _dev_index(*_coords())
        copies = [pltpu.make_async_copy(src[a].at[li], land[a].at[me], sems.at[a]) for a in range(n)]
        for cp in copies:
            cp.start()
        for cp in copies:
            cp.wait()

    return pl.pallas_call(
        body, name=name, out_shape=[jax.ShapeDtypeStruct((N_DEV,) + s.shape[1:], s.dtype) for s in srcs],
        in_specs=[ANY] * n, out_specs=[ANY] * n, scratch_shapes=[pltpu.SemaphoreType.DMA((n,))])(*srcs)


def ag_start(srcs, lands, li, name):
    n = len(srcs)

    def body(*refs):
        src, land = refs[:n], refs[n:2 * n]
        send_sem, recv_sem, token = refs[2 * n], refs[2 * n + 1], refs[-1]
        x, y, c = _coords()
        me = _dev_index(x, y, c)
        for k, peer in enumerate(_peers(x, y, c)):
            for a in range(n):
                pltpu.make_async_remote_copy(src_ref=src[a].at[li], dst_ref=land[a].at[me], send_sem=send_sem.at[k],
                                             recv_sem=recv_sem.at[k], device_id=peer, device_id_type=MESH).start()
        token[...] = jnp.zeros_like(token)

    res = pl.pallas_call(
        body, name=name,
        out_shape=(pltpu.SemaphoreType.DMA((4,)), pltpu.SemaphoreType.DMA((4,)), *[_hbm_like(l) for l in lands], _TOKEN),
        in_specs=[HBM] * (2 * n), out_specs=(SEM, SEM, *[HBM] * n, pl.BlockSpec(memory_space=pltpu.VMEM)),
        input_output_aliases={n + a: 2 + a for a in range(n)},
        compiler_params=pltpu.CompilerParams(has_side_effects=_EFFECT),
    )(*[_in_hbm(s) for s in srcs], *[_in_hbm(l) for l in lands])
    return res[0], res[1], list(res[2:2 + n]), res[-1]


def ag_forward(send_sem, recv_sem, lands, after, name):
    n = len(lands)

    def body(*refs):
        send_sem, recv_sem = refs[0], refs[1]
        land = refs[2:2 + n]
        fsend, frecv = refs[3 + n], refs[4 + n]
        x, y, c = _coords()
        peers = _peers(x, y, c)
        for k in range(1, 4):
            blk = _dev_index(*peers[k])
            for a in range(n):
                pltpu.make_async_remote_copy(src_ref=land[a].at[blk], dst_ref=land[a].at[blk], send_sem=send_sem.at[k],
                                             recv_sem=recv_sem.at[k], device_id=peers[k], device_id_type=MESH).wait_recv()
        for k in range(1, 4):
            blk = _dev_index(*peers[k])
            for a in range(n):
                pltpu.make_async_remote_copy(src_ref=land[a].at[blk], dst_ref=land[a].at[blk], send_sem=fsend.at[k - 1],
                                             recv_sem=frecv.at[k - 1], device_id=peers[0], device_id_type=MESH).start()

    res = pl.pallas_call(
        body, name=name,
        out_shape=(pltpu.SemaphoreType.DMA((3,)), pltpu.SemaphoreType.DMA((3,)), *[_hbm_like(l) for l in lands]),
        in_specs=[SEM, SEM, *[HBM] * n, ANY], out_specs=(SEM, SEM, *[HBM] * n),
        input_output_aliases={2 + a: 2 + a for a in range(n)},
        compiler_params=pltpu.CompilerParams(has_side_effects=_EFFECT),
    )(send_sem, recv_sem, *lands, after)
    return res[0], res[1], list(res[2:])


def ag_finish(send_sem, recv_sem, fsend, frecv, srcs, lands, li, after, name):
    n = len(lands)

    def body(*refs):
        send_sem, recv_sem, fsend, frecv = refs[:4]
        src, land = refs[4:4 + n], refs[4 + n:4 + 2 * n]
        x, y, c = _coords()
        me = _dev_index(x, y, c)
        peers = _peers(x, y, c)
        for k in range(4):
            for a in range(n):
                pltpu.make_async_remote_copy(src_ref=src[a].at[li], dst_ref=land[a].at[me], send_sem=send_sem.at[k],
                                             recv_sem=recv_sem.at[k], device_id=peers[k], device_id_type=MESH).wait_send()
        sib = _dev_index(*peers[0])
        for a in range(n):
            pltpu.make_async_remote_copy(src_ref=land[a].at[sib], dst_ref=land[a].at[sib], send_sem=send_sem.at[0],
                                         recv_sem=recv_sem.at[0], device_id=peers[0], device_id_type=MESH).wait_recv()
        for k in range(1, 4):
            mine = _dev_index(*peers[k])
            theirs = _dev_index(peers[k][0], peers[k][1], 1 - c)
            for a in range(n):
                pltpu.make_async_remote_copy(src_ref=land[a].at[mine], dst_ref=land[a].at[theirs], send_sem=fsend.at[k - 1],
                                             recv_sem=frecv.at[k - 1], device_id=peers[0], device_id_type=MESH).wait()

    res = pl.pallas_call(
        body, name=name, out_shape=tuple(_hbm_like(l) for l in lands),
        in_specs=[SEM] * 4 + [HBM] * (2 * n) + [ANY], out_specs=tuple([HBM] * n),
        input_output_aliases={4 + n + a: a for a in range(n)},
        compiler_params=pltpu.CompilerParams(has_side_effects=_EFFECT),
    )(send_sem, recv_sem, fsend, frecv, *srcs, *lands, after)
    return list(res)


def rs_start(psums, name):
    n = len(psums)
    lands = [lax.empty(p.shape, p.dtype) for p in psums]

    def body(*refs):
        src, land = refs[:n], refs[n:2 * n]
        send_sem, recv_sem, token = refs[2 * n], refs[2 * n + 1], refs[-1]
        peers = _peers(*_coords())
        for k in range(3):
            for a in range(n):
                pltpu.make_async_remote_copy(src_ref=src[a].at[k], dst_ref=land[a].at[k], send_sem=send_sem.at[k],
                                             recv_sem=recv_sem.at[k], device_id=peers[k + 1], device_id_type=MESH).start()
        token[...] = jnp.zeros_like(token)

    res = pl.pallas_call(
        body, name=name,
        out_shape=(pltpu.SemaphoreType.DMA((3,)), pltpu.SemaphoreType.DMA((3,)), *[_hbm_like(p) for p in psums],
                   *[_hbm_like(l) for l in lands], _TOKEN),
        in_specs=[HBM] * (2 * n), out_specs=(SEM, SEM, *[HBM] * (2 * n), pl.BlockSpec(memory_space=pltpu.VMEM)),
        input_output_aliases={a: 2 + a for a in range(2 * n)},
        compiler_params=pltpu.CompilerParams(has_side_effects=_EFFECT),
    )(*[_in_hbm(p) for p in psums], *[_in_hbm(l) for l in lands])
    return res[0], res[1], list(res[2:2 + n]), list(res[2 + n:2 + 2 * n]), res[-1]


def rs_finish(send_sem, recv_sem, psums, lands, after, name):
    n = len(psums)

    def body(*refs):
        send_sem, recv_sem = refs[0], refs[1]
        src, land = refs[2:2 + n], refs[2 + n:2 + 2 * n]
        peers = _peers(*_coords())
        for k in range(3):
            for a in range(n):
                pltpu.make_async_remote_copy(src_ref=src[a].at[k], dst_ref=land[a].at[k], send_sem=send_sem.at[k],
                                             recv_sem=recv_sem.at[k], device_id=peers[k + 1], device_id_type=MESH).wait()

    res = pl.pallas_call(
        body, name=name, out_shape=tuple(_hbm_like(l) for l in lands),
        in_specs=[SEM, SEM] + [HBM] * (2 * n) + [ANY], out_specs=tuple([HBM] * n),
        input_output_aliases={2 + n + a: a for a in range(n)},
        compiler_params=pltpu.CompilerParams(has_side_effects=_EFFECT),
    )(send_sem, recv_sem, *psums, *lands, after)
    return list(res)


def _row_tile(r):
    return r if r <= 512 else 512


def pair_add(blk_idx, g, recv, name):
    _, r, c = g.shape
    tr = _row_tile(r)

    def body(idx_ref, g_ref, r_ref, own_ref, oth_ref):
        k = pl.program_id(1)
        sm = g_ref[...].astype(F32) + r_ref[...].astype(F32)

        @pl.when(k == 0)
        def _():
            own_ref[...] = sm

        @pl.when(k > 0)
        def _():
            oth_ref[...] = sm.astype(BF16)

    grid_spec = pltpu.PrefetchScalarGridSpec(
        num_scalar_prefetch=1, grid=(r // tr, 4),
        in_specs=[pl.BlockSpec((None, tr, c), lambda t, k, idx: (idx[k], t, 0)),
                  pl.BlockSpec((None, tr, c), lambda t, k, idx: (k, t, 0))],
        out_specs=[pl.BlockSpec((tr, c), lambda t, k, idx: (t, 0)),
                   pl.BlockSpec((None, tr, c), lambda t, k, idx: (jnp.maximum(k - 1, 0), t, 0))])
    return pl.pallas_call(
        body, name=name, grid_spec=grid_spec,
        out_shape=[jax.ShapeDtypeStruct((r, c), F32), jax.ShapeDtypeStruct((3, r, c), BF16)],
        compiler_params=_cparams(2))(blk_idx, g, recv)


def _adamw(w, g, m, v):
    m = ADAM_B1 * m + (1.0 - ADAM_B1) * g
    v = ADAM_B2 * v + (1.0 - ADAM_B2) * (g * g)
    m_hat = m / (1.0 - ADAM_B1 ** ADAM_STEP)
    v_hat = v / (1.0 - ADAM_B2 ** ADAM_STEP)
    delta = -ADAM_LR * (m_hat / (jnp.sqrt(v_hat) + ADAM_EPS) + ADAM_WD * w)
    return delta, m, v


def adamw_shard(own, recv, w, m, v, li, prev, name):
    r, c = own.shape
    tr = _row_tile(r)

    def body(own_ref, recv_ref, w_ref, m_ref, v_ref, p0, p1, p2, p3, g_ref, d_ref, nm_ref, nv_ref):
        g = own_ref[...] + recv_ref[0].astype(F32) + recv_ref[1].astype(F32) + recv_ref[2].astype(F32)
        delta, nm, nv = _adamw(w_ref[...], g, m_ref[...], v_ref[...])
        g_ref[...] = g
        d_ref[...] = delta
        nm_ref[...] = nm
        nv_ref[...] = nv

    lay = pl.BlockSpec((None, tr, c), lambda t: (li, t, 0))
    stack = jax.ShapeDtypeStruct((DEPTH, r, c), F32)
    return pl.pallas_call(
        body, name=name, grid=(r // tr,),
        in_specs=[pl.BlockSpec((tr, c), lambda t: (t, 0)), pl.BlockSpec((3, tr, c), lambda t: (0, t, 0)),
                  lay, lay, lay, ANY, ANY, ANY, ANY],
        out_specs=[lay] * 4, out_shape=[stack] * 4,
        input_output_aliases={5: 0, 6: 1, 7: 2, 8: 3},
        compiler_params=_cparams(1))(own, recv, w, m, v, *prev)


def adamw_replicated(gathered, w, m, v, name):
    _, r, c = gathered.shape
    tr = 88

    def body(gs_ref, w_ref, m_ref, v_ref, g_ref, d_ref, nm_ref, nv_ref):
        g = gs_ref[0]
        for d in range(1, N_DEV):
            g = g + gs_ref[d]
        delta, nm, nv = _adamw(w_ref[...], g, m_ref[...], v_ref[...])
        g_ref[...] = g
        d_ref[...] = delta
        nm_ref[...] = nm
        nv_ref[...] = nv

    row = pl.BlockSpec((tr, c), lambda t: (t, 0))
    return pl.pallas_call(
        body, name=name, grid=(r // tr,),
        in_specs=[pl.BlockSpec((N_DEV, tr, c), lambda t: (0, t, 0)), row, row, row],
        out_specs=[row] * 4, out_shape=[jax.ShapeDtypeStruct((r, c), F32)] * 4,
        compiler_params=_cparams(1))(gathered, w, m, v)


def adamw_plain(g, w, m, v, name):
    def body(g_ref, w_ref, m_ref, v_ref, d_ref, nm_ref, nv_ref):
        delta, nm, nv = _adamw(w_ref[...], g_ref[...], m_ref[...], v_ref[...])
        d_ref[...] = delta
        nm_ref[...] = nm
        nv_ref[...] = nv

    return pl.pallas_call(body, name=name, out_shape=[jax.ShapeDtypeStruct(w.shape, F32)] * 3)(g, w, m, v)


_PACK = (("ln_in_g", (D,)), ("ln_in_b", (D,)), ("b_in", (DEPTH, N_PROJ)), ("rpb", (DEPTH, N_HEADS, 2 * KH - 1, 2 * KW - 1)),
         ("pool_w", (DEPTH, 4, PGD, PGD)), ("pool_scale", (DEPTH, D_POOL)), ("ln1_g", (DEPTH, D)), ("ln1_b", (DEPTH, D)),
         ("conv_b", (DEPTH, D_FF)), ("ln2_g", (DEPTH, D)), ("ln2_b", (DEPTH, D)), ("conv_w", (DEPTH, 3, D_FF)))
_PACK_LANES = 1024


def _pack_rows(shape):
    return -(-int(np.prod(shape)) // _PACK_LANES)


_PACK_ROWS = -(-sum(_pack_rows(s) for _, s in _PACK) // 88) * 88


def _pack(parts):
    rows = []
    for name, shape in _PACK:
        flat = parts[name].reshape(-1).astype(F32)
        rows.append(jnp.pad(flat, (0, _pack_rows(shape) * _PACK_LANES - flat.shape[0])))
    used = sum(_pack_rows(s) for _, s in _PACK)
    rows.append(jnp.zeros(((_PACK_ROWS - used) * _PACK_LANES,), F32))
    return jnp.concatenate(rows).reshape(_PACK_ROWS, _PACK_LANES)


def _unpack(packed):
    out, r0 = {}, 0
    for name, shape in _PACK:
        n = int(np.prod(shape))
        nr = _pack_rows(shape)
        out[name] = packed[r0:r0 + nr].reshape(-1)[:n].reshape(shape)
        r0 += nr
    return out


def _bias_tables(rpb_l):
    qc = np.arange(GRID_W)[:, None]
    kc = np.arange(GRID_W)[None, :]
    start = np.clip(qc - KW // 2, 0, GRID_W - KW)
    valid = (kc >= start) & (kc < start + KW)
    col = np.clip(kc - qc, -(KW - 1), KW - 1) + KW - 1
    onehot = (col.reshape(-1)[None, :] == np.arange(2 * KW - 1)[:, None]).astype(np.float32)
    rows = jnp.pad(rpb_l, ((0, 0), (0, 1), (0, 0)))
    tab = jnp.einsum("hij,jm->him", rows, jnp.asarray(onehot), precision=lax.Precision.HIGHEST)
    tab = tab.reshape(N_HEADS, KROWS, GRID_W, GRID_W).transpose(0, 2, 1, 3)
    ok = valid[None, :, None, :] & (np.arange(KROWS) < 2 * KH - 1)[None, None, :, None]
    tab = jnp.where(jnp.asarray(ok), tab, NEG_INF).reshape(N_HEADS, GRID_W, KB)
    return tab, tab[:, ::-1, :]


_SHARDED = ("w_in", "w_attn_out", "w_pool_out", "w_mix_out", "w_up", "w_down", "w_ple_gate", "w_ple_proj")
_NAMES = ("ln_in_g", "ln_in_b", "w_in", "b_in", "rpb", "w_attn_out", "pool_w", "pool_scale", "w_pool_out", "w_mix_out",
          "ln1_g", "ln1_b", "w_up", "conv_w", "conv_b", "w_down", "w_ple_gate", "w_ple_proj", "ln2_g", "ln2_b")


def kernel(x, p, ln_in_g, ln_in_b, w_in, b_in, rpb, w_attn_out, pool_w, pool_scale, w_pool_out, w_mix_out, ln1_g, ln1_b, w_up, conv_w, conv_b, w_down, w_ple_gate, w_ple_proj, ln2_g, ln2_b, loss_target, m_ln_in_g, m_ln_in_b, m_w_in, m_b_in, m_rpb, m_w_attn_out, m_pool_w, m_pool_scale, m_w_pool_out, m_w_mix_out, m_ln1_g, m_ln1_b, m_w_up, m_conv_w, m_conv_b, m_w_down, m_w_ple_gate, m_w_ple_proj, m_ln2_g, m_ln2_b, v_ln_in_g, v_ln_in_b, v_w_in, v_b_in, v_rpb, v_w_attn_out, v_pool_w, v_pool_scale, v_w_pool_out, v_w_mix_out, v_ln1_g, v_ln1_b, v_w_up, v_conv_w, v_conv_b, v_w_down, v_w_ple_gate, v_w_ple_proj, v_ln2_g, v_ln2_b):
    a = dict(locals())
    W = {n: a[n] for n in _NAMES}
    M = {n: a["m_" + n] for n in _NAMES}
    V = {n: a["v_" + n] for n in _NAMES}
    xi, yi, ci = _coords()
    me = _dev_index(xi, yi, ci)
    x2, tgt = x[0], loss_target[0]
    pb = p[:, 0].astype(BF16)

    ex = _Exchange(W, M, V)
    loss_part, dx, parts = _local_step(x2, tgt, pb, W, ex)
    loss = lax.psum(loss_part[0, 0], AXES)
    stacks = ex.stacks

    (gath,) = all_gather([_pack(parts)], "ag_small_grads")
    zero_cw = jnp.zeros((DEPTH, 3, D_FF), F32)
    packs = [_pack({**{n: src[n] for n, _ in _PACK if n != "conv_w"}, "conv_w": zero_cw}) for src in (W, M, V)]
    outs = [_unpack(o) for o in adamw_replicated(gath, *packs, "adamw_replicated")]
    g_cw = lax.dynamic_slice_in_dim(outs[0]["conv_w"], me * FF_SHARD, FF_SHARD, axis=2)
    flat = lambda t: t.reshape(DEPTH * 3, FF_SHARD)
    cw_out = [o.reshape(DEPTH, 3, FF_SHARD) for o in
              adamw_plain(flat(g_cw), flat(conv_w), flat(m_conv_w), flat(v_conv_w), "adamw_conv_w")]
    res = []
    for k in range(4):
        d = {n: stacks[n][k] for n in _SHARDED}
        d.update({n: outs[k][n] for n, _ in _PACK if n != "conv_w"})
        d["conv_w"] = g_cw if k == 0 else cw_out[k - 1]
        res.append(d)
    return (loss, dx[None], *[res[k][n] for k in range(4) for n in _NAMES])


class _Exchange:
    def __init__(self, W, M, V):
        self.W, self.M, self.V = W, M, V
        self.srcs = [W[n].astype(BF16) for n in _SHARDED] + [W["conv_w"]]
        self.ag, self.rs = [], {}
        for li in range(DEPTH):
            lands = ag_own(self.srcs, li, f"ag_own{li}")
            self.ag.append(ag_start(self.srcs, lands, li, f"ag_start{li}"))
        self.fwd = {}
        self.stacks = {n: [lax.empty((DEPTH,) + W[n].shape[1:], F32) for _ in range(4)] for n in _SHARDED}
        xi, yi, ci = _coords()
        self.rel_idx = jnp.stack([_dev_index(xi ^ (k & 1), yi ^ (k >> 1), ci) for k in range(4)]).astype(I32)

    def tokens(self):
        return [a[3] for a in self.ag]

    def prefetch(self, li, after):
        send, recv, lands, _ = self.ag[li]
        self.fwd[li] = ag_forward(send, recv, lands, after, f"ag_forward{li}")

    def weights(self, li, after):
        send, recv, _, _ = self.ag[li]
        fsend, frecv, lands = self.fwd[li]
        lands = ag_finish(send, recv, fsend, frecv, self.srcs, lands, li, after, f"ag_finish{li}")
        cw = lands[-1].transpose(1, 0, 2).reshape(3, 4, FF_BLK).transpose(1, 0, 2)
        return dict(zip(_SHARDED, lands[:-1])), cw

    def grads(self, li, gw):
        glist = [gw[n] for n in _SHARDED]
        recv1 = exchange_sibling(glist, f"rs_d2d{li}")
        sums = [pair_add(self.rel_idx, g, r1, f"pair_add_{n}{li}") for n, g, r1 in zip(_SHARDED, glist, recv1)]
        send, recv, psums, lands, token = rs_start([s_[1] for s_ in sums], f"rs_start{li}")
        self.rs[li] = (send, recv, psums, lands, [s_[0] for s_ in sums])
        return token

    def update(self, li, after):
        send, recv, psums, lands, owns = self.rs.pop(li)
        recv2 = rs_finish(send, recv, psums, lands, after, f"rs_finish{li}")
        for n, own, r2 in zip(_SHARDED, owns, recv2):
            self.stacks[n] = adamw_shard(own, r2, self.W[n], self.M[n], self.V[n], li, self.stacks[n], f"adamw_{n}{li}")


def _local_step(x2, tgt, pb, W, ex):
    depth = W["rpb"].shape[0]
    vec = lambda t: t.reshape(1, -1)
    ln1_g, ln1_b, ln2_g, ln2_b = W["ln1_g"], W["ln1_b"], W["ln2_g"], W["ln2_b"]
    b_in, rpb, pool_scale = W["b_in"], W["rpb"], W["pool_scale"]
    cb_full = W["conv_b"].reshape(depth, 4, 1, FF_BLK)
    pool_w_b = W["pool_w"].astype(BF16)

    h, hb = ln_fwd(x2, vec(W["ln_in_g"]), vec(W["ln_in_b"]), "ln_in", after=ex.tokens())
    ex.prefetch(0, hb)
    saved = []
    for li in range(depth):
        G, cw = ex.weights(li, hb)
        e_tab, e_rev = _bias_tables(rpb[li])
        bias = vec(b_in[li])
        proj = proj_fwd(hb, G["w_in"], bias, li, 0, N_DEV, BF16, f"proj{li}")
        u = proj_fwd(hb, G["w_in"], bias, li, 3, 1, F32, f"proj_u{li}")
        att = attn_fwd(proj, e_tab, f"attn{li}")
        pm, pw = pool_fwd(u, pool_w_b[li], vec(pool_scale[li]), f"pool{li}")
        mg, ya, yp = merge_fwd(att, pw, G["w_attn_out"], G["w_pool_out"], proj, li, f"merge{li}")
        if li + 1 < depth:
            ex.prefetch(li + 1, mg)
        z1, h1, h1b = mix_ln_fwd(mg, G["w_mix_out"], h, vec(ln1_g[li]), vec(ln1_b[li]), li, f"mix_ln{li}")
        up = up_fwd(h1b, G["w_up"], li, f"up{li}")
        t = ffn_act_fwd(up, cw, cb_full[li], f"ffn_act{li}")
        z2, h2, h2b, pg, pp = down_ple_ln_fwd(t, G["w_down"], h1b, G["w_ple_gate"], pb[li], G["w_ple_proj"], h1,
                                              vec(ln2_g[li]), vec(ln2_b[li]), li, f"down_ln{li}")
        saved.append(dict(hb=hb, proj=proj, att=att, pm=pm, pw=pw, mg=mg, ya=ya, yp=yp, z1=z1, h1b=h1b, up=up, t=t,
                          z2=z2, pg=pg, pp=pp, e_rev=e_rev, G=G, cw=cw))
        h, hb = h2, h2b

    dh, loss_part = loss_bwd(h, tgt, "loss")
    small = {n: [None] * depth for n in ("b_in", "rpb", "pool_w", "pool_scale", "ln1_g", "ln1_b", "conv_b", "ln2_g",
                                         "ln2_b", "conv_w")}
    token = ()
    for li in reversed(range(depth)):
        sv = saved[li]
        G, cw = sv["G"], sv["cw"]
        dz2, dz2b, dpg, dpp, dg2, db2 = ln2_ple_bwd(dh, sv["z2"], vec(ln2_g[li]), sv["pg"], sv["pp"], f"ln2_bwd{li}",
                                                    after=token)
        gw = {}
        gw["w_ple_gate"] = wgrad_rows(sv["h1b"], dpg, f"dw_pg{li}")
        gw["w_ple_proj"] = wgrad_cols(pb[li], dpp, f"dw_pp{li}")
        gw["w_down"] = wgrad_down(sv["t"], dz2b, f"dw_down{li}").reshape(N_DEV, FF_SHARD, D)
        dhv, dhg, dcw, dcb = ffn_act_bwd(dz2b, G["w_down"], sv["up"], cw, cb_full[li], li, f"ffn_bwd{li}")
        gw["w_up"] = wgrad_up(sv["h1b"], dhv, dhg, f"dw_up{li}")
        dz1, dz1b, dg1, db1 = dh1_ln1_bwd(dz2, dpg, G["w_ple_gate"], dhv, dhg, G["w_up"], sv["z1"], vec(ln1_g[li]), li,
                                          f"ln1_bwd{li}")
        gw["w_mix_out"] = wgrad_rows(sv["mg"], dz1b, f"dw_mix{li}")
        dya, dyp, dga, dgb = merge_bwd(dz1b, G["w_mix_out"], sv["proj"], sv["ya"], sv["yp"], li, f"merge_bwd{li}")
        gw["w_attn_out"] = wgrad_cols(sv["att"], dya, f"dw_ao{li}")
        gw["w_pool_out"] = wgrad_cols(sv["pw"], dyp, f"dw_po{li}")
        da = attn_out_bwd(dya, G["w_attn_out"], li, f"da{li}")
        du, dpool_w, dpool_sc = pool_bwd(dyp, G["w_pool_out"], sv["pm"], pool_w_b[li], vec(pool_scale[li]), li,
                                         f"pool_bwd{li}")
        dq, dk, dv, drpb = attn_bwd(sv["proj"], da, sv["e_rev"], f"attn_bwd{li}")
        dproj = jnp.concatenate([dq, dk, dv, du, dga, dgb], axis=1)
        gw["w_in"], db_in = wgrad_cols(sv["hb"], dproj, f"dw_in{li}", with_colsum=True)
        dh = dh0_bwd(dz1, dproj, G["w_in"], li, f"dh0{li}")
        small["b_in"][li] = db_in.reshape(N_PROJ)
        small["rpb"][li] = drpb.reshape(N_HEADS, KROWS, GRID_W)[:, :2 * KH - 1, :2 * KW - 1]
        small["pool_w"][li] = dpool_w
        small["pool_scale"][li] = dpool_sc.reshape(D_POOL)
        small["ln1_g"][li], small["ln1_b"][li] = dg1.reshape(D), db1.reshape(D)
        small["ln2_g"][li], small["ln2_b"][li] = dg2.reshape(D), db2.reshape(D)
        small["conv_b"][li] = dcb.reshape(D_FF)
        small["conv_w"][li] = dcw.transpose(1, 0, 2).reshape(3, D_FF)
        token = (ex.grads(li, gw),)
        if li + 1 < depth:
            ex.update(li + 1, dh)
    dx, dg_in, db_in0 = ln_bwd(dh, x2, vec(W["ln_in_g"]), "ln_in_bwd", after=token)
    ex.update(0, dx)
    parts = {n: jnp.stack(v_) for n, v_ in small.items()}
    parts["ln_in_g"], parts["ln_in_b"] = dg_in.reshape(D), db_in0.reshape(D)
    return loss_part, dx, parts
```

```python
import numpy as np
import jax
import jax.numpy as jnp
from jax import lax
from jax.experimental import pallas as pl
from jax.experimental.pallas import tpu as pltpu

F32 = jnp.float32
BF16 = jnp.bfloat16
I32 = jnp.int32

D = 1024
DEPTH = 4
GRID_W = 64
N_HEADS = 8
HEAD_DIM = 64
D_ATTN = 512
KH = 8
KW = 16
POOL_WINDOWS = (2, 4, 8, 16)
D_POOL = 512
PGD = 128
D_FF = 2816
PLE_DIM = 256
N_PROJ = 4096
ALPHA = (2 * DEPTH) ** 0.25
LN_EPS = 1e-5
NEG_INF = -1e30
ATT_SCALE = HEAD_DIM ** -0.5
ADAM_LR = 0.001
ADAM_B1 = 0.9
ADAM_B2 = 0.999
ADAM_EPS = 1e-08
ADAM_WD = 0.01
ADAM_STEP = 10

N_DEV = 8
AXES = ("x", "y", "c")
FF_BLK = D_FF // 4
FF_SHARD = D_FF // N_DEV
QROWS = 8
KROWS = 16
QB = QROWS * GRID_W
KB = KROWS * GRID_W
V7X_VMEM_LIMIT = 56 * 2 ** 20
MESH = pl.DeviceIdType.MESH
ANY = pl.BlockSpec(memory_space=pl.ANY)


def _cparams(n_grid):
    return pltpu.CompilerParams(dimension_semantics=("arbitrary",) * n_grid, vmem_limit_bytes=V7X_VMEM_LIMIT)


def _nn(a, b):
    return lax.dot_general(a, b, (((1,), (0,)), ((), ())), preferred_element_type=F32)


def _nt(a, b):
    return lax.dot_general(a, b, (((1,), (1,)), ((), ())), preferred_element_type=F32)


def _tn(a, b):
    return lax.dot_general(a, b, (((0,), (0,)), ((), ())), preferred_element_type=F32)


def _sigmoid(x):
    return 1.0 / (1.0 + jnp.exp(-x))


def _ln_fwd(z, g, b):
    mu = jnp.mean(z, axis=-1, keepdims=True)
    xc = z - mu
    var = jnp.mean(xc * xc, axis=-1, keepdims=True)
    return xc * lax.rsqrt(var + LN_EPS) * g + b


def _ln_bwd(dh, z, g):
    mu = jnp.mean(z, axis=-1, keepdims=True)
    xc = z - mu
    var = jnp.mean(xc * xc, axis=-1, keepdims=True)
    rstd = lax.rsqrt(var + LN_EPS)
    xhat = xc * rstd
    dxh = dh * g
    m1 = jnp.mean(dxh, axis=-1, keepdims=True)
    m2 = jnp.mean(dxh * xhat, axis=-1, keepdims=True)
    return rstd * (dxh - m1 - xhat * m2), dh * xhat


def _colsum(x):
    return jnp.sum(x, axis=0, keepdims=True)


def _lane_cat(ref):
    return jnp.concatenate([ref[j] for j in range(ref.shape[0])], axis=1)


def _row_cat(ref):
    n, r, c = ref.shape
    return ref[...].reshape(n * r, c)


def _shards(n, r, c, li, j_of=None):
    del li
    if j_of is None:
        return pl.BlockSpec((n, r, c), lambda *_: (0, 0, 0))
    return pl.BlockSpec((n, r, c), lambda *g: (j_of(*g), 0, 0))


def _shard(r, c, li, j_of):
    del li
    return pl.BlockSpec((None, r, c), lambda *g: (j_of(*g), 0, 0))


def ln_fwd(x, g, b, name, after=()):
    s = x.shape[0]
    tm = 512
    na = len(after)

    def body(x_ref, g_ref, b_ref, *rest):
        h_ref, hb_ref = rest[na:]
        h = _ln_fwd(x_ref[...], g_ref[...], b_ref[...])
        h_ref[...] = h
        hb_ref[...] = h.astype(BF16)

    row = pl.BlockSpec((tm, D), lambda i: (i, 0))
    vec = pl.BlockSpec((1, D), lambda i: (0, 0))
    return pl.pallas_call(
        body, name=name, grid=(s // tm,), in_specs=[row, vec, vec] + [ANY] * na, out_specs=[row, row],
        out_shape=[jax.ShapeDtypeStruct((s, D), F32), jax.ShapeDtypeStruct((s, D), BF16)],
        compiler_params=_cparams(1))(x, g, b, *after)


def proj_fwd(hb, win, bias, li, j0, nj, out_dtype, name):
    s = hb.shape[0]
    bn = N_PROJ // N_DEV
    tm = 1024

    def body(a_ref, w_ref, b_ref, o_ref):
        o_ref[...] = (_nn(a_ref[...], w_ref[...]) + b_ref[...]).astype(out_dtype)

    return pl.pallas_call(
        body, name=name, grid=(s // tm, nj),
        in_specs=[pl.BlockSpec((tm, D), lambda i, j: (i, 0)),
                  _shard(D, bn, li, lambda i, j: j0 + j),
                  pl.BlockSpec((1, bn), lambda i, j: (0, j0 + j))],
        out_specs=pl.BlockSpec((tm, bn), lambda i, j: (i, j)),
        out_shape=jax.ShapeDtypeStruct((s, nj * bn), out_dtype),
        compiler_params=_cparams(2))(hb, win, bias)


def _attn_block_scalars(b, n_rows):
    ks = jnp.clip(QROWS * b - KH // 2, 0, n_rows - KROWS)
    delta = ks - QROWS * b
    k0 = pl.multiple_of(ks * GRID_W, 256)
    return ks, delta, k0


def _attn_band_logits(sb, e, b, qr, ks, delta, n_rows, klane):
    shift = ((qr - delta - (KH - 1)) * GRID_W) % KB
    bias = pltpu.roll(e, shift, 1)
    rs = jnp.clip(QROWS * b + qr - KH // 2, 0, n_rows - KH)
    lo = (rs - ks) * GRID_W
    ok = (klane >= lo) & (klane < lo + KH * GRID_W)
    return jnp.where(ok, sb + bias, NEG_INF)


def attn_fwd(proj, e_tab, name):
    s = proj.shape[0]
    n_rows = s // GRID_W
    nb = n_rows // QROWS

    def body(q_ref, k_ref, v_ref, e_ref, o_ref, s_ref, p_ref):
        b = pl.program_id(1)
        ks, delta, k0 = _attn_block_scalars(b, n_rows)
        kwin = k_ref[pl.ds(k0, KB), :]
        vwin = v_ref[pl.ds(k0, KB), :]
        q = q_ref[...] * ATT_SCALE
        lane = lax.broadcasted_iota(I32, (1, 128), 1)
        klane = lax.broadcasted_iota(I32, (1, KB), 1)
        acc = jnp.zeros((QB, 128), F32)
        for hh in range(2):
            lm = (lane // HEAD_DIM) == hh
            qh = jnp.where(lm, q, jnp.zeros_like(q))
            vh = jnp.where(lm, vwin, jnp.zeros_like(vwin))
            s_ref[...] = _nt(qh, kwin)
            e = e_ref[hh]

            def band(qr, carry):
                r0 = pl.multiple_of(qr * GRID_W, GRID_W)
                sb = _attn_band_logits(s_ref[pl.ds(r0, GRID_W), :], e, b, qr, ks, delta, n_rows, klane)
                m = jnp.max(sb, axis=1, keepdims=True)
                p = jnp.exp(sb - m)
                l = jnp.sum(p, axis=1, keepdims=True)
                p_ref[pl.ds(r0, GRID_W), :] = (p * (1.0 / l)).astype(BF16)
                return carry

            lax.fori_loop(0, QROWS, band, 0, unroll=True)
            acc = acc + _nn(p_ref[...], vh)
        o_ref[...] = acc.astype(BF16)

    return pl.pallas_call(
        body, name=name, grid=(4, nb),
        in_specs=[pl.BlockSpec((QB, 128), lambda j, b: (b, j)),
                  pl.BlockSpec((s, 128), lambda j, b: (0, 4 + j)),
                  pl.BlockSpec((s, 128), lambda j, b: (0, 8 + j)),
                  pl.BlockSpec((2, GRID_W, KB), lambda j, b: (j, 0, 0))],
        out_specs=pl.BlockSpec((QB, 128), lambda j, b: (b, j)),
        out_shape=jax.ShapeDtypeStruct((s, D_ATTN), BF16),
        scratch_shapes=[pltpu.VMEM((QB, KB), F32), pltpu.VMEM((QB, KB), BF16)],
        compiler_params=_cparams(2))(proj, proj, proj, e_tab)


_POOL_PAD = 8


def _pool_counts(s, w):
    t = lax.broadcasted_iota(I32, (s, 1), 0)
    return (jnp.minimum(t + w // 2, s) - jnp.maximum(t - w // 2, 0)).astype(F32)


def _window_sum(x, w, back_first):
    s = x.shape[0]
    z = jnp.zeros((_POOL_PAD, x.shape[1]), F32)
    xe = jnp.concatenate([z, x, z], axis=0)
    n = s + 2 * _POOL_PAD
    acc = xe + pltpu.roll(xe, 1 if back_first else n - 1, 0)
    k = 1
    while 2 * k < w:
        acc = pltpu.roll(acc, k, 0) + pltpu.roll(acc, n - k, 0)
        k *= 2
    return acc[_POOL_PAD:_POOL_PAD + s, :]


def pool_fwd(u, pool_w, pool_scale, name):
    s = u.shape[0]

    def body(u_ref, w_ref, sc_ref, pm_ref, pw_ref):
        for g, w in enumerate(POOL_WINDOWS):
            cols = slice(g * PGD, (g + 1) * PGD)
            ug = u_ref[:, cols]
            pm = (_window_sum(ug, w, True) / _pool_counts(s, w) - ug).astype(BF16)
            pm_ref[:, cols] = pm
            pw_ref[:, cols] = (_nn(pm, w_ref[g]) * sc_ref[:, cols]).astype(BF16)

    full = lambda shape: pl.BlockSpec(shape, lambda i: (0,) * len(shape))
    return pl.pallas_call(
        body, name=name, grid=(1,),
        in_specs=[full((s, D_POOL)), full((4, PGD, PGD)), full((1, D_POOL))],
        out_specs=[full((s, D_POOL)), full((s, D_POOL))],
        out_shape=[jax.ShapeDtypeStruct((s, D_POOL), BF16)] * 2,
        compiler_params=_cparams(1))(u, pool_w, pool_scale)


def merge_fwd(a, pw, wao, wpo, proj, li, name):
    s = a.shape[0]
    tm, tn = 512, 512
    nt = D // tn
    per = tn // 128

    def body(a_ref, pw_ref, wa_ref, wp_ref, ga_ref, gb_ref, mg_ref, ya_ref, yp_ref):
        ya = _nn(a_ref[...], _lane_cat(wa_ref))
        yp = _nn(pw_ref[...], _lane_cat(wp_ref))
        mg = _sigmoid(ga_ref[...].astype(F32)) * ya + _sigmoid(gb_ref[...].astype(F32)) * yp
        mg_ref[...] = mg.astype(BF16)
        ya_ref[...] = ya.astype(BF16)
        yp_ref[...] = yp.astype(BF16)

    act = pl.BlockSpec((tm, D_ATTN), lambda i, j: (i, 0))
    wsp = _shards(per, D_ATTN, 128, li, lambda i, j: j)
    out = pl.BlockSpec((tm, tn), lambda i, j: (i, j))
    ga0 = (3 * D_ATTN + D_POOL) // tn
    return pl.pallas_call(
        body, name=name, grid=(s // tm, nt),
        in_specs=[act, act, wsp, wsp,
                  pl.BlockSpec((tm, tn), lambda i, j: (i, ga0 + j)),
                  pl.BlockSpec((tm, tn), lambda i, j: (i, ga0 + nt + j))],
        out_specs=[out, out, out],
        out_shape=[jax.ShapeDtypeStruct((s, D), BF16)] * 3,
        compiler_params=_cparams(2))(a, pw, wao, wpo, proj, proj)


def mix_ln_fwd(mg, wmix, h0, g, b, li, name):
    s = mg.shape[0]
    tm = 256

    def body(mg_ref, w_ref, h0_ref, g_ref, b_ref, z_ref, h_ref, hb_ref):
        z = ALPHA * h0_ref[...] + _nn(mg_ref[...], _row_cat(w_ref))
        h = _ln_fwd(z, g_ref[...], b_ref[...])
        z_ref[...] = z
        h_ref[...] = h
        hb_ref[...] = h.astype(BF16)

    row = pl.BlockSpec((tm, D), lambda i: (i, 0))
    vec = pl.BlockSpec((1, D), lambda i: (0, 0))
    return pl.pallas_call(
        body, name=name, grid=(s // tm,),
        in_specs=[row, _shards(N_DEV, D // N_DEV, D, li), row, vec, vec],
        out_specs=[row, row, row],
        out_shape=[jax.ShapeDtypeStruct((s, D), F32), jax.ShapeDtypeStruct((s, D), F32),
                   jax.ShapeDtypeStruct((s, D), BF16)],
        compiler_params=_cparams(1))(mg, wmix, h0, g, b)


def up_fwd(hb, wup, li, name):
    s = hb.shape[0]
    tm = 1024

    def body(a_ref, w_ref, o_ref):
        o_ref[...] = _nn(a_ref[...], w_ref[...]).astype(BF16)

    return pl.pallas_call(
        body, name=name, grid=(s // tm, N_DEV),
        in_specs=[pl.BlockSpec((tm, D), lambda i, j: (i, 0)), _shard(D, FF_BLK, li, lambda i, j: j)],
        out_specs=pl.BlockSpec((None, tm, FF_BLK), lambda i, j: (j, i, 0)),
        out_shape=jax.ShapeDtypeStruct((N_DEV, s, FF_BLK), BF16),
        compiler_params=_cparams(2))(hb, wup)


_SQRT_HALF = 0.7071067811865476
_INV_SQRT_2PI = 0.3989422804014327


def _shift_rows(x, prev_row, next_row):
    n = x.shape[0]
    r = lax.broadcasted_iota(I32, (n, 1), 0)
    back = jnp.where(r == 0, prev_row, pltpu.roll(x, 1, 0))
    fwd = jnp.where(r == n - 1, next_row, pltpu.roll(x, n - 1, 0))
    return back, fwd


HALO = 16


def _halo_maps(tm, s):
    th = tm // HALO
    return (lambda i: jnp.maximum(i * th - 1, 0)), (lambda i: jnp.minimum((i + 1) * th, s // HALO - 1))


def _slab_specs(tm, s, blk_of):
    before, after = _halo_maps(tm, s)
    main = pl.BlockSpec((None, tm, FF_BLK), lambda c, i: (blk_of(c), i, 0))
    prev = pl.BlockSpec((None, HALO, FF_BLK), lambda c, i: (blk_of(c), before(i), 0))
    nxt = pl.BlockSpec((None, HALO, FF_BLK), lambda c, i: (blk_of(c), after(i), 0))
    return main, prev, nxt


def ffn_act_fwd(up, conv_w, conv_b, name):
    s = up.shape[1]
    tm = 512
    nt = s // tm
    hv_main, _, _ = _slab_specs(tm, s, lambda c: c)
    hg_main, hg_prev, hg_next = _slab_specs(tm, s, lambda c: 4 + c)

    def body(hv_ref, hg_ref, hp_ref, hn_ref, cw_ref, cb_ref, t_ref):
        i = pl.program_id(1)
        hg = hg_ref[...].astype(F32)
        prow = jnp.where(i == 0, 0.0, hp_ref[...].astype(F32)[HALO - 1:HALO, :])
        nrow = jnp.where(i == nt - 1, 0.0, hn_ref[...].astype(F32)[0:1, :])
        back, fwd = _shift_rows(hg, prow, nrow)
        c = back * cw_ref[0:1, :] + hg * cw_ref[1:2, :] + fwd * cw_ref[2:3, :] + cb_ref[...]
        act = 0.5 * c * (1.0 + lax.erf(c * _SQRT_HALF))
        t_ref[...] = (act * hv_ref[...].astype(F32)).astype(BF16)

    return pl.pallas_call(
        body, name=name, grid=(4, nt),
        in_specs=[hv_main, hg_main, hg_prev, hg_next,
                  pl.BlockSpec((None, 3, FF_BLK), lambda c, i: (c, 0, 0)),
                  pl.BlockSpec((None, 1, FF_BLK), lambda c, i: (c, 0, 0))],
        out_specs=pl.BlockSpec((None, tm, FF_BLK), lambda c, i: (c, i, 0)),
        out_shape=jax.ShapeDtypeStruct((4, s, FF_BLK), BF16),
        compiler_params=_cparams(2))(up, up, up, up, conv_w, conv_b)


def down_ple_ln_fwd(t, wdown, hb, wpg, pb, wpp, h1, g, b, li, name):
    s = hb.shape[0]
    tm = 256

    def body(t_ref, wd_ref, hb_ref, wpg_ref, p_ref, wpp_ref, h1_ref, g_ref, b_ref,
             z_ref, h_ref, hbo_ref, pg_ref, pp_ref):
        wd = _row_cat(wd_ref)
        ffn = _nn(t_ref[0], wd[0:FF_BLK, :])
        for c in range(1, 4):
            ffn = ffn + _nn(t_ref[c], wd[c * FF_BLK:(c + 1) * FF_BLK, :])
        pg = _nn(hb_ref[...], _row_cat(wpg_ref))
        pp = _nn(p_ref[...], _lane_cat(wpp_ref))
        z = ALPHA * h1_ref[...] + ffn + _sigmoid(pg) * pp
        h = _ln_fwd(z, g_ref[...], b_ref[...])
        z_ref[...] = z
        h_ref[...] = h
        hbo_ref[...] = h.astype(BF16)
        pg_ref[...] = pg.astype(BF16)
        pp_ref[...] = pp.astype(BF16)

    row = pl.BlockSpec((tm, D), lambda i: (i, 0))
    vec = pl.BlockSpec((1, D), lambda i: (0, 0))
    return pl.pallas_call(
        body, name=name, grid=(s // tm,),
        in_specs=[pl.BlockSpec((4, tm, FF_BLK), lambda i: (0, i, 0)),
                  _shards(N_DEV, FF_SHARD, D, li),
                  row, _shards(N_DEV, D // N_DEV, D, li),
                  pl.BlockSpec((tm, PLE_DIM), lambda i: (i, 0)),
                  _shards(N_DEV, PLE_DIM, 128, li),
                  row, vec, vec],
        out_specs=[row] * 5,
        out_shape=[jax.ShapeDtypeStruct((s, D), F32), jax.ShapeDtypeStruct((s, D), F32),
                   jax.ShapeDtypeStruct((s, D), BF16), jax.ShapeDtypeStruct((s, D), BF16),
                   jax.ShapeDtypeStruct((s, D), BF16)],
        compiler_params=_cparams(1))(t, wdown, hb, wpg, pb, wpp, h1, g, b)


def loss_bwd(h, target, name):
    s = h.shape[0]
    tm = 512

    def body(h_ref, t_ref, dh_ref, l_ref):
        @pl.when(pl.program_id(0) == 0)
        def _():
            l_ref[...] = jnp.zeros_like(l_ref)
        e = h_ref[...] - t_ref[...]
        dh_ref[...] = e * (1.0 / D)
        l_ref[...] += 0.5 * jnp.sum(jnp.mean(e * e, axis=-1, keepdims=True), axis=0, keepdims=True)

    row = pl.BlockSpec((tm, D), lambda i: (i, 0))
    return pl.pallas_call(
        body, name=name, grid=(s // tm,), in_specs=[row, row],
        out_specs=[row, pl.BlockSpec((1, 1), lambda i: (0, 0))],
        out_shape=[jax.ShapeDtypeStruct((s, D), F32), jax.ShapeDtypeStruct((1, 1), F32)],
        compiler_params=_cparams(1))(h, target)


def ln_bwd(dh, z, g, name, after=()):
    s = dh.shape[0]
    tm = 512
    na = len(after)

    def body(dh_ref, z_ref, g_ref, *rest):
        dz_ref, dg_ref, db_ref = rest[na:]

        @pl.when(pl.program_id(0) == 0)
        def _():
            dg_ref[...] = jnp.zeros_like(dg_ref)
            db_ref[...] = jnp.zeros_like(db_ref)
        dh = dh_ref[...]
        dz, dgx = _ln_bwd(dh, z_ref[...], g_ref[...])
        dz_ref[...] = dz
        dg_ref[...] += _colsum(dgx)
        db_ref[...] += _colsum(dh)

    row = pl.BlockSpec((tm, D), lambda i: (i, 0))
    vec = pl.BlockSpec((1, D), lambda i: (0, 0))
    return pl.pallas_call(
        body, name=name, grid=(s // tm,), in_specs=[row, row, vec] + [ANY] * na, out_specs=[row, vec, vec],
        out_shape=[jax.ShapeDtypeStruct((s, D), F32), jax.ShapeDtypeStruct((1, D), F32),
                   jax.ShapeDtypeStruct((1, D), F32)],
        compiler_params=_cparams(1))(dh, z, g, *after)


def ln2_ple_bwd(dh, z, g, pg, pp, name, after=()):
    s = dh.shape[0]
    tm = 512
    na = len(after)

    def body(dh_ref, z_ref, g_ref, pg_ref, pp_ref, *rest):
        dz_ref, dzb_ref, dpg_ref, dpp_ref, dg_ref, db_ref = rest[na:]

        @pl.when(pl.program_id(0) == 0)
        def _():
            dg_ref[...] = jnp.zeros_like(dg_ref)
            db_ref[...] = jnp.zeros_like(db_ref)
        dh = dh_ref[...]
        dz, dgx = _ln_bwd(dh, z_ref[...], g_ref[...])
        sg = _sigmoid(pg_ref[...].astype(F32))
        dz_ref[...] = dz
        dzb_ref[...] = dz.astype(BF16)
        dpg_ref[...] = (dz * pp_ref[...].astype(F32) * sg * (1.0 - sg)).astype(BF16)
        dpp_ref[...] = (dz * sg).astype(BF16)
        dg_ref[...] += _colsum(dgx)
        db_ref[...] += _colsum(dh)

    row = pl.BlockSpec((tm, D), lambda i: (i, 0))
    vec = pl.BlockSpec((1, D), lambda i: (0, 0))
    return pl.pallas_call(
        body, name=name, grid=(s // tm,), in_specs=[row, row, vec, row, row] + [ANY] * na,
        out_specs=[row, row, row, row, vec, vec],
        out_shape=[jax.ShapeDtypeStruct((s, D), F32)] + [jax.ShapeDtypeStruct((s, D), BF16)] * 3
        + [jax.ShapeDtypeStruct((1, D), F32)] * 2,
        compiler_params=_cparams(1))(dh, z, g, pg, pp, *after)


def wgrad_rows(a, dy, name):
    s, k = a.shape
    n = dy.shape[1]
    kb = k // N_DEV

    def body(a_ref, dy_ref, o_ref):
        o_ref[...] = _tn(a_ref[...], dy_ref[...]).astype(BF16)

    return pl.pallas_call(
        body, name=name, grid=(N_DEV,),
        in_specs=[pl.BlockSpec((s, kb), lambda j: (0, j)), pl.BlockSpec((s, n), lambda j: (0, 0))],
        out_specs=pl.BlockSpec((None, kb, n), lambda j: (j, 0, 0)),
        out_shape=jax.ShapeDtypeStruct((N_DEV, kb, n), BF16),
        compiler_params=_cparams(1))(a, dy)


def wgrad_cols(a, dy, name, with_colsum=False):
    s, k = a.shape
    n = dy.shape[1]
    nb = n // N_DEV

    def body(a_ref, dy_ref, o_ref, *cs_ref):
        dy = dy_ref[...]
        o_ref[...] = _tn(a_ref[...], dy).astype(BF16)
        if with_colsum:
            cs_ref[0][...] = _colsum(dy.astype(F32))

    out_specs = [pl.BlockSpec((None, k, nb), lambda j: (j, 0, 0))]
    out_shape = [jax.ShapeDtypeStruct((N_DEV, k, nb), BF16)]
    if with_colsum:
        out_specs.append(pl.BlockSpec((1, nb), lambda j: (0, j)))
        out_shape.append(jax.ShapeDtypeStruct((1, n), F32))
    res = pl.pallas_call(
        body, name=name, grid=(N_DEV,),
        in_specs=[pl.BlockSpec((s, k), lambda j: (0, 0)), pl.BlockSpec((s, nb), lambda j: (0, j))],
        out_specs=out_specs, out_shape=out_shape,
        compiler_params=_cparams(1))(a, dy)
    return res if with_colsum else res[0]


def wgrad_down(t, dy, name):
    _, s, k = t.shape
    n = dy.shape[1]

    def body(a_ref, dy_ref, o_ref):
        o_ref[...] = _tn(a_ref[...], dy_ref[...]).astype(BF16)

    return pl.pallas_call(
        body, name=name, grid=(4,),
        in_specs=[pl.BlockSpec((None, s, k), lambda j: (j, 0, 0)), pl.BlockSpec((s, n), lambda j: (0, 0))],
        out_specs=pl.BlockSpec((None, k, n), lambda j: (j, 0, 0)),
        out_shape=jax.ShapeDtypeStruct((4, k, n), BF16),
        compiler_params=_cparams(1))(t, dy)


def wgrad_up(a, dhv, dhg, name):
    s, k = a.shape

    def body(a_ref, dv_ref, dg_ref, o_ref):
        j = pl.program_id(0)

        @pl.when(j < 4)
        def _():
            o_ref[...] = _tn(a_ref[...], dv_ref[...]).astype(BF16)

        @pl.when(j >= 4)
        def _():
            o_ref[...] = _tn(a_ref[...], dg_ref[...]).astype(BF16)

    return pl.pallas_call(
        body, name=name, grid=(N_DEV,),
        in_specs=[pl.BlockSpec((s, k), lambda j: (0, 0)),
                  pl.BlockSpec((None, s, FF_BLK), lambda j: (jnp.minimum(j, 3), 0, 0)),
                  pl.BlockSpec((None, s, FF_BLK), lambda j: (jnp.maximum(j - 4, 0), 0, 0))],
        out_specs=pl.BlockSpec((None, k, FF_BLK), lambda j: (j, 0, 0)),
        out_shape=jax.ShapeDtypeStruct((N_DEV, k, FF_BLK), BF16),
        compiler_params=_cparams(1))(a, dhv, dhg)


def ffn_act_bwd(dzb, wdown, up, conv_w, conv_b, li, name):
    s = up.shape[1]
    tm = 512
    nt = s // tm
    before, after = _halo_maps(tm, s)
    hv_main, hv_prev, hv_next = _slab_specs(tm, s, lambda c: c)
    hg_main, hg_prev, hg_next = _slab_specs(tm, s, lambda c: 4 + c)

    def dc_of(dz, wd, hv, hg, back, fwd, cw_ref, cb_ref):
        dt = _nt(dz, wd)
        c = back * cw_ref[0:1, :] + hg * cw_ref[1:2, :] + fwd * cw_ref[2:3, :] + cb_ref[...]
        cdf = 0.5 * (1.0 + lax.erf(c * _SQRT_HALF))
        pdf = jnp.exp(-0.5 * c * c) * _INV_SQRT_2PI
        return dt, c * cdf, dt * hv * (cdf + c * pdf)

    def body(dz_ref, dzp_ref, dzn_ref, wd_ref, hv_ref, hvp_ref, hvn_ref, hg_ref, hgp_ref, hgn_ref, cw_ref, cb_ref,
             dhv_ref, dhg_ref, dcw_ref, dcb_ref):
        i = pl.program_id(1)

        @pl.when(i == 0)
        def _():
            dcw_ref[...] = jnp.zeros_like(dcw_ref)
            dcb_ref[...] = jnp.zeros_like(dcb_ref)

        wd = _row_cat(wd_ref)
        hg = hg_ref[...].astype(F32)
        hgp = hgp_ref[...].astype(F32)
        hgn = hgn_ref[...].astype(F32)
        first, last = i == 0, i == nt - 1
        e = HALO - 1
        back, fwd = _shift_rows(hg, jnp.where(first, 0.0, hgp[e:e + 1, :]), jnp.where(last, 0.0, hgn[0:1, :]))
        dt, act, dc = dc_of(dz_ref[...], wd, hv_ref[...].astype(F32), hg, back, fwd, cw_ref, cb_ref)
        dhv_ref[...] = (dt * act).astype(BF16)
        bp, fp = _shift_rows(hgp, hgp[0:1, :], hg[0:1, :])
        _, _, dcp = dc_of(dzp_ref[...], wd, hvp_ref[...].astype(F32), hgp, bp, fp, cw_ref, cb_ref)
        bn, fn = _shift_rows(hgn, hg[tm - 1:tm, :], hgn[e:e + 1, :])
        _, _, dcn = dc_of(dzn_ref[...], wd, hvn_ref[...].astype(F32), hgn, bn, fn, cw_ref, cb_ref)
        dc_back, dc_fwd = _shift_rows(dc, jnp.where(first, 0.0, dcp[e:e + 1, :]), jnp.where(last, 0.0, dcn[0:1, :]))
        dhg_ref[...] = (dc_fwd * cw_ref[0:1, :] + dc * cw_ref[1:2, :] + dc_back * cw_ref[2:3, :]).astype(BF16)
        dcw_ref[0:1, :] += _colsum(dc * back)
        dcw_ref[1:2, :] += _colsum(dc * hg)
        dcw_ref[2:3, :] += _colsum(dc * fwd)
        dcb_ref[...] += _colsum(dc)

    out_slab = pl.BlockSpec((None, tm, FF_BLK), lambda c, i: (c, i, 0))
    cw_spec = pl.BlockSpec((None, 3, FF_BLK), lambda c, i: (c, 0, 0))
    cb_spec = pl.BlockSpec((None, 1, FF_BLK), lambda c, i: (c, 0, 0))
    return pl.pallas_call(
        body, name=name, grid=(4, nt),
        in_specs=[pl.BlockSpec((tm, D), lambda c, i: (i, 0)),
                  pl.BlockSpec((HALO, D), lambda c, i: (before(i), 0)),
                  pl.BlockSpec((HALO, D), lambda c, i: (after(i), 0)),
                  _shards(2, FF_SHARD, D, li, lambda c, i: c),
                  hv_main, hv_prev, hv_next, hg_main, hg_prev, hg_next, cw_spec, cb_spec],
        out_specs=[out_slab, out_slab, cw_spec, cb_spec],
        out_shape=[jax.ShapeDtypeStruct((4, s, FF_BLK), BF16), jax.ShapeDtypeStruct((4, s, FF_BLK), BF16),
                   jax.ShapeDtypeStruct((4, 3, FF_BLK), F32), jax.ShapeDtypeStruct((4, 1, FF_BLK), F32)],
        compiler_params=_cparams(2))(dzb, dzb, dzb, wdown, up, up, up, up, up, up, conv_w, conv_b)


def dh1_ln1_bwd(dz2, dpg, wpg, dhv, dhg, wup, z1, g1, li, name, after=()):
    s = dz2.shape[0]
    tm = 256
    na = len(after)

    def body(dz2_ref, dpg_ref, wpg_ref, dhv_ref, dhg_ref, wup_ref, z1_ref, g_ref, *rest):
        dz_ref, dzb_ref, dg_ref, db_ref = rest[na:]

        @pl.when(pl.program_id(0) == 0)
        def _():
            dg_ref[...] = jnp.zeros_like(dg_ref)
            db_ref[...] = jnp.zeros_like(db_ref)
        dh = ALPHA * dz2_ref[...] + _nt(dpg_ref[...], _row_cat(wpg_ref))
        for c in range(4):
            dh = dh + _nt(dhv_ref[c], wup_ref[c]) + _nt(dhg_ref[c], wup_ref[4 + c])
        dz, dgx = _ln_bwd(dh, z1_ref[...], g_ref[...])
        dz_ref[...] = dz
        dzb_ref[...] = dz.astype(BF16)
        dg_ref[...] += _colsum(dgx)
        db_ref[...] += _colsum(dh)

    row = pl.BlockSpec((tm, D), lambda i: (i, 0))
    vec = pl.BlockSpec((1, D), lambda i: (0, 0))
    slab = pl.BlockSpec((4, tm, FF_BLK), lambda i: (0, i, 0))
    return pl.pallas_call(
        body, name=name, grid=(s // tm,),
        in_specs=[row, row, _shards(N_DEV, D // N_DEV, D, li), slab, slab, _shards(N_DEV, D, FF_BLK, li), row, vec]
        + [ANY] * na,
        out_specs=[row, row, vec, vec],
        out_shape=[jax.ShapeDtypeStruct((s, D), F32), jax.ShapeDtypeStruct((s, D), BF16),
                   jax.ShapeDtypeStruct((1, D), F32), jax.ShapeDtypeStruct((1, D), F32)],
        compiler_params=_cparams(1))(dz2, dpg, wpg, dhv, dhg, wup, z1, g1, *after)


def merge_bwd(dz1b, wmix, proj, ya, yp, li, name):
    s = dz1b.shape[0]
    tm, tn = 512, 512
    nt = D // tn
    per = tn // (D // N_DEV)
    ga0 = (3 * D_ATTN + D_POOL) // tn

    def body(dz_ref, w_ref, ga_ref, gb_ref, ya_ref, yp_ref, dya_ref, dyp_ref, dga_ref, dgb_ref):
        dm = _nt(dz_ref[...], _row_cat(w_ref))
        sa = _sigmoid(ga_ref[...].astype(F32))
        sb = _sigmoid(gb_ref[...].astype(F32))
        dya_ref[...] = (dm * sa).astype(BF16)
        dyp_ref[...] = (dm * sb).astype(BF16)
        dga_ref[...] = (dm * ya_ref[...].astype(F32) * sa * (1.0 - sa)).astype(BF16)
        dgb_ref[...] = (dm * yp_ref[...].astype(F32) * sb * (1.0 - sb)).astype(BF16)

    tile = pl.BlockSpec((tm, tn), lambda i, j: (i, j))
    return pl.pallas_call(
        body, name=name, grid=(s // tm, nt),
        in_specs=[pl.BlockSpec((tm, D), lambda i, j: (i, 0)),
                  _shards(per, D // N_DEV, D, li, lambda i, j: j),
                  pl.BlockSpec((tm, tn), lambda i, j: (i, ga0 + j)),
                  pl.BlockSpec((tm, tn), lambda i, j: (i, ga0 + nt + j)),
                  tile, tile],
        out_specs=[tile] * 4,
        out_shape=[jax.ShapeDtypeStruct((s, D), BF16)] * 4,
        compiler_params=_cparams(2))(dz1b, wmix, proj, proj, ya, yp)


def attn_out_bwd(dya, wao, li, name, after=()):
    s = dya.shape[0]
    tm = 512

    def body(d_ref, w_ref, *rest):
        rest[-1][...] = _nt(d_ref[...], _lane_cat(w_ref)).astype(BF16)

    return pl.pallas_call(
        body, name=name, grid=(s // tm,),
        in_specs=[pl.BlockSpec((tm, D), lambda i: (i, 0)), _shards(N_DEV, D_ATTN, 128, li)] + [ANY] * len(after),
        out_specs=pl.BlockSpec((tm, D_ATTN), lambda i: (i, 0)),
        out_shape=jax.ShapeDtypeStruct((s, D_ATTN), BF16),
        compiler_params=_cparams(1))(dya, wao, *after)


def pool_bwd(dyp, wpo, pm, pool_w, pool_scale, li, name):
    s = dyp.shape[0]

    def body(dyp_ref, wpo_ref, pm_ref, w_ref, sc_ref, du_ref, dw_ref, dsc_ref):
        wpo = _lane_cat(wpo_ref)
        dyp = dyp_ref[...]
        for g, w in enumerate(POOL_WINDOWS):
            cols = slice(g * PGD, (g + 1) * PGD)
            dpw = _nt(dyp, wpo[g * PGD:(g + 1) * PGD, :])
            pmg = pm_ref[:, cols]
            dsc_ref[:, cols] = _colsum(dpw * _nn(pmg, w_ref[g]))
            dpmw = (dpw * sc_ref[:, cols]).astype(BF16)
            dw_ref[g] = _tn(pmg, dpmw)
            dpm = _nt(dpmw, w_ref[g])
            du_ref[:, cols] = (_window_sum(dpm / _pool_counts(s, w), w, False) - dpm).astype(BF16)

    full = lambda shape: pl.BlockSpec(shape, lambda i: (0,) * len(shape))
    return pl.pallas_call(
        body, name=name, grid=(1,),
        in_specs=[full((s, D)), _shards(N_DEV, D_POOL, 128, li), full((s, D_POOL)), full((4, PGD, PGD)),
                  full((1, D_POOL))],
        out_specs=[full((s, D_POOL)), full((4, PGD, PGD)), full((1, D_POOL))],
        out_shape=[jax.ShapeDtypeStruct((s, D_POOL), BF16), jax.ShapeDtypeStruct((4, PGD, PGD), F32),
                   jax.ShapeDtypeStruct((1, D_POOL), F32)],
        compiler_params=_cparams(1))(dyp, wpo, pm, pool_w, pool_scale)


_SKEW_BASE = KB - (GRID_W - KW) - GRID_W


def attn_bwd(proj, da, e_rev, name):
    s = proj.shape[0]
    n_rows = s // GRID_W
    nb = n_rows // QROWS

    def body(q_ref, k_ref, v_ref, do_ref, e_ref, dq_ref, dk_ref, dv_ref, g_ref,
             s_ref, dp_ref, ds_ref, p_ref, dk_acc, dv_acc):
        b = pl.program_id(1)
        ks, delta, k0 = _attn_block_scalars(b, n_rows)

        @pl.when(b == 0)
        def _():
            dk_acc[...] = jnp.zeros_like(dk_acc)
            dv_acc[...] = jnp.zeros_like(dv_acc)
            g_ref[...] = jnp.zeros_like(g_ref)

        kwin = k_ref[pl.ds(k0, KB), :]
        vwin = v_ref[pl.ds(k0, KB), :]
        ri = lax.broadcasted_iota(I32, (QB, QB), 0)
        ci = lax.broadcasted_iota(I32, (QB, QB), 1)
        rev = jnp.where(ri + ci == QB - 1, 1.0, 0.0).astype(BF16)
        q = _nn(rev, q_ref[...]).astype(BF16) * ATT_SCALE
        do = _nn(rev, do_ref[...]).astype(BF16)
        lane = lax.broadcasted_iota(I32, (1, 128), 1)
        klane = lax.broadcasted_iota(I32, (1, KB), 1)
        dq = jnp.zeros((QB, 128), F32)
        for hh in range(2):
            lm = (lane // HEAD_DIM) == hh
            qh = jnp.where(lm, q, jnp.zeros_like(q))
            doh = jnp.where(lm, do, jnp.zeros_like(do))
            kh = jnp.where(lm, kwin, jnp.zeros_like(kwin))
            s_ref[...] = _nt(qh, kwin)
            dp_ref[...] = _nt(doh, vwin)
            e = e_ref[hh]

            def band(ib, carry):
                r0 = pl.multiple_of(ib * GRID_W, GRID_W)
                rows = pl.ds(r0, GRID_W)
                sb = _attn_band_logits(s_ref[rows, :], e, b, QROWS - 1 - ib, ks, delta, n_rows, klane)
                m = jnp.max(sb, axis=1, keepdims=True)
                p = jnp.exp(sb - m)
                p = p * (1.0 / jnp.sum(p, axis=1, keepdims=True))
                dp = dp_ref[rows, :]
                ds_ref[rows, :] = p * (dp - jnp.sum(p * dp, axis=1, keepdims=True))
                p_ref[rows, :] = p.astype(BF16)
                return carry

            lax.fori_loop(0, QROWS, band, 0, unroll=True)
            ds = ds_ref[...]
            dsb = ds.astype(BF16)
            dq = dq + _nn(dsb, kh) * ATT_SCALE
            dk_acc[pl.ds(k0, KB), :] += _tn(dsb, qh)
            dv_acc[pl.ds(k0, KB), :] += _tn(p_ref[...], doh)
            t = pltpu.roll(ds.reshape(QROWS, GRID_W, KB), _SKEW_BASE, 2, stride=1, stride_axis=1).sum(axis=1)
            g = jnp.zeros((1, KB), F32)
            for ib in range(QROWS):
                shift = ((1 + ib + delta) * GRID_W) % KB
                g = g + pltpu.roll(t[ib:ib + 1, :], shift, 1)
            g_ref[hh] += g
        dq_ref[...] = _nn(rev, dq.astype(BF16)).astype(BF16)

        @pl.when(b == nb - 1)
        def _():
            dk_ref[...] = dk_acc[...].astype(BF16)
            dv_ref[...] = dv_acc[...].astype(BF16)

    col = pl.BlockSpec((s, 128), lambda j, b: (0, j))
    return pl.pallas_call(
        body, name=name, grid=(4, nb),
        in_specs=[pl.BlockSpec((QB, 128), lambda j, b: (b, j)),
                  pl.BlockSpec((s, 128), lambda j, b: (0, 4 + j)),
                  pl.BlockSpec((s, 128), lambda j, b: (0, 8 + j)),
                  pl.BlockSpec((QB, 128), lambda j, b: (b, j)),
                  pl.BlockSpec((2, GRID_W, KB), lambda j, b: (j, 0, 0))],
        out_specs=[pl.BlockSpec((QB, 128), lambda j, b: (b, j)), col, col,
                   pl.BlockSpec((2, 1, KB), lambda j, b: (j, 0, 0))],
        out_shape=[jax.ShapeDtypeStruct((s, D_ATTN), BF16)] * 3 + [jax.ShapeDtypeStruct((N_HEADS, 1, KB), F32)],
        scratch_shapes=[pltpu.VMEM((QB, KB), F32), pltpu.VMEM((QB, KB), F32), pltpu.VMEM((QB, KB), F32),
                        pltpu.VMEM((QB, KB), BF16), pltpu.VMEM((s, 128), F32), pltpu.VMEM((s, 128), F32)],
        compiler_params=_cparams(2))(proj, proj, proj, da, e_rev)


def dh0_bwd(dz1, dproj, win, li, name):
    s = dz1.shape[0]
    tm = 256
    bn = N_PROJ // N_DEV

    def body(dz_ref, dp_ref, w_ref, o_ref):
        acc = ALPHA * dz_ref[...]
        for j in range(N_DEV):
            acc = acc + _nt(dp_ref[:, j * bn:(j + 1) * bn], w_ref[j])
        o_ref[...] = acc

    row = pl.BlockSpec((tm, D), lambda i: (i, 0))
    return pl.pallas_call(
        body, name=name, grid=(s // tm,),
        in_specs=[row, pl.BlockSpec((tm, N_PROJ), lambda i: (i, 0)), _shards(N_DEV, D, bn, li)],
        out_specs=row, out_shape=jax.ShapeDtypeStruct((s, D), F32),
        compiler_params=_cparams(1))(dz1, dproj, win)


def _coords():
    return lax.axis_index("x"), lax.axis_index("y"), lax.axis_index("c")


def _dev_index(px, py, pc):
    return 4 * px + 2 * py + pc


def all_gather(arrs, name):
    n = len(arrs)

    def body(*refs):
        ins, outs = refs[:n], refs[n:2 * n]
        send_sems, recv_sems, local_sems = refs[2 * n:]
        x, y, c = _coords()
        me, sibling = (x, y, c), (x, y, 1 - c)
        chips = [(1 - x, y), (x, 1 - y), (1 - x, 1 - y)]

        def copy(a, k, block, to, src=None):
            dst = outs[a].at[_dev_index(*block)]
            return pltpu.make_async_remote_copy(
                src_ref=dst if src is None else src, dst_ref=dst,
                send_sem=send_sems.at[a, k], recv_sem=recv_sems.at[a, k], device_id=to, device_id_type=MESH)

        mine = [pltpu.make_async_copy(ins[a], outs[a].at[_dev_index(*me)], local_sems.at[a]) for a in range(n)]
        for cp in mine:
            cp.start()
        first = []
        for a in range(n):
            first.append(copy(a, 0, me, sibling, src=ins[a]))
            first += [copy(a, 1 + j, me, (*chip, c), src=ins[a]) for j, chip in enumerate(chips)]
        for cp in first:
            cp.start()
        passed = []
        for j, chip in enumerate(chips):
            for a in range(n):
                copy(a, 1 + j, (*chip, c), me).wait_recv()
                cp = copy(a, 4 + j, (*chip, c), sibling)
                cp.start()
                passed.append(cp)
        for a in range(n):
            copy(a, 0, sibling, me).wait_recv()
            for j, chip in enumerate(chips):
                copy(a, 4 + j, (*chip, 1 - c), me).wait_recv()
        for cp in first + passed:
            cp.wait_send()
        for cp in mine:
            cp.wait()

    return pl.pallas_call(
        body, name=name,
        out_shape=[jax.ShapeDtypeStruct((N_DEV,) + a.shape, a.dtype) for a in arrs],
        in_specs=[ANY] * n, out_specs=[ANY] * n,
        scratch_shapes=[pltpu.SemaphoreType.DMA((n, 7)), pltpu.SemaphoreType.DMA((n, 7)),
                        pltpu.SemaphoreType.DMA((n,))],
    )(*arrs)


def exchange_sibling(grads, name):
    n = len(grads)

    def body(*refs):
        ins, outs = refs[:n], refs[n:2 * n]
        send_sems, recv_sems = refs[2 * n:]
        x, y, c = _coords()
        sibling = (x, y, 1 - c)
        copies = []
        for a in range(n):
            for k in range(4):
                blk = _dev_index(x ^ (k & 1), y ^ (k >> 1), 1 - c)
                copies.append(pltpu.make_async_remote_copy(
                    src_ref=ins[a].at[blk], dst_ref=outs[a].at[k],
                    send_sem=send_sems.at[a, k], recv_sem=recv_sems.at[a, k], device_id=sibling, device_id_type=MESH))
        for cp in copies:
            cp.start()
        for cp in copies:
            cp.wait()

    return pl.pallas_call(
        body, name=name,
        out_shape=[jax.ShapeDtypeStruct((4,) + g.shape[1:], g.dtype) for g in grads],
        in_specs=[ANY] * n, out_specs=[ANY] * n,
        scratch_shapes=[pltpu.SemaphoreType.DMA((n, 4)), pltpu.SemaphoreType.DMA((n, 4))],
    )(*grads)


HBM = pl.BlockSpec(memory_space=pltpu.HBM)
SEM = pl.BlockSpec(memory_space=pltpu.SEMAPHORE)
_EFFECT = pltpu.SideEffectType.DATAFLOW_SIDE_EFFECTING
_TOKEN = jax.ShapeDtypeStruct((8, 128), F32)


def _in_hbm(a):
    return pltpu.with_memory_space_constraint(a, pltpu.HBM)


def _hbm_like(a):
    return pltpu.HBM(a.shape, a.dtype)


def _peers(x, y, c):
    return [(x, y, 1 - c), (1 - x, y, c), (x, 1 - y, c), (1 - x, 1 - y, c)]


def ag_start(lands, after, name):
    n = len(lands)

    def body(*refs):
        land = refs[:n]
        send_sem, recv_sem, token = refs[n + 1], refs[n + 2], refs[-1]
        x, y, c = _coords()
        me = _dev_index(x, y, c)
        for k, peer in enumerate(_peers(x, y, c)):
            for a in range(n):
                pltpu.make_async_remote_copy(src_ref=land[a].at[me], dst_ref=land[a].at[me], send_sem=send_sem.at[k],
                                             recv_sem=recv_sem.at[k], device_id=peer, device_id_type=MESH).start()
        token[...] = jnp.zeros_like(token)

    res = pl.pallas_call(
        body, name=name,
        out_shape=(pltpu.SemaphoreType.DMA((4,)), pltpu.SemaphoreType.DMA((4,)), *[_hbm_like(l) for l in lands], _TOKEN),
        in_specs=[HBM] * n + [ANY], out_specs=(SEM, SEM, *[HBM] * n, pl.BlockSpec(memory_space=pltpu.VMEM)),
        input_output_aliases={a: 2 + a for a in range(n)},
        compiler_params=pltpu.CompilerParams(has_side_effects=_EFFECT),
    )(*[_in_hbm(l) for l in lands], after)
    return res[0], res[1], list(res[2:2 + n]), res[-1]


def ag_forward(send_sem, recv_sem, lands, after, name):
    n = len(lands)

    def body(*refs):
        send_sem, recv_sem = refs[0], refs[1]
        land = refs[2:2 + n]
        fsend, frecv = refs[3 + n], refs[4 + n]
        x, y, c = _coords()
        peers = _peers(x, y, c)
        for k in range(1, 4):
            blk = _dev_index(*peers[k])
            for a in range(n):
                pltpu.make_async_remote_copy(src_ref=land[a].at[blk], dst_ref=land[a].at[blk], send_sem=send_sem.at[k],
                                             recv_sem=recv_sem.at[k], device_id=peers[k], device_id_type=MESH).wait_recv()
        for k in range(1, 4):
            blk = _dev_index(*peers[k])
            for a in range(n):
                pltpu.make_async_remote_copy(src_ref=land[a].at[blk], dst_ref=land[a].at[blk], send_sem=fsend.at[k - 1],
                                             recv_sem=frecv.at[k - 1], device_id=peers[0], device_id_type=MESH).start()

    res = pl.pallas_call(
        body, name=name,
        out_shape=(pltpu.SemaphoreType.DMA((3,)), pltpu.SemaphoreType.DMA((3,)), *[_hbm_like(l) for l in lands]),
        in_specs=[SEM, SEM, *[HBM] * n, ANY], out_specs=(SEM, SEM, *[HBM] * n),
        input_output_aliases={2 + a: 2 + a for a in range(n)},
        compiler_params=pltpu.CompilerParams(has_side_effects=_EFFECT),
    )(send_sem, recv_sem, *lands, after)
    return res[0], res[1], list(res[2:])


def ag_finish(send_sem, recv_sem, fsend, frecv, lands, after, name):
    n = len(lands)

    def body(*refs):
        send_sem, recv_sem, fsend, frecv = refs[:4]
        land = refs[4:4 + n]
        x, y, c = _coords()
        me = _dev_index(x, y, c)
        peers = _peers(x, y, c)
        for k in range(4):
            for a in range(n):
                pltpu.make_async_remote_copy(src_ref=land[a].at[me], dst_ref=land[a].at[me], send_sem=send_sem.at[k],
                                             recv_sem=recv_sem.at[k], device_id=peers[k], device_id_type=MESH).wait_send()
        sib = _dev_index(*peers[0])
        for a in range(n):
            pltpu.make_async_remote_copy(src_ref=land[a].at[sib], dst_ref=land[a].at[sib], send_sem=send_sem.at[0],
                                         recv_sem=recv_sem.at[0], device_id=peers[0], device_id_type=MESH).wait_recv()
        for k in range(1, 4):
            mine = _dev_index(*peers[k])
            theirs = _dev_index(peers[k][0], peers[k][1], 1 - c)
            for a in range(n):
                pltpu.make_async_remote_copy(src_ref=land[a].at[mine], dst_ref=land[a].at[theirs], send_sem=fsend.at[k - 1],
                                             recv_sem=frecv.at[k - 1], device_id=peers[0], device_id_type=MESH).wait()

    res = pl.pallas_call(
        body, name=name, out_shape=tuple(_hbm_like(l) for l in lands),
        in_specs=[SEM] * 4 + [HBM] * n + [ANY], out_specs=tuple([HBM] * n),
        input_output_aliases={4 + a: a for a in range(n)},
        compiler_params=pltpu.CompilerParams(has_side_effects=_EFFECT),
    )(send_sem, recv_sem, fsend, frecv, *lands, after)
    return list(res)


def rs_start(psums, name):
    n = len(psums)
    lands = [lax.empty(p.shape, p.dtype) for p in psums]

    def body(*refs):
        src, land = refs[:n], refs[n:2 * n]
        send_sem, recv_sem, token = refs[2 * n], refs[2 * n + 1], refs[-1]
        peers = _peers(*_coords())
        for k in range(3):
            for a in range(n):
                pltpu.make_async_remote_copy(src_ref=src[a].at[k], dst_ref=land[a].at[k], send_sem=send_sem.at[k],
                                             recv_sem=recv_sem.at[k], device_id=peers[k + 1], device_id_type=MESH).start()
        token[...] = jnp.zeros_like(token)

    res = pl.pallas_call(
        body, name=name,
        out_shape=(pltpu.SemaphoreType.DMA((3,)), pltpu.SemaphoreType.DMA((3,)), *[_hbm_like(p) for p in psums],
                   *[_hbm_like(l) for l in lands], _TOKEN),
        in_specs=[HBM] * (2 * n), out_specs=(SEM, SEM, *[HBM] * (2 * n), pl.BlockSpec(memory_space=pltpu.VMEM)),
        input_output_aliases={a: 2 + a for a in range(2 * n)},
        compiler_params=pltpu.CompilerParams(has_side_effects=_EFFECT),
    )(*[_in_hbm(p) for p in psums], *[_in_hbm(l) for l in lands])
    return res[0], res[1], list(res[2:2 + n]), list(res[2 + n:2 + 2 * n]), res[-1]


def rs_finish(send_sem, recv_sem, psums, lands, after, name):
    n = len(psums)

    def body(*refs):
        send_sem, recv_sem = refs[0], refs[1]
        src, land = refs[2:2 + n], refs[2 + n:2 + 2 * n]
        peers = _peers(*_coords())
        for k in range(3):
            for a in range(n):
                pltpu.make_async_remote_copy(src_ref=src[a].at[k], dst_ref=land[a].at[k], send_sem=send_sem.at[k],
                                             recv_sem=recv_sem.at[k], device_id=peers[k + 1], device_id_type=MESH).wait()

    res = pl.pallas_call(
        body, name=name, out_shape=tuple(_hbm_like(l) for l in lands),
        in_specs=[SEM, SEM] + [HBM] * (2 * n) + [ANY], out_specs=tuple([HBM] * n),
        input_output_aliases={2 + n + a: a for a in range(n)},
        compiler_params=pltpu.CompilerParams(has_side_effects=_EFFECT),
    )(send_sem, recv_sem, *psums, *lands, after)
    return list(res)


def _row_tile(r):
    return r if r <= 512 else 512


def pair_add(blk_idx, g, recv, name):
    _, r, c = g.shape
    tr = _row_tile(r)

    def body(idx_ref, g_ref, r_ref, own_ref, oth_ref):
        k = pl.program_id(1)
        sm = g_ref[...].astype(F32) + r_ref[...].astype(F32)

        @pl.when(k == 0)
        def _():
            own_ref[...] = sm

        @pl.when(k > 0)
        def _():
            oth_ref[...] = sm.astype(BF16)

    grid_spec = pltpu.PrefetchScalarGridSpec(
        num_scalar_prefetch=1, grid=(r // tr, 4),
        in_specs=[pl.BlockSpec((None, tr, c), lambda t, k, idx: (idx[k], t, 0)),
                  pl.BlockSpec((None, tr, c), lambda t, k, idx: (k, t, 0))],
        out_specs=[pl.BlockSpec((tr, c), lambda t, k, idx: (t, 0)),
                   pl.BlockSpec((None, tr, c), lambda t, k, idx: (jnp.maximum(k - 1, 0), t, 0))])
    return pl.pallas_call(
        body, name=name, grid_spec=grid_spec,
        out_shape=[jax.ShapeDtypeStruct((r, c), F32), jax.ShapeDtypeStruct((3, r, c), BF16)],
        compiler_params=_cparams(2))(blk_idx, g, recv)


def _adamw(w, g, m, v):
    m = ADAM_B1 * m + (1.0 - ADAM_B1) * g
    v = ADAM_B2 * v + (1.0 - ADAM_B2) * (g * g)
    m_hat = m / (1.0 - ADAM_B1 ** ADAM_STEP)
    v_hat = v / (1.0 - ADAM_B2 ** ADAM_STEP)
    delta = -ADAM_LR * (m_hat / (jnp.sqrt(v_hat) + ADAM_EPS) + ADAM_WD * w)
    return delta, m, v


def adamw_shard(own, recv, w, m, v, li, prev, name):
    r, c = own.shape
    tr = _row_tile(r)

    def body(own_ref, recv_ref, w_ref, m_ref, v_ref, p0, p1, p2, p3, g_ref, d_ref, nm_ref, nv_ref):
        g = own_ref[...] + recv_ref[0].astype(F32) + recv_ref[1].astype(F32) + recv_ref[2].astype(F32)
        delta, nm, nv = _adamw(w_ref[...], g, m_ref[...], v_ref[...])
        g_ref[...] = g
        d_ref[...] = delta
        nm_ref[...] = nm
        nv_ref[...] = nv

    lay = pl.BlockSpec((None, tr, c), lambda t: (li, t, 0))
    stack = jax.ShapeDtypeStruct((DEPTH, r, c), F32)
    return pl.pallas_call(
        body, name=name, grid=(r // tr,),
        in_specs=[pl.BlockSpec((tr, c), lambda t: (t, 0)), pl.BlockSpec((3, tr, c), lambda t: (0, t, 0)),
                  lay, lay, lay, ANY, ANY, ANY, ANY],
        out_specs=[lay] * 4, out_shape=[stack] * 4,
        input_output_aliases={5: 0, 6: 1, 7: 2, 8: 3},
        compiler_params=_cparams(1))(own, recv, w, m, v, *prev)


def adamw_replicated(gathered, w, m, v, name):
    _, r, c = gathered.shape
    tr = 88

    def body(gs_ref, w_ref, m_ref, v_ref, g_ref, d_ref, nm_ref, nv_ref):
        g = gs_ref[0]
        for d in range(1, N_DEV):
            g = g + gs_ref[d]
        delta, nm, nv = _adamw(w_ref[...], g, m_ref[...], v_ref[...])
        g_ref[...] = g
        d_ref[...] = delta
        nm_ref[...] = nm
        nv_ref[...] = nv

    row = pl.BlockSpec((tr, c), lambda t: (t, 0))
    return pl.pallas_call(
        body, name=name, grid=(r // tr,),
        in_specs=[pl.BlockSpec((N_DEV, tr, c), lambda t: (0, t, 0)), row, row, row],
        out_specs=[row] * 4, out_shape=[jax.ShapeDtypeStruct((r, c), F32)] * 4,
        compiler_params=_cparams(1))(gathered, w, m, v)


def adamw_plain(g, w, m, v, name):
    def body(g_ref, w_ref, m_ref, v_ref, d_ref, nm_ref, nv_ref):
        delta, nm, nv = _adamw(w_ref[...], g_ref[...], m_ref[...], v_ref[...])
        d_ref[...] = delta
        nm_ref[...] = nm
        nv_ref[...] = nv

    return pl.pallas_call(body, name=name, out_shape=[jax.ShapeDtypeStruct(w.shape, F32)] * 3)(g, w, m, v)


_PACK = (("ln_in_g", (D,)), ("ln_in_b", (D,)), ("b_in", (DEPTH, N_PROJ)), ("rpb", (DEPTH, N_HEADS, 2 * KH - 1, 2 * KW - 1)),
         ("pool_w", (DEPTH, 4, PGD, PGD)), ("pool_scale", (DEPTH, D_POOL)), ("ln1_g", (DEPTH, D)), ("ln1_b", (DEPTH, D)),
         ("conv_b", (DEPTH, D_FF)), ("ln2_g", (DEPTH, D)), ("ln2_b", (DEPTH, D)), ("conv_w", (DEPTH, 3, D_FF)))
_PACK_LANES = 1024


def _pack_rows(shape):
    return -(-int(np.prod(shape)) // _PACK_LANES)


_PACK_ROWS = -(-sum(_pack_rows(s) for _, s in _PACK) // 88) * 88


def _pack(parts):
    rows = []
    for name, shape in _PACK:
        flat = parts[name].reshape(-1).astype(F32)
        rows.append(jnp.pad(flat, (0, _pack_rows(shape) * _PACK_LANES - flat.shape[0])))
    used = sum(_pack_rows(s) for _, s in _PACK)
    rows.append(jnp.zeros(((_PACK_ROWS - used) * _PACK_LANES,), F32))
    return jnp.concatenate(rows).reshape(_PACK_ROWS, _PACK_LANES)


def _unpack(packed):
    out, r0 = {}, 0
    for name, shape in _PACK:
        n = int(np.prod(shape))
        nr = _pack_rows(shape)
        out[name] = packed[r0:r0 + nr].reshape(-1)[:n].reshape(shape)
        r0 += nr
    return out


def _bias_tables(rpb_l):
    qc = np.arange(GRID_W)[:, None]
    kc = np.arange(GRID_W)[None, :]
    start = np.clip(qc - KW // 2, 0, GRID_W - KW)
    valid = (kc >= start) & (kc < start + KW)
    col = np.clip(kc - qc, -(KW - 1), KW - 1) + KW - 1
    onehot = (col.reshape(-1)[None, :] == np.arange(2 * KW - 1)[:, None]).astype(np.float32)
    rows = jnp.pad(rpb_l, ((0, 0), (0, 1), (0, 0)))
    tab = jnp.einsum("hij,jm->him", rows, jnp.asarray(onehot), precision=lax.Precision.HIGHEST)
    tab = tab.reshape(N_HEADS, KROWS, GRID_W, GRID_W).transpose(0, 2, 1, 3)
    ok = valid[None, :, None, :] & (np.arange(KROWS) < 2 * KH - 1)[None, None, :, None]
    tab = jnp.where(jnp.asarray(ok), tab, NEG_INF).reshape(N_HEADS, GRID_W, KB)
    return tab, tab[:, ::-1, :]


_SHARDED = ("w_in", "w_attn_out", "w_pool_out", "w_mix_out", "w_up", "w_down", "w_ple_gate", "w_ple_proj")
_NAMES = ("ln_in_g", "ln_in_b", "w_in", "b_in", "rpb", "w_attn_out", "pool_w", "pool_scale", "w_pool_out", "w_mix_out",
          "ln1_g", "ln1_b", "w_up", "conv_w", "conv_b", "w_down", "w_ple_gate", "w_ple_proj", "ln2_g", "ln2_b")


def kernel(x, p, ln_in_g, ln_in_b, w_in, b_in, rpb, w_attn_out, pool_w, pool_scale, w_pool_out, w_mix_out, ln1_g, ln1_b, w_up, conv_w, conv_b, w_down, w_ple_gate, w_ple_proj, ln2_g, ln2_b, loss_target, m_ln_in_g, m_ln_in_b, m_w_in, m_b_in, m_rpb, m_w_attn_out, m_pool_w, m_pool_scale, m_w_pool_out, m_w_mix_out, m_ln1_g, m_ln1_b, m_w_up, m_conv_w, m_conv_b, m_w_down, m_w_ple_gate, m_w_ple_proj, m_ln2_g, m_ln2_b, v_ln_in_g, v_ln_in_b, v_w_in, v_b_in, v_rpb, v_w_attn_out, v_pool_w, v_pool_scale, v_w_pool_out, v_w_mix_out, v_ln1_g, v_ln1_b, v_w_up, v_conv_w, v_conv_b, v_w_down, v_w_ple_gate, v_w_ple_proj, v_ln2_g, v_ln2_b):
    a = dict(locals())
    W = {n: a[n] for n in _NAMES}
    M = {n: a["m_" + n] for n in _NAMES}
    V = {n: a["v_" + n] for n in _NAMES}
    xi, yi, ci = _coords()
    me = _dev_index(xi, yi, ci)
    x2, tgt = x[0], loss_target[0]
    pb = p[:, 0].astype(BF16)

    ex = _Exchange(W, M, V)
    loss_part, dx, parts = _local_step(x2, tgt, pb, W, ex)
    loss = lax.psum(loss_part[0, 0], AXES)
    stacks = ex.stacks

    (gath,) = all_gather([_pack(parts)], "ag_small_grads")
    zero_cw = jnp.zeros((DEPTH, 3, D_FF), F32)
    packs = [_pack({**{n: src[n] for n, _ in _PACK if n != "conv_w"}, "conv_w": zero_cw}) for src in (W, M, V)]
    outs = [_unpack(o) for o in adamw_replicated(gath, *packs, "adamw_replicated")]
    g_cw = lax.dynamic_slice_in_dim(outs[0]["conv_w"], me * FF_SHARD, FF_SHARD, axis=2)
    flat = lambda t: t.reshape(DEPTH * 3, FF_SHARD)
    cw_out = [o.reshape(DEPTH, 3, FF_SHARD) for o in
              adamw_plain(flat(g_cw), flat(conv_w), flat(m_conv_w), flat(v_conv_w), "adamw_conv_w")]
    res = []
    for k in range(4):
        d = {n: stacks[n][k] for n in _SHARDED}
        d.update({n: outs[k][n] for n, _ in _PACK if n != "conv_w"})
        d["conv_w"] = g_cw if k == 0 else cw_out[k - 1]
        res.append(d)
    return (loss, dx[None], *[res[k][n] for k in range(4) for n in _NAMES])


class _Exchange:
    GROUPS = (("w_ple_gate", "w_ple_proj", "w_down", "w_up"), ("w_mix_out", "w_attn_out", "w_pool_out"), ("w_in",))

    def __init__(self, W, M, V):
        self.W, self.M, self.V = W, M, V
        xi, yi, ci = _coords()
        me = _dev_index(xi, yi, ci)
        self.rel_idx = jnp.stack([_dev_index(xi ^ (k & 1), yi ^ (k >> 1), ci) for k in range(4)]).astype(I32)
        self.lands = [[lax.dynamic_update_index_in_dim(lax.empty((N_DEV,) + W[n].shape[1:], BF16),
                                                       W[n][li].astype(BF16), me, 0) for n in _SHARDED]
                      for li in range(DEPTH)]
        (cw,) = all_gather([W["conv_w"]], "ag_conv_w")
        self.cw = cw.transpose(1, 2, 0, 3).reshape(DEPTH, 3, 4, FF_BLK).transpose(0, 2, 1, 3)
        self.ag, self.fwd, self.rs, self.pending = {}, {}, {}, {}
        self.stacks = {n: [lax.empty((DEPTH,) + W[n].shape[1:], F32) for _ in range(4)] for n in _SHARDED}
        self.ag[0] = ag_start(self.lands[0], cw, "ag_start0")

    def tokens(self):
        return [self.ag[0][3]]

    def prefetch(self, li, after):
        send, recv, lands, _ = self.ag[li]
        self.fwd[li] = ag_forward(send, recv, lands, after, f"ag_forward{li}")

    def weights(self, li, after):
        send, recv, _, _ = self.ag.pop(li)
        fsend, frecv, lands = self.fwd.pop(li)
        lands = ag_finish(send, recv, fsend, frecv, lands, after, f"ag_finish{li}")
        if li + 1 < DEPTH:
            self.ag[li + 1] = ag_start(self.lands[li + 1], lands[0], f"ag_start{li + 1}")
        return dict(zip(_SHARDED, lands)), self.cw[li]

    def grads(self, li, group, gw):
        self.pending.setdefault(li, {}).update(gw)
        if li != 0 and group != len(self.GROUPS) - 1:
            return None
        gw = self.pending.pop(li)
        names = tuple(gw)
        tag = f"{li}_{group}" if li == 0 else f"{li}"
        glist = [gw[n] for n in names]
        recv1 = exchange_sibling(glist, f"rs_d2d{tag}")
        sums = [pair_add(self.rel_idx, g, r1, f"pair_add_{n}{li}") for n, g, r1 in zip(names, glist, recv1)]
        send, recv, psums, lands, token = rs_start([s_[1] for s_ in sums], f"rs_start{tag}")
        self.rs.setdefault(li, []).append((tag, names, send, recv, psums, lands, [s_[0] for s_ in sums]))
        return token

    def update(self, li, after):
        for tag, names, send, recv, psums, lands, owns in self.rs.pop(li):
            recv2 = rs_finish(send, recv, psums, lands, after, f"rs_finish{tag}")
            for n, own, r2 in zip(names, owns, recv2):
                self.stacks[n] = adamw_shard(own, r2, self.W[n], self.M[n], self.V[n], li, self.stacks[n],
                                             f"adamw_{n}{li}")


def _local_step(x2, tgt, pb, W, ex):
    depth = W["rpb"].shape[0]
    vec = lambda t: t.reshape(1, -1)
    ln1_g, ln1_b, ln2_g, ln2_b = W["ln1_g"], W["ln1_b"], W["ln2_g"], W["ln2_b"]
    b_in, rpb, pool_scale = W["b_in"], W["rpb"], W["pool_scale"]
    cb_full = W["conv_b"].reshape(depth, 4, 1, FF_BLK)
    pool_w_b = W["pool_w"].astype(BF16)

    h, hb = ln_fwd(x2, vec(W["ln_in_g"]), vec(W["ln_in_b"]), "ln_in", after=ex.tokens())
    ex.prefetch(0, hb)
    saved = []
    for li in range(depth):
        G, cw = ex.weights(li, hb)
        e_tab, e_rev = _bias_tables(rpb[li])
        bias = vec(b_in[li])
        proj = proj_fwd(hb, G["w_in"], bias, li, 0, N_DEV, BF16, f"proj{li}")
        u = proj_fwd(hb, G["w_in"], bias, li, 3, 1, F32, f"proj_u{li}")
        att = attn_fwd(proj, e_tab, f"attn{li}")
        pm, pw = pool_fwd(u, pool_w_b[li], vec(pool_scale[li]), f"pool{li}")
        mg, ya, yp = merge_fwd(att, pw, G["w_attn_out"], G["w_pool_out"], proj, li, f"merge{li}")
        if li + 1 < depth:
            ex.prefetch(li + 1, mg)
        z1, h1, h1b = mix_ln_fwd(mg, G["w_mix_out"], h, vec(ln1_g[li]), vec(ln1_b[li]), li, f"mix_ln{li}")
        up = up_fwd(h1b, G["w_up"], li, f"up{li}")
        t = ffn_act_fwd(up, cw, cb_full[li], f"ffn_act{li}")
        z2, h2, h2b, pg, pp = down_ple_ln_fwd(t, G["w_down"], h1b, G["w_ple_gate"], pb[li], G["w_ple_proj"], h1,
                                              vec(ln2_g[li]), vec(ln2_b[li]), li, f"down_ln{li}")
        saved.append(dict(hb=hb, proj=proj, att=att, pm=pm, pw=pw, mg=mg, ya=ya, yp=yp, z1=z1, h1b=h1b, up=up, t=t,
                          z2=z2, pg=pg, pp=pp, e_rev=e_rev, G=G, cw=cw))
        h, hb = h2, h2b

    dh, loss_part = loss_bwd(h, tgt, "loss")
    small = {n: [None] * depth for n in ("b_in", "rpb", "pool_w", "pool_scale", "ln1_g", "ln1_b", "conv_b", "ln2_g",
                                         "ln2_b", "conv_w")}
    token = ()
    tok = lambda t: () if t is None else (t,)
    for li in reversed(range(depth)):
        sv = saved[li]
        G, cw = sv["G"], sv["cw"]
        dz2, dz2b, dpg, dpp, dg2, db2 = ln2_ple_bwd(dh, sv["z2"], vec(ln2_g[li]), sv["pg"], sv["pp"], f"ln2_bwd{li}",
                                                    after=token)
        gw = {}
        gw["w_ple_gate"] = wgrad_rows(sv["h1b"], dpg, f"dw_pg{li}")
        gw["w_ple_proj"] = wgrad_cols(pb[li], dpp, f"dw_pp{li}")
        gw["w_down"] = wgrad_down(sv["t"], dz2b, f"dw_down{li}").reshape(N_DEV, FF_SHARD, D)
        dhv, dhg, dcw, dcb = ffn_act_bwd(dz2b, G["w_down"], sv["up"], cw, cb_full[li], li, f"ffn_bwd{li}")
        gw["w_up"] = wgrad_up(sv["h1b"], dhv, dhg, f"dw_up{li}")
        token = tok(ex.grads(li, 0, gw))
        dz1, dz1b, dg1, db1 = dh1_ln1_bwd(dz2, dpg, G["w_ple_gate"], dhv, dhg, G["w_up"], sv["z1"], vec(ln1_g[li]), li,
                                          f"ln1_bwd{li}", after=token)
        gw = {"w_mix_out": wgrad_rows(sv["mg"], dz1b, f"dw_mix{li}")}
        dya, dyp, dga, dgb = merge_bwd(dz1b, G["w_mix_out"], sv["proj"], sv["ya"], sv["yp"], li, f"merge_bwd{li}")
        gw["w_attn_out"] = wgrad_cols(sv["att"], dya, f"dw_ao{li}")
        gw["w_pool_out"] = wgrad_cols(sv["pw"], dyp, f"dw_po{li}")
        token = tok(ex.grads(li, 1, gw))
        da = attn_out_bwd(dya, G["w_attn_out"], li, f"da{li}", after=token)
        du, dpool_w, dpool_sc = pool_bwd(dyp, G["w_pool_out"], sv["pm"], pool_w_b[li], vec(pool_scale[li]), li,
                                         f"pool_bwd{li}")
        dq, dk, dv, drpb = attn_bwd(sv["proj"], da, sv["e_rev"], f"attn_bwd{li}")
        dproj = jnp.concatenate([dq, dk, dv, du, dga, dgb], axis=1)
        dw_in, db_in = wgrad_cols(sv["hb"], dproj, f"dw_in{li}", with_colsum=True)
        dh = dh0_bwd(dz1, dproj, G["w_in"], li, f"dh0{li}")
        small["b_in"][li] = db_in.reshape(N_PROJ)
        small["rpb"][li] = drpb.reshape(N_HEADS, KROWS, GRID_W)[:, :2 * KH - 1, :2 * KW - 1]
        small["pool_w"][li] = dpool_w
        small["pool_scale"][li] = dpool_sc.reshape(D_POOL)
        small["ln1_g"][li], small["ln1_b"][li] = dg1.reshape(D), db1.reshape(D)
        small["ln2_g"][li], small["ln2_b"][li] = dg2.reshape(D), db2.reshape(D)
        small["conv_b"][li] = dcb.reshape(D_FF)
        small["conv_w"][li] = dcw.transpose(1, 0, 2).reshape(3, D_FF)
        token = tok(ex.grads(li, 2, {"w_in": dw_in}))
        if li + 1 < depth:
            ex.update(li + 1, dh)
    dx, dg_in, db_in0 = ln_bwd(dh, x2, vec(W["ln_in_g"]), "ln_in_bwd", after=token)
    ex.update(0, dx)
    parts = {n: jnp.stack(v_) for n, v_ in small.items()}
    parts["ln_in_g"], parts["ln_in_b"] = dg_in.reshape(D), db_in0.reshape(D)
    return loss_part, dx, parts
```

```python
import numpy as np
import jax
import jax.numpy as jnp
from jax import lax
from jax.experimental import pallas as pl
from jax.experimental.pallas import tpu as pltpu

F32 = jnp.float32
BF16 = jnp.bfloat16
I32 = jnp.int32

D = 1024
DEPTH = 4
GRID_W = 64
N_HEADS = 8
HEAD_DIM = 64
D_ATTN = 512
KH = 8
KW = 16
POOL_WINDOWS = (2, 4, 8, 16)
D_POOL = 512
PGD = 128
D_FF = 2816
PLE_DIM = 256
N_PROJ = 4096
ALPHA = (2 * DEPTH) ** 0.25
LN_EPS = 1e-5
NEG_INF = -1e30
ATT_SCALE = HEAD_DIM ** -0.5
ADAM_LR = 0.001
ADAM_B1 = 0.9
ADAM_B2 = 0.999
ADAM_EPS = 1e-08
ADAM_WD = 0.01
ADAM_STEP = 10

N_DEV = 8
AXES = ("x", "y", "c")
FF_BLK = D_FF // 4
FF_SHARD = D_FF // N_DEV
QROWS = 8
KROWS = 16
QB = QROWS * GRID_W
KB = KROWS * GRID_W
V7X_VMEM_LIMIT = 56 * 2 ** 20
MESH = pl.DeviceIdType.MESH
ANY = pl.BlockSpec(memory_space=pl.ANY)


def _cparams(n_grid):
    return pltpu.CompilerParams(dimension_semantics=("arbitrary",) * n_grid, vmem_limit_bytes=V7X_VMEM_LIMIT)


def _nn(a, b):
    return lax.dot_general(a, b, (((1,), (0,)), ((), ())), preferred_element_type=F32)


def _nt(a, b):
    return lax.dot_general(a, b, (((1,), (1,)), ((), ())), preferred_element_type=F32)


def _tn(a, b):
    return lax.dot_general(a, b, (((0,), (0,)), ((), ())), preferred_element_type=F32)


def _sigmoid(x):
    return 1.0 / (1.0 + jnp.exp(-x))


def _ln_fwd(z, g, b):
    mu = jnp.mean(z, axis=-1, keepdims=True)
    xc = z - mu
    var = jnp.mean(xc * xc, axis=-1, keepdims=True)
    return xc * lax.rsqrt(var + LN_EPS) * g + b


def _ln_bwd(dh, z, g):
    mu = jnp.mean(z, axis=-1, keepdims=True)
    xc = z - mu
    var = jnp.mean(xc * xc, axis=-1, keepdims=True)
    rstd = lax.rsqrt(var + LN_EPS)
    xhat = xc * rstd
    dxh = dh * g
    m1 = jnp.mean(dxh, axis=-1, keepdims=True)
    m2 = jnp.mean(dxh * xhat, axis=-1, keepdims=True)
    return rstd * (dxh - m1 - xhat * m2), dh * xhat


def _colsum(x):
    return jnp.sum(x, axis=0, keepdims=True)


def _lane_cat(ref):
    return jnp.concatenate([ref[j] for j in range(ref.shape[0])], axis=1)


def _row_cat(ref):
    n, r, c = ref.shape
    return ref[...].reshape(n * r, c)


def _shards(n, r, c, li, j_of=None):
    del li
    if j_of is None:
        return pl.BlockSpec((n, r, c), lambda *_: (0, 0, 0))
    return pl.BlockSpec((n, r, c), lambda *g: (j_of(*g), 0, 0))


def _shard(r, c, li, j_of):
    del li
    return pl.BlockSpec((None, r, c), lambda *g: (j_of(*g), 0, 0))


def ln_fwd(x, g, b, name, after=()):
    s = x.shape[0]
    tm = 512
    na = len(after)

    def body(x_ref, g_ref, b_ref, *rest):
        h_ref, hb_ref = rest[na:]
        h = _ln_fwd(x_ref[...], g_ref[...], b_ref[...])
        h_ref[...] = h
        hb_ref[...] = h.astype(BF16)

    row = pl.BlockSpec((tm, D), lambda i: (i, 0))
    vec = pl.BlockSpec((1, D), lambda i: (0, 0))
    return pl.pallas_call(
        body, name=name, grid=(s // tm,), in_specs=[row, vec, vec] + [ANY] * na, out_specs=[row, row],
        out_shape=[jax.ShapeDtypeStruct((s, D), F32), jax.ShapeDtypeStruct((s, D), BF16)],
        compiler_params=_cparams(1))(x, g, b, *after)


def proj_fwd(hb, win, bias, li, j0, nj, out_dtype, name, after=()):
    s = hb.shape[0]
    bn = N_PROJ // N_DEV
    tm = 1024

    def body(a_ref, w_ref, b_ref, *rest):
        rest[-1][...] = (_nn(a_ref[...], w_ref[...]) + b_ref[...]).astype(out_dtype)

    return pl.pallas_call(
        body, name=name, grid=(s // tm, nj),
        in_specs=[pl.BlockSpec((tm, D), lambda i, j: (i, 0)),
                  _shard(D, bn, li, lambda i, j: j0 + j),
                  pl.BlockSpec((1, bn), lambda i, j: (0, j0 + j))] + [ANY] * len(after),
        out_specs=pl.BlockSpec((tm, bn), lambda i, j: (i, j)),
        out_shape=jax.ShapeDtypeStruct((s, nj * bn), out_dtype),
        compiler_params=_cparams(2))(hb, win, bias, *after)


def _attn_block_scalars(b, n_rows):
    ks = jnp.clip(QROWS * b - KH // 2, 0, n_rows - KROWS)
    delta = ks - QROWS * b
    k0 = pl.multiple_of(ks * GRID_W, 256)
    return ks, delta, k0


def _attn_band_logits(sb, e, b, qr, ks, delta, n_rows, klane):
    shift = ((qr - delta - (KH - 1)) * GRID_W) % KB
    bias = pltpu.roll(e, shift, 1)
    rs = jnp.clip(QROWS * b + qr - KH // 2, 0, n_rows - KH)
    lo = (rs - ks) * GRID_W
    ok = (klane >= lo) & (klane < lo + KH * GRID_W)
    return jnp.where(ok, sb + bias, NEG_INF)


def attn_fwd(proj, e_tab, name):
    s = proj.shape[0]
    n_rows = s // GRID_W
    nb = n_rows // QROWS

    def body(q_ref, k_ref, v_ref, e_ref, o_ref, s_ref, p_ref):
        b = pl.program_id(1)
        ks, delta, k0 = _attn_block_scalars(b, n_rows)
        kwin = k_ref[pl.ds(k0, KB), :]
        vwin = v_ref[pl.ds(k0, KB), :]
        q = q_ref[...] * ATT_SCALE
        lane = lax.broadcasted_iota(I32, (1, 128), 1)
        klane = lax.broadcasted_iota(I32, (1, KB), 1)
        acc = jnp.zeros((QB, 128), F32)
        for hh in range(2):
            lm = (lane // HEAD_DIM) == hh
            qh = jnp.where(lm, q, jnp.zeros_like(q))
            vh = jnp.where(lm, vwin, jnp.zeros_like(vwin))
            s_ref[...] = _nt(qh, kwin)
            e = e_ref[hh]

            def band(qr, carry):
                r0 = pl.multiple_of(qr * GRID_W, GRID_W)
                sb = _attn_band_logits(s_ref[pl.ds(r0, GRID_W), :], e, b, qr, ks, delta, n_rows, klane)
                m = jnp.max(sb, axis=1, keepdims=True)
                p = jnp.exp(sb - m)
                l = jnp.sum(p, axis=1, keepdims=True)
                p_ref[pl.ds(r0, GRID_W), :] = (p * (1.0 / l)).astype(BF16)
                return carry

            lax.fori_loop(0, QROWS, band, 0, unroll=True)
            acc = acc + _nn(p_ref[...], vh)
        o_ref[...] = acc.astype(BF16)

    return pl.pallas_call(
        body, name=name, grid=(4, nb),
        in_specs=[pl.BlockSpec((QB, 128), lambda j, b: (b, j)),
                  pl.BlockSpec((s, 128), lambda j, b: (0, 4 + j)),
                  pl.BlockSpec((s, 128), lambda j, b: (0, 8 + j)),
                  pl.BlockSpec((2, GRID_W, KB), lambda j, b: (j, 0, 0))],
        out_specs=pl.BlockSpec((QB, 128), lambda j, b: (b, j)),
        out_shape=jax.ShapeDtypeStruct((s, D_ATTN), BF16),
        scratch_shapes=[pltpu.VMEM((QB, KB), F32), pltpu.VMEM((QB, KB), BF16)],
        compiler_params=_cparams(2))(proj, proj, proj, e_tab)


_POOL_PAD = 8


def _pool_counts(s, w):
    t = lax.broadcasted_iota(I32, (s, 1), 0)
    return (jnp.minimum(t + w // 2, s) - jnp.maximum(t - w // 2, 0)).astype(F32)


def _window_sum(x, w, back_first):
    s = x.shape[0]
    z = jnp.zeros((_POOL_PAD, x.shape[1]), F32)
    xe = jnp.concatenate([z, x, z], axis=0)
    n = s + 2 * _POOL_PAD
    acc = xe + pltpu.roll(xe, 1 if back_first else n - 1, 0)
    k = 1
    while 2 * k < w:
        acc = pltpu.roll(acc, k, 0) + pltpu.roll(acc, n - k, 0)
        k *= 2
    return acc[_POOL_PAD:_POOL_PAD + s, :]


def pool_fwd(u, pool_w, pool_scale, name):
    s = u.shape[0]

    def body(u_ref, w_ref, sc_ref, pm_ref, pw_ref):
        for g, w in enumerate(POOL_WINDOWS):
            cols = slice(g * PGD, (g + 1) * PGD)
            ug = u_ref[:, cols]
            pm = (_window_sum(ug, w, True) / _pool_counts(s, w) - ug).astype(BF16)
            pm_ref[:, cols] = pm
            pw_ref[:, cols] = (_nn(pm, w_ref[g]) * sc_ref[:, cols]).astype(BF16)

    full = lambda shape: pl.BlockSpec(shape, lambda i: (0,) * len(shape))
    return pl.pallas_call(
        body, name=name, grid=(1,),
        in_specs=[full((s, D_POOL)), full((4, PGD, PGD)), full((1, D_POOL))],
        out_specs=[full((s, D_POOL)), full((s, D_POOL))],
        out_shape=[jax.ShapeDtypeStruct((s, D_POOL), BF16)] * 2,
        compiler_params=_cparams(1))(u, pool_w, pool_scale)


def merge_fwd(a, pw, wao, wpo, proj, li, name, after=()):
    s = a.shape[0]
    tm, tn = 512, 512
    nt = D // tn
    per = tn // 128

    def body(a_ref, pw_ref, wa_ref, wp_ref, ga_ref, gb_ref, *rest):
        mg_ref, ya_ref, yp_ref = rest[len(after):]
        ya = _nn(a_ref[...], _lane_cat(wa_ref))
        yp = _nn(pw_ref[...], _lane_cat(wp_ref))
        mg = _sigmoid(ga_ref[...].astype(F32)) * ya + _sigmoid(gb_ref[...].astype(F32)) * yp
        mg_ref[...] = mg.astype(BF16)
        ya_ref[...] = ya.astype(BF16)
        yp_ref[...] = yp.astype(BF16)

    act = pl.BlockSpec((tm, D_ATTN), lambda i, j: (i, 0))
    wsp = _shards(per, D_ATTN, 128, li, lambda i, j: j)
    out = pl.BlockSpec((tm, tn), lambda i, j: (i, j))
    ga0 = (3 * D_ATTN + D_POOL) // tn
    return pl.pallas_call(
        body, name=name, grid=(s // tm, nt),
        in_specs=[act, act, wsp, wsp,
                  pl.BlockSpec((tm, tn), lambda i, j: (i, ga0 + j)),
                  pl.BlockSpec((tm, tn), lambda i, j: (i, ga0 + nt + j))] + [ANY] * len(after),
        out_specs=[out, out, out],
        out_shape=[jax.ShapeDtypeStruct((s, D), BF16)] * 3,
        compiler_params=_cparams(2))(a, pw, wao, wpo, proj, proj, *after)


def mix_ln_fwd(mg, wmix, h0, g, b, li, name):
    s = mg.shape[0]
    tm = 256

    def body(mg_ref, w_ref, h0_ref, g_ref, b_ref, z_ref, h_ref, hb_ref):
        z = ALPHA * h0_ref[...] + _nn(mg_ref[...], _row_cat(w_ref))
        h = _ln_fwd(z, g_ref[...], b_ref[...])
        z_ref[...] = z
        h_ref[...] = h
        hb_ref[...] = h.astype(BF16)

    row = pl.BlockSpec((tm, D), lambda i: (i, 0))
    vec = pl.BlockSpec((1, D), lambda i: (0, 0))
    return pl.pallas_call(
        body, name=name, grid=(s // tm,),
        in_specs=[row, _shards(N_DEV, D // N_DEV, D, li), row, vec, vec],
        out_specs=[row, row, row],
        out_shape=[jax.ShapeDtypeStruct((s, D), F32), jax.ShapeDtypeStruct((s, D), F32),
                   jax.ShapeDtypeStruct((s, D), BF16)],
        compiler_params=_cparams(1))(mg, wmix, h0, g, b)


def up_fwd(hb, wup, li, name):
    s = hb.shape[0]
    tm = 1024

    def body(a_ref, w_ref, o_ref):
        o_ref[...] = _nt(a_ref[...], w_ref[...]).astype(BF16)

    return pl.pallas_call(
        body, name=name, grid=(s // tm, N_DEV),
        in_specs=[pl.BlockSpec((tm, D), lambda i, j: (i, 0)), _shard(FF_BLK, D, li, lambda i, j: j)],
        out_specs=pl.BlockSpec((None, tm, FF_BLK), lambda i, j: (j, i, 0)),
        out_shape=jax.ShapeDtypeStruct((N_DEV, s, FF_BLK), BF16),
        compiler_params=_cparams(2))(hb, wup)


_SQRT_HALF = 0.7071067811865476
_INV_SQRT_2PI = 0.3989422804014327


def _shift_rows(x, prev_row, next_row):
    n = x.shape[0]
    r = lax.broadcasted_iota(I32, (n, 1), 0)
    back = jnp.where(r == 0, prev_row, pltpu.roll(x, 1, 0))
    fwd = jnp.where(r == n - 1, next_row, pltpu.roll(x, n - 1, 0))
    return back, fwd


HALO = 16


def _halo_maps(tm, s):
    th = tm // HALO
    return (lambda i: jnp.maximum(i * th - 1, 0)), (lambda i: jnp.minimum((i + 1) * th, s // HALO - 1))


def _slab_specs(tm, s, blk_of):
    before, after = _halo_maps(tm, s)
    main = pl.BlockSpec((None, tm, FF_BLK), lambda c, i: (blk_of(c), i, 0))
    prev = pl.BlockSpec((None, HALO, FF_BLK), lambda c, i: (blk_of(c), before(i), 0))
    nxt = pl.BlockSpec((None, HALO, FF_BLK), lambda c, i: (blk_of(c), after(i), 0))
    return main, prev, nxt


def ffn_act_fwd(up, conv_w, conv_b, name):
    s = up.shape[1]
    tm = 512
    nt = s // tm
    hv_main, _, _ = _slab_specs(tm, s, lambda c: c)
    hg_main, hg_prev, hg_next = _slab_specs(tm, s, lambda c: 4 + c)

    def body(hv_ref, hg_ref, hp_ref, hn_ref, cw_ref, cb_ref, t_ref):
        i = pl.program_id(1)
        hg = hg_ref[...].astype(F32)
        prow = jnp.where(i == 0, 0.0, hp_ref[...].astype(F32)[HALO - 1:HALO, :])
        nrow = jnp.where(i == nt - 1, 0.0, hn_ref[...].astype(F32)[0:1, :])
        back, fwd = _shift_rows(hg, prow, nrow)
        c = back * cw_ref[0:1, :] + hg * cw_ref[1:2, :] + fwd * cw_ref[2:3, :] + cb_ref[...]
        act = 0.5 * c * (1.0 + lax.erf(c * _SQRT_HALF))
        t_ref[...] = (act * hv_ref[...].astype(F32)).astype(BF16)

    return pl.pallas_call(
        body, name=name, grid=(4, nt),
        in_specs=[hv_main, hg_main, hg_prev, hg_next,
                  pl.BlockSpec((None, 3, FF_BLK), lambda c, i: (c, 0, 0)),
                  pl.BlockSpec((None, 1, FF_BLK), lambda c, i: (c, 0, 0))],
        out_specs=pl.BlockSpec((None, tm, FF_BLK), lambda c, i: (c, i, 0)),
        out_shape=jax.ShapeDtypeStruct((4, s, FF_BLK), BF16),
        compiler_params=_cparams(2))(up, up, up, up, conv_w, conv_b)


def down_ple_ln_fwd(t, wdown, hb, wpg, pb, wpp, h1, g, b, li, name):
    s = hb.shape[0]
    tm = 256

    def body(t_ref, wd_ref, hb_ref, wpg_ref, p_ref, wpp_ref, h1_ref, g_ref, b_ref,
             z_ref, h_ref, hbo_ref, pg_ref, pp_ref):
        wd = _row_cat(wd_ref)
        ffn = _nn(t_ref[0], wd[0:FF_BLK, :])
        for c in range(1, 4):
            ffn = ffn + _nn(t_ref[c], wd[c * FF_BLK:(c + 1) * FF_BLK, :])
        pg = _nn(hb_ref[...], _row_cat(wpg_ref))
        pp = _nn(p_ref[...], _lane_cat(wpp_ref))
        z = ALPHA * h1_ref[...] + ffn + _sigmoid(pg) * pp
        h = _ln_fwd(z, g_ref[...], b_ref[...])
        z_ref[...] = z
        h_ref[...] = h
        hbo_ref[...] = h.astype(BF16)
        pg_ref[...] = pg.astype(BF16)
        pp_ref[...] = pp.astype(BF16)

    row = pl.BlockSpec((tm, D), lambda i: (i, 0))
    vec = pl.BlockSpec((1, D), lambda i: (0, 0))
    return pl.pallas_call(
        body, name=name, grid=(s // tm,),
        in_specs=[pl.BlockSpec((4, tm, FF_BLK), lambda i: (0, i, 0)),
                  _shards(N_DEV, FF_SHARD, D, li),
                  row, _shards(N_DEV, D // N_DEV, D, li),
                  pl.BlockSpec((tm, PLE_DIM), lambda i: (i, 0)),
                  _shards(N_DEV, PLE_DIM, 128, li),
                  row, vec, vec],
        out_specs=[row] * 5,
        out_shape=[jax.ShapeDtypeStruct((s, D), F32), jax.ShapeDtypeStruct((s, D), F32),
                   jax.ShapeDtypeStruct((s, D), BF16), jax.ShapeDtypeStruct((s, D), BF16),
                   jax.ShapeDtypeStruct((s, D), BF16)],
        compiler_params=_cparams(1))(t, wdown, hb, wpg, pb, wpp, h1, g, b)


def loss_bwd(h, target, name):
    s = h.shape[0]
    tm = 512

    def body(h_ref, t_ref, dh_ref, l_ref):
        @pl.when(pl.program_id(0) == 0)
        def _():
            l_ref[...] = jnp.zeros_like(l_ref)
        e = h_ref[...] - t_ref[...]
        dh_ref[...] = e * (1.0 / D)
        l_ref[...] += 0.5 * jnp.sum(jnp.mean(e * e, axis=-1, keepdims=True), axis=0, keepdims=True)

    row = pl.BlockSpec((tm, D), lambda i: (i, 0))
    return pl.pallas_call(
        body, name=name, grid=(s // tm,), in_specs=[row, row],
        out_specs=[row, pl.BlockSpec((1, 1), lambda i: (0, 0))],
        out_shape=[jax.ShapeDtypeStruct((s, D), F32), jax.ShapeDtypeStruct((1, 1), F32)],
        compiler_params=_cparams(1))(h, target)


def ln_bwd(dh, z, g, name, after=()):
    s = dh.shape[0]
    tm = 512
    na = len(after)

    def body(dh_ref, z_ref, g_ref, *rest):
        dz_ref, dg_ref, db_ref = rest[na:]

        @pl.when(pl.program_id(0) == 0)
        def _():
            dg_ref[...] = jnp.zeros_like(dg_ref)
            db_ref[...] = jnp.zeros_like(db_ref)
        dh = dh_ref[...]
        dz, dgx = _ln_bwd(dh, z_ref[...], g_ref[...])
        dz_ref[...] = dz
        dg_ref[...] += _colsum(dgx)
        db_ref[...] += _colsum(dh)

    row = pl.BlockSpec((tm, D), lambda i: (i, 0))
    vec = pl.BlockSpec((1, D), lambda i: (0, 0))
    return pl.pallas_call(
        body, name=name, grid=(s // tm,), in_specs=[row, row, vec] + [ANY] * na, out_specs=[row, vec, vec],
        out_shape=[jax.ShapeDtypeStruct((s, D), F32), jax.ShapeDtypeStruct((1, D), F32),
                   jax.ShapeDtypeStruct((1, D), F32)],
        compiler_params=_cparams(1))(dh, z, g, *after)


def ln2_ple_bwd(dh, z, g, pg, pp, name, after=()):
    s = dh.shape[0]
    tm = 512
    na = len(after)

    def body(dh_ref, z_ref, g_ref, pg_ref, pp_ref, *rest):
        dz_ref, dzb_ref, dpg_ref, dpp_ref, dg_ref, db_ref = rest[na:]

        @pl.when(pl.program_id(0) == 0)
        def _():
            dg_ref[...] = jnp.zeros_like(dg_ref)
            db_ref[...] = jnp.zeros_like(db_ref)
        dh = dh_ref[...]
        dz, dgx = _ln_bwd(dh, z_ref[...], g_ref[...])
        sg = _sigmoid(pg_ref[...].astype(F32))
        dz_ref[...] = dz
        dzb_ref[...] = dz.astype(BF16)
        dpg_ref[...] = (dz * pp_ref[...].astype(F32) * sg * (1.0 - sg)).astype(BF16)
        dpp_ref[...] = (dz * sg).astype(BF16)
        dg_ref[...] += _colsum(dgx)
        db_ref[...] += _colsum(dh)

    row = pl.BlockSpec((tm, D), lambda i: (i, 0))
    vec = pl.BlockSpec((1, D), lambda i: (0, 0))
    return pl.pallas_call(
        body, name=name, grid=(s // tm,), in_specs=[row, row, vec, row, row] + [ANY] * na,
        out_specs=[row, row, row, row, vec, vec],
        out_shape=[jax.ShapeDtypeStruct((s, D), F32)] + [jax.ShapeDtypeStruct((s, D), BF16)] * 3
        + [jax.ShapeDtypeStruct((1, D), F32)] * 2,
        compiler_params=_cparams(1))(dh, z, g, pg, pp, *after)


def wgrad_rows(a, dy, name):
    s, k = a.shape
    n = dy.shape[1]
    kb = k // N_DEV

    def body(a_ref, dy_ref, o_ref):
        o_ref[...] = _tn(a_ref[...], dy_ref[...]).astype(BF16)

    return pl.pallas_call(
        body, name=name, grid=(N_DEV,),
        in_specs=[pl.BlockSpec((s, kb), lambda j: (0, j)), pl.BlockSpec((s, n), lambda j: (0, 0))],
        out_specs=pl.BlockSpec((None, kb, n), lambda j: (j, 0, 0)),
        out_shape=jax.ShapeDtypeStruct((N_DEV, kb, n), BF16),
        compiler_params=_cparams(1))(a, dy)


def wgrad_cols(a, dy, name, with_colsum=False):
    s, k = a.shape
    n = dy.shape[1]
    nb = n // N_DEV

    def body(a_ref, dy_ref, o_ref, *cs_ref):
        dy = dy_ref[...]
        o_ref[...] = _tn(a_ref[...], dy).astype(BF16)
        if with_colsum:
            cs_ref[0][...] = _colsum(dy.astype(F32))

    out_specs = [pl.BlockSpec((None, k, nb), lambda j: (j, 0, 0))]
    out_shape = [jax.ShapeDtypeStruct((N_DEV, k, nb), BF16)]
    if with_colsum:
        out_specs.append(pl.BlockSpec((1, nb), lambda j: (0, j)))
        out_shape.append(jax.ShapeDtypeStruct((1, n), F32))
    res = pl.pallas_call(
        body, name=name, grid=(N_DEV,),
        in_specs=[pl.BlockSpec((s, k), lambda j: (0, 0)), pl.BlockSpec((s, nb), lambda j: (0, j))],
        out_specs=out_specs, out_shape=out_shape,
        compiler_params=_cparams(1))(a, dy)
    return res if with_colsum else res[0]


def wgrad_down(t, dy, name):
    _, s, k = t.shape
    n = dy.shape[1]

    def body(a_ref, dy_ref, o_ref):
        o_ref[...] = _tn(a_ref[...], dy_ref[...]).astype(BF16)

    return pl.pallas_call(
        body, name=name, grid=(4,),
        in_specs=[pl.BlockSpec((None, s, k), lambda j: (j, 0, 0)), pl.BlockSpec((s, n), lambda j: (0, 0))],
        out_specs=pl.BlockSpec((None, k, n), lambda j: (j, 0, 0)),
        out_shape=jax.ShapeDtypeStruct((4, k, n), BF16),
        compiler_params=_cparams(1))(t, dy)


def wgrad_up(a, dhv, dhg, name):
    s, k = a.shape

    def body(a_ref, dv_ref, dg_ref, o_ref):
        j = pl.program_id(0)

        @pl.when(j < 4)
        def _():
            o_ref[...] = _tn(dv_ref[...], a_ref[...]).astype(BF16)

        @pl.when(j >= 4)
        def _():
            o_ref[...] = _tn(dg_ref[...], a_ref[...]).astype(BF16)

    return pl.pallas_call(
        body, name=name, grid=(N_DEV,),
        in_specs=[pl.BlockSpec((s, k), lambda j: (0, 0)),
                  pl.BlockSpec((None, s, FF_BLK), lambda j: (jnp.minimum(j, 3), 0, 0)),
                  pl.BlockSpec((None, s, FF_BLK), lambda j: (jnp.maximum(j - 4, 0), 0, 0))],
        out_specs=pl.BlockSpec((None, FF_BLK, k), lambda j: (j, 0, 0)),
        out_shape=jax.ShapeDtypeStruct((N_DEV, FF_BLK, k), BF16),
        compiler_params=_cparams(1))(a, dhv, dhg)


def ffn_act_bwd(dzb, wdown, up, conv_w, conv_b, li, name):
    s = up.shape[1]
    tm = 512
    nt = s // tm
    before, after = _halo_maps(tm, s)
    hv_main, hv_prev, hv_next = _slab_specs(tm, s, lambda c: c)
    hg_main, hg_prev, hg_next = _slab_specs(tm, s, lambda c: 4 + c)

    def dc_of(dz, wd, hv, hg, back, fwd, cw_ref, cb_ref):
        dt = _nt(dz, wd)
        c = back * cw_ref[0:1, :] + hg * cw_ref[1:2, :] + fwd * cw_ref[2:3, :] + cb_ref[...]
        cdf = 0.5 * (1.0 + lax.erf(c * _SQRT_HALF))
        pdf = jnp.exp(-0.5 * c * c) * _INV_SQRT_2PI
        return dt, c * cdf, dt * hv * (cdf + c * pdf)

    def body(dz_ref, dzp_ref, dzn_ref, wd_ref, hv_ref, hvp_ref, hvn_ref, hg_ref, hgp_ref, hgn_ref, cw_ref, cb_ref,
             dhv_ref, dhg_ref, dcw_ref, dcb_ref):
        i = pl.program_id(1)

        @pl.when(i == 0)
        def _():
            dcw_ref[...] = jnp.zeros_like(dcw_ref)
            dcb_ref[...] = jnp.zeros_like(dcb_ref)

        wd = _row_cat(wd_ref)
        hg = hg_ref[...].astype(F32)
        hgp = hgp_ref[...].astype(F32)
        hgn = hgn_ref[...].astype(F32)
        first, last = i == 0, i == nt - 1
        e = HALO - 1
        back, fwd = _shift_rows(hg, jnp.where(first, 0.0, hgp[e:e + 1, :]), jnp.where(last, 0.0, hgn[0:1, :]))
        dt, act, dc = dc_of(dz_ref[...], wd, hv_ref[...].astype(F32), hg, back, fwd, cw_ref, cb_ref)
        dhv_ref[...] = (dt * act).astype(BF16)
        bp, fp = _shift_rows(hgp, hgp[0:1, :], hg[0:1, :])
        _, _, dcp = dc_of(dzp_ref[...], wd, hvp_ref[...].astype(F32), hgp, bp, fp, cw_ref, cb_ref)
        bn, fn = _shift_rows(hgn, hg[tm - 1:tm, :], hgn[e:e + 1, :])
        _, _, dcn = dc_of(dzn_ref[...], wd, hvn_ref[...].astype(F32), hgn, bn, fn, cw_ref, cb_ref)
        dc_back, dc_fwd = _shift_rows(dc, jnp.where(first, 0.0, dcp[e:e + 1, :]), jnp.where(last, 0.0, dcn[0:1, :]))
        dhg_ref[...] = (dc_fwd * cw_ref[0:1, :] + dc * cw_ref[1:2, :] + dc_back * cw_ref[2:3, :]).astype(BF16)
        dcw_ref[0:1, :] += _colsum(dc * back)
        dcw_ref[1:2, :] += _colsum(dc * hg)
        dcw_ref[2:3, :] += _colsum(dc * fwd)
        dcb_ref[...] += _colsum(dc)

    out_slab = pl.BlockSpec((None, tm, FF_BLK), lambda c, i: (c, i, 0))
    cw_spec = pl.BlockSpec((None, 3, FF_BLK), lambda c, i: (c, 0, 0))
    cb_spec = pl.BlockSpec((None, 1, FF_BLK), lambda c, i: (c, 0, 0))
    return pl.pallas_call(
        body, name=name, grid=(4, nt),
        in_specs=[pl.BlockSpec((tm, D), lambda c, i: (i, 0)),
                  pl.BlockSpec((HALO, D), lambda c, i: (before(i), 0)),
                  pl.BlockSpec((HALO, D), lambda c, i: (after(i), 0)),
                  _shards(2, FF_SHARD, D, li, lambda c, i: c),
                  hv_main, hv_prev, hv_next, hg_main, hg_prev, hg_next, cw_spec, cb_spec],
        out_specs=[out_slab, out_slab, cw_spec, cb_spec],
        out_shape=[jax.ShapeDtypeStruct((4, s, FF_BLK), BF16), jax.ShapeDtypeStruct((4, s, FF_BLK), BF16),
                   jax.ShapeDtypeStruct((4, 3, FF_BLK), F32), jax.ShapeDtypeStruct((4, 1, FF_BLK), F32)],
        compiler_params=_cparams(2))(dzb, dzb, dzb, wdown, up, up, up, up, up, up, conv_w, conv_b)


def dh1_ln1_bwd(dz2, dpg, wpg, dhv, dhg, wup, z1, g1, li, name, after=()):
    s = dz2.shape[0]
    tm = 256
    na = len(after)

    def body(dz2_ref, dpg_ref, wpg_ref, dhv_ref, dhg_ref, wup_ref, z1_ref, g_ref, *rest):
        dz_ref, dzb_ref, dg_ref, db_ref = rest[na:]

        @pl.when(pl.program_id(0) == 0)
        def _():
            dg_ref[...] = jnp.zeros_like(dg_ref)
            db_ref[...] = jnp.zeros_like(db_ref)
        dh = ALPHA * dz2_ref[...] + _nt(dpg_ref[...], _row_cat(wpg_ref))
        for c in range(4):
            dh = dh + _nn(dhv_ref[c], wup_ref[c]) + _nn(dhg_ref[c], wup_ref[4 + c])
        dz, dgx = _ln_bwd(dh, z1_ref[...], g_ref[...])
        dz_ref[...] = dz
        dzb_ref[...] = dz.astype(BF16)
        dg_ref[...] += _colsum(dgx)
        db_ref[...] += _colsum(dh)

    row = pl.BlockSpec((tm, D), lambda i: (i, 0))
    vec = pl.BlockSpec((1, D), lambda i: (0, 0))
    slab = pl.BlockSpec((4, tm, FF_BLK), lambda i: (0, i, 0))
    return pl.pallas_call(
        body, name=name, grid=(s // tm,),
        in_specs=[row, row, _shards(N_DEV, D // N_DEV, D, li), slab, slab, _shards(N_DEV, FF_BLK, D, li), row, vec]
        + [ANY] * na,
        out_specs=[row, row, vec, vec],
        out_shape=[jax.ShapeDtypeStruct((s, D), F32), jax.ShapeDtypeStruct((s, D), BF16),
                   jax.ShapeDtypeStruct((1, D), F32), jax.ShapeDtypeStruct((1, D), F32)],
        compiler_params=_cparams(1))(dz2, dpg, wpg, dhv, dhg, wup, z1, g1, *after)


def merge_bwd(dz1b, wmix, proj, ya, yp, li, name):
    s = dz1b.shape[0]
    tm, tn = 512, 512
    nt = D // tn
    per = tn // (D // N_DEV)
    ga0 = (3 * D_ATTN + D_POOL) // tn

    def body(dz_ref, w_ref, ga_ref, gb_ref, ya_ref, yp_ref, dya_ref, dyp_ref, dga_ref, dgb_ref):
        dm = _nt(dz_ref[...], _row_cat(w_ref))
        sa = _sigmoid(ga_ref[...].astype(F32))
        sb = _sigmoid(gb_ref[...].astype(F32))
        dya_ref[...] = (dm * sa).astype(BF16)
        dyp_ref[...] = (dm * sb).astype(BF16)
        dga_ref[...] = (dm * ya_ref[...].astype(F32) * sa * (1.0 - sa)).astype(BF16)
        dgb_ref[...] = (dm * yp_ref[...].astype(F32) * sb * (1.0 - sb)).astype(BF16)

    tile = pl.BlockSpec((tm, tn), lambda i, j: (i, j))
    return pl.pallas_call(
        body, name=name, grid=(s // tm, nt),
        in_specs=[pl.BlockSpec((tm, D), lambda i, j: (i, 0)),
                  _shards(per, D // N_DEV, D, li, lambda i, j: j),
                  pl.BlockSpec((tm, tn), lambda i, j: (i, ga0 + j)),
                  pl.BlockSpec((tm, tn), lambda i, j: (i, ga0 + nt + j)),
                  tile, tile],
        out_specs=[tile] * 4,
        out_shape=[jax.ShapeDtypeStruct((s, D), BF16)] * 4,
        compiler_params=_cparams(2))(dz1b, wmix, proj, proj, ya, yp)


def attn_out_bwd(dya, wao, li, name, after=()):
    s = dya.shape[0]
    tm = 512

    def body(d_ref, w_ref, *rest):
        rest[-1][...] = _nt(d_ref[...], _lane_cat(w_ref)).astype(BF16)

    return pl.pallas_call(
        body, name=name, grid=(s // tm,),
        in_specs=[pl.BlockSpec((tm, D), lambda i: (i, 0)), _shards(N_DEV, D_ATTN, 128, li)] + [ANY] * len(after),
        out_specs=pl.BlockSpec((tm, D_ATTN), lambda i: (i, 0)),
        out_shape=jax.ShapeDtypeStruct((s, D_ATTN), BF16),
        compiler_params=_cparams(1))(dya, wao, *after)


def pool_bwd(dyp, wpo, pm, pool_w, pool_scale, li, name):
    s = dyp.shape[0]

    def body(dyp_ref, wpo_ref, pm_ref, w_ref, sc_ref, du_ref, dw_ref, dsc_ref):
        wpo = _lane_cat(wpo_ref)
        dyp = dyp_ref[...]
        for g, w in enumerate(POOL_WINDOWS):
            cols = slice(g * PGD, (g + 1) * PGD)
            dpw = _nt(dyp, wpo[g * PGD:(g + 1) * PGD, :])
            pmg = pm_ref[:, cols]
            dsc_ref[:, cols] = _colsum(dpw * _nn(pmg, w_ref[g]))
            dpmw = (dpw * sc_ref[:, cols]).astype(BF16)
            dw_ref[g] = _tn(pmg, dpmw)
            dpm = _nt(dpmw, w_ref[g])
            du_ref[:, cols] = (_window_sum(dpm / _pool_counts(s, w), w, False) - dpm).astype(BF16)

    full = lambda shape: pl.BlockSpec(shape, lambda i: (0,) * len(shape))
    return pl.pallas_call(
        body, name=name, grid=(1,),
        in_specs=[full((s, D)), _shards(N_DEV, D_POOL, 128, li), full((s, D_POOL)), full((4, PGD, PGD)),
                  full((1, D_POOL))],
        out_specs=[full((s, D_POOL)), full((4, PGD, PGD)), full((1, D_POOL))],
        out_shape=[jax.ShapeDtypeStruct((s, D_POOL), BF16), jax.ShapeDtypeStruct((4, PGD, PGD), F32),
                   jax.ShapeDtypeStruct((1, D_POOL), F32)],
        compiler_params=_cparams(1))(dyp, wpo, pm, pool_w, pool_scale)


_SKEW_BASE = KB - (GRID_W - KW) - GRID_W


def attn_bwd(proj, da, e_rev, name):
    s = proj.shape[0]
    n_rows = s // GRID_W
    nb = n_rows // QROWS

    def body(q_ref, k_ref, v_ref, do_ref, e_ref, dq_ref, dk_ref, dv_ref, g_ref,
             s_ref, dp_ref, ds_ref, p_ref, dk_acc, dv_acc):
        b = pl.program_id(1)
        ks, delta, k0 = _attn_block_scalars(b, n_rows)

        @pl.when(b == 0)
        def _():
            dk_acc[...] = jnp.zeros_like(dk_acc)
            dv_acc[...] = jnp.zeros_like(dv_acc)
            g_ref[...] = jnp.zeros_like(g_ref)

        kwin = k_ref[pl.ds(k0, KB), :]
        vwin = v_ref[pl.ds(k0, KB), :]
        ri = lax.broadcasted_iota(I32, (QB, QB), 0)
        ci = lax.broadcasted_iota(I32, (QB, QB), 1)
        rev = jnp.where(ri + ci == QB - 1, 1.0, 0.0).astype(BF16)
        q = _nn(rev, q_ref[...]).astype(BF16) * ATT_SCALE
        do = _nn(rev, do_ref[...]).astype(BF16)
        lane = lax.broadcasted_iota(I32, (1, 128), 1)
        klane = lax.broadcasted_iota(I32, (1, KB), 1)
        dq = jnp.zeros((QB, 128), F32)
        for hh in range(2):
            lm = (lane // HEAD_DIM) == hh
            qh = jnp.where(lm, q, jnp.zeros_like(q))
            doh = jnp.where(lm, do, jnp.zeros_like(do))
            kh = jnp.where(lm, kwin, jnp.zeros_like(kwin))
            s_ref[...] = _nt(qh, kwin)
            dp_ref[...] = _nt(doh, vwin)
            e = e_ref[hh]

            def band(ib, carry):
                r0 = pl.multiple_of(ib * GRID_W, GRID_W)
                rows = pl.ds(r0, GRID_W)
                sb = _attn_band_logits(s_ref[rows, :], e, b, QROWS - 1 - ib, ks, delta, n_rows, klane)
                m = jnp.max(sb, axis=1, keepdims=True)
                p = jnp.exp(sb - m)
                p = p * (1.0 / jnp.sum(p, axis=1, keepdims=True))
                dp = dp_ref[rows, :]
                ds_ref[rows, :] = p * (dp - jnp.sum(p * dp, axis=1, keepdims=True))
                p_ref[rows, :] = p.astype(BF16)
                return carry

            lax.fori_loop(0, QROWS, band, 0, unroll=True)
            ds = ds_ref[...]
            dsb = ds.astype(BF16)
            dq = dq + _nn(dsb, kh) * ATT_SCALE
            dk_acc[pl.ds(k0, KB), :] += _tn(dsb, qh)
            dv_acc[pl.ds(k0, KB), :] += _tn(p_ref[...], doh)
            t = pltpu.roll(ds.reshape(QROWS, GRID_W, KB), _SKEW_BASE, 2, stride=1, stride_axis=1).sum(axis=1)
            g = jnp.zeros((1, KB), F32)
            for ib in range(QROWS):
                shift = ((1 + ib + delta) * GRID_W) % KB
                g = g + pltpu.roll(t[ib:ib + 1, :], shift, 1)
            g_ref[hh] += g
        dq_ref[...] = _nn(rev, dq.astype(BF16)).astype(BF16)

        @pl.when(b == nb - 1)
        def _():
            dk_ref[...] = dk_acc[...].astype(BF16)
            dv_ref[...] = dv_acc[...].astype(BF16)

    col = pl.BlockSpec((s, 128), lambda j, b: (0, j))
    return pl.pallas_call(
        body, name=name, grid=(4, nb),
        in_specs=[pl.BlockSpec((QB, 128), lambda j, b: (b, j)),
                  pl.BlockSpec((s, 128), lambda j, b: (0, 4 + j)),
                  pl.BlockSpec((s, 128), lambda j, b: (0, 8 + j)),
                  pl.BlockSpec((QB, 128), lambda j, b: (b, j)),
                  pl.BlockSpec((2, GRID_W, KB), lambda j, b: (j, 0, 0))],
        out_specs=[pl.BlockSpec((QB, 128), lambda j, b: (b, j)), col, col,
                   pl.BlockSpec((2, 1, KB), lambda j, b: (j, 0, 0))],
        out_shape=[jax.ShapeDtypeStruct((s, D_ATTN), BF16)] * 3 + [jax.ShapeDtypeStruct((N_HEADS, 1, KB), F32)],
        scratch_shapes=[pltpu.VMEM((QB, KB), F32), pltpu.VMEM((QB, KB), F32), pltpu.VMEM((QB, KB), F32),
                        pltpu.VMEM((QB, KB), BF16), pltpu.VMEM((s, 128), F32), pltpu.VMEM((s, 128), F32)],
        compiler_params=_cparams(2))(proj, proj, proj, da, e_rev)


def dh0_bwd(dz1, dproj, win, li, name):
    s = dz1.shape[0]
    tm = 256
    bn = N_PROJ // N_DEV

    def body(dz_ref, dp_ref, w_ref, o_ref):
        acc = ALPHA * dz_ref[...]
        for j in range(N_DEV):
            acc = acc + _nt(dp_ref[:, j * bn:(j + 1) * bn], w_ref[j])
        o_ref[...] = acc

    row = pl.BlockSpec((tm, D), lambda i: (i, 0))
    return pl.pallas_call(
        body, name=name, grid=(s // tm,),
        in_specs=[row, pl.BlockSpec((tm, N_PROJ), lambda i: (i, 0)), _shards(N_DEV, D, bn, li)],
        out_specs=row, out_shape=jax.ShapeDtypeStruct((s, D), F32),
        compiler_params=_cparams(1))(dz1, dproj, win)


def _coords():
    return lax.axis_index("x"), lax.axis_index("y"), lax.axis_index("c")


def _dev_index(px, py, pc):
    return 4 * px + 2 * py + pc


def all_gather(arrs, name):
    n = len(arrs)

    def body(*refs):
        ins, outs = refs[:n], refs[n:2 * n]
        send_sems, recv_sems, local_sems = refs[2 * n:]
        x, y, c = _coords()
        me, sibling = (x, y, c), (x, y, 1 - c)
        chips = [(1 - x, y), (x, 1 - y), (1 - x, 1 - y)]

        def copy(a, k, block, to, src=None):
            dst = outs[a].at[_dev_index(*block)]
            return pltpu.make_async_remote_copy(
                src_ref=dst if src is None else src, dst_ref=dst,
                send_sem=send_sems.at[a, k], recv_sem=recv_sems.at[a, k], device_id=to, device_id_type=MESH)

        mine = [pltpu.make_async_copy(ins[a], outs[a].at[_dev_index(*me)], local_sems.at[a]) for a in range(n)]
        for cp in mine:
            cp.start()
        first = []
        for a in range(n):
            first.append(copy(a, 0, me, sibling, src=ins[a]))
            first += [copy(a, 1 + j, me, (*chip, c), src=ins[a]) for j, chip in enumerate(chips)]
        for cp in first:
            cp.start()
        passed = []
        for j, chip in enumerate(chips):
            for a in range(n):
                copy(a, 1 + j, (*chip, c), me).wait_recv()
                cp = copy(a, 4 + j, (*chip, c), sibling)
                cp.start()
                passed.append(cp)
        for a in range(n):
            copy(a, 0, sibling, me).wait_recv()
            for j, chip in enumerate(chips):
                copy(a, 4 + j, (*chip, 1 - c), me).wait_recv()
        for cp in first + passed:
            cp.wait_send()
        for cp in mine:
            cp.wait()

    return pl.pallas_call(
        body, name=name,
        out_shape=[jax.ShapeDtypeStruct((N_DEV,) + a.shape, a.dtype) for a in arrs],
        in_specs=[ANY] * n, out_specs=[ANY] * n,
        scratch_shapes=[pltpu.SemaphoreType.DMA((n, 7)), pltpu.SemaphoreType.DMA((n, 7)),
                        pltpu.SemaphoreType.DMA((n,))],
    )(*arrs)


def exchange_sibling(grads, name):
    n = len(grads)

    def body(*refs):
        ins, outs = refs[:n], refs[n:2 * n]
        send_sems, recv_sems = refs[2 * n:]
        x, y, c = _coords()
        sibling = (x, y, 1 - c)
        copies = []
        for a in range(n):
            for k in range(4):
                blk = _dev_index(x ^ (k & 1), y ^ (k >> 1), 1 - c)
                copies.append(pltpu.make_async_remote_copy(
                    src_ref=ins[a].at[blk], dst_ref=outs[a].at[k],
                    send_sem=send_sems.at[a, k], recv_sem=recv_sems.at[a, k], device_id=sibling, device_id_type=MESH))
        for cp in copies:
            cp.start()
        for cp in copies:
            cp.wait()

    return pl.pallas_call(
        body, name=name,
        out_shape=[jax.ShapeDtypeStruct((4,) + g.shape[1:], g.dtype) for g in grads],
        in_specs=[ANY] * n, out_specs=[ANY] * n,
        scratch_shapes=[pltpu.SemaphoreType.DMA((n, 4)), pltpu.SemaphoreType.DMA((n, 4))],
    )(*grads)


HBM = pl.BlockSpec(memory_space=pltpu.HBM)
SEM = pl.BlockSpec(memory_space=pltpu.SEMAPHORE)
_EFFECT = pltpu.SideEffectType.DATAFLOW_SIDE_EFFECTING
_TOKEN = jax.ShapeDtypeStruct((8, 128), F32)


def _in_hbm(a):
    return pltpu.with_memory_space_constraint(a, pltpu.HBM)


def _hbm_like(a):
    return pltpu.HBM(a.shape, a.dtype)


def _peers(x, y, c):
    return [(x, y, 1 - c), (1 - x, y, c), (x, 1 - y, c), (1 - x, 1 - y, c)]


def ag_start(lands, after, name):
    n = len(lands)

    def body(*refs):
        land = refs[:n]
        send_sem, recv_sem, token = refs[n + 1], refs[n + 2], refs[-1]
        x, y, c = _coords()
        me = _dev_index(x, y, c)
        for k, peer in enumerate(_peers(x, y, c)):
            for a in range(n):
                pltpu.make_async_remote_copy(src_ref=land[a].at[me], dst_ref=land[a].at[me], send_sem=send_sem.at[k],
                                             recv_sem=recv_sem.at[k], device_id=peer, device_id_type=MESH).start()
        token[...] = jnp.zeros_like(token)

    res = pl.pallas_call(
        body, name=name,
        out_shape=(pltpu.SemaphoreType.DMA((4,)), pltpu.SemaphoreType.DMA((4,)), *[_hbm_like(l) for l in lands], _TOKEN),
        in_specs=[HBM] * n + [ANY], out_specs=(SEM, SEM, *[HBM] * n, pl.BlockSpec(memory_space=pltpu.VMEM)),
        input_output_aliases={a: 2 + a for a in range(n)},
        compiler_params=pltpu.CompilerParams(has_side_effects=_EFFECT),
    )(*[_in_hbm(l) for l in lands], after)
    return res[0], res[1], list(res[2:2 + n]), res[-1]


def ag_forward(send_sem, recv_sem, lands, after, name):
    n = len(lands)

    def body(*refs):
        send_sem, recv_sem = refs[0], refs[1]
        land = refs[2:2 + n]
        fsend, frecv = refs[3 + n], refs[4 + n]
        x, y, c = _coords()
        peers = _peers(x, y, c)
        for k in range(1, 4):
            blk = _dev_index(*peers[k])
            for a in range(n):
                pltpu.make_async_remote_copy(src_ref=land[a].at[blk], dst_ref=land[a].at[blk], send_sem=send_sem.at[k],
                                             recv_sem=recv_sem.at[k], device_id=peers[k], device_id_type=MESH).wait_recv()
        for k in range(1, 4):
            blk = _dev_index(*peers[k])
            for a in range(n):
                pltpu.make_async_remote_copy(src_ref=land[a].at[blk], dst_ref=land[a].at[blk], send_sem=fsend.at[k - 1],
                                             recv_sem=frecv.at[k - 1], device_id=peers[0], device_id_type=MESH).start()

    res = pl.pallas_call(
        body, name=name,
        out_shape=(pltpu.SemaphoreType.DMA((3,)), pltpu.SemaphoreType.DMA((3,)), *[_hbm_like(l) for l in lands]),
        in_specs=[SEM, SEM, *[HBM] * n, ANY], out_specs=(SEM, SEM, *[HBM] * n),
        input_output_aliases={2 + a: 2 + a for a in range(n)},
        compiler_params=pltpu.CompilerParams(has_side_effects=_EFFECT),
    )(send_sem, recv_sem, *lands, after)
    return res[0], res[1], list(res[2:])


def ag_finish(send_sem, recv_sem, fsend, frecv, lands, after, name):
    n = len(lands)

    def body(*refs):
        send_sem, recv_sem, fsend, frecv = refs[:4]
        land = refs[4:4 + n]
        x, y, c = _coords()
        me = _dev_index(x, y, c)
        peers = _peers(x, y, c)
        for k in range(4):
            for a in range(n):
                pltpu.make_async_remote_copy(src_ref=land[a].at[me], dst_ref=land[a].at[me], send_sem=send_sem.at[k],
                                             recv_sem=recv_sem.at[k], device_id=peers[k], device_id_type=MESH).wait_send()
        sib = _dev_index(*peers[0])
        for a in range(n):
            pltpu.make_async_remote_copy(src_ref=land[a].at[sib], dst_ref=land[a].at[sib], send_sem=send_sem.at[0],
                                         recv_sem=recv_sem.at[0], device_id=peers[0], device_id_type=MESH).wait_recv()
        for k in range(1, 4):
            mine = _dev_index(*peers[k])
            theirs = _dev_index(peers[k][0], peers[k][1], 1 - c)
            for a in range(n):
                pltpu.make_async_remote_copy(src_ref=land[a].at[mine], dst_ref=land[a].at[theirs], send_sem=fsend.at[k - 1],
                                             recv_sem=frecv.at[k - 1], device_id=peers[0], device_id_type=MESH).wait()

    res = pl.pallas_call(
        body, name=name, out_shape=tuple(_hbm_like(l) for l in lands),
        in_specs=[SEM] * 4 + [HBM] * n + [ANY], out_specs=tuple([HBM] * n),
        input_output_aliases={4 + a: a for a in range(n)},
        compiler_params=pltpu.CompilerParams(has_side_effects=_EFFECT),
    )(send_sem, recv_sem, fsend, frecv, *lands, after)
    return list(res)


def rs_start(psums, name):
    n = len(psums)
    lands = [lax.empty(p.shape, p.dtype) for p in psums]

    def body(*refs):
        src, land = refs[:n], refs[n:2 * n]
        send_sem, recv_sem, token = refs[2 * n], refs[2 * n + 1], refs[-1]
        peers = _peers(*_coords())
        for k in range(3):
            for a in range(n):
                pltpu.make_async_remote_copy(src_ref=src[a].at[k], dst_ref=land[a].at[k], send_sem=send_sem.at[k],
                                             recv_sem=recv_sem.at[k], device_id=peers[k + 1], device_id_type=MESH).start()
        token[...] = jnp.zeros_like(token)

    res = pl.pallas_call(
        body, name=name,
        out_shape=(pltpu.SemaphoreType.DMA((3,)), pltpu.SemaphoreType.DMA((3,)), *[_hbm_like(p) for p in psums],
                   *[_hbm_like(l) for l in lands], _TOKEN),
        in_specs=[HBM] * (2 * n), out_specs=(SEM, SEM, *[HBM] * (2 * n), pl.BlockSpec(memory_space=pltpu.VMEM)),
        input_output_aliases={a: 2 + a for a in range(2 * n)},
        compiler_params=pltpu.CompilerParams(has_side_effects=_EFFECT),
    )(*[_in_hbm(p) for p in psums], *[_in_hbm(l) for l in lands])
    return res[0], res[1], list(res[2:2 + n]), list(res[2 + n:2 + 2 * n]), res[-1]


def rs_finish(send_sem, recv_sem, psums, lands, after, name):
    n = len(psums)

    def body(*refs):
        send_sem, recv_sem = refs[0], refs[1]
        src, land = refs[2:2 + n], refs[2 + n:2 + 2 * n]
        peers = _peers(*_coords())
        for k in range(3):
            for a in range(n):
                pltpu.make_async_remote_copy(src_ref=src[a].at[k], dst_ref=land[a].at[k], send_sem=send_sem.at[k],
                                             recv_sem=recv_sem.at[k], device_id=peers[k + 1], device_id_type=MESH).wait()

    res = pl.pallas_call(
        body, name=name, out_shape=tuple(_hbm_like(l) for l in lands),
        in_specs=[SEM, SEM] + [HBM] * (2 * n) + [ANY], out_specs=tuple([HBM] * n),
        input_output_aliases={2 + n + a: a for a in range(n)},
        compiler_params=pltpu.CompilerParams(has_side_effects=_EFFECT),
    )(send_sem, recv_sem, *psums, *lands, after)
    return list(res)


def _row_tile(r):
    return next(t for t in (512, 352, 256, 128) if r % t == 0)


def pair_add(blk_idx, g, recv, name):
    _, r, c = g.shape
    tr = _row_tile(r)

    def body(idx_ref, g_ref, r_ref, own_ref, oth_ref):
        k = pl.program_id(1)
        sm = g_ref[...].astype(F32) + r_ref[...].astype(F32)

        @pl.when(k == 0)
        def _():
            own_ref[...] = sm

        @pl.when(k > 0)
        def _():
            oth_ref[...] = sm.astype(BF16)

    grid_spec = pltpu.PrefetchScalarGridSpec(
        num_scalar_prefetch=1, grid=(r // tr, 4),
        in_specs=[pl.BlockSpec((None, tr, c), lambda t, k, idx: (idx[k], t, 0)),
                  pl.BlockSpec((None, tr, c), lambda t, k, idx: (k, t, 0))],
        out_specs=[pl.BlockSpec((tr, c), lambda t, k, idx: (t, 0)),
                   pl.BlockSpec((None, tr, c), lambda t, k, idx: (jnp.maximum(k - 1, 0), t, 0))])
    return pl.pallas_call(
        body, name=name, grid_spec=grid_spec,
        out_shape=[jax.ShapeDtypeStruct((r, c), F32), jax.ShapeDtypeStruct((3, r, c), BF16)],
        compiler_params=_cparams(2))(blk_idx, g, recv)


def _adamw(w, g, m, v):
    m = ADAM_B1 * m + (1.0 - ADAM_B1) * g
    v = ADAM_B2 * v + (1.0 - ADAM_B2) * (g * g)
    m_hat = m / (1.0 - ADAM_B1 ** ADAM_STEP)
    v_hat = v / (1.0 - ADAM_B2 ** ADAM_STEP)
    delta = -ADAM_LR * (m_hat / (jnp.sqrt(v_hat) + ADAM_EPS) + ADAM_WD * w)
    return delta, m, v


def adamw_shard(own, recv, w, m, v, li, prev, name):
    r, c = own.shape
    tr = _row_tile(r)

    def body(own_ref, recv_ref, w_ref, m_ref, v_ref, p0, p1, p2, p3, g_ref, d_ref, nm_ref, nv_ref):
        g = own_ref[...] + recv_ref[0].astype(F32) + recv_ref[1].astype(F32) + recv_ref[2].astype(F32)
        delta, nm, nv = _adamw(w_ref[...], g, m_ref[...], v_ref[...])
        g_ref[...] = g
        d_ref[...] = delta
        nm_ref[...] = nm
        nv_ref[...] = nv

    lay = pl.BlockSpec((None, tr, c), lambda t: (li, t, 0))
    stack = jax.ShapeDtypeStruct((DEPTH, r, c), F32)
    return pl.pallas_call(
        body, name=name, grid=(r // tr,),
        in_specs=[pl.BlockSpec((tr, c), lambda t: (t, 0)), pl.BlockSpec((3, tr, c), lambda t: (0, t, 0)),
                  lay, lay, lay, ANY, ANY, ANY, ANY],
        out_specs=[lay] * 4, out_shape=[stack] * 4,
        input_output_aliases={5: 0, 6: 1, 7: 2, 8: 3},
        compiler_params=_cparams(1))(own, recv, w, m, v, *prev)


def adamw_replicated(gathered, w, m, v, name):
    _, r, c = gathered.shape
    tr = 88

    def body(gs_ref, w_ref, m_ref, v_ref, g_ref, d_ref, nm_ref, nv_ref):
        g = gs_ref[0]
        for d in range(1, N_DEV):
            g = g + gs_ref[d]
        delta, nm, nv = _adamw(w_ref[...], g, m_ref[...], v_ref[...])
        g_ref[...] = g
        d_ref[...] = delta
        nm_ref[...] = nm
        nv_ref[...] = nv

    row = pl.BlockSpec((tr, c), lambda t: (t, 0))
    return pl.pallas_call(
        body, name=name, grid=(r // tr,),
        in_specs=[pl.BlockSpec((N_DEV, tr, c), lambda t: (0, t, 0)), row, row, row],
        out_specs=[row] * 4, out_shape=[jax.ShapeDtypeStruct((r, c), F32)] * 4,
        compiler_params=_cparams(1))(gathered, w, m, v)


def adamw_plain(g, w, m, v, name):
    def body(g_ref, w_ref, m_ref, v_ref, d_ref, nm_ref, nv_ref):
        delta, nm, nv = _adamw(w_ref[...], g_ref[...], m_ref[...], v_ref[...])
        d_ref[...] = delta
        nm_ref[...] = nm
        nv_ref[...] = nv

    return pl.pallas_call(body, name=name, out_shape=[jax.ShapeDtypeStruct(w.shape, F32)] * 3)(g, w, m, v)


_PACK = (("ln_in_g", (D,)), ("ln_in_b", (D,)), ("b_in", (DEPTH, N_PROJ)), ("rpb", (DEPTH, N_HEADS, 2 * KH - 1, 2 * KW - 1)),
         ("pool_w", (DEPTH, 4, PGD, PGD)), ("pool_scale", (DEPTH, D_POOL)), ("ln1_g", (DEPTH, D)), ("ln1_b", (DEPTH, D)),
         ("conv_b", (DEPTH, D_FF)), ("ln2_g", (DEPTH, D)), ("ln2_b", (DEPTH, D)), ("conv_w", (DEPTH, 3, D_FF)))
_PACK_LANES = 1024


def _pack_rows(shape):
    return -(-int(np.prod(shape)) // _PACK_LANES)


_PACK_ROWS = -(-sum(_pack_rows(s) for _, s in _PACK) // 88) * 88


def _pack(parts):
    rows = []
    for name, shape in _PACK:
        flat = parts[name].reshape(-1).astype(F32)
        rows.append(jnp.pad(flat, (0, _pack_rows(shape) * _PACK_LANES - flat.shape[0])))
    used = sum(_pack_rows(s) for _, s in _PACK)
    rows.append(jnp.zeros(((_PACK_ROWS - used) * _PACK_LANES,), F32))
    return jnp.concatenate(rows).reshape(_PACK_ROWS, _PACK_LANES)


def _unpack(packed):
    out, r0 = {}, 0
    for name, shape in _PACK:
        n = int(np.prod(shape))
        nr = _pack_rows(shape)
        out[name] = packed[r0:r0 + nr].reshape(-1)[:n].reshape(shape)
        r0 += nr
    return out


def _bias_tables(rpb_l):
    qc = np.arange(GRID_W)[:, None]
    kc = np.arange(GRID_W)[None, :]
    start = np.clip(qc - KW // 2, 0, GRID_W - KW)
    valid = (kc >= start) & (kc < start + KW)
    col = np.clip(kc - qc, -(KW - 1), KW - 1) + KW - 1
    onehot = (col.reshape(-1)[None, :] == np.arange(2 * KW - 1)[:, None]).astype(np.float32)
    rows = jnp.pad(rpb_l, ((0, 0), (0, 1), (0, 0)))
    tab = jnp.einsum("hij,jm->him", rows, jnp.asarray(onehot), precision=lax.Precision.HIGHEST)
    tab = tab.reshape(N_HEADS, KROWS, GRID_W, GRID_W).transpose(0, 2, 1, 3)
    ok = valid[None, :, None, :] & (np.arange(KROWS) < 2 * KH - 1)[None, None, :, None]
    tab = jnp.where(jnp.asarray(ok), tab, NEG_INF).reshape(N_HEADS, GRID_W, KB)
    return tab, tab[:, ::-1, :]


_SHARDED = ("w_in", "w_attn_out", "w_pool_out", "w_mix_out", "w_up", "w_down", "w_ple_gate", "w_ple_proj")
_NAMES = ("ln_in_g", "ln_in_b", "w_in", "b_in", "rpb", "w_attn_out", "pool_w", "pool_scale", "w_pool_out", "w_mix_out",
          "ln1_g", "ln1_b", "w_up", "conv_w", "conv_b", "w_down", "w_ple_gate", "w_ple_proj", "ln2_g", "ln2_b")


def kernel(x, p, ln_in_g, ln_in_b, w_in, b_in, rpb, w_attn_out, pool_w, pool_scale, w_pool_out, w_mix_out, ln1_g, ln1_b, w_up, conv_w, conv_b, w_down, w_ple_gate, w_ple_proj, ln2_g, ln2_b, loss_target, m_ln_in_g, m_ln_in_b, m_w_in, m_b_in, m_rpb, m_w_attn_out, m_pool_w, m_pool_scale, m_w_pool_out, m_w_mix_out, m_ln1_g, m_ln1_b, m_w_up, m_conv_w, m_conv_b, m_w_down, m_w_ple_gate, m_w_ple_proj, m_ln2_g, m_ln2_b, v_ln_in_g, v_ln_in_b, v_w_in, v_b_in, v_rpb, v_w_attn_out, v_pool_w, v_pool_scale, v_w_pool_out, v_w_mix_out, v_ln1_g, v_ln1_b, v_w_up, v_conv_w, v_conv_b, v_w_down, v_w_ple_gate, v_w_ple_proj, v_ln2_g, v_ln2_b):
    a = dict(locals())
    W = {n: a[n] for n in _NAMES}
    M = {n: a["m_" + n] for n in _NAMES}
    V = {n: a["v_" + n] for n in _NAMES}
    xi, yi, ci = _coords()
    me = _dev_index(xi, yi, ci)
    x2, tgt = x[0], loss_target[0]
    pb = p[:, 0].astype(BF16)

    flip = lambda d: {**d, "w_up": d["w_up"].transpose(0, 2, 1)}
    ex = _Exchange(flip(W), flip(M), flip(V))
    loss_part, dx, parts = _local_step(x2, tgt, pb, W, ex)
    loss = lax.psum(loss_part[0, 0], AXES)
    stacks = {**ex.stacks, "w_up": [t.transpose(0, 2, 1) for t in ex.stacks["w_up"]]}

    (gath,) = all_gather([_pack(parts)], "ag_small_grads")
    zero_cw = jnp.zeros((DEPTH, 3, D_FF), F32)
    packs = [_pack({**{n: src[n] for n, _ in _PACK if n != "conv_w"}, "conv_w": zero_cw}) for src in (W, M, V)]
    outs = [_unpack(o) for o in adamw_replicated(gath, *packs, "adamw_replicated")]
    g_cw = lax.dynamic_slice_in_dim(outs[0]["conv_w"], me * FF_SHARD, FF_SHARD, axis=2)
    flat = lambda t: t.reshape(DEPTH * 3, FF_SHARD)
    cw_out = [o.reshape(DEPTH, 3, FF_SHARD) for o in
              adamw_plain(flat(g_cw), flat(conv_w), flat(m_conv_w), flat(v_conv_w), "adamw_conv_w")]
    res = []
    for k in range(4):
        d = {n: stacks[n][k] for n in _SHARDED}
        d.update({n: outs[k][n] for n, _ in _PACK if n != "conv_w"})
        d["conv_w"] = g_cw if k == 0 else cw_out[k - 1]
        res.append(d)
    return (loss, dx[None], *[res[k][n] for k in range(4) for n in _NAMES])


class _Exchange:
    GROUPS = (("w_ple_gate", "w_ple_proj", "w_down", "w_up"), ("w_mix_out", "w_attn_out", "w_pool_out"), ("w_in",))
    FIRST = ("w_in",)

    def __init__(self, W, M, V):
        self.W, self.M, self.V = W, M, V
        xi, yi, ci = _coords()
        me = _dev_index(xi, yi, ci)
        self.rel_idx = jnp.stack([_dev_index(xi ^ (k & 1), yi ^ (k >> 1), ci) for k in range(4)]).astype(I32)
        self.lands = [{n: lax.dynamic_update_index_in_dim(lax.empty((N_DEV,) + W[n].shape[1:], BF16),
                                                          W[n][li].astype(BF16), me, 0) for n in _SHARDED}
                      for li in range(DEPTH)]
        (cw,) = all_gather([W["conv_w"]], "ag_conv_w")
        self.cw = cw.transpose(1, 2, 0, 3).reshape(DEPTH, 3, 4, FF_BLK).transpose(0, 2, 1, 3)
        self.ag, self.fwd, self.rs, self.pending = {}, {}, {}, {}
        self.stacks = {n: [lax.empty((DEPTH,) + W[n].shape[1:], F32) for _ in range(4)] for n in _SHARDED}
        self.late = tuple(n for n in _SHARDED if n not in self.FIRST)
        self.ag[0] = ag_start([self.lands[0][n] for n in self.FIRST], cw, "ag_start0")

    def tokens(self):
        return [self.ag[0][3]]

    def prefetch(self, li, after):
        send, recv, lands, _ = self.ag[li]
        self.fwd[li] = ag_forward(send, recv, lands, after, f"ag_forward{li}")
        if li == 0:
            self.ag["0b"] = ag_start([self.lands[0][n] for n in self.late], self.fwd[0][2][0], "ag_start0b")

    def weights(self, li, after):
        send, recv, _, _ = self.ag.pop(li)
        fsend, frecv, lands = self.fwd.pop(li)
        lands = ag_finish(send, recv, fsend, frecv, lands, after, f"ag_finish{li}")
        if li == 0:
            return dict(zip(self.FIRST, lands)), self.cw[li], (self.ag["0b"][3],)
        tokens = ()
        if li + 1 < DEPTH:
            self.ag[li + 1] = ag_start([self.lands[li + 1][n] for n in _SHARDED], lands[0], f"ag_start{li + 1}")
            tokens = (self.ag[li + 1][3],)
        return dict(zip(_SHARDED, lands)), self.cw[li], tokens

    def rest(self, li, G, mid, after):
        if li != 0:
            return G, ()
        send, recv, lands, _ = self.ag.pop("0b")
        fsend, frecv, lands = ag_forward(send, recv, lands, mid, "ag_forward0b")
        lands = ag_finish(send, recv, fsend, frecv, lands, after, "ag_finish0b")
        self.ag[1] = ag_start([self.lands[1][n] for n in _SHARDED], lands[0], "ag_start1")
        return {**G, **dict(zip(self.late, lands))}, (self.ag[1][3],)

    def grads(self, li, group, gw):
        self.pending.setdefault(li, {}).update(gw)
        if li != 0 and group != len(self.GROUPS) - 1:
            return None
        gw = self.pending.pop(li)
        names = tuple(gw)
        tag = f"{li}_{group}" if li == 0 else f"{li}"
        glist = [gw[n] for n in names]
        recv1 = exchange_sibling(glist, f"rs_d2d{tag}")
        sums = [pair_add(self.rel_idx, g, r1, f"pair_add_{n}{li}") for n, g, r1 in zip(names, glist, recv1)]
        send, recv, psums, lands, token = rs_start([s_[1] for s_ in sums], f"rs_start{tag}")
        self.rs.setdefault(li, []).append((tag, names, send, recv, psums, lands, [s_[0] for s_ in sums]))
        return token

    def update(self, li, after):
        for tag, names, send, recv, psums, lands, owns in self.rs.pop(li):
            recv2 = rs_finish(send, recv, psums, lands, after, f"rs_finish{tag}")
            for n, own, r2 in zip(names, owns, recv2):
                self.stacks[n] = adamw_shard(own, r2, self.W[n], self.M[n], self.V[n], li, self.stacks[n],
                                             f"adamw_{n}{li}")


def _local_step(x2, tgt, pb, W, ex):
    depth = W["rpb"].shape[0]
    vec = lambda t: t.reshape(1, -1)
    ln1_g, ln1_b, ln2_g, ln2_b = W["ln1_g"], W["ln1_b"], W["ln2_g"], W["ln2_b"]
    b_in, rpb, pool_scale = W["b_in"], W["rpb"], W["pool_scale"]
    cb_full = W["conv_b"].reshape(depth, 4, 1, FF_BLK)
    pool_w_b = W["pool_w"].astype(BF16)

    h, hb = ln_fwd(x2, vec(W["ln_in_g"]), vec(W["ln_in_b"]), "ln_in", after=ex.tokens())
    ex.prefetch(0, hb)
    saved = []
    for li in range(depth):
        G, cw, tokens = ex.weights(li, hb)
        e_tab, e_rev = _bias_tables(rpb[li])
        bias = vec(b_in[li])
        proj = proj_fwd(hb, G["w_in"], bias, li, 0, N_DEV, BF16, f"proj{li}", after=tokens)
        u = proj_fwd(hb, G["w_in"], bias, li, 3, 1, F32, f"proj_u{li}")
        att = attn_fwd(proj, e_tab, f"attn{li}")
        pm, pw = pool_fwd(u, pool_w_b[li], vec(pool_scale[li]), f"pool{li}")
        G, tokens = ex.rest(li, G, att, pw)
        mg, ya, yp = merge_fwd(att, pw, G["w_attn_out"], G["w_pool_out"], proj, li, f"merge{li}", after=tokens)
        if li + 1 < depth:
            ex.prefetch(li + 1, mg)
        z1, h1, h1b = mix_ln_fwd(mg, G["w_mix_out"], h, vec(ln1_g[li]), vec(ln1_b[li]), li, f"mix_ln{li}")
        up = up_fwd(h1b, G["w_up"], li, f"up{li}")
        t = ffn_act_fwd(up, cw, cb_full[li], f"ffn_act{li}")
        z2, h2, h2b, pg, pp = down_ple_ln_fwd(t, G["w_down"], h1b, G["w_ple_gate"], pb[li], G["w_ple_proj"], h1,
                                              vec(ln2_g[li]), vec(ln2_b[li]), li, f"down_ln{li}")
        saved.append(dict(hb=hb, proj=proj, att=att, pm=pm, pw=pw, mg=mg, ya=ya, yp=yp, z1=z1, h1b=h1b, up=up, t=t,
                          z2=z2, pg=pg, pp=pp, e_rev=e_rev, G=G, cw=cw))
        h, hb = h2, h2b

    dh, loss_part = loss_bwd(h, tgt, "loss")
    small = {n: [None] * depth for n in ("b_in", "rpb", "pool_w", "pool_scale", "ln1_g", "ln1_b", "conv_b", "ln2_g",
                                         "ln2_b", "conv_w")}
    token = ()
    tok = lambda t: () if t is None else (t,)
    for li in reversed(range(depth)):
        sv = saved[li]
        G, cw = sv["G"], sv["cw"]
        dz2, dz2b, dpg, dpp, dg2, db2 = ln2_ple_bwd(dh, sv["z2"], vec(ln2_g[li]), sv["pg"], sv["pp"], f"ln2_bwd{li}",
                                                    after=token)
        gw = {}
        gw["w_ple_gate"] = wgrad_rows(sv["h1b"], dpg, f"dw_pg{li}")
        gw["w_ple_proj"] = wgrad_cols(pb[li], dpp, f"dw_pp{li}")
        gw["w_down"] = wgrad_down(sv["t"], dz2b, f"dw_down{li}").reshape(N_DEV, FF_SHARD, D)
        dhv, dhg, dcw, dcb = ffn_act_bwd(dz2b, G["w_down"], sv["up"], cw, cb_full[li], li, f"ffn_bwd{li}")
        gw["w_up"] = wgrad_up(sv["h1b"], dhv, dhg, f"dw_up{li}")
        token = tok(ex.grads(li, 0, gw))
        dz1, dz1b, dg1, db1 = dh1_ln1_bwd(dz2, dpg, G["w_ple_gate"], dhv, dhg, G["w_up"], sv["z1"], vec(ln1_g[li]), li,
                                          f"ln1_bwd{li}", after=token)
        gw = {"w_mix_out": wgrad_rows(sv["mg"], dz1b, f"dw_mix{li}")}
        dya, dyp, dga, dgb = merge_bwd(dz1b, G["w_mix_out"], sv["proj"], sv["ya"], sv["yp"], li, f"merge_bwd{li}")
        gw["w_attn_out"] = wgrad_cols(sv["att"], dya, f"dw_ao{li}")
        gw["w_pool_out"] = wgrad_cols(sv["pw"], dyp, f"dw_po{li}")
        token = tok(ex.grads(li, 1, gw))
        da = attn_out_bwd(dya, G["w_attn_out"], li, f"da{li}", after=token)
        du, dpool_w, dpool_sc = pool_bwd(dyp, G["w_pool_out"], sv["pm"], pool_w_b[li], vec(pool_scale[li]), li,
                                         f"pool_bwd{li}")
        dq, dk, dv, drpb = attn_bwd(sv["proj"], da, sv["e_rev"], f"attn_bwd{li}")
        dproj = jnp.concatenate([dq, dk, dv, du, dga, dgb], axis=1)
        dw_in, db_in = wgrad_cols(sv["hb"], dproj, f"dw_in{li}", with_colsum=True)
        dh = dh0_bwd(dz1, dproj, G["w_in"], li, f"dh0{li}")
        small["b_in"][li] = db_in.reshape(N_PROJ)
        small["rpb"][li] = drpb.reshape(N_HEADS, KROWS, GRID_W)[:, :2 * KH - 1, :2 * KW - 1]
        small["pool_w"][li] = dpool_w
        small["pool_scale"][li] = dpool_sc.reshape(D_POOL)
        small["ln1_g"][li], small["ln1_b"][li] = dg1.reshape(D), db1.reshape(D)
        small["ln2_g"][li], small["ln2_b"][li] = dg2.reshape(D), db2.reshape(D)
        small["conv_b"][li] = dcb.reshape(D_FF)
        small["conv_w"][li] = dcw.transpose(1, 0, 2).reshape(3, D_FF)
        token = tok(ex.grads(li, 2, {"w_in": dw_in}))
        if li + 1 < depth:
            ex.update(li + 1, dh)
    dx, dg_in, db_in0 = ln_bwd(dh, x2, vec(W["ln_in_g"]), "ln_in_bwd", after=token)
    ex.update(0, dx)
    parts = {n: jnp.stack(v_) for n, v_ in small.items()}
    parts["ln_in_g"], parts["ln_in_b"] = dg_in.reshape(D), db_in0.reshape(D)
    return loss_part, dx, parts
```

```python
import numpy as np
import jax
import jax.numpy as jnp
from jax import lax
from jax.experimental import pallas as pl
from jax.experimental.pallas import tpu as pltpu

F32 = jnp.float32
BF16 = jnp.bfloat16
I32 = jnp.int32

D = 1024
DEPTH = 4
GRID_W = 64
N_HEADS = 8
HEAD_DIM = 64
D_ATTN = 512
KH = 8
KW = 16
POOL_WINDOWS = (2, 4, 8, 16)
D_POOL = 512
PGD = 128
D_FF = 2816
PLE_DIM = 256
N_PROJ = 4096
ALPHA = (2 * DEPTH) ** 0.25
LN_EPS = 1e-5
NEG_INF = -1e30
ATT_SCALE = HEAD_DIM ** -0.5
ADAM_LR = 0.001
ADAM_B1 = 0.9
ADAM_B2 = 0.999
ADAM_EPS = 1e-08
ADAM_WD = 0.01
ADAM_STEP = 10

N_DEV = 8
AXES = ("x", "y", "c")
FF_BLK = D_FF // 4
FF_SHARD = D_FF // N_DEV
QROWS = 8
KROWS = 16
QB = QROWS * GRID_W
KB = KROWS * GRID_W
V7X_VMEM_LIMIT = 56 * 2 ** 20
MESH = pl.DeviceIdType.MESH
ANY = pl.BlockSpec(memory_space=pl.ANY)


def _cparams(n_grid):
    return pltpu.CompilerParams(dimension_semantics=("arbitrary",) * n_grid, vmem_limit_bytes=V7X_VMEM_LIMIT)


def _nn(a, b):
    return lax.dot_general(a, b, (((1,), (0,)), ((), ())), preferred_element_type=F32)


def _nt(a, b):
    return lax.dot_general(a, b, (((1,), (1,)), ((), ())), preferred_element_type=F32)


def _tn(a, b):
    return lax.dot_general(a, b, (((0,), (0,)), ((), ())), preferred_element_type=F32)


def _sigmoid(x):
    return 1.0 / (1.0 + jnp.exp(-x))


def _ln_fwd(z, g, b):
    mu = jnp.mean(z, axis=-1, keepdims=True)
    xc = z - mu
    var = jnp.mean(xc * xc, axis=-1, keepdims=True)
    return xc * lax.rsqrt(var + LN_EPS) * g + b


def _ln_bwd(dh, z, g):
    mu = jnp.mean(z, axis=-1, keepdims=True)
    xc = z - mu
    var = jnp.mean(xc * xc, axis=-1, keepdims=True)
    rstd = lax.rsqrt(var + LN_EPS)
    xhat = xc * rstd
    dxh = dh * g
    m1 = jnp.mean(dxh, axis=-1, keepdims=True)
    m2 = jnp.mean(dxh * xhat, axis=-1, keepdims=True)
    return rstd * (dxh - m1 - xhat * m2), dh * xhat


def _colsum(x):
    return jnp.sum(x, axis=0, keepdims=True)


def _lane_cat(ref):
    return jnp.concatenate([ref[j] for j in range(ref.shape[0])], axis=1)


def _row_cat(ref):
    n, r, c = ref.shape
    return ref[...].reshape(n * r, c)


def _shards(n, r, c, li, j_of=None):
    del li
    if j_of is None:
        return pl.BlockSpec((n, r, c), lambda *_: (0, 0, 0))
    return pl.BlockSpec((n, r, c), lambda *g: (j_of(*g), 0, 0))


def _shard(r, c, li, j_of):
    del li
    return pl.BlockSpec((None, r, c), lambda *g: (j_of(*g), 0, 0))


def ln_fwd(x, g, b, name, after=()):
    s = x.shape[0]
    tm = 512
    na = len(after)

    def body(x_ref, g_ref, b_ref, *rest):
        h_ref, hb_ref = rest[na:]
        h = _ln_fwd(x_ref[...], g_ref[...], b_ref[...])
        h_ref[...] = h
        hb_ref[...] = h.astype(BF16)

    row = pl.BlockSpec((tm, D), lambda i: (i, 0))
    vec = pl.BlockSpec((1, D), lambda i: (0, 0))
    return pl.pallas_call(
        body, name=name, grid=(s // tm,), in_specs=[row, vec, vec] + [ANY] * na, out_specs=[row, row],
        out_shape=[jax.ShapeDtypeStruct((s, D), F32), jax.ShapeDtypeStruct((s, D), BF16)],
        compiler_params=_cparams(1))(x, g, b, *after)


def proj_fwd(hb, win, bias, li, j0, nj, out_dtype, name, after=()):
    s = hb.shape[0]
    bn = N_PROJ // N_DEV
    tm = 1024

    def body(a_ref, w_ref, b_ref, *rest):
        rest[-1][...] = (_nn(a_ref[...], w_ref[...]) + b_ref[...]).astype(out_dtype)

    return pl.pallas_call(
        body, name=name, grid=(s // tm, nj),
        in_specs=[pl.BlockSpec((tm, D), lambda i, j: (i, 0)),
                  _shard(D, bn, li, lambda i, j: j0 + j),
                  pl.BlockSpec((1, bn), lambda i, j: (0, j0 + j))] + [ANY] * len(after),
        out_specs=pl.BlockSpec((tm, bn), lambda i, j: (i, j)),
        out_shape=jax.ShapeDtypeStruct((s, nj * bn), out_dtype),
        compiler_params=_cparams(2))(hb, win, bias, *after)


def _attn_types(b, nb):
    first, last = 0, (nb * QROWS - KROWS) * GRID_W
    mid = pl.multiple_of((QROWS * b - KH // 2) * GRID_W, 256)
    return ((b == 0, first), ((b > 0) & (b < nb - 1), mid), (b == nb - 1, last))


def _attn_row(btype, qr):
    lo, delta = ((max(qr - KH // 2, 0), 0), (qr, -(KH // 2)), (min(qr + KH // 2, KH), -KH))[btype]
    return lo, (qr - delta - (KH - 1)) % KROWS, lo - qr + delta + KH - 1


def _row_window(lo):
    pad = (lo % 2) * GRID_W
    return (lo // 2) * 128, KH * GRID_W + 2 * pad, pad


def _lanes(ref, start, width):
    start %= KB
    if start + width <= KB:
        return ref[:, start:start + width]
    return jnp.concatenate([ref[:, start:], ref[:, :start + width - KB]], axis=1)


def _row_logits(s_ref, e_ref, hh, rows, btype, qr):
    lo, shift, _ = _attn_row(btype, qr)
    a0, w, pad = _row_window(lo)
    e = e_ref.at[hh, shift % 2]
    sb = s_ref[rows, a0:a0 + w] + _lanes(e, a0 - (shift - shift % 2) * GRID_W, w)
    if pad:
        lane = lax.broadcasted_iota(I32, (1, w), 1)
        sb = jnp.where((lane >= pad) & (lane < w - pad), sb, NEG_INF)
    return sb, a0, w, pad


def _store_row(ref, rows, a0, w, val):
    if a0:
        ref[rows, 0:a0] = jnp.zeros((GRID_W, a0), ref.dtype)
    ref[rows, a0:a0 + w] = val.astype(ref.dtype)
    if a0 + w < KB:
        ref[rows, a0 + w:KB] = jnp.zeros((GRID_W, KB - a0 - w), ref.dtype)


def attn_fwd(proj, e_tab, name):
    s = proj.shape[0]
    nb = s // QB

    def body(q_ref, k_ref, v_ref, e_ref, o_ref, s_ref, p_ref):
        q = q_ref[...] * ATT_SCALE
        lane = lax.broadcasted_iota(I32, (1, 128), 1)

        def block(btype, k0):
            kwin = k_ref[pl.ds(k0, KB), :]
            vwin = v_ref[pl.ds(k0, KB), :]
            acc = jnp.zeros((QB, 128), F32)
            for hh in range(2):
                lm = (lane // HEAD_DIM) == hh
                qh = jnp.where(lm, q, jnp.zeros_like(q))
                vh = jnp.where(lm, vwin, jnp.zeros_like(vwin))
                s_ref[...] = _nt(qh, kwin)
                for qr in range(QROWS):
                    rows = slice(qr * GRID_W, (qr + 1) * GRID_W)
                    sb, a0, w, _ = _row_logits(s_ref, e_ref, hh, rows, btype, qr)
                    p = jnp.exp(sb - jnp.max(sb, axis=1, keepdims=True))
                    _store_row(p_ref, rows, a0, w, p * (1.0 / jnp.sum(p, axis=1, keepdims=True)))
                acc = acc + _nn(p_ref[...], vh)
            o_ref[...] = acc.astype(BF16)

        for btype, (cond, k0) in enumerate(_attn_types(pl.program_id(1), nb)):
            pl.when(cond)(lambda btype=btype, k0=k0: block(btype, k0))

    return pl.pallas_call(
        body, name=name, grid=(4, nb),
        in_specs=[pl.BlockSpec((QB, 128), lambda j, b: (b, j)),
                  pl.BlockSpec((s, 128), lambda j, b: (0, 4 + j)),
                  pl.BlockSpec((s, 128), lambda j, b: (0, 8 + j)),
                  pl.BlockSpec((2, 2, GRID_W, KB), lambda j, b: (j, 0, 0, 0))],
        out_specs=pl.BlockSpec((QB, 128), lambda j, b: (b, j)),
        out_shape=jax.ShapeDtypeStruct((s, D_ATTN), BF16),
        scratch_shapes=[pltpu.VMEM((QB, KB), F32), pltpu.VMEM((QB, KB), BF16)],
        compiler_params=_cparams(2))(proj, proj, proj, e_tab)


_POOL_PAD = 8


def _pool_counts(s, w):
    t = lax.broadcasted_iota(I32, (s, 1), 0)
    return (jnp.minimum(t + w // 2, s) - jnp.maximum(t - w // 2, 0)).astype(F32)


def _window_sum(x, w, back_first):
    s = x.shape[0]
    z = jnp.zeros((_POOL_PAD, x.shape[1]), F32)
    xe = jnp.concatenate([z, x, z], axis=0)
    n = s + 2 * _POOL_PAD
    acc = xe + pltpu.roll(xe, 1 if back_first else n - 1, 0)
    k = 1
    while 2 * k < w:
        acc = pltpu.roll(acc, k, 0) + pltpu.roll(acc, n - k, 0)
        k *= 2
    return acc[_POOL_PAD:_POOL_PAD + s, :]


def pool_fwd(u, pool_w, pool_scale, name):
    s = u.shape[0]

    def body(u_ref, w_ref, sc_ref, pm_ref, pw_ref):
        for g, w in enumerate(POOL_WINDOWS):
            cols = slice(g * PGD, (g + 1) * PGD)
            ug = u_ref[:, cols]
            pm = (_window_sum(ug, w, True) / _pool_counts(s, w) - ug).astype(BF16)
            pm_ref[:, cols] = pm
            pw_ref[:, cols] = (_nn(pm, w_ref[g]) * sc_ref[:, cols]).astype(BF16)

    full = lambda shape: pl.BlockSpec(shape, lambda i: (0,) * len(shape))
    return pl.pallas_call(
        body, name=name, grid=(1,),
        in_specs=[full((s, D_POOL)), full((4, PGD, PGD)), full((1, D_POOL))],
        out_specs=[full((s, D_POOL)), full((s, D_POOL))],
        out_shape=[jax.ShapeDtypeStruct((s, D_POOL), BF16)] * 2,
        compiler_params=_cparams(1))(u, pool_w, pool_scale)


def merge_fwd(a, pw, wao, wpo, proj, li, name, after=()):
    s = a.shape[0]
    tm, tn = 512, 512
    nt = D // tn
    per = tn // 128

    def body(a_ref, pw_ref, wa_ref, wp_ref, ga_ref, gb_ref, *rest):
        mg_ref, ya_ref, yp_ref = rest[len(after):]
        ya = _nn(a_ref[...], _lane_cat(wa_ref))
        yp = _nn(pw_ref[...], _lane_cat(wp_ref))
        mg = _sigmoid(ga_ref[...].astype(F32)) * ya + _sigmoid(gb_ref[...].astype(F32)) * yp
        mg_ref[...] = mg.astype(BF16)
        ya_ref[...] = ya.astype(BF16)
        yp_ref[...] = yp.astype(BF16)

    act = pl.BlockSpec((tm, D_ATTN), lambda i, j: (i, 0))
    wsp = _shards(per, D_ATTN, 128, li, lambda i, j: j)
    out = pl.BlockSpec((tm, tn), lambda i, j: (i, j))
    ga0 = (3 * D_ATTN + D_POOL) // tn
    return pl.pallas_call(
        body, name=name, grid=(s // tm, nt),
        in_specs=[act, act, wsp, wsp,
                  pl.BlockSpec((tm, tn), lambda i, j: (i, ga0 + j)),
                  pl.BlockSpec((tm, tn), lambda i, j: (i, ga0 + nt + j))] + [ANY] * len(after),
        out_specs=[out, out, out],
        out_shape=[jax.ShapeDtypeStruct((s, D), BF16)] * 3,
        compiler_params=_cparams(2))(a, pw, wao, wpo, proj, proj, *after)


def mix_ln_fwd(mg, wmix, h0, g, b, li, name):
    s = mg.shape[0]
    tm = 256

    def body(mg_ref, w_ref, h0_ref, g_ref, b_ref, z_ref, h_ref, hb_ref):
        z = ALPHA * h0_ref[...] + _nn(mg_ref[...], _row_cat(w_ref))
        h = _ln_fwd(z, g_ref[...], b_ref[...])
        z_ref[...] = z
        h_ref[...] = h
        hb_ref[...] = h.astype(BF16)

    row = pl.BlockSpec((tm, D), lambda i: (i, 0))
    vec = pl.BlockSpec((1, D), lambda i: (0, 0))
    return pl.pallas_call(
        body, name=name, grid=(s // tm,),
        in_specs=[row, _shards(N_DEV, D // N_DEV, D, li), row, vec, vec],
        out_specs=[row, row, row],
        out_shape=[jax.ShapeDtypeStruct((s, D), F32), jax.ShapeDtypeStruct((s, D), F32),
                   jax.ShapeDtypeStruct((s, D), BF16)],
        compiler_params=_cparams(1))(mg, wmix, h0, g, b)


def up_fwd(hb, wup, li, name):
    s = hb.shape[0]
    tm = 1024

    def body(a_ref, w_ref, o_ref):
        o_ref[...] = _nt(a_ref[...], w_ref[...]).astype(BF16)

    return pl.pallas_call(
        body, name=name, grid=(s // tm, N_DEV),
        in_specs=[pl.BlockSpec((tm, D), lambda i, j: (i, 0)), _shard(FF_BLK, D, li, lambda i, j: j)],
        out_specs=pl.BlockSpec((None, tm, FF_BLK), lambda i, j: (j, i, 0)),
        out_shape=jax.ShapeDtypeStruct((N_DEV, s, FF_BLK), BF16),
        compiler_params=_cparams(2))(hb, wup)


_SQRT_HALF = 0.7071067811865476
_INV_SQRT_2PI = 0.3989422804014327


def _shift_rows(x, prev_row, next_row):
    n = x.shape[0]
    r = lax.broadcasted_iota(I32, (n, 1), 0)
    back = jnp.where(r == 0, prev_row, pltpu.roll(x, 1, 0))
    fwd = jnp.where(r == n - 1, next_row, pltpu.roll(x, n - 1, 0))
    return back, fwd


HALO = 16


def _halo_maps(tm, s):
    th = tm // HALO
    return (lambda i: jnp.maximum(i * th - 1, 0)), (lambda i: jnp.minimum((i + 1) * th, s // HALO - 1))


def _slab_specs(tm, s, blk_of):
    before, after = _halo_maps(tm, s)
    main = pl.BlockSpec((None, tm, FF_BLK), lambda c, i: (blk_of(c), i, 0))
    prev = pl.BlockSpec((None, HALO, FF_BLK), lambda c, i: (blk_of(c), before(i), 0))
    nxt = pl.BlockSpec((None, HALO, FF_BLK), lambda c, i: (blk_of(c), after(i), 0))
    return main, prev, nxt


def ffn_act_fwd(up, conv_w, conv_b, name):
    s = up.shape[1]
    tm = 512
    nt = s // tm
    hv_main, _, _ = _slab_specs(tm, s, lambda c: c)
    hg_main, hg_prev, hg_next = _slab_specs(tm, s, lambda c: 4 + c)

    def body(hv_ref, hg_ref, hp_ref, hn_ref, cw_ref, cb_ref, t_ref):
        i = pl.program_id(1)
        hg = hg_ref[...].astype(F32)
        prow = jnp.where(i == 0, 0.0, hp_ref[...].astype(F32)[HALO - 1:HALO, :])
        nrow = jnp.where(i == nt - 1, 0.0, hn_ref[...].astype(F32)[0:1, :])
        back, fwd = _shift_rows(hg, prow, nrow)
        c = back * cw_ref[0:1, :] + hg * cw_ref[1:2, :] + fwd * cw_ref[2:3, :] + cb_ref[...]
        act = 0.5 * c * (1.0 + lax.erf(c * _SQRT_HALF))
        t_ref[...] = (act * hv_ref[...].astype(F32)).astype(BF16)

    return pl.pallas_call(
        body, name=name, grid=(4, nt),
        in_specs=[hv_main, hg_main, hg_prev, hg_next,
                  pl.BlockSpec((None, 3, FF_BLK), lambda c, i: (c, 0, 0)),
                  pl.BlockSpec((None, 1, FF_BLK), lambda c, i: (c, 0, 0))],
        out_specs=pl.BlockSpec((None, tm, FF_BLK), lambda c, i: (c, i, 0)),
        out_shape=jax.ShapeDtypeStruct((4, s, FF_BLK), BF16),
        compiler_params=_cparams(2))(up, up, up, up, conv_w, conv_b)


def down_ple_ln_fwd(t, wdown, hb, wpg, pb, wpp, h1, g, b, li, name):
    s = hb.shape[0]
    tm = 256

    def body(t_ref, wd_ref, hb_ref, wpg_ref, p_ref, wpp_ref, h1_ref, g_ref, b_ref,
             z_ref, h_ref, hbo_ref, pg_ref, pp_ref):
        wd = _row_cat(wd_ref)
        ffn = _nn(t_ref[0], wd[0:FF_BLK, :])
        for c in range(1, 4):
            ffn = ffn + _nn(t_ref[c], wd[c * FF_BLK:(c + 1) * FF_BLK, :])
        pg = _nn(hb_ref[...], _row_cat(wpg_ref))
        pp = _nn(p_ref[...], _lane_cat(wpp_ref))
        z = ALPHA * h1_ref[...] + ffn + _sigmoid(pg) * pp
        h = _ln_fwd(z, g_ref[...], b_ref[...])
        z_ref[...] = z
        h_ref[...] = h
        hbo_ref[...] = h.astype(BF16)
        pg_ref[...] = pg.astype(BF16)
        pp_ref[...] = pp.astype(BF16)

    row = pl.BlockSpec((tm, D), lambda i: (i, 0))
    vec = pl.BlockSpec((1, D), lambda i: (0, 0))
    return pl.pallas_call(
        body, name=name, grid=(s // tm,),
        in_specs=[pl.BlockSpec((4, tm, FF_BLK), lambda i: (0, i, 0)),
                  _shards(N_DEV, FF_SHARD, D, li),
                  row, _shards(N_DEV, D // N_DEV, D, li),
                  pl.BlockSpec((tm, PLE_DIM), lambda i: (i, 0)),
                  _shards(N_DEV, PLE_DIM, 128, li),
                  row, vec, vec],
        out_specs=[row] * 5,
        out_shape=[jax.ShapeDtypeStruct((s, D), F32), jax.ShapeDtypeStruct((s, D), F32),
                   jax.ShapeDtypeStruct((s, D), BF16), jax.ShapeDtypeStruct((s, D), BF16),
                   jax.ShapeDtypeStruct((s, D), BF16)],
        compiler_params=_cparams(1))(t, wdown, hb, wpg, pb, wpp, h1, g, b)


def loss_bwd(h, target, name):
    s = h.shape[0]
    tm = 512

    def body(h_ref, t_ref, dh_ref, l_ref):
        @pl.when(pl.program_id(0) == 0)
        def _():
            l_ref[...] = jnp.zeros_like(l_ref)
        e = h_ref[...] - t_ref[...]
        dh_ref[...] = e * (1.0 / D)
        l_ref[...] += 0.5 * jnp.sum(jnp.mean(e * e, axis=-1, keepdims=True), axis=0, keepdims=True)

    row = pl.BlockSpec((tm, D), lambda i: (i, 0))
    return pl.pallas_call(
        body, name=name, grid=(s // tm,), in_specs=[row, row],
        out_specs=[row, pl.BlockSpec((1, 1), lambda i: (0, 0))],
        out_shape=[jax.ShapeDtypeStruct((s, D), F32), jax.ShapeDtypeStruct((1, 1), F32)],
        compiler_params=_cparams(1))(h, target)


def ln_bwd(dh, z, g, name, after=()):
    s = dh.shape[0]
    tm = 512
    na = len(after)

    def body(dh_ref, z_ref, g_ref, *rest):
        dz_ref, dg_ref, db_ref = rest[na:]

        @pl.when(pl.program_id(0) == 0)
        def _():
            dg_ref[...] = jnp.zeros_like(dg_ref)
            db_ref[...] = jnp.zeros_like(db_ref)
        dh = dh_ref[...]
        dz, dgx = _ln_bwd(dh, z_ref[...], g_ref[...])
        dz_ref[...] = dz
        dg_ref[...] += _colsum(dgx)
        db_ref[...] += _colsum(dh)

    row = pl.BlockSpec((tm, D), lambda i: (i, 0))
    vec = pl.BlockSpec((1, D), lambda i: (0, 0))
    return pl.pallas_call(
        body, name=name, grid=(s // tm,), in_specs=[row, row, vec] + [ANY] * na, out_specs=[row, vec, vec],
        out_shape=[jax.ShapeDtypeStruct((s, D), F32), jax.ShapeDtypeStruct((1, D), F32),
                   jax.ShapeDtypeStruct((1, D), F32)],
        compiler_params=_cparams(1))(dh, z, g, *after)


def ln2_ple_bwd(dh, z, g, pg, pp, name, after=()):
    s = dh.shape[0]
    tm = 512
    na = len(after)

    def body(dh_ref, z_ref, g_ref, pg_ref, pp_ref, *rest):
        dz_ref, dzb_ref, dpg_ref, dpp_ref, dg_ref, db_ref = rest[na:]

        @pl.when(pl.program_id(0) == 0)
        def _():
            dg_ref[...] = jnp.zeros_like(dg_ref)
            db_ref[...] = jnp.zeros_like(db_ref)
        dh = dh_ref[...]
        dz, dgx = _ln_bwd(dh, z_ref[...], g_ref[...])
        sg = _sigmoid(pg_ref[...].astype(F32))
        dz_ref[...] = dz
        dzb_ref[...] = dz.astype(BF16)
        dpg_ref[...] = (dz * pp_ref[...].astype(F32) * sg * (1.0 - sg)).astype(BF16)
        dpp_ref[...] = (dz * sg).astype(BF16)
        dg_ref[...] += _colsum(dgx)
        db_ref[...] += _colsum(dh)

    row = pl.BlockSpec((tm, D), lambda i: (i, 0))
    vec = pl.BlockSpec((1, D), lambda i: (0, 0))
    return pl.pallas_call(
        body, name=name, grid=(s // tm,), in_specs=[row, row, vec, row, row] + [ANY] * na,
        out_specs=[row, row, row, row, vec, vec],
        out_shape=[jax.ShapeDtypeStruct((s, D), F32)] + [jax.ShapeDtypeStruct((s, D), BF16)] * 3
        + [jax.ShapeDtypeStruct((1, D), F32)] * 2,
        compiler_params=_cparams(1))(dh, z, g, pg, pp, *after)


def wgrad_rows(a, dy, name):
    s, k = a.shape
    n = dy.shape[1]
    kb = k // N_DEV

    def body(a_ref, dy_ref, o_ref):
        o_ref[...] = _tn(a_ref[...], dy_ref[...]).astype(BF16)

    return pl.pallas_call(
        body, name=name, grid=(N_DEV,),
        in_specs=[pl.BlockSpec((s, kb), lambda j: (0, j)), pl.BlockSpec((s, n), lambda j: (0, 0))],
        out_specs=pl.BlockSpec((None, kb, n), lambda j: (j, 0, 0)),
        out_shape=jax.ShapeDtypeStruct((N_DEV, kb, n), BF16),
        compiler_params=_cparams(1))(a, dy)


def wgrad_cols(a, dy, name, with_colsum=False):
    s, k = a.shape
    n = dy.shape[1]
    nb = n // N_DEV

    def body(a_ref, dy_ref, o_ref, *cs_ref):
        dy = dy_ref[...]
        o_ref[...] = _tn(a_ref[...], dy).astype(BF16)
        if with_colsum:
            cs_ref[0][...] = _colsum(dy.astype(F32))

    out_specs = [pl.BlockSpec((None, k, nb), lambda j: (j, 0, 0))]
    out_shape = [jax.ShapeDtypeStruct((N_DEV, k, nb), BF16)]
    if with_colsum:
        out_specs.append(pl.BlockSpec((1, nb), lambda j: (0, j)))
        out_shape.append(jax.ShapeDtypeStruct((1, n), F32))
    res = pl.pallas_call(
        body, name=name, grid=(N_DEV,),
        in_specs=[pl.BlockSpec((s, k), lambda j: (0, 0)), pl.BlockSpec((s, nb), lambda j: (0, j))],
        out_specs=out_specs, out_shape=out_shape,
        compiler_params=_cparams(1))(a, dy)
    return res if with_colsum else res[0]


def wgrad_down(t, dy, name):
    _, s, k = t.shape
    n = dy.shape[1]

    def body(a_ref, dy_ref, o_ref):
        o_ref[...] = _tn(a_ref[...], dy_ref[...]).astype(BF16)

    return pl.pallas_call(
        body, name=name, grid=(4,),
        in_specs=[pl.BlockSpec((None, s, k), lambda j: (j, 0, 0)), pl.BlockSpec((s, n), lambda j: (0, 0))],
        out_specs=pl.BlockSpec((None, k, n), lambda j: (j, 0, 0)),
        out_shape=jax.ShapeDtypeStruct((4, k, n), BF16),
        compiler_params=_cparams(1))(t, dy)


def wgrad_up(a, dhv, dhg, name):
    s, k = a.shape

    def body(a_ref, dv_ref, dg_ref, o_ref):
        j = pl.program_id(0)

        @pl.when(j < 4)
        def _():
            o_ref[...] = _tn(dv_ref[...], a_ref[...]).astype(BF16)

        @pl.when(j >= 4)
        def _():
            o_ref[...] = _tn(dg_ref[...], a_ref[...]).astype(BF16)

    return pl.pallas_call(
        body, name=name, grid=(N_DEV,),
        in_specs=[pl.BlockSpec((s, k), lambda j: (0, 0)),
                  pl.BlockSpec((None, s, FF_BLK), lambda j: (jnp.minimum(j, 3), 0, 0)),
                  pl.BlockSpec((None, s, FF_BLK), lambda j: (jnp.maximum(j - 4, 0), 0, 0))],
        out_specs=pl.BlockSpec((None, FF_BLK, k), lambda j: (j, 0, 0)),
        out_shape=jax.ShapeDtypeStruct((N_DEV, FF_BLK, k), BF16),
        compiler_params=_cparams(1))(a, dhv, dhg)


def ffn_act_bwd(dzb, wdown, up, conv_w, conv_b, li, name):
    s = up.shape[1]
    tm = 512
    nt = s // tm
    before, after = _halo_maps(tm, s)
    hv_main, hv_prev, hv_next = _slab_specs(tm, s, lambda c: c)
    hg_main, hg_prev, hg_next = _slab_specs(tm, s, lambda c: 4 + c)

    def dc_of(dz, wd, hv, hg, back, fwd, cw_ref, cb_ref):
        dt = _nt(dz, wd)
        c = back * cw_ref[0:1, :] + hg * cw_ref[1:2, :] + fwd * cw_ref[2:3, :] + cb_ref[...]
        cdf = 0.5 * (1.0 + lax.erf(c * _SQRT_HALF))
        pdf = jnp.exp(-0.5 * c * c) * _INV_SQRT_2PI
        return dt, c * cdf, dt * hv * (cdf + c * pdf)

    def body(dz_ref, dzp_ref, dzn_ref, wd_ref, hv_ref, hvp_ref, hvn_ref, hg_ref, hgp_ref, hgn_ref, cw_ref, cb_ref,
             dhv_ref, dhg_ref, dcw_ref, dcb_ref):
        i = pl.program_id(1)

        @pl.when(i == 0)
        def _():
            dcw_ref[...] = jnp.zeros_like(dcw_ref)
            dcb_ref[...] = jnp.zeros_like(dcb_ref)

        wd = _row_cat(wd_ref)
        hg = hg_ref[...].astype(F32)
        hgp = hgp_ref[...].astype(F32)
        hgn = hgn_ref[...].astype(F32)
        first, last = i == 0, i == nt - 1
        e = HALO - 1
        back, fwd = _shift_rows(hg, jnp.where(first, 0.0, hgp[e:e + 1, :]), jnp.where(last, 0.0, hgn[0:1, :]))
        dt, act, dc = dc_of(dz_ref[...], wd, hv_ref[...].astype(F32), hg, back, fwd, cw_ref, cb_ref)
        dhv_ref[...] = (dt * act).astype(BF16)
        bp, fp = _shift_rows(hgp, hgp[0:1, :], hg[0:1, :])
        _, _, dcp = dc_of(dzp_ref[...], wd, hvp_ref[...].astype(F32), hgp, bp, fp, cw_ref, cb_ref)
        bn, fn = _shift_rows(hgn, hg[tm - 1:tm, :], hgn[e:e + 1, :])
        _, _, dcn = dc_of(dzn_ref[...], wd, hvn_ref[...].astype(F32), hgn, bn, fn, cw_ref, cb_ref)
        dc_back, dc_fwd = _shift_rows(dc, jnp.where(first, 0.0, dcp[e:e + 1, :]), jnp.where(last, 0.0, dcn[0:1, :]))
        dhg_ref[...] = (dc_fwd * cw_ref[0:1, :] + dc * cw_ref[1:2, :] + dc_back * cw_ref[2:3, :]).astype(BF16)
        dcw_ref[0:1, :] += _colsum(dc * back)
        dcw_ref[1:2, :] += _colsum(dc * hg)
        dcw_ref[2:3, :] += _colsum(dc * fwd)
        dcb_ref[...] += _colsum(dc)

    out_slab = pl.BlockSpec((None, tm, FF_BLK), lambda c, i: (c, i, 0))
    cw_spec = pl.BlockSpec((None, 3, FF_BLK), lambda c, i: (c, 0, 0))
    cb_spec = pl.BlockSpec((None, 1, FF_BLK), lambda c, i: (c, 0, 0))
    return pl.pallas_call(
        body, name=name, grid=(4, nt),
        in_specs=[pl.BlockSpec((tm, D), lambda c, i: (i, 0)),
                  pl.BlockSpec((HALO, D), lambda c, i: (before(i), 0)),
                  pl.BlockSpec((HALO, D), lambda c, i: (after(i), 0)),
                  _shards(2, FF_SHARD, D, li, lambda c, i: c),
                  hv_main, hv_prev, hv_next, hg_main, hg_prev, hg_next, cw_spec, cb_spec],
        out_specs=[out_slab, out_slab, cw_spec, cb_spec],
        out_shape=[jax.ShapeDtypeStruct((4, s, FF_BLK), BF16), jax.ShapeDtypeStruct((4, s, FF_BLK), BF16),
                   jax.ShapeDtypeStruct((4, 3, FF_BLK), F32), jax.ShapeDtypeStruct((4, 1, FF_BLK), F32)],
        compiler_params=_cparams(2))(dzb, dzb, dzb, wdown, up, up, up, up, up, up, conv_w, conv_b)


def dh1_ln1_bwd(dz2, dpg, wpg, dhv, dhg, wup, z1, g1, li, name, after=()):
    s = dz2.shape[0]
    tm = 256
    na = len(after)

    def body(dz2_ref, dpg_ref, wpg_ref, dhv_ref, dhg_ref, wup_ref, z1_ref, g_ref, *rest):
        dz_ref, dzb_ref, dg_ref, db_ref = rest[na:]

        @pl.when(pl.program_id(0) == 0)
        def _():
            dg_ref[...] = jnp.zeros_like(dg_ref)
            db_ref[...] = jnp.zeros_like(db_ref)
        dh = ALPHA * dz2_ref[...] + _nt(dpg_ref[...], _row_cat(wpg_ref))
        for c in range(4):
            dh = dh + _nn(dhv_ref[c], wup_ref[c]) + _nn(dhg_ref[c], wup_ref[4 + c])
        dz, dgx = _ln_bwd(dh, z1_ref[...], g_ref[...])
        dz_ref[...] = dz
        dzb_ref[...] = dz.astype(BF16)
        dg_ref[...] += _colsum(dgx)
        db_ref[...] += _colsum(dh)

    row = pl.BlockSpec((tm, D), lambda i: (i, 0))
    vec = pl.BlockSpec((1, D), lambda i: (0, 0))
    slab = pl.BlockSpec((4, tm, FF_BLK), lambda i: (0, i, 0))
    return pl.pallas_call(
        body, name=name, grid=(s // tm,),
        in_specs=[row, row, _shards(N_DEV, D // N_DEV, D, li), slab, slab, _shards(N_DEV, FF_BLK, D, li), row, vec]
        + [ANY] * na,
        out_specs=[row, row, vec, vec],
        out_shape=[jax.ShapeDtypeStruct((s, D), F32), jax.ShapeDtypeStruct((s, D), BF16),
                   jax.ShapeDtypeStruct((1, D), F32), jax.ShapeDtypeStruct((1, D), F32)],
        compiler_params=_cparams(1))(dz2, dpg, wpg, dhv, dhg, wup, z1, g1, *after)


def merge_bwd(dz1b, wmix, proj, ya, yp, li, name):
    s = dz1b.shape[0]
    tm, tn = 512, 512
    nt = D // tn
    per = tn // (D // N_DEV)
    ga0 = (3 * D_ATTN + D_POOL) // tn

    def body(dz_ref, w_ref, ga_ref, gb_ref, ya_ref, yp_ref, dya_ref, dyp_ref, dga_ref, dgb_ref):
        dm = _nt(dz_ref[...], _row_cat(w_ref))
        sa = _sigmoid(ga_ref[...].astype(F32))
        sb = _sigmoid(gb_ref[...].astype(F32))
        dya_ref[...] = (dm * sa).astype(BF16)
        dyp_ref[...] = (dm * sb).astype(BF16)
        dga_ref[...] = (dm * ya_ref[...].astype(F32) * sa * (1.0 - sa)).astype(BF16)
        dgb_ref[...] = (dm * yp_ref[...].astype(F32) * sb * (1.0 - sb)).astype(BF16)

    tile = pl.BlockSpec((tm, tn), lambda i, j: (i, j))
    return pl.pallas_call(
        body, name=name, grid=(s // tm, nt),
        in_specs=[pl.BlockSpec((tm, D), lambda i, j: (i, 0)),
                  _shards(per, D // N_DEV, D, li, lambda i, j: j),
                  pl.BlockSpec((tm, tn), lambda i, j: (i, ga0 + j)),
                  pl.BlockSpec((tm, tn), lambda i, j: (i, ga0 + nt + j)),
                  tile, tile],
        out_specs=[tile] * 4,
        out_shape=[jax.ShapeDtypeStruct((s, D), BF16)] * 4,
        compiler_params=_cparams(2))(dz1b, wmix, proj, proj, ya, yp)


def attn_out_bwd(dya, wao, li, name, after=()):
    s = dya.shape[0]
    tm = 512

    def body(d_ref, w_ref, *rest):
        rest[-1][...] = _nt(d_ref[...], _lane_cat(w_ref)).astype(BF16)

    return pl.pallas_call(
        body, name=name, grid=(s // tm,),
        in_specs=[pl.BlockSpec((tm, D), lambda i: (i, 0)), _shards(N_DEV, D_ATTN, 128, li)] + [ANY] * len(after),
        out_specs=pl.BlockSpec((tm, D_ATTN), lambda i: (i, 0)),
        out_shape=jax.ShapeDtypeStruct((s, D_ATTN), BF16),
        compiler_params=_cparams(1))(dya, wao, *after)


def pool_bwd(dyp, wpo, pm, pool_w, pool_scale, li, name):
    s = dyp.shape[0]

    def body(dyp_ref, wpo_ref, pm_ref, w_ref, sc_ref, du_ref, dw_ref, dsc_ref):
        wpo = _lane_cat(wpo_ref)
        dyp = dyp_ref[...]
        for g, w in enumerate(POOL_WINDOWS):
            cols = slice(g * PGD, (g + 1) * PGD)
            dpw = _nt(dyp, wpo[g * PGD:(g + 1) * PGD, :])
            pmg = pm_ref[:, cols]
            dsc_ref[:, cols] = _colsum(dpw * _nn(pmg, w_ref[g]))
            dpmw = (dpw * sc_ref[:, cols]).astype(BF16)
            dw_ref[g] = _tn(pmg, dpmw)
            dpm = _nt(dpmw, w_ref[g])
            du_ref[:, cols] = (_window_sum(dpm / _pool_counts(s, w), w, False) - dpm).astype(BF16)

    full = lambda shape: pl.BlockSpec(shape, lambda i: (0,) * len(shape))
    return pl.pallas_call(
        body, name=name, grid=(1,),
        in_specs=[full((s, D)), _shards(N_DEV, D_POOL, 128, li), full((s, D_POOL)), full((4, PGD, PGD)),
                  full((1, D_POOL))],
        out_specs=[full((s, D_POOL)), full((4, PGD, PGD)), full((1, D_POOL))],
        out_shape=[jax.ShapeDtypeStruct((s, D_POOL), BF16), jax.ShapeDtypeStruct((4, PGD, PGD), F32),
                   jax.ShapeDtypeStruct((1, D_POOL), F32)],
        compiler_params=_cparams(1))(dyp, wpo, pm, pool_w, pool_scale)


def attn_bwd(proj, da, e_rev, name):
    s = proj.shape[0]
    nb = s // QB
    skew = GRID_W + (GRID_W - KW)

    def body(q_ref, k_ref, v_ref, do_ref, e_ref, dq_ref, dk_ref, dv_ref, g_ref,
             s_ref, dp_ref, ds_ref, p_ref, dkt_acc, dvt_acc):
        b = pl.program_id(1)

        @pl.when(b == 0)
        def _():
            dkt_acc[...] = jnp.zeros_like(dkt_acc)
            dvt_acc[...] = jnp.zeros_like(dvt_acc)
            g_ref[...] = jnp.zeros_like(g_ref)

        ri = lax.broadcasted_iota(I32, (QB, QB), 0)
        ci = lax.broadcasted_iota(I32, (QB, QB), 1)
        rev = jnp.where(ri + ci == QB - 1, 1.0, 0.0).astype(BF16)
        q = _nn(rev, q_ref[...]).astype(BF16) * ATT_SCALE
        do = _nn(rev, do_ref[...]).astype(BF16)
        lane = lax.broadcasted_iota(I32, (1, 128), 1)

        def block(btype, k0):
            kwin = k_ref[pl.ds(k0, KB), :]
            vwin = v_ref[pl.ds(k0, KB), :]
            dq = jnp.zeros((QB, 128), F32)
            for hh in range(2):
                lm = (lane // HEAD_DIM) == hh
                qh = jnp.where(lm, q, jnp.zeros_like(q))
                doh = jnp.where(lm, do, jnp.zeros_like(do))
                kh = jnp.where(lm, kwin, jnp.zeros_like(kwin))
                s_ref[...] = _nt(qh, kwin)
                dp_ref[...] = _nt(doh, vwin)
                g = jnp.zeros((1, KB), F32)
                for ib in range(QROWS):
                    qr = QROWS - 1 - ib
                    rows = slice(ib * GRID_W, (ib + 1) * GRID_W)
                    sb, a0, w, pad = _row_logits(s_ref, e_ref, hh, rows, btype, qr)
                    p = jnp.exp(sb - jnp.max(sb, axis=1, keepdims=True))
                    p = p * (1.0 / jnp.sum(p, axis=1, keepdims=True))
                    dp = dp_ref[rows, a0:a0 + w]
                    ds = p * (dp - jnp.sum(p * dp, axis=1, keepdims=True))
                    _store_row(ds_ref, rows, a0, w, ds)
                    _store_row(p_ref, rows, a0, w, p)
                    t = jnp.sum(pltpu.roll(ds, w - skew, 1, stride=1, stride_axis=0), axis=0, keepdims=True)
                    t = t[:, :KH * GRID_W] if pad else pltpu.roll(t, GRID_W, 1)
                    i0 = _attn_row(btype, qr)[2]
                    g = g + pltpu.roll(jnp.concatenate([t, jnp.zeros_like(t)], axis=1), i0 * GRID_W, 1)
                g_ref[hh] += g
                dsb = ds_ref[...]
                dq = dq + _nn(dsb, kh) * ATT_SCALE
                dkt_acc[:, pl.ds(k0, KB)] += _tn(qh, dsb)
                dvt_acc[:, pl.ds(k0, KB)] += _tn(doh, p_ref[...])
            dq_ref[...] = _nn(rev, dq.astype(BF16)).astype(BF16)

        for btype, (cond, k0) in enumerate(_attn_types(b, nb)):
            pl.when(cond)(lambda btype=btype, k0=k0: block(btype, k0))

        @pl.when(b == nb - 1)
        def _():
            dk_ref[...] = dkt_acc[...].T.astype(BF16)
            dv_ref[...] = dvt_acc[...].T.astype(BF16)

    col = pl.BlockSpec((s, 128), lambda j, b: (0, j))
    return pl.pallas_call(
        body, name=name, grid=(4, nb),
        in_specs=[pl.BlockSpec((QB, 128), lambda j, b: (b, j)),
                  pl.BlockSpec((s, 128), lambda j, b: (0, 4 + j)),
                  pl.BlockSpec((s, 128), lambda j, b: (0, 8 + j)),
                  pl.BlockSpec((QB, 128), lambda j, b: (b, j)),
                  pl.BlockSpec((2, 2, GRID_W, KB), lambda j, b: (j, 0, 0, 0))],
        out_specs=[pl.BlockSpec((QB, 128), lambda j, b: (b, j)), col, col,
                   pl.BlockSpec((2, 1, KB), lambda j, b: (j, 0, 0))],
        out_shape=[jax.ShapeDtypeStruct((s, D_ATTN), BF16)] * 3 + [jax.ShapeDtypeStruct((N_HEADS, 1, KB), F32)],
        scratch_shapes=[pltpu.VMEM((QB, KB), F32), pltpu.VMEM((QB, KB), F32), pltpu.VMEM((QB, KB), BF16),
                        pltpu.VMEM((QB, KB), BF16), pltpu.VMEM((128, s), F32), pltpu.VMEM((128, s), F32)],
        compiler_params=_cparams(2))(proj, proj, proj, da, e_rev)


def dh0_bwd(dz1, dproj, win, li, name):
    s = dz1.shape[0]
    tm = 256
    bn = N_PROJ // N_DEV

    def body(dz_ref, dp_ref, w_ref, o_ref):
        acc = ALPHA * dz_ref[...]
        for j in range(N_DEV):
            acc = acc + _nt(dp_ref[:, j * bn:(j + 1) * bn], w_ref[j])
        o_ref[...] = acc

    row = pl.BlockSpec((tm, D), lambda i: (i, 0))
    return pl.pallas_call(
        body, name=name, grid=(s // tm,),
        in_specs=[row, pl.BlockSpec((tm, N_PROJ), lambda i: (i, 0)), _shards(N_DEV, D, bn, li)],
        out_specs=row, out_shape=jax.ShapeDtypeStruct((s, D), F32),
        compiler_params=_cparams(1))(dz1, dproj, win)


def _coords():
    return lax.axis_index("x"), lax.axis_index("y"), lax.axis_index("c")


def _dev_index(px, py, pc):
    return 4 * px + 2 * py + pc


def all_gather(arrs, name):
    n = len(arrs)

    def body(*refs):
        ins, outs = refs[:n], refs[n:2 * n]
        send_sems, recv_sems, local_sems = refs[2 * n:]
        x, y, c = _coords()
        me, sibling = (x, y, c), (x, y, 1 - c)
        chips = [(1 - x, y), (x, 1 - y), (1 - x, 1 - y)]

        def copy(a, k, block, to, src=None):
            dst = outs[a].at[_dev_index(*block)]
            return pltpu.make_async_remote_copy(
                src_ref=dst if src is None else src, dst_ref=dst,
                send_sem=send_sems.at[a, k], recv_sem=recv_sems.at[a, k], device_id=to, device_id_type=MESH)

        mine = [pltpu.make_async_copy(ins[a], outs[a].at[_dev_index(*me)], local_sems.at[a]) for a in range(n)]
        for cp in mine:
            cp.start()
        first = []
        for a in range(n):
            first.append(copy(a, 0, me, sibling, src=ins[a]))
            first += [copy(a, 1 + j, me, (*chip, c), src=ins[a]) for j, chip in enumerate(chips)]
        for cp in first:
            cp.start()
        passed = []
        for j, chip in enumerate(chips):
            for a in range(n):
                copy(a, 1 + j, (*chip, c), me).wait_recv()
                cp = copy(a, 4 + j, (*chip, c), sibling)
                cp.start()
                passed.append(cp)
        for a in range(n):
            copy(a, 0, sibling, me).wait_recv()
            for j, chip in enumerate(chips):
                copy(a, 4 + j, (*chip, 1 - c), me).wait_recv()
        for cp in first + passed:
            cp.wait_send()
        for cp in mine:
            cp.wait()

    return pl.pallas_call(
        body, name=name,
        out_shape=[jax.ShapeDtypeStruct((N_DEV,) + a.shape, a.dtype) for a in arrs],
        in_specs=[ANY] * n, out_specs=[ANY] * n,
        scratch_shapes=[pltpu.SemaphoreType.DMA((n, 7)), pltpu.SemaphoreType.DMA((n, 7)),
                        pltpu.SemaphoreType.DMA((n,))],
    )(*arrs)


def exchange_sibling(grads, name):
    n = len(grads)

    def body(*refs):
        ins, outs = refs[:n], refs[n:2 * n]
        send_sems, recv_sems = refs[2 * n:]
        x, y, c = _coords()
        sibling = (x, y, 1 - c)
        copies = []
        for a in range(n):
            for k in range(4):
                blk = _dev_index(x ^ (k & 1), y ^ (k >> 1), 1 - c)
                copies.append(pltpu.make_async_remote_copy(
                    src_ref=ins[a].at[blk], dst_ref=outs[a].at[k],
                    send_sem=send_sems.at[a, k], recv_sem=recv_sems.at[a, k], device_id=sibling, device_id_type=MESH))
        for cp in copies:
            cp.start()
        for cp in copies:
            cp.wait()

    return pl.pallas_call(
        body, name=name,
        out_shape=[jax.ShapeDtypeStruct((4,) + g.shape[1:], g.dtype) for g in grads],
        in_specs=[ANY] * n, out_specs=[ANY] * n,
        scratch_shapes=[pltpu.SemaphoreType.DMA((n, 4)), pltpu.SemaphoreType.DMA((n, 4))],
    )(*grads)


HBM = pl.BlockSpec(memory_space=pltpu.HBM)
SEM = pl.BlockSpec(memory_space=pltpu.SEMAPHORE)
_EFFECT = pltpu.SideEffectType.DATAFLOW_SIDE_EFFECTING
_TOKEN = jax.ShapeDtypeStruct((8, 128), F32)


def _in_hbm(a):
    return pltpu.with_memory_space_constraint(a, pltpu.HBM)


def _hbm_like(a):
    return pltpu.HBM(a.shape, a.dtype)


def _peers(x, y, c):
    return [(x, y, 1 - c), (1 - x, y, c), (x, 1 - y, c), (1 - x, 1 - y, c)]


def ag_start(lands, after, name):
    n = len(lands)

    def body(*refs):
        land = refs[:n]
        send_sem, recv_sem, token = refs[n + 1], refs[n + 2], refs[-1]
        x, y, c = _coords()
        me = _dev_index(x, y, c)
        for k, peer in enumerate(_peers(x, y, c)):
            for a in range(n):
                pltpu.make_async_remote_copy(src_ref=land[a].at[me], dst_ref=land[a].at[me], send_sem=send_sem.at[k],
                                             recv_sem=recv_sem.at[k], device_id=peer, device_id_type=MESH).start()
        token[...] = jnp.zeros_like(token)

    res = pl.pallas_call(
        body, name=name,
        out_shape=(pltpu.SemaphoreType.DMA((4,)), pltpu.SemaphoreType.DMA((4,)), *[_hbm_like(l) for l in lands], _TOKEN),
        in_specs=[HBM] * n + [ANY], out_specs=(SEM, SEM, *[HBM] * n, pl.BlockSpec(memory_space=pltpu.VMEM)),
        input_output_aliases={a: 2 + a for a in range(n)},
        compiler_params=pltpu.CompilerParams(has_side_effects=_EFFECT),
    )(*[_in_hbm(l) for l in lands], after)
    return res[0], res[1], list(res[2:2 + n]), res[-1]


def ag_forward(send_sem, recv_sem, lands, after, name):
    n = len(lands)

    def body(*refs):
        send_sem, recv_sem = refs[0], refs[1]
        land = refs[2:2 + n]
        fsend, frecv = refs[3 + n], refs[4 + n]
        x, y, c = _coords()
        peers = _peers(x, y, c)
        for k in range(1, 4):
            blk = _dev_index(*peers[k])
            for a in range(n):
                pltpu.make_async_remote_copy(src_ref=land[a].at[blk], dst_ref=land[a].at[blk], send_sem=send_sem.at[k],
                                             recv_sem=recv_sem.at[k], device_id=peers[k], device_id_type=MESH).wait_recv()
        for k in range(1, 4):
            blk = _dev_index(*peers[k])
            for a in range(n):
                pltpu.make_async_remote_copy(src_ref=land[a].at[blk], dst_ref=land[a].at[blk], send_sem=fsend.at[k - 1],
                                             recv_sem=frecv.at[k - 1], device_id=peers[0], device_id_type=MESH).start()

    res = pl.pallas_call(
        body, name=name,
        out_shape=(pltpu.SemaphoreType.DMA((3,)), pltpu.SemaphoreType.DMA((3,)), *[_hbm_like(l) for l in lands]),
        in_specs=[SEM, SEM, *[HBM] * n, ANY], out_specs=(SEM, SEM, *[HBM] * n),
        input_output_aliases={2 + a: 2 + a for a in range(n)},
        compiler_params=pltpu.CompilerParams(has_side_effects=_EFFECT),
    )(send_sem, recv_sem, *lands, after)
    return res[0], res[1], list(res[2:])


def ag_finish(send_sem, recv_sem, fsend, frecv, lands, after, name):
    n = len(lands)

    def body(*refs):
        send_sem, recv_sem, fsend, frecv = refs[:4]
        land = refs[4:4 + n]
        x, y, c = _coords()
        me = _dev_index(x, y, c)
        peers = _peers(x, y, c)
        for k in range(4):
            for a in range(n):
                pltpu.make_async_remote_copy(src_ref=land[a].at[me], dst_ref=land[a].at[me], send_sem=send_sem.at[k],
                                             recv_sem=recv_sem.at[k], device_id=peers[k], device_id_type=MESH).wait_send()
        sib = _dev_index(*peers[0])
        for a in range(n):
            pltpu.make_async_remote_copy(src_ref=land[a].at[sib], dst_ref=land[a].at[sib], send_sem=send_sem.at[0],
                                         recv_sem=recv_sem.at[0], device_id=peers[0], device_id_type=MESH).wait_recv()
        for k in range(1, 4):
            mine = _dev_index(*peers[k])
            theirs = _dev_index(peers[k][0], peers[k][1], 1 - c)
            for a in range(n):
                pltpu.make_async_remote_copy(src_ref=land[a].at[mine], dst_ref=land[a].at[theirs], send_sem=fsend.at[k - 1],
                                             recv_sem=frecv.at[k - 1], device_id=peers[0], device_id_type=MESH).wait()

    res = pl.pallas_call(
        body, name=name, out_shape=tuple(_hbm_like(l) for l in lands),
        in_specs=[SEM] * 4 + [HBM] * n + [ANY], out_specs=tuple([HBM] * n),
        input_output_aliases={4 + a: a for a in range(n)},
        compiler_params=pltpu.CompilerParams(has_side_effects=_EFFECT),
    )(send_sem, recv_sem, fsend, frecv, *lands, after)
    return list(res)


def rs_start(psums, name):
    n = len(psums)
    lands = [lax.empty(p.shape, p.dtype) for p in psums]

    def body(*refs):
        src, land = refs[:n], refs[n:2 * n]
        send_sem, recv_sem, token = refs[2 * n], refs[2 * n + 1], refs[-1]
        peers = _peers(*_coords())
        for k in range(3):
            for a in range(n):
                pltpu.make_async_remote_copy(src_ref=src[a].at[k], dst_ref=land[a].at[k], send_sem=send_sem.at[k],
                                             recv_sem=recv_sem.at[k], device_id=peers[k + 1], device_id_type=MESH).start()
        token[...] = jnp.zeros_like(token)

    res = pl.pallas_call(
        body, name=name,
        out_shape=(pltpu.SemaphoreType.DMA((3,)), pltpu.SemaphoreType.DMA((3,)), *[_hbm_like(p) for p in psums],
                   *[_hbm_like(l) for l in lands], _TOKEN),
        in_specs=[HBM] * (2 * n), out_specs=(SEM, SEM, *[HBM] * (2 * n), pl.BlockSpec(memory_space=pltpu.VMEM)),
        input_output_aliases={a: 2 + a for a in range(2 * n)},
        compiler_params=pltpu.CompilerParams(has_side_effects=_EFFECT),
    )(*[_in_hbm(p) for p in psums], *[_in_hbm(l) for l in lands])
    return res[0], res[1], list(res[2:2 + n]), list(res[2 + n:2 + 2 * n]), res[-1]


def rs_finish(send_sem, recv_sem, psums, lands, after, name):
    n = len(psums)

    def body(*refs):
        send_sem, recv_sem = refs[0], refs[1]
        src, land = refs[2:2 + n], refs[2 + n:2 + 2 * n]
        peers = _peers(*_coords())
        for k in range(3):
            for a in range(n):
                pltpu.make_async_remote_copy(src_ref=src[a].at[k], dst_ref=land[a].at[k], send_sem=send_sem.at[k],
                                             recv_sem=recv_sem.at[k], device_id=peers[k + 1], device_id_type=MESH).wait()

    res = pl.pallas_call(
        body, name=name, out_shape=tuple(_hbm_like(l) for l in lands),
        in_specs=[SEM, SEM] + [HBM] * (2 * n) + [ANY], out_specs=tuple([HBM] * n),
        input_output_aliases={2 + n + a: a for a in range(n)},
        compiler_params=pltpu.CompilerParams(has_side_effects=_EFFECT),
    )(send_sem, recv_sem, *psums, *lands, after)
    return list(res)


def _row_tile(r):
    return next(t for t in (512, 352, 256, 128) if r % t == 0)


def pair_add(blk_idx, g, recv, name):
    _, r, c = g.shape
    tr = _row_tile(r)

    def body(idx_ref, g_ref, r_ref, own_ref, oth_ref):
        k = pl.program_id(1)
        sm = g_ref[...].astype(F32) + r_ref[...].astype(F32)

        @pl.when(k == 0)
        def _():
            own_ref[...] = sm

        @pl.when(k > 0)
        def _():
            oth_ref[...] = sm.astype(BF16)

    grid_spec = pltpu.PrefetchScalarGridSpec(
        num_scalar_prefetch=1, grid=(r // tr, 4),
        in_specs=[pl.BlockSpec((None, tr, c), lambda t, k, idx: (idx[k], t, 0)),
                  pl.BlockSpec((None, tr, c), lambda t, k, idx: (k, t, 0))],
        out_specs=[pl.BlockSpec((tr, c), lambda t, k, idx: (t, 0)),
                   pl.BlockSpec((None, tr, c), lambda t, k, idx: (jnp.maximum(k - 1, 0), t, 0))])
    return pl.pallas_call(
        body, name=name, grid_spec=grid_spec,
        out_shape=[jax.ShapeDtypeStruct((r, c), F32), jax.ShapeDtypeStruct((3, r, c), BF16)],
        compiler_params=_cparams(2))(blk_idx, g, recv)


def _adamw(w, g, m, v):
    m = ADAM_B1 * m + (1.0 - ADAM_B1) * g
    v = ADAM_B2 * v + (1.0 - ADAM_B2) * (g * g)
    m_hat = m / (1.0 - ADAM_B1 ** ADAM_STEP)
    v_hat = v / (1.0 - ADAM_B2 ** ADAM_STEP)
    delta = -ADAM_LR * (m_hat / (jnp.sqrt(v_hat) + ADAM_EPS) + ADAM_WD * w)
    return delta, m, v


def adamw_shard(own, recv, w, m, v, li, prev, name):
    r, c = own.shape
    tr = _row_tile(r)

    def body(own_ref, recv_ref, w_ref, m_ref, v_ref, p0, p1, p2, p3, g_ref, d_ref, nm_ref, nv_ref):
        g = own_ref[...] + recv_ref[0].astype(F32) + recv_ref[1].astype(F32) + recv_ref[2].astype(F32)
        delta, nm, nv = _adamw(w_ref[...], g, m_ref[...], v_ref[...])
        g_ref[...] = g
        d_ref[...] = delta
        nm_ref[...] = nm
        nv_ref[...] = nv

    lay = pl.BlockSpec((None, tr, c), lambda t: (li, t, 0))
    stack = jax.ShapeDtypeStruct((DEPTH, r, c), F32)
    return pl.pallas_call(
        body, name=name, grid=(r // tr,),
        in_specs=[pl.BlockSpec((tr, c), lambda t: (t, 0)), pl.BlockSpec((3, tr, c), lambda t: (0, t, 0)),
                  lay, lay, lay, ANY, ANY, ANY, ANY],
        out_specs=[lay] * 4, out_shape=[stack] * 4,
        input_output_aliases={5: 0, 6: 1, 7: 2, 8: 3},
        compiler_params=_cparams(1))(own, recv, w, m, v, *prev)


def adamw_replicated(gathered, w, m, v, name):
    _, r, c = gathered.shape
    tr = 88

    def body(gs_ref, w_ref, m_ref, v_ref, g_ref, d_ref, nm_ref, nv_ref):
        g = gs_ref[0]
        for d in range(1, N_DEV):
            g = g + gs_ref[d]
        delta, nm, nv = _adamw(w_ref[...], g, m_ref[...], v_ref[...])
        g_ref[...] = g
        d_ref[...] = delta
        nm_ref[...] = nm
        nv_ref[...] = nv

    row = pl.BlockSpec((tr, c), lambda t: (t, 0))
    return pl.pallas_call(
        body, name=name, grid=(r // tr,),
        in_specs=[pl.BlockSpec((N_DEV, tr, c), lambda t: (0, t, 0)), row, row, row],
        out_specs=[row] * 4, out_shape=[jax.ShapeDtypeStruct((r, c), F32)] * 4,
        compiler_params=_cparams(1))(gathered, w, m, v)


def adamw_plain(g, w, m, v, name):
    def body(g_ref, w_ref, m_ref, v_ref, d_ref, nm_ref, nv_ref):
        delta, nm, nv = _adamw(w_ref[...], g_ref[...], m_ref[...], v_ref[...])
        d_ref[...] = delta
        nm_ref[...] = nm
        nv_ref[...] = nv

    return pl.pallas_call(body, name=name, out_shape=[jax.ShapeDtypeStruct(w.shape, F32)] * 3)(g, w, m, v)


_PACK = (("ln_in_g", (D,)), ("ln_in_b", (D,)), ("b_in", (DEPTH, N_PROJ)), ("rpb", (DEPTH, N_HEADS, 2 * KH - 1, 2 * KW - 1)),
         ("pool_w", (DEPTH, 4, PGD, PGD)), ("pool_scale", (DEPTH, D_POOL)), ("ln1_g", (DEPTH, D)), ("ln1_b", (DEPTH, D)),
         ("conv_b", (DEPTH, D_FF)), ("ln2_g", (DEPTH, D)), ("ln2_b", (DEPTH, D)), ("conv_w", (DEPTH, 3, D_FF)))
_PACK_LANES = 1024


def _pack_rows(shape):
    return -(-int(np.prod(shape)) // _PACK_LANES)


_PACK_ROWS = -(-sum(_pack_rows(s) for _, s in _PACK) // 88) * 88


def _pack(parts):
    rows = []
    for name, shape in _PACK:
        flat = parts[name].reshape(-1).astype(F32)
        rows.append(jnp.pad(flat, (0, _pack_rows(shape) * _PACK_LANES - flat.shape[0])))
    used = sum(_pack_rows(s) for _, s in _PACK)
    rows.append(jnp.zeros(((_PACK_ROWS - used) * _PACK_LANES,), F32))
    return jnp.concatenate(rows).reshape(_PACK_ROWS, _PACK_LANES)


def _unpack(packed):
    out, r0 = {}, 0
    for name, shape in _PACK:
        n = int(np.prod(shape))
        nr = _pack_rows(shape)
        out[name] = packed[r0:r0 + nr].reshape(-1)[:n].reshape(shape)
        r0 += nr
    return out


def _bias_tables(rpb_l):
    qc = np.arange(GRID_W)[:, None]
    kc = np.arange(GRID_W)[None, :]
    start = np.clip(qc - KW // 2, 0, GRID_W - KW)
    valid = (kc >= start) & (kc < start + KW)
    col = np.clip(kc - qc, -(KW - 1), KW - 1) + KW - 1
    onehot = (col.reshape(-1)[None, :] == np.arange(2 * KW - 1)[:, None]).astype(np.float32)
    rows = jnp.pad(rpb_l, ((0, 0), (0, 1), (0, 0)))
    tab = jnp.einsum("hij,jm->him", rows, jnp.asarray(onehot), precision=lax.Precision.HIGHEST)
    tab = tab.reshape(N_HEADS, KROWS, GRID_W, GRID_W).transpose(0, 2, 1, 3)
    ok = valid[None, :, None, :] & (np.arange(KROWS) < 2 * KH - 1)[None, None, :, None]
    tab = jnp.where(jnp.asarray(ok), tab, NEG_INF).reshape(N_HEADS, GRID_W, KB)
    tab = jnp.stack([tab, jnp.roll(tab, GRID_W, axis=-1)], axis=1)
    return tab, tab[:, :, ::-1, :]


_SHARDED = ("w_in", "w_attn_out", "w_pool_out", "w_mix_out", "w_up", "w_down", "w_ple_gate", "w_ple_proj")
_NAMES = ("ln_in_g", "ln_in_b", "w_in", "b_in", "rpb", "w_attn_out", "pool_w", "pool_scale", "w_pool_out", "w_mix_out",
          "ln1_g", "ln1_b", "w_up", "conv_w", "conv_b", "w_down", "w_ple_gate", "w_ple_proj", "ln2_g", "ln2_b")


def kernel(x, p, ln_in_g, ln_in_b, w_in, b_in, rpb, w_attn_out, pool_w, pool_scale, w_pool_out, w_mix_out, ln1_g, ln1_b, w_up, conv_w, conv_b, w_down, w_ple_gate, w_ple_proj, ln2_g, ln2_b, loss_target, m_ln_in_g, m_ln_in_b, m_w_in, m_b_in, m_rpb, m_w_attn_out, m_pool_w, m_pool_scale, m_w_pool_out, m_w_mix_out, m_ln1_g, m_ln1_b, m_w_up, m_conv_w, m_conv_b, m_w_down, m_w_ple_gate, m_w_ple_proj, m_ln2_g, m_ln2_b, v_ln_in_g, v_ln_in_b, v_w_in, v_b_in, v_rpb, v_w_attn_out, v_pool_w, v_pool_scale, v_w_pool_out, v_w_mix_out, v_ln1_g, v_ln1_b, v_w_up, v_conv_w, v_conv_b, v_w_down, v_w_ple_gate, v_w_ple_proj, v_ln2_g, v_ln2_b):
    a = dict(locals())
    W = {n: a[n] for n in _NAMES}
    M = {n: a["m_" + n] for n in _NAMES}
    V = {n: a["v_" + n] for n in _NAMES}
    xi, yi, ci = _coords()
    me = _dev_index(xi, yi, ci)
    x2, tgt = x[0], loss_target[0]
    pb = p[:, 0].astype(BF16)

    flip = lambda d: {**d, "w_up": d["w_up"].transpose(0, 2, 1)}
    ex = _Exchange(flip(W), flip(M), flip(V))
    loss_part, dx, parts = _local_step(x2, tgt, pb, W, ex)
    loss = lax.psum(loss_part[0, 0], AXES)
    stacks = {**ex.stacks, "w_up": [t.transpose(0, 2, 1) for t in ex.stacks["w_up"]]}

    (gath,) = all_gather([_pack(parts)], "ag_small_grads")
    zero_cw = jnp.zeros((DEPTH, 3, D_FF), F32)
    packs = [_pack({**{n: src[n] for n, _ in _PACK if n != "conv_w"}, "conv_w": zero_cw}) for src in (W, M, V)]
    outs = [_unpack(o) for o in adamw_replicated(gath, *packs, "adamw_replicated")]
    g_cw = lax.dynamic_slice_in_dim(outs[0]["conv_w"], me * FF_SHARD, FF_SHARD, axis=2)
    flat = lambda t: t.reshape(DEPTH * 3, FF_SHARD)
    cw_out = [o.reshape(DEPTH, 3, FF_SHARD) for o in
              adamw_plain(flat(g_cw), flat(conv_w), flat(m_conv_w), flat(v_conv_w), "adamw_conv_w")]
    res = []
    for k in range(4):
        d = {n: stacks[n][k] for n in _SHARDED}
        d.update({n: outs[k][n] for n, _ in _PACK if n != "conv_w"})
        d["conv_w"] = g_cw if k == 0 else cw_out[k - 1]
        res.append(d)
    return (loss, dx[None], *[res[k][n] for k in range(4) for n in _NAMES])


class _Exchange:
    GROUPS = (("w_ple_gate", "w_ple_proj", "w_down", "w_up"), ("w_mix_out", "w_attn_out", "w_pool_out"), ("w_in",))
    FIRST = ("w_in",)

    def __init__(self, W, M, V):
        self.W, self.M, self.V = W, M, V
        xi, yi, ci = _coords()
        me = _dev_index(xi, yi, ci)
        self.rel_idx = jnp.stack([_dev_index(xi ^ (k & 1), yi ^ (k >> 1), ci) for k in range(4)]).astype(I32)
        self.lands = [{n: lax.dynamic_update_index_in_dim(lax.empty((N_DEV,) + W[n].shape[1:], BF16),
                                                          W[n][li].astype(BF16), me, 0) for n in _SHARDED}
                      for li in range(DEPTH)]
        (cw,) = all_gather([W["conv_w"]], "ag_conv_w")
        self.cw = cw.transpose(1, 2, 0, 3).reshape(DEPTH, 3, 4, FF_BLK).transpose(0, 2, 1, 3)
        self.ag, self.fwd, self.rs, self.pending = {}, {}, {}, {}
        self.stacks = {n: [lax.empty((DEPTH,) + W[n].shape[1:], F32) for _ in range(4)] for n in _SHARDED}
        self.late = tuple(n for n in _SHARDED if n not in self.FIRST)
        self.ag[0] = ag_start([self.lands[0][n] for n in self.FIRST], cw, "ag_start0")

    def tokens(self):
        return [self.ag[0][3]]

    def prefetch(self, li, after):
        send, recv, lands, _ = self.ag[li]
        self.fwd[li] = ag_forward(send, recv, lands, after, f"ag_forward{li}")
        if li == 0:
            self.ag["0b"] = ag_start([self.lands[0][n] for n in self.late], self.fwd[0][2][0], "ag_start0b")

    def weights(self, li, after):
        send, recv, _, _ = self.ag.pop(li)
        fsend, frecv, lands = self.fwd.pop(li)
        lands = ag_finish(send, recv, fsend, frecv, lands, after, f"ag_finish{li}")
        if li == 0:
            return dict(zip(self.FIRST, lands)), self.cw[li], (self.ag["0b"][3],)
        tokens = ()
        if li + 1 < DEPTH:
            self.ag[li + 1] = ag_start([self.lands[li + 1][n] for n in _SHARDED], lands[0], f"ag_start{li + 1}")
            tokens = (self.ag[li + 1][3],)
        return dict(zip(_SHARDED, lands)), self.cw[li], tokens

    def rest(self, li, G, mid, after):
        if li != 0:
            return G, ()
        send, recv, lands, _ = self.ag.pop("0b")
        fsend, frecv, lands = ag_forward(send, recv, lands, mid, "ag_forward0b")
        lands = ag_finish(send, recv, fsend, frecv, lands, after, "ag_finish0b")
        self.ag[1] = ag_start([self.lands[1][n] for n in _SHARDED], lands[0], "ag_start1")
        return {**G, **dict(zip(self.late, lands))}, (self.ag[1][3],)

    def grads(self, li, group, gw):
        self.pending.setdefault(li, {}).update(gw)
        if li != 0 and group != len(self.GROUPS) - 1:
            return None
        gw = self.pending.pop(li)
        names = tuple(gw)
        tag = f"{li}_{group}" if li == 0 else f"{li}"
        glist = [gw[n] for n in names]
        recv1 = exchange_sibling(glist, f"rs_d2d{tag}")
        sums = [pair_add(self.rel_idx, g, r1, f"pair_add_{n}{li}") for n, g, r1 in zip(names, glist, recv1)]
        send, recv, psums, lands, token = rs_start([s_[1] for s_ in sums], f"rs_start{tag}")
        self.rs.setdefault(li, []).append((tag, names, send, recv, psums, lands, [s_[0] for s_ in sums]))
        return token

    def update(self, li, after):
        for tag, names, send, recv, psums, lands, owns in self.rs.pop(li):
            recv2 = rs_finish(send, recv, psums, lands, after, f"rs_finish{tag}")
            for n, own, r2 in zip(names, owns, recv2):
                self.stacks[n] = adamw_shard(own, r2, self.W[n], self.M[n], self.V[n], li, self.stacks[n],
                                             f"adamw_{n}{li}")


def _local_step(x2, tgt, pb, W, ex):
    depth = W["rpb"].shape[0]
    vec = lambda t: t.reshape(1, -1)
    ln1_g, ln1_b, ln2_g, ln2_b = W["ln1_g"], W["ln1_b"], W["ln2_g"], W["ln2_b"]
    b_in, rpb, pool_scale = W["b_in"], W["rpb"], W["pool_scale"]
    cb_full = W["conv_b"].reshape(depth, 4, 1, FF_BLK)
    pool_w_b = W["pool_w"].astype(BF16)

    h, hb = ln_fwd(x2, vec(W["ln_in_g"]), vec(W["ln_in_b"]), "ln_in", after=ex.tokens())
    ex.prefetch(0, hb)
    saved = []
    for li in range(depth):
        G, cw, tokens = ex.weights(li, hb)
        e_tab, e_rev = _bias_tables(rpb[li])
        bias = vec(b_in[li])
        proj = proj_fwd(hb, G["w_in"], bias, li, 0, N_DEV, BF16, f"proj{li}", after=tokens)
        u = proj_fwd(hb, G["w_in"], bias, li, 3, 1, F32, f"proj_u{li}")
        att = attn_fwd(proj, e_tab, f"attn{li}")
        pm, pw = pool_fwd(u, pool_w_b[li], vec(pool_scale[li]), f"pool{li}")
        G, tokens = ex.rest(li, G, att, pw)
        mg, ya, yp = merge_fwd(att, pw, G["w_attn_out"], G["w_pool_out"], proj, li, f"merge{li}", after=tokens)
        if li + 1 < depth:
            ex.prefetch(li + 1, mg)
        z1, h1, h1b = mix_ln_fwd(mg, G["w_mix_out"], h, vec(ln1_g[li]), vec(ln1_b[li]), li, f"mix_ln{li}")
        up = up_fwd(h1b, G["w_up"], li, f"up{li}")
        t = ffn_act_fwd(up, cw, cb_full[li], f"ffn_act{li}")
        z2, h2, h2b, pg, pp = down_ple_ln_fwd(t, G["w_down"], h1b, G["w_ple_gate"], pb[li], G["w_ple_proj"], h1,
                                              vec(ln2_g[li]), vec(ln2_b[li]), li, f"down_ln{li}")
        saved.append(dict(hb=hb, proj=proj, att=att, pm=pm, pw=pw, mg=mg, ya=ya, yp=yp, z1=z1, h1b=h1b, up=up, t=t,
                          z2=z2, pg=pg, pp=pp, e_rev=e_rev, G=G, cw=cw))
        h, hb = h2, h2b

    dh, loss_part = loss_bwd(h, tgt, "loss")
    small = {n: [None] * depth for n in ("b_in", "rpb", "pool_w", "pool_scale", "ln1_g", "ln1_b", "conv_b", "ln2_g",
                                         "ln2_b", "conv_w")}
    token = ()
    tok = lambda t: () if t is None else (t,)
    for li in reversed(range(depth)):
        sv = saved[li]
        G, cw = sv["G"], sv["cw"]
        dz2, dz2b, dpg, dpp, dg2, db2 = ln2_ple_bwd(dh, sv["z2"], vec(ln2_g[li]), sv["pg"], sv["pp"], f"ln2_bwd{li}",
                                                    after=token)
        gw = {}
        gw["w_ple_gate"] = wgrad_rows(sv["h1b"], dpg, f"dw_pg{li}")
        gw["w_ple_proj"] = wgrad_cols(pb[li], dpp, f"dw_pp{li}")
        gw["w_down"] = wgrad_down(sv["t"], dz2b, f"dw_down{li}").reshape(N_DEV, FF_SHARD, D)
        dhv, dhg, dcw, dcb = ffn_act_bwd(dz2b, G["w_down"], sv["up"], cw, cb_full[li], li, f"ffn_bwd{li}")
        gw["w_up"] = wgrad_up(sv["h1b"], dhv, dhg, f"dw_up{li}")
        token = tok(ex.grads(li, 0, gw))
        dz1, dz1b, dg1, db1 = dh1_ln1_bwd(dz2, dpg, G["w_ple_gate"], dhv, dhg, G["w_up"], sv["z1"], vec(ln1_g[li]), li,
                                          f"ln1_bwd{li}", after=token)
        gw = {"w_mix_out": wgrad_rows(sv["mg"], dz1b, f"dw_mix{li}")}
        dya, dyp, dga, dgb = merge_bwd(dz1b, G["w_mix_out"], sv["proj"], sv["ya"], sv["yp"], li, f"merge_bwd{li}")
        gw["w_attn_out"] = wgrad_cols(sv["att"], dya, f"dw_ao{li}")
        gw["w_pool_out"] = wgrad_cols(sv["pw"], dyp, f"dw_po{li}")
        token = tok(ex.grads(li, 1, gw))
        da = attn_out_bwd(dya, G["w_attn_out"], li, f"da{li}", after=token)
        du, dpool_w, dpool_sc = pool_bwd(dyp, G["w_pool_out"], sv["pm"], pool_w_b[li], vec(pool_scale[li]), li,
                                         f"pool_bwd{li}")
        dq, dk, dv, drpb = attn_bwd(sv["proj"], da, sv["e_rev"], f"attn_bwd{li}")
        dproj = jnp.concatenate([dq, dk, dv, du, dga, dgb], axis=1)
        dw_in, db_in = wgrad_cols(sv["hb"], dproj, f"dw_in{li}", with_colsum=True)
        dh = dh0_bwd(dz1, dproj, G["w_in"], li, f"dh0{li}")
        small["b_in"][li] = db_in.reshape(N_PROJ)
        small["rpb"][li] = drpb.reshape(N_HEADS, KROWS, GRID_W)[:, :2 * KH - 1, :2 * KW - 1]
        small["pool_w"][li] = dpool_w
        small["pool_scale"][li] = dpool_sc.reshape(D_POOL)
        small["ln1_g"][li], small["ln1_b"][li] = dg1.reshape(D), db1.reshape(D)
        small["ln2_g"][li], small["ln2_b"][li] = dg2.reshape(D), db2.reshape(D)
        small["conv_b"][li] = dcb.reshape(D_FF)
        small["conv_w"][li] = dcw.transpose(1, 0, 2).reshape(3, D_FF)
        token = tok(ex.grads(li, 2, {"w_in": dw_in}))
        if li + 1 < depth:
            ex.update(li + 1, dh)
    dx, dg_in, db_in0 = ln_bwd(dh, x2, vec(W["ln_in_g"]), "ln_in_bwd", after=token)
    ex.update(0, dx)
    parts = {n: jnp.stack(v_) for n, v_ in small.items()}
    parts["ln_in_g"], parts["ln_in_b"] = dg_in.reshape(D), db_in0.reshape(D)
    return loss_part, dx, parts
```

```python
import numpy as np
import jax
import jax.numpy as jnp
from jax import lax
from jax.experimental import pallas as pl
from jax.experimental.pallas import tpu as pltpu

F32 = jnp.float32
BF16 = jnp.bfloat16
I32 = jnp.int32

D = 1024
DEPTH = 4
GRID_W = 64
N_HEADS = 8
HEAD_DIM = 64
D_ATTN = 512
KH = 8
KW = 16
POOL_WINDOWS = (2, 4, 8, 16)
D_POOL = 512
PGD = 128
D_FF = 2816
PLE_DIM = 256
N_PROJ = 4096
ALPHA = (2 * DEPTH) ** 0.25
LN_EPS = 1e-5
NEG_INF = -1e30
ATT_SCALE = HEAD_DIM ** -0.5
ADAM_LR = 0.001
ADAM_B1 = 0.9
ADAM_B2 = 0.999
ADAM_EPS = 1e-08
ADAM_WD = 0.01
ADAM_STEP = 10

N_DEV = 8
AXES = ("x", "y", "c")
FF_BLK = D_FF // 4
FF_SHARD = D_FF // N_DEV
QROWS = 8
KROWS = 16
QB = QROWS * GRID_W
KB = KROWS * GRID_W
V7X_VMEM_LIMIT = 56 * 2 ** 20
MESH = pl.DeviceIdType.MESH
ANY = pl.BlockSpec(memory_space=pl.ANY)


def _cparams(n_grid):
    return pltpu.CompilerParams(dimension_semantics=("arbitrary",) * n_grid, vmem_limit_bytes=V7X_VMEM_LIMIT)


def _nn(a, b):
    return lax.dot_general(a, b, (((1,), (0,)), ((), ())), preferred_element_type=F32)


def _nt(a, b):
    return lax.dot_general(a, b, (((1,), (1,)), ((), ())), preferred_element_type=F32)


def _tn(a, b):
    return lax.dot_general(a, b, (((0,), (0,)), ((), ())), preferred_element_type=F32)


def _sigmoid(x):
    return 1.0 / (1.0 + jnp.exp(-x))


def _ln_fwd(z, g, b):
    mu = jnp.mean(z, axis=-1, keepdims=True)
    xc = z - mu
    var = jnp.mean(xc * xc, axis=-1, keepdims=True)
    return xc * lax.rsqrt(var + LN_EPS) * g + b


def _ln_bwd(dh, z, g):
    mu = jnp.mean(z, axis=-1, keepdims=True)
    xc = z - mu
    var = jnp.mean(xc * xc, axis=-1, keepdims=True)
    rstd = lax.rsqrt(var + LN_EPS)
    xhat = xc * rstd
    dxh = dh * g
    m1 = jnp.mean(dxh, axis=-1, keepdims=True)
    m2 = jnp.mean(dxh * xhat, axis=-1, keepdims=True)
    return rstd * (dxh - m1 - xhat * m2), dh * xhat


def _colsum(x):
    return jnp.sum(x, axis=0, keepdims=True)


def _lane_cat(ref):
    return jnp.concatenate([ref[j] for j in range(ref.shape[0])], axis=1)


def _row_cat(ref):
    n, r, c = ref.shape
    return ref[...].reshape(n * r, c)


def _shards(n, r, c, li, j_of=None):
    del li
    if j_of is None:
        return pl.BlockSpec((n, r, c), lambda *_: (0, 0, 0))
    return pl.BlockSpec((n, r, c), lambda *g: (j_of(*g), 0, 0))


def _shard(r, c, li, j_of):
    del li
    return pl.BlockSpec((None, r, c), lambda *g: (j_of(*g), 0, 0))


def ln_fwd(x, g, b, name, after=()):
    s = x.shape[0]
    tm = 512
    na = len(after)

    def body(x_ref, g_ref, b_ref, *rest):
        h_ref, hb_ref = rest[na:]
        h = _ln_fwd(x_ref[...], g_ref[...], b_ref[...])
        h_ref[...] = h
        hb_ref[...] = h.astype(BF16)

    row = pl.BlockSpec((tm, D), lambda i: (i, 0))
    vec = pl.BlockSpec((1, D), lambda i: (0, 0))
    return pl.pallas_call(
        body, name=name, grid=(s // tm,), in_specs=[row, vec, vec] + [ANY] * na, out_specs=[row, row],
        out_shape=[jax.ShapeDtypeStruct((s, D), F32), jax.ShapeDtypeStruct((s, D), BF16)],
        compiler_params=_cparams(1))(x, g, b, *after)


def proj_fwd(hb, win, bias, li, j0, nj, out_dtype, name, after=()):
    s = hb.shape[0]
    bn = N_PROJ // N_DEV
    tm = 1024

    def body(a_ref, w_ref, b_ref, *rest):
        rest[-1][...] = (_nn(a_ref[...], w_ref[...]) + b_ref[...]).astype(out_dtype)

    return pl.pallas_call(
        body, name=name, grid=(s // tm, nj),
        in_specs=[pl.BlockSpec((tm, D), lambda i, j: (i, 0)),
                  _shard(D, bn, li, lambda i, j: j0 + j),
                  pl.BlockSpec((1, bn), lambda i, j: (0, j0 + j))] + [ANY] * len(after),
        out_specs=pl.BlockSpec((tm, bn), lambda i, j: (i, j)),
        out_shape=jax.ShapeDtypeStruct((s, nj * bn), out_dtype),
        compiler_params=_cparams(2))(hb, win, bias, *after)


def _attn_types(b, nb):
    first, last = 0, (nb * QROWS - KROWS) * GRID_W
    mid = pl.multiple_of((QROWS * b - KH // 2) * GRID_W, 256)
    return ((b == 0, first), ((b > 0) & (b < nb - 1), mid), (b == nb - 1, last))


def _attn_row(btype, qr):
    lo, delta = ((max(qr - KH // 2, 0), 0), (qr, -(KH // 2)), (min(qr + KH // 2, KH), -KH))[btype]
    return lo, (qr - delta - (KH - 1)) % KROWS, lo - qr + delta + KH - 1


def _row_window(lo):
    pad = (lo % 2) * GRID_W
    return (lo // 2) * 128, KH * GRID_W + 2 * pad, pad


def _lanes(ref, start, width):
    start %= KB
    if start + width <= KB:
        return ref[:, start:start + width]
    return jnp.concatenate([ref[:, start:], ref[:, :start + width - KB]], axis=1)


def _row_logits(s_ref, e_ref, hh, rows, btype, qr):
    lo, shift, _ = _attn_row(btype, qr)
    a0, w, pad = _row_window(lo)
    e = e_ref.at[hh, shift % 2]
    sb = s_ref[rows, a0:a0 + w] + _lanes(e, a0 - (shift - shift % 2) * GRID_W, w)
    if pad:
        lane = lax.broadcasted_iota(I32, (1, w), 1)
        sb = jnp.where((lane >= pad) & (lane < w - pad), sb, NEG_INF)
    return sb, a0, w, pad


def _store_row(ref, rows, a0, w, val):
    if a0:
        ref[rows, 0:a0] = jnp.zeros((GRID_W, a0), ref.dtype)
    ref[rows, a0:a0 + w] = val.astype(ref.dtype)
    if a0 + w < KB:
        ref[rows, a0 + w:KB] = jnp.zeros((GRID_W, KB - a0 - w), ref.dtype)


def attn_fwd(proj, e_tab, name):
    s = proj.shape[0]
    nb = s // QB

    def body(q_ref, k_ref, v_ref, e_ref, o_ref, s_ref, p_ref):
        q = q_ref[...] * ATT_SCALE
        lane = lax.broadcasted_iota(I32, (1, 128), 1)

        def block(btype, k0):
            kwin = k_ref[pl.ds(k0, KB), :]
            vwin = v_ref[pl.ds(k0, KB), :]
            acc = jnp.zeros((QB, 128), F32)
            for hh in range(2):
                lm = (lane // HEAD_DIM) == hh
                qh = jnp.where(lm, q, jnp.zeros_like(q))
                vh = jnp.where(lm, vwin, jnp.zeros_like(vwin))
                s_ref[...] = _nt(qh, kwin)
                for qr in range(QROWS):
                    rows = slice(qr * GRID_W, (qr + 1) * GRID_W)
                    sb, a0, w, _ = _row_logits(s_ref, e_ref, hh, rows, btype, qr)
                    p = jnp.exp(sb - jnp.max(sb, axis=1, keepdims=True))
                    _store_row(p_ref, rows, a0, w, p * (1.0 / jnp.sum(p, axis=1, keepdims=True)))
                acc = acc + _nn(p_ref[...], vh)
            o_ref[...] = acc.astype(BF16)

        for btype, (cond, k0) in enumerate(_attn_types(pl.program_id(1), nb)):
            pl.when(cond)(lambda btype=btype, k0=k0: block(btype, k0))

    return pl.pallas_call(
        body, name=name, grid=(4, nb),
        in_specs=[pl.BlockSpec((QB, 128), lambda j, b: (b, j)),
                  pl.BlockSpec((s, 128), lambda j, b: (0, 4 + j)),
                  pl.BlockSpec((s, 128), lambda j, b: (0, 8 + j)),
                  pl.BlockSpec((2, 2, GRID_W, KB), lambda j, b: (j, 0, 0, 0))],
        out_specs=pl.BlockSpec((QB, 128), lambda j, b: (b, j)),
        out_shape=jax.ShapeDtypeStruct((s, D_ATTN), BF16),
        scratch_shapes=[pltpu.VMEM((QB, KB), F32), pltpu.VMEM((QB, KB), BF16)],
        compiler_params=_cparams(2))(proj, proj, proj, e_tab)


_POOL_PAD = 8


def _pool_counts(s, w):
    t = lax.broadcasted_iota(I32, (s, 1), 0)
    return (jnp.minimum(t + w // 2, s) - jnp.maximum(t - w // 2, 0)).astype(F32)


def _window_sum(x, w, back_first):
    s = x.shape[0]
    z = jnp.zeros((_POOL_PAD, x.shape[1]), F32)
    xe = jnp.concatenate([z, x, z], axis=0)
    n = s + 2 * _POOL_PAD
    acc = xe + pltpu.roll(xe, 1 if back_first else n - 1, 0)
    k = 1
    while 2 * k < w:
        acc = pltpu.roll(acc, k, 0) + pltpu.roll(acc, n - k, 0)
        k *= 2
    return acc[_POOL_PAD:_POOL_PAD + s, :]


def pool_fwd(u, pool_w, pool_scale, name):
    s = u.shape[0]

    def body(u_ref, w_ref, sc_ref, pm_ref, pw_ref):
        for g, w in enumerate(POOL_WINDOWS):
            cols = slice(g * PGD, (g + 1) * PGD)
            ug = u_ref[:, cols]
            pm = (_window_sum(ug, w, True) / _pool_counts(s, w) - ug).astype(BF16)
            pm_ref[:, cols] = pm
            pw_ref[:, cols] = (_nn(pm, w_ref[g]) * sc_ref[:, cols]).astype(BF16)

    full = lambda shape: pl.BlockSpec(shape, lambda i: (0,) * len(shape))
    return pl.pallas_call(
        body, name=name, grid=(1,),
        in_specs=[full((s, D_POOL)), full((4, PGD, PGD)), full((1, D_POOL))],
        out_specs=[full((s, D_POOL)), full((s, D_POOL))],
        out_shape=[jax.ShapeDtypeStruct((s, D_POOL), BF16)] * 2,
        compiler_params=_cparams(1))(u, pool_w, pool_scale)


def merge_fwd(a, pw, wao, wpo, proj, li, name, after=()):
    s = a.shape[0]
    tm, tn = 512, 512
    nt = D // tn
    per = tn // 128

    def body(a_ref, pw_ref, wa_ref, wp_ref, ga_ref, gb_ref, *rest):
        mg_ref, ya_ref, yp_ref = rest[len(after):]
        ya = _nn(a_ref[...], _lane_cat(wa_ref))
        yp = _nn(pw_ref[...], _lane_cat(wp_ref))
        mg = _sigmoid(ga_ref[...].astype(F32)) * ya + _sigmoid(gb_ref[...].astype(F32)) * yp
        mg_ref[...] = mg.astype(BF16)
        ya_ref[...] = ya.astype(BF16)
        yp_ref[...] = yp.astype(BF16)

    act = pl.BlockSpec((tm, D_ATTN), lambda i, j: (i, 0))
    wsp = _shards(per, D_ATTN, 128, li, lambda i, j: j)
    out = pl.BlockSpec((tm, tn), lambda i, j: (i, j))
    ga0 = (3 * D_ATTN + D_POOL) // tn
    return pl.pallas_call(
        body, name=name, grid=(s // tm, nt),
        in_specs=[act, act, wsp, wsp,
                  pl.BlockSpec((tm, tn), lambda i, j: (i, ga0 + j)),
                  pl.BlockSpec((tm, tn), lambda i, j: (i, ga0 + nt + j))] + [ANY] * len(after),
        out_specs=[out, out, out],
        out_shape=[jax.ShapeDtypeStruct((s, D), BF16)] * 3,
        compiler_params=_cparams(2))(a, pw, wao, wpo, proj, proj, *after)


def mix_ln_fwd(mg, wmix, h0, g, b, li, name):
    s = mg.shape[0]
    tm = 256

    def body(mg_ref, w_ref, h0_ref, g_ref, b_ref, z_ref, h_ref, hb_ref):
        z = ALPHA * h0_ref[...] + _nn(mg_ref[...], _row_cat(w_ref))
        h = _ln_fwd(z, g_ref[...], b_ref[...])
        z_ref[...] = z
        h_ref[...] = h
        hb_ref[...] = h.astype(BF16)

    row = pl.BlockSpec((tm, D), lambda i: (i, 0))
    vec = pl.BlockSpec((1, D), lambda i: (0, 0))
    return pl.pallas_call(
        body, name=name, grid=(s // tm,),
        in_specs=[row, _shards(N_DEV, D // N_DEV, D, li), row, vec, vec],
        out_specs=[row, row, row],
        out_shape=[jax.ShapeDtypeStruct((s, D), F32), jax.ShapeDtypeStruct((s, D), F32),
                   jax.ShapeDtypeStruct((s, D), BF16)],
        compiler_params=_cparams(1))(mg, wmix, h0, g, b)


def up_fwd(hb, wup, li, name):
    s = hb.shape[0]
    tm = 1024

    def body(a_ref, w_ref, o_ref):
        o_ref[...] = _nt(a_ref[...], w_ref[...]).astype(BF16)

    return pl.pallas_call(
        body, name=name, grid=(s // tm, N_DEV),
        in_specs=[pl.BlockSpec((tm, D), lambda i, j: (i, 0)), _shard(FF_BLK, D, li, lambda i, j: j)],
        out_specs=pl.BlockSpec((None, tm, FF_BLK), lambda i, j: (j, i, 0)),
        out_shape=jax.ShapeDtypeStruct((N_DEV, s, FF_BLK), BF16),
        compiler_params=_cparams(2))(hb, wup)


_SQRT_HALF = 0.7071067811865476
_INV_SQRT_2PI = 0.3989422804014327


def _shift_rows(x, prev_row, next_row):
    n = x.shape[0]
    r = lax.broadcasted_iota(I32, (n, 1), 0)
    back = jnp.where(r == 0, prev_row, pltpu.roll(x, 1, 0))
    fwd = jnp.where(r == n - 1, next_row, pltpu.roll(x, n - 1, 0))
    return back, fwd


HALO = 16


def _halo_maps(tm, s):
    th = tm // HALO
    return (lambda i: jnp.maximum(i * th - 1, 0)), (lambda i: jnp.minimum((i + 1) * th, s // HALO - 1))


def _slab_specs(tm, s, blk_of):
    before, after = _halo_maps(tm, s)
    main = pl.BlockSpec((None, tm, FF_BLK), lambda c, i: (blk_of(c), i, 0))
    prev = pl.BlockSpec((None, HALO, FF_BLK), lambda c, i: (blk_of(c), before(i), 0))
    nxt = pl.BlockSpec((None, HALO, FF_BLK), lambda c, i: (blk_of(c), after(i), 0))
    return main, prev, nxt


def ffn_act_fwd(up, conv_w, conv_b, name):
    s = up.shape[1]
    tm = 512
    nt = s // tm
    hv_main, _, _ = _slab_specs(tm, s, lambda c: c)
    hg_main, hg_prev, hg_next = _slab_specs(tm, s, lambda c: 4 + c)

    def body(hv_ref, hg_ref, hp_ref, hn_ref, cw_ref, cb_ref, t_ref):
        i = pl.program_id(1)
        hg = hg_ref[...].astype(F32)
        prow = jnp.where(i == 0, 0.0, hp_ref[...].astype(F32)[HALO - 1:HALO, :])
        nrow = jnp.where(i == nt - 1, 0.0, hn_ref[...].astype(F32)[0:1, :])
        back, fwd = _shift_rows(hg, prow, nrow)
        c = back * cw_ref[0:1, :] + hg * cw_ref[1:2, :] + fwd * cw_ref[2:3, :] + cb_ref[...]
        act = 0.5 * c * (1.0 + lax.erf(c * _SQRT_HALF))
        t_ref[...] = (act * hv_ref[...].astype(F32)).astype(BF16)

    return pl.pallas_call(
        body, name=name, grid=(4, nt),
        in_specs=[hv_main, hg_main, hg_prev, hg_next,
                  pl.BlockSpec((None, 3, FF_BLK), lambda c, i: (c, 0, 0)),
                  pl.BlockSpec((None, 1, FF_BLK), lambda c, i: (c, 0, 0))],
        out_specs=pl.BlockSpec((None, tm, FF_BLK), lambda c, i: (c, i, 0)),
        out_shape=jax.ShapeDtypeStruct((4, s, FF_BLK), BF16),
        compiler_params=_cparams(2))(up, up, up, up, conv_w, conv_b)


def down_ple_ln_fwd(t, wdown, hb, wpg, pb, wpp, h1, g, b, li, name):
    s = hb.shape[0]
    tm = 256

    def body(t_ref, wd_ref, hb_ref, wpg_ref, p_ref, wpp_ref, h1_ref, g_ref, b_ref,
             z_ref, h_ref, hbo_ref, pg_ref, pp_ref):
        wd = _row_cat(wd_ref)
        ffn = _nn(t_ref[0], wd[0:FF_BLK, :])
        for c in range(1, 4):
            ffn = ffn + _nn(t_ref[c], wd[c * FF_BLK:(c + 1) * FF_BLK, :])
        pg = _nn(hb_ref[...], _row_cat(wpg_ref))
        pp = _nn(p_ref[...], _lane_cat(wpp_ref))
        z = ALPHA * h1_ref[...] + ffn + _sigmoid(pg) * pp
        h = _ln_fwd(z, g_ref[...], b_ref[...])
        z_ref[...] = z
        h_ref[...] = h
        hbo_ref[...] = h.astype(BF16)
        pg_ref[...] = pg.astype(BF16)
        pp_ref[...] = pp.astype(BF16)

    row = pl.BlockSpec((tm, D), lambda i: (i, 0))
    vec = pl.BlockSpec((1, D), lambda i: (0, 0))
    return pl.pallas_call(
        body, name=name, grid=(s // tm,),
        in_specs=[pl.BlockSpec((4, tm, FF_BLK), lambda i: (0, i, 0)),
                  _shards(N_DEV, FF_SHARD, D, li),
                  row, _shards(N_DEV, D // N_DEV, D, li),
                  pl.BlockSpec((tm, PLE_DIM), lambda i: (i, 0)),
                  _shards(N_DEV, PLE_DIM, 128, li),
                  row, vec, vec],
        out_specs=[row] * 5,
        out_shape=[jax.ShapeDtypeStruct((s, D), F32), jax.ShapeDtypeStruct((s, D), F32),
                   jax.ShapeDtypeStruct((s, D), BF16), jax.ShapeDtypeStruct((s, D), BF16),
                   jax.ShapeDtypeStruct((s, D), BF16)],
        compiler_params=_cparams(1))(t, wdown, hb, wpg, pb, wpp, h1, g, b)


def loss_bwd(h, target, name):
    s = h.shape[0]
    tm = 512

    def body(h_ref, t_ref, dh_ref, l_ref):
        @pl.when(pl.program_id(0) == 0)
        def _():
            l_ref[...] = jnp.zeros_like(l_ref)
        e = h_ref[...] - t_ref[...]
        dh_ref[...] = e * (1.0 / D)
        l_ref[...] += 0.5 * jnp.sum(jnp.mean(e * e, axis=-1, keepdims=True), axis=0, keepdims=True)

    row = pl.BlockSpec((tm, D), lambda i: (i, 0))
    return pl.pallas_call(
        body, name=name, grid=(s // tm,), in_specs=[row, row],
        out_specs=[row, pl.BlockSpec((1, 1), lambda i: (0, 0))],
        out_shape=[jax.ShapeDtypeStruct((s, D), F32), jax.ShapeDtypeStruct((1, 1), F32)],
        compiler_params=_cparams(1))(h, target)


def ln_bwd(dh, z, g, name, after=()):
    s = dh.shape[0]
    tm = 512
    na = len(after)

    def body(dh_ref, z_ref, g_ref, *rest):
        dz_ref, dg_ref, db_ref = rest[na:]

        @pl.when(pl.program_id(0) == 0)
        def _():
            dg_ref[...] = jnp.zeros_like(dg_ref)
            db_ref[...] = jnp.zeros_like(db_ref)
        dh = dh_ref[...]
        dz, dgx = _ln_bwd(dh, z_ref[...], g_ref[...])
        dz_ref[...] = dz
        dg_ref[...] += _colsum(dgx)
        db_ref[...] += _colsum(dh)

    row = pl.BlockSpec((tm, D), lambda i: (i, 0))
    vec = pl.BlockSpec((1, D), lambda i: (0, 0))
    return pl.pallas_call(
        body, name=name, grid=(s // tm,), in_specs=[row, row, vec] + [ANY] * na, out_specs=[row, vec, vec],
        out_shape=[jax.ShapeDtypeStruct((s, D), F32), jax.ShapeDtypeStruct((1, D), F32),
                   jax.ShapeDtypeStruct((1, D), F32)],
        compiler_params=_cparams(1))(dh, z, g, *after)


def ln2_ple_bwd(dh, z, g, pg, pp, name, after=()):
    s = dh.shape[0]
    tm = 512
    na = len(after)

    def body(dh_ref, z_ref, g_ref, pg_ref, pp_ref, *rest):
        dz_ref, dzb_ref, dpg_ref, dpp_ref, dg_ref, db_ref = rest[na:]

        @pl.when(pl.program_id(0) == 0)
        def _():
            dg_ref[...] = jnp.zeros_like(dg_ref)
            db_ref[...] = jnp.zeros_like(db_ref)
        dh = dh_ref[...]
        dz, dgx = _ln_bwd(dh, z_ref[...], g_ref[...])
        sg = _sigmoid(pg_ref[...].astype(F32))
        dz_ref[...] = dz
        dzb_ref[...] = dz.astype(BF16)
        dpg_ref[...] = (dz * pp_ref[...].astype(F32) * sg * (1.0 - sg)).astype(BF16)
        dpp_ref[...] = (dz * sg).astype(BF16)
        dg_ref[...] += _colsum(dgx)
        db_ref[...] += _colsum(dh)

    row = pl.BlockSpec((tm, D), lambda i: (i, 0))
    vec = pl.BlockSpec((1, D), lambda i: (0, 0))
    return pl.pallas_call(
        body, name=name, grid=(s // tm,), in_specs=[row, row, vec, row, row] + [ANY] * na,
        out_specs=[row, row, row, row, vec, vec],
        out_shape=[jax.ShapeDtypeStruct((s, D), F32)] + [jax.ShapeDtypeStruct((s, D), BF16)] * 3
        + [jax.ShapeDtypeStruct((1, D), F32)] * 2,
        compiler_params=_cparams(1))(dh, z, g, pg, pp, *after)


def wgrad_rows(a, dy, name, after=()):
    s, k = a.shape
    n = dy.shape[1]
    kb = k // N_DEV

    def body(a_ref, dy_ref, *rest):
        rest[-1][...] = _tn(a_ref[...], dy_ref[...]).astype(BF16)

    return pl.pallas_call(
        body, name=name, grid=(N_DEV,),
        in_specs=[pl.BlockSpec((s, kb), lambda j: (0, j)), pl.BlockSpec((s, n), lambda j: (0, 0))] + [ANY] * len(after),
        out_specs=pl.BlockSpec((None, kb, n), lambda j: (j, 0, 0)),
        out_shape=jax.ShapeDtypeStruct((N_DEV, kb, n), BF16),
        compiler_params=_cparams(1))(a, dy, *after)


def wgrad_cols(a, dy, name, with_colsum=False):
    s, k = a.shape
    n = dy.shape[1]
    nb = n // N_DEV

    def body(a_ref, dy_ref, o_ref, *cs_ref):
        dy = dy_ref[...]
        o_ref[...] = _tn(a_ref[...], dy).astype(BF16)
        if with_colsum:
            cs_ref[0][...] = _colsum(dy.astype(F32))

    out_specs = [pl.BlockSpec((None, k, nb), lambda j: (j, 0, 0))]
    out_shape = [jax.ShapeDtypeStruct((N_DEV, k, nb), BF16)]
    if with_colsum:
        out_specs.append(pl.BlockSpec((1, nb), lambda j: (0, j)))
        out_shape.append(jax.ShapeDtypeStruct((1, n), F32))
    res = pl.pallas_call(
        body, name=name, grid=(N_DEV,),
        in_specs=[pl.BlockSpec((s, k), lambda j: (0, 0)), pl.BlockSpec((s, nb), lambda j: (0, j))],
        out_specs=out_specs, out_shape=out_shape,
        compiler_params=_cparams(1))(a, dy)
    return res if with_colsum else res[0]


def wgrad_down(t, dy, name):
    _, s, k = t.shape
    n = dy.shape[1]

    def body(a_ref, dy_ref, o_ref):
        o_ref[...] = _tn(a_ref[...], dy_ref[...]).astype(BF16)

    return pl.pallas_call(
        body, name=name, grid=(4,),
        in_specs=[pl.BlockSpec((None, s, k), lambda j: (j, 0, 0)), pl.BlockSpec((s, n), lambda j: (0, 0))],
        out_specs=pl.BlockSpec((None, k, n), lambda j: (j, 0, 0)),
        out_shape=jax.ShapeDtypeStruct((4, k, n), BF16),
        compiler_params=_cparams(1))(t, dy)


def wgrad_up(a, dhv, dhg, name):
    s, k = a.shape

    def body(a_ref, dv_ref, dg_ref, o_ref):
        j = pl.program_id(0)

        @pl.when(j < 4)
        def _():
            o_ref[...] = _tn(dv_ref[...], a_ref[...]).astype(BF16)

        @pl.when(j >= 4)
        def _():
            o_ref[...] = _tn(dg_ref[...], a_ref[...]).astype(BF16)

    return pl.pallas_call(
        body, name=name, grid=(N_DEV,),
        in_specs=[pl.BlockSpec((s, k), lambda j: (0, 0)),
                  pl.BlockSpec((None, s, FF_BLK), lambda j: (jnp.minimum(j, 3), 0, 0)),
                  pl.BlockSpec((None, s, FF_BLK), lambda j: (jnp.maximum(j - 4, 0), 0, 0))],
        out_specs=pl.BlockSpec((None, FF_BLK, k), lambda j: (j, 0, 0)),
        out_shape=jax.ShapeDtypeStruct((N_DEV, FF_BLK, k), BF16),
        compiler_params=_cparams(1))(a, dhv, dhg)


def ffn_act_bwd(dzb, wdown, up, conv_w, conv_b, li, name):
    s = up.shape[1]
    tm = 512
    nt = s // tm
    before, after = _halo_maps(tm, s)
    hv_main, hv_prev, hv_next = _slab_specs(tm, s, lambda c: c)
    hg_main, hg_prev, hg_next = _slab_specs(tm, s, lambda c: 4 + c)

    def dc_of(dz, wd, hv, hg, back, fwd, cw_ref, cb_ref):
        dt = _nt(dz, wd)
        c = back * cw_ref[0:1, :] + hg * cw_ref[1:2, :] + fwd * cw_ref[2:3, :] + cb_ref[...]
        cdf = 0.5 * (1.0 + lax.erf(c * _SQRT_HALF))
        pdf = jnp.exp(-0.5 * c * c) * _INV_SQRT_2PI
        return dt, c * cdf, dt * hv * (cdf + c * pdf)

    def body(dz_ref, dzp_ref, dzn_ref, wd_ref, hv_ref, hvp_ref, hvn_ref, hg_ref, hgp_ref, hgn_ref, cw_ref, cb_ref,
             dhv_ref, dhg_ref, dcw_ref, dcb_ref):
        i = pl.program_id(1)

        @pl.when(i == 0)
        def _():
            dcw_ref[...] = jnp.zeros_like(dcw_ref)
            dcb_ref[...] = jnp.zeros_like(dcb_ref)

        wd = _row_cat(wd_ref)
        hg = hg_ref[...].astype(F32)
        hgp = hgp_ref[...].astype(F32)
        hgn = hgn_ref[...].astype(F32)
        first, last = i == 0, i == nt - 1
        e = HALO - 1
        back, fwd = _shift_rows(hg, jnp.where(first, 0.0, hgp[e:e + 1, :]), jnp.where(last, 0.0, hgn[0:1, :]))
        dt, act, dc = dc_of(dz_ref[...], wd, hv_ref[...].astype(F32), hg, back, fwd, cw_ref, cb_ref)
        dhv_ref[...] = (dt * act).astype(BF16)
        bp, fp = _shift_rows(hgp, hgp[0:1, :], hg[0:1, :])
        _, _, dcp = dc_of(dzp_ref[...], wd, hvp_ref[...].astype(F32), hgp, bp, fp, cw_ref, cb_ref)
        bn, fn = _shift_rows(hgn, hg[tm - 1:tm, :], hgn[e:e + 1, :])
        _, _, dcn = dc_of(dzn_ref[...], wd, hvn_ref[...].astype(F32), hgn, bn, fn, cw_ref, cb_ref)
        dc_back, dc_fwd = _shift_rows(dc, jnp.where(first, 0.0, dcp[e:e + 1, :]), jnp.where(last, 0.0, dcn[0:1, :]))
        dhg_ref[...] = (dc_fwd * cw_ref[0:1, :] + dc * cw_ref[1:2, :] + dc_back * cw_ref[2:3, :]).astype(BF16)
        dcw_ref[0:1, :] += _colsum(dc * back)
        dcw_ref[1:2, :] += _colsum(dc * hg)
        dcw_ref[2:3, :] += _colsum(dc * fwd)
        dcb_ref[...] += _colsum(dc)

    out_slab = pl.BlockSpec((None, tm, FF_BLK), lambda c, i: (c, i, 0))
    cw_spec = pl.BlockSpec((None, 3, FF_BLK), lambda c, i: (c, 0, 0))
    cb_spec = pl.BlockSpec((None, 1, FF_BLK), lambda c, i: (c, 0, 0))
    return pl.pallas_call(
        body, name=name, grid=(4, nt),
        in_specs=[pl.BlockSpec((tm, D), lambda c, i: (i, 0)),
                  pl.BlockSpec((HALO, D), lambda c, i: (before(i), 0)),
                  pl.BlockSpec((HALO, D), lambda c, i: (after(i), 0)),
                  _shards(2, FF_SHARD, D, li, lambda c, i: c),
                  hv_main, hv_prev, hv_next, hg_main, hg_prev, hg_next, cw_spec, cb_spec],
        out_specs=[out_slab, out_slab, cw_spec, cb_spec],
        out_shape=[jax.ShapeDtypeStruct((4, s, FF_BLK), BF16), jax.ShapeDtypeStruct((4, s, FF_BLK), BF16),
                   jax.ShapeDtypeStruct((4, 3, FF_BLK), F32), jax.ShapeDtypeStruct((4, 1, FF_BLK), F32)],
        compiler_params=_cparams(2))(dzb, dzb, dzb, wdown, up, up, up, up, up, up, conv_w, conv_b)


def dh1_ln1_bwd(dz2, dpg, wpg, dhv, dhg, wup, z1, g1, li, name, after=()):
    s = dz2.shape[0]
    tm = 256
    na = len(after)

    def body(dz2_ref, dpg_ref, wpg_ref, dhv_ref, dhg_ref, wup_ref, z1_ref, g_ref, *rest):
        dz_ref, dzb_ref, dg_ref, db_ref = rest[na:]

        @pl.when(pl.program_id(0) == 0)
        def _():
            dg_ref[...] = jnp.zeros_like(dg_ref)
            db_ref[...] = jnp.zeros_like(db_ref)
        dh = ALPHA * dz2_ref[...] + _nt(dpg_ref[...], _row_cat(wpg_ref))
        for c in range(4):
            dh = dh + _nn(dhv_ref[c], wup_ref[c]) + _nn(dhg_ref[c], wup_ref[4 + c])
        dz, dgx = _ln_bwd(dh, z1_ref[...], g_ref[...])
        dz_ref[...] = dz
        dzb_ref[...] = dz.astype(BF16)
        dg_ref[...] += _colsum(dgx)
        db_ref[...] += _colsum(dh)

    row = pl.BlockSpec((tm, D), lambda i: (i, 0))
    vec = pl.BlockSpec((1, D), lambda i: (0, 0))
    slab = pl.BlockSpec((4, tm, FF_BLK), lambda i: (0, i, 0))
    return pl.pallas_call(
        body, name=name, grid=(s // tm,),
        in_specs=[row, row, _shards(N_DEV, D // N_DEV, D, li), slab, slab, _shards(N_DEV, FF_BLK, D, li), row, vec]
        + [ANY] * na,
        out_specs=[row, row, vec, vec],
        out_shape=[jax.ShapeDtypeStruct((s, D), F32), jax.ShapeDtypeStruct((s, D), BF16),
                   jax.ShapeDtypeStruct((1, D), F32), jax.ShapeDtypeStruct((1, D), F32)],
        compiler_params=_cparams(1))(dz2, dpg, wpg, dhv, dhg, wup, z1, g1, *after)


def merge_bwd(dz1b, wmix, proj, ya, yp, li, name):
    s = dz1b.shape[0]
    tm, tn = 512, 512
    nt = D // tn
    per = tn // (D // N_DEV)
    ga0 = (3 * D_ATTN + D_POOL) // tn

    def body(dz_ref, w_ref, ga_ref, gb_ref, ya_ref, yp_ref, dya_ref, dyp_ref, dga_ref, dgb_ref):
        dm = _nt(dz_ref[...], _row_cat(w_ref))
        sa = _sigmoid(ga_ref[...].astype(F32))
        sb = _sigmoid(gb_ref[...].astype(F32))
        dya_ref[...] = (dm * sa).astype(BF16)
        dyp_ref[...] = (dm * sb).astype(BF16)
        dga_ref[...] = (dm * ya_ref[...].astype(F32) * sa * (1.0 - sa)).astype(BF16)
        dgb_ref[...] = (dm * yp_ref[...].astype(F32) * sb * (1.0 - sb)).astype(BF16)

    tile = pl.BlockSpec((tm, tn), lambda i, j: (i, j))
    return pl.pallas_call(
        body, name=name, grid=(s // tm, nt),
        in_specs=[pl.BlockSpec((tm, D), lambda i, j: (i, 0)),
                  _shards(per, D // N_DEV, D, li, lambda i, j: j),
                  pl.BlockSpec((tm, tn), lambda i, j: (i, ga0 + j)),
                  pl.BlockSpec((tm, tn), lambda i, j: (i, ga0 + nt + j)),
                  tile, tile],
        out_specs=[tile] * 4,
        out_shape=[jax.ShapeDtypeStruct((s, D), BF16)] * 4,
        compiler_params=_cparams(2))(dz1b, wmix, proj, proj, ya, yp)


def attn_out_bwd(dya, wao, li, name, after=()):
    s = dya.shape[0]
    tm = 512

    def body(d_ref, w_ref, *rest):
        rest[-1][...] = _nt(d_ref[...], _lane_cat(w_ref)).astype(BF16)

    return pl.pallas_call(
        body, name=name, grid=(s // tm,),
        in_specs=[pl.BlockSpec((tm, D), lambda i: (i, 0)), _shards(N_DEV, D_ATTN, 128, li)] + [ANY] * len(after),
        out_specs=pl.BlockSpec((tm, D_ATTN), lambda i: (i, 0)),
        out_shape=jax.ShapeDtypeStruct((s, D_ATTN), BF16),
        compiler_params=_cparams(1))(dya, wao, *after)


def pool_bwd(dyp, wpo, pm, pool_w, pool_scale, li, name):
    s = dyp.shape[0]

    def body(dyp_ref, wpo_ref, pm_ref, w_ref, sc_ref, du_ref, dw_ref, dsc_ref):
        wpo = _lane_cat(wpo_ref)
        dyp = dyp_ref[...]
        for g, w in enumerate(POOL_WINDOWS):
            cols = slice(g * PGD, (g + 1) * PGD)
            dpw = _nt(dyp, wpo[g * PGD:(g + 1) * PGD, :])
            pmg = pm_ref[:, cols]
            dsc_ref[:, cols] = _colsum(dpw * _nn(pmg, w_ref[g]))
            dpmw = (dpw * sc_ref[:, cols]).astype(BF16)
            dw_ref[g] = _tn(pmg, dpmw)
            dpm = _nt(dpmw, w_ref[g])
            du_ref[:, cols] = (_window_sum(dpm / _pool_counts(s, w), w, False) - dpm).astype(BF16)

    full = lambda shape: pl.BlockSpec(shape, lambda i: (0,) * len(shape))
    return pl.pallas_call(
        body, name=name, grid=(1,),
        in_specs=[full((s, D)), _shards(N_DEV, D_POOL, 128, li), full((s, D_POOL)), full((4, PGD, PGD)),
                  full((1, D_POOL))],
        out_specs=[full((s, D_POOL)), full((4, PGD, PGD)), full((1, D_POOL))],
        out_shape=[jax.ShapeDtypeStruct((s, D_POOL), BF16), jax.ShapeDtypeStruct((4, PGD, PGD), F32),
                   jax.ShapeDtypeStruct((1, D_POOL), F32)],
        compiler_params=_cparams(1))(dyp, wpo, pm, pool_w, pool_scale)


def attn_bwd(proj, da, e_rev, name, after=()):
    s = proj.shape[0]
    nb = s // QB
    skew = GRID_W + (GRID_W - KW)

    def body(q_ref, k_ref, v_ref, do_ref, e_ref, *rest):
        dq_ref, dk_ref, dv_ref, g_ref, s_ref, dp_ref, ds_ref, p_ref, dkt_acc, dvt_acc = rest[len(after):]
        b = pl.program_id(1)

        @pl.when(b == 0)
        def _():
            dkt_acc[...] = jnp.zeros_like(dkt_acc)
            dvt_acc[...] = jnp.zeros_like(dvt_acc)
            g_ref[...] = jnp.zeros_like(g_ref)

        ri = lax.broadcasted_iota(I32, (QB, QB), 0)
        ci = lax.broadcasted_iota(I32, (QB, QB), 1)
        rev = jnp.where(ri + ci == QB - 1, 1.0, 0.0).astype(BF16)
        q = _nn(rev, q_ref[...]).astype(BF16) * ATT_SCALE
        do = _nn(rev, do_ref[...]).astype(BF16)
        lane = lax.broadcasted_iota(I32, (1, 128), 1)

        def block(btype, k0):
            kwin = k_ref[pl.ds(k0, KB), :]
            vwin = v_ref[pl.ds(k0, KB), :]
            dq = jnp.zeros((QB, 128), F32)
            for hh in range(2):
                lm = (lane // HEAD_DIM) == hh
                qh = jnp.where(lm, q, jnp.zeros_like(q))
                doh = jnp.where(lm, do, jnp.zeros_like(do))
                kh = jnp.where(lm, kwin, jnp.zeros_like(kwin))
                s_ref[...] = _nt(qh, kwin)
                dp_ref[...] = _nt(doh, vwin)
                g = jnp.zeros((1, KB), F32)
                for ib in range(QROWS):
                    qr = QROWS - 1 - ib
                    rows = slice(ib * GRID_W, (ib + 1) * GRID_W)
                    sb, a0, w, pad = _row_logits(s_ref, e_ref, hh, rows, btype, qr)
                    p = jnp.exp(sb - jnp.max(sb, axis=1, keepdims=True))
                    p = p * (1.0 / jnp.sum(p, axis=1, keepdims=True))
                    dp = dp_ref[rows, a0:a0 + w]
                    ds = p * (dp - jnp.sum(p * dp, axis=1, keepdims=True))
                    _store_row(ds_ref, rows, a0, w, ds)
                    _store_row(p_ref, rows, a0, w, p)
                    t = jnp.sum(pltpu.roll(ds, w - skew, 1, stride=1, stride_axis=0), axis=0, keepdims=True)
                    t = t[:, :KH * GRID_W] if pad else pltpu.roll(t, GRID_W, 1)
                    i0 = _attn_row(btype, qr)[2]
                    g = g + pltpu.roll(jnp.concatenate([t, jnp.zeros_like(t)], axis=1), i0 * GRID_W, 1)
                g_ref[hh] += g
                dsb = ds_ref[...]
                dq = dq + _nn(dsb, kh) * ATT_SCALE
                dkt_acc[:, pl.ds(k0, KB)] += _tn(qh, dsb)
                dvt_acc[:, pl.ds(k0, KB)] += _tn(doh, p_ref[...])
            dq_ref[...] = _nn(rev, dq.astype(BF16)).astype(BF16)

        for btype, (cond, k0) in enumerate(_attn_types(b, nb)):
            pl.when(cond)(lambda btype=btype, k0=k0: block(btype, k0))

        @pl.when(b == nb - 1)
        def _():
            dk_ref[...] = dkt_acc[...].T.astype(BF16)
            dv_ref[...] = dvt_acc[...].T.astype(BF16)

    col = pl.BlockSpec((s, 128), lambda j, b: (0, j))
    return pl.pallas_call(
        body, name=name, grid=(4, nb),
        in_specs=[pl.BlockSpec((QB, 128), lambda j, b: (b, j)),
                  pl.BlockSpec((s, 128), lambda j, b: (0, 4 + j)),
                  pl.BlockSpec((s, 128), lambda j, b: (0, 8 + j)),
                  pl.BlockSpec((QB, 128), lambda j, b: (b, j)),
                  pl.BlockSpec((2, 2, GRID_W, KB), lambda j, b: (j, 0, 0, 0))] + [ANY] * len(after),
        out_specs=[pl.BlockSpec((QB, 128), lambda j, b: (b, j)), col, col,
                   pl.BlockSpec((2, 1, KB), lambda j, b: (j, 0, 0))],
        out_shape=[jax.ShapeDtypeStruct((s, D_ATTN), BF16)] * 3 + [jax.ShapeDtypeStruct((N_HEADS, 1, KB), F32)],
        scratch_shapes=[pltpu.VMEM((QB, KB), F32), pltpu.VMEM((QB, KB), F32), pltpu.VMEM((QB, KB), BF16),
                        pltpu.VMEM((QB, KB), BF16), pltpu.VMEM((128, s), F32), pltpu.VMEM((128, s), F32)],
        compiler_params=_cparams(2))(proj, proj, proj, da, e_rev, *after)


def dh0_bwd(dz1, dproj, win, li, name, after=()):
    s = dz1.shape[0]
    tm = 256
    bn = N_PROJ // N_DEV

    def body(dz_ref, dp_ref, w_ref, *rest):
        acc = ALPHA * dz_ref[...]
        for j in range(N_DEV):
            acc = acc + _nt(dp_ref[:, j * bn:(j + 1) * bn], w_ref[j])
        rest[-1][...] = acc

    row = pl.BlockSpec((tm, D), lambda i: (i, 0))
    return pl.pallas_call(
        body, name=name, grid=(s // tm,),
        in_specs=[row, pl.BlockSpec((tm, N_PROJ), lambda i: (i, 0)), _shards(N_DEV, D, bn, li)] + [ANY] * len(after),
        out_specs=row, out_shape=jax.ShapeDtypeStruct((s, D), F32),
        compiler_params=_cparams(1))(dz1, dproj, win, *after)


def _coords():
    return lax.axis_index("x"), lax.axis_index("y"), lax.axis_index("c")


def _dev_index(px, py, pc):
    return 4 * px + 2 * py + pc


def all_gather(arrs, name):
    n = len(arrs)

    def body(*refs):
        ins, outs = refs[:n], refs[n:2 * n]
        send_sems, recv_sems, local_sems = refs[2 * n:]
        x, y, c = _coords()
        me, sibling = (x, y, c), (x, y, 1 - c)
        chips = [(1 - x, y), (x, 1 - y), (1 - x, 1 - y)]

        def copy(a, k, block, to, src=None):
            dst = outs[a].at[_dev_index(*block)]
            return pltpu.make_async_remote_copy(
                src_ref=dst if src is None else src, dst_ref=dst,
                send_sem=send_sems.at[a, k], recv_sem=recv_sems.at[a, k], device_id=to, device_id_type=MESH)

        mine = [pltpu.make_async_copy(ins[a], outs[a].at[_dev_index(*me)], local_sems.at[a]) for a in range(n)]
        for cp in mine:
            cp.start()
        first = []
        for a in range(n):
            first.append(copy(a, 0, me, sibling, src=ins[a]))
            first += [copy(a, 1 + j, me, (*chip, c), src=ins[a]) for j, chip in enumerate(chips)]
        for cp in first:
            cp.start()
        passed = []
        for j, chip in enumerate(chips):
            for a in range(n):
                copy(a, 1 + j, (*chip, c), me).wait_recv()
                cp = copy(a, 4 + j, (*chip, c), sibling)
                cp.start()
                passed.append(cp)
        for a in range(n):
            copy(a, 0, sibling, me).wait_recv()
            for j, chip in enumerate(chips):
                copy(a, 4 + j, (*chip, 1 - c), me).wait_recv()
        for cp in first + passed:
            cp.wait_send()
        for cp in mine:
            cp.wait()

    return pl.pallas_call(
        body, name=name,
        out_shape=[jax.ShapeDtypeStruct((N_DEV,) + a.shape, a.dtype) for a in arrs],
        in_specs=[ANY] * n, out_specs=[ANY] * n,
        scratch_shapes=[pltpu.SemaphoreType.DMA((n, 7)), pltpu.SemaphoreType.DMA((n, 7)),
                        pltpu.SemaphoreType.DMA((n,))],
    )(*arrs)


def exchange_sibling(grads, name):
    n = len(grads)

    def body(*refs):
        ins, outs = refs[:n], refs[n:2 * n]
        send_sems, recv_sems = refs[2 * n:]
        x, y, c = _coords()
        sibling = (x, y, 1 - c)
        copies = []
        for a in range(n):
            for k in range(4):
                blk = _dev_index(x ^ (k & 1), y ^ (k >> 1), 1 - c)
                copies.append(pltpu.make_async_remote_copy(
                    src_ref=ins[a].at[blk], dst_ref=outs[a].at[k],
                    send_sem=send_sems.at[a, k], recv_sem=recv_sems.at[a, k], device_id=sibling, device_id_type=MESH))
        for cp in copies:
            cp.start()
        for cp in copies:
            cp.wait()

    return pl.pallas_call(
        body, name=name,
        out_shape=[jax.ShapeDtypeStruct((4,) + g.shape[1:], g.dtype) for g in grads],
        in_specs=[ANY] * n, out_specs=[ANY] * n,
        scratch_shapes=[pltpu.SemaphoreType.DMA((n, 4)), pltpu.SemaphoreType.DMA((n, 4))],
    )(*grads)


HBM = pl.BlockSpec(memory_space=pltpu.HBM)
SEM = pl.BlockSpec(memory_space=pltpu.SEMAPHORE)
_EFFECT = pltpu.SideEffectType.DATAFLOW_SIDE_EFFECTING
_TOKEN = jax.ShapeDtypeStruct((8, 128), F32)


def _in_hbm(a):
    return pltpu.with_memory_space_constraint(a, pltpu.HBM)


def _hbm_like(a):
    return pltpu.HBM(a.shape, a.dtype)


def _peers(x, y, c):
    return [(x, y, 1 - c), (1 - x, y, c), (x, 1 - y, c), (1 - x, 1 - y, c)]


def ag_start(lands, after, name):
    n = len(lands)

    def body(*refs):
        land = refs[:n]
        send_sem, recv_sem, token = refs[n + 1], refs[n + 2], refs[-1]
        x, y, c = _coords()
        me = _dev_index(x, y, c)
        for k, peer in enumerate(_peers(x, y, c)):
            for a in range(n):
                pltpu.make_async_remote_copy(src_ref=land[a].at[me], dst_ref=land[a].at[me], send_sem=send_sem.at[k],
                                             recv_sem=recv_sem.at[k], device_id=peer, device_id_type=MESH).start()
        token[...] = jnp.zeros_like(token)

    res = pl.pallas_call(
        body, name=name,
        out_shape=(pltpu.SemaphoreType.DMA((4,)), pltpu.SemaphoreType.DMA((4,)), *[_hbm_like(l) for l in lands], _TOKEN),
        in_specs=[HBM] * n + [ANY], out_specs=(SEM, SEM, *[HBM] * n, pl.BlockSpec(memory_space=pltpu.VMEM)),
        input_output_aliases={a: 2 + a for a in range(n)},
        compiler_params=pltpu.CompilerParams(has_side_effects=_EFFECT),
    )(*[_in_hbm(l) for l in lands], after)
    return res[0], res[1], list(res[2:2 + n]), res[-1]


def ag_forward(send_sem, recv_sem, lands, after, name):
    n = len(lands)

    def body(*refs):
        send_sem, recv_sem = refs[0], refs[1]
        land = refs[2:2 + n]
        fsend, frecv = refs[3 + n], refs[4 + n]
        x, y, c = _coords()
        peers = _peers(x, y, c)
        for k in range(1, 4):
            blk = _dev_index(*peers[k])
            for a in range(n):
                pltpu.make_async_remote_copy(src_ref=land[a].at[blk], dst_ref=land[a].at[blk], send_sem=send_sem.at[k],
                                             recv_sem=recv_sem.at[k], device_id=peers[k], device_id_type=MESH).wait_recv()
        for k in range(1, 4):
            blk = _dev_index(*peers[k])
            for a in range(n):
                pltpu.make_async_remote_copy(src_ref=land[a].at[blk], dst_ref=land[a].at[blk], send_sem=fsend.at[k - 1],
                                             recv_sem=frecv.at[k - 1], device_id=peers[0], device_id_type=MESH).start()

    res = pl.pallas_call(
        body, name=name,
        out_shape=(pltpu.SemaphoreType.DMA((3,)), pltpu.SemaphoreType.DMA((3,)), *[_hbm_like(l) for l in lands]),
        in_specs=[SEM, SEM, *[HBM] * n, ANY], out_specs=(SEM, SEM, *[HBM] * n),
        input_output_aliases={2 + a: 2 + a for a in range(n)},
        compiler_params=pltpu.CompilerParams(has_side_effects=_EFFECT),
    )(send_sem, recv_sem, *lands, after)
    return res[0], res[1], list(res[2:])


def ag_finish(send_sem, recv_sem, fsend, frecv, lands, after, name):
    n = len(lands)

    def body(*refs):
        send_sem, recv_sem, fsend, frecv = refs[:4]
        land = refs[4:4 + n]
        x, y, c = _coords()
        me = _dev_index(x, y, c)
        peers = _peers(x, y, c)
        for k in range(4):
            for a in range(n):
                pltpu.make_async_remote_copy(src_ref=land[a].at[me], dst_ref=land[a].at[me], send_sem=send_sem.at[k],
                                             recv_sem=recv_sem.at[k], device_id=peers[k], device_id_type=MESH).wait_send()
        sib = _dev_index(*peers[0])
        for a in range(n):
            pltpu.make_async_remote_copy(src_ref=land[a].at[sib], dst_ref=land[a].at[sib], send_sem=send_sem.at[0],
                                         recv_sem=recv_sem.at[0], device_id=peers[0], device_id_type=MESH).wait_recv()
        for k in range(1, 4):
            mine = _dev_index(*peers[k])
            theirs = _dev_index(peers[k][0], peers[k][1], 1 - c)
            for a in range(n):
                pltpu.make_async_remote_copy(src_ref=land[a].at[mine], dst_ref=land[a].at[theirs], send_sem=fsend.at[k - 1],
                                             recv_sem=frecv.at[k - 1], device_id=peers[0], device_id_type=MESH).wait()

    res = pl.pallas_call(
        body, name=name, out_shape=tuple(_hbm_like(l) for l in lands),
        in_specs=[SEM] * 4 + [HBM] * n + [ANY], out_specs=tuple([HBM] * n),
        input_output_aliases={4 + a: a for a in range(n)},
        compiler_params=pltpu.CompilerParams(has_side_effects=_EFFECT),
    )(send_sem, recv_sem, fsend, frecv, *lands, after)
    return list(res)


def rs_start(psums, name):
    n = len(psums)
    lands = [lax.empty(p.shape, p.dtype) for p in psums]

    def body(*refs):
        src, land = refs[:n], refs[n:2 * n]
        send_sem, recv_sem, token = refs[2 * n], refs[2 * n + 1], refs[-1]
        peers = _peers(*_coords())
        for k in range(3):
            for a in range(n):
                pltpu.make_async_remote_copy(src_ref=src[a].at[k], dst_ref=land[a].at[k], send_sem=send_sem.at[k],
                                             recv_sem=recv_sem.at[k], device_id=peers[k + 1], device_id_type=MESH).start()
        token[...] = jnp.zeros_like(token)

    res = pl.pallas_call(
        body, name=name,
        out_shape=(pltpu.SemaphoreType.DMA((3,)), pltpu.SemaphoreType.DMA((3,)), *[_hbm_like(p) for p in psums],
                   *[_hbm_like(l) for l in lands], _TOKEN),
        in_specs=[HBM] * (2 * n), out_specs=(SEM, SEM, *[HBM] * (2 * n), pl.BlockSpec(memory_space=pltpu.VMEM)),
        input_output_aliases={a: 2 + a for a in range(2 * n)},
        compiler_params=pltpu.CompilerParams(has_side_effects=_EFFECT),
    )(*[_in_hbm(p) for p in psums], *[_in_hbm(l) for l in lands])
    return res[0], res[1], list(res[2:2 + n]), list(res[2 + n:2 + 2 * n]), res[-1]


def rs_finish(send_sem, recv_sem, psums, lands, after, name):
    n = len(psums)

    def body(*refs):
        send_sem, recv_sem = refs[0], refs[1]
        src, land = refs[2:2 + n], refs[2 + n:2 + 2 * n]
        peers = _peers(*_coords())
        for k in range(3):
            for a in range(n):
                pltpu.make_async_remote_copy(src_ref=src[a].at[k], dst_ref=land[a].at[k], send_sem=send_sem.at[k],
                                             recv_sem=recv_sem.at[k], device_id=peers[k + 1], device_id_type=MESH).wait()

    res = pl.pallas_call(
        body, name=name, out_shape=tuple(_hbm_like(l) for l in lands),
        in_specs=[SEM, SEM] + [HBM] * (2 * n) + [ANY], out_specs=tuple([HBM] * n),
        input_output_aliases={2 + n + a: a for a in range(n)},
        compiler_params=pltpu.CompilerParams(has_side_effects=_EFFECT),
    )(send_sem, recv_sem, *psums, *lands, after)
    return list(res)


def d2d_start(grads, name):
    n = len(grads)
    lands = [lax.empty((4,) + g.shape[1:], g.dtype) for g in grads]

    def body(*refs):
        src, land = refs[:n], refs[n:2 * n]
        send_sem, recv_sem, token = refs[2 * n], refs[2 * n + 1], refs[-1]
        x, y, c = _coords()
        for a in range(n):
            for k in range(4):
                blk = _dev_index(x ^ (k & 1), y ^ (k >> 1), 1 - c)
                pltpu.make_async_remote_copy(src_ref=src[a].at[blk], dst_ref=land[a].at[k], send_sem=send_sem.at[0],
                                             recv_sem=recv_sem.at[0], device_id=(x, y, 1 - c), device_id_type=MESH).start()
        token[...] = jnp.zeros_like(token)

    res = pl.pallas_call(
        body, name=name,
        out_shape=(pltpu.SemaphoreType.DMA((1,)), pltpu.SemaphoreType.DMA((1,)), *[_hbm_like(g) for g in grads],
                   *[_hbm_like(l) for l in lands], _TOKEN),
        in_specs=[HBM] * (2 * n), out_specs=(SEM, SEM, *[HBM] * (2 * n), pl.BlockSpec(memory_space=pltpu.VMEM)),
        input_output_aliases={a: 2 + a for a in range(2 * n)},
        compiler_params=pltpu.CompilerParams(has_side_effects=_EFFECT),
    )(*[_in_hbm(g) for g in grads], *[_in_hbm(l) for l in lands])
    return res[0], res[1], list(res[2:2 + n]), list(res[2 + n:2 + 2 * n]), res[-1]


def d2d_finish(send_sem, recv_sem, grads, lands, after, name):
    n = len(grads)

    def body(*refs):
        send_sem, recv_sem = refs[0], refs[1]
        src, land = refs[2:2 + n], refs[2 + n:2 + 2 * n]
        x, y, c = _coords()
        for a in range(n):
            for k in range(4):
                blk = _dev_index(x ^ (k & 1), y ^ (k >> 1), 1 - c)
                pltpu.make_async_remote_copy(src_ref=src[a].at[blk], dst_ref=land[a].at[k], send_sem=send_sem.at[0],
                                             recv_sem=recv_sem.at[0], device_id=(x, y, 1 - c), device_id_type=MESH).wait()

    res = pl.pallas_call(
        body, name=name, out_shape=tuple(_hbm_like(t) for t in list(grads) + list(lands)),
        in_specs=[SEM, SEM] + [HBM] * (2 * n) + [ANY], out_specs=tuple([HBM] * (2 * n)),
        input_output_aliases={2 + a: a for a in range(2 * n)},
        compiler_params=pltpu.CompilerParams(has_side_effects=_EFFECT),
    )(send_sem, recv_sem, *grads, *lands, after)
    return list(res[:n]), list(res[n:])


def _row_tile(r):
    return next(t for t in (512, 352, 256, 128) if r % t == 0)


def pair_add(blk_idx, g, recv, name):
    _, r, c = g.shape
    tr = _row_tile(r)

    def body(idx_ref, g_ref, r_ref, own_ref, oth_ref):
        k = pl.program_id(1)
        sm = g_ref[...].astype(F32) + r_ref[...].astype(F32)

        @pl.when(k == 0)
        def _():
            own_ref[...] = sm

        @pl.when(k > 0)
        def _():
            oth_ref[...] = sm.astype(BF16)

    grid_spec = pltpu.PrefetchScalarGridSpec(
        num_scalar_prefetch=1, grid=(r // tr, 4),
        in_specs=[pl.BlockSpec((None, tr, c), lambda t, k, idx: (idx[k], t, 0)),
                  pl.BlockSpec((None, tr, c), lambda t, k, idx: (k, t, 0))],
        out_specs=[pl.BlockSpec((tr, c), lambda t, k, idx: (t, 0)),
                   pl.BlockSpec((None, tr, c), lambda t, k, idx: (jnp.maximum(k - 1, 0), t, 0))])
    return pl.pallas_call(
        body, name=name, grid_spec=grid_spec,
        out_shape=[jax.ShapeDtypeStruct((r, c), F32), jax.ShapeDtypeStruct((3, r, c), BF16)],
        compiler_params=_cparams(2))(blk_idx, g, recv)


def _adamw(w, g, m, v):
    m = ADAM_B1 * m + (1.0 - ADAM_B1) * g
    v = ADAM_B2 * v + (1.0 - ADAM_B2) * (g * g)
    m_hat = m / (1.0 - ADAM_B1 ** ADAM_STEP)
    v_hat = v / (1.0 - ADAM_B2 ** ADAM_STEP)
    delta = -ADAM_LR * (m_hat / (jnp.sqrt(v_hat) + ADAM_EPS) + ADAM_WD * w)
    return delta, m, v


def adamw_shard(own, recv, w, m, v, li, prev, name):
    r, c = own.shape
    tr = _row_tile(r)

    def body(own_ref, recv_ref, w_ref, m_ref, v_ref, p0, p1, p2, p3, g_ref, d_ref, nm_ref, nv_ref):
        g = own_ref[...] + recv_ref[0].astype(F32) + recv_ref[1].astype(F32) + recv_ref[2].astype(F32)
        delta, nm, nv = _adamw(w_ref[...], g, m_ref[...], v_ref[...])
        g_ref[...] = g
        d_ref[...] = delta
        nm_ref[...] = nm
        nv_ref[...] = nv

    lay = pl.BlockSpec((None, tr, c), lambda t: (li, t, 0))
    stack = jax.ShapeDtypeStruct((DEPTH, r, c), F32)
    return pl.pallas_call(
        body, name=name, grid=(r // tr,),
        in_specs=[pl.BlockSpec((tr, c), lambda t: (t, 0)), pl.BlockSpec((3, tr, c), lambda t: (0, t, 0)),
                  lay, lay, lay, ANY, ANY, ANY, ANY],
        out_specs=[lay] * 4, out_shape=[stack] * 4,
        input_output_aliases={5: 0, 6: 1, 7: 2, 8: 3},
        compiler_params=_cparams(1))(own, recv, w, m, v, *prev)


def adamw_replicated(gathered, w, m, v, name):
    _, r, c = gathered.shape
    tr = 88

    def body(gs_ref, w_ref, m_ref, v_ref, g_ref, d_ref, nm_ref, nv_ref):
        g = gs_ref[0]
        for d in range(1, N_DEV):
            g = g + gs_ref[d]
        delta, nm, nv = _adamw(w_ref[...], g, m_ref[...], v_ref[...])
        g_ref[...] = g
        d_ref[...] = delta
        nm_ref[...] = nm
        nv_ref[...] = nv

    row = pl.BlockSpec((tr, c), lambda t: (t, 0))
    return pl.pallas_call(
        body, name=name, grid=(r // tr,),
        in_specs=[pl.BlockSpec((N_DEV, tr, c), lambda t: (0, t, 0)), row, row, row],
        out_specs=[row] * 4, out_shape=[jax.ShapeDtypeStruct((r, c), F32)] * 4,
        compiler_params=_cparams(1))(gathered, w, m, v)


def adamw_plain(g, w, m, v, name):
    def body(g_ref, w_ref, m_ref, v_ref, d_ref, nm_ref, nv_ref):
        delta, nm, nv = _adamw(w_ref[...], g_ref[...], m_ref[...], v_ref[...])
        d_ref[...] = delta
        nm_ref[...] = nm
        nv_ref[...] = nv

    return pl.pallas_call(body, name=name, out_shape=[jax.ShapeDtypeStruct(w.shape, F32)] * 3)(g, w, m, v)


_PACK = (("ln_in_g", (D,)), ("ln_in_b", (D,)), ("b_in", (DEPTH, N_PROJ)), ("rpb", (DEPTH, N_HEADS, 2 * KH - 1, 2 * KW - 1)),
         ("pool_w", (DEPTH, 4, PGD, PGD)), ("pool_scale", (DEPTH, D_POOL)), ("ln1_g", (DEPTH, D)), ("ln1_b", (DEPTH, D)),
         ("conv_b", (DEPTH, D_FF)), ("ln2_g", (DEPTH, D)), ("ln2_b", (DEPTH, D)), ("conv_w", (DEPTH, 3, D_FF)))
_PACK_LANES = 1024


def _pack_rows(shape):
    return -(-int(np.prod(shape)) // _PACK_LANES)


_PACK_ROWS = -(-sum(_pack_rows(s) for _, s in _PACK) // 88) * 88


def _pack(parts):
    rows = []
    for name, shape in _PACK:
        flat = parts[name].reshape(-1).astype(F32)
        rows.append(jnp.pad(flat, (0, _pack_rows(shape) * _PACK_LANES - flat.shape[0])))
    used = sum(_pack_rows(s) for _, s in _PACK)
    rows.append(jnp.zeros(((_PACK_ROWS - used) * _PACK_LANES,), F32))
    return jnp.concatenate(rows).reshape(_PACK_ROWS, _PACK_LANES)


def _unpack(packed):
    out, r0 = {}, 0
    for name, shape in _PACK:
        n = int(np.prod(shape))
        nr = _pack_rows(shape)
        out[name] = packed[r0:r0 + nr].reshape(-1)[:n].reshape(shape)
        r0 += nr
    return out


def _bias_tables(rpb_l):
    qc = np.arange(GRID_W)[:, None]
    kc = np.arange(GRID_W)[None, :]
    start = np.clip(qc - KW // 2, 0, GRID_W - KW)
    valid = (kc >= start) & (kc < start + KW)
    col = np.clip(kc - qc, -(KW - 1), KW - 1) + KW - 1
    onehot = (col.reshape(-1)[None, :] == np.arange(2 * KW - 1)[:, None]).astype(np.float32)
    rows = jnp.pad(rpb_l, ((0, 0), (0, 1), (0, 0)))
    tab = jnp.einsum("hij,jm->him", rows, jnp.asarray(onehot), precision=lax.Precision.HIGHEST)
    tab = tab.reshape(N_HEADS, KROWS, GRID_W, GRID_W).transpose(0, 2, 1, 3)
    ok = valid[None, :, None, :] & (np.arange(KROWS) < 2 * KH - 1)[None, None, :, None]
    tab = jnp.where(jnp.asarray(ok), tab, NEG_INF).reshape(N_HEADS, GRID_W, KB)
    tab = jnp.stack([tab, jnp.roll(tab, GRID_W, axis=-1)], axis=1)
    return tab, tab[:, :, ::-1, :]


_SHARDED = ("w_in", "w_attn_out", "w_pool_out", "w_mix_out", "w_up", "w_down", "w_ple_gate", "w_ple_proj")
_NAMES = ("ln_in_g", "ln_in_b", "w_in", "b_in", "rpb", "w_attn_out", "pool_w", "pool_scale", "w_pool_out", "w_mix_out",
          "ln1_g", "ln1_b", "w_up", "conv_w", "conv_b", "w_down", "w_ple_gate", "w_ple_proj", "ln2_g", "ln2_b")


def kernel(x, p, ln_in_g, ln_in_b, w_in, b_in, rpb, w_attn_out, pool_w, pool_scale, w_pool_out, w_mix_out, ln1_g, ln1_b, w_up, conv_w, conv_b, w_down, w_ple_gate, w_ple_proj, ln2_g, ln2_b, loss_target, m_ln_in_g, m_ln_in_b, m_w_in, m_b_in, m_rpb, m_w_attn_out, m_pool_w, m_pool_scale, m_w_pool_out, m_w_mix_out, m_ln1_g, m_ln1_b, m_w_up, m_conv_w, m_conv_b, m_w_down, m_w_ple_gate, m_w_ple_proj, m_ln2_g, m_ln2_b, v_ln_in_g, v_ln_in_b, v_w_in, v_b_in, v_rpb, v_w_attn_out, v_pool_w, v_pool_scale, v_w_pool_out, v_w_mix_out, v_ln1_g, v_ln1_b, v_w_up, v_conv_w, v_conv_b, v_w_down, v_w_ple_gate, v_w_ple_proj, v_ln2_g, v_ln2_b):
    a = dict(locals())
    W = {n: a[n] for n in _NAMES}
    M = {n: a["m_" + n] for n in _NAMES}
    V = {n: a["v_" + n] for n in _NAMES}
    xi, yi, ci = _coords()
    me = _dev_index(xi, yi, ci)
    x2, tgt = x[0], loss_target[0]
    pb = p[:, 0].astype(BF16)

    flip = lambda d: {**d, "w_up": d["w_up"].transpose(0, 2, 1)}
    ex = _Exchange(flip(W), flip(M), flip(V))
    loss_part, dx, parts = _local_step(x2, tgt, pb, W, ex)
    loss = lax.psum(loss_part[0, 0], AXES)
    stacks = {**ex.stacks, "w_up": [t.transpose(0, 2, 1) for t in ex.stacks["w_up"]]}

    (gath,) = all_gather([_pack(parts)], "ag_small_grads")
    zero_cw = jnp.zeros((DEPTH, 3, D_FF), F32)
    packs = [_pack({**{n: src[n] for n, _ in _PACK if n != "conv_w"}, "conv_w": zero_cw}) for src in (W, M, V)]
    outs = [_unpack(o) for o in adamw_replicated(gath, *packs, "adamw_replicated")]
    g_cw = lax.dynamic_slice_in_dim(outs[0]["conv_w"], me * FF_SHARD, FF_SHARD, axis=2)
    flat = lambda t: t.reshape(DEPTH * 3, FF_SHARD)
    cw_out = [o.reshape(DEPTH, 3, FF_SHARD) for o in
              adamw_plain(flat(g_cw), flat(conv_w), flat(m_conv_w), flat(v_conv_w), "adamw_conv_w")]
    res = []
    for k in range(4):
        d = {n: stacks[n][k] for n in _SHARDED}
        d.update({n: outs[k][n] for n, _ in _PACK if n != "conv_w"})
        d["conv_w"] = g_cw if k == 0 else cw_out[k - 1]
        res.append(d)
    return (loss, dx[None], *[res[k][n] for k in range(4) for n in _NAMES])


class _Exchange:
    GROUPS = (("w_ple_gate", "w_ple_proj", "w_down", "w_up"), ("w_mix_out", "w_attn_out", "w_pool_out"), ("w_in",))
    FIRST = ("w_in",)

    def __init__(self, W, M, V):
        self.W, self.M, self.V = W, M, V
        xi, yi, ci = _coords()
        me = _dev_index(xi, yi, ci)
        self.rel_idx = jnp.stack([_dev_index(xi ^ (k & 1), yi ^ (k >> 1), ci) for k in range(4)]).astype(I32)
        self.lands = [{n: lax.dynamic_update_index_in_dim(lax.empty((N_DEV,) + W[n].shape[1:], BF16),
                                                          W[n][li].astype(BF16), me, 0) for n in _SHARDED}
                      for li in range(DEPTH)]
        (cw,) = all_gather([W["conv_w"]], "ag_conv_w")
        self.cw = cw.transpose(1, 2, 0, 3).reshape(DEPTH, 3, 4, FF_BLK).transpose(0, 2, 1, 3)
        self.ag, self.fwd, self.rs, self.pending = {}, {}, {}, {}
        self.stacks = {n: [lax.empty((DEPTH,) + W[n].shape[1:], F32) for _ in range(4)] for n in _SHARDED}
        self.late = tuple(n for n in _SHARDED if n not in self.FIRST)
        self.ag[0] = ag_start([self.lands[0][n] for n in self.FIRST], cw, "ag_start0")

    def tokens(self):
        return [self.ag[0][3]]

    def prefetch(self, li, after):
        send, recv, lands, _ = self.ag[li]
        self.fwd[li] = ag_forward(send, recv, lands, after, f"ag_forward{li}")
        if li == 0:
            self.ag["0b"] = ag_start([self.lands[0][n] for n in self.late], self.fwd[0][2][0], "ag_start0b")

    def weights(self, li, after):
        send, recv, _, _ = self.ag.pop(li)
        fsend, frecv, lands = self.fwd.pop(li)
        lands = ag_finish(send, recv, fsend, frecv, lands, after, f"ag_finish{li}")
        if li == 0:
            return dict(zip(self.FIRST, lands)), self.cw[li], (self.ag["0b"][3],)
        tokens = ()
        if li + 1 < DEPTH:
            self.ag[li + 1] = ag_start([self.lands[li + 1][n] for n in _SHARDED], lands[0], f"ag_start{li + 1}")
            tokens = (self.ag[li + 1][3],)
        return dict(zip(_SHARDED, lands)), self.cw[li], tokens

    def rest(self, li, G, mid, after):
        if li != 0:
            return G, ()
        send, recv, lands, _ = self.ag.pop("0b")
        fsend, frecv, lands = ag_forward(send, recv, lands, mid, "ag_forward0b")
        lands = ag_finish(send, recv, fsend, frecv, lands, after, "ag_finish0b")
        self.ag[1] = ag_start([self.lands[1][n] for n in _SHARDED], lands[0], "ag_start1")
        return {**G, **dict(zip(self.late, lands))}, (self.ag[1][3],)

    def grads(self, li, group, gw):
        self.pending.setdefault(li, {}).update(gw)
        if li != 0 and group != len(self.GROUPS) - 1:
            return None
        gw = self.pending.pop(li)
        tag = f"{li}_{group}" if li == 0 else f"{li}"
        send, recv, glist, lands, token = d2d_start(list(gw.values()), f"d2d_start{tag}")
        self.d2d = (tag, tuple(gw), send, recv, glist, lands)
        return token

    def flush(self, li, group, after):
        if li != 0 and group != len(self.GROUPS) - 1:
            return None
        tag, names, send, recv, glist, lands = self.d2d
        glist, recv1 = d2d_finish(send, recv, glist, lands, after, f"d2d_finish{tag}")
        sums = [pair_add(self.rel_idx, g, r1, f"pair_add_{n}{li}") for n, g, r1 in zip(names, glist, recv1)]
        send, recv, psums, lands, token = rs_start([s_[1] for s_ in sums], f"rs_start{tag}")
        self.rs.setdefault(li, []).append((tag, names, send, recv, psums, lands, [s_[0] for s_ in sums]))
        return token

    def update(self, li, after):
        for tag, names, send, recv, psums, lands, owns in self.rs.pop(li):
            recv2 = rs_finish(send, recv, psums, lands, after, f"rs_finish{tag}")
            for n, own, r2 in zip(names, owns, recv2):
                self.stacks[n] = adamw_shard(own, r2, self.W[n], self.M[n], self.V[n], li, self.stacks[n],
                                             f"adamw_{n}{li}")


def _local_step(x2, tgt, pb, W, ex):
    depth = W["rpb"].shape[0]
    vec = lambda t: t.reshape(1, -1)
    ln1_g, ln1_b, ln2_g, ln2_b = W["ln1_g"], W["ln1_b"], W["ln2_g"], W["ln2_b"]
    b_in, rpb, pool_scale = W["b_in"], W["rpb"], W["pool_scale"]
    cb_full = W["conv_b"].reshape(depth, 4, 1, FF_BLK)
    pool_w_b = W["pool_w"].astype(BF16)

    h, hb = ln_fwd(x2, vec(W["ln_in_g"]), vec(W["ln_in_b"]), "ln_in", after=ex.tokens())
    ex.prefetch(0, hb)
    saved = []
    for li in range(depth):
        G, cw, tokens = ex.weights(li, hb)
        e_tab, e_rev = _bias_tables(rpb[li])
        bias = vec(b_in[li])
        proj = proj_fwd(hb, G["w_in"], bias, li, 0, N_DEV, BF16, f"proj{li}", after=tokens)
        u = proj_fwd(hb, G["w_in"], bias, li, 3, 1, F32, f"proj_u{li}")
        att = attn_fwd(proj, e_tab, f"attn{li}")
        pm, pw = pool_fwd(u, pool_w_b[li], vec(pool_scale[li]), f"pool{li}")
        G, tokens = ex.rest(li, G, att, pw)
        mg, ya, yp = merge_fwd(att, pw, G["w_attn_out"], G["w_pool_out"], proj, li, f"merge{li}", after=tokens)
        if li + 1 < depth:
            ex.prefetch(li + 1, mg)
        z1, h1, h1b = mix_ln_fwd(mg, G["w_mix_out"], h, vec(ln1_g[li]), vec(ln1_b[li]), li, f"mix_ln{li}")
        up = up_fwd(h1b, G["w_up"], li, f"up{li}")
        t = ffn_act_fwd(up, cw, cb_full[li], f"ffn_act{li}")
        z2, h2, h2b, pg, pp = down_ple_ln_fwd(t, G["w_down"], h1b, G["w_ple_gate"], pb[li], G["w_ple_proj"], h1,
                                              vec(ln2_g[li]), vec(ln2_b[li]), li, f"down_ln{li}")
        saved.append(dict(hb=hb, proj=proj, att=att, pm=pm, pw=pw, mg=mg, ya=ya, yp=yp, z1=z1, h1b=h1b, up=up, t=t,
                          z2=z2, pg=pg, pp=pp, e_rev=e_rev, G=G, cw=cw))
        h, hb = h2, h2b

    dh, loss_part = loss_bwd(h, tgt, "loss")
    small = {n: [None] * depth for n in ("b_in", "rpb", "pool_w", "pool_scale", "ln1_g", "ln1_b", "conv_b", "ln2_g",
                                         "ln2_b", "conv_w")}
    token = ()
    tok = lambda t: () if t is None else (t,)
    for li in reversed(range(depth)):
        sv = saved[li]
        G, cw = sv["G"], sv["cw"]
        dz2, dz2b, dpg, dpp, dg2, db2 = ln2_ple_bwd(dh, sv["z2"], vec(ln2_g[li]), sv["pg"], sv["pp"], f"ln2_bwd{li}",
                                                    after=token)
        gw = {}
        gw["w_ple_gate"] = wgrad_rows(sv["h1b"], dpg, f"dw_pg{li}")
        gw["w_ple_proj"] = wgrad_cols(pb[li], dpp, f"dw_pp{li}")
        gw["w_down"] = wgrad_down(sv["t"], dz2b, f"dw_down{li}").reshape(N_DEV, FF_SHARD, D)
        dhv, dhg, dcw, dcb = ffn_act_bwd(dz2b, G["w_down"], sv["up"], cw, cb_full[li], li, f"ffn_bwd{li}")
        gw["w_up"] = wgrad_up(sv["h1b"], dhv, dhg, f"dw_up{li}")
        token = tok(ex.grads(li, 0, gw))
        dz1, dz1b, dg1, db1 = dh1_ln1_bwd(dz2, dpg, G["w_ple_gate"], dhv, dhg, G["w_up"], sv["z1"], vec(ln1_g[li]), li,
                                          f"ln1_bwd{li}", after=token)
        token = tok(ex.flush(li, 0, dz1b))
        gw = {"w_mix_out": wgrad_rows(sv["mg"], dz1b, f"dw_mix{li}", after=token)}
        dya, dyp, dga, dgb = merge_bwd(dz1b, G["w_mix_out"], sv["proj"], sv["ya"], sv["yp"], li, f"merge_bwd{li}")
        gw["w_attn_out"] = wgrad_cols(sv["att"], dya, f"dw_ao{li}")
        gw["w_pool_out"] = wgrad_cols(sv["pw"], dyp, f"dw_po{li}")
        token = tok(ex.grads(li, 1, gw))
        da = attn_out_bwd(dya, G["w_attn_out"], li, f"da{li}", after=token)
        du, dpool_w, dpool_sc = pool_bwd(dyp, G["w_pool_out"], sv["pm"], pool_w_b[li], vec(pool_scale[li]), li,
                                         f"pool_bwd{li}")
        token = tok(ex.flush(li, 1, du))
        dq, dk, dv, drpb = attn_bwd(sv["proj"], da, sv["e_rev"], f"attn_bwd{li}", after=token)
        dproj = jnp.concatenate([dq, dk, dv, du, dga, dgb], axis=1)
        dw_in, db_in = wgrad_cols(sv["hb"], dproj, f"dw_in{li}", with_colsum=True)
        token = tok(ex.grads(li, 2, {"w_in": dw_in}))
        dh = dh0_bwd(dz1, dproj, G["w_in"], li, f"dh0{li}", after=token)
        small["b_in"][li] = db_in.reshape(N_PROJ)
        small["rpb"][li] = drpb.reshape(N_HEADS, KROWS, GRID_W)[:, :2 * KH - 1, :2 * KW - 1]
        small["pool_w"][li] = dpool_w
        small["pool_scale"][li] = dpool_sc.reshape(D_POOL)
        small["ln1_g"][li], small["ln1_b"][li] = dg1.reshape(D), db1.reshape(D)
        small["ln2_g"][li], small["ln2_b"][li] = dg2.reshape(D), db2.reshape(D)
        small["conv_b"][li] = dcb.reshape(D_FF)
        small["conv_w"][li] = dcw.transpose(1, 0, 2).reshape(3, D_FF)
        token = tok(ex.flush(li, 2, dh))
        if li + 1 < depth:
            ex.update(li + 1, dh)
    dx, dg_in, db_in0 = ln_bwd(dh, x2, vec(W["ln_in_g"]), "ln_in_bwd", after=token)
    ex.update(0, dx)
    parts = {n: jnp.stack(v_) for n, v_ in small.items()}
    parts["ln_in_g"], parts["ln_in_b"] = dg_in.reshape(D), db_in0.reshape(D)
    return loss_part, dx, parts
```

```python
import numpy as np
import jax
import jax.numpy as jnp
from jax import lax
from jax.experimental import pallas as pl
from jax.experimental.pallas import tpu as pltpu

F32 = jnp.float32
BF16 = jnp.bfloat16
I32 = jnp.int32

D = 1024
DEPTH = 4
GRID_W = 64
N_HEADS = 8
HEAD_DIM = 64
D_ATTN = 512
KH = 8
KW = 16
POOL_WINDOWS = (2, 4, 8, 16)
D_POOL = 512
PGD = 128
D_FF = 2816
PLE_DIM = 256
N_PROJ = 4096
ALPHA = (2 * DEPTH) ** 0.25
LN_EPS = 1e-5
NEG_INF = -1e30
ATT_SCALE = HEAD_DIM ** -0.5
ADAM_LR = 0.001
ADAM_B1 = 0.9
ADAM_B2 = 0.999
ADAM_EPS = 1e-08
ADAM_WD = 0.01
ADAM_STEP = 10

N_DEV = 8
AXES = ("x", "y", "c")
FF_BLK = D_FF // 4
FF_SHARD = D_FF // N_DEV
QROWS = 8
KROWS = 16
QB = QROWS * GRID_W
KB = KROWS * GRID_W
V7X_VMEM_LIMIT = 56 * 2 ** 20
MESH = pl.DeviceIdType.MESH
ANY = pl.BlockSpec(memory_space=pl.ANY)


def _cparams(n_grid):
    return pltpu.CompilerParams(dimension_semantics=("arbitrary",) * n_grid, vmem_limit_bytes=V7X_VMEM_LIMIT)


def _nn(a, b):
    return lax.dot_general(a, b, (((1,), (0,)), ((), ())), preferred_element_type=F32)


def _nt(a, b):
    return lax.dot_general(a, b, (((1,), (1,)), ((), ())), preferred_element_type=F32)


def _tn(a, b):
    return lax.dot_general(a, b, (((0,), (0,)), ((), ())), preferred_element_type=F32)


def _sigmoid(x):
    return 1.0 / (1.0 + jnp.exp(-x))


def _ln_fwd(z, g, b):
    mu = jnp.mean(z, axis=-1, keepdims=True)
    xc = z - mu
    var = jnp.mean(xc * xc, axis=-1, keepdims=True)
    return xc * lax.rsqrt(var + LN_EPS) * g + b


def _ln_bwd(dh, z, g):
    mu = jnp.mean(z, axis=-1, keepdims=True)
    xc = z - mu
    var = jnp.mean(xc * xc, axis=-1, keepdims=True)
    rstd = lax.rsqrt(var + LN_EPS)
    xhat = xc * rstd
    dxh = dh * g
    m1 = jnp.mean(dxh, axis=-1, keepdims=True)
    m2 = jnp.mean(dxh * xhat, axis=-1, keepdims=True)
    return rstd * (dxh - m1 - xhat * m2), dh * xhat


def _colsum(x):
    return jnp.sum(x, axis=0, keepdims=True)


def _lane_cat(ref):
    return jnp.concatenate([ref[j] for j in range(ref.shape[0])], axis=1)


def _row_cat(ref):
    n, r, c = ref.shape
    return ref[...].reshape(n * r, c)


def _shards(n, r, c, li, j_of=None):
    del li
    if j_of is None:
        return pl.BlockSpec((n, r, c), lambda *_: (0, 0, 0))
    return pl.BlockSpec((n, r, c), lambda *g: (j_of(*g), 0, 0))


def _shard(r, c, li, j_of):
    del li
    return pl.BlockSpec((None, r, c), lambda *g: (j_of(*g), 0, 0))


def ln_fwd(x, g, b, name, after=()):
    s = x.shape[0]
    tm = 512
    na = len(after)

    def body(x_ref, g_ref, b_ref, *rest):
        h_ref, hb_ref = rest[na:]
        h = _ln_fwd(x_ref[...], g_ref[...], b_ref[...])
        h_ref[...] = h
        hb_ref[...] = h.astype(BF16)

    row = pl.BlockSpec((tm, D), lambda i: (i, 0))
    vec = pl.BlockSpec((1, D), lambda i: (0, 0))
    return pl.pallas_call(
        body, name=name, grid=(s // tm,), in_specs=[row, vec, vec] + [ANY] * na, out_specs=[row, row],
        out_shape=[jax.ShapeDtypeStruct((s, D), F32), jax.ShapeDtypeStruct((s, D), BF16)],
        compiler_params=_cparams(1))(x, g, b, *after)


def proj_fwd(hb, win, bias, li, j0, nj, out_dtype, name, after=()):
    s = hb.shape[0]
    bn = N_PROJ // N_DEV
    tm = 1024

    def body(a_ref, w_ref, b_ref, *rest):
        rest[-1][...] = (_nn(a_ref[...], w_ref[...]) + b_ref[...]).astype(out_dtype)

    return pl.pallas_call(
        body, name=name, grid=(s // tm, nj),
        in_specs=[pl.BlockSpec((tm, D), lambda i, j: (i, 0)),
                  _shard(D, bn, li, lambda i, j: j0 + j),
                  pl.BlockSpec((1, bn), lambda i, j: (0, j0 + j))] + [ANY] * len(after),
        out_specs=pl.BlockSpec((tm, bn), lambda i, j: (i, j)),
        out_shape=jax.ShapeDtypeStruct((s, nj * bn), out_dtype),
        compiler_params=_cparams(2))(hb, win, bias, *after)


def _attn_types(b, nb):
    first, last = 0, (nb * QROWS - KROWS) * GRID_W
    mid = pl.multiple_of((QROWS * b - KH // 2) * GRID_W, 256)
    return ((b == 0, first), ((b > 0) & (b < nb - 1), mid), (b == nb - 1, last))


def _attn_row(btype, qr):
    lo, delta = ((max(qr - KH // 2, 0), 0), (qr, -(KH // 2)), (min(qr + KH // 2, KH), -KH))[btype]
    return lo, (qr - delta - (KH - 1)) % KROWS, lo - qr + delta + KH - 1


def _row_window(lo):
    pad = (lo % 2) * GRID_W
    return (lo // 2) * 128, KH * GRID_W + 2 * pad, pad


def _lanes(ref, start, width):
    start %= KB
    if start + width <= KB:
        return ref[:, start:start + width]
    return jnp.concatenate([ref[:, start:], ref[:, :start + width - KB]], axis=1)


def _row_logits(s_ref, e_ref, hh, rows, btype, qr):
    lo, shift, _ = _attn_row(btype, qr)
    a0, w, pad = _row_window(lo)
    e = e_ref.at[hh, shift % 2]
    sb = s_ref[rows, a0:a0 + w] + _lanes(e, a0 - (shift - shift % 2) * GRID_W, w)
    if pad:
        lane = lax.broadcasted_iota(I32, (1, w), 1)
        sb = jnp.where((lane >= pad) & (lane < w - pad), sb, NEG_INF)
    return sb, a0, w, pad


def _store_row(ref, rows, a0, w, val):
    if a0:
        ref[rows, 0:a0] = jnp.zeros((GRID_W, a0), ref.dtype)
    ref[rows, a0:a0 + w] = val.astype(ref.dtype)
    if a0 + w < KB:
        ref[rows, a0 + w:KB] = jnp.zeros((GRID_W, KB - a0 - w), ref.dtype)


def attn_fwd(proj, e_tab, name):
    s = proj.shape[0]
    nb = s // QB

    def body(q_ref, k_ref, v_ref, e_ref, o_ref, s_ref, p_ref):
        q = q_ref[...] * ATT_SCALE
        lane = lax.broadcasted_iota(I32, (1, 128), 1)

        def block(btype, k0):
            kwin = k_ref[pl.ds(k0, KB), :]
            vwin = v_ref[pl.ds(k0, KB), :]
            acc = jnp.zeros((QB, 128), F32)
            for hh in range(2):
                lm = (lane // HEAD_DIM) == hh
                qh = jnp.where(lm, q, jnp.zeros_like(q))
                vh = jnp.where(lm, vwin, jnp.zeros_like(vwin))
                s_ref[...] = _nt(qh, kwin)
                for qr in range(QROWS):
                    rows = slice(qr * GRID_W, (qr + 1) * GRID_W)
                    sb, a0, w, _ = _row_logits(s_ref, e_ref, hh, rows, btype, qr)
                    p = jnp.exp(sb - jnp.max(sb, axis=1, keepdims=True))
                    _store_row(p_ref, rows, a0, w, p * (1.0 / jnp.sum(p, axis=1, keepdims=True)))
                acc = acc + _nn(p_ref[...], vh)
            o_ref[...] = acc.astype(BF16)

        for btype, (cond, k0) in enumerate(_attn_types(pl.program_id(1), nb)):
            pl.when(cond)(lambda btype=btype, k0=k0: block(btype, k0))

    return pl.pallas_call(
        body, name=name, grid=(4, nb),
        in_specs=[pl.BlockSpec((QB, 128), lambda j, b: (b, j)),
                  pl.BlockSpec((s, 128), lambda j, b: (0, 4 + j)),
                  pl.BlockSpec((s, 128), lambda j, b: (0, 8 + j)),
                  pl.BlockSpec((2, 2, GRID_W, KB), lambda j, b: (j, 0, 0, 0))],
        out_specs=pl.BlockSpec((QB, 128), lambda j, b: (b, j)),
        out_shape=jax.ShapeDtypeStruct((s, D_ATTN), BF16),
        scratch_shapes=[pltpu.VMEM((QB, KB), F32), pltpu.VMEM((QB, KB), BF16)],
        compiler_params=_cparams(2))(proj, proj, proj, e_tab)


_POOL_PAD = 8


def _pool_counts(s, w):
    t = lax.broadcasted_iota(I32, (s, 1), 0)
    return (jnp.minimum(t + w // 2, s) - jnp.maximum(t - w // 2, 0)).astype(F32)


def _window_sum(x, w, back_first):
    s = x.shape[0]
    z = jnp.zeros((_POOL_PAD, x.shape[1]), F32)
    xe = jnp.concatenate([z, x, z], axis=0)
    n = s + 2 * _POOL_PAD
    acc = xe + pltpu.roll(xe, 1 if back_first else n - 1, 0)
    k = 1
    while 2 * k < w:
        acc = pltpu.roll(acc, k, 0) + pltpu.roll(acc, n - k, 0)
        k *= 2
    return acc[_POOL_PAD:_POOL_PAD + s, :]


def pool_fwd(u, pool_w, pool_scale, name):
    s = u.shape[0]

    def body(u_ref, w_ref, sc_ref, pm_ref, pw_ref):
        for g, w in enumerate(POOL_WINDOWS):
            cols = slice(g * PGD, (g + 1) * PGD)
            ug = u_ref[:, cols]
            pm = (_window_sum(ug, w, True) / _pool_counts(s, w) - ug).astype(BF16)
            pm_ref[:, cols] = pm
            pw_ref[:, cols] = (_nn(pm, w_ref[g]) * sc_ref[:, cols]).astype(BF16)

    full = lambda shape: pl.BlockSpec(shape, lambda i: (0,) * len(shape))
    return pl.pallas_call(
        body, name=name, grid=(1,),
        in_specs=[full((s, D_POOL)), full((4, PGD, PGD)), full((1, D_POOL))],
        out_specs=[full((s, D_POOL)), full((s, D_POOL))],
        out_shape=[jax.ShapeDtypeStruct((s, D_POOL), BF16)] * 2,
        compiler_params=_cparams(1))(u, pool_w, pool_scale)


def merge_fwd(a, pw, wao, wpo, proj, li, name, after=()):
    s = a.shape[0]
    tm, tn = 512, 512
    nt = D // tn
    per = tn // 128

    def body(a_ref, pw_ref, wa_ref, wp_ref, ga_ref, gb_ref, *rest):
        mg_ref, ya_ref, yp_ref = rest[len(after):]
        ya = _nn(a_ref[...], _lane_cat(wa_ref))
        yp = _nn(pw_ref[...], _lane_cat(wp_ref))
        mg = _sigmoid(ga_ref[...].astype(F32)) * ya + _sigmoid(gb_ref[...].astype(F32)) * yp
        mg_ref[...] = mg.astype(BF16)
        ya_ref[...] = ya.astype(BF16)
        yp_ref[...] = yp.astype(BF16)

    act = pl.BlockSpec((tm, D_ATTN), lambda i, j: (i, 0))
    wsp = _shards(per, D_ATTN, 128, li, lambda i, j: j)
    out = pl.BlockSpec((tm, tn), lambda i, j: (i, j))
    ga0 = (3 * D_ATTN + D_POOL) // tn
    return pl.pallas_call(
        body, name=name, grid=(s // tm, nt),
        in_specs=[act, act, wsp, wsp,
                  pl.BlockSpec((tm, tn), lambda i, j: (i, ga0 + j)),
                  pl.BlockSpec((tm, tn), lambda i, j: (i, ga0 + nt + j))] + [ANY] * len(after),
        out_specs=[out, out, out],
        out_shape=[jax.ShapeDtypeStruct((s, D), BF16)] * 3,
        compiler_params=_cparams(2))(a, pw, wao, wpo, proj, proj, *after)


def mix_ln_fwd(mg, wmix, h0, g, b, li, name):
    s = mg.shape[0]
    tm = 256

    def body(mg_ref, w_ref, h0_ref, g_ref, b_ref, z_ref, h_ref, hb_ref):
        z = ALPHA * h0_ref[...] + _nn(mg_ref[...], _row_cat(w_ref))
        h = _ln_fwd(z, g_ref[...], b_ref[...])
        z_ref[...] = z
        h_ref[...] = h
        hb_ref[...] = h.astype(BF16)

    row = pl.BlockSpec((tm, D), lambda i: (i, 0))
    vec = pl.BlockSpec((1, D), lambda i: (0, 0))
    return pl.pallas_call(
        body, name=name, grid=(s // tm,),
        in_specs=[row, _shards(N_DEV, D // N_DEV, D, li), row, vec, vec],
        out_specs=[row, row, row],
        out_shape=[jax.ShapeDtypeStruct((s, D), F32), jax.ShapeDtypeStruct((s, D), F32),
                   jax.ShapeDtypeStruct((s, D), BF16)],
        compiler_params=_cparams(1))(mg, wmix, h0, g, b)


def up_fwd(hb, wup, li, name):
    s = hb.shape[0]
    tm = 1024

    def body(a_ref, w_ref, o_ref):
        o_ref[...] = _nt(a_ref[...], w_ref[...]).astype(BF16)

    return pl.pallas_call(
        body, name=name, grid=(s // tm, N_DEV),
        in_specs=[pl.BlockSpec((tm, D), lambda i, j: (i, 0)), _shard(FF_BLK, D, li, lambda i, j: j)],
        out_specs=pl.BlockSpec((None, tm, FF_BLK), lambda i, j: (j, i, 0)),
        out_shape=jax.ShapeDtypeStruct((N_DEV, s, FF_BLK), BF16),
        compiler_params=_cparams(2))(hb, wup)


_SQRT_HALF = 0.7071067811865476
_INV_SQRT_2PI = 0.3989422804014327


def _shift_rows(x, prev_row, next_row):
    n = x.shape[0]
    r = lax.broadcasted_iota(I32, (n, 1), 0)
    back = jnp.where(r == 0, prev_row, pltpu.roll(x, 1, 0))
    fwd = jnp.where(r == n - 1, next_row, pltpu.roll(x, n - 1, 0))
    return back, fwd


HALO = 16


def _halo_maps(tm, s):
    th = tm // HALO
    return (lambda i: jnp.maximum(i * th - 1, 0)), (lambda i: jnp.minimum((i + 1) * th, s // HALO - 1))


def _slab_specs(tm, s, blk_of):
    before, after = _halo_maps(tm, s)
    main = pl.BlockSpec((None, tm, FF_BLK), lambda c, i: (blk_of(c), i, 0))
    prev = pl.BlockSpec((None, HALO, FF_BLK), lambda c, i: (blk_of(c), before(i), 0))
    nxt = pl.BlockSpec((None, HALO, FF_BLK), lambda c, i: (blk_of(c), after(i), 0))
    return main, prev, nxt


def ffn_act_fwd(up, conv_w, conv_b, name):
    s = up.shape[1]
    tm = 512
    nt = s // tm
    hv_main, _, _ = _slab_specs(tm, s, lambda c: c)
    hg_main, hg_prev, hg_next = _slab_specs(tm, s, lambda c: 4 + c)

    def body(hv_ref, hg_ref, hp_ref, hn_ref, cw_ref, cb_ref, t_ref):
        i = pl.program_id(1)
        hg = hg_ref[...].astype(F32)
        prow = jnp.where(i == 0, 0.0, hp_ref[...].astype(F32)[HALO - 1:HALO, :])
        nrow = jnp.where(i == nt - 1, 0.0, hn_ref[...].astype(F32)[0:1, :])
        back, fwd = _shift_rows(hg, prow, nrow)
        c = back * cw_ref[0:1, :] + hg * cw_ref[1:2, :] + fwd * cw_ref[2:3, :] + cb_ref[...]
        act = 0.5 * c * (1.0 + lax.erf(c * _SQRT_HALF))
        t_ref[...] = (act * hv_ref[...].astype(F32)).astype(BF16)

    return pl.pallas_call(
        body, name=name, grid=(4, nt),
        in_specs=[hv_main, hg_main, hg_prev, hg_next,
                  pl.BlockSpec((None, 3, FF_BLK), lambda c, i: (c, 0, 0)),
                  pl.BlockSpec((None, 1, FF_BLK), lambda c, i: (c, 0, 0))],
        out_specs=pl.BlockSpec((None, tm, FF_BLK), lambda c, i: (c, i, 0)),
        out_shape=jax.ShapeDtypeStruct((4, s, FF_BLK), BF16),
        compiler_params=_cparams(2))(up, up, up, up, conv_w, conv_b)


def down_ple_ln_fwd(t, wdown, hb, wpg, pb, wpp, h1, g, b, li, name):
    s = hb.shape[0]
    tm = 256

    def body(t_ref, wd_ref, hb_ref, wpg_ref, p_ref, wpp_ref, h1_ref, g_ref, b_ref,
             z_ref, h_ref, hbo_ref, pg_ref, pp_ref):
        wd = _row_cat(wd_ref)
        ffn = _nn(t_ref[0], wd[0:FF_BLK, :])
        for c in range(1, 4):
            ffn = ffn + _nn(t_ref[c], wd[c * FF_BLK:(c + 1) * FF_BLK, :])
        pg = _nn(hb_ref[...], _row_cat(wpg_ref))
        pp = _nn(p_ref[...], _lane_cat(wpp_ref))
        z = ALPHA * h1_ref[...] + ffn + _sigmoid(pg) * pp
        h = _ln_fwd(z, g_ref[...], b_ref[...])
        z_ref[...] = z
        h_ref[...] = h
        hbo_ref[...] = h.astype(BF16)
        pg_ref[...] = pg.astype(BF16)
        pp_ref[...] = pp.astype(BF16)

    row = pl.BlockSpec((tm, D), lambda i: (i, 0))
    vec = pl.BlockSpec((1, D), lambda i: (0, 0))
    return pl.pallas_call(
        body, name=name, grid=(s // tm,),
        in_specs=[pl.BlockSpec((4, tm, FF_BLK), lambda i: (0, i, 0)),
                  _shards(N_DEV, FF_SHARD, D, li),
                  row, _shards(N_DEV, D // N_DEV, D, li),
                  pl.BlockSpec((tm, PLE_DIM), lambda i: (i, 0)),
                  _shards(N_DEV, PLE_DIM, 128, li),
                  row, vec, vec],
        out_specs=[row] * 5,
        out_shape=[jax.ShapeDtypeStruct((s, D), F32), jax.ShapeDtypeStruct((s, D), F32),
                   jax.ShapeDtypeStruct((s, D), BF16), jax.ShapeDtypeStruct((s, D), BF16),
                   jax.ShapeDtypeStruct((s, D), BF16)],
        compiler_params=_cparams(1))(t, wdown, hb, wpg, pb, wpp, h1, g, b)


def loss_bwd(h, target, name):
    s = h.shape[0]
    tm = 512

    def body(h_ref, t_ref, dh_ref, l_ref):
        @pl.when(pl.program_id(0) == 0)
        def _():
            l_ref[...] = jnp.zeros_like(l_ref)
        e = h_ref[...] - t_ref[...]
        dh_ref[...] = e * (1.0 / D)
        l_ref[...] += 0.5 * jnp.sum(jnp.mean(e * e, axis=-1, keepdims=True), axis=0, keepdims=True)

    row = pl.BlockSpec((tm, D), lambda i: (i, 0))
    return pl.pallas_call(
        body, name=name, grid=(s // tm,), in_specs=[row, row],
        out_specs=[row, pl.BlockSpec((1, 1), lambda i: (0, 0))],
        out_shape=[jax.ShapeDtypeStruct((s, D), F32), jax.ShapeDtypeStruct((1, 1), F32)],
        compiler_params=_cparams(1))(h, target)


def ln_bwd(dh, z, g, name, after=()):
    s = dh.shape[0]
    tm = 512
    na = len(after)

    def body(dh_ref, z_ref, g_ref, *rest):
        dz_ref, dg_ref, db_ref = rest[na:]

        @pl.when(pl.program_id(0) == 0)
        def _():
            dg_ref[...] = jnp.zeros_like(dg_ref)
            db_ref[...] = jnp.zeros_like(db_ref)
        dh = dh_ref[...]
        dz, dgx = _ln_bwd(dh, z_ref[...], g_ref[...])
        dz_ref[...] = dz
        dg_ref[...] += _colsum(dgx)
        db_ref[...] += _colsum(dh)

    row = pl.BlockSpec((tm, D), lambda i: (i, 0))
    vec = pl.BlockSpec((1, D), lambda i: (0, 0))
    return pl.pallas_call(
        body, name=name, grid=(s // tm,), in_specs=[row, row, vec] + [ANY] * na, out_specs=[row, vec, vec],
        out_shape=[jax.ShapeDtypeStruct((s, D), F32), jax.ShapeDtypeStruct((1, D), F32),
                   jax.ShapeDtypeStruct((1, D), F32)],
        compiler_params=_cparams(1))(dh, z, g, *after)


def ln2_ple_bwd(dh, z, g, pg, pp, name, after=()):
    s = dh.shape[0]
    tm = 512
    na = len(after)

    def body(dh_ref, z_ref, g_ref, pg_ref, pp_ref, *rest):
        dz_ref, dzb_ref, dpg_ref, dpp_ref, dg_ref, db_ref = rest[na:]

        @pl.when(pl.program_id(0) == 0)
        def _():
            dg_ref[...] = jnp.zeros_like(dg_ref)
            db_ref[...] = jnp.zeros_like(db_ref)
        dh = dh_ref[...]
        dz, dgx = _ln_bwd(dh, z_ref[...], g_ref[...])
        sg = _sigmoid(pg_ref[...].astype(F32))
        dz_ref[...] = dz
        dzb_ref[...] = dz.astype(BF16)
        dpg_ref[...] = (dz * pp_ref[...].astype(F32) * sg * (1.0 - sg)).astype(BF16)
        dpp_ref[...] = (dz * sg).astype(BF16)
        dg_ref[...] += _colsum(dgx)
        db_ref[...] += _colsum(dh)

    row = pl.BlockSpec((tm, D), lambda i: (i, 0))
    vec = pl.BlockSpec((1, D), lambda i: (0, 0))
    return pl.pallas_call(
        body, name=name, grid=(s // tm,), in_specs=[row, row, vec, row, row] + [ANY] * na,
        out_specs=[row, row, row, row, vec, vec],
        out_shape=[jax.ShapeDtypeStruct((s, D), F32)] + [jax.ShapeDtypeStruct((s, D), BF16)] * 3
        + [jax.ShapeDtypeStruct((1, D), F32)] * 2,
        compiler_params=_cparams(1))(dh, z, g, pg, pp, *after)


def wgrad_rows(a, dy, name, after=()):
    s, k = a.shape
    n = dy.shape[1]
    kb = k // N_DEV

    def body(a_ref, dy_ref, *rest):
        rest[-1][...] = _tn(a_ref[...], dy_ref[...]).astype(BF16)

    return pl.pallas_call(
        body, name=name, grid=(N_DEV,),
        in_specs=[pl.BlockSpec((s, kb), lambda j: (0, j)), pl.BlockSpec((s, n), lambda j: (0, 0))] + [ANY] * len(after),
        out_specs=pl.BlockSpec((None, kb, n), lambda j: (j, 0, 0)),
        out_shape=jax.ShapeDtypeStruct((N_DEV, kb, n), BF16),
        compiler_params=_cparams(1))(a, dy, *after)


def wgrad_cols(a, dy, name, with_colsum=False):
    s, k = a.shape
    n = dy.shape[1]
    nb = n // N_DEV

    def body(a_ref, dy_ref, o_ref, *cs_ref):
        dy = dy_ref[...]
        o_ref[...] = _tn(a_ref[...], dy).astype(BF16)
        if with_colsum:
            cs_ref[0][...] = _colsum(dy.astype(F32))

    out_specs = [pl.BlockSpec((None, k, nb), lambda j: (j, 0, 0))]
    out_shape = [jax.ShapeDtypeStruct((N_DEV, k, nb), BF16)]
    if with_colsum:
        out_specs.append(pl.BlockSpec((1, nb), lambda j: (0, j)))
        out_shape.append(jax.ShapeDtypeStruct((1, n), F32))
    res = pl.pallas_call(
        body, name=name, grid=(N_DEV,),
        in_specs=[pl.BlockSpec((s, k), lambda j: (0, 0)), pl.BlockSpec((s, nb), lambda j: (0, j))],
        out_specs=out_specs, out_shape=out_shape,
        compiler_params=_cparams(1))(a, dy)
    return res if with_colsum else res[0]


def wgrad_down(t, dy, name):
    _, s, k = t.shape
    n = dy.shape[1]

    def body(a_ref, dy_ref, o_ref):
        o_ref[...] = _tn(a_ref[...], dy_ref[...]).astype(BF16)

    return pl.pallas_call(
        body, name=name, grid=(4,),
        in_specs=[pl.BlockSpec((None, s, k), lambda j: (j, 0, 0)), pl.BlockSpec((s, n), lambda j: (0, 0))],
        out_specs=pl.BlockSpec((None, k, n), lambda j: (j, 0, 0)),
        out_shape=jax.ShapeDtypeStruct((4, k, n), BF16),
        compiler_params=_cparams(1))(t, dy)


def wgrad_up(a, dhv, dhg, name):
    s, k = a.shape

    def body(a_ref, dv_ref, dg_ref, o_ref):
        j = pl.program_id(0)

        @pl.when(j < 4)
        def _():
            o_ref[...] = _tn(dv_ref[...], a_ref[...]).astype(BF16)

        @pl.when(j >= 4)
        def _():
            o_ref[...] = _tn(dg_ref[...], a_ref[...]).astype(BF16)

    return pl.pallas_call(
        body, name=name, grid=(N_DEV,),
        in_specs=[pl.BlockSpec((s, k), lambda j: (0, 0)),
                  pl.BlockSpec((None, s, FF_BLK), lambda j: (jnp.minimum(j, 3), 0, 0)),
                  pl.BlockSpec((None, s, FF_BLK), lambda j: (jnp.maximum(j - 4, 0), 0, 0))],
        out_specs=pl.BlockSpec((None, FF_BLK, k), lambda j: (j, 0, 0)),
        out_shape=jax.ShapeDtypeStruct((N_DEV, FF_BLK, k), BF16),
        compiler_params=_cparams(1))(a, dhv, dhg)


def ffn_act_bwd(dzb, wdown, up, conv_w, conv_b, li, name):
    s = up.shape[1]
    tm = 512
    nt = s // tm
    before, after = _halo_maps(tm, s)
    hv_main, hv_prev, hv_next = _slab_specs(tm, s, lambda c: c)
    hg_main, hg_prev, hg_next = _slab_specs(tm, s, lambda c: 4 + c)

    def dc_of(dz, wd, hv, hg, back, fwd, cw_ref, cb_ref):
        dt = _nt(dz, wd)
        c = back * cw_ref[0:1, :] + hg * cw_ref[1:2, :] + fwd * cw_ref[2:3, :] + cb_ref[...]
        cdf = 0.5 * (1.0 + lax.erf(c * _SQRT_HALF))
        pdf = jnp.exp(-0.5 * c * c) * _INV_SQRT_2PI
        return dt, c * cdf, dt * hv * (cdf + c * pdf)

    def body(dz_ref, dzp_ref, dzn_ref, wd_ref, hv_ref, hvp_ref, hvn_ref, hg_ref, hgp_ref, hgn_ref, cw_ref, cb_ref,
             dhv_ref, dhg_ref, dcw_ref, dcb_ref):
        i = pl.program_id(1)

        @pl.when(i == 0)
        def _():
            dcw_ref[...] = jnp.zeros_like(dcw_ref)
            dcb_ref[...] = jnp.zeros_like(dcb_ref)

        wd = _row_cat(wd_ref)
        hg = hg_ref[...].astype(F32)
        hgp = hgp_ref[...].astype(F32)
        hgn = hgn_ref[...].astype(F32)
        first, last = i == 0, i == nt - 1
        e = HALO - 1
        back, fwd = _shift_rows(hg, jnp.where(first, 0.0, hgp[e:e + 1, :]), jnp.where(last, 0.0, hgn[0:1, :]))
        dt, act, dc = dc_of(dz_ref[...], wd, hv_ref[...].astype(F32), hg, back, fwd, cw_ref, cb_ref)
        dhv_ref[...] = (dt * act).astype(BF16)
        bp, fp = _shift_rows(hgp, hgp[0:1, :], hg[0:1, :])
        _, _, dcp = dc_of(dzp_ref[...], wd, hvp_ref[...].astype(F32), hgp, bp, fp, cw_ref, cb_ref)
        bn, fn = _shift_rows(hgn, hg[tm - 1:tm, :], hgn[e:e + 1, :])
        _, _, dcn = dc_of(dzn_ref[...], wd, hvn_ref[...].astype(F32), hgn, bn, fn, cw_ref, cb_ref)
        dc_back, dc_fwd = _shift_rows(dc, jnp.where(first, 0.0, dcp[e:e + 1, :]), jnp.where(last, 0.0, dcn[0:1, :]))
        dhg_ref[...] = (dc_fwd * cw_ref[0:1, :] + dc * cw_ref[1:2, :] + dc_back * cw_ref[2:3, :]).astype(BF16)
        dcw_ref[0:1, :] += _colsum(dc * back)
        dcw_ref[1:2, :] += _colsum(dc * hg)
        dcw_ref[2:3, :] += _colsum(dc * fwd)
        dcb_ref[...] += _colsum(dc)

    out_slab = pl.BlockSpec((None, tm, FF_BLK), lambda c, i: (c, i, 0))
    cw_spec = pl.BlockSpec((None, 3, FF_BLK), lambda c, i: (c, 0, 0))
    cb_spec = pl.BlockSpec((None, 1, FF_BLK), lambda c, i: (c, 0, 0))
    return pl.pallas_call(
        body, name=name, grid=(4, nt),
        in_specs=[pl.BlockSpec((tm, D), lambda c, i: (i, 0)),
                  pl.BlockSpec((HALO, D), lambda c, i: (before(i), 0)),
                  pl.BlockSpec((HALO, D), lambda c, i: (after(i), 0)),
                  _shards(2, FF_SHARD, D, li, lambda c, i: c),
                  hv_main, hv_prev, hv_next, hg_main, hg_prev, hg_next, cw_spec, cb_spec],
        out_specs=[out_slab, out_slab, cw_spec, cb_spec],
        out_shape=[jax.ShapeDtypeStruct((4, s, FF_BLK), BF16), jax.ShapeDtypeStruct((4, s, FF_BLK), BF16),
                   jax.ShapeDtypeStruct((4, 3, FF_BLK), F32), jax.ShapeDtypeStruct((4, 1, FF_BLK), F32)],
        compiler_params=_cparams(2))(dzb, dzb, dzb, wdown, up, up, up, up, up, up, conv_w, conv_b)


def dh1_ln1_bwd(dz2, dpg, wpg, dhv, dhg, wup, z1, g1, li, name, after=()):
    s = dz2.shape[0]
    tm = 256
    na = len(after)

    def body(dz2_ref, dpg_ref, wpg_ref, dhv_ref, dhg_ref, wup_ref, z1_ref, g_ref, *rest):
        dz_ref, dzb_ref, dg_ref, db_ref = rest[na:]

        @pl.when(pl.program_id(0) == 0)
        def _():
            dg_ref[...] = jnp.zeros_like(dg_ref)
            db_ref[...] = jnp.zeros_like(db_ref)
        dh = ALPHA * dz2_ref[...] + _nt(dpg_ref[...], _row_cat(wpg_ref))
        for c in range(4):
            dh = dh + _nn(dhv_ref[c], wup_ref[c]) + _nn(dhg_ref[c], wup_ref[4 + c])
        dz, dgx = _ln_bwd(dh, z1_ref[...], g_ref[...])
        dz_ref[...] = dz
        dzb_ref[...] = dz.astype(BF16)
        dg_ref[...] += _colsum(dgx)
        db_ref[...] += _colsum(dh)

    row = pl.BlockSpec((tm, D), lambda i: (i, 0))
    vec = pl.BlockSpec((1, D), lambda i: (0, 0))
    slab = pl.BlockSpec((4, tm, FF_BLK), lambda i: (0, i, 0))
    return pl.pallas_call(
        body, name=name, grid=(s // tm,),
        in_specs=[row, row, _shards(N_DEV, D // N_DEV, D, li), slab, slab, _shards(N_DEV, FF_BLK, D, li), row, vec]
        + [ANY] * na,
        out_specs=[row, row, vec, vec],
        out_shape=[jax.ShapeDtypeStruct((s, D), F32), jax.ShapeDtypeStruct((s, D), BF16),
                   jax.ShapeDtypeStruct((1, D), F32), jax.ShapeDtypeStruct((1, D), F32)],
        compiler_params=_cparams(1))(dz2, dpg, wpg, dhv, dhg, wup, z1, g1, *after)


def merge_bwd(dz1b, wmix, proj, ya, yp, li, name):
    s = dz1b.shape[0]
    tm, tn = 512, 512
    nt = D // tn
    per = tn // (D // N_DEV)
    ga0 = (3 * D_ATTN + D_POOL) // tn

    def body(dz_ref, w_ref, ga_ref, gb_ref, ya_ref, yp_ref, dya_ref, dyp_ref, dga_ref, dgb_ref):
        dm = _nt(dz_ref[...], _row_cat(w_ref))
        sa = _sigmoid(ga_ref[...].astype(F32))
        sb = _sigmoid(gb_ref[...].astype(F32))
        dya_ref[...] = (dm * sa).astype(BF16)
        dyp_ref[...] = (dm * sb).astype(BF16)
        dga_ref[...] = (dm * ya_ref[...].astype(F32) * sa * (1.0 - sa)).astype(BF16)
        dgb_ref[...] = (dm * yp_ref[...].astype(F32) * sb * (1.0 - sb)).astype(BF16)

    tile = pl.BlockSpec((tm, tn), lambda i, j: (i, j))
    return pl.pallas_call(
        body, name=name, grid=(s // tm, nt),
        in_specs=[pl.BlockSpec((tm, D), lambda i, j: (i, 0)),
                  _shards(per, D // N_DEV, D, li, lambda i, j: j),
                  pl.BlockSpec((tm, tn), lambda i, j: (i, ga0 + j)),
                  pl.BlockSpec((tm, tn), lambda i, j: (i, ga0 + nt + j)),
                  tile, tile],
        out_specs=[tile] * 4,
        out_shape=[jax.ShapeDtypeStruct((s, D), BF16)] * 4,
        compiler_params=_cparams(2))(dz1b, wmix, proj, proj, ya, yp)


def attn_out_bwd(dya, wao, li, name, after=()):
    s = dya.shape[0]
    tm = 512

    def body(d_ref, w_ref, *rest):
        rest[-1][...] = _nt(d_ref[...], _lane_cat(w_ref)).astype(BF16)

    return pl.pallas_call(
        body, name=name, grid=(s // tm,),
        in_specs=[pl.BlockSpec((tm, D), lambda i: (i, 0)), _shards(N_DEV, D_ATTN, 128, li)] + [ANY] * len(after),
        out_specs=pl.BlockSpec((tm, D_ATTN), lambda i: (i, 0)),
        out_shape=jax.ShapeDtypeStruct((s, D_ATTN), BF16),
        compiler_params=_cparams(1))(dya, wao, *after)


def pool_bwd(dyp, wpo, pm, pool_w, pool_scale, li, name):
    s = dyp.shape[0]

    def body(dyp_ref, wpo_ref, pm_ref, w_ref, sc_ref, du_ref, dw_ref, dsc_ref):
        wpo = _lane_cat(wpo_ref)
        dyp = dyp_ref[...]
        for g, w in enumerate(POOL_WINDOWS):
            cols = slice(g * PGD, (g + 1) * PGD)
            dpw = _nt(dyp, wpo[g * PGD:(g + 1) * PGD, :])
            pmg = pm_ref[:, cols]
            dsc_ref[:, cols] = _colsum(dpw * _nn(pmg, w_ref[g]))
            dpmw = (dpw * sc_ref[:, cols]).astype(BF16)
            dw_ref[g] = _tn(pmg, dpmw)
            dpm = _nt(dpmw, w_ref[g])
            du_ref[:, cols] = (_window_sum(dpm / _pool_counts(s, w), w, False) - dpm).astype(BF16)

    full = lambda shape: pl.BlockSpec(shape, lambda i: (0,) * len(shape))
    return pl.pallas_call(
        body, name=name, grid=(1,),
        in_specs=[full((s, D)), _shards(N_DEV, D_POOL, 128, li), full((s, D_POOL)), full((4, PGD, PGD)),
                  full((1, D_POOL))],
        out_specs=[full((s, D_POOL)), full((4, PGD, PGD)), full((1, D_POOL))],
        out_shape=[jax.ShapeDtypeStruct((s, D_POOL), BF16), jax.ShapeDtypeStruct((4, PGD, PGD), F32),
                   jax.ShapeDtypeStruct((1, D_POOL), F32)],
        compiler_params=_cparams(1))(dyp, wpo, pm, pool_w, pool_scale)


def attn_bwd(proj, da, e_rev, name, after=()):
    s = proj.shape[0]
    nb = s // QB
    skew = GRID_W + (GRID_W - KW)

    def body(q_ref, k_ref, v_ref, do_ref, e_ref, *rest):
        dq_ref, dk_ref, dv_ref, g_ref, s_ref, dp_ref, ds_ref, p_ref, dkt_acc, dvt_acc = rest[len(after):]
        b = pl.program_id(1)

        @pl.when(b == 0)
        def _():
            dkt_acc[...] = jnp.zeros_like(dkt_acc)
            dvt_acc[...] = jnp.zeros_like(dvt_acc)
            g_ref[...] = jnp.zeros_like(g_ref)

        ri = lax.broadcasted_iota(I32, (QB, QB), 0)
        ci = lax.broadcasted_iota(I32, (QB, QB), 1)
        rev = jnp.where(ri + ci == QB - 1, 1.0, 0.0).astype(BF16)
        q = _nn(rev, q_ref[...]).astype(BF16) * ATT_SCALE
        do = _nn(rev, do_ref[...]).astype(BF16)
        lane = lax.broadcasted_iota(I32, (1, 128), 1)

        def block(btype, k0):
            kwin = k_ref[pl.ds(k0, KB), :]
            vwin = v_ref[pl.ds(k0, KB), :]
            dq = jnp.zeros((QB, 128), F32)
            for hh in range(2):
                lm = (lane // HEAD_DIM) == hh
                qh = jnp.where(lm, q, jnp.zeros_like(q))
                doh = jnp.where(lm, do, jnp.zeros_like(do))
                kh = jnp.where(lm, kwin, jnp.zeros_like(kwin))
                s_ref[...] = _nt(qh, kwin)
                dp_ref[...] = _nt(doh, vwin)
                g = jnp.zeros((1, KB), F32)
                for ib in range(QROWS):
                    qr = QROWS - 1 - ib
                    rows = slice(ib * GRID_W, (ib + 1) * GRID_W)
                    sb, a0, w, pad = _row_logits(s_ref, e_ref, hh, rows, btype, qr)
                    p = jnp.exp(sb - jnp.max(sb, axis=1, keepdims=True))
                    p = p * (1.0 / jnp.sum(p, axis=1, keepdims=True))
                    dp = dp_ref[rows, a0:a0 + w]
                    ds = p * (dp - jnp.sum(p * dp, axis=1, keepdims=True))
                    _store_row(ds_ref, rows, a0, w, ds)
                    _store_row(p_ref, rows, a0, w, p)
                    t = jnp.sum(pltpu.roll(ds, w - skew, 1, stride=1, stride_axis=0), axis=0, keepdims=True)
                    t = t[:, :KH * GRID_W] if pad else pltpu.roll(t, GRID_W, 1)
                    i0 = _attn_row(btype, qr)[2]
                    g = g + pltpu.roll(jnp.concatenate([t, jnp.zeros_like(t)], axis=1), i0 * GRID_W, 1)
                g_ref[hh] += g
                dsb = ds_ref[...]
                dq = dq + _nn(dsb, kh) * ATT_SCALE
                dkt_acc[:, pl.ds(k0, KB)] += _tn(qh, dsb)
                dvt_acc[:, pl.ds(k0, KB)] += _tn(doh, p_ref[...])
            dq_ref[...] = _nn(rev, dq.astype(BF16)).astype(BF16)

        for btype, (cond, k0) in enumerate(_attn_types(b, nb)):
            pl.when(cond)(lambda btype=btype, k0=k0: block(btype, k0))

        @pl.when(b == nb - 1)
        def _():
            dk_ref[...] = dkt_acc[...].T.astype(BF16)
            dv_ref[...] = dvt_acc[...].T.astype(BF16)

    col = pl.BlockSpec((s, 128), lambda j, b: (0, j))
    return pl.pallas_call(
        body, name=name, grid=(4, nb),
        in_specs=[pl.BlockSpec((QB, 128), lambda j, b: (b, j)),
                  pl.BlockSpec((s, 128), lambda j, b: (0, 4 + j)),
                  pl.BlockSpec((s, 128), lambda j, b: (0, 8 + j)),
                  pl.BlockSpec((QB, 128), lambda j, b: (b, j)),
                  pl.BlockSpec((2, 2, GRID_W, KB), lambda j, b: (j, 0, 0, 0))] + [ANY] * len(after),
        out_specs=[pl.BlockSpec((QB, 128), lambda j, b: (b, j)), col, col,
                   pl.BlockSpec((2, 1, KB), lambda j, b: (j, 0, 0))],
        out_shape=[jax.ShapeDtypeStruct((s, D_ATTN), BF16)] * 3 + [jax.ShapeDtypeStruct((N_HEADS, 1, KB), F32)],
        scratch_shapes=[pltpu.VMEM((QB, KB), F32), pltpu.VMEM((QB, KB), F32), pltpu.VMEM((QB, KB), BF16),
                        pltpu.VMEM((QB, KB), BF16), pltpu.VMEM((128, s), F32), pltpu.VMEM((128, s), F32)],
        compiler_params=_cparams(2))(proj, proj, proj, da, e_rev, *after)


def dh0_bwd(dz1, dproj, win, li, name, after=()):
    s = dz1.shape[0]
    tm = 256
    bn = N_PROJ // N_DEV

    def body(dz_ref, dp_ref, w_ref, *rest):
        acc = ALPHA * dz_ref[...]
        for j in range(N_DEV):
            acc = acc + _nt(dp_ref[:, j * bn:(j + 1) * bn], w_ref[j])
        rest[-1][...] = acc

    row = pl.BlockSpec((tm, D), lambda i: (i, 0))
    return pl.pallas_call(
        body, name=name, grid=(s // tm,),
        in_specs=[row, pl.BlockSpec((tm, N_PROJ), lambda i: (i, 0)), _shards(N_DEV, D, bn, li)] + [ANY] * len(after),
        out_specs=row, out_shape=jax.ShapeDtypeStruct((s, D), F32),
        compiler_params=_cparams(1))(dz1, dproj, win, *after)


def _coords():
    return lax.axis_index("x"), lax.axis_index("y"), lax.axis_index("c")


def _dev_index(px, py, pc):
    return 4 * px + 2 * py + pc


def all_gather(arrs, name):
    n = len(arrs)

    def body(*refs):
        ins, outs = refs[:n], refs[n:2 * n]
        send_sems, recv_sems, local_sems = refs[2 * n:]
        x, y, c = _coords()
        me, sibling = (x, y, c), (x, y, 1 - c)
        chips = [(1 - x, y), (x, 1 - y), (1 - x, 1 - y)]

        def copy(a, k, block, to, src=None):
            dst = outs[a].at[_dev_index(*block)]
            return pltpu.make_async_remote_copy(
                src_ref=dst if src is None else src, dst_ref=dst,
                send_sem=send_sems.at[a, k], recv_sem=recv_sems.at[a, k], device_id=to, device_id_type=MESH)

        mine = [pltpu.make_async_copy(ins[a], outs[a].at[_dev_index(*me)], local_sems.at[a]) for a in range(n)]
        for cp in mine:
            cp.start()
        first = []
        for a in range(n):
            first.append(copy(a, 0, me, sibling, src=ins[a]))
            first += [copy(a, 1 + j, me, (*chip, c), src=ins[a]) for j, chip in enumerate(chips)]
        for cp in first:
            cp.start()
        passed = []
        for j, chip in enumerate(chips):
            for a in range(n):
                copy(a, 1 + j, (*chip, c), me).wait_recv()
                cp = copy(a, 4 + j, (*chip, c), sibling)
                cp.start()
                passed.append(cp)
        for a in range(n):
            copy(a, 0, sibling, me).wait_recv()
            for j, chip in enumerate(chips):
                copy(a, 4 + j, (*chip, 1 - c), me).wait_recv()
        for cp in first + passed:
            cp.wait_send()
        for cp in mine:
            cp.wait()

    return pl.pallas_call(
        body, name=name,
        out_shape=[jax.ShapeDtypeStruct((N_DEV,) + a.shape, a.dtype) for a in arrs],
        in_specs=[ANY] * n, out_specs=[ANY] * n,
        scratch_shapes=[pltpu.SemaphoreType.DMA((n, 7)), pltpu.SemaphoreType.DMA((n, 7)),
                        pltpu.SemaphoreType.DMA((n,))],
    )(*arrs)


HBM = pl.BlockSpec(memory_space=pltpu.HBM)
SEM = pl.BlockSpec(memory_space=pltpu.SEMAPHORE)
_EFFECT = pltpu.SideEffectType.DATAFLOW_SIDE_EFFECTING
_TOKEN = jax.ShapeDtypeStruct((8, 128), F32)


def _in_hbm(a):
    return pltpu.with_memory_space_constraint(a, pltpu.HBM)


def _hbm_like(a):
    return pltpu.HBM(a.shape, a.dtype)


def _peers(x, y, c):
    return [(x, y, 1 - c), (1 - x, y, c), (x, 1 - y, c), (1 - x, 1 - y, c)]


def ag_start(lands, after, name):
    n = len(lands)

    def body(*refs):
        land = refs[:n]
        send_sem, recv_sem, token = refs[n + 1], refs[n + 2], refs[-1]
        x, y, c = _coords()
        me = _dev_index(x, y, c)
        for k, peer in enumerate(_peers(x, y, c)):
            for a in range(n):
                pltpu.make_async_remote_copy(src_ref=land[a].at[me], dst_ref=land[a].at[me], send_sem=send_sem.at[k],
                                             recv_sem=recv_sem.at[k], device_id=peer, device_id_type=MESH).start()
        token[...] = jnp.zeros_like(token)

    res = pl.pallas_call(
        body, name=name,
        out_shape=(pltpu.SemaphoreType.DMA((4,)), pltpu.SemaphoreType.DMA((4,)), *[_hbm_like(l) for l in lands], _TOKEN),
        in_specs=[HBM] * n + [ANY], out_specs=(SEM, SEM, *[HBM] * n, pl.BlockSpec(memory_space=pltpu.VMEM)),
        input_output_aliases={a: 2 + a for a in range(n)},
        compiler_params=pltpu.CompilerParams(has_side_effects=_EFFECT),
    )(*[_in_hbm(l) for l in lands], after)
    return res[0], res[1], list(res[2:2 + n]), res[-1]


def ag_forward(send_sem, recv_sem, lands, after, name):
    n = len(lands)

    def body(*refs):
        send_sem, recv_sem = refs[0], refs[1]
        land = refs[2:2 + n]
        fsend, frecv = refs[3 + n], refs[4 + n]
        x, y, c = _coords()
        peers = _peers(x, y, c)
        for k in range(1, 4):
            blk = _dev_index(*peers[k])
            for a in range(n):
                pltpu.make_async_remote_copy(src_ref=land[a].at[blk], dst_ref=land[a].at[blk], send_sem=send_sem.at[k],
                                             recv_sem=recv_sem.at[k], device_id=peers[k], device_id_type=MESH).wait_recv()
        for k in range(1, 4):
            blk = _dev_index(*peers[k])
            for a in range(n):
                pltpu.make_async_remote_copy(src_ref=land[a].at[blk], dst_ref=land[a].at[blk], send_sem=fsend.at[k - 1],
                                             recv_sem=frecv.at[k - 1], device_id=peers[0], device_id_type=MESH).start()

    res = pl.pallas_call(
        body, name=name,
        out_shape=(pltpu.SemaphoreType.DMA((3,)), pltpu.SemaphoreType.DMA((3,)), *[_hbm_like(l) for l in lands]),
        in_specs=[SEM, SEM, *[HBM] * n, ANY], out_specs=(SEM, SEM, *[HBM] * n),
        input_output_aliases={2 + a: 2 + a for a in range(n)},
        compiler_params=pltpu.CompilerParams(has_side_effects=_EFFECT),
    )(send_sem, recv_sem, *lands, after)
    return res[0], res[1], list(res[2:])


def ag_finish(send_sem, recv_sem, fsend, frecv, lands, after, name):
    n = len(lands)

    def body(*refs):
        send_sem, recv_sem, fsend, frecv = refs[:4]
        land = refs[4:4 + n]
        x, y, c = _coords()
        me = _dev_index(x, y, c)
        peers = _peers(x, y, c)
        for k in range(4):
            for a in range(n):
                pltpu.make_async_remote_copy(src_ref=land[a].at[me], dst_ref=land[a].at[me], send_sem=send_sem.at[k],
                                             recv_sem=recv_sem.at[k], device_id=peers[k], device_id_type=MESH).wait_send()
        sib = _dev_index(*peers[0])
        for a in range(n):
            pltpu.make_async_remote_copy(src_ref=land[a].at[sib], dst_ref=land[a].at[sib], send_sem=send_sem.at[0],
                                         recv_sem=recv_sem.at[0], device_id=peers[0], device_id_type=MESH).wait_recv()
        for k in range(1, 4):
            mine = _dev_index(*peers[k])
            theirs = _dev_index(peers[k][0], peers[k][1], 1 - c)
            for a in range(n):
                pltpu.make_async_remote_copy(src_ref=land[a].at[mine], dst_ref=land[a].at[theirs], send_sem=fsend.at[k - 1],
                                             recv_sem=frecv.at[k - 1], device_id=peers[0], device_id_type=MESH).wait()

    res = pl.pallas_call(
        body, name=name, out_shape=tuple(_hbm_like(l) for l in lands),
        in_specs=[SEM] * 4 + [HBM] * n + [ANY], out_specs=tuple([HBM] * n),
        input_output_aliases={4 + a: a for a in range(n)},
        compiler_params=pltpu.CompilerParams(has_side_effects=_EFFECT),
    )(send_sem, recv_sem, fsend, frecv, *lands, after)
    return list(res)


def _others(x, y, c):
    return [(x ^ (k & 1), y ^ ((k >> 1) & 1), c ^ (k >> 2)) for k in range(1, N_DEV)]


def rs_start(parts, name):
    n = len(parts)
    lands = [lax.empty((N_DEV - 1,) + p.shape[1:], p.dtype) for p in parts]

    def body(*refs):
        src, land = refs[:n], refs[n:2 * n]
        send_sem, recv_sem, token = refs[2 * n], refs[2 * n + 1], refs[-1]
        for k, peer in enumerate(_others(*_coords())):
            for a in range(n):
                pltpu.make_async_remote_copy(src_ref=src[a].at[_dev_index(*peer)], dst_ref=land[a].at[k],
                                             send_sem=send_sem.at[k], recv_sem=recv_sem.at[k], device_id=peer,
                                             device_id_type=MESH).start()
        token[...] = jnp.zeros_like(token)

    res = pl.pallas_call(
        body, name=name,
        out_shape=(pltpu.SemaphoreType.DMA((N_DEV - 1,)), pltpu.SemaphoreType.DMA((N_DEV - 1,)),
                   *[_hbm_like(p) for p in parts], *[_hbm_like(l) for l in lands], _TOKEN),
        in_specs=[HBM] * (2 * n), out_specs=(SEM, SEM, *[HBM] * (2 * n), pl.BlockSpec(memory_space=pltpu.VMEM)),
        input_output_aliases={a: 2 + a for a in range(2 * n)},
        compiler_params=pltpu.CompilerParams(has_side_effects=_EFFECT),
    )(*[_in_hbm(p) for p in parts], *[_in_hbm(l) for l in lands])
    return res[0], res[1], list(res[2:2 + n]), list(res[2 + n:2 + 2 * n]), res[-1]


def rs_finish(send_sem, recv_sem, parts, lands, after, name):
    n = len(parts)

    def body(*refs):
        send_sem, recv_sem = refs[0], refs[1]
        src, land = refs[2:2 + n], refs[2 + n:2 + 2 * n]
        for k, peer in enumerate(_others(*_coords())):
            for a in range(n):
                pltpu.make_async_remote_copy(src_ref=src[a].at[_dev_index(*peer)], dst_ref=land[a].at[k],
                                             send_sem=send_sem.at[k], recv_sem=recv_sem.at[k], device_id=peer,
                                             device_id_type=MESH).wait()

    res = pl.pallas_call(
        body, name=name, out_shape=tuple(_hbm_like(t) for t in list(parts) + list(lands)),
        in_specs=[SEM, SEM] + [HBM] * (2 * n) + [ANY], out_specs=tuple([HBM] * (2 * n)),
        input_output_aliases={2 + a: a for a in range(2 * n)},
        compiler_params=pltpu.CompilerParams(has_side_effects=_EFFECT),
    )(send_sem, recv_sem, *parts, *lands, after)
    return list(res[:n]), list(res[n:])


def _row_tile(r):
    return next(t for t in (512, 352, 256, 128) if r % t == 0)


def _adamw(w, g, m, v):
    m = ADAM_B1 * m + (1.0 - ADAM_B1) * g
    v = ADAM_B2 * v + (1.0 - ADAM_B2) * (g * g)
    m_hat = m / (1.0 - ADAM_B1 ** ADAM_STEP)
    v_hat = v / (1.0 - ADAM_B2 ** ADAM_STEP)
    delta = -ADAM_LR * (m_hat / (jnp.sqrt(v_hat) + ADAM_EPS) + ADAM_WD * w)
    return delta, m, v


def adamw_shard(me, part, recv, w, m, v, li, prev, name):
    _, r, c = part.shape
    tr = _row_tile(r)

    def body(me_ref, own_ref, recv_ref, w_ref, m_ref, v_ref, p0, p1, p2, p3, g_ref, d_ref, nm_ref, nv_ref):
        g = own_ref[...].astype(F32)
        for k in range(N_DEV - 1):
            g = g + recv_ref[k].astype(F32)
        delta, nm, nv = _adamw(w_ref[...], g, m_ref[...], v_ref[...])
        g_ref[...] = g
        d_ref[...] = delta
        nm_ref[...] = nm
        nv_ref[...] = nv

    lay = pl.BlockSpec((None, tr, c), lambda t, me: (li, t, 0))
    stack = jax.ShapeDtypeStruct((DEPTH, r, c), F32)
    grid_spec = pltpu.PrefetchScalarGridSpec(
        num_scalar_prefetch=1, grid=(r // tr,),
        in_specs=[pl.BlockSpec((None, tr, c), lambda t, me: (me[0], t, 0)),
                  pl.BlockSpec((N_DEV - 1, tr, c), lambda t, me: (0, t, 0)), lay, lay, lay, ANY, ANY, ANY, ANY],
        out_specs=[lay] * 4)
    return pl.pallas_call(
        body, name=name, grid_spec=grid_spec, out_shape=[stack] * 4,
        input_output_aliases={6: 0, 7: 1, 8: 2, 9: 3},
        compiler_params=_cparams(1))(me, part, recv, w, m, v, *prev)


def adamw_replicated(gathered, w, m, v, name):
    _, r, c = gathered.shape
    tr = 88

    def body(gs_ref, w_ref, m_ref, v_ref, g_ref, d_ref, nm_ref, nv_ref):
        g = gs_ref[0]
        for d in range(1, N_DEV):
            g = g + gs_ref[d]
        delta, nm, nv = _adamw(w_ref[...], g, m_ref[...], v_ref[...])
        g_ref[...] = g
        d_ref[...] = delta
        nm_ref[...] = nm
        nv_ref[...] = nv

    row = pl.BlockSpec((tr, c), lambda t: (t, 0))
    return pl.pallas_call(
        body, name=name, grid=(r // tr,),
        in_specs=[pl.BlockSpec((N_DEV, tr, c), lambda t: (0, t, 0)), row, row, row],
        out_specs=[row] * 4, out_shape=[jax.ShapeDtypeStruct((r, c), F32)] * 4,
        compiler_params=_cparams(1))(gathered, w, m, v)


def adamw_plain(g, w, m, v, name):
    def body(g_ref, w_ref, m_ref, v_ref, d_ref, nm_ref, nv_ref):
        delta, nm, nv = _adamw(w_ref[...], g_ref[...], m_ref[...], v_ref[...])
        d_ref[...] = delta
        nm_ref[...] = nm
        nv_ref[...] = nv

    return pl.pallas_call(body, name=name, out_shape=[jax.ShapeDtypeStruct(w.shape, F32)] * 3)(g, w, m, v)


_PACK = (("ln_in_g", (D,)), ("ln_in_b", (D,)), ("b_in", (DEPTH, N_PROJ)), ("rpb", (DEPTH, N_HEADS, 2 * KH - 1, 2 * KW - 1)),
         ("pool_w", (DEPTH, 4, PGD, PGD)), ("pool_scale", (DEPTH, D_POOL)), ("ln1_g", (DEPTH, D)), ("ln1_b", (DEPTH, D)),
         ("conv_b", (DEPTH, D_FF)), ("ln2_g", (DEPTH, D)), ("ln2_b", (DEPTH, D)), ("conv_w", (DEPTH, 3, D_FF)))
_PACK_LANES = 1024


def _pack_rows(shape):
    return -(-int(np.prod(shape)) // _PACK_LANES)


_PACK_ROWS = -(-sum(_pack_rows(s) for _, s in _PACK) // 88) * 88


def _pack(parts):
    rows = []
    for name, shape in _PACK:
        flat = parts[name].reshape(-1).astype(F32)
        rows.append(jnp.pad(flat, (0, _pack_rows(shape) * _PACK_LANES - flat.shape[0])))
    used = sum(_pack_rows(s) for _, s in _PACK)
    rows.append(jnp.zeros(((_PACK_ROWS - used) * _PACK_LANES,), F32))
    return jnp.concatenate(rows).reshape(_PACK_ROWS, _PACK_LANES)


def _unpack(packed):
    out, r0 = {}, 0
    for name, shape in _PACK:
        n = int(np.prod(shape))
        nr = _pack_rows(shape)
        out[name] = packed[r0:r0 + nr].reshape(-1)[:n].reshape(shape)
        r0 += nr
    return out


def _bias_tables(rpb_l):
    qc = np.arange(GRID_W)[:, None]
    kc = np.arange(GRID_W)[None, :]
    start = np.clip(qc - KW // 2, 0, GRID_W - KW)
    valid = (kc >= start) & (kc < start + KW)
    col = np.clip(kc - qc, -(KW - 1), KW - 1) + KW - 1
    onehot = (col.reshape(-1)[None, :] == np.arange(2 * KW - 1)[:, None]).astype(np.float32)
    rows = jnp.pad(rpb_l, ((0, 0), (0, 1), (0, 0)))
    tab = jnp.einsum("hij,jm->him", rows, jnp.asarray(onehot), precision=lax.Precision.HIGHEST)
    tab = tab.reshape(N_HEADS, KROWS, GRID_W, GRID_W).transpose(0, 2, 1, 3)
    ok = valid[None, :, None, :] & (np.arange(KROWS) < 2 * KH - 1)[None, None, :, None]
    tab = jnp.where(jnp.asarray(ok), tab, NEG_INF).reshape(N_HEADS, GRID_W, KB)
    tab = jnp.stack([tab, jnp.roll(tab, GRID_W, axis=-1)], axis=1)
    return tab, tab[:, :, ::-1, :]


_SHARDED = ("w_in", "w_attn_out", "w_pool_out", "w_mix_out", "w_up", "w_down", "w_ple_gate", "w_ple_proj")
_NAMES = ("ln_in_g", "ln_in_b", "w_in", "b_in", "rpb", "w_attn_out", "pool_w", "pool_scale", "w_pool_out", "w_mix_out",
          "ln1_g", "ln1_b", "w_up", "conv_w", "conv_b", "w_down", "w_ple_gate", "w_ple_proj", "ln2_g", "ln2_b")


def kernel(x, p, ln_in_g, ln_in_b, w_in, b_in, rpb, w_attn_out, pool_w, pool_scale, w_pool_out, w_mix_out, ln1_g, ln1_b, w_up, conv_w, conv_b, w_down, w_ple_gate, w_ple_proj, ln2_g, ln2_b, loss_target, m_ln_in_g, m_ln_in_b, m_w_in, m_b_in, m_rpb, m_w_attn_out, m_pool_w, m_pool_scale, m_w_pool_out, m_w_mix_out, m_ln1_g, m_ln1_b, m_w_up, m_conv_w, m_conv_b, m_w_down, m_w_ple_gate, m_w_ple_proj, m_ln2_g, m_ln2_b, v_ln_in_g, v_ln_in_b, v_w_in, v_b_in, v_rpb, v_w_attn_out, v_pool_w, v_pool_scale, v_w_pool_out, v_w_mix_out, v_ln1_g, v_ln1_b, v_w_up, v_conv_w, v_conv_b, v_w_down, v_w_ple_gate, v_w_ple_proj, v_ln2_g, v_ln2_b):
    a = dict(locals())
    W = {n: a[n] for n in _NAMES}
    M = {n: a["m_" + n] for n in _NAMES}
    V = {n: a["v_" + n] for n in _NAMES}
    xi, yi, ci = _coords()
    me = _dev_index(xi, yi, ci)
    x2, tgt = x[0], loss_target[0]
    pb = p[:, 0].astype(BF16)

    flip = lambda d: {**d, "w_up": d["w_up"].transpose(0, 2, 1)}
    ex = _Exchange(flip(W), flip(M), flip(V))
    loss_part, dx, parts = _local_step(x2, tgt, pb, W, ex)
    loss = lax.psum(loss_part[0, 0], AXES)
    stacks = {**ex.stacks, "w_up": [t.transpose(0, 2, 1) for t in ex.stacks["w_up"]]}

    (gath,) = all_gather([_pack(parts)], "ag_small_grads")
    zero_cw = jnp.zeros((DEPTH, 3, D_FF), F32)
    packs = [_pack({**{n: src[n] for n, _ in _PACK if n != "conv_w"}, "conv_w": zero_cw}) for src in (W, M, V)]
    outs = [_unpack(o) for o in adamw_replicated(gath, *packs, "adamw_replicated")]
    g_cw = lax.dynamic_slice_in_dim(outs[0]["conv_w"], me * FF_SHARD, FF_SHARD, axis=2)
    flat = lambda t: t.reshape(DEPTH * 3, FF_SHARD)
    cw_out = [o.reshape(DEPTH, 3, FF_SHARD) for o in
              adamw_plain(flat(g_cw), flat(conv_w), flat(m_conv_w), flat(v_conv_w), "adamw_conv_w")]
    res = []
    for k in range(4):
        d = {n: stacks[n][k] for n in _SHARDED}
        d.update({n: outs[k][n] for n, _ in _PACK if n != "conv_w"})
        d["conv_w"] = g_cw if k == 0 else cw_out[k - 1]
        res.append(d)
    return (loss, dx[None], *[res[k][n] for k in range(4) for n in _NAMES])


class _Exchange:
    GROUPS = (("w_ple_gate", "w_ple_proj", "w_down", "w_up"), ("w_mix_out", "w_attn_out", "w_pool_out"), ("w_in",))
    FIRST = ("w_in",)

    def __init__(self, W, M, V):
        self.W, self.M, self.V = W, M, V
        xi, yi, ci = _coords()
        me = _dev_index(xi, yi, ci)
        self.me = me.astype(I32).reshape(1)
        self.lands = [{n: lax.dynamic_update_index_in_dim(lax.empty((N_DEV,) + W[n].shape[1:], BF16),
                                                          W[n][li].astype(BF16), me, 0) for n in _SHARDED}
                      for li in range(DEPTH)]
        (cw,) = all_gather([W["conv_w"]], "ag_conv_w")
        self.cw = cw.transpose(1, 2, 0, 3).reshape(DEPTH, 3, 4, FF_BLK).transpose(0, 2, 1, 3)
        self.ag, self.fwd, self.rs, self.pending = {}, {}, {}, {}
        self.stacks = {n: [lax.empty((DEPTH,) + W[n].shape[1:], F32) for _ in range(4)] for n in _SHARDED}
        self.late = tuple(n for n in _SHARDED if n not in self.FIRST)
        self.ag[0] = ag_start([self.lands[0][n] for n in self.FIRST], cw, "ag_start0")

    def tokens(self):
        return [self.ag[0][3]]

    def prefetch(self, li, after):
        send, recv, lands, _ = self.ag[li]
        self.fwd[li] = ag_forward(send, recv, lands, after, f"ag_forward{li}")
        if li == 0:
            self.ag["0b"] = ag_start([self.lands[0][n] for n in self.late], self.fwd[0][2][0], "ag_start0b")

    def weights(self, li, after):
        send, recv, _, _ = self.ag.pop(li)
        fsend, frecv, lands = self.fwd.pop(li)
        lands = ag_finish(send, recv, fsend, frecv, lands, after, f"ag_finish{li}")
        if li == 0:
            return dict(zip(self.FIRST, lands)), self.cw[li], (self.ag["0b"][3],)
        tokens = ()
        if li + 1 < DEPTH:
            self.ag[li + 1] = ag_start([self.lands[li + 1][n] for n in _SHARDED], lands[0], f"ag_start{li + 1}")
            tokens = (self.ag[li + 1][3],)
        return dict(zip(_SHARDED, lands)), self.cw[li], tokens

    def rest(self, li, G, mid, after):
        if li != 0:
            return G, ()
        send, recv, lands, _ = self.ag.pop("0b")
        fsend, frecv, lands = ag_forward(send, recv, lands, mid, "ag_forward0b")
        lands = ag_finish(send, recv, fsend, frecv, lands, after, "ag_finish0b")
        self.ag[1] = ag_start([self.lands[1][n] for n in _SHARDED], lands[0], "ag_start1")
        return {**G, **dict(zip(self.late, lands))}, (self.ag[1][3],)

    def grads(self, li, group, gw):
        self.pending.setdefault(li, {}).update(gw)
        if li != 0 and group != len(self.GROUPS) - 1:
            return None
        gw = self.pending.pop(li)
        tag = f"{li}_{group}" if li == 0 else f"{li}"
        send, recv, parts, lands, token = rs_start(list(gw.values()), f"rs_start{tag}")
        self.rs.setdefault(li, []).append((tag, tuple(gw), send, recv, parts, lands))
        return token

    def update(self, li, after):
        for tag, names, send, recv, parts, lands in self.rs.pop(li):
            parts, lands = rs_finish(send, recv, parts, lands, after, f"rs_finish{tag}")
            for n, part, land in zip(names, parts, lands):
                self.stacks[n] = adamw_shard(self.me, part, land, self.W[n], self.M[n], self.V[n], li, self.stacks[n],
                                             f"adamw_{n}{li}")


def _local_step(x2, tgt, pb, W, ex):
    depth = W["rpb"].shape[0]
    vec = lambda t: t.reshape(1, -1)
    ln1_g, ln1_b, ln2_g, ln2_b = W["ln1_g"], W["ln1_b"], W["ln2_g"], W["ln2_b"]
    b_in, rpb, pool_scale = W["b_in"], W["rpb"], W["pool_scale"]
    cb_full = W["conv_b"].reshape(depth, 4, 1, FF_BLK)
    pool_w_b = W["pool_w"].astype(BF16)

    h, hb = ln_fwd(x2, vec(W["ln_in_g"]), vec(W["ln_in_b"]), "ln_in", after=ex.tokens())
    ex.prefetch(0, hb)
    saved = []
    for li in range(depth):
        G, cw, tokens = ex.weights(li, hb)
        e_tab, e_rev = _bias_tables(rpb[li])
        bias = vec(b_in[li])
        proj = proj_fwd(hb, G["w_in"], bias, li, 0, N_DEV, BF16, f"proj{li}", after=tokens)
        u = proj_fwd(hb, G["w_in"], bias, li, 3, 1, F32, f"proj_u{li}")
        att = attn_fwd(proj, e_tab, f"attn{li}")
        pm, pw = pool_fwd(u, pool_w_b[li], vec(pool_scale[li]), f"pool{li}")
        G, tokens = ex.rest(li, G, att, pw)
        mg, ya, yp = merge_fwd(att, pw, G["w_attn_out"], G["w_pool_out"], proj, li, f"merge{li}", after=tokens)
        if li + 1 < depth:
            ex.prefetch(li + 1, mg)
        z1, h1, h1b = mix_ln_fwd(mg, G["w_mix_out"], h, vec(ln1_g[li]), vec(ln1_b[li]), li, f"mix_ln{li}")
        up = up_fwd(h1b, G["w_up"], li, f"up{li}")
        t = ffn_act_fwd(up, cw, cb_full[li], f"ffn_act{li}")
        z2, h2, h2b, pg, pp = down_ple_ln_fwd(t, G["w_down"], h1b, G["w_ple_gate"], pb[li], G["w_ple_proj"], h1,
                                              vec(ln2_g[li]), vec(ln2_b[li]), li, f"down_ln{li}")
        saved.append(dict(hb=hb, proj=proj, att=att, pm=pm, pw=pw, mg=mg, ya=ya, yp=yp, z1=z1, h1b=h1b, up=up, t=t,
                          z2=z2, pg=pg, pp=pp, e_rev=e_rev, G=G, cw=cw))
        h, hb = h2, h2b

    dh, loss_part = loss_bwd(h, tgt, "loss")
    small = {n: [None] * depth for n in ("b_in", "rpb", "pool_w", "pool_scale", "ln1_g", "ln1_b", "conv_b", "ln2_g",
                                         "ln2_b", "conv_w")}
    token = ()
    tok = lambda t: () if t is None else (t,)
    for li in reversed(range(depth)):
        sv = saved[li]
        G, cw = sv["G"], sv["cw"]
        dz2, dz2b, dpg, dpp, dg2, db2 = ln2_ple_bwd(dh, sv["z2"], vec(ln2_g[li]), sv["pg"], sv["pp"], f"ln2_bwd{li}",
                                                    after=token)
        gw = {}
        gw["w_ple_gate"] = wgrad_rows(sv["h1b"], dpg, f"dw_pg{li}")
        gw["w_ple_proj"] = wgrad_cols(pb[li], dpp, f"dw_pp{li}")
        gw["w_down"] = wgrad_down(sv["t"], dz2b, f"dw_down{li}").reshape(N_DEV, FF_SHARD, D)
        dhv, dhg, dcw, dcb = ffn_act_bwd(dz2b, G["w_down"], sv["up"], cw, cb_full[li], li, f"ffn_bwd{li}")
        gw["w_up"] = wgrad_up(sv["h1b"], dhv, dhg, f"dw_up{li}")
        token = tok(ex.grads(li, 0, gw))
        dz1, dz1b, dg1, db1 = dh1_ln1_bwd(dz2, dpg, G["w_ple_gate"], dhv, dhg, G["w_up"], sv["z1"], vec(ln1_g[li]), li,
                                          f"ln1_bwd{li}", after=token)
        gw = {"w_mix_out": wgrad_rows(sv["mg"], dz1b, f"dw_mix{li}")}
        dya, dyp, dga, dgb = merge_bwd(dz1b, G["w_mix_out"], sv["proj"], sv["ya"], sv["yp"], li, f"merge_bwd{li}")
        gw["w_attn_out"] = wgrad_cols(sv["att"], dya, f"dw_ao{li}")
        gw["w_pool_out"] = wgrad_cols(sv["pw"], dyp, f"dw_po{li}")
        token = tok(ex.grads(li, 1, gw))
        da = attn_out_bwd(dya, G["w_attn_out"], li, f"da{li}", after=token)
        du, dpool_w, dpool_sc = pool_bwd(dyp, G["w_pool_out"], sv["pm"], pool_w_b[li], vec(pool_scale[li]), li,
                                         f"pool_bwd{li}")
        dq, dk, dv, drpb = attn_bwd(sv["proj"], da, sv["e_rev"], f"attn_bwd{li}")
        dproj = jnp.concatenate([dq, dk, dv, du, dga, dgb], axis=1)
        dw_in, db_in = wgrad_cols(sv["hb"], dproj, f"dw_in{li}", with_colsum=True)
        token = tok(ex.grads(li, 2, {"w_in": dw_in}))
        dh = dh0_bwd(dz1, dproj, G["w_in"], li, f"dh0{li}", after=token)
        small["b_in"][li] = db_in.reshape(N_PROJ)
        small["rpb"][li] = drpb.reshape(N_HEADS, KROWS, GRID_W)[:, :2 * KH - 1, :2 * KW - 1]
        small["pool_w"][li] = dpool_w
        small["pool_scale"][li] = dpool_sc.reshape(D_POOL)
        small["ln1_g"][li], small["ln1_b"][li] = dg1.reshape(D), db1.reshape(D)
        small["ln2_g"][li], small["ln2_b"][li] = dg2.reshape(D), db2.reshape(D)
        small["conv_b"][li] = dcb.reshape(D_FF)
        small["conv_w"][li] = dcw.transpose(1, 0, 2).reshape(3, D_FF)
        if li + 1 < depth:
            ex.update(li + 1, dh)
    dx, dg_in, db_in0 = ln_bwd(dh, x2, vec(W["ln_in_g"]), "ln_in_bwd", after=token)
    ex.update(0, dx)
    parts = {n: jnp.stack(v_) for n, v_ in small.items()}
    parts["ln_in_g"], parts["ln_in_b"] = dg_in.reshape(D), db_in0.reshape(D)
    return loss_part, dx, parts
```

```python
import numpy as np
import jax
import jax.numpy as jnp
from jax import lax
from jax.experimental import pallas as pl
from jax.experimental.pallas import tpu as pltpu

F32 = jnp.float32
BF16 = jnp.bfloat16
I32 = jnp.int32

D = 1024
DEPTH = 4
GRID_W = 64
N_HEADS = 8
HEAD_DIM = 64
D_ATTN = 512
KH = 8
KW = 16
POOL_WINDOWS = (2, 4, 8, 16)
D_POOL = 512
PGD = 128
D_FF = 2816
PLE_DIM = 256
N_PROJ = 4096
ALPHA = (2 * DEPTH) ** 0.25
LN_EPS = 1e-5
NEG_INF = -1e30
ATT_SCALE = HEAD_DIM ** -0.5
ADAM_LR = 0.001
ADAM_B1 = 0.9
ADAM_B2 = 0.999
ADAM_EPS = 1e-08
ADAM_WD = 0.01
ADAM_STEP = 10

N_DEV = 8
AXES = ("x", "y", "c")
FF_BLK = D_FF // 4
FF_SHARD = D_FF // N_DEV
QROWS = 8
KROWS = 16
QB = QROWS * GRID_W
KB = KROWS * GRID_W
V7X_VMEM_LIMIT = 56 * 2 ** 20
MESH = pl.DeviceIdType.MESH
ANY = pl.BlockSpec(memory_space=pl.ANY)


def _cparams(n_grid):
    return pltpu.CompilerParams(dimension_semantics=("arbitrary",) * n_grid, vmem_limit_bytes=V7X_VMEM_LIMIT)


def _nn(a, b):
    return lax.dot_general(a, b, (((1,), (0,)), ((), ())), preferred_element_type=F32)


def _nt(a, b):
    return lax.dot_general(a, b, (((1,), (1,)), ((), ())), preferred_element_type=F32)


def _tn(a, b):
    return lax.dot_general(a, b, (((0,), (0,)), ((), ())), preferred_element_type=F32)


def _sigmoid(x):
    return 1.0 / (1.0 + jnp.exp(-x))


def _ln_fwd(z, g, b):
    mu = jnp.mean(z, axis=-1, keepdims=True)
    xc = z - mu
    var = jnp.mean(xc * xc, axis=-1, keepdims=True)
    return xc * lax.rsqrt(var + LN_EPS) * g + b


def _ln_bwd(dh, z, g):
    mu = jnp.mean(z, axis=-1, keepdims=True)
    xc = z - mu
    var = jnp.mean(xc * xc, axis=-1, keepdims=True)
    rstd = lax.rsqrt(var + LN_EPS)
    xhat = xc * rstd
    dxh = dh * g
    m1 = jnp.mean(dxh, axis=-1, keepdims=True)
    m2 = jnp.mean(dxh * xhat, axis=-1, keepdims=True)
    return rstd * (dxh - m1 - xhat * m2), dh * xhat


def _colsum(x):
    return jnp.sum(x, axis=0, keepdims=True)


def _lane_cat(ref):
    return jnp.concatenate([ref[j] for j in range(ref.shape[0])], axis=1)


def _row_cat(ref):
    n, r, c = ref.shape
    return ref[...].reshape(n * r, c)


def _shards(n, r, c, li, j_of=None):
    del li
    if j_of is None:
        return pl.BlockSpec((n, r, c), lambda *_: (0, 0, 0))
    return pl.BlockSpec((n, r, c), lambda *g: (j_of(*g), 0, 0))


def _shard(r, c, li, j_of):
    del li
    return pl.BlockSpec((None, r, c), lambda *g: (j_of(*g), 0, 0))


def ln_fwd(x, g, b, name, after=()):
    s = x.shape[0]
    tm = 512
    na = len(after)

    def body(x_ref, g_ref, b_ref, *rest):
        h_ref, hb_ref = rest[na:]
        h = _ln_fwd(x_ref[...], g_ref[...], b_ref[...])
        h_ref[...] = h
        hb_ref[...] = h.astype(BF16)

    row = pl.BlockSpec((tm, D), lambda i: (i, 0))
    vec = pl.BlockSpec((1, D), lambda i: (0, 0))
    return pl.pallas_call(
        body, name=name, grid=(s // tm,), in_specs=[row, vec, vec] + [ANY] * na, out_specs=[row, row],
        out_shape=[jax.ShapeDtypeStruct((s, D), F32), jax.ShapeDtypeStruct((s, D), BF16)],
        compiler_params=_cparams(1))(x, g, b, *after)


def proj_fwd(hb, win, bias, li, j0, nj, out_dtype, name, after=()):
    s = hb.shape[0]
    bn = N_PROJ // N_DEV
    tm = 1024

    def body(a_ref, w_ref, b_ref, *rest):
        rest[-1][...] = (_nn(a_ref[...], w_ref[...]) + b_ref[...]).astype(out_dtype)

    return pl.pallas_call(
        body, name=name, grid=(s // tm, nj),
        in_specs=[pl.BlockSpec((tm, D), lambda i, j: (i, 0)),
                  _shard(D, bn, li, lambda i, j: j0 + j),
                  pl.BlockSpec((1, bn), lambda i, j: (0, j0 + j))] + [ANY] * len(after),
        out_specs=pl.BlockSpec((tm, bn), lambda i, j: (i, j)),
        out_shape=jax.ShapeDtypeStruct((s, nj * bn), out_dtype),
        compiler_params=_cparams(2))(hb, win, bias, *after)


def _attn_types(b, nb):
    first, last = 0, (nb * QROWS - KROWS) * GRID_W
    mid = pl.multiple_of((QROWS * b - KH // 2) * GRID_W, 256)
    return ((b == 0, first), ((b > 0) & (b < nb - 1), mid), (b == nb - 1, last))


def _attn_row(btype, qr):
    lo, delta = ((max(qr - KH // 2, 0), 0), (qr, -(KH // 2)), (min(qr + KH // 2, KH), -KH))[btype]
    return lo, (qr - delta - (KH - 1)) % KROWS, lo - qr + delta + KH - 1


def _row_window(lo):
    pad = (lo % 2) * GRID_W
    return (lo // 2) * 128, KH * GRID_W + 2 * pad, pad


def _lanes(ref, start, width):
    start %= KB
    if start + width <= KB:
        return ref[:, start:start + width]
    return jnp.concatenate([ref[:, start:], ref[:, :start + width - KB]], axis=1)


def _row_logits(s_ref, e_ref, hh, rows, btype, qr):
    lo, shift, _ = _attn_row(btype, qr)
    a0, w, pad = _row_window(lo)
    e = e_ref.at[hh, shift % 2]
    sb = s_ref[rows, a0:a0 + w] + _lanes(e, a0 - (shift - shift % 2) * GRID_W, w)
    if pad:
        lane = lax.broadcasted_iota(I32, (1, w), 1)
        sb = jnp.where((lane >= pad) & (lane < w - pad), sb, NEG_INF)
    return sb, a0, w, pad


def _store_row(ref, rows, a0, w, val):
    if a0:
        ref[rows, 0:a0] = jnp.zeros((GRID_W, a0), ref.dtype)
    ref[rows, a0:a0 + w] = val.astype(ref.dtype)
    if a0 + w < KB:
        ref[rows, a0 + w:KB] = jnp.zeros((GRID_W, KB - a0 - w), ref.dtype)


def attn_fwd(proj, e_tab, name):
    s = proj.shape[0]
    nb = s // QB

    def body(q_ref, k_ref, v_ref, e_ref, o_ref, s_ref, p_ref):
        q = q_ref[...] * ATT_SCALE
        lane = lax.broadcasted_iota(I32, (1, 128), 1)

        def block(btype, k0):
            kwin = k_ref[pl.ds(k0, KB), :]
            vwin = v_ref[pl.ds(k0, KB), :]
            acc = jnp.zeros((QB, 128), F32)
            for hh in range(2):
                lm = (lane // HEAD_DIM) == hh
                qh = jnp.where(lm, q, jnp.zeros_like(q))
                vh = jnp.where(lm, vwin, jnp.zeros_like(vwin))
                s_ref[...] = _nt(qh, kwin)
                for qr in range(QROWS):
                    rows = slice(qr * GRID_W, (qr + 1) * GRID_W)
                    sb, a0, w, _ = _row_logits(s_ref, e_ref, hh, rows, btype, qr)
                    p = jnp.exp(sb - jnp.max(sb, axis=1, keepdims=True))
                    _store_row(p_ref, rows, a0, w, p * (1.0 / jnp.sum(p, axis=1, keepdims=True)))
                acc = acc + _nn(p_ref[...], vh)
            o_ref[...] = acc.astype(BF16)

        for btype, (cond, k0) in enumerate(_attn_types(pl.program_id(1), nb)):
            pl.when(cond)(lambda btype=btype, k0=k0: block(btype, k0))

    return pl.pallas_call(
        body, name=name, grid=(4, nb),
        in_specs=[pl.BlockSpec((QB, 128), lambda j, b: (b, j)),
                  pl.BlockSpec((s, 128), lambda j, b: (0, 4 + j)),
                  pl.BlockSpec((s, 128), lambda j, b: (0, 8 + j)),
                  pl.BlockSpec((2, 2, GRID_W, KB), lambda j, b: (j, 0, 0, 0))],
        out_specs=pl.BlockSpec((QB, 128), lambda j, b: (b, j)),
        out_shape=jax.ShapeDtypeStruct((s, D_ATTN), BF16),
        scratch_shapes=[pltpu.VMEM((QB, KB), F32), pltpu.VMEM((QB, KB), BF16)],
        compiler_params=_cparams(2))(proj, proj, proj, e_tab)


_POOL_PAD = 8


def _pool_counts(s, w):
    t = lax.broadcasted_iota(I32, (s, 1), 0)
    return (jnp.minimum(t + w // 2, s) - jnp.maximum(t - w // 2, 0)).astype(F32)


def _window_sum(x, w, back_first):
    s = x.shape[0]
    z = jnp.zeros((_POOL_PAD, x.shape[1]), F32)
    xe = jnp.concatenate([z, x, z], axis=0)
    n = s + 2 * _POOL_PAD
    acc = xe + pltpu.roll(xe, 1 if back_first else n - 1, 0)
    k = 1
    while 2 * k < w:
        acc = pltpu.roll(acc, k, 0) + pltpu.roll(acc, n - k, 0)
        k *= 2
    return acc[_POOL_PAD:_POOL_PAD + s, :]


def pool_fwd(u, pool_w, pool_scale, name):
    s = u.shape[0]

    def body(u_ref, w_ref, sc_ref, pm_ref, pw_ref):
        for g, w in enumerate(POOL_WINDOWS):
            cols = slice(g * PGD, (g + 1) * PGD)
            ug = u_ref[:, cols]
            pm = (_window_sum(ug, w, True) / _pool_counts(s, w) - ug).astype(BF16)
            pm_ref[:, cols] = pm
            pw_ref[:, cols] = (_nn(pm, w_ref[g]) * sc_ref[:, cols]).astype(BF16)

    full = lambda shape: pl.BlockSpec(shape, lambda i: (0,) * len(shape))
    return pl.pallas_call(
        body, name=name, grid=(1,),
        in_specs=[full((s, D_POOL)), full((4, PGD, PGD)), full((1, D_POOL))],
        out_specs=[full((s, D_POOL)), full((s, D_POOL))],
        out_shape=[jax.ShapeDtypeStruct((s, D_POOL), BF16)] * 2,
        compiler_params=_cparams(1))(u, pool_w, pool_scale)


def merge_fwd(a, pw, wao, wpo, proj, li, name, after=()):
    s = a.shape[0]
    tm, tn = 512, 512
    nt = D // tn
    per = tn // 128

    def body(a_ref, pw_ref, wa_ref, wp_ref, ga_ref, gb_ref, *rest):
        mg_ref, ya_ref, yp_ref = rest[len(after):]
        ya = _nn(a_ref[...], _lane_cat(wa_ref))
        yp = _nn(pw_ref[...], _lane_cat(wp_ref))
        mg = _sigmoid(ga_ref[...].astype(F32)) * ya + _sigmoid(gb_ref[...].astype(F32)) * yp
        mg_ref[...] = mg.astype(BF16)
        ya_ref[...] = ya.astype(BF16)
        yp_ref[...] = yp.astype(BF16)

    act = pl.BlockSpec((tm, D_ATTN), lambda i, j: (i, 0))
    wsp = _shards(per, D_ATTN, 128, li, lambda i, j: j)
    out = pl.BlockSpec((tm, tn), lambda i, j: (i, j))
    ga0 = (3 * D_ATTN + D_POOL) // tn
    return pl.pallas_call(
        body, name=name, grid=(s // tm, nt),
        in_specs=[act, act, wsp, wsp,
                  pl.BlockSpec((tm, tn), lambda i, j: (i, ga0 + j)),
                  pl.BlockSpec((tm, tn), lambda i, j: (i, ga0 + nt + j))] + [ANY] * len(after),
        out_specs=[out, out, out],
        out_shape=[jax.ShapeDtypeStruct((s, D), BF16)] * 3,
        compiler_params=_cparams(2))(a, pw, wao, wpo, proj, proj, *after)


def mix_ln_fwd(mg, wmix, h0, g, b, li, name):
    s = mg.shape[0]
    tm = 256

    def body(mg_ref, w_ref, h0_ref, g_ref, b_ref, z_ref, h_ref, hb_ref):
        z = ALPHA * h0_ref[...] + _nn(mg_ref[...], _row_cat(w_ref))
        h = _ln_fwd(z, g_ref[...], b_ref[...])
        z_ref[...] = z
        h_ref[...] = h
        hb_ref[...] = h.astype(BF16)

    row = pl.BlockSpec((tm, D), lambda i: (i, 0))
    vec = pl.BlockSpec((1, D), lambda i: (0, 0))
    return pl.pallas_call(
        body, name=name, grid=(s // tm,),
        in_specs=[row, _shards(N_DEV, D // N_DEV, D, li), row, vec, vec],
        out_specs=[row, row, row],
        out_shape=[jax.ShapeDtypeStruct((s, D), F32), jax.ShapeDtypeStruct((s, D), F32),
                   jax.ShapeDtypeStruct((s, D), BF16)],
        compiler_params=_cparams(1))(mg, wmix, h0, g, b)


def up_fwd(hb, wup, li, name):
    s = hb.shape[0]
    tm = 1024

    def body(a_ref, w_ref, o_ref):
        o_ref[...] = _nt(a_ref[...], w_ref[...]).astype(BF16)

    return pl.pallas_call(
        body, name=name, grid=(s // tm, N_DEV),
        in_specs=[pl.BlockSpec((tm, D), lambda i, j: (i, 0)), _shard(FF_BLK, D, li, lambda i, j: j)],
        out_specs=pl.BlockSpec((None, tm, FF_BLK), lambda i, j: (j, i, 0)),
        out_shape=jax.ShapeDtypeStruct((N_DEV, s, FF_BLK), BF16),
        compiler_params=_cparams(2))(hb, wup)


_SQRT_HALF = 0.7071067811865476
_INV_SQRT_2PI = 0.3989422804014327


def _shift_rows(x, prev_row, next_row):
    n = x.shape[0]
    r = lax.broadcasted_iota(I32, (n, 1), 0)
    back = jnp.where(r == 0, prev_row, pltpu.roll(x, 1, 0))
    fwd = jnp.where(r == n - 1, next_row, pltpu.roll(x, n - 1, 0))
    return back, fwd


HALO = 16


def _halo_maps(tm, s):
    th = tm // HALO
    return (lambda i: jnp.maximum(i * th - 1, 0)), (lambda i: jnp.minimum((i + 1) * th, s // HALO - 1))


def _slab_specs(tm, s, blk_of):
    before, after = _halo_maps(tm, s)
    main = pl.BlockSpec((None, tm, FF_BLK), lambda c, i: (blk_of(c), i, 0))
    prev = pl.BlockSpec((None, HALO, FF_BLK), lambda c, i: (blk_of(c), before(i), 0))
    nxt = pl.BlockSpec((None, HALO, FF_BLK), lambda c, i: (blk_of(c), after(i), 0))
    return main, prev, nxt


def ffn_act_fwd(up, conv_w, conv_b, name):
    s = up.shape[1]
    tm = 512
    nt = s // tm
    hv_main, _, _ = _slab_specs(tm, s, lambda c: c)
    hg_main, hg_prev, hg_next = _slab_specs(tm, s, lambda c: 4 + c)

    def body(hv_ref, hg_ref, hp_ref, hn_ref, cw_ref, cb_ref, t_ref):
        i = pl.program_id(1)
        hg = hg_ref[...].astype(F32)
        prow = jnp.where(i == 0, 0.0, hp_ref[...].astype(F32)[HALO - 1:HALO, :])
        nrow = jnp.where(i == nt - 1, 0.0, hn_ref[...].astype(F32)[0:1, :])
        back, fwd = _shift_rows(hg, prow, nrow)
        c = back * cw_ref[0:1, :] + hg * cw_ref[1:2, :] + fwd * cw_ref[2:3, :] + cb_ref[...]
        act = 0.5 * c * (1.0 + lax.erf(c * _SQRT_HALF))
        t_ref[...] = (act * hv_ref[...].astype(F32)).astype(BF16)

    return pl.pallas_call(
        body, name=name, grid=(4, nt),
        in_specs=[hv_main, hg_main, hg_prev, hg_next,
                  pl.BlockSpec((None, 3, FF_BLK), lambda c, i: (c, 0, 0)),
                  pl.BlockSpec((None, 1, FF_BLK), lambda c, i: (c, 0, 0))],
        out_specs=pl.BlockSpec((None, tm, FF_BLK), lambda c, i: (c, i, 0)),
        out_shape=jax.ShapeDtypeStruct((4, s, FF_BLK), BF16),
        compiler_params=_cparams(2))(up, up, up, up, conv_w, conv_b)


def down_ple_ln_fwd(t, wdown, hb, wpg, pb, wpp, h1, g, b, li, name):
    s = hb.shape[0]
    tm = 256

    def body(t_ref, wd_ref, hb_ref, wpg_ref, p_ref, wpp_ref, h1_ref, g_ref, b_ref,
             z_ref, h_ref, hbo_ref, pg_ref, pp_ref):
        wd = _row_cat(wd_ref)
        ffn = _nn(t_ref[0], wd[0:FF_BLK, :])
        for c in range(1, 4):
            ffn = ffn + _nn(t_ref[c], wd[c * FF_BLK:(c + 1) * FF_BLK, :])
        pg = _nn(hb_ref[...], _row_cat(wpg_ref))
        pp = _nn(p_ref[...], _lane_cat(wpp_ref))
        z = ALPHA * h1_ref[...] + ffn + _sigmoid(pg) * pp
        h = _ln_fwd(z, g_ref[...], b_ref[...])
        z_ref[...] = z
        h_ref[...] = h
        hbo_ref[...] = h.astype(BF16)
        pg_ref[...] = pg.astype(BF16)
        pp_ref[...] = pp.astype(BF16)

    row = pl.BlockSpec((tm, D), lambda i: (i, 0))
    vec = pl.BlockSpec((1, D), lambda i: (0, 0))
    return pl.pallas_call(
        body, name=name, grid=(s // tm,),
        in_specs=[pl.BlockSpec((4, tm, FF_BLK), lambda i: (0, i, 0)),
                  _shards(N_DEV, FF_SHARD, D, li),
                  row, _shards(N_DEV, D // N_DEV, D, li),
                  pl.BlockSpec((tm, PLE_DIM), lambda i: (i, 0)),
                  _shards(N_DEV, PLE_DIM, 128, li),
                  row, vec, vec],
        out_specs=[row] * 5,
        out_shape=[jax.ShapeDtypeStruct((s, D), F32), jax.ShapeDtypeStruct((s, D), F32),
                   jax.ShapeDtypeStruct((s, D), BF16), jax.ShapeDtypeStruct((s, D), BF16),
                   jax.ShapeDtypeStruct((s, D), BF16)],
        compiler_params=_cparams(1))(t, wdown, hb, wpg, pb, wpp, h1, g, b)


def loss_bwd(h, target, name):
    s = h.shape[0]
    tm = 512

    def body(h_ref, t_ref, dh_ref, l_ref):
        @pl.when(pl.program_id(0) == 0)
        def _():
            l_ref[...] = jnp.zeros_like(l_ref)
        e = h_ref[...] - t_ref[...]
        dh_ref[...] = e * (1.0 / D)
        l_ref[...] += 0.5 * jnp.sum(jnp.mean(e * e, axis=-1, keepdims=True), axis=0, keepdims=True)

    row = pl.BlockSpec((tm, D), lambda i: (i, 0))
    return pl.pallas_call(
        body, name=name, grid=(s // tm,), in_specs=[row, row],
        out_specs=[row, pl.BlockSpec((1, 1), lambda i: (0, 0))],
        out_shape=[jax.ShapeDtypeStruct((s, D), F32), jax.ShapeDtypeStruct((1, 1), F32)],
        compiler_params=_cparams(1))(h, target)


def ln_bwd(dh, z, g, name, after=()):
    s = dh.shape[0]
    tm = 512
    na = len(after)

    def body(dh_ref, z_ref, g_ref, *rest):
        dz_ref, dg_ref, db_ref = rest[na:]

        @pl.when(pl.program_id(0) == 0)
        def _():
            dg_ref[...] = jnp.zeros_like(dg_ref)
            db_ref[...] = jnp.zeros_like(db_ref)
        dh = dh_ref[...]
        dz, dgx = _ln_bwd(dh, z_ref[...], g_ref[...])
        dz_ref[...] = dz
        dg_ref[...] += _colsum(dgx)
        db_ref[...] += _colsum(dh)

    row = pl.BlockSpec((tm, D), lambda i: (i, 0))
    vec = pl.BlockSpec((1, D), lambda i: (0, 0))
    return pl.pallas_call(
        body, name=name, grid=(s // tm,), in_specs=[row, row, vec] + [ANY] * na, out_specs=[row, vec, vec],
        out_shape=[jax.ShapeDtypeStruct((s, D), F32), jax.ShapeDtypeStruct((1, D), F32),
                   jax.ShapeDtypeStruct((1, D), F32)],
        compiler_params=_cparams(1))(dh, z, g, *after)


def ln2_ple_bwd(dh, z, g, pg, pp, name, after=()):
    s = dh.shape[0]
    tm = 512
    na = len(after)

    def body(dh_ref, z_ref, g_ref, pg_ref, pp_ref, *rest):
        dz_ref, dzb_ref, dpg_ref, dpp_ref, dg_ref, db_ref = rest[na:]

        @pl.when(pl.program_id(0) == 0)
        def _():
            dg_ref[...] = jnp.zeros_like(dg_ref)
            db_ref[...] = jnp.zeros_like(db_ref)
        dh = dh_ref[...]
        dz, dgx = _ln_bwd(dh, z_ref[...], g_ref[...])
        sg = _sigmoid(pg_ref[...].astype(F32))
        dz_ref[...] = dz
        dzb_ref[...] = dz.astype(BF16)
        dpg_ref[...] = (dz * pp_ref[...].astype(F32) * sg * (1.0 - sg)).astype(BF16)
        dpp_ref[...] = (dz * sg).astype(BF16)
        dg_ref[...] += _colsum(dgx)
        db_ref[...] += _colsum(dh)

    row = pl.BlockSpec((tm, D), lambda i: (i, 0))
    vec = pl.BlockSpec((1, D), lambda i: (0, 0))
    return pl.pallas_call(
        body, name=name, grid=(s // tm,), in_specs=[row, row, vec, row, row] + [ANY] * na,
        out_specs=[row, row, row, row, vec, vec],
        out_shape=[jax.ShapeDtypeStruct((s, D), F32)] + [jax.ShapeDtypeStruct((s, D), BF16)] * 3
        + [jax.ShapeDtypeStruct((1, D), F32)] * 2,
        compiler_params=_cparams(1))(dh, z, g, pg, pp, *after)


def wgrad_rows(a, dy, name, after=()):
    s, k = a.shape
    n = dy.shape[1]
    kb = k // N_DEV

    def body(a_ref, dy_ref, *rest):
        rest[-1][...] = _tn(a_ref[...], dy_ref[...]).astype(BF16)

    return pl.pallas_call(
        body, name=name, grid=(N_DEV,),
        in_specs=[pl.BlockSpec((s, kb), lambda j: (0, j)), pl.BlockSpec((s, n), lambda j: (0, 0))] + [ANY] * len(after),
        out_specs=pl.BlockSpec((None, kb, n), lambda j: (j, 0, 0)),
        out_shape=jax.ShapeDtypeStruct((N_DEV, kb, n), BF16),
        compiler_params=_cparams(1))(a, dy, *after)


def wgrad_cols(a, dy, name, with_colsum=False):
    s, k = a.shape
    n = dy.shape[1]
    nb = n // N_DEV

    def body(a_ref, dy_ref, o_ref, *cs_ref):
        dy = dy_ref[...]
        o_ref[...] = _tn(a_ref[...], dy).astype(BF16)
        if with_colsum:
            cs_ref[0][...] = _colsum(dy.astype(F32))

    out_specs = [pl.BlockSpec((None, k, nb), lambda j: (j, 0, 0))]
    out_shape = [jax.ShapeDtypeStruct((N_DEV, k, nb), BF16)]
    if with_colsum:
        out_specs.append(pl.BlockSpec((1, nb), lambda j: (0, j)))
        out_shape.append(jax.ShapeDtypeStruct((1, n), F32))
    res = pl.pallas_call(
        body, name=name, grid=(N_DEV,),
        in_specs=[pl.BlockSpec((s, k), lambda j: (0, 0)), pl.BlockSpec((s, nb), lambda j: (0, j))],
        out_specs=out_specs, out_shape=out_shape,
        compiler_params=_cparams(1))(a, dy)
    return res if with_colsum else res[0]


def wgrad_down(t, dy, name):
    _, s, k = t.shape
    n = dy.shape[1]

    def body(a_ref, dy_ref, o_ref):
        o_ref[...] = _tn(a_ref[...], dy_ref[...]).astype(BF16)

    return pl.pallas_call(
        body, name=name, grid=(4,),
        in_specs=[pl.BlockSpec((None, s, k), lambda j: (j, 0, 0)), pl.BlockSpec((s, n), lambda j: (0, 0))],
        out_specs=pl.BlockSpec((None, k, n), lambda j: (j, 0, 0)),
        out_shape=jax.ShapeDtypeStruct((4, k, n), BF16),
        compiler_params=_cparams(1))(t, dy)


def wgrad_up(a, dhv, dhg, name):
    s, k = a.shape

    def body(a_ref, dv_ref, dg_ref, o_ref):
        j = pl.program_id(0)

        @pl.when(j < 4)
        def _():
            o_ref[...] = _tn(dv_ref[...], a_ref[...]).astype(BF16)

        @pl.when(j >= 4)
        def _():
            o_ref[...] = _tn(dg_ref[...], a_ref[...]).astype(BF16)

    return pl.pallas_call(
        body, name=name, grid=(N_DEV,),
        in_specs=[pl.BlockSpec((s, k), lambda j: (0, 0)),
                  pl.BlockSpec((None, s, FF_BLK), lambda j: (jnp.minimum(j, 3), 0, 0)),
                  pl.BlockSpec((None, s, FF_BLK), lambda j: (jnp.maximum(j - 4, 0), 0, 0))],
        out_specs=pl.BlockSpec((None, FF_BLK, k), lambda j: (j, 0, 0)),
        out_shape=jax.ShapeDtypeStruct((N_DEV, FF_BLK, k), BF16),
        compiler_params=_cparams(1))(a, dhv, dhg)


def ffn_act_bwd(dzb, wdown, up, conv_w, conv_b, li, name):
    s = up.shape[1]
    tm = 512
    nt = s // tm
    before, after = _halo_maps(tm, s)
    hv_main, hv_prev, hv_next = _slab_specs(tm, s, lambda c: c)
    hg_main, hg_prev, hg_next = _slab_specs(tm, s, lambda c: 4 + c)

    def dc_of(dz, wd, hv, hg, back, fwd, cw_ref, cb_ref):
        dt = _nt(dz, wd)
        c = back * cw_ref[0:1, :] + hg * cw_ref[1:2, :] + fwd * cw_ref[2:3, :] + cb_ref[...]
        cdf = 0.5 * (1.0 + lax.erf(c * _SQRT_HALF))
        pdf = jnp.exp(-0.5 * c * c) * _INV_SQRT_2PI
        return dt, c * cdf, dt * hv * (cdf + c * pdf)

    def body(dz_ref, dzp_ref, dzn_ref, wd_ref, hv_ref, hvp_ref, hvn_ref, hg_ref, hgp_ref, hgn_ref, cw_ref, cb_ref,
             dhv_ref, dhg_ref, dcw_ref, dcb_ref):
        i = pl.program_id(1)

        @pl.when(i == 0)
        def _():
            dcw_ref[...] = jnp.zeros_like(dcw_ref)
            dcb_ref[...] = jnp.zeros_like(dcb_ref)

        wd = _row_cat(wd_ref)
        hg = hg_ref[...].astype(F32)
        hgp = hgp_ref[...].astype(F32)
        hgn = hgn_ref[...].astype(F32)
        first, last = i == 0, i == nt - 1
        e = HALO - 1
        back, fwd = _shift_rows(hg, jnp.where(first, 0.0, hgp[e:e + 1, :]), jnp.where(last, 0.0, hgn[0:1, :]))
        dt, act, dc = dc_of(dz_ref[...], wd, hv_ref[...].astype(F32), hg, back, fwd, cw_ref, cb_ref)
        dhv_ref[...] = (dt * act).astype(BF16)
        bp, fp = _shift_rows(hgp, hgp[0:1, :], hg[0:1, :])
        _, _, dcp = dc_of(dzp_ref[...], wd, hvp_ref[...].astype(F32), hgp, bp, fp, cw_ref, cb_ref)
        bn, fn = _shift_rows(hgn, hg[tm - 1:tm, :], hgn[e:e + 1, :])
        _, _, dcn = dc_of(dzn_ref[...], wd, hvn_ref[...].astype(F32), hgn, bn, fn, cw_ref, cb_ref)
        dc_back, dc_fwd = _shift_rows(dc, jnp.where(first, 0.0, dcp[e:e + 1, :]), jnp.where(last, 0.0, dcn[0:1, :]))
        dhg_ref[...] = (dc_fwd * cw_ref[0:1, :] + dc * cw_ref[1:2, :] + dc_back * cw_ref[2:3, :]).astype(BF16)
        dcw_ref[0:1, :] += _colsum(dc * back)
        dcw_ref[1:2, :] += _colsum(dc * hg)
        dcw_ref[2:3, :] += _colsum(dc * fwd)
        dcb_ref[...] += _colsum(dc)

    out_slab = pl.BlockSpec((None, tm, FF_BLK), lambda c, i: (c, i, 0))
    cw_spec = pl.BlockSpec((None, 3, FF_BLK), lambda c, i: (c, 0, 0))
    cb_spec = pl.BlockSpec((None, 1, FF_BLK), lambda c, i: (c, 0, 0))
    return pl.pallas_call(
        body, name=name, grid=(4, nt),
        in_specs=[pl.BlockSpec((tm, D), lambda c, i: (i, 0)),
                  pl.BlockSpec((HALO, D), lambda c, i: (before(i), 0)),
                  pl.BlockSpec((HALO, D), lambda c, i: (after(i), 0)),
                  _shards(2, FF_SHARD, D, li, lambda c, i: c),
                  hv_main, hv_prev, hv_next, hg_main, hg_prev, hg_next, cw_spec, cb_spec],
        out_specs=[out_slab, out_slab, cw_spec, cb_spec],
        out_shape=[jax.ShapeDtypeStruct((4, s, FF_BLK), BF16), jax.ShapeDtypeStruct((4, s, FF_BLK), BF16),
                   jax.ShapeDtypeStruct((4, 3, FF_BLK), F32), jax.ShapeDtypeStruct((4, 1, FF_BLK), F32)],
        compiler_params=_cparams(2))(dzb, dzb, dzb, wdown, up, up, up, up, up, up, conv_w, conv_b)


def dh1_ln1_bwd(dz2, dpg, wpg, dhv, dhg, wup, z1, g1, li, name, after=()):
    s = dz2.shape[0]
    tm = 256
    na = len(after)

    def body(dz2_ref, dpg_ref, wpg_ref, dhv_ref, dhg_ref, wup_ref, z1_ref, g_ref, *rest):
        dz_ref, dzb_ref, dg_ref, db_ref = rest[na:]

        @pl.when(pl.program_id(0) == 0)
        def _():
            dg_ref[...] = jnp.zeros_like(dg_ref)
            db_ref[...] = jnp.zeros_like(db_ref)
        dh = ALPHA * dz2_ref[...] + _nt(dpg_ref[...], _row_cat(wpg_ref))
        for c in range(4):
            dh = dh + _nn(dhv_ref[c], wup_ref[c]) + _nn(dhg_ref[c], wup_ref[4 + c])
        dz, dgx = _ln_bwd(dh, z1_ref[...], g_ref[...])
        dz_ref[...] = dz
        dzb_ref[...] = dz.astype(BF16)
        dg_ref[...] += _colsum(dgx)
        db_ref[...] += _colsum(dh)

    row = pl.BlockSpec((tm, D), lambda i: (i, 0))
    vec = pl.BlockSpec((1, D), lambda i: (0, 0))
    slab = pl.BlockSpec((4, tm, FF_BLK), lambda i: (0, i, 0))
    return pl.pallas_call(
        body, name=name, grid=(s // tm,),
        in_specs=[row, row, _shards(N_DEV, D // N_DEV, D, li), slab, slab, _shards(N_DEV, FF_BLK, D, li), row, vec]
        + [ANY] * na,
        out_specs=[row, row, vec, vec],
        out_shape=[jax.ShapeDtypeStruct((s, D), F32), jax.ShapeDtypeStruct((s, D), BF16),
                   jax.ShapeDtypeStruct((1, D), F32), jax.ShapeDtypeStruct((1, D), F32)],
        compiler_params=_cparams(1))(dz2, dpg, wpg, dhv, dhg, wup, z1, g1, *after)


def merge_bwd(dz1b, wmix, proj, ya, yp, li, name):
    s = dz1b.shape[0]
    tm, tn = 512, 512
    nt = D // tn
    per = tn // (D // N_DEV)
    ga0 = (3 * D_ATTN + D_POOL) // tn

    def body(dz_ref, w_ref, ga_ref, gb_ref, ya_ref, yp_ref, dya_ref, dyp_ref, dga_ref, dgb_ref):
        dm = _nt(dz_ref[...], _row_cat(w_ref))
        sa = _sigmoid(ga_ref[...].astype(F32))
        sb = _sigmoid(gb_ref[...].astype(F32))
        dya_ref[...] = (dm * sa).astype(BF16)
        dyp_ref[...] = (dm * sb).astype(BF16)
        dga_ref[...] = (dm * ya_ref[...].astype(F32) * sa * (1.0 - sa)).astype(BF16)
        dgb_ref[...] = (dm * yp_ref[...].astype(F32) * sb * (1.0 - sb)).astype(BF16)

    tile = pl.BlockSpec((tm, tn), lambda i, j: (i, j))
    return pl.pallas_call(
        body, name=name, grid=(s // tm, nt),
        in_specs=[pl.BlockSpec((tm, D), lambda i, j: (i, 0)),
                  _shards(per, D // N_DEV, D, li, lambda i, j: j),
                  pl.BlockSpec((tm, tn), lambda i, j: (i, ga0 + j)),
                  pl.BlockSpec((tm, tn), lambda i, j: (i, ga0 + nt + j)),
                  tile, tile],
        out_specs=[tile] * 4,
        out_shape=[jax.ShapeDtypeStruct((s, D), BF16)] * 4,
        compiler_params=_cparams(2))(dz1b, wmix, proj, proj, ya, yp)


def attn_out_bwd(dya, wao, li, name, after=()):
    s = dya.shape[0]
    tm = 512

    def body(d_ref, w_ref, *rest):
        rest[-1][...] = _nt(d_ref[...], _lane_cat(w_ref)).astype(BF16)

    return pl.pallas_call(
        body, name=name, grid=(s // tm,),
        in_specs=[pl.BlockSpec((tm, D), lambda i: (i, 0)), _shards(N_DEV, D_ATTN, 128, li)] + [ANY] * len(after),
        out_specs=pl.BlockSpec((tm, D_ATTN), lambda i: (i, 0)),
        out_shape=jax.ShapeDtypeStruct((s, D_ATTN), BF16),
        compiler_params=_cparams(1))(dya, wao, *after)


def pool_bwd(dyp, wpo, pm, pool_w, pool_scale, li, name):
    s = dyp.shape[0]

    def body(dyp_ref, wpo_ref, pm_ref, w_ref, sc_ref, du_ref, dw_ref, dsc_ref):
        wpo = _lane_cat(wpo_ref)
        dyp = dyp_ref[...]
        for g, w in enumerate(POOL_WINDOWS):
            cols = slice(g * PGD, (g + 1) * PGD)
            dpw = _nt(dyp, wpo[g * PGD:(g + 1) * PGD, :])
            pmg = pm_ref[:, cols]
            dsc_ref[:, cols] = _colsum(dpw * _nn(pmg, w_ref[g]))
            dpmw = (dpw * sc_ref[:, cols]).astype(BF16)
            dw_ref[g] = _tn(pmg, dpmw)
            dpm = _nt(dpmw, w_ref[g])
            du_ref[:, cols] = (_window_sum(dpm / _pool_counts(s, w), w, False) - dpm).astype(BF16)

    full = lambda shape: pl.BlockSpec(shape, lambda i: (0,) * len(shape))
    return pl.pallas_call(
        body, name=name, grid=(1,),
        in_specs=[full((s, D)), _shards(N_DEV, D_POOL, 128, li), full((s, D_POOL)), full((4, PGD, PGD)),
                  full((1, D_POOL))],
        out_specs=[full((s, D_POOL)), full((4, PGD, PGD)), full((1, D_POOL))],
        out_shape=[jax.ShapeDtypeStruct((s, D_POOL), BF16), jax.ShapeDtypeStruct((4, PGD, PGD), F32),
                   jax.ShapeDtypeStruct((1, D_POOL), F32)],
        compiler_params=_cparams(1))(dyp, wpo, pm, pool_w, pool_scale)


def attn_bwd(proj, da, e_rev, name, after=()):
    s = proj.shape[0]
    nb = s // QB
    skew = GRID_W + (GRID_W - KW)

    def body(q_ref, k_ref, v_ref, do_ref, e_ref, *rest):
        dq_ref, dk_ref, dv_ref, g_ref, s_ref, dp_ref, ds_ref, p_ref, dkt_acc, dvt_acc = rest[len(after):]
        b = pl.program_id(1)

        @pl.when(b == 0)
        def _():
            dkt_acc[...] = jnp.zeros_like(dkt_acc)
            dvt_acc[...] = jnp.zeros_like(dvt_acc)
            g_ref[...] = jnp.zeros_like(g_ref)

        ri = lax.broadcasted_iota(I32, (QB, QB), 0)
        ci = lax.broadcasted_iota(I32, (QB, QB), 1)
        rev = jnp.where(ri + ci == QB - 1, 1.0, 0.0).astype(BF16)
        q = _nn(rev, q_ref[...]).astype(BF16) * ATT_SCALE
        do = _nn(rev, do_ref[...]).astype(BF16)
        lane = lax.broadcasted_iota(I32, (1, 128), 1)

        def block(btype, k0):
            kwin = k_ref[pl.ds(k0, KB), :]
            vwin = v_ref[pl.ds(k0, KB), :]
            dq = jnp.zeros((QB, 128), F32)
            for hh in range(2):
                lm = (lane // HEAD_DIM) == hh
                qh = jnp.where(lm, q, jnp.zeros_like(q))
                doh = jnp.where(lm, do, jnp.zeros_like(do))
                kh = jnp.where(lm, kwin, jnp.zeros_like(kwin))
                s_ref[...] = _nt(qh, kwin)
                dp_ref[...] = _nt(doh, vwin)
                g = jnp.zeros((1, KB), F32)
                for ib in range(QROWS):
                    qr = QROWS - 1 - ib
                    rows = slice(ib * GRID_W, (ib + 1) * GRID_W)
                    sb, a0, w, pad = _row_logits(s_ref, e_ref, hh, rows, btype, qr)
                    p = jnp.exp(sb - jnp.max(sb, axis=1, keepdims=True))
                    p = p * (1.0 / jnp.sum(p, axis=1, keepdims=True))
                    dp = dp_ref[rows, a0:a0 + w]
                    ds = p * (dp - jnp.sum(p * dp, axis=1, keepdims=True))
                    _store_row(ds_ref, rows, a0, w, ds)
                    _store_row(p_ref, rows, a0, w, p)
                    t = jnp.sum(pltpu.roll(ds, w - skew, 1, stride=1, stride_axis=0), axis=0, keepdims=True)
                    t = t[:, :KH * GRID_W] if pad else pltpu.roll(t, GRID_W, 1)
                    i0 = _attn_row(btype, qr)[2]
                    g = g + pltpu.roll(jnp.concatenate([t, jnp.zeros_like(t)], axis=1), i0 * GRID_W, 1)
                g_ref[hh] += g
                dsb = ds_ref[...]
                dq = dq + _nn(dsb, kh) * ATT_SCALE
                dkt_acc[:, pl.ds(k0, KB)] += _tn(qh, dsb)
                dvt_acc[:, pl.ds(k0, KB)] += _tn(doh, p_ref[...])
            dq_ref[...] = _nn(rev, dq.astype(BF16)).astype(BF16)

        for btype, (cond, k0) in enumerate(_attn_types(b, nb)):
            pl.when(cond)(lambda btype=btype, k0=k0: block(btype, k0))

        @pl.when(b == nb - 1)
        def _():
            dk_ref[...] = dkt_acc[...].T.astype(BF16)
            dv_ref[...] = dvt_acc[...].T.astype(BF16)

    col = pl.BlockSpec((s, 128), lambda j, b: (0, j))
    return pl.pallas_call(
        body, name=name, grid=(4, nb),
        in_specs=[pl.BlockSpec((QB, 128), lambda j, b: (b, j)),
                  pl.BlockSpec((s, 128), lambda j, b: (0, 4 + j)),
                  pl.BlockSpec((s, 128), lambda j, b: (0, 8 + j)),
                  pl.BlockSpec((QB, 128), lambda j, b: (b, j)),
                  pl.BlockSpec((2, 2, GRID_W, KB), lambda j, b: (j, 0, 0, 0))] + [ANY] * len(after),
        out_specs=[pl.BlockSpec((QB, 128), lambda j, b: (b, j)), col, col,
                   pl.BlockSpec((2, 1, KB), lambda j, b: (j, 0, 0))],
        out_shape=[jax.ShapeDtypeStruct((s, D_ATTN), BF16)] * 3 + [jax.ShapeDtypeStruct((N_HEADS, 1, KB), F32)],
        scratch_shapes=[pltpu.VMEM((QB, KB), F32), pltpu.VMEM((QB, KB), F32), pltpu.VMEM((QB, KB), BF16),
                        pltpu.VMEM((QB, KB), BF16), pltpu.VMEM((128, s), F32), pltpu.VMEM((128, s), F32)],
        compiler_params=_cparams(2))(proj, proj, proj, da, e_rev, *after)


def dh0_bwd(dz1, dproj, win, li, name, after=()):
    s = dz1.shape[0]
    tm = 256
    bn = N_PROJ // N_DEV

    def body(dz_ref, dp_ref, w_ref, *rest):
        acc = ALPHA * dz_ref[...]
        for j in range(N_DEV):
            acc = acc + _nt(dp_ref[:, j * bn:(j + 1) * bn], w_ref[j])
        rest[-1][...] = acc

    row = pl.BlockSpec((tm, D), lambda i: (i, 0))
    return pl.pallas_call(
        body, name=name, grid=(s // tm,),
        in_specs=[row, pl.BlockSpec((tm, N_PROJ), lambda i: (i, 0)), _shards(N_DEV, D, bn, li)] + [ANY] * len(after),
        out_specs=row, out_shape=jax.ShapeDtypeStruct((s, D), F32),
        compiler_params=_cparams(1))(dz1, dproj, win, *after)


def _coords():
    return lax.axis_index("x"), lax.axis_index("y"), lax.axis_index("c")


def _dev_index(px, py, pc):
    return 4 * px + 2 * py + pc


def all_gather(arrs, name):
    n = len(arrs)

    def body(*refs):
        ins, outs = refs[:n], refs[n:2 * n]
        send_sems, recv_sems, local_sems = refs[2 * n:]
        x, y, c = _coords()
        me, sibling = (x, y, c), (x, y, 1 - c)
        chips = [(1 - x, y), (x, 1 - y), (1 - x, 1 - y)]

        def copy(a, k, block, to, src=None):
            dst = outs[a].at[_dev_index(*block)]
            return pltpu.make_async_remote_copy(
                src_ref=dst if src is None else src, dst_ref=dst,
                send_sem=send_sems.at[a, k], recv_sem=recv_sems.at[a, k], device_id=to, device_id_type=MESH)

        mine = [pltpu.make_async_copy(ins[a], outs[a].at[_dev_index(*me)], local_sems.at[a]) for a in range(n)]
        for cp in mine:
            cp.start()
        first = []
        for a in range(n):
            first.append(copy(a, 0, me, sibling, src=ins[a]))
            first += [copy(a, 1 + j, me, (*chip, c), src=ins[a]) for j, chip in enumerate(chips)]
        for cp in first:
            cp.start()
        passed = []
        for j, chip in enumerate(chips):
            for a in range(n):
                copy(a, 1 + j, (*chip, c), me).wait_recv()
                cp = copy(a, 4 + j, (*chip, c), sibling)
                cp.start()
                passed.append(cp)
        for a in range(n):
            copy(a, 0, sibling, me).wait_recv()
            for j, chip in enumerate(chips):
                copy(a, 4 + j, (*chip, 1 - c), me).wait_recv()
        for cp in first + passed:
            cp.wait_send()
        for cp in mine:
            cp.wait()

    return pl.pallas_call(
        body, name=name,
        out_shape=[jax.ShapeDtypeStruct((N_DEV,) + a.shape, a.dtype) for a in arrs],
        in_specs=[ANY] * n, out_specs=[ANY] * n,
        scratch_shapes=[pltpu.SemaphoreType.DMA((n, 7)), pltpu.SemaphoreType.DMA((n, 7)),
                        pltpu.SemaphoreType.DMA((n,))],
    )(*arrs)


HBM = pl.BlockSpec(memory_space=pltpu.HBM)
SEM = pl.BlockSpec(memory_space=pltpu.SEMAPHORE)
_EFFECT = pltpu.SideEffectType.DATAFLOW_SIDE_EFFECTING
_TOKEN = jax.ShapeDtypeStruct((8, 128), F32)


def _in_hbm(a):
    return pltpu.with_memory_space_constraint(a, pltpu.HBM)


def _hbm_like(a):
    return pltpu.HBM(a.shape, a.dtype)


def _peers(x, y, c):
    return [(x, y, 1 - c), (1 - x, y, c), (x, 1 - y, c), (1 - x, 1 - y, c)]


def ag_start(lands, after, name):
    n = len(lands)

    def body(*refs):
        land = refs[:n]
        send_sem, recv_sem, token = refs[n + 1], refs[n + 2], refs[-1]
        x, y, c = _coords()
        me = _dev_index(x, y, c)
        for k, peer in enumerate(_peers(x, y, c)):
            for a in range(n):
                pltpu.make_async_remote_copy(src_ref=land[a].at[me], dst_ref=land[a].at[me], send_sem=send_sem.at[k],
                                             recv_sem=recv_sem.at[k], device_id=peer, device_id_type=MESH).start()
        token[...] = jnp.zeros_like(token)

    res = pl.pallas_call(
        body, name=name,
        out_shape=(pltpu.SemaphoreType.DMA((4,)), pltpu.SemaphoreType.DMA((4,)), *[_hbm_like(l) for l in lands], _TOKEN),
        in_specs=[HBM] * n + [ANY], out_specs=(SEM, SEM, *[HBM] * n, pl.BlockSpec(memory_space=pltpu.VMEM)),
        input_output_aliases={a: 2 + a for a in range(n)},
        compiler_params=pltpu.CompilerParams(has_side_effects=_EFFECT),
    )(*[_in_hbm(l) for l in lands], after)
    return res[0], res[1], list(res[2:2 + n]), res[-1]


def ag_forward(send_sem, recv_sem, lands, after, name):
    n = len(lands)

    def body(*refs):
        send_sem, recv_sem = refs[0], refs[1]
        land = refs[2:2 + n]
        fsend, frecv = refs[3 + n], refs[4 + n]
        x, y, c = _coords()
        peers = _peers(x, y, c)
        for k in range(1, 4):
            blk = _dev_index(*peers[k])
            for a in range(n):
                pltpu.make_async_remote_copy(src_ref=land[a].at[blk], dst_ref=land[a].at[blk], send_sem=send_sem.at[k],
                                             recv_sem=recv_sem.at[k], device_id=peers[k], device_id_type=MESH).wait_recv()
        for k in range(1, 4):
            blk = _dev_index(*peers[k])
            for a in range(n):
                pltpu.make_async_remote_copy(src_ref=land[a].at[blk], dst_ref=land[a].at[blk], send_sem=fsend.at[k - 1],
                                             recv_sem=frecv.at[k - 1], device_id=peers[0], device_id_type=MESH).start()

    res = pl.pallas_call(
        body, name=name,
        out_shape=(pltpu.SemaphoreType.DMA((3,)), pltpu.SemaphoreType.DMA((3,)), *[_hbm_like(l) for l in lands]),
        in_specs=[SEM, SEM, *[HBM] * n, ANY], out_specs=(SEM, SEM, *[HBM] * n),
        input_output_aliases={2 + a: 2 + a for a in range(n)},
        compiler_params=pltpu.CompilerParams(has_side_effects=_EFFECT),
    )(send_sem, recv_sem, *lands, after)
    return res[0], res[1], list(res[2:])


def ag_finish(send_sem, recv_sem, fsend, frecv, lands, after, name):
    n = len(lands)

    def body(*refs):
        send_sem, recv_sem, fsend, frecv = refs[:4]
        land = refs[4:4 + n]
        x, y, c = _coords()
        me = _dev_index(x, y, c)
        peers = _peers(x, y, c)
        for k in range(4):
            for a in range(n):
                pltpu.make_async_remote_copy(src_ref=land[a].at[me], dst_ref=land[a].at[me], send_sem=send_sem.at[k],
                                             recv_sem=recv_sem.at[k], device_id=peers[k], device_id_type=MESH).wait_send()
        sib = _dev_index(*peers[0])
        for a in range(n):
            pltpu.make_async_remote_copy(src_ref=land[a].at[sib], dst_ref=land[a].at[sib], send_sem=send_sem.at[0],
                                         recv_sem=recv_sem.at[0], device_id=peers[0], device_id_type=MESH).wait_recv()
        for k in range(1, 4):
            mine = _dev_index(*peers[k])
            theirs = _dev_index(peers[k][0], peers[k][1], 1 - c)
            for a in range(n):
                pltpu.make_async_remote_copy(src_ref=land[a].at[mine], dst_ref=land[a].at[theirs], send_sem=fsend.at[k - 1],
                                             recv_sem=frecv.at[k - 1], device_id=peers[0], device_id_type=MESH).wait()

    res = pl.pallas_call(
        body, name=name, out_shape=tuple(_hbm_like(l) for l in lands),
        in_specs=[SEM] * 4 + [HBM] * n + [ANY], out_specs=tuple([HBM] * n),
        input_output_aliases={4 + a: a for a in range(n)},
        compiler_params=pltpu.CompilerParams(has_side_effects=_EFFECT),
    )(send_sem, recv_sem, fsend, frecv, *lands, after)
    return list(res)


def _others(x, y, c):
    return [(x ^ (k & 1), y ^ ((k >> 1) & 1), c ^ (k >> 2)) for k in range(1, N_DEV)]


def rs_start(parts, name):
    n = len(parts)
    lands = [lax.empty((N_DEV - 1,) + p.shape[1:], p.dtype) for p in parts]

    def body(*refs):
        src, land = refs[:n], refs[n:2 * n]
        send_sem, recv_sem, token = refs[2 * n], refs[2 * n + 1], refs[-1]
        for k, peer in enumerate(_others(*_coords())):
            for a in range(n):
                pltpu.make_async_remote_copy(src_ref=src[a].at[_dev_index(*peer)], dst_ref=land[a].at[k],
                                             send_sem=send_sem.at[k], recv_sem=recv_sem.at[k], device_id=peer,
                                             device_id_type=MESH).start()
        token[...] = jnp.zeros_like(token)

    res = pl.pallas_call(
        body, name=name,
        out_shape=(pltpu.SemaphoreType.DMA((N_DEV - 1,)), pltpu.SemaphoreType.DMA((N_DEV - 1,)),
                   *[_hbm_like(p) for p in parts], *[_hbm_like(l) for l in lands], _TOKEN),
        in_specs=[HBM] * (2 * n), out_specs=(SEM, SEM, *[HBM] * (2 * n), pl.BlockSpec(memory_space=pltpu.VMEM)),
        input_output_aliases={a: 2 + a for a in range(2 * n)},
        compiler_params=pltpu.CompilerParams(has_side_effects=_EFFECT),
    )(*[_in_hbm(p) for p in parts], *[_in_hbm(l) for l in lands])
    return res[0], res[1], list(res[2:2 + n]), list(res[2 + n:2 + 2 * n]), res[-1]


def rs_finish(send_sem, recv_sem, parts, lands, after, name):
    n = len(parts)

    def body(*refs):
        send_sem, recv_sem = refs[0], refs[1]
        src, land = refs[2:2 + n], refs[2 + n:2 + 2 * n]
        for k, peer in enumerate(_others(*_coords())):
            for a in range(n):
                pltpu.make_async_remote_copy(src_ref=src[a].at[_dev_index(*peer)], dst_ref=land[a].at[k],
                                             send_sem=send_sem.at[k], recv_sem=recv_sem.at[k], device_id=peer,
                                             device_id_type=MESH).wait()

    res = pl.pallas_call(
        body, name=name, out_shape=tuple(_hbm_like(t) for t in list(parts) + list(lands)),
        in_specs=[SEM, SEM] + [HBM] * (2 * n) + [ANY], out_specs=tuple([HBM] * (2 * n)),
        input_output_aliases={2 + a: a for a in range(2 * n)},
        compiler_params=pltpu.CompilerParams(has_side_effects=_EFFECT),
    )(send_sem, recv_sem, *parts, *lands, after)
    return list(res[:n]), list(res[n:])


def _row_tile(r):
    return next(t for t in (512, 352, 256, 128) if r % t == 0)


def _adamw(w, g, m, v):
    m = ADAM_B1 * m + (1.0 - ADAM_B1) * g
    v = ADAM_B2 * v + (1.0 - ADAM_B2) * (g * g)
    m_hat = m / (1.0 - ADAM_B1 ** ADAM_STEP)
    v_hat = v / (1.0 - ADAM_B2 ** ADAM_STEP)
    delta = -ADAM_LR * (m_hat / (jnp.sqrt(v_hat) + ADAM_EPS) + ADAM_WD * w)
    return delta, m, v


def adamw_shard(me, part, recv, w, m, v, li, prev, name):
    _, r, c = part.shape
    tr = _row_tile(r)

    def body(me_ref, own_ref, recv_ref, w_ref, m_ref, v_ref, p0, p1, p2, p3, g_ref, d_ref, nm_ref, nv_ref):
        g = own_ref[...].astype(F32)
        for k in range(N_DEV - 1):
            g = g + recv_ref[k].astype(F32)
        delta, nm, nv = _adamw(w_ref[...], g, m_ref[...], v_ref[...])
        g_ref[...] = g
        d_ref[...] = delta
        nm_ref[...] = nm
        nv_ref[...] = nv

    lay = pl.BlockSpec((None, tr, c), lambda t, me: (li, t, 0))
    stack = jax.ShapeDtypeStruct((DEPTH, r, c), F32)
    grid_spec = pltpu.PrefetchScalarGridSpec(
        num_scalar_prefetch=1, grid=(r // tr,),
        in_specs=[pl.BlockSpec((None, tr, c), lambda t, me: (me[0], t, 0)),
                  pl.BlockSpec((N_DEV - 1, tr, c), lambda t, me: (0, t, 0)), lay, lay, lay, ANY, ANY, ANY, ANY],
        out_specs=[lay] * 4)
    return pl.pallas_call(
        body, name=name, grid_spec=grid_spec, out_shape=[stack] * 4,
        input_output_aliases={6: 0, 7: 1, 8: 2, 9: 3},
        compiler_params=_cparams(1))(me, part, recv, w, m, v, *prev)


def adamw_replicated(gathered, w, m, v, name):
    _, r, c = gathered.shape
    tr = _PACK_SECTION

    def body(gs_ref, w_ref, m_ref, v_ref, g_ref, d_ref, nm_ref, nv_ref):
        g = gs_ref[0]
        for d in range(1, N_DEV):
            g = g + gs_ref[d]
        delta, nm, nv = _adamw(w_ref[...], g, m_ref[...], v_ref[...])
        g_ref[...] = g
        d_ref[...] = delta
        nm_ref[...] = nm
        nv_ref[...] = nv

    row = pl.BlockSpec((tr, c), lambda t: (t, 0))
    return pl.pallas_call(
        body, name=name, grid=(r // tr,),
        in_specs=[pl.BlockSpec((N_DEV, tr, c), lambda t: (0, t, 0)), row, row, row],
        out_specs=[row] * 4, out_shape=[jax.ShapeDtypeStruct((r, c), F32)] * 4,
        compiler_params=_cparams(1))(gathered, w, m, v)


def adamw_plain(g, w, m, v, name):
    def body(g_ref, w_ref, m_ref, v_ref, d_ref, nm_ref, nv_ref):
        delta, nm, nv = _adamw(w_ref[...], g_ref[...], m_ref[...], v_ref[...])
        d_ref[...] = delta
        nm_ref[...] = nm
        nv_ref[...] = nv

    return pl.pallas_call(body, name=name, out_shape=[jax.ShapeDtypeStruct(w.shape, F32)] * 3)(g, w, m, v)


_PACK_LAYER = (("b_in", (N_PROJ,)), ("rpb", (N_HEADS, 2 * KH - 1, 2 * KW - 1)), ("pool_w", (4, PGD, PGD)),
               ("pool_scale", (D_POOL,)), ("ln1_g", (D,)), ("ln1_b", (D,)), ("conv_b", (D_FF,)), ("ln2_g", (D,)),
               ("ln2_b", (D,)), ("conv_w", (3, D_FF)))
_PACK_INPUT = (("ln_in_g", (D,)), ("ln_in_b", (D,)))
_PACK_LANES = 1024
_PACK_SECTION = 96
_EARLY = tuple(range(DEPTH - 1, 0, -1))


def _pack_rows(shape):
    return -(-int(np.prod(shape)) // _PACK_LANES)


def _pack(parts, layers):
    rows = []
    for li in layers:
        items = [(parts[n][li], s) for n, s in _PACK_LAYER]
        if li == 0:
            items += [(parts[n], s) for n, s in _PACK_INPUT]
        used = 0
        for arr, shape in items:
            flat = arr.reshape(-1).astype(F32)
            rows.append(jnp.pad(flat, (0, _pack_rows(shape) * _PACK_LANES - flat.shape[0])))
            used += _pack_rows(shape)
        rows.append(jnp.zeros(((_PACK_SECTION - used) * _PACK_LANES,), F32))
    return jnp.concatenate(rows).reshape(_PACK_SECTION * len(layers), _PACK_LANES)


def _unpack(packed, layers, out):
    for i, li in enumerate(layers):
        r0 = i * _PACK_SECTION
        for name, shape in _PACK_LAYER + (_PACK_INPUT if li == 0 else ()):
            n, nr = int(np.prod(shape)), _pack_rows(shape)
            val = packed[r0:r0 + nr].reshape(-1)[:n].reshape(shape)
            if (name, shape) in _PACK_INPUT:
                out[name] = val
            else:
                out.setdefault(name, {})[li] = val
            r0 += nr
    return out


def _bias_tables(rpb_l):
    qc = np.arange(GRID_W)[:, None]
    kc = np.arange(GRID_W)[None, :]
    start = np.clip(qc - KW // 2, 0, GRID_W - KW)
    valid = (kc >= start) & (kc < start + KW)
    col = np.clip(kc - qc, -(KW - 1), KW - 1) + KW - 1
    onehot = (col.reshape(-1)[None, :] == np.arange(2 * KW - 1)[:, None]).astype(np.float32)
    rows = jnp.pad(rpb_l, ((0, 0), (0, 1), (0, 0)))
    tab = jnp.einsum("hij,jm->him", rows, jnp.asarray(onehot), precision=lax.Precision.HIGHEST)
    tab = tab.reshape(N_HEADS, KROWS, GRID_W, GRID_W).transpose(0, 2, 1, 3)
    ok = valid[None, :, None, :] & (np.arange(KROWS) < 2 * KH - 1)[None, None, :, None]
    tab = jnp.where(jnp.asarray(ok), tab, NEG_INF).reshape(N_HEADS, GRID_W, KB)
    tab = jnp.stack([tab, jnp.roll(tab, GRID_W, axis=-1)], axis=1)
    return tab, tab[:, :, ::-1, :]


_SHARDED = ("w_in", "w_attn_out", "w_pool_out", "w_mix_out", "w_up", "w_down", "w_ple_gate", "w_ple_proj")
_NAMES = ("ln_in_g", "ln_in_b", "w_in", "b_in", "rpb", "w_attn_out", "pool_w", "pool_scale", "w_pool_out", "w_mix_out",
          "ln1_g", "ln1_b", "w_up", "conv_w", "conv_b", "w_down", "w_ple_gate", "w_ple_proj", "ln2_g", "ln2_b")


def kernel(x, p, ln_in_g, ln_in_b, w_in, b_in, rpb, w_attn_out, pool_w, pool_scale, w_pool_out, w_mix_out, ln1_g, ln1_b, w_up, conv_w, conv_b, w_down, w_ple_gate, w_ple_proj, ln2_g, ln2_b, loss_target, m_ln_in_g, m_ln_in_b, m_w_in, m_b_in, m_rpb, m_w_attn_out, m_pool_w, m_pool_scale, m_w_pool_out, m_w_mix_out, m_ln1_g, m_ln1_b, m_w_up, m_conv_w, m_conv_b, m_w_down, m_w_ple_gate, m_w_ple_proj, m_ln2_g, m_ln2_b, v_ln_in_g, v_ln_in_b, v_w_in, v_b_in, v_rpb, v_w_attn_out, v_pool_w, v_pool_scale, v_w_pool_out, v_w_mix_out, v_ln1_g, v_ln1_b, v_w_up, v_conv_w, v_conv_b, v_w_down, v_w_ple_gate, v_w_ple_proj, v_ln2_g, v_ln2_b):
    a = dict(locals())
    W = {n: a[n] for n in _NAMES}
    M = {n: a["m_" + n] for n in _NAMES}
    V = {n: a["v_" + n] for n in _NAMES}
    xi, yi, ci = _coords()
    me = _dev_index(xi, yi, ci)
    x2, tgt = x[0], loss_target[0]
    pb = p[:, 0].astype(BF16)

    flip = lambda d: {**d, "w_up": d["w_up"].transpose(0, 2, 1)}
    ex = _Exchange(flip(W), flip(M), flip(V))
    loss_part, dx, parts = _local_step(x2, tgt, pb, W, ex)
    loss = lax.psum(loss_part[0, 0], AXES)
    stacks = {**ex.stacks, "w_up": [t.transpose(0, 2, 1) for t in ex.stacks["w_up"]]}

    zero_cw = jnp.zeros((DEPTH, 3, D_FF), F32)
    outs = [{}, {}, {}, {}]
    (late,) = all_gather([_pack(parts, (0,))], "ag_small_grads")
    for layers, gath, name in ((_EARLY, ex.replicated_finish(late), "adamw_replicated_early"),
                               ((0,), late, "adamw_replicated")):
        packs = [_pack({**src, "conv_w": zero_cw}, layers) for src in (W, M, V)]
        for out, packed in zip(outs, adamw_replicated(gath, *packs, name)):
            _unpack(packed, layers, out)
    outs = [{n: (jnp.stack([v_[li] for li in range(DEPTH)]) if isinstance(v_, dict) else v_) for n, v_ in o.items()}
            for o in outs]
    g_cw = lax.dynamic_slice_in_dim(outs[0]["conv_w"], me * FF_SHARD, FF_SHARD, axis=2)
    flat = lambda t: t.reshape(DEPTH * 3, FF_SHARD)
    cw_out = [o.reshape(DEPTH, 3, FF_SHARD) for o in
              adamw_plain(flat(g_cw), flat(conv_w), flat(m_conv_w), flat(v_conv_w), "adamw_conv_w")]
    res = []
    for k in range(4):
        d = {n: stacks[n][k] for n in _SHARDED}
        d.update({n: outs[k][n] for n in outs[k] if n != "conv_w"})
        d["conv_w"] = g_cw if k == 0 else cw_out[k - 1]
        res.append(d)
    return (loss, dx[None], *[res[k][n] for k in range(4) for n in _NAMES])


class _Exchange:
    GROUPS = (("w_ple_gate", "w_ple_proj", "w_down", "w_up"), ("w_mix_out", "w_attn_out", "w_pool_out"), ("w_in",))
    FIRST = ("w_in",)

    def __init__(self, W, M, V):
        self.W, self.M, self.V = W, M, V
        xi, yi, ci = _coords()
        me = _dev_index(xi, yi, ci)
        self.me = me.astype(I32).reshape(1)
        self.lands = [{n: lax.dynamic_update_index_in_dim(lax.empty((N_DEV,) + W[n].shape[1:], BF16),
                                                          W[n][li].astype(BF16), me, 0) for n in _SHARDED}
                      for li in range(DEPTH)]
        (cw,) = all_gather([W["conv_w"]], "ag_conv_w")
        self.cw = cw.transpose(1, 2, 0, 3).reshape(DEPTH, 3, 4, FF_BLK).transpose(0, 2, 1, 3)
        self.ag, self.fwd, self.rs, self.pending, self.small = {}, {}, {}, {}, None
        self.stacks = {n: [lax.empty((DEPTH,) + W[n].shape[1:], F32) for _ in range(4)] for n in _SHARDED}
        self.late = tuple(n for n in _SHARDED if n not in self.FIRST)
        self.ag[0] = ag_start([self.lands[0][n] for n in self.FIRST], cw, "ag_start0")

    def tokens(self):
        return [self.ag[0][3]]

    def prefetch(self, li, after):
        send, recv, lands, _ = self.ag[li]
        self.fwd[li] = ag_forward(send, recv, lands, after, f"ag_forward{li}")
        if li == 0:
            self.ag["0b"] = ag_start([self.lands[0][n] for n in self.late], self.fwd[0][2][0], "ag_start0b")

    def weights(self, li, after):
        send, recv, _, _ = self.ag.pop(li)
        fsend, frecv, lands = self.fwd.pop(li)
        lands = ag_finish(send, recv, fsend, frecv, lands, after, f"ag_finish{li}")
        if li == 0:
            return dict(zip(self.FIRST, lands)), self.cw[li], (self.ag["0b"][3],)
        tokens = ()
        if li + 1 < DEPTH:
            self.ag[li + 1] = ag_start([self.lands[li + 1][n] for n in _SHARDED], lands[0], f"ag_start{li + 1}")
            tokens = (self.ag[li + 1][3],)
        return dict(zip(_SHARDED, lands)), self.cw[li], tokens

    def rest(self, li, G, mid, after):
        if li != 0:
            return G, ()
        send, recv, lands, _ = self.ag.pop("0b")
        fsend, frecv, lands = ag_forward(send, recv, lands, mid, "ag_forward0b")
        lands = ag_finish(send, recv, fsend, frecv, lands, after, "ag_finish0b")
        self.ag[1] = ag_start([self.lands[1][n] for n in _SHARDED], lands[0], "ag_start1")
        return {**G, **dict(zip(self.late, lands))}, (self.ag[1][3],)

    def grads(self, li, group, gw):
        self.pending.setdefault(li, {}).update(gw)
        if li != 0 and group != len(self.GROUPS) - 1:
            return None
        gw = self.pending.pop(li)
        tag = f"{li}_{group}" if li == 0 else f"{li}"
        send, recv, parts, lands, token = rs_start(list(gw.values()), f"rs_start{tag}")
        self.rs.setdefault(li, []).append((tag, tuple(gw), send, recv, parts, lands))
        if li == 0 and group == 1 and self.small is not None:
            send, recv, lands, _ = self.small
            self.small = (send, recv) + ag_forward(send, recv, lands, token, "ag_forward_small")
        return token

    def update(self, after):
        for li in sorted(self.rs, reverse=True):
            for tag, names, send, recv, parts, lands in self.rs.pop(li):
                parts, lands = rs_finish(send, recv, parts, lands, after, f"rs_finish{tag}")
                for n, part, land in zip(names, parts, lands):
                    self.stacks[n] = adamw_shard(self.me, part, land, self.W[n], self.M[n], self.V[n], li,
                                                 self.stacks[n], f"adamw_{n}{li}")
                    after = self.stacks[n][0]

    def replicated_early(self, small, after):
        land = lax.dynamic_update_index_in_dim(lax.empty((N_DEV, _PACK_SECTION * len(_EARLY), _PACK_LANES), F32),
                                               _pack(small, _EARLY), self.me[0], 0)
        self.small = ag_start([land], after, "ag_start_small")
        return self.small[3]

    def replicated_finish(self, after):
        send, recv, fsend, frecv, lands = self.small
        return ag_finish(send, recv, fsend, frecv, lands, after, "ag_finish_small")[0]


def _local_step(x2, tgt, pb, W, ex):
    depth = W["rpb"].shape[0]
    vec = lambda t: t.reshape(1, -1)
    ln1_g, ln1_b, ln2_g, ln2_b = W["ln1_g"], W["ln1_b"], W["ln2_g"], W["ln2_b"]
    b_in, rpb, pool_scale = W["b_in"], W["rpb"], W["pool_scale"]
    cb_full = W["conv_b"].reshape(depth, 4, 1, FF_BLK)
    pool_w_b = W["pool_w"].astype(BF16)

    h, hb = ln_fwd(x2, vec(W["ln_in_g"]), vec(W["ln_in_b"]), "ln_in", after=ex.tokens())
    ex.prefetch(0, hb)
    saved = []
    for li in range(depth):
        G, cw, tokens = ex.weights(li, hb)
        e_tab, e_rev = _bias_tables(rpb[li])
        bias = vec(b_in[li])
        proj = proj_fwd(hb, G["w_in"], bias, li, 0, N_DEV, BF16, f"proj{li}", after=tokens)
        u = proj_fwd(hb, G["w_in"], bias, li, 3, 1, F32, f"proj_u{li}")
        att = attn_fwd(proj, e_tab, f"attn{li}")
        pm, pw = pool_fwd(u, pool_w_b[li], vec(pool_scale[li]), f"pool{li}")
        G, tokens = ex.rest(li, G, att, pw)
        mg, ya, yp = merge_fwd(att, pw, G["w_attn_out"], G["w_pool_out"], proj, li, f"merge{li}", after=tokens)
        if li + 1 < depth:
            ex.prefetch(li + 1, mg)
        z1, h1, h1b = mix_ln_fwd(mg, G["w_mix_out"], h, vec(ln1_g[li]), vec(ln1_b[li]), li, f"mix_ln{li}")
        up = up_fwd(h1b, G["w_up"], li, f"up{li}")
        t = ffn_act_fwd(up, cw, cb_full[li], f"ffn_act{li}")
        z2, h2, h2b, pg, pp = down_ple_ln_fwd(t, G["w_down"], h1b, G["w_ple_gate"], pb[li], G["w_ple_proj"], h1,
                                              vec(ln2_g[li]), vec(ln2_b[li]), li, f"down_ln{li}")
        saved.append(dict(hb=hb, proj=proj, att=att, pm=pm, pw=pw, mg=mg, ya=ya, yp=yp, z1=z1, h1b=h1b, up=up, t=t,
                          z2=z2, pg=pg, pp=pp, e_rev=e_rev, G=G, cw=cw))
        h, hb = h2, h2b

    dh, loss_part = loss_bwd(h, tgt, "loss")
    small = {n: [None] * depth for n in ("b_in", "rpb", "pool_w", "pool_scale", "ln1_g", "ln1_b", "conv_b", "ln2_g",
                                         "ln2_b", "conv_w")}
    token = ()
    tok = lambda t: () if t is None else (t,)
    for li in reversed(range(depth)):
        sv = saved[li]
        G, cw = sv["G"], sv["cw"]
        dz2, dz2b, dpg, dpp, dg2, db2 = ln2_ple_bwd(dh, sv["z2"], vec(ln2_g[li]), sv["pg"], sv["pp"], f"ln2_bwd{li}",
                                                    after=token)
        gw = {}
        gw["w_ple_gate"] = wgrad_rows(sv["h1b"], dpg, f"dw_pg{li}")
        gw["w_ple_proj"] = wgrad_cols(pb[li], dpp, f"dw_pp{li}")
        gw["w_down"] = wgrad_down(sv["t"], dz2b, f"dw_down{li}").reshape(N_DEV, FF_SHARD, D)
        dhv, dhg, dcw, dcb = ffn_act_bwd(dz2b, G["w_down"], sv["up"], cw, cb_full[li], li, f"ffn_bwd{li}")
        gw["w_up"] = wgrad_up(sv["h1b"], dhv, dhg, f"dw_up{li}")
        token = tok(ex.grads(li, 0, gw))
        dz1, dz1b, dg1, db1 = dh1_ln1_bwd(dz2, dpg, G["w_ple_gate"], dhv, dhg, G["w_up"], sv["z1"], vec(ln1_g[li]), li,
                                          f"ln1_bwd{li}", after=token)
        gw = {"w_mix_out": wgrad_rows(sv["mg"], dz1b, f"dw_mix{li}")}
        dya, dyp, dga, dgb = merge_bwd(dz1b, G["w_mix_out"], sv["proj"], sv["ya"], sv["yp"], li, f"merge_bwd{li}")
        gw["w_attn_out"] = wgrad_cols(sv["att"], dya, f"dw_ao{li}")
        gw["w_pool_out"] = wgrad_cols(sv["pw"], dyp, f"dw_po{li}")
        token = tok(ex.grads(li, 1, gw))
        da = attn_out_bwd(dya, G["w_attn_out"], li, f"da{li}", after=token)
        du, dpool_w, dpool_sc = pool_bwd(dyp, G["w_pool_out"], sv["pm"], pool_w_b[li], vec(pool_scale[li]), li,
                                         f"pool_bwd{li}")
        dq, dk, dv, drpb = attn_bwd(sv["proj"], da, sv["e_rev"], f"attn_bwd{li}")
        dproj = jnp.concatenate([dq, dk, dv, du, dga, dgb], axis=1)
        dw_in, db_in = wgrad_cols(sv["hb"], dproj, f"dw_in{li}", with_colsum=True)
        token = tok(ex.grads(li, 2, {"w_in": dw_in}))
        dh = dh0_bwd(dz1, dproj, G["w_in"], li, f"dh0{li}", after=token)
        small["b_in"][li] = db_in.reshape(N_PROJ)
        small["rpb"][li] = drpb.reshape(N_HEADS, KROWS, GRID_W)[:, :2 * KH - 1, :2 * KW - 1]
        small["pool_w"][li] = dpool_w
        small["pool_scale"][li] = dpool_sc.reshape(D_POOL)
        small["ln1_g"][li], small["ln1_b"][li] = dg1.reshape(D), db1.reshape(D)
        small["ln2_g"][li], small["ln2_b"][li] = dg2.reshape(D), db2.reshape(D)
        small["conv_b"][li] = dcb.reshape(D_FF)
        small["conv_w"][li] = dcw.transpose(1, 0, 2).reshape(3, D_FF)
        if li == 1:
            token = token + tok(ex.replicated_early(small, dh))
    dx, dg_in, db_in0 = ln_bwd(dh, x2, vec(W["ln_in_g"]), "ln_in_bwd", after=token)
    ex.update(dx)
    parts = {n: jnp.stack(v_) for n, v_ in small.items()}
    parts["ln_in_g"], parts["ln_in_b"] = dg_in.reshape(D), db_in0.reshape(D)
    return loss_part, dx, parts
```

```python
import numpy as np
import jax
import jax.numpy as jnp
from jax import lax
from jax.experimental import pallas as pl
from jax.experimental.pallas import tpu as pltpu

F32 = jnp.float32
BF16 = jnp.bfloat16
I32 = jnp.int32

D = 1024
DEPTH = 4
GRID_W = 64
N_HEADS = 8
HEAD_DIM = 64
D_ATTN = 512
KH = 8
KW = 16
POOL_WINDOWS = (2, 4, 8, 16)
D_POOL = 512
PGD = 128
D_FF = 2816
PLE_DIM = 256
N_PROJ = 4096
ALPHA = (2 * DEPTH) ** 0.25
LN_EPS = 1e-5
NEG_INF = -1e30
ATT_SCALE = HEAD_DIM ** -0.5
ADAM_LR = 0.001
ADAM_B1 = 0.9
ADAM_B2 = 0.999
ADAM_EPS = 1e-08
ADAM_WD = 0.01
ADAM_STEP = 10

N_DEV = 8
AXES = ("x", "y", "c")
FF_BLK = D_FF // 4
FF_SHARD = D_FF // N_DEV
QROWS = 8
KROWS = 16
QB = QROWS * GRID_W
KB = KROWS * GRID_W
V7X_VMEM_LIMIT = 56 * 2 ** 20
MESH = pl.DeviceIdType.MESH
ANY = pl.BlockSpec(memory_space=pl.ANY)


def _cparams(n_grid):
    return pltpu.CompilerParams(dimension_semantics=("arbitrary",) * n_grid, vmem_limit_bytes=V7X_VMEM_LIMIT)


def _nn(a, b):
    return lax.dot_general(a, b, (((1,), (0,)), ((), ())), preferred_element_type=F32)


def _nt(a, b):
    return lax.dot_general(a, b, (((1,), (1,)), ((), ())), preferred_element_type=F32)


def _tn(a, b):
    return lax.dot_general(a, b, (((0,), (0,)), ((), ())), preferred_element_type=F32)


def _sigmoid(x):
    return 1.0 / (1.0 + jnp.exp(-x))


def _ln_fwd(z, g, b):
    mu = jnp.mean(z, axis=-1, keepdims=True)
    xc = z - mu
    var = jnp.mean(xc * xc, axis=-1, keepdims=True)
    return xc * lax.rsqrt(var + LN_EPS) * g + b


def _ln_bwd(dh, z, g):
    mu = jnp.mean(z, axis=-1, keepdims=True)
    xc = z - mu
    var = jnp.mean(xc * xc, axis=-1, keepdims=True)
    rstd = lax.rsqrt(var + LN_EPS)
    xhat = xc * rstd
    dxh = dh * g
    m1 = jnp.mean(dxh, axis=-1, keepdims=True)
    m2 = jnp.mean(dxh * xhat, axis=-1, keepdims=True)
    return rstd * (dxh - m1 - xhat * m2), dh * xhat


def _colsum(x):
    return jnp.sum(x, axis=0, keepdims=True)


def _lane_cat(ref):
    return jnp.concatenate([ref[j] for j in range(ref.shape[0])], axis=1)


def _row_cat(ref):
    n, r, c = ref.shape
    return ref[...].reshape(n * r, c)


def _shards(n, r, c, li, j_of=None):
    del li
    if j_of is None:
        return pl.BlockSpec((n, r, c), lambda *_: (0, 0, 0))
    return pl.BlockSpec((n, r, c), lambda *g: (j_of(*g), 0, 0))


def _shard(r, c, li, j_of):
    del li
    return pl.BlockSpec((None, r, c), lambda *g: (j_of(*g), 0, 0))


def ln_fwd(x, g, b, name, after=()):
    s = x.shape[0]
    tm = 512
    na = len(after)

    def body(x_ref, g_ref, b_ref, *rest):
        h_ref, hb_ref = rest[na:]
        h = _ln_fwd(x_ref[...], g_ref[...], b_ref[...])
        h_ref[...] = h
        hb_ref[...] = h.astype(BF16)

    row = pl.BlockSpec((tm, D), lambda i: (i, 0))
    vec = pl.BlockSpec((1, D), lambda i: (0, 0))
    return pl.pallas_call(
        body, name=name, grid=(s // tm,), in_specs=[row, vec, vec] + [ANY] * na, out_specs=[row, row],
        out_shape=[jax.ShapeDtypeStruct((s, D), F32), jax.ShapeDtypeStruct((s, D), BF16)],
        compiler_params=_cparams(1))(x, g, b, *after)


def proj_fwd(hb, win, bias, li, j0, nj, out_dtype, name, after=()):
    s = hb.shape[0]
    bn = N_PROJ // N_DEV
    tm = 1024

    def body(a_ref, w_ref, b_ref, *rest):
        rest[-1][...] = (_nn(a_ref[...], w_ref[...]) + b_ref[...]).astype(out_dtype)

    return pl.pallas_call(
        body, name=name, grid=(s // tm, nj),
        in_specs=[pl.BlockSpec((tm, D), lambda i, j: (i, 0)),
                  _shard(D, bn, li, lambda i, j: j0 + j),
                  pl.BlockSpec((1, bn), lambda i, j: (0, j0 + j))] + [ANY] * len(after),
        out_specs=pl.BlockSpec((tm, bn), lambda i, j: (i, j)),
        out_shape=jax.ShapeDtypeStruct((s, nj * bn), out_dtype),
        compiler_params=_cparams(2))(hb, win, bias, *after)


def _attn_types(b, nb):
    first, last = 0, (nb * QROWS - KROWS) * GRID_W
    mid = pl.multiple_of((QROWS * b - KH // 2) * GRID_W, 256)
    return ((b == 0, first), ((b > 0) & (b < nb - 1), mid), (b == nb - 1, last))


def _attn_row(btype, qr):
    lo, delta = ((max(qr - KH // 2, 0), 0), (qr, -(KH // 2)), (min(qr + KH // 2, KH), -KH))[btype]
    return lo, (qr - delta - (KH - 1)) % KROWS, lo - qr + delta + KH - 1


def _row_window(lo):
    pad = (lo % 2) * GRID_W
    return (lo // 2) * 128, KH * GRID_W + 2 * pad, pad


def _lanes(ref, start, width):
    start %= KB
    if start + width <= KB:
        return ref[:, start:start + width]
    return jnp.concatenate([ref[:, start:], ref[:, :start + width - KB]], axis=1)


def _row_logits(s_ref, e_ref, hh, rows, btype, qr):
    lo, shift, _ = _attn_row(btype, qr)
    a0, w, pad = _row_window(lo)
    e = e_ref.at[hh, shift % 2]
    sb = s_ref[rows, a0:a0 + w] + _lanes(e, a0 - (shift - shift % 2) * GRID_W, w)
    if pad:
        lane = lax.broadcasted_iota(I32, (1, w), 1)
        sb = jnp.where((lane >= pad) & (lane < w - pad), sb, NEG_INF)
    return sb, a0, w, pad


def _store_row(ref, rows, a0, w, val):
    if a0:
        ref[rows, 0:a0] = jnp.zeros((GRID_W, a0), ref.dtype)
    ref[rows, a0:a0 + w] = val.astype(ref.dtype)
    if a0 + w < KB:
        ref[rows, a0 + w:KB] = jnp.zeros((GRID_W, KB - a0 - w), ref.dtype)


def attn_fwd(proj, e_tab, name):
    s = proj.shape[0]
    nb = s // QB

    def body(q_ref, k_ref, v_ref, e_ref, o_ref, s_ref, p_ref):
        q = q_ref[...] * ATT_SCALE
        lane = lax.broadcasted_iota(I32, (1, 128), 1)

        def block(btype, k0):
            kwin = k_ref[pl.ds(k0, KB), :]
            vwin = v_ref[pl.ds(k0, KB), :]
            acc = jnp.zeros((QB, 128), F32)
            for hh in range(2):
                lm = (lane // HEAD_DIM) == hh
                qh = jnp.where(lm, q, jnp.zeros_like(q))
                vh = jnp.where(lm, vwin, jnp.zeros_like(vwin))
                s_ref[...] = _nt(qh, kwin)
                for qr in range(QROWS):
                    rows = slice(qr * GRID_W, (qr + 1) * GRID_W)
                    sb, a0, w, _ = _row_logits(s_ref, e_ref, hh, rows, btype, qr)
                    p = jnp.exp(sb - jnp.max(sb, axis=1, keepdims=True))
                    _store_row(p_ref, rows, a0, w, p * (1.0 / jnp.sum(p, axis=1, keepdims=True)))
                acc = acc + _nn(p_ref[...], vh)
            o_ref[...] = acc.astype(BF16)

        for btype, (cond, k0) in enumerate(_attn_types(pl.program_id(1), nb)):
            pl.when(cond)(lambda btype=btype, k0=k0: block(btype, k0))

    return pl.pallas_call(
        body, name=name, grid=(4, nb),
        in_specs=[pl.BlockSpec((QB, 128), lambda j, b: (b, j)),
                  pl.BlockSpec((s, 128), lambda j, b: (0, 4 + j)),
                  pl.BlockSpec((s, 128), lambda j, b: (0, 8 + j)),
                  pl.BlockSpec((2, 2, GRID_W, KB), lambda j, b: (j, 0, 0, 0))],
        out_specs=pl.BlockSpec((QB, 128), lambda j, b: (b, j)),
        out_shape=jax.ShapeDtypeStruct((s, D_ATTN), BF16),
        scratch_shapes=[pltpu.VMEM((QB, KB), F32), pltpu.VMEM((QB, KB), BF16)],
        compiler_params=_cparams(2))(proj, proj, proj, e_tab)


_POOL_PAD = 8


def _pool_counts(s, w):
    t = lax.broadcasted_iota(I32, (s, 1), 0)
    return (jnp.minimum(t + w // 2, s) - jnp.maximum(t - w // 2, 0)).astype(F32)


def _window_sum(x, w, back_first):
    s = x.shape[0]
    z = jnp.zeros((_POOL_PAD, x.shape[1]), F32)
    xe = jnp.concatenate([z, x, z], axis=0)
    n = s + 2 * _POOL_PAD
    acc = xe + pltpu.roll(xe, 1 if back_first else n - 1, 0)
    k = 1
    while 2 * k < w:
        acc = pltpu.roll(acc, k, 0) + pltpu.roll(acc, n - k, 0)
        k *= 2
    return acc[_POOL_PAD:_POOL_PAD + s, :]


def pool_fwd(u, pool_w, pool_scale, name):
    s = u.shape[0]

    def body(u_ref, w_ref, sc_ref, pm_ref, pw_ref):
        for g, w in enumerate(POOL_WINDOWS):
            cols = slice(g * PGD, (g + 1) * PGD)
            ug = u_ref[:, cols]
            pm = (_window_sum(ug, w, True) / _pool_counts(s, w) - ug).astype(BF16)
            pm_ref[:, cols] = pm
            pw_ref[:, cols] = (_nn(pm, w_ref[g]) * sc_ref[:, cols]).astype(BF16)

    full = lambda shape: pl.BlockSpec(shape, lambda i: (0,) * len(shape))
    return pl.pallas_call(
        body, name=name, grid=(1,),
        in_specs=[full((s, D_POOL)), full((4, PGD, PGD)), full((1, D_POOL))],
        out_specs=[full((s, D_POOL)), full((s, D_POOL))],
        out_shape=[jax.ShapeDtypeStruct((s, D_POOL), BF16)] * 2,
        compiler_params=_cparams(1))(u, pool_w, pool_scale)


def merge_fwd(a, pw, wao, wpo, proj, li, name, after=()):
    s = a.shape[0]
    tm, tn = 512, 512
    nt = D // tn
    per = tn // 128

    def body(a_ref, pw_ref, wa_ref, wp_ref, ga_ref, gb_ref, *rest):
        mg_ref, ya_ref, yp_ref = rest[len(after):]
        ya = _nn(a_ref[...], _lane_cat(wa_ref))
        yp = _nn(pw_ref[...], _lane_cat(wp_ref))
        mg = _sigmoid(ga_ref[...].astype(F32)) * ya + _sigmoid(gb_ref[...].astype(F32)) * yp
        mg_ref[...] = mg.astype(BF16)
        ya_ref[...] = ya.astype(BF16)
        yp_ref[...] = yp.astype(BF16)

    act = pl.BlockSpec((tm, D_ATTN), lambda i, j: (i, 0))
    wsp = _shards(per, D_ATTN, 128, li, lambda i, j: j)
    out = pl.BlockSpec((tm, tn), lambda i, j: (i, j))
    ga0 = (3 * D_ATTN + D_POOL) // tn
    return pl.pallas_call(
        body, name=name, grid=(s // tm, nt),
        in_specs=[act, act, wsp, wsp,
                  pl.BlockSpec((tm, tn), lambda i, j: (i, ga0 + j)),
                  pl.BlockSpec((tm, tn), lambda i, j: (i, ga0 + nt + j))] + [ANY] * len(after),
        out_specs=[out, out, out],
        out_shape=[jax.ShapeDtypeStruct((s, D), BF16)] * 3,
        compiler_params=_cparams(2))(a, pw, wao, wpo, proj, proj, *after)


def mix_ln_fwd(mg, wmix, h0, g, b, li, name):
    s = mg.shape[0]
    tm = 256

    def body(mg_ref, w_ref, h0_ref, g_ref, b_ref, z_ref, h_ref, hb_ref):
        z = ALPHA * h0_ref[...] + _nn(mg_ref[...], _row_cat(w_ref))
        h = _ln_fwd(z, g_ref[...], b_ref[...])
        z_ref[...] = z
        h_ref[...] = h
        hb_ref[...] = h.astype(BF16)

    row = pl.BlockSpec((tm, D), lambda i: (i, 0))
    vec = pl.BlockSpec((1, D), lambda i: (0, 0))
    return pl.pallas_call(
        body, name=name, grid=(s // tm,),
        in_specs=[row, _shards(N_DEV, D // N_DEV, D, li), row, vec, vec],
        out_specs=[row, row, row],
        out_shape=[jax.ShapeDtypeStruct((s, D), F32), jax.ShapeDtypeStruct((s, D), F32),
                   jax.ShapeDtypeStruct((s, D), BF16)],
        compiler_params=_cparams(1))(mg, wmix, h0, g, b)


def up_fwd(hb, wup, li, name):
    s = hb.shape[0]
    tm = 1024

    def body(a_ref, w_ref, o_ref):
        o_ref[...] = _nt(a_ref[...], w_ref[...]).astype(BF16)

    return pl.pallas_call(
        body, name=name, grid=(s // tm, N_DEV),
        in_specs=[pl.BlockSpec((tm, D), lambda i, j: (i, 0)), _shard(FF_BLK, D, li, lambda i, j: j)],
        out_specs=pl.BlockSpec((None, tm, FF_BLK), lambda i, j: (j, i, 0)),
        out_shape=jax.ShapeDtypeStruct((N_DEV, s, FF_BLK), BF16),
        compiler_params=_cparams(2))(hb, wup)


_SQRT_HALF = 0.7071067811865476
_INV_SQRT_2PI = 0.3989422804014327


def _shift_rows(x, prev_row, next_row):
    n = x.shape[0]
    r = lax.broadcasted_iota(I32, (n, 1), 0)
    back = jnp.where(r == 0, prev_row, pltpu.roll(x, 1, 0))
    fwd = jnp.where(r == n - 1, next_row, pltpu.roll(x, n - 1, 0))
    return back, fwd


HALO = 16


def _halo_maps(tm, s):
    th = tm // HALO
    return (lambda i: jnp.maximum(i * th - 1, 0)), (lambda i: jnp.minimum((i + 1) * th, s // HALO - 1))


def _slab_specs(tm, s, blk_of):
    before, after = _halo_maps(tm, s)
    main = pl.BlockSpec((None, tm, FF_BLK), lambda c, i: (blk_of(c), i, 0))
    prev = pl.BlockSpec((None, HALO, FF_BLK), lambda c, i: (blk_of(c), before(i), 0))
    nxt = pl.BlockSpec((None, HALO, FF_BLK), lambda c, i: (blk_of(c), after(i), 0))
    return main, prev, nxt


def ffn_act_fwd(up, conv_w, conv_b, name):
    s = up.shape[1]
    tm = 512
    nt = s // tm
    hv_main, _, _ = _slab_specs(tm, s, lambda c: c)
    hg_main, hg_prev, hg_next = _slab_specs(tm, s, lambda c: 4 + c)

    def body(hv_ref, hg_ref, hp_ref, hn_ref, cw_ref, cb_ref, t_ref):
        i = pl.program_id(1)
        hg = hg_ref[...].astype(F32)
        prow = jnp.where(i == 0, 0.0, hp_ref[...].astype(F32)[HALO - 1:HALO, :])
        nrow = jnp.where(i == nt - 1, 0.0, hn_ref[...].astype(F32)[0:1, :])
        back, fwd = _shift_rows(hg, prow, nrow)
        c = back * cw_ref[0:1, :] + hg * cw_ref[1:2, :] + fwd * cw_ref[2:3, :] + cb_ref[...]
        act = 0.5 * c * (1.0 + lax.erf(c * _SQRT_HALF))
        t_ref[...] = (act * hv_ref[...].astype(F32)).astype(BF16)

    return pl.pallas_call(
        body, name=name, grid=(4, nt),
        in_specs=[hv_main, hg_main, hg_prev, hg_next,
                  pl.BlockSpec((None, 3, FF_BLK), lambda c, i: (c, 0, 0)),
                  pl.BlockSpec((None, 1, FF_BLK), lambda c, i: (c, 0, 0))],
        out_specs=pl.BlockSpec((None, tm, FF_BLK), lambda c, i: (c, i, 0)),
        out_shape=jax.ShapeDtypeStruct((4, s, FF_BLK), BF16),
        compiler_params=_cparams(2))(up, up, up, up, conv_w, conv_b)


def down_ple_ln_fwd(t, wdown, hb, wpg, pb, wpp, h1, g, b, li, name):
    s = hb.shape[0]
    tm = 256

    def body(t_ref, wd_ref, hb_ref, wpg_ref, p_ref, wpp_ref, h1_ref, g_ref, b_ref,
             z_ref, h_ref, hbo_ref, pg_ref, pp_ref):
        wd = _row_cat(wd_ref)
        ffn = _nn(t_ref[0], wd[0:FF_BLK, :])
        for c in range(1, 4):
            ffn = ffn + _nn(t_ref[c], wd[c * FF_BLK:(c + 1) * FF_BLK, :])
        pg = _nn(hb_ref[...], _row_cat(wpg_ref))
        pp = _nn(p_ref[...], _lane_cat(wpp_ref))
        z = ALPHA * h1_ref[...] + ffn + _sigmoid(pg) * pp
        h = _ln_fwd(z, g_ref[...], b_ref[...])
        z_ref[...] = z
        h_ref[...] = h
        hbo_ref[...] = h.astype(BF16)
        pg_ref[...] = pg.astype(BF16)
        pp_ref[...] = pp.astype(BF16)

    row = pl.BlockSpec((tm, D), lambda i: (i, 0))
    vec = pl.BlockSpec((1, D), lambda i: (0, 0))
    return pl.pallas_call(
        body, name=name, grid=(s // tm,),
        in_specs=[pl.BlockSpec((4, tm, FF_BLK), lambda i: (0, i, 0)),
                  _shards(N_DEV, FF_SHARD, D, li),
                  row, _shards(N_DEV, D // N_DEV, D, li),
                  pl.BlockSpec((tm, PLE_DIM), lambda i: (i, 0)),
                  _shards(N_DEV, PLE_DIM, 128, li),
                  row, vec, vec],
        out_specs=[row] * 5,
        out_shape=[jax.ShapeDtypeStruct((s, D), F32), jax.ShapeDtypeStruct((s, D), F32),
                   jax.ShapeDtypeStruct((s, D), BF16), jax.ShapeDtypeStruct((s, D), BF16),
                   jax.ShapeDtypeStruct((s, D), BF16)],
        compiler_params=_cparams(1))(t, wdown, hb, wpg, pb, wpp, h1, g, b)


def loss_bwd(h, target, name):
    s = h.shape[0]
    tm = 512

    def body(h_ref, t_ref, dh_ref, l_ref):
        @pl.when(pl.program_id(0) == 0)
        def _():
            l_ref[...] = jnp.zeros_like(l_ref)
        e = h_ref[...] - t_ref[...]
        dh_ref[...] = e * (1.0 / D)
        l_ref[...] += 0.5 * jnp.sum(jnp.mean(e * e, axis=-1, keepdims=True), axis=0, keepdims=True)

    row = pl.BlockSpec((tm, D), lambda i: (i, 0))
    return pl.pallas_call(
        body, name=name, grid=(s // tm,), in_specs=[row, row],
        out_specs=[row, pl.BlockSpec((1, 1), lambda i: (0, 0))],
        out_shape=[jax.ShapeDtypeStruct((s, D), F32), jax.ShapeDtypeStruct((1, 1), F32)],
        compiler_params=_cparams(1))(h, target)


def ln_bwd(dh, z, g, name, after=()):
    s = dh.shape[0]
    tm = 512
    na = len(after)

    def body(dh_ref, z_ref, g_ref, *rest):
        dz_ref, dg_ref, db_ref = rest[na:]

        @pl.when(pl.program_id(0) == 0)
        def _():
            dg_ref[...] = jnp.zeros_like(dg_ref)
            db_ref[...] = jnp.zeros_like(db_ref)
        dh = dh_ref[...]
        dz, dgx = _ln_bwd(dh, z_ref[...], g_ref[...])
        dz_ref[...] = dz
        dg_ref[...] += _colsum(dgx)
        db_ref[...] += _colsum(dh)

    row = pl.BlockSpec((tm, D), lambda i: (i, 0))
    vec = pl.BlockSpec((1, D), lambda i: (0, 0))
    return pl.pallas_call(
        body, name=name, grid=(s // tm,), in_specs=[row, row, vec] + [ANY] * na, out_specs=[row, vec, vec],
        out_shape=[jax.ShapeDtypeStruct((s, D), F32), jax.ShapeDtypeStruct((1, D), F32),
                   jax.ShapeDtypeStruct((1, D), F32)],
        compiler_params=_cparams(1))(dh, z, g, *after)


def ln2_ple_bwd(dh, z, g, pg, pp, name, after=()):
    s = dh.shape[0]
    tm = 512
    na = len(after)

    def body(dh_ref, z_ref, g_ref, pg_ref, pp_ref, *rest):
        dz_ref, dzb_ref, dpg_ref, dpp_ref, dg_ref, db_ref = rest[na:]

        @pl.when(pl.program_id(0) == 0)
        def _():
            dg_ref[...] = jnp.zeros_like(dg_ref)
            db_ref[...] = jnp.zeros_like(db_ref)
        dh = dh_ref[...]
        dz, dgx = _ln_bwd(dh, z_ref[...], g_ref[...])
        sg = _sigmoid(pg_ref[...].astype(F32))
        dz_ref[...] = dz
        dzb_ref[...] = dz.astype(BF16)
        dpg_ref[...] = (dz * pp_ref[...].astype(F32) * sg * (1.0 - sg)).astype(BF16)
        dpp_ref[...] = (dz * sg).astype(BF16)
        dg_ref[...] += _colsum(dgx)
        db_ref[...] += _colsum(dh)

    row = pl.BlockSpec((tm, D), lambda i: (i, 0))
    vec = pl.BlockSpec((1, D), lambda i: (0, 0))
    return pl.pallas_call(
        body, name=name, grid=(s // tm,), in_specs=[row, row, vec, row, row] + [ANY] * na,
        out_specs=[row, row, row, row, vec, vec],
        out_shape=[jax.ShapeDtypeStruct((s, D), F32)] + [jax.ShapeDtypeStruct((s, D), BF16)] * 3
        + [jax.ShapeDtypeStruct((1, D), F32)] * 2,
        compiler_params=_cparams(1))(dh, z, g, pg, pp, *after)


def wgrad_rows(a, dy, name, after=()):
    s, k = a.shape
    n = dy.shape[1]
    kb = k // N_DEV

    def body(a_ref, dy_ref, *rest):
        rest[-1][...] = _tn(a_ref[...], dy_ref[...]).astype(BF16)

    return pl.pallas_call(
        body, name=name, grid=(N_DEV,),
        in_specs=[pl.BlockSpec((s, kb), lambda j: (0, j)), pl.BlockSpec((s, n), lambda j: (0, 0))] + [ANY] * len(after),
        out_specs=pl.BlockSpec((None, kb, n), lambda j: (j, 0, 0)),
        out_shape=jax.ShapeDtypeStruct((N_DEV, kb, n), BF16),
        compiler_params=_cparams(1))(a, dy, *after)


def wgrad_cols(a, dy, name, with_colsum=False):
    s, k = a.shape
    n = dy.shape[1]
    nb = n // N_DEV

    def body(a_ref, dy_ref, o_ref, *cs_ref):
        dy = dy_ref[...]
        o_ref[...] = _tn(a_ref[...], dy).astype(BF16)
        if with_colsum:
            cs_ref[0][...] = _colsum(dy.astype(F32))

    out_specs = [pl.BlockSpec((None, k, nb), lambda j: (j, 0, 0))]
    out_shape = [jax.ShapeDtypeStruct((N_DEV, k, nb), BF16)]
    if with_colsum:
        out_specs.append(pl.BlockSpec((1, nb), lambda j: (0, j)))
        out_shape.append(jax.ShapeDtypeStruct((1, n), F32))
    res = pl.pallas_call(
        body, name=name, grid=(N_DEV,),
        in_specs=[pl.BlockSpec((s, k), lambda j: (0, 0)), pl.BlockSpec((s, nb), lambda j: (0, j))],
        out_specs=out_specs, out_shape=out_shape,
        compiler_params=_cparams(1))(a, dy)
    return res if with_colsum else res[0]


def wgrad_down(t, dy, name):
    _, s, k = t.shape
    n = dy.shape[1]

    def body(a_ref, dy_ref, o_ref):
        o_ref[...] = _tn(a_ref[...], dy_ref[...]).astype(BF16)

    return pl.pallas_call(
        body, name=name, grid=(4,),
        in_specs=[pl.BlockSpec((None, s, k), lambda j: (j, 0, 0)), pl.BlockSpec((s, n), lambda j: (0, 0))],
        out_specs=pl.BlockSpec((None, k, n), lambda j: (j, 0, 0)),
        out_shape=jax.ShapeDtypeStruct((4, k, n), BF16),
        compiler_params=_cparams(1))(t, dy)


def wgrad_up(a, dhv, dhg, name):
    s, k = a.shape

    def body(a_ref, dv_ref, dg_ref, o_ref):
        j = pl.program_id(0)

        @pl.when(j < 4)
        def _():
            o_ref[...] = _tn(dv_ref[...], a_ref[...]).astype(BF16)

        @pl.when(j >= 4)
        def _():
            o_ref[...] = _tn(dg_ref[...], a_ref[...]).astype(BF16)

    return pl.pallas_call(
        body, name=name, grid=(N_DEV,),
        in_specs=[pl.BlockSpec((s, k), lambda j: (0, 0)),
                  pl.BlockSpec((None, s, FF_BLK), lambda j: (jnp.minimum(j, 3), 0, 0)),
                  pl.BlockSpec((None, s, FF_BLK), lambda j: (jnp.maximum(j - 4, 0), 0, 0))],
        out_specs=pl.BlockSpec((None, FF_BLK, k), lambda j: (j, 0, 0)),
        out_shape=jax.ShapeDtypeStruct((N_DEV, FF_BLK, k), BF16),
        compiler_params=_cparams(1))(a, dhv, dhg)


def ffn_act_bwd(dzb, wdown, up, conv_w, conv_b, li, name):
    s = up.shape[1]
    tm = 512
    nt = s // tm
    before, after = _halo_maps(tm, s)
    hv_main, hv_prev, hv_next = _slab_specs(tm, s, lambda c: c)
    hg_main, hg_prev, hg_next = _slab_specs(tm, s, lambda c: 4 + c)

    def dc_of(dz, wd, hv, hg, back, fwd, cw_ref, cb_ref):
        dt = _nt(dz, wd)
        c = back * cw_ref[0:1, :] + hg * cw_ref[1:2, :] + fwd * cw_ref[2:3, :] + cb_ref[...]
        cdf = 0.5 * (1.0 + lax.erf(c * _SQRT_HALF))
        pdf = jnp.exp(-0.5 * c * c) * _INV_SQRT_2PI
        return dt, c * cdf, dt * hv * (cdf + c * pdf)

    def body(dz_ref, dzp_ref, dzn_ref, wd_ref, hv_ref, hvp_ref, hvn_ref, hg_ref, hgp_ref, hgn_ref, cw_ref, cb_ref,
             dhv_ref, dhg_ref, dcw_ref, dcb_ref):
        i = pl.program_id(1)

        @pl.when(i == 0)
        def _():
            dcw_ref[...] = jnp.zeros_like(dcw_ref)
            dcb_ref[...] = jnp.zeros_like(dcb_ref)

        wd = _row_cat(wd_ref)
        hg = hg_ref[...].astype(F32)
        hgp = hgp_ref[...].astype(F32)
        hgn = hgn_ref[...].astype(F32)
        first, last = i == 0, i == nt - 1
        e = HALO - 1
        back, fwd = _shift_rows(hg, jnp.where(first, 0.0, hgp[e:e + 1, :]), jnp.where(last, 0.0, hgn[0:1, :]))
        dt, act, dc = dc_of(dz_ref[...], wd, hv_ref[...].astype(F32), hg, back, fwd, cw_ref, cb_ref)
        dhv_ref[...] = (dt * act).astype(BF16)
        bp, fp = _shift_rows(hgp, hgp[0:1, :], hg[0:1, :])
        _, _, dcp = dc_of(dzp_ref[...], wd, hvp_ref[...].astype(F32), hgp, bp, fp, cw_ref, cb_ref)
        bn, fn = _shift_rows(hgn, hg[tm - 1:tm, :], hgn[e:e + 1, :])
        _, _, dcn = dc_of(dzn_ref[...], wd, hvn_ref[...].astype(F32), hgn, bn, fn, cw_ref, cb_ref)
        dc_back, dc_fwd = _shift_rows(dc, jnp.where(first, 0.0, dcp[e:e + 1, :]), jnp.where(last, 0.0, dcn[0:1, :]))
        dhg_ref[...] = (dc_fwd * cw_ref[0:1, :] + dc * cw_ref[1:2, :] + dc_back * cw_ref[2:3, :]).astype(BF16)
        dcw_ref[0:1, :] += _colsum(dc * back)
        dcw_ref[1:2, :] += _colsum(dc * hg)
        dcw_ref[2:3, :] += _colsum(dc * fwd)
        dcb_ref[...] += _colsum(dc)

    out_slab = pl.BlockSpec((None, tm, FF_BLK), lambda c, i: (c, i, 0))
    cw_spec = pl.BlockSpec((None, 3, FF_BLK), lambda c, i: (c, 0, 0))
    cb_spec = pl.BlockSpec((None, 1, FF_BLK), lambda c, i: (c, 0, 0))
    return pl.pallas_call(
        body, name=name, grid=(4, nt),
        in_specs=[pl.BlockSpec((tm, D), lambda c, i: (i, 0)),
                  pl.BlockSpec((HALO, D), lambda c, i: (before(i), 0)),
                  pl.BlockSpec((HALO, D), lambda c, i: (after(i), 0)),
                  _shards(2, FF_SHARD, D, li, lambda c, i: c),
                  hv_main, hv_prev, hv_next, hg_main, hg_prev, hg_next, cw_spec, cb_spec],
        out_specs=[out_slab, out_slab, cw_spec, cb_spec],
        out_shape=[jax.ShapeDtypeStruct((4, s, FF_BLK), BF16), jax.ShapeDtypeStruct((4, s, FF_BLK), BF16),
                   jax.ShapeDtypeStruct((4, 3, FF_BLK), F32), jax.ShapeDtypeStruct((4, 1, FF_BLK), F32)],
        compiler_params=_cparams(2))(dzb, dzb, dzb, wdown, up, up, up, up, up, up, conv_w, conv_b)


def dh1_ln1_bwd(dz2, dpg, wpg, dhv, dhg, wup, z1, g1, li, name, after=()):
    s = dz2.shape[0]
    tm = 256
    na = len(after)

    def body(dz2_ref, dpg_ref, wpg_ref, dhv_ref, dhg_ref, wup_ref, z1_ref, g_ref, *rest):
        dz_ref, dzb_ref, dg_ref, db_ref = rest[na:]

        @pl.when(pl.program_id(0) == 0)
        def _():
            dg_ref[...] = jnp.zeros_like(dg_ref)
            db_ref[...] = jnp.zeros_like(db_ref)
        dh = ALPHA * dz2_ref[...] + _nt(dpg_ref[...], _row_cat(wpg_ref))
        for c in range(4):
            dh = dh + _nn(dhv_ref[c], wup_ref[c]) + _nn(dhg_ref[c], wup_ref[4 + c])
        dz, dgx = _ln_bwd(dh, z1_ref[...], g_ref[...])
        dz_ref[...] = dz
        dzb_ref[...] = dz.astype(BF16)
        dg_ref[...] += _colsum(dgx)
        db_ref[...] += _colsum(dh)

    row = pl.BlockSpec((tm, D), lambda i: (i, 0))
    vec = pl.BlockSpec((1, D), lambda i: (0, 0))
    slab = pl.BlockSpec((4, tm, FF_BLK), lambda i: (0, i, 0))
    return pl.pallas_call(
        body, name=name, grid=(s // tm,),
        in_specs=[row, row, _shards(N_DEV, D // N_DEV, D, li), slab, slab, _shards(N_DEV, FF_BLK, D, li), row, vec]
        + [ANY] * na,
        out_specs=[row, row, vec, vec],
        out_shape=[jax.ShapeDtypeStruct((s, D), F32), jax.ShapeDtypeStruct((s, D), BF16),
                   jax.ShapeDtypeStruct((1, D), F32), jax.ShapeDtypeStruct((1, D), F32)],
        compiler_params=_cparams(1))(dz2, dpg, wpg, dhv, dhg, wup, z1, g1, *after)


def merge_bwd(dz1b, wmix, proj, ya, yp, li, name):
    s = dz1b.shape[0]
    tm, tn = 512, 512
    nt = D // tn
    per = tn // (D // N_DEV)
    ga0 = (3 * D_ATTN + D_POOL) // tn

    def body(dz_ref, w_ref, ga_ref, gb_ref, ya_ref, yp_ref, dya_ref, dyp_ref, dga_ref, dgb_ref):
        dm = _nt(dz_ref[...], _row_cat(w_ref))
        sa = _sigmoid(ga_ref[...].astype(F32))
        sb = _sigmoid(gb_ref[...].astype(F32))
        dya_ref[...] = (dm * sa).astype(BF16)
        dyp_ref[...] = (dm * sb).astype(BF16)
        dga_ref[...] = (dm * ya_ref[...].astype(F32) * sa * (1.0 - sa)).astype(BF16)
        dgb_ref[...] = (dm * yp_ref[...].astype(F32) * sb * (1.0 - sb)).astype(BF16)

    tile = pl.BlockSpec((tm, tn), lambda i, j: (i, j))
    return pl.pallas_call(
        body, name=name, grid=(s // tm, nt),
        in_specs=[pl.BlockSpec((tm, D), lambda i, j: (i, 0)),
                  _shards(per, D // N_DEV, D, li, lambda i, j: j),
                  pl.BlockSpec((tm, tn), lambda i, j: (i, ga0 + j)),
                  pl.BlockSpec((tm, tn), lambda i, j: (i, ga0 + nt + j)),
                  tile, tile],
        out_specs=[tile] * 4,
        out_shape=[jax.ShapeDtypeStruct((s, D), BF16)] * 4,
        compiler_params=_cparams(2))(dz1b, wmix, proj, proj, ya, yp)


def attn_out_bwd(dya, wao, li, name, after=()):
    s = dya.shape[0]
    tm = 512

    def body(d_ref, w_ref, *rest):
        rest[-1][...] = _nt(d_ref[...], _lane_cat(w_ref)).astype(BF16)

    return pl.pallas_call(
        body, name=name, grid=(s // tm,),
        in_specs=[pl.BlockSpec((tm, D), lambda i: (i, 0)), _shards(N_DEV, D_ATTN, 128, li)] + [ANY] * len(after),
        out_specs=pl.BlockSpec((tm, D_ATTN), lambda i: (i, 0)),
        out_shape=jax.ShapeDtypeStruct((s, D_ATTN), BF16),
        compiler_params=_cparams(1))(dya, wao, *after)


def pool_bwd(dyp, wpo, pm, pool_w, pool_scale, li, name):
    s = dyp.shape[0]

    def body(dyp_ref, wpo_ref, pm_ref, w_ref, sc_ref, du_ref, dw_ref, dsc_ref):
        wpo = _lane_cat(wpo_ref)
        dyp = dyp_ref[...]
        for g, w in enumerate(POOL_WINDOWS):
            cols = slice(g * PGD, (g + 1) * PGD)
            dpw = _nt(dyp, wpo[g * PGD:(g + 1) * PGD, :])
            pmg = pm_ref[:, cols]
            dsc_ref[:, cols] = _colsum(dpw * _nn(pmg, w_ref[g]))
            dpmw = (dpw * sc_ref[:, cols]).astype(BF16)
            dw_ref[g] = _tn(pmg, dpmw)
            dpm = _nt(dpmw, w_ref[g])
            du_ref[:, cols] = (_window_sum(dpm / _pool_counts(s, w), w, False) - dpm).astype(BF16)

    full = lambda shape: pl.BlockSpec(shape, lambda i: (0,) * len(shape))
    return pl.pallas_call(
        body, name=name, grid=(1,),
        in_specs=[full((s, D)), _shards(N_DEV, D_POOL, 128, li), full((s, D_POOL)), full((4, PGD, PGD)),
                  full((1, D_POOL))],
        out_specs=[full((s, D_POOL)), full((4, PGD, PGD)), full((1, D_POOL))],
        out_shape=[jax.ShapeDtypeStruct((s, D_POOL), BF16), jax.ShapeDtypeStruct((4, PGD, PGD), F32),
                   jax.ShapeDtypeStruct((1, D_POOL), F32)],
        compiler_params=_cparams(1))(dyp, wpo, pm, pool_w, pool_scale)


def attn_bwd(proj, da, e_rev, name, after=()):
    s = proj.shape[0]
    nb = s // QB
    skew = GRID_W + (GRID_W - KW)

    def body(q_ref, k_ref, v_ref, do_ref, e_ref, *rest):
        dq_ref, dk_ref, dv_ref, g_ref, s_ref, dp_ref, ds_ref, p_ref, dkt_acc, dvt_acc = rest[len(after):]
        b = pl.program_id(1)

        @pl.when(b == 0)
        def _():
            dkt_acc[...] = jnp.zeros_like(dkt_acc)
            dvt_acc[...] = jnp.zeros_like(dvt_acc)
            g_ref[...] = jnp.zeros_like(g_ref)

        ri = lax.broadcasted_iota(I32, (QB, QB), 0)
        ci = lax.broadcasted_iota(I32, (QB, QB), 1)
        rev = jnp.where(ri + ci == QB - 1, 1.0, 0.0).astype(BF16)
        q = _nn(rev, q_ref[...]).astype(BF16) * ATT_SCALE
        do = _nn(rev, do_ref[...]).astype(BF16)
        lane = lax.broadcasted_iota(I32, (1, 128), 1)

        def block(btype, k0):
            kwin = k_ref[pl.ds(k0, KB), :]
            vwin = v_ref[pl.ds(k0, KB), :]
            dq = jnp.zeros((QB, 128), F32)
            for hh in range(2):
                lm = (lane // HEAD_DIM) == hh
                qh = jnp.where(lm, q, jnp.zeros_like(q))
                doh = jnp.where(lm, do, jnp.zeros_like(do))
                kh = jnp.where(lm, kwin, jnp.zeros_like(kwin))
                s_ref[...] = _nt(qh, kwin)
                dp_ref[...] = _nt(doh, vwin)
                g = jnp.zeros((1, KB), F32)
                for ib in range(QROWS):
                    qr = QROWS - 1 - ib
                    rows = slice(ib * GRID_W, (ib + 1) * GRID_W)
                    sb, a0, w, pad = _row_logits(s_ref, e_ref, hh, rows, btype, qr)
                    p = jnp.exp(sb - jnp.max(sb, axis=1, keepdims=True))
                    p = p * (1.0 / jnp.sum(p, axis=1, keepdims=True))
                    dp = dp_ref[rows, a0:a0 + w]
                    ds = p * (dp - jnp.sum(p * dp, axis=1, keepdims=True))
                    _store_row(ds_ref, rows, a0, w, ds)
                    _store_row(p_ref, rows, a0, w, p)
                    t = jnp.sum(pltpu.roll(ds, w - skew, 1, stride=1, stride_axis=0), axis=0, keepdims=True)
                    t = t[:, :KH * GRID_W] if pad else pltpu.roll(t, GRID_W, 1)
                    i0 = _attn_row(btype, qr)[2]
                    g = g + pltpu.roll(jnp.concatenate([t, jnp.zeros_like(t)], axis=1), i0 * GRID_W, 1)
                g_ref[hh] += g
                dsb = ds_ref[...]
                dq = dq + _nn(dsb, kh) * ATT_SCALE
                dkt_acc[:, pl.ds(k0, KB)] += _tn(qh, dsb)
                dvt_acc[:, pl.ds(k0, KB)] += _tn(doh, p_ref[...])
            dq_ref[...] = _nn(rev, dq.astype(BF16)).astype(BF16)

        for btype, (cond, k0) in enumerate(_attn_types(b, nb)):
            pl.when(cond)(lambda btype=btype, k0=k0: block(btype, k0))

        @pl.when(b == nb - 1)
        def _():
            dk_ref[...] = dkt_acc[...].T.astype(BF16)
            dv_ref[...] = dvt_acc[...].T.astype(BF16)

    col = pl.BlockSpec((s, 128), lambda j, b: (0, j))
    return pl.pallas_call(
        body, name=name, grid=(4, nb),
        in_specs=[pl.BlockSpec((QB, 128), lambda j, b: (b, j)),
                  pl.BlockSpec((s, 128), lambda j, b: (0, 4 + j)),
                  pl.BlockSpec((s, 128), lambda j, b: (0, 8 + j)),
                  pl.BlockSpec((QB, 128), lambda j, b: (b, j)),
                  pl.BlockSpec((2, 2, GRID_W, KB), lambda j, b: (j, 0, 0, 0))] + [ANY] * len(after),
        out_specs=[pl.BlockSpec((QB, 128), lambda j, b: (b, j)), col, col,
                   pl.BlockSpec((2, 1, KB), lambda j, b: (j, 0, 0))],
        out_shape=[jax.ShapeDtypeStruct((s, D_ATTN), BF16)] * 3 + [jax.ShapeDtypeStruct((N_HEADS, 1, KB), F32)],
        scratch_shapes=[pltpu.VMEM((QB, KB), F32), pltpu.VMEM((QB, KB), F32), pltpu.VMEM((QB, KB), BF16),
                        pltpu.VMEM((QB, KB), BF16), pltpu.VMEM((128, s), F32), pltpu.VMEM((128, s), F32)],
        compiler_params=_cparams(2))(proj, proj, proj, da, e_rev, *after)


def dh0_bwd(dz1, dproj, win, li, name, after=()):
    s = dz1.shape[0]
    tm = 256
    bn = N_PROJ // N_DEV

    def body(dz_ref, dp_ref, w_ref, *rest):
        acc = ALPHA * dz_ref[...]
        for j in range(N_DEV):
            acc = acc + _nt(dp_ref[:, j * bn:(j + 1) * bn], w_ref[j])
        rest[-1][...] = acc

    row = pl.BlockSpec((tm, D), lambda i: (i, 0))
    return pl.pallas_call(
        body, name=name, grid=(s // tm,),
        in_specs=[row, pl.BlockSpec((tm, N_PROJ), lambda i: (i, 0)), _shards(N_DEV, D, bn, li)] + [ANY] * len(after),
        out_specs=row, out_shape=jax.ShapeDtypeStruct((s, D), F32),
        compiler_params=_cparams(1))(dz1, dproj, win, *after)


def _coords():
    return lax.axis_index("x"), lax.axis_index("y"), lax.axis_index("c")


def _dev_index(px, py, pc):
    return 4 * px + 2 * py + pc


def all_gather(arrs, name):
    n = len(arrs)

    def body(*refs):
        ins, outs = refs[:n], refs[n:2 * n]
        send_sems, recv_sems, local_sems = refs[2 * n:]
        x, y, c = _coords()
        me, sibling = (x, y, c), (x, y, 1 - c)
        chips = [(1 - x, y), (x, 1 - y), (1 - x, 1 - y)]

        def copy(a, k, block, to, src=None):
            dst = outs[a].at[_dev_index(*block)]
            return pltpu.make_async_remote_copy(
                src_ref=dst if src is None else src, dst_ref=dst,
                send_sem=send_sems.at[a, k], recv_sem=recv_sems.at[a, k], device_id=to, device_id_type=MESH)

        mine = [pltpu.make_async_copy(ins[a], outs[a].at[_dev_index(*me)], local_sems.at[a]) for a in range(n)]
        for cp in mine:
            cp.start()
        first = []
        for a in range(n):
            first.append(copy(a, 0, me, sibling, src=ins[a]))
            first += [copy(a, 1 + j, me, (*chip, c), src=ins[a]) for j, chip in enumerate(chips)]
        for cp in first:
            cp.start()
        passed = []
        for j, chip in enumerate(chips):
            for a in range(n):
                copy(a, 1 + j, (*chip, c), me).wait_recv()
                cp = copy(a, 4 + j, (*chip, c), sibling)
                cp.start()
                passed.append(cp)
        for a in range(n):
            copy(a, 0, sibling, me).wait_recv()
            for j, chip in enumerate(chips):
                copy(a, 4 + j, (*chip, 1 - c), me).wait_recv()
        for cp in first + passed:
            cp.wait_send()
        for cp in mine:
            cp.wait()

    return pl.pallas_call(
        body, name=name,
        out_shape=[jax.ShapeDtypeStruct((N_DEV,) + a.shape, a.dtype) for a in arrs],
        in_specs=[ANY] * n, out_specs=[ANY] * n,
        scratch_shapes=[pltpu.SemaphoreType.DMA((n, 7)), pltpu.SemaphoreType.DMA((n, 7)),
                        pltpu.SemaphoreType.DMA((n,))],
    )(*arrs)


HBM = pl.BlockSpec(memory_space=pltpu.HBM)
SEM = pl.BlockSpec(memory_space=pltpu.SEMAPHORE)
_EFFECT = pltpu.SideEffectType.DATAFLOW_SIDE_EFFECTING
_TOKEN = jax.ShapeDtypeStruct((8, 128), F32)


def _in_hbm(a):
    return pltpu.with_memory_space_constraint(a, pltpu.HBM)


def _hbm_like(a):
    return pltpu.HBM(a.shape, a.dtype)


def _peers(x, y, c):
    return [(x, y, 1 - c), (1 - x, y, c), (x, 1 - y, c), (1 - x, 1 - y, c)]


def ag_start(lands, after, name):
    n = len(lands)

    def body(*refs):
        land = refs[:n]
        send_sem, recv_sem, token = refs[n + 1], refs[n + 2], refs[-1]
        x, y, c = _coords()
        me = _dev_index(x, y, c)
        for k, peer in enumerate(_peers(x, y, c)):
            for a in range(n):
                pltpu.make_async_remote_copy(src_ref=land[a].at[me], dst_ref=land[a].at[me], send_sem=send_sem.at[k],
                                             recv_sem=recv_sem.at[k], device_id=peer, device_id_type=MESH).start()
        token[...] = jnp.zeros_like(token)

    res = pl.pallas_call(
        body, name=name,
        out_shape=(pltpu.SemaphoreType.DMA((4,)), pltpu.SemaphoreType.DMA((4,)), *[_hbm_like(l) for l in lands], _TOKEN),
        in_specs=[HBM] * n + [ANY], out_specs=(SEM, SEM, *[HBM] * n, pl.BlockSpec(memory_space=pltpu.VMEM)),
        input_output_aliases={a: 2 + a for a in range(n)},
        compiler_params=pltpu.CompilerParams(has_side_effects=_EFFECT),
    )(*[_in_hbm(l) for l in lands], after)
    return res[0], res[1], list(res[2:2 + n]), res[-1]


def ag_forward(send_sem, recv_sem, lands, after, name):
    n = len(lands)

    def body(*refs):
        send_sem, recv_sem = refs[0], refs[1]
        land = refs[2:2 + n]
        fsend, frecv = refs[3 + n], refs[4 + n]
        x, y, c = _coords()
        peers = _peers(x, y, c)
        for k in range(1, 4):
            blk = _dev_index(*peers[k])
            for a in range(n):
                pltpu.make_async_remote_copy(src_ref=land[a].at[blk], dst_ref=land[a].at[blk], send_sem=send_sem.at[k],
                                             recv_sem=recv_sem.at[k], device_id=peers[k], device_id_type=MESH).wait_recv()
        for k in range(1, 4):
            blk = _dev_index(*peers[k])
            for a in range(n):
                pltpu.make_async_remote_copy(src_ref=land[a].at[blk], dst_ref=land[a].at[blk], send_sem=fsend.at[k - 1],
                                             recv_sem=frecv.at[k - 1], device_id=peers[0], device_id_type=MESH).start()

    res = pl.pallas_call(
        body, name=name,
        out_shape=(pltpu.SemaphoreType.DMA((3,)), pltpu.SemaphoreType.DMA((3,)), *[_hbm_like(l) for l in lands]),
        in_specs=[SEM, SEM, *[HBM] * n, ANY], out_specs=(SEM, SEM, *[HBM] * n),
        input_output_aliases={2 + a: 2 + a for a in range(n)},
        compiler_params=pltpu.CompilerParams(has_side_effects=_EFFECT),
    )(send_sem, recv_sem, *lands, after)
    return res[0], res[1], list(res[2:])


def ag_finish(send_sem, recv_sem, fsend, frecv, lands, after, name):
    n = len(lands)

    def body(*refs):
        send_sem, recv_sem, fsend, frecv = refs[:4]
        land = refs[4:4 + n]
        x, y, c = _coords()
        me = _dev_index(x, y, c)
        peers = _peers(x, y, c)
        for k in range(4):
            for a in range(n):
                pltpu.make_async_remote_copy(src_ref=land[a].at[me], dst_ref=land[a].at[me], send_sem=send_sem.at[k],
                                             recv_sem=recv_sem.at[k], device_id=peers[k], device_id_type=MESH).wait_send()
        sib = _dev_index(*peers[0])
        for a in range(n):
            pltpu.make_async_remote_copy(src_ref=land[a].at[sib], dst_ref=land[a].at[sib], send_sem=send_sem.at[0],
                                         recv_sem=recv_sem.at[0], device_id=peers[0], device_id_type=MESH).wait_recv()
        for k in range(1, 4):
            mine = _dev_index(*peers[k])
            theirs = _dev_index(peers[k][0], peers[k][1], 1 - c)
            for a in range(n):
                pltpu.make_async_remote_copy(src_ref=land[a].at[mine], dst_ref=land[a].at[theirs], send_sem=fsend.at[k - 1],
                                             recv_sem=frecv.at[k - 1], device_id=peers[0], device_id_type=MESH).wait()

    res = pl.pallas_call(
        body, name=name, out_shape=tuple(_hbm_like(l) for l in lands),
        in_specs=[SEM] * 4 + [HBM] * n + [ANY], out_specs=tuple([HBM] * n),
        input_output_aliases={4 + a: a for a in range(n)},
        compiler_params=pltpu.CompilerParams(has_side_effects=_EFFECT),
    )(send_sem, recv_sem, fsend, frecv, *lands, after)
    return list(res)


def _others(x, y, c):
    return [(x ^ (k & 1), y ^ ((k >> 1) & 1), c ^ (k >> 2)) for k in range(1, N_DEV)]


def rs_start(parts, name):
    n = len(parts)
    lands = [lax.empty((N_DEV - 1,) + p.shape[1:], p.dtype) for p in parts]

    def body(*refs):
        src, land = refs[:n], refs[n:2 * n]
        send_sem, recv_sem, token = refs[2 * n], refs[2 * n + 1], refs[-1]
        for k, peer in enumerate(_others(*_coords())):
            for a in range(n):
                pltpu.make_async_remote_copy(src_ref=src[a].at[_dev_index(*peer)], dst_ref=land[a].at[k],
                                             send_sem=send_sem.at[k], recv_sem=recv_sem.at[k], device_id=peer,
                                             device_id_type=MESH).start()
        token[...] = jnp.zeros_like(token)

    res = pl.pallas_call(
        body, name=name,
        out_shape=(pltpu.SemaphoreType.DMA((N_DEV - 1,)), pltpu.SemaphoreType.DMA((N_DEV - 1,)),
                   *[_hbm_like(p) for p in parts], *[_hbm_like(l) for l in lands], _TOKEN),
        in_specs=[HBM] * (2 * n), out_specs=(SEM, SEM, *[HBM] * (2 * n), pl.BlockSpec(memory_space=pltpu.VMEM)),
        input_output_aliases={a: 2 + a for a in range(2 * n)},
        compiler_params=pltpu.CompilerParams(has_side_effects=_EFFECT),
    )(*[_in_hbm(p) for p in parts], *[_in_hbm(l) for l in lands])
    return res[0], res[1], list(res[2:2 + n]), list(res[2 + n:2 + 2 * n]), res[-1]


def rs_finish(send_sem, recv_sem, parts, lands, after, name):
    n = len(parts)

    def body(*refs):
        send_sem, recv_sem = refs[0], refs[1]
        src, land = refs[2:2 + n], refs[2 + n:2 + 2 * n]
        for k, peer in enumerate(_others(*_coords())):
            for a in range(n):
                pltpu.make_async_remote_copy(src_ref=src[a].at[_dev_index(*peer)], dst_ref=land[a].at[k],
                                             send_sem=send_sem.at[k], recv_sem=recv_sem.at[k], device_id=peer,
                                             device_id_type=MESH).wait()

    res = pl.pallas_call(
        body, name=name, out_shape=tuple(_hbm_like(t) for t in list(parts) + list(lands)),
        in_specs=[SEM, SEM] + [HBM] * (2 * n) + [ANY], out_specs=tuple([HBM] * (2 * n)),
        input_output_aliases={2 + a: a for a in range(2 * n)},
        compiler_params=pltpu.CompilerParams(has_side_effects=_EFFECT),
    )(send_sem, recv_sem, *parts, *lands, after)
    return list(res[:n]), list(res[n:])


def _row_tile(r):
    return next(t for t in (512, 352, 256, 128) if r % t == 0)


def _adamw(w, g, m, v):
    m = ADAM_B1 * m + (1.0 - ADAM_B1) * g
    v = ADAM_B2 * v + (1.0 - ADAM_B2) * (g * g)
    m_hat = m / (1.0 - ADAM_B1 ** ADAM_STEP)
    v_hat = v / (1.0 - ADAM_B2 ** ADAM_STEP)
    delta = -ADAM_LR * (m_hat / (jnp.sqrt(v_hat) + ADAM_EPS) + ADAM_WD * w)
    return delta, m, v


def adamw_shard(me, part, recv, w, m, v, li, prev, name):
    _, r, c = part.shape
    tr = _row_tile(r)

    def body(me_ref, own_ref, recv_ref, w_ref, m_ref, v_ref, p0, p1, p2, p3, g_ref, d_ref, nm_ref, nv_ref):
        g = own_ref[...].astype(F32)
        for k in range(N_DEV - 1):
            g = g + recv_ref[k].astype(F32)
        delta, nm, nv = _adamw(w_ref[...], g, m_ref[...], v_ref[...])
        g_ref[...] = g
        d_ref[...] = delta
        nm_ref[...] = nm
        nv_ref[...] = nv

    lay = pl.BlockSpec((None, tr, c), lambda t, me: (li, t, 0))
    stack = jax.ShapeDtypeStruct((DEPTH, r, c), F32)
    grid_spec = pltpu.PrefetchScalarGridSpec(
        num_scalar_prefetch=1, grid=(r // tr,),
        in_specs=[pl.BlockSpec((None, tr, c), lambda t, me: (me[0], t, 0)),
                  pl.BlockSpec((N_DEV - 1, tr, c), lambda t, me: (0, t, 0)), lay, lay, lay, ANY, ANY, ANY, ANY],
        out_specs=[lay] * 4)
    return pl.pallas_call(
        body, name=name, grid_spec=grid_spec, out_shape=[stack] * 4,
        input_output_aliases={6: 0, 7: 1, 8: 2, 9: 3},
        compiler_params=_cparams(1))(me, part, recv, w, m, v, *prev)


def adamw_replicated(gathered, w, m, v, name):
    _, r, c = gathered.shape
    tr = next(t for t in (96, 88, 64, _PACK_TILE) if r % t == 0)

    def body(gs_ref, w_ref, m_ref, v_ref, g_ref, d_ref, nm_ref, nv_ref):
        g = gs_ref[0]
        for d in range(1, N_DEV):
            g = g + gs_ref[d]
        delta, nm, nv = _adamw(w_ref[...], g, m_ref[...], v_ref[...])
        g_ref[...] = g
        d_ref[...] = delta
        nm_ref[...] = nm
        nv_ref[...] = nv

    row = pl.BlockSpec((tr, c), lambda t: (t, 0))
    return pl.pallas_call(
        body, name=name, grid=(r // tr,),
        in_specs=[pl.BlockSpec((N_DEV, tr, c), lambda t: (0, t, 0)), row, row, row],
        out_specs=[row] * 4, out_shape=[jax.ShapeDtypeStruct((r, c), F32)] * 4,
        compiler_params=_cparams(1))(gathered, w, m, v)


def adamw_plain(g, w, m, v, name):
    def body(g_ref, w_ref, m_ref, v_ref, d_ref, nm_ref, nv_ref):
        delta, nm, nv = _adamw(w_ref[...], g_ref[...], m_ref[...], v_ref[...])
        d_ref[...] = delta
        nm_ref[...] = nm
        nv_ref[...] = nv

    return pl.pallas_call(body, name=name, out_shape=[jax.ShapeDtypeStruct(w.shape, F32)] * 3)(g, w, m, v)


_PACK_LAYER = (("b_in", (N_PROJ,)), ("rpb", (N_HEADS, 2 * KH - 1, 2 * KW - 1)), ("pool_w", (4, PGD, PGD)),
               ("pool_scale", (D_POOL,)), ("ln1_g", (D,)), ("ln1_b", (D,)), ("conv_b", (D_FF,)), ("ln2_g", (D,)),
               ("ln2_b", (D,)), ("conv_w", (3, D_FF)))
_PACK_INPUT = (("ln_in_g", (D,)), ("ln_in_b", (D,)))
_PACK_LANES = 1024
_PACK_TILE = 8
_EARLY = tuple(range(1, DEPTH))


def _pack_items(layers):
    items = [(n, (len(layers),) + s) for n, s in _PACK_LAYER]
    return items + ([(n, s) for n, s in _PACK_INPUT] if 0 in layers else [])


def _pack_rows(shape):
    return -(-int(np.prod(shape)) // _PACK_LANES)


def _pack(parts, layers):
    rows, used = [], 0
    for name, shape in _pack_items(layers):
        arr = parts[name] if (name, shape) in _PACK_INPUT else jnp.stack([parts[name][li] for li in layers])
        flat = arr.reshape(-1).astype(F32)
        rows.append(jnp.pad(flat, (0, _pack_rows(shape) * _PACK_LANES - flat.shape[0])))
        used += _pack_rows(shape)
    total = -(-used // _PACK_TILE) * _PACK_TILE
    rows.append(jnp.zeros(((total - used) * _PACK_LANES,), F32))
    return jnp.concatenate(rows).reshape(total, _PACK_LANES)


def _unpack(packed, layers):
    out, r0 = {}, 0
    for name, shape in _pack_items(layers):
        n, nr = int(np.prod(shape)), _pack_rows(shape)
        out[name] = packed[r0:r0 + nr].reshape(-1)[:n].reshape(shape)
        r0 += nr
    return out


def _bias_tables(rpb_l):
    qc = np.arange(GRID_W)[:, None]
    kc = np.arange(GRID_W)[None, :]
    start = np.clip(qc - KW // 2, 0, GRID_W - KW)
    valid = (kc >= start) & (kc < start + KW)
    col = np.clip(kc - qc, -(KW - 1), KW - 1) + KW - 1
    onehot = (col.reshape(-1)[None, :] == np.arange(2 * KW - 1)[:, None]).astype(np.float32)
    rows = jnp.pad(rpb_l, ((0, 0), (0, 1), (0, 0)))
    tab = jnp.einsum("hij,jm->him", rows, jnp.asarray(onehot), precision=lax.Precision.HIGHEST)
    tab = tab.reshape(N_HEADS, KROWS, GRID_W, GRID_W).transpose(0, 2, 1, 3)
    ok = valid[None, :, None, :] & (np.arange(KROWS) < 2 * KH - 1)[None, None, :, None]
    tab = jnp.where(jnp.asarray(ok), tab, NEG_INF).reshape(N_HEADS, GRID_W, KB)
    tab = jnp.stack([tab, jnp.roll(tab, GRID_W, axis=-1)], axis=1)
    return tab, tab[:, :, ::-1, :]


_SHARDED = ("w_in", "w_attn_out", "w_pool_out", "w_mix_out", "w_up", "w_down", "w_ple_gate", "w_ple_proj")
_NAMES = ("ln_in_g", "ln_in_b", "w_in", "b_in", "rpb", "w_attn_out", "pool_w", "pool_scale", "w_pool_out", "w_mix_out",
          "ln1_g", "ln1_b", "w_up", "conv_w", "conv_b", "w_down", "w_ple_gate", "w_ple_proj", "ln2_g", "ln2_b")


def kernel(x, p, ln_in_g, ln_in_b, w_in, b_in, rpb, w_attn_out, pool_w, pool_scale, w_pool_out, w_mix_out, ln1_g, ln1_b, w_up, conv_w, conv_b, w_down, w_ple_gate, w_ple_proj, ln2_g, ln2_b, loss_target, m_ln_in_g, m_ln_in_b, m_w_in, m_b_in, m_rpb, m_w_attn_out, m_pool_w, m_pool_scale, m_w_pool_out, m_w_mix_out, m_ln1_g, m_ln1_b, m_w_up, m_conv_w, m_conv_b, m_w_down, m_w_ple_gate, m_w_ple_proj, m_ln2_g, m_ln2_b, v_ln_in_g, v_ln_in_b, v_w_in, v_b_in, v_rpb, v_w_attn_out, v_pool_w, v_pool_scale, v_w_pool_out, v_w_mix_out, v_ln1_g, v_ln1_b, v_w_up, v_conv_w, v_conv_b, v_w_down, v_w_ple_gate, v_w_ple_proj, v_ln2_g, v_ln2_b):
    a = dict(locals())
    W = {n: a[n] for n in _NAMES}
    M = {n: a["m_" + n] for n in _NAMES}
    V = {n: a["v_" + n] for n in _NAMES}
    xi, yi, ci = _coords()
    me = _dev_index(xi, yi, ci)
    x2, tgt = x[0], loss_target[0]
    pb = p[:, 0].astype(BF16)

    flip = lambda d: {**d, "w_up": d["w_up"].transpose(0, 2, 1)}
    ex = _Exchange(flip(W), flip(M), flip(V))
    loss_part, dx, parts = _local_step(x2, tgt, pb, W, ex)
    loss = lax.psum(loss_part[0, 0], AXES)

    started = ex.replicated_start("late", _pack(parts, (0,)), dx)
    done = ex.update(range(DEPTH - 1, 0, -1), started)
    ex.replicated_forward("late", done)
    done = ex.update((0,), done)
    stacks = {**ex.stacks, "w_up": [t.transpose(0, 2, 1) for t in ex.stacks["w_up"]]}
    zero_cw = jnp.zeros((DEPTH, 3, D_FF), F32)
    halves = []
    for layers, tag in (((0,), "late"), (_EARLY, "early")):
        gath = ex.replicated_finish(tag, done)
        packs = [_pack({**src, "conv_w": zero_cw}, layers) for src in (W, M, V)]
        halves.append([_unpack(o, layers) for o in adamw_replicated(gath, *packs, f"adamw_replicated_{tag}")])
    outs = [{**{n: jnp.concatenate([lo[n], hi[n]]) for n, _ in _PACK_LAYER}, **{n: lo[n] for n, _ in _PACK_INPUT}}
            for lo, hi in zip(*halves)]
    g_cw = lax.dynamic_slice_in_dim(outs[0]["conv_w"], me * FF_SHARD, FF_SHARD, axis=2)
    flat = lambda t: t.reshape(DEPTH * 3, FF_SHARD)
    cw_out = [o.reshape(DEPTH, 3, FF_SHARD) for o in
              adamw_plain(flat(g_cw), flat(conv_w), flat(m_conv_w), flat(v_conv_w), "adamw_conv_w")]
    res = []
    for k in range(4):
        d = {n: stacks[n][k] for n in _SHARDED}
        d.update({n: outs[k][n] for n in outs[k] if n != "conv_w"})
        d["conv_w"] = g_cw if k == 0 else cw_out[k - 1]
        res.append(d)
    return (loss, dx[None], *[res[k][n] for k in range(4) for n in _NAMES])


class _Exchange:
    GROUPS = (("w_ple_gate", "w_ple_proj", "w_down", "w_up"), ("w_mix_out", "w_attn_out", "w_pool_out"), ("w_in",))
    FIRST = ("w_in",)

    def __init__(self, W, M, V):
        self.W, self.M, self.V = W, M, V
        xi, yi, ci = _coords()
        me = _dev_index(xi, yi, ci)
        self.me = me.astype(I32).reshape(1)
        self.lands = [{n: lax.dynamic_update_index_in_dim(lax.empty((N_DEV,) + W[n].shape[1:], BF16),
                                                          W[n][li].astype(BF16), me, 0) for n in _SHARDED}
                      for li in range(DEPTH)]
        (cw,) = all_gather([W["conv_w"]], "ag_conv_w")
        self.cw = cw.transpose(1, 2, 0, 3).reshape(DEPTH, 3, 4, FF_BLK).transpose(0, 2, 1, 3)
        self.ag, self.fwd, self.rs, self.pending, self.small = {}, {}, {}, {}, {}
        self.stacks = {n: [lax.empty((DEPTH,) + W[n].shape[1:], F32) for _ in range(4)] for n in _SHARDED}
        self.late = tuple(n for n in _SHARDED if n not in self.FIRST)
        self.ag[0] = ag_start([self.lands[0][n] for n in self.FIRST], cw, "ag_start0")

    def tokens(self):
        return [self.ag[0][3]]

    def prefetch(self, li, after):
        send, recv, lands, _ = self.ag[li]
        self.fwd[li] = ag_forward(send, recv, lands, after, f"ag_forward{li}")
        if li == 0:
            self.ag["0b"] = ag_start([self.lands[0][n] for n in self.late], self.fwd[0][2][0], "ag_start0b")

    def weights(self, li, after):
        send, recv, _, _ = self.ag.pop(li)
        fsend, frecv, lands = self.fwd.pop(li)
        lands = ag_finish(send, recv, fsend, frecv, lands, after, f"ag_finish{li}")
        if li == 0:
            return dict(zip(self.FIRST, lands)), self.cw[li], (self.ag["0b"][3],)
        tokens = ()
        if li + 1 < DEPTH:
            self.ag[li + 1] = ag_start([self.lands[li + 1][n] for n in _SHARDED], lands[0], f"ag_start{li + 1}")
            tokens = (self.ag[li + 1][3],)
        return dict(zip(_SHARDED, lands)), self.cw[li], tokens

    def rest(self, li, G, mid, after):
        if li != 0:
            return G, ()
        send, recv, lands, _ = self.ag.pop("0b")
        fsend, frecv, lands = ag_forward(send, recv, lands, mid, "ag_forward0b")
        lands = ag_finish(send, recv, fsend, frecv, lands, after, "ag_finish0b")
        self.ag[1] = ag_start([self.lands[1][n] for n in _SHARDED], lands[0], "ag_start1")
        return {**G, **dict(zip(self.late, lands))}, (self.ag[1][3],)

    def grads(self, li, group, gw):
        self.pending.setdefault(li, {}).update(gw)
        if li != 0 and group != len(self.GROUPS) - 1:
            return None
        gw = self.pending.pop(li)
        tag = f"{li}_{group}" if li == 0 else f"{li}"
        send, recv, parts, lands, token = rs_start(list(gw.values()), f"rs_start{tag}")
        self.rs.setdefault(li, []).append((tag, tuple(gw), send, recv, parts, lands))
        if li == 0 and group == 1 and "early" in self.small:
            self.replicated_forward("early", token)
        return token

    def update(self, layers, after):
        for li in layers:
            for tag, names, send, recv, parts, lands in self.rs.pop(li):
                parts, lands = rs_finish(send, recv, parts, lands, after, f"rs_finish{tag}")
                for n, part, land in zip(names, parts, lands):
                    self.stacks[n] = adamw_shard(self.me, part, land, self.W[n], self.M[n], self.V[n], li,
                                                 self.stacks[n], f"adamw_{n}{li}")
                    after = self.stacks[n][0]
        return after

    def replicated_start(self, tag, pack, after):
        land = lax.dynamic_update_index_in_dim(lax.empty((N_DEV,) + pack.shape, F32), pack, self.me[0], 0)
        self.small[tag] = ag_start([land], after, f"ag_start_small_{tag}")
        return self.small[tag][3]

    def replicated_early(self, small, after):
        return self.replicated_start("early", _pack(small, _EARLY), after)

    def replicated_forward(self, tag, after):
        send, recv, lands, _ = self.small[tag]
        self.small[tag] = (send, recv) + ag_forward(send, recv, lands, after, f"ag_forward_small_{tag}")

    def replicated_finish(self, tag, after):
        send, recv, fsend, frecv, lands = self.small.pop(tag)
        return ag_finish(send, recv, fsend, frecv, lands, after, f"ag_finish_small_{tag}")[0]


def _local_step(x2, tgt, pb, W, ex):
    depth = W["rpb"].shape[0]
    vec = lambda t: t.reshape(1, -1)
    ln1_g, ln1_b, ln2_g, ln2_b = W["ln1_g"], W["ln1_b"], W["ln2_g"], W["ln2_b"]
    b_in, rpb, pool_scale = W["b_in"], W["rpb"], W["pool_scale"]
    cb_full = W["conv_b"].reshape(depth, 4, 1, FF_BLK)
    pool_w_b = W["pool_w"].astype(BF16)

    h, hb = ln_fwd(x2, vec(W["ln_in_g"]), vec(W["ln_in_b"]), "ln_in", after=ex.tokens())
    ex.prefetch(0, hb)
    saved = []
    for li in range(depth):
        G, cw, tokens = ex.weights(li, hb)
        e_tab, e_rev = _bias_tables(rpb[li])
        bias = vec(b_in[li])
        proj = proj_fwd(hb, G["w_in"], bias, li, 0, N_DEV, BF16, f"proj{li}", after=tokens)
        u = proj_fwd(hb, G["w_in"], bias, li, 3, 1, F32, f"proj_u{li}")
        att = attn_fwd(proj, e_tab, f"attn{li}")
        pm, pw = pool_fwd(u, pool_w_b[li], vec(pool_scale[li]), f"pool{li}")
        G, tokens = ex.rest(li, G, att, pw)
        mg, ya, yp = merge_fwd(att, pw, G["w_attn_out"], G["w_pool_out"], proj, li, f"merge{li}", after=tokens)
        if li + 1 < depth:
            ex.prefetch(li + 1, mg)
        z1, h1, h1b = mix_ln_fwd(mg, G["w_mix_out"], h, vec(ln1_g[li]), vec(ln1_b[li]), li, f"mix_ln{li}")
        up = up_fwd(h1b, G["w_up"], li, f"up{li}")
        t = ffn_act_fwd(up, cw, cb_full[li], f"ffn_act{li}")
        z2, h2, h2b, pg, pp = down_ple_ln_fwd(t, G["w_down"], h1b, G["w_ple_gate"], pb[li], G["w_ple_proj"], h1,
                                              vec(ln2_g[li]), vec(ln2_b[li]), li, f"down_ln{li}")
        saved.append(dict(hb=hb, proj=proj, att=att, pm=pm, pw=pw, mg=mg, ya=ya, yp=yp, z1=z1, h1b=h1b, up=up, t=t,
                          z2=z2, pg=pg, pp=pp, e_rev=e_rev, G=G, cw=cw))
        h, hb = h2, h2b

    dh, loss_part = loss_bwd(h, tgt, "loss")
    small = {n: [None] * depth for n in ("b_in", "rpb", "pool_w", "pool_scale", "ln1_g", "ln1_b", "conv_b", "ln2_g",
                                         "ln2_b", "conv_w")}
    token = ()
    tok = lambda t: () if t is None else (t,)
    for li in reversed(range(depth)):
        sv = saved[li]
        G, cw = sv["G"], sv["cw"]
        dz2, dz2b, dpg, dpp, dg2, db2 = ln2_ple_bwd(dh, sv["z2"], vec(ln2_g[li]), sv["pg"], sv["pp"], f"ln2_bwd{li}",
                                                    after=token)
        gw = {}
        gw["w_ple_gate"] = wgrad_rows(sv["h1b"], dpg, f"dw_pg{li}")
        gw["w_ple_proj"] = wgrad_cols(pb[li], dpp, f"dw_pp{li}")
        gw["w_down"] = wgrad_down(sv["t"], dz2b, f"dw_down{li}").reshape(N_DEV, FF_SHARD, D)
        dhv, dhg, dcw, dcb = ffn_act_bwd(dz2b, G["w_down"], sv["up"], cw, cb_full[li], li, f"ffn_bwd{li}")
        gw["w_up"] = wgrad_up(sv["h1b"], dhv, dhg, f"dw_up{li}")
        token = tok(ex.grads(li, 0, gw))
        dz1, dz1b, dg1, db1 = dh1_ln1_bwd(dz2, dpg, G["w_ple_gate"], dhv, dhg, G["w_up"], sv["z1"], vec(ln1_g[li]), li,
                                          f"ln1_bwd{li}", after=token)
        gw = {"w_mix_out": wgrad_rows(sv["mg"], dz1b, f"dw_mix{li}")}
        dya, dyp, dga, dgb = merge_bwd(dz1b, G["w_mix_out"], sv["proj"], sv["ya"], sv["yp"], li, f"merge_bwd{li}")
        gw["w_attn_out"] = wgrad_cols(sv["att"], dya, f"dw_ao{li}")
        gw["w_pool_out"] = wgrad_cols(sv["pw"], dyp, f"dw_po{li}")
        token = tok(ex.grads(li, 1, gw))
        da = attn_out_bwd(dya, G["w_attn_out"], li, f"da{li}", after=token)
        du, dpool_w, dpool_sc = pool_bwd(dyp, G["w_pool_out"], sv["pm"], pool_w_b[li], vec(pool_scale[li]), li,
                                         f"pool_bwd{li}")
        dq, dk, dv, drpb = attn_bwd(sv["proj"], da, sv["e_rev"], f"attn_bwd{li}")
        dproj = jnp.concatenate([dq, dk, dv, du, dga, dgb], axis=1)
        dw_in, db_in = wgrad_cols(sv["hb"], dproj, f"dw_in{li}", with_colsum=True)
        token = tok(ex.grads(li, 2, {"w_in": dw_in}))
        dh = dh0_bwd(dz1, dproj, G["w_in"], li, f"dh0{li}", after=token)
        small["b_in"][li] = db_in.reshape(N_PROJ)
        small["rpb"][li] = drpb.reshape(N_HEADS, KROWS, GRID_W)[:, :2 * KH - 1, :2 * KW - 1]
        small["pool_w"][li] = dpool_w
        small["pool_scale"][li] = dpool_sc.reshape(D_POOL)
        small["ln1_g"][li], small["ln1_b"][li] = dg1.reshape(D), db1.reshape(D)
        small["ln2_g"][li], small["ln2_b"][li] = dg2.reshape(D), db2.reshape(D)
        small["conv_b"][li] = dcb.reshape(D_FF)
        small["conv_w"][li] = dcw.transpose(1, 0, 2).reshape(3, D_FF)
        if li == 1:
            token = token + tok(ex.replicated_early(small, dh))
    dx, dg_in, db_in0 = ln_bwd(dh, x2, vec(W["ln_in_g"]), "ln_in_bwd", after=token)
    parts = {n: jnp.stack(v_) for n, v_ in small.items()}
    parts["ln_in_g"], parts["ln_in_b"] = dg_in.reshape(D), db_in0.reshape(D)
    return loss_part, dx, parts
```

```python
import numpy as np
import jax
import jax.numpy as jnp
from jax import lax
from jax.experimental import pallas as pl
from jax.experimental.pallas import tpu as pltpu

F32 = jnp.float32
BF16 = jnp.bfloat16
I32 = jnp.int32

D = 1024
DEPTH = 4
GRID_W = 64
N_HEADS = 8
HEAD_DIM = 64
D_ATTN = 512
KH = 8
KW = 16
POOL_WINDOWS = (2, 4, 8, 16)
D_POOL = 512
PGD = 128
D_FF = 2816
PLE_DIM = 256
N_PROJ = 4096
ALPHA = (2 * DEPTH) ** 0.25
LN_EPS = 1e-5
NEG_INF = -1e30
ATT_SCALE = HEAD_DIM ** -0.5
ADAM_LR = 0.001
ADAM_B1 = 0.9
ADAM_B2 = 0.999
ADAM_EPS = 1e-08
ADAM_WD = 0.01
ADAM_STEP = 10

N_DEV = 8
AXES = ("x", "y", "c")
FF_BLK = D_FF // 4
FF_SHARD = D_FF // N_DEV
QROWS = 8
KROWS = 16
QB = QROWS * GRID_W
KB = KROWS * GRID_W
V7X_VMEM_LIMIT = 56 * 2 ** 20
MESH = pl.DeviceIdType.MESH
ANY = pl.BlockSpec(memory_space=pl.ANY)


def _cparams(n_grid):
    return pltpu.CompilerParams(dimension_semantics=("arbitrary",) * n_grid, vmem_limit_bytes=V7X_VMEM_LIMIT)


def _nn(a, b):
    return lax.dot_general(a, b, (((1,), (0,)), ((), ())), preferred_element_type=F32)


def _nt(a, b):
    return lax.dot_general(a, b, (((1,), (1,)), ((), ())), preferred_element_type=F32)


def _tn(a, b):
    return lax.dot_general(a, b, (((0,), (0,)), ((), ())), preferred_element_type=F32)


def _sigmoid(x):
    return 1.0 / (1.0 + jnp.exp(-x))


def _ln_fwd(z, g, b):
    mu = jnp.mean(z, axis=-1, keepdims=True)
    xc = z - mu
    var = jnp.mean(xc * xc, axis=-1, keepdims=True)
    return xc * lax.rsqrt(var + LN_EPS) * g + b


def _ln_bwd(dh, z, g):
    mu = jnp.mean(z, axis=-1, keepdims=True)
    xc = z - mu
    var = jnp.mean(xc * xc, axis=-1, keepdims=True)
    rstd = lax.rsqrt(var + LN_EPS)
    xhat = xc * rstd
    dxh = dh * g
    m1 = jnp.mean(dxh, axis=-1, keepdims=True)
    m2 = jnp.mean(dxh * xhat, axis=-1, keepdims=True)
    return rstd * (dxh - m1 - xhat * m2), dh * xhat


def _colsum(x):
    return jnp.sum(x, axis=0, keepdims=True)


def _lane_cat(ref):
    return jnp.concatenate([ref[j] for j in range(ref.shape[0])], axis=1)


def _row_cat(ref):
    n, r, c = ref.shape
    return ref[...].reshape(n * r, c)


def _shards(n, r, c, li, j_of=None):
    del li
    if j_of is None:
        return pl.BlockSpec((n, r, c), lambda *_: (0, 0, 0))
    return pl.BlockSpec((n, r, c), lambda *g: (j_of(*g), 0, 0))


def _shard(r, c, li, j_of):
    del li
    return pl.BlockSpec((None, r, c), lambda *g: (j_of(*g), 0, 0))


def ln_fwd(x, g, b, name, after=()):
    s = x.shape[0]
    tm = 512
    na = len(after)

    def body(x_ref, g_ref, b_ref, *rest):
        h_ref, hb_ref = rest[na:]
        h = _ln_fwd(x_ref[...], g_ref[...], b_ref[...])
        h_ref[...] = h
        hb_ref[...] = h.astype(BF16)

    row = pl.BlockSpec((tm, D), lambda i: (i, 0))
    vec = pl.BlockSpec((1, D), lambda i: (0, 0))
    return pl.pallas_call(
        body, name=name, grid=(s // tm,), in_specs=[row, vec, vec] + [ANY] * na, out_specs=[row, row],
        out_shape=[jax.ShapeDtypeStruct((s, D), F32), jax.ShapeDtypeStruct((s, D), BF16)],
        compiler_params=_cparams(1))(x, g, b, *after)


def proj_fwd(hb, win, bias, li, j0, nj, out_dtype, name, after=()):
    s = hb.shape[0]
    bn = N_PROJ // N_DEV
    tm = 1024

    def body(a_ref, w_ref, b_ref, *rest):
        rest[-1][...] = (_nn(a_ref[...], w_ref[...]) + b_ref[...]).astype(out_dtype)

    return pl.pallas_call(
        body, name=name, grid=(s // tm, nj),
        in_specs=[pl.BlockSpec((tm, D), lambda i, j: (i, 0)),
                  _shard(D, bn, li, lambda i, j: j0 + j),
                  pl.BlockSpec((1, bn), lambda i, j: (0, j0 + j))] + [ANY] * len(after),
        out_specs=pl.BlockSpec((tm, bn), lambda i, j: (i, j)),
        out_shape=jax.ShapeDtypeStruct((s, nj * bn), out_dtype),
        compiler_params=_cparams(2))(hb, win, bias, *after)


def _attn_types(b, nb):
    first, last = 0, (nb * QROWS - KROWS) * GRID_W
    mid = pl.multiple_of((QROWS * b - KH // 2) * GRID_W, 256)
    return ((b == 0, first), ((b > 0) & (b < nb - 1), mid), (b == nb - 1, last))


def _attn_row(btype, qr):
    lo, delta = ((max(qr - KH // 2, 0), 0), (qr, -(KH // 2)), (min(qr + KH // 2, KH), -KH))[btype]
    return lo, (qr - delta - (KH - 1)) % KROWS, lo - qr + delta + KH - 1


def _row_window(lo):
    pad = (lo % 2) * GRID_W
    return (lo // 2) * 128, KH * GRID_W + 2 * pad, pad


def _lanes(ref, start, width):
    start %= KB
    if start + width <= KB:
        return ref[:, start:start + width]
    return jnp.concatenate([ref[:, start:], ref[:, :start + width - KB]], axis=1)


def _row_logits(s_ref, e_ref, hh, rows, btype, qr):
    lo, shift, _ = _attn_row(btype, qr)
    a0, w, pad = _row_window(lo)
    e = e_ref.at[hh, shift % 2]
    sb = s_ref[rows, a0:a0 + w] + _lanes(e, a0 - (shift - shift % 2) * GRID_W, w)
    if pad:
        lane = lax.broadcasted_iota(I32, (1, w), 1)
        sb = jnp.where((lane >= pad) & (lane < w - pad), sb, NEG_INF)
    return sb, a0, w, pad


def _store_row(ref, rows, a0, w, val):
    if a0:
        ref[rows, 0:a0] = jnp.zeros((GRID_W, a0), ref.dtype)
    ref[rows, a0:a0 + w] = val.astype(ref.dtype)
    if a0 + w < KB:
        ref[rows, a0 + w:KB] = jnp.zeros((GRID_W, KB - a0 - w), ref.dtype)


def attn_fwd(proj, e_tab, name):
    s = proj.shape[0]
    nb = s // QB

    def body(q_ref, k_ref, v_ref, e_ref, o_ref, s_ref, p_ref):
        q = q_ref[...] * ATT_SCALE
        lane = lax.broadcasted_iota(I32, (1, 128), 1)

        def block(btype, k0):
            kwin = k_ref[pl.ds(k0, KB), :]
            vwin = v_ref[pl.ds(k0, KB), :]
            acc = jnp.zeros((QB, 128), F32)
            for hh in range(2):
                lm = (lane // HEAD_DIM) == hh
                qh = jnp.where(lm, q, jnp.zeros_like(q))
                vh = jnp.where(lm, vwin, jnp.zeros_like(vwin))
                s_ref[...] = _nt(qh, kwin)
                for qr in range(QROWS):
                    rows = slice(qr * GRID_W, (qr + 1) * GRID_W)
                    sb, a0, w, _ = _row_logits(s_ref, e_ref, hh, rows, btype, qr)
                    p = jnp.exp(sb - jnp.max(sb, axis=1, keepdims=True))
                    _store_row(p_ref, rows, a0, w, p * (1.0 / jnp.sum(p, axis=1, keepdims=True)))
                acc = acc + _nn(p_ref[...], vh)
            o_ref[...] = acc.astype(BF16)

        for btype, (cond, k0) in enumerate(_attn_types(pl.program_id(1), nb)):
            pl.when(cond)(lambda btype=btype, k0=k0: block(btype, k0))

    return pl.pallas_call(
        body, name=name, grid=(4, nb),
        in_specs=[pl.BlockSpec((QB, 128), lambda j, b: (b, j)),
                  pl.BlockSpec((s, 128), lambda j, b: (0, 4 + j)),
                  pl.BlockSpec((s, 128), lambda j, b: (0, 8 + j)),
                  pl.BlockSpec((2, 2, GRID_W, KB), lambda j, b: (j, 0, 0, 0))],
        out_specs=pl.BlockSpec((QB, 128), lambda j, b: (b, j)),
        out_shape=jax.ShapeDtypeStruct((s, D_ATTN), BF16),
        scratch_shapes=[pltpu.VMEM((QB, KB), F32), pltpu.VMEM((QB, KB), BF16)],
        compiler_params=_cparams(2))(proj, proj, proj, e_tab)


_POOL_PAD = 8


def _pool_counts(s, w):
    t = lax.broadcasted_iota(I32, (s, 1), 0)
    return (jnp.minimum(t + w // 2, s) - jnp.maximum(t - w // 2, 0)).astype(F32)


def _window_sum(x, w, back_first):
    s = x.shape[0]
    z = jnp.zeros((_POOL_PAD, x.shape[1]), F32)
    xe = jnp.concatenate([z, x, z], axis=0)
    n = s + 2 * _POOL_PAD
    acc = xe + pltpu.roll(xe, 1 if back_first else n - 1, 0)
    k = 1
    while 2 * k < w:
        acc = pltpu.roll(acc, k, 0) + pltpu.roll(acc, n - k, 0)
        k *= 2
    return acc[_POOL_PAD:_POOL_PAD + s, :]


def pool_fwd(u, pool_w, pool_scale, name):
    s = u.shape[0]

    def body(u_ref, w_ref, sc_ref, pm_ref, pw_ref):
        for g, w in enumerate(POOL_WINDOWS):
            cols = slice(g * PGD, (g + 1) * PGD)
            ug = u_ref[:, cols]
            pm = (_window_sum(ug, w, True) / _pool_counts(s, w) - ug).astype(BF16)
            pm_ref[:, cols] = pm
            pw_ref[:, cols] = (_nn(pm, w_ref[g]) * sc_ref[:, cols]).astype(BF16)

    full = lambda shape: pl.BlockSpec(shape, lambda i: (0,) * len(shape))
    return pl.pallas_call(
        body, name=name, grid=(1,),
        in_specs=[full((s, D_POOL)), full((4, PGD, PGD)), full((1, D_POOL))],
        out_specs=[full((s, D_POOL)), full((s, D_POOL))],
        out_shape=[jax.ShapeDtypeStruct((s, D_POOL), BF16)] * 2,
        compiler_params=_cparams(1))(u, pool_w, pool_scale)


def merge_fwd(a, pw, wao, wpo, proj, li, name, after=()):
    s = a.shape[0]
    tm, tn = 512, 512
    nt = D // tn
    per = tn // 128

    def body(a_ref, pw_ref, wa_ref, wp_ref, ga_ref, gb_ref, *rest):
        mg_ref, ya_ref, yp_ref = rest[len(after):]
        ya = _nn(a_ref[...], _lane_cat(wa_ref))
        yp = _nn(pw_ref[...], _lane_cat(wp_ref))
        mg = _sigmoid(ga_ref[...].astype(F32)) * ya + _sigmoid(gb_ref[...].astype(F32)) * yp
        mg_ref[...] = mg.astype(BF16)
        ya_ref[...] = ya.astype(BF16)
        yp_ref[...] = yp.astype(BF16)

    act = pl.BlockSpec((tm, D_ATTN), lambda i, j: (i, 0))
    wsp = _shards(per, D_ATTN, 128, li, lambda i, j: j)
    out = pl.BlockSpec((tm, tn), lambda i, j: (i, j))
    ga0 = (3 * D_ATTN + D_POOL) // tn
    return pl.pallas_call(
        body, name=name, grid=(s // tm, nt),
        in_specs=[act, act, wsp, wsp,
                  pl.BlockSpec((tm, tn), lambda i, j: (i, ga0 + j)),
                  pl.BlockSpec((tm, tn), lambda i, j: (i, ga0 + nt + j))] + [ANY] * len(after),
        out_specs=[out, out, out],
        out_shape=[jax.ShapeDtypeStruct((s, D), BF16)] * 3,
        compiler_params=_cparams(2))(a, pw, wao, wpo, proj, proj, *after)


def mix_ln_fwd(mg, wmix, h0, g, b, li, name):
    s = mg.shape[0]
    tm = 256

    def body(mg_ref, w_ref, h0_ref, g_ref, b_ref, z_ref, h_ref, hb_ref):
        z = ALPHA * h0_ref[...] + _nn(mg_ref[...], _row_cat(w_ref))
        h = _ln_fwd(z, g_ref[...], b_ref[...])
        z_ref[...] = z
        h_ref[...] = h
        hb_ref[...] = h.astype(BF16)

    row = pl.BlockSpec((tm, D), lambda i: (i, 0))
    vec = pl.BlockSpec((1, D), lambda i: (0, 0))
    return pl.pallas_call(
        body, name=name, grid=(s // tm,),
        in_specs=[row, _shards(N_DEV, D // N_DEV, D, li), row, vec, vec],
        out_specs=[row, row, row],
        out_shape=[jax.ShapeDtypeStruct((s, D), F32), jax.ShapeDtypeStruct((s, D), F32),
                   jax.ShapeDtypeStruct((s, D), BF16)],
        compiler_params=_cparams(1))(mg, wmix, h0, g, b)


def up_fwd(hb, wup, li, name):
    s = hb.shape[0]
    tm = 1024

    def body(a_ref, w_ref, o_ref):
        o_ref[...] = _nt(a_ref[...], w_ref[...]).astype(BF16)

    return pl.pallas_call(
        body, name=name, grid=(s // tm, N_DEV),
        in_specs=[pl.BlockSpec((tm, D), lambda i, j: (i, 0)), _shard(FF_BLK, D, li, lambda i, j: j)],
        out_specs=pl.BlockSpec((None, tm, FF_BLK), lambda i, j: (j, i, 0)),
        out_shape=jax.ShapeDtypeStruct((N_DEV, s, FF_BLK), BF16),
        compiler_params=_cparams(2))(hb, wup)


_SQRT_HALF = 0.7071067811865476
_INV_SQRT_2PI = 0.3989422804014327


def _shift_rows(x, prev_row, next_row):
    n = x.shape[0]
    r = lax.broadcasted_iota(I32, (n, 1), 0)
    back = jnp.where(r == 0, prev_row, pltpu.roll(x, 1, 0))
    fwd = jnp.where(r == n - 1, next_row, pltpu.roll(x, n - 1, 0))
    return back, fwd


HALO = 16


def _halo_maps(tm, s):
    th = tm // HALO
    return (lambda i: jnp.maximum(i * th - 1, 0)), (lambda i: jnp.minimum((i + 1) * th, s // HALO - 1))


def _slab_specs(tm, s, blk_of):
    before, after = _halo_maps(tm, s)
    main = pl.BlockSpec((None, tm, FF_BLK), lambda c, i: (blk_of(c), i, 0))
    prev = pl.BlockSpec((None, HALO, FF_BLK), lambda c, i: (blk_of(c), before(i), 0))
    nxt = pl.BlockSpec((None, HALO, FF_BLK), lambda c, i: (blk_of(c), after(i), 0))
    return main, prev, nxt


def ffn_act_fwd(up, conv_w, conv_b, name):
    s = up.shape[1]
    tm = 512
    nt = s // tm
    hv_main, _, _ = _slab_specs(tm, s, lambda c: c)
    hg_main, hg_prev, hg_next = _slab_specs(tm, s, lambda c: 4 + c)

    def body(hv_ref, hg_ref, hp_ref, hn_ref, cw_ref, cb_ref, t_ref):
        i = pl.program_id(1)
        hg = hg_ref[...].astype(F32)
        prow = jnp.where(i == 0, 0.0, hp_ref[...].astype(F32)[HALO - 1:HALO, :])
        nrow = jnp.where(i == nt - 1, 0.0, hn_ref[...].astype(F32)[0:1, :])
        back, fwd = _shift_rows(hg, prow, nrow)
        c = back * cw_ref[0:1, :] + hg * cw_ref[1:2, :] + fwd * cw_ref[2:3, :] + cb_ref[...]
        act = 0.5 * c * (1.0 + lax.erf(c * _SQRT_HALF))
        t_ref[...] = (act * hv_ref[...].astype(F32)).astype(BF16)

    return pl.pallas_call(
        body, name=name, grid=(4, nt),
        in_specs=[hv_main, hg_main, hg_prev, hg_next,
                  pl.BlockSpec((None, 3, FF_BLK), lambda c, i: (c, 0, 0)),
                  pl.BlockSpec((None, 1, FF_BLK), lambda c, i: (c, 0, 0))],
        out_specs=pl.BlockSpec((None, tm, FF_BLK), lambda c, i: (c, i, 0)),
        out_shape=jax.ShapeDtypeStruct((4, s, FF_BLK), BF16),
        compiler_params=_cparams(2))(up, up, up, up, conv_w, conv_b)


def down_ple_ln_fwd(t, wdown, hb, wpg, pb, wpp, h1, g, b, li, name):
    s = hb.shape[0]
    tm = 256

    def body(t_ref, wd_ref, hb_ref, wpg_ref, p_ref, wpp_ref, h1_ref, g_ref, b_ref,
             z_ref, h_ref, hbo_ref, pg_ref, pp_ref):
        wd = _row_cat(wd_ref)
        ffn = _nn(t_ref[0], wd[0:FF_BLK, :])
        for c in range(1, 4):
            ffn = ffn + _nn(t_ref[c], wd[c * FF_BLK:(c + 1) * FF_BLK, :])
        pg = _nn(hb_ref[...], _row_cat(wpg_ref))
        pp = _nn(p_ref[...], _lane_cat(wpp_ref))
        z = ALPHA * h1_ref[...] + ffn + _sigmoid(pg) * pp
        h = _ln_fwd(z, g_ref[...], b_ref[...])
        z_ref[...] = z
        h_ref[...] = h
        hbo_ref[...] = h.astype(BF16)
        pg_ref[...] = pg.astype(BF16)
        pp_ref[...] = pp.astype(BF16)

    row = pl.BlockSpec((tm, D), lambda i: (i, 0))
    vec = pl.BlockSpec((1, D), lambda i: (0, 0))
    return pl.pallas_call(
        body, name=name, grid=(s // tm,),
        in_specs=[pl.BlockSpec((4, tm, FF_BLK), lambda i: (0, i, 0)),
                  _shards(N_DEV, FF_SHARD, D, li),
                  row, _shards(N_DEV, D // N_DEV, D, li),
                  pl.BlockSpec((tm, PLE_DIM), lambda i: (i, 0)),
                  _shards(N_DEV, PLE_DIM, 128, li),
                  row, vec, vec],
        out_specs=[row] * 5,
        out_shape=[jax.ShapeDtypeStruct((s, D), F32), jax.ShapeDtypeStruct((s, D), F32),
                   jax.ShapeDtypeStruct((s, D), BF16), jax.ShapeDtypeStruct((s, D), BF16),
                   jax.ShapeDtypeStruct((s, D), BF16)],
        compiler_params=_cparams(1))(t, wdown, hb, wpg, pb, wpp, h1, g, b)


def loss_bwd(h, target, name):
    s = h.shape[0]
    tm = 512

    def body(h_ref, t_ref, dh_ref, l_ref):
        @pl.when(pl.program_id(0) == 0)
        def _():
            l_ref[...] = jnp.zeros_like(l_ref)
        e = h_ref[...] - t_ref[...]
        dh_ref[...] = e * (1.0 / D)
        l_ref[...] += 0.5 * jnp.sum(jnp.mean(e * e, axis=-1, keepdims=True), axis=0, keepdims=True)

    row = pl.BlockSpec((tm, D), lambda i: (i, 0))
    return pl.pallas_call(
        body, name=name, grid=(s // tm,), in_specs=[row, row],
        out_specs=[row, pl.BlockSpec((1, 1), lambda i: (0, 0))],
        out_shape=[jax.ShapeDtypeStruct((s, D), F32), jax.ShapeDtypeStruct((1, 1), F32)],
        compiler_params=_cparams(1))(h, target)


def ln_bwd(dh, z, g, name, after=()):
    s = dh.shape[0]
    tm = 512
    na = len(after)

    def body(dh_ref, z_ref, g_ref, *rest):
        dz_ref, dg_ref, db_ref = rest[na:]

        @pl.when(pl.program_id(0) == 0)
        def _():
            dg_ref[...] = jnp.zeros_like(dg_ref)
            db_ref[...] = jnp.zeros_like(db_ref)
        dh = dh_ref[...]
        dz, dgx = _ln_bwd(dh, z_ref[...], g_ref[...])
        dz_ref[...] = dz
        dg_ref[...] += _colsum(dgx)
        db_ref[...] += _colsum(dh)

    row = pl.BlockSpec((tm, D), lambda i: (i, 0))
    vec = pl.BlockSpec((1, D), lambda i: (0, 0))
    return pl.pallas_call(
        body, name=name, grid=(s // tm,), in_specs=[row, row, vec] + [ANY] * na, out_specs=[row, vec, vec],
        out_shape=[jax.ShapeDtypeStruct((s, D), F32), jax.ShapeDtypeStruct((1, D), F32),
                   jax.ShapeDtypeStruct((1, D), F32)],
        compiler_params=_cparams(1))(dh, z, g, *after)


def ln2_ple_bwd(dh, z, g, pg, pp, name, after=()):
    s = dh.shape[0]
    tm = 512
    na = len(after)

    def body(dh_ref, z_ref, g_ref, pg_ref, pp_ref, *rest):
        dz_ref, dzb_ref, dpg_ref, dpp_ref, dg_ref, db_ref = rest[na:]

        @pl.when(pl.program_id(0) == 0)
        def _():
            dg_ref[...] = jnp.zeros_like(dg_ref)
            db_ref[...] = jnp.zeros_like(db_ref)
        dh = dh_ref[...]
        dz, dgx = _ln_bwd(dh, z_ref[...], g_ref[...])
        sg = _sigmoid(pg_ref[...].astype(F32))
        dz_ref[...] = dz
        dzb_ref[...] = dz.astype(BF16)
        dpg_ref[...] = (dz * pp_ref[...].astype(F32) * sg * (1.0 - sg)).astype(BF16)
        dpp_ref[...] = (dz * sg).astype(BF16)
        dg_ref[...] += _colsum(dgx)
        db_ref[...] += _colsum(dh)

    row = pl.BlockSpec((tm, D), lambda i: (i, 0))
    vec = pl.BlockSpec((1, D), lambda i: (0, 0))
    return pl.pallas_call(
        body, name=name, grid=(s // tm,), in_specs=[row, row, vec, row, row] + [ANY] * na,
        out_specs=[row, row, row, row, vec, vec],
        out_shape=[jax.ShapeDtypeStruct((s, D), F32)] + [jax.ShapeDtypeStruct((s, D), BF16)] * 3
        + [jax.ShapeDtypeStruct((1, D), F32)] * 2,
        compiler_params=_cparams(1))(dh, z, g, pg, pp, *after)


def wgrad_rows(a, dy, name, after=()):
    s, k = a.shape
    n = dy.shape[1]
    kb = k // N_DEV

    def body(a_ref, dy_ref, *rest):
        rest[-1][...] = _tn(a_ref[...], dy_ref[...]).astype(BF16)

    return pl.pallas_call(
        body, name=name, grid=(N_DEV,),
        in_specs=[pl.BlockSpec((s, kb), lambda j: (0, j)), pl.BlockSpec((s, n), lambda j: (0, 0))] + [ANY] * len(after),
        out_specs=pl.BlockSpec((None, kb, n), lambda j: (j, 0, 0)),
        out_shape=jax.ShapeDtypeStruct((N_DEV, kb, n), BF16),
        compiler_params=_cparams(1))(a, dy, *after)


def wgrad_cols(a, dy, name, with_colsum=False):
    s, k = a.shape
    n = dy.shape[1]
    nb = n // N_DEV

    def body(a_ref, dy_ref, o_ref, *cs_ref):
        dy = dy_ref[...]
        o_ref[...] = _tn(a_ref[...], dy).astype(BF16)
        if with_colsum:
            cs_ref[0][...] = _colsum(dy.astype(F32))

    out_specs = [pl.BlockSpec((None, k, nb), lambda j: (j, 0, 0))]
    out_shape = [jax.ShapeDtypeStruct((N_DEV, k, nb), BF16)]
    if with_colsum:
        out_specs.append(pl.BlockSpec((1, nb), lambda j: (0, j)))
        out_shape.append(jax.ShapeDtypeStruct((1, n), F32))
    res = pl.pallas_call(
        body, name=name, grid=(N_DEV,),
        in_specs=[pl.BlockSpec((s, k), lambda j: (0, 0)), pl.BlockSpec((s, nb), lambda j: (0, j))],
        out_specs=out_specs, out_shape=out_shape,
        compiler_params=_cparams(1))(a, dy)
    return res if with_colsum else res[0]


def wgrad_down(t, dy, name):
    _, s, k = t.shape
    n = dy.shape[1]

    def body(a_ref, dy_ref, o_ref):
        o_ref[...] = _tn(a_ref[...], dy_ref[...]).astype(BF16)

    return pl.pallas_call(
        body, name=name, grid=(4,),
        in_specs=[pl.BlockSpec((None, s, k), lambda j: (j, 0, 0)), pl.BlockSpec((s, n), lambda j: (0, 0))],
        out_specs=pl.BlockSpec((None, k, n), lambda j: (j, 0, 0)),
        out_shape=jax.ShapeDtypeStruct((4, k, n), BF16),
        compiler_params=_cparams(1))(t, dy)


def wgrad_up(a, dhv, dhg, name):
    s, k = a.shape

    def body(a_ref, dv_ref, dg_ref, o_ref):
        j = pl.program_id(0)

        @pl.when(j < 4)
        def _():
            o_ref[...] = _tn(dv_ref[...], a_ref[...]).astype(BF16)

        @pl.when(j >= 4)
        def _():
            o_ref[...] = _tn(dg_ref[...], a_ref[...]).astype(BF16)

    return pl.pallas_call(
        body, name=name, grid=(N_DEV,),
        in_specs=[pl.BlockSpec((s, k), lambda j: (0, 0)),
                  pl.BlockSpec((None, s, FF_BLK), lambda j: (jnp.minimum(j, 3), 0, 0)),
                  pl.BlockSpec((None, s, FF_BLK), lambda j: (jnp.maximum(j - 4, 0), 0, 0))],
        out_specs=pl.BlockSpec((None, FF_BLK, k), lambda j: (j, 0, 0)),
        out_shape=jax.ShapeDtypeStruct((N_DEV, FF_BLK, k), BF16),
        compiler_params=_cparams(1))(a, dhv, dhg)


def ffn_act_bwd(dzb, wdown, up, conv_w, conv_b, li, name):
    s = up.shape[1]
    tm = 512
    nt = s // tm
    before, after = _halo_maps(tm, s)
    hv_main, hv_prev, hv_next = _slab_specs(tm, s, lambda c: c)
    hg_main, hg_prev, hg_next = _slab_specs(tm, s, lambda c: 4 + c)

    def dc_of(dz, wd, hv, hg, back, fwd, cw_ref, cb_ref):
        dt = _nt(dz, wd)
        c = back * cw_ref[0:1, :] + hg * cw_ref[1:2, :] + fwd * cw_ref[2:3, :] + cb_ref[...]
        cdf = 0.5 * (1.0 + lax.erf(c * _SQRT_HALF))
        pdf = jnp.exp(-0.5 * c * c) * _INV_SQRT_2PI
        return dt, c * cdf, dt * hv * (cdf + c * pdf)

    def body(dz_ref, dzp_ref, dzn_ref, wd_ref, hv_ref, hvp_ref, hvn_ref, hg_ref, hgp_ref, hgn_ref, cw_ref, cb_ref,
             dhv_ref, dhg_ref, dcw_ref, dcb_ref):
        i = pl.program_id(1)

        @pl.when(i == 0)
        def _():
            dcw_ref[...] = jnp.zeros_like(dcw_ref)
            dcb_ref[...] = jnp.zeros_like(dcb_ref)

        wd = _row_cat(wd_ref)
        hg = hg_ref[...].astype(F32)
        hgp = hgp_ref[...].astype(F32)
        hgn = hgn_ref[...].astype(F32)
        first, last = i == 0, i == nt - 1
        e = HALO - 1
        back, fwd = _shift_rows(hg, jnp.where(first, 0.0, hgp[e:e + 1, :]), jnp.where(last, 0.0, hgn[0:1, :]))
        dt, act, dc = dc_of(dz_ref[...], wd, hv_ref[...].astype(F32), hg, back, fwd, cw_ref, cb_ref)
        dhv_ref[...] = (dt * act).astype(BF16)
        bp, fp = _shift_rows(hgp, hgp[0:1, :], hg[0:1, :])
        _, _, dcp = dc_of(dzp_ref[...], wd, hvp_ref[...].astype(F32), hgp, bp, fp, cw_ref, cb_ref)
        bn, fn = _shift_rows(hgn, hg[tm - 1:tm, :], hgn[e:e + 1, :])
        _, _, dcn = dc_of(dzn_ref[...], wd, hvn_ref[...].astype(F32), hgn, bn, fn, cw_ref, cb_ref)
        dc_back, dc_fwd = _shift_rows(dc, jnp.where(first, 0.0, dcp[e:e + 1, :]), jnp.where(last, 0.0, dcn[0:1, :]))
        dhg_ref[...] = (dc_fwd * cw_ref[0:1, :] + dc * cw_ref[1:2, :] + dc_back * cw_ref[2:3, :]).astype(BF16)
        dcw_ref[0:1, :] += _colsum(dc * back)
        dcw_ref[1:2, :] += _colsum(dc * hg)
        dcw_ref[2:3, :] += _colsum(dc * fwd)
        dcb_ref[...] += _colsum(dc)

    out_slab = pl.BlockSpec((None, tm, FF_BLK), lambda c, i: (c, i, 0))
    cw_spec = pl.BlockSpec((None, 3, FF_BLK), lambda c, i: (c, 0, 0))
    cb_spec = pl.BlockSpec((None, 1, FF_BLK), lambda c, i: (c, 0, 0))
    return pl.pallas_call(
        body, name=name, grid=(4, nt),
        in_specs=[pl.BlockSpec((tm, D), lambda c, i: (i, 0)),
                  pl.BlockSpec((HALO, D), lambda c, i: (before(i), 0)),
                  pl.BlockSpec((HALO, D), lambda c, i: (after(i), 0)),
                  _shards(2, FF_SHARD, D, li, lambda c, i: c),
                  hv_main, hv_prev, hv_next, hg_main, hg_prev, hg_next, cw_spec, cb_spec],
        out_specs=[out_slab, out_slab, cw_spec, cb_spec],
        out_shape=[jax.ShapeDtypeStruct((4, s, FF_BLK), BF16), jax.ShapeDtypeStruct((4, s, FF_BLK), BF16),
                   jax.ShapeDtypeStruct((4, 3, FF_BLK), F32), jax.ShapeDtypeStruct((4, 1, FF_BLK), F32)],
        compiler_params=_cparams(2))(dzb, dzb, dzb, wdown, up, up, up, up, up, up, conv_w, conv_b)


def dh1_ln1_bwd(dz2, dpg, wpg, dhv, dhg, wup, z1, g1, li, name, after=()):
    s = dz2.shape[0]
    tm = 256
    na = len(after)

    def body(dz2_ref, dpg_ref, wpg_ref, dhv_ref, dhg_ref, wup_ref, z1_ref, g_ref, *rest):
        dz_ref, dzb_ref, dg_ref, db_ref = rest[na:]

        @pl.when(pl.program_id(0) == 0)
        def _():
            dg_ref[...] = jnp.zeros_like(dg_ref)
            db_ref[...] = jnp.zeros_like(db_ref)
        dh = ALPHA * dz2_ref[...] + _nt(dpg_ref[...], _row_cat(wpg_ref))
        for c in range(4):
            dh = dh + _nn(dhv_ref[c], wup_ref[c]) + _nn(dhg_ref[c], wup_ref[4 + c])
        dz, dgx = _ln_bwd(dh, z1_ref[...], g_ref[...])
        dz_ref[...] = dz
        dzb_ref[...] = dz.astype(BF16)
        dg_ref[...] += _colsum(dgx)
        db_ref[...] += _colsum(dh)

    row = pl.BlockSpec((tm, D), lambda i: (i, 0))
    vec = pl.BlockSpec((1, D), lambda i: (0, 0))
    slab = pl.BlockSpec((4, tm, FF_BLK), lambda i: (0, i, 0))
    return pl.pallas_call(
        body, name=name, grid=(s // tm,),
        in_specs=[row, row, _shards(N_DEV, D // N_DEV, D, li), slab, slab, _shards(N_DEV, FF_BLK, D, li), row, vec]
        + [ANY] * na,
        out_specs=[row, row, vec, vec],
        out_shape=[jax.ShapeDtypeStruct((s, D), F32), jax.ShapeDtypeStruct((s, D), BF16),
                   jax.ShapeDtypeStruct((1, D), F32), jax.ShapeDtypeStruct((1, D), F32)],
        compiler_params=_cparams(1))(dz2, dpg, wpg, dhv, dhg, wup, z1, g1, *after)


def merge_bwd(dz1b, wmix, proj, ya, yp, li, name):
    s = dz1b.shape[0]
    tm, tn = 512, 512
    nt = D // tn
    per = tn // (D // N_DEV)
    ga0 = (3 * D_ATTN + D_POOL) // tn

    def body(dz_ref, w_ref, ga_ref, gb_ref, ya_ref, yp_ref, dya_ref, dyp_ref, dga_ref, dgb_ref):
        dm = _nt(dz_ref[...], _row_cat(w_ref))
        sa = _sigmoid(ga_ref[...].astype(F32))
        sb = _sigmoid(gb_ref[...].astype(F32))
        dya_ref[...] = (dm * sa).astype(BF16)
        dyp_ref[...] = (dm * sb).astype(BF16)
        dga_ref[...] = (dm * ya_ref[...].astype(F32) * sa * (1.0 - sa)).astype(BF16)
        dgb_ref[...] = (dm * yp_ref[...].astype(F32) * sb * (1.0 - sb)).astype(BF16)

    tile = pl.BlockSpec((tm, tn), lambda i, j: (i, j))
    return pl.pallas_call(
        body, name=name, grid=(s // tm, nt),
        in_specs=[pl.BlockSpec((tm, D), lambda i, j: (i, 0)),
                  _shards(per, D // N_DEV, D, li, lambda i, j: j),
                  pl.BlockSpec((tm, tn), lambda i, j: (i, ga0 + j)),
                  pl.BlockSpec((tm, tn), lambda i, j: (i, ga0 + nt + j)),
                  tile, tile],
        out_specs=[tile] * 4,
        out_shape=[jax.ShapeDtypeStruct((s, D), BF16)] * 4,
        compiler_params=_cparams(2))(dz1b, wmix, proj, proj, ya, yp)


def attn_out_bwd(dya, wao, li, name, after=()):
    s = dya.shape[0]
    tm = 512

    def body(d_ref, w_ref, *rest):
        rest[-1][...] = _nt(d_ref[...], _lane_cat(w_ref)).astype(BF16)

    return pl.pallas_call(
        body, name=name, grid=(s // tm,),
        in_specs=[pl.BlockSpec((tm, D), lambda i: (i, 0)), _shards(N_DEV, D_ATTN, 128, li)] + [ANY] * len(after),
        out_specs=pl.BlockSpec((tm, D_ATTN), lambda i: (i, 0)),
        out_shape=jax.ShapeDtypeStruct((s, D_ATTN), BF16),
        compiler_params=_cparams(1))(dya, wao, *after)


def pool_bwd(dyp, wpo, pm, pool_w, pool_scale, li, name):
    s = dyp.shape[0]

    def body(dyp_ref, wpo_ref, pm_ref, w_ref, sc_ref, du_ref, dw_ref, dsc_ref):
        wpo = _lane_cat(wpo_ref)
        dyp = dyp_ref[...]
        for g, w in enumerate(POOL_WINDOWS):
            cols = slice(g * PGD, (g + 1) * PGD)
            dpw = _nt(dyp, wpo[g * PGD:(g + 1) * PGD, :])
            pmg = pm_ref[:, cols]
            dsc_ref[:, cols] = _colsum(dpw * _nn(pmg, w_ref[g]))
            dpmw = (dpw * sc_ref[:, cols]).astype(BF16)
            dw_ref[g] = _tn(pmg, dpmw)
            dpm = _nt(dpmw, w_ref[g])
            du_ref[:, cols] = (_window_sum(dpm / _pool_counts(s, w), w, False) - dpm).astype(BF16)

    full = lambda shape: pl.BlockSpec(shape, lambda i: (0,) * len(shape))
    return pl.pallas_call(
        body, name=name, grid=(1,),
        in_specs=[full((s, D)), _shards(N_DEV, D_POOL, 128, li), full((s, D_POOL)), full((4, PGD, PGD)),
                  full((1, D_POOL))],
        out_specs=[full((s, D_POOL)), full((4, PGD, PGD)), full((1, D_POOL))],
        out_shape=[jax.ShapeDtypeStruct((s, D_POOL), BF16), jax.ShapeDtypeStruct((4, PGD, PGD), F32),
                   jax.ShapeDtypeStruct((1, D_POOL), F32)],
        compiler_params=_cparams(1))(dyp, wpo, pm, pool_w, pool_scale)


def attn_bwd(proj, da, e_rev, name, after=()):
    s = proj.shape[0]
    nb = s // QB
    skew = GRID_W + (GRID_W - KW)

    def body(q_ref, k_ref, v_ref, do_ref, e_ref, *rest):
        dq_ref, dk_ref, dv_ref, g_ref, s_ref, dp_ref, ds_ref, p_ref, dkt_acc, dvt_acc = rest[len(after):]
        b = pl.program_id(1)

        @pl.when(b == 0)
        def _():
            dkt_acc[...] = jnp.zeros_like(dkt_acc)
            dvt_acc[...] = jnp.zeros_like(dvt_acc)
            g_ref[...] = jnp.zeros_like(g_ref)

        ri = lax.broadcasted_iota(I32, (QB, QB), 0)
        ci = lax.broadcasted_iota(I32, (QB, QB), 1)
        rev = jnp.where(ri + ci == QB - 1, 1.0, 0.0).astype(BF16)
        q = _nn(rev, q_ref[...]).astype(BF16) * ATT_SCALE
        do = _nn(rev, do_ref[...]).astype(BF16)
        lane = lax.broadcasted_iota(I32, (1, 128), 1)

        def block(btype, k0):
            kwin = k_ref[pl.ds(k0, KB), :]
            vwin = v_ref[pl.ds(k0, KB), :]
            dq = jnp.zeros((QB, 128), F32)
            for hh in range(2):
                lm = (lane // HEAD_DIM) == hh
                qh = jnp.where(lm, q, jnp.zeros_like(q))
                doh = jnp.where(lm, do, jnp.zeros_like(do))
                kh = jnp.where(lm, kwin, jnp.zeros_like(kwin))
                s_ref[...] = _nt(qh, kwin)
                dp_ref[...] = _nt(doh, vwin)
                g = jnp.zeros((1, KB), F32)
                for ib in range(QROWS):
                    qr = QROWS - 1 - ib
                    rows = slice(ib * GRID_W, (ib + 1) * GRID_W)
                    sb, a0, w, pad = _row_logits(s_ref, e_ref, hh, rows, btype, qr)
                    p = jnp.exp(sb - jnp.max(sb, axis=1, keepdims=True))
                    p = p * (1.0 / jnp.sum(p, axis=1, keepdims=True))
                    dp = dp_ref[rows, a0:a0 + w]
                    ds = p * (dp - jnp.sum(p * dp, axis=1, keepdims=True))
                    _store_row(ds_ref, rows, a0, w, ds)
                    _store_row(p_ref, rows, a0, w, p)
                    t = jnp.sum(pltpu.roll(ds, w - skew, 1, stride=1, stride_axis=0), axis=0, keepdims=True)
                    t = t[:, :KH * GRID_W] if pad else pltpu.roll(t, GRID_W, 1)
                    i0 = _attn_row(btype, qr)[2]
                    g = g + pltpu.roll(jnp.concatenate([t, jnp.zeros_like(t)], axis=1), i0 * GRID_W, 1)
                g_ref[hh] += g
                dsb = ds_ref[...]
                dq = dq + _nn(dsb, kh) * ATT_SCALE
                dkt_acc[:, pl.ds(k0, KB)] += _tn(qh, dsb)
                dvt_acc[:, pl.ds(k0, KB)] += _tn(doh, p_ref[...])
            dq_ref[...] = _nn(rev, dq.astype(BF16)).astype(BF16)

        for btype, (cond, k0) in enumerate(_attn_types(b, nb)):
            pl.when(cond)(lambda btype=btype, k0=k0: block(btype, k0))

        @pl.when(b == nb - 1)
        def _():
            dk_ref[...] = dkt_acc[...].T.astype(BF16)
            dv_ref[...] = dvt_acc[...].T.astype(BF16)

    col = pl.BlockSpec((s, 128), lambda j, b: (0, j))
    return pl.pallas_call(
        body, name=name, grid=(4, nb),
        in_specs=[pl.BlockSpec((QB, 128), lambda j, b: (b, j)),
                  pl.BlockSpec((s, 128), lambda j, b: (0, 4 + j)),
                  pl.BlockSpec((s, 128), lambda j, b: (0, 8 + j)),
                  pl.BlockSpec((QB, 128), lambda j, b: (b, j)),
                  pl.BlockSpec((2, 2, GRID_W, KB), lambda j, b: (j, 0, 0, 0))] + [ANY] * len(after),
        out_specs=[pl.BlockSpec((QB, 128), lambda j, b: (b, j)), col, col,
                   pl.BlockSpec((2, 1, KB), lambda j, b: (j, 0, 0))],
        out_shape=[jax.ShapeDtypeStruct((s, D_ATTN), BF16)] * 3 + [jax.ShapeDtypeStruct((N_HEADS, 1, KB), F32)],
        scratch_shapes=[pltpu.VMEM((QB, KB), F32), pltpu.VMEM((QB, KB), F32), pltpu.VMEM((QB, KB), BF16),
                        pltpu.VMEM((QB, KB), BF16), pltpu.VMEM((128, s), F32), pltpu.VMEM((128, s), F32)],
        compiler_params=_cparams(2))(proj, proj, proj, da, e_rev, *after)


def dh0_bwd(dz1, dproj, win, li, name, after=()):
    s = dz1.shape[0]
    tm = 256
    bn = N_PROJ // N_DEV

    def body(dz_ref, dp_ref, w_ref, *rest):
        acc = ALPHA * dz_ref[...]
        for j in range(N_DEV):
            acc = acc + _nt(dp_ref[:, j * bn:(j + 1) * bn], w_ref[j])
        rest[-1][...] = acc

    row = pl.BlockSpec((tm, D), lambda i: (i, 0))
    return pl.pallas_call(
        body, name=name, grid=(s // tm,),
        in_specs=[row, pl.BlockSpec((tm, N_PROJ), lambda i: (i, 0)), _shards(N_DEV, D, bn, li)] + [ANY] * len(after),
        out_specs=row, out_shape=jax.ShapeDtypeStruct((s, D), F32),
        compiler_params=_cparams(1))(dz1, dproj, win, *after)


def _coords():
    return lax.axis_index("x"), lax.axis_index("y"), lax.axis_index("c")


def _dev_index(px, py, pc):
    return 4 * px + 2 * py + pc


def all_gather(arrs, name):
    n = len(arrs)

    def body(*refs):
        ins, outs = refs[:n], refs[n:2 * n]
        send_sems, recv_sems, local_sems = refs[2 * n:]
        x, y, c = _coords()
        me, sibling = (x, y, c), (x, y, 1 - c)
        chips = [(1 - x, y), (x, 1 - y), (1 - x, 1 - y)]

        def copy(a, k, block, to, src=None):
            dst = outs[a].at[_dev_index(*block)]
            return pltpu.make_async_remote_copy(
                src_ref=dst if src is None else src, dst_ref=dst,
                send_sem=send_sems.at[a, k], recv_sem=recv_sems.at[a, k], device_id=to, device_id_type=MESH)

        mine = [pltpu.make_async_copy(ins[a], outs[a].at[_dev_index(*me)], local_sems.at[a]) for a in range(n)]
        for cp in mine:
            cp.start()
        first = []
        for a in range(n):
            first.append(copy(a, 0, me, sibling, src=ins[a]))
            first += [copy(a, 1 + j, me, (*chip, c), src=ins[a]) for j, chip in enumerate(chips)]
        for cp in first:
            cp.start()
        passed = []
        for j, chip in enumerate(chips):
            for a in range(n):
                copy(a, 1 + j, (*chip, c), me).wait_recv()
                cp = copy(a, 4 + j, (*chip, c), sibling)
                cp.start()
                passed.append(cp)
        for a in range(n):
            copy(a, 0, sibling, me).wait_recv()
            for j, chip in enumerate(chips):
                copy(a, 4 + j, (*chip, 1 - c), me).wait_recv()
        for cp in first + passed:
            cp.wait_send()
        for cp in mine:
            cp.wait()

    return pl.pallas_call(
        body, name=name,
        out_shape=[jax.ShapeDtypeStruct((N_DEV,) + a.shape, a.dtype) for a in arrs],
        in_specs=[ANY] * n, out_specs=[ANY] * n,
        scratch_shapes=[pltpu.SemaphoreType.DMA((n, 7)), pltpu.SemaphoreType.DMA((n, 7)),
                        pltpu.SemaphoreType.DMA((n,))],
    )(*arrs)


HBM = pl.BlockSpec(memory_space=pltpu.HBM)
SEM = pl.BlockSpec(memory_space=pltpu.SEMAPHORE)
_EFFECT = pltpu.SideEffectType.DATAFLOW_SIDE_EFFECTING
_TOKEN = jax.ShapeDtypeStruct((8, 128), F32)


def _in_hbm(a):
    return pltpu.with_memory_space_constraint(a, pltpu.HBM)


def _hbm_like(a):
    return pltpu.HBM(a.shape, a.dtype)


def _peers(x, y, c):
    return [(x, y, 1 - c), (1 - x, y, c), (x, 1 - y, c), (1 - x, 1 - y, c)]


def ag_start(lands, after, name):
    n = len(lands)

    def body(*refs):
        land = refs[:n]
        send_sem, recv_sem, token = refs[n + 1], refs[n + 2], refs[-1]
        x, y, c = _coords()
        me = _dev_index(x, y, c)
        for k, peer in enumerate(_peers(x, y, c)):
            for a in range(n):
                pltpu.make_async_remote_copy(src_ref=land[a].at[me], dst_ref=land[a].at[me], send_sem=send_sem.at[k],
                                             recv_sem=recv_sem.at[k], device_id=peer, device_id_type=MESH).start()
        token[...] = jnp.zeros_like(token)

    res = pl.pallas_call(
        body, name=name,
        out_shape=(pltpu.SemaphoreType.DMA((4,)), pltpu.SemaphoreType.DMA((4,)), *[_hbm_like(l) for l in lands], _TOKEN),
        in_specs=[HBM] * n + [ANY], out_specs=(SEM, SEM, *[HBM] * n, pl.BlockSpec(memory_space=pltpu.VMEM)),
        input_output_aliases={a: 2 + a for a in range(n)},
        compiler_params=pltpu.CompilerParams(has_side_effects=_EFFECT),
    )(*[_in_hbm(l) for l in lands], after)
    return res[0], res[1], list(res[2:2 + n]), res[-1]


def ag_forward(send_sem, recv_sem, lands, after, name):
    n = len(lands)

    def body(*refs):
        send_sem, recv_sem = refs[0], refs[1]
        land = refs[2:2 + n]
        fsend, frecv = refs[3 + n], refs[4 + n]
        x, y, c = _coords()
        peers = _peers(x, y, c)
        for k in range(1, 4):
            blk = _dev_index(*peers[k])
            for a in range(n):
                pltpu.make_async_remote_copy(src_ref=land[a].at[blk], dst_ref=land[a].at[blk], send_sem=send_sem.at[k],
                                             recv_sem=recv_sem.at[k], device_id=peers[k], device_id_type=MESH).wait_recv()
        for k in range(1, 4):
            blk = _dev_index(*peers[k])
            for a in range(n):
                pltpu.make_async_remote_copy(src_ref=land[a].at[blk], dst_ref=land[a].at[blk], send_sem=fsend.at[k - 1],
                                             recv_sem=frecv.at[k - 1], device_id=peers[0], device_id_type=MESH).start()

    res = pl.pallas_call(
        body, name=name,
        out_shape=(pltpu.SemaphoreType.DMA((3,)), pltpu.SemaphoreType.DMA((3,)), *[_hbm_like(l) for l in lands]),
        in_specs=[SEM, SEM, *[HBM] * n, ANY], out_specs=(SEM, SEM, *[HBM] * n),
        input_output_aliases={2 + a: 2 + a for a in range(n)},
        compiler_params=pltpu.CompilerParams(has_side_effects=_EFFECT),
    )(send_sem, recv_sem, *lands, after)
    return res[0], res[1], list(res[2:])


def ag_finish(send_sem, recv_sem, fsend, frecv, lands, after, name):
    n = len(lands)

    def body(*refs):
        send_sem, recv_sem, fsend, frecv = refs[:4]
        land = refs[4:4 + n]
        x, y, c = _coords()
        me = _dev_index(x, y, c)
        peers = _peers(x, y, c)
        for k in range(4):
            for a in range(n):
                pltpu.make_async_remote_copy(src_ref=land[a].at[me], dst_ref=land[a].at[me], send_sem=send_sem.at[k],
                                             recv_sem=recv_sem.at[k], device_id=peers[k], device_id_type=MESH).wait_send()
        sib = _dev_index(*peers[0])
        for a in range(n):
            pltpu.make_async_remote_copy(src_ref=land[a].at[sib], dst_ref=land[a].at[sib], send_sem=send_sem.at[0],
                                         recv_sem=recv_sem.at[0], device_id=peers[0], device_id_type=MESH).wait_recv()
        for k in range(1, 4):
            mine = _dev_index(*peers[k])
            theirs = _dev_index(peers[k][0], peers[k][1], 1 - c)
            for a in range(n):
                pltpu.make_async_remote_copy(src_ref=land[a].at[mine], dst_ref=land[a].at[theirs], send_sem=fsend.at[k - 1],
                                             recv_sem=frecv.at[k - 1], device_id=peers[0], device_id_type=MESH).wait()

    res = pl.pallas_call(
        body, name=name, out_shape=tuple(_hbm_like(l) for l in lands),
        in_specs=[SEM] * 4 + [HBM] * n + [ANY], out_specs=tuple([HBM] * n),
        input_output_aliases={4 + a: a for a in range(n)},
        compiler_params=pltpu.CompilerParams(has_side_effects=_EFFECT),
    )(send_sem, recv_sem, fsend, frecv, *lands, after)
    return list(res)


def rs_start(psums, name):
    n = len(psums)
    lands = [lax.empty(p.shape, p.dtype) for p in psums]

    def body(*refs):
        src, land = refs[:n], refs[n:2 * n]
        send_sem, recv_sem, token = refs[2 * n], refs[2 * n + 1], refs[-1]
        peers = _peers(*_coords())
        for k in range(3):
            for a in range(n):
                pltpu.make_async_remote_copy(src_ref=src[a].at[k], dst_ref=land[a].at[k], send_sem=send_sem.at[k],
                                             recv_sem=recv_sem.at[k], device_id=peers[k + 1], device_id_type=MESH).start()
        token[...] = jnp.zeros_like(token)

    res = pl.pallas_call(
        body, name=name,
        out_shape=(pltpu.SemaphoreType.DMA((3,)), pltpu.SemaphoreType.DMA((3,)), *[_hbm_like(p) for p in psums],
                   *[_hbm_like(l) for l in lands], _TOKEN),
        in_specs=[HBM] * (2 * n), out_specs=(SEM, SEM, *[HBM] * (2 * n), pl.BlockSpec(memory_space=pltpu.VMEM)),
        input_output_aliases={a: 2 + a for a in range(2 * n)},
        compiler_params=pltpu.CompilerParams(has_side_effects=_EFFECT),
    )(*[_in_hbm(p) for p in psums], *[_in_hbm(l) for l in lands])
    return res[0], res[1], list(res[2:2 + n]), list(res[2 + n:2 + 2 * n]), res[-1]


def rs_finish(send_sem, recv_sem, psums, lands, after, name):
    n = len(psums)

    def body(*refs):
        send_sem, recv_sem = refs[0], refs[1]
        src, land = refs[2:2 + n], refs[2 + n:2 + 2 * n]
        peers = _peers(*_coords())
        for k in range(3):
            for a in range(n):
                pltpu.make_async_remote_copy(src_ref=src[a].at[k], dst_ref=land[a].at[k], send_sem=send_sem.at[k],
                                             recv_sem=recv_sem.at[k], device_id=peers[k + 1], device_id_type=MESH).wait()

    res = pl.pallas_call(
        body, name=name, out_shape=tuple(_hbm_like(l) for l in lands),
        in_specs=[SEM, SEM] + [HBM] * (2 * n) + [ANY], out_specs=tuple([HBM] * n),
        input_output_aliases={2 + n + a: a for a in range(n)},
        compiler_params=pltpu.CompilerParams(has_side_effects=_EFFECT),
    )(send_sem, recv_sem, *psums, *lands, after)
    return list(res)


def d2d_start(grads, name):
    n = len(grads)
    lands = [lax.empty((4,) + g.shape[1:], g.dtype) for g in grads]

    def body(*refs):
        src, land = refs[:n], refs[n:2 * n]
        send_sem, recv_sem, token = refs[2 * n], refs[2 * n + 1], refs[-1]
        x, y, c = _coords()
        for a in range(n):
            for k in range(4):
                blk = _dev_index(x ^ (k & 1), y ^ (k >> 1), 1 - c)
                pltpu.make_async_remote_copy(src_ref=src[a].at[blk], dst_ref=land[a].at[k], send_sem=send_sem.at[0],
                                             recv_sem=recv_sem.at[0], device_id=(x, y, 1 - c), device_id_type=MESH).start()
        token[...] = jnp.zeros_like(token)

    res = pl.pallas_call(
        body, name=name,
        out_shape=(pltpu.SemaphoreType.DMA((1,)), pltpu.SemaphoreType.DMA((1,)), *[_hbm_like(g) for g in grads],
                   *[_hbm_like(l) for l in lands], _TOKEN),
        in_specs=[HBM] * (2 * n), out_specs=(SEM, SEM, *[HBM] * (2 * n), pl.BlockSpec(memory_space=pltpu.VMEM)),
        input_output_aliases={a: 2 + a for a in range(2 * n)},
        compiler_params=pltpu.CompilerParams(has_side_effects=_EFFECT),
    )(*[_in_hbm(g) for g in grads], *[_in_hbm(l) for l in lands])
    return res[0], res[1], list(res[2:2 + n]), list(res[2 + n:2 + 2 * n]), res[-1]


def d2d_finish(send_sem, recv_sem, grads, lands, after, name):
    n = len(grads)

    def body(*refs):
        send_sem, recv_sem = refs[0], refs[1]
        src, land = refs[2:2 + n], refs[2 + n:2 + 2 * n]
        x, y, c = _coords()
        for a in range(n):
            for k in range(4):
                blk = _dev_index(x ^ (k & 1), y ^ (k >> 1), 1 - c)
                pltpu.make_async_remote_copy(src_ref=src[a].at[blk], dst_ref=land[a].at[k], send_sem=send_sem.at[0],
                                             recv_sem=recv_sem.at[0], device_id=(x, y, 1 - c), device_id_type=MESH).wait()

    res = pl.pallas_call(
        body, name=name, out_shape=tuple(_hbm_like(t) for t in list(grads) + list(lands)),
        in_specs=[SEM, SEM] + [HBM] * (2 * n) + [ANY], out_specs=tuple([HBM] * (2 * n)),
        input_output_aliases={2 + a: a for a in range(2 * n)},
        compiler_params=pltpu.CompilerParams(has_side_effects=_EFFECT),
    )(send_sem, recv_sem, *grads, *lands, after)
    return list(res[:n]), list(res[n:])


def pair_add(blk_idx, g, recv, name):
    _, r, c = g.shape
    tr = _row_tile(r)

    def body(idx_ref, g0, g1, g2, g3, r_ref, own_ref, oth_ref):
        own_ref[...] = g0[...].astype(F32) + r_ref[0].astype(F32)
        for k, gk in enumerate((g1, g2, g3)):
            oth_ref[k] = (gk[...].astype(F32) + r_ref[k + 1].astype(F32)).astype(BF16)

    def blk(k):
        return pl.BlockSpec((None, tr, c), lambda t, idx: (idx[k], t, 0))

    grid_spec = pltpu.PrefetchScalarGridSpec(
        num_scalar_prefetch=1, grid=(r // tr,),
        in_specs=[blk(0), blk(1), blk(2), blk(3), pl.BlockSpec((4, tr, c), lambda t, idx: (0, t, 0))],
        out_specs=[pl.BlockSpec((tr, c), lambda t, idx: (t, 0)), pl.BlockSpec((3, tr, c), lambda t, idx: (0, t, 0))])
    return pl.pallas_call(
        body, name=name, grid_spec=grid_spec,
        out_shape=[jax.ShapeDtypeStruct((r, c), F32), jax.ShapeDtypeStruct((3, r, c), BF16)],
        compiler_params=_cparams(1))(blk_idx, g, g, g, g, recv)


def _row_tile(r):
    return next(t for t in (512, 352, 256, 128) if r % t == 0)


def _adamw(w, g, m, v):
    m = ADAM_B1 * m + (1.0 - ADAM_B1) * g
    v = ADAM_B2 * v + (1.0 - ADAM_B2) * (g * g)
    m_hat = m / (1.0 - ADAM_B1 ** ADAM_STEP)
    v_hat = v / (1.0 - ADAM_B2 ** ADAM_STEP)
    delta = -ADAM_LR * (m_hat / (jnp.sqrt(v_hat) + ADAM_EPS) + ADAM_WD * w)
    return delta, m, v


def adamw_shard(own, recv, w, m, v, li, prev, name):
    r, c = own.shape
    tr = _row_tile(r)

    def body(own_ref, recv_ref, w_ref, m_ref, v_ref, p0, p1, p2, p3, g_ref, d_ref, nm_ref, nv_ref):
        g = own_ref[...] + recv_ref[0].astype(F32) + recv_ref[1].astype(F32) + recv_ref[2].astype(F32)
        delta, nm, nv = _adamw(w_ref[...], g, m_ref[...], v_ref[...])
        g_ref[...] = g
        d_ref[...] = delta
        nm_ref[...] = nm
        nv_ref[...] = nv

    lay = pl.BlockSpec((None, tr, c), lambda t: (li, t, 0))
    stack = jax.ShapeDtypeStruct((DEPTH, r, c), F32)
    return pl.pallas_call(
        body, name=name, grid=(r // tr,),
        in_specs=[pl.BlockSpec((tr, c), lambda t: (t, 0)), pl.BlockSpec((3, tr, c), lambda t: (0, t, 0)),
                  lay, lay, lay, ANY, ANY, ANY, ANY],
        out_specs=[lay] * 4, out_shape=[stack] * 4,
        input_output_aliases={5: 0, 6: 1, 7: 2, 8: 3},
        compiler_params=_cparams(1))(own, recv, w, m, v, *prev)


def adamw_replicated(gathered, w, m, v, name):
    _, r, c = gathered.shape
    tr = next(t for t in (96, 88, 64, _PACK_TILE) if r % t == 0)

    def body(gs_ref, w_ref, m_ref, v_ref, g_ref, d_ref, nm_ref, nv_ref):
        g = gs_ref[0]
        for d in range(1, N_DEV):
            g = g + gs_ref[d]
        delta, nm, nv = _adamw(w_ref[...], g, m_ref[...], v_ref[...])
        g_ref[...] = g
        d_ref[...] = delta
        nm_ref[...] = nm
        nv_ref[...] = nv

    row = pl.BlockSpec((tr, c), lambda t: (t, 0))
    return pl.pallas_call(
        body, name=name, grid=(r // tr,),
        in_specs=[pl.BlockSpec((N_DEV, tr, c), lambda t: (0, t, 0)), row, row, row],
        out_specs=[row] * 4, out_shape=[jax.ShapeDtypeStruct((r, c), F32)] * 4,
        compiler_params=_cparams(1))(gathered, w, m, v)


def adamw_plain(g, w, m, v, name):
    def body(g_ref, w_ref, m_ref, v_ref, d_ref, nm_ref, nv_ref):
        delta, nm, nv = _adamw(w_ref[...], g_ref[...], m_ref[...], v_ref[...])
        d_ref[...] = delta
        nm_ref[...] = nm
        nv_ref[...] = nv

    return pl.pallas_call(body, name=name, out_shape=[jax.ShapeDtypeStruct(w.shape, F32)] * 3)(g, w, m, v)


_PACK_LAYER = (("b_in", (N_PROJ,)), ("rpb", (N_HEADS, 2 * KH - 1, 2 * KW - 1)), ("pool_w", (4, PGD, PGD)),
               ("pool_scale", (D_POOL,)), ("ln1_g", (D,)), ("ln1_b", (D,)), ("conv_b", (D_FF,)), ("ln2_g", (D,)),
               ("ln2_b", (D,)), ("conv_w", (3, D_FF)))
_PACK_INPUT = (("ln_in_g", (D,)), ("ln_in_b", (D,)))
_PACK_LANES = 1024
_PACK_TILE = 8
_EARLY = tuple(range(1, DEPTH))


def _pack_items(layers):
    items = [(n, (len(layers),) + s) for n, s in _PACK_LAYER]
    return items + ([(n, s) for n, s in _PACK_INPUT] if 0 in layers else [])


def _pack_rows(shape):
    return -(-int(np.prod(shape)) // _PACK_LANES)


def _pack(parts, layers):
    rows, used = [], 0
    for name, shape in _pack_items(layers):
        arr = parts[name] if (name, shape) in _PACK_INPUT else jnp.stack([parts[name][li] for li in layers])
        flat = arr.reshape(-1).astype(F32)
        rows.append(jnp.pad(flat, (0, _pack_rows(shape) * _PACK_LANES - flat.shape[0])))
        used += _pack_rows(shape)
    total = -(-used // _PACK_TILE) * _PACK_TILE
    rows.append(jnp.zeros(((total - used) * _PACK_LANES,), F32))
    return jnp.concatenate(rows).reshape(total, _PACK_LANES)


def _unpack(packed, layers):
    out, r0 = {}, 0
    for name, shape in _pack_items(layers):
        n, nr = int(np.prod(shape)), _pack_rows(shape)
        out[name] = packed[r0:r0 + nr].reshape(-1)[:n].reshape(shape)
        r0 += nr
    return out


def _bias_tables(rpb_l):
    qc = np.arange(GRID_W)[:, None]
    kc = np.arange(GRID_W)[None, :]
    start = np.clip(qc - KW // 2, 0, GRID_W - KW)
    valid = (kc >= start) & (kc < start + KW)
    col = np.clip(kc - qc, -(KW - 1), KW - 1) + KW - 1
    onehot = (col.reshape(-1)[None, :] == np.arange(2 * KW - 1)[:, None]).astype(np.float32)
    rows = jnp.pad(rpb_l, ((0, 0), (0, 1), (0, 0)))
    tab = jnp.einsum("hij,jm->him", rows, jnp.asarray(onehot), precision=lax.Precision.HIGHEST)
    tab = tab.reshape(N_HEADS, KROWS, GRID_W, GRID_W).transpose(0, 2, 1, 3)
    ok = valid[None, :, None, :] & (np.arange(KROWS) < 2 * KH - 1)[None, None, :, None]
    tab = jnp.where(jnp.asarray(ok), tab, NEG_INF).reshape(N_HEADS, GRID_W, KB)
    tab = jnp.stack([tab, jnp.roll(tab, GRID_W, axis=-1)], axis=1)
    return tab, tab[:, :, ::-1, :]


_SHARDED = ("w_in", "w_attn_out", "w_pool_out", "w_mix_out", "w_up", "w_down", "w_ple_gate", "w_ple_proj")
_NAMES = ("ln_in_g", "ln_in_b", "w_in", "b_in", "rpb", "w_attn_out", "pool_w", "pool_scale", "w_pool_out", "w_mix_out",
          "ln1_g", "ln1_b", "w_up", "conv_w", "conv_b", "w_down", "w_ple_gate", "w_ple_proj", "ln2_g", "ln2_b")


def kernel(x, p, ln_in_g, ln_in_b, w_in, b_in, rpb, w_attn_out, pool_w, pool_scale, w_pool_out, w_mix_out, ln1_g, ln1_b, w_up, conv_w, conv_b, w_down, w_ple_gate, w_ple_proj, ln2_g, ln2_b, loss_target, m_ln_in_g, m_ln_in_b, m_w_in, m_b_in, m_rpb, m_w_attn_out, m_pool_w, m_pool_scale, m_w_pool_out, m_w_mix_out, m_ln1_g, m_ln1_b, m_w_up, m_conv_w, m_conv_b, m_w_down, m_w_ple_gate, m_w_ple_proj, m_ln2_g, m_ln2_b, v_ln_in_g, v_ln_in_b, v_w_in, v_b_in, v_rpb, v_w_attn_out, v_pool_w, v_pool_scale, v_w_pool_out, v_w_mix_out, v_ln1_g, v_ln1_b, v_w_up, v_conv_w, v_conv_b, v_w_down, v_w_ple_gate, v_w_ple_proj, v_ln2_g, v_ln2_b):
    a = dict(locals())
    W = {n: a[n] for n in _NAMES}
    M = {n: a["m_" + n] for n in _NAMES}
    V = {n: a["v_" + n] for n in _NAMES}
    xi, yi, ci = _coords()
    me = _dev_index(xi, yi, ci)
    x2, tgt = x[0], loss_target[0]
    pb = p[:, 0].astype(BF16)

    flip = lambda d: {**d, "w_up": d["w_up"].transpose(0, 2, 1)}
    ex = _Exchange(flip(W), flip(M), flip(V))
    loss_part, dx, parts = _local_step(x2, tgt, pb, W, ex)
    loss = lax.psum(loss_part[0, 0], AXES)

    started = ex.replicated_start("late", _pack(parts, (0,)), dx)
    done = ex.update(range(DEPTH - 1, 0, -1), started)
    ex.replicated_forward("late", done)
    done = ex.update((0,), done)
    stacks = {**ex.stacks, "w_up": [t.transpose(0, 2, 1) for t in ex.stacks["w_up"]]}
    zero_cw = jnp.zeros((DEPTH, 3, D_FF), F32)
    halves = []
    for layers, tag in (((0,), "late"), (_EARLY, "early")):
        gath = ex.replicated_finish(tag, done)
        packs = [_pack({**src, "conv_w": zero_cw}, layers) for src in (W, M, V)]
        halves.append([_unpack(o, layers) for o in adamw_replicated(gath, *packs, f"adamw_replicated_{tag}")])
    outs = [{**{n: jnp.concatenate([lo[n], hi[n]]) for n, _ in _PACK_LAYER}, **{n: lo[n] for n, _ in _PACK_INPUT}}
            for lo, hi in zip(*halves)]
    g_cw = lax.dynamic_slice_in_dim(outs[0]["conv_w"], me * FF_SHARD, FF_SHARD, axis=2)
    flat = lambda t: t.reshape(DEPTH * 3, FF_SHARD)
    cw_out = [o.reshape(DEPTH, 3, FF_SHARD) for o in
              adamw_plain(flat(g_cw), flat(conv_w), flat(m_conv_w), flat(v_conv_w), "adamw_conv_w")]
    res = []
    for k in range(4):
        d = {n: stacks[n][k] for n in _SHARDED}
        d.update({n: outs[k][n] for n in outs[k] if n != "conv_w"})
        d["conv_w"] = g_cw if k == 0 else cw_out[k - 1]
        res.append(d)
    return (loss, dx[None], *[res[k][n] for k in range(4) for n in _NAMES])


class _Exchange:
    GROUPS = (("w_ple_gate", "w_ple_proj", "w_down", "w_up"), ("w_mix_out", "w_attn_out", "w_pool_out"), ("w_in",))
    FIRST = ("w_in",)

    def __init__(self, W, M, V):
        self.W, self.M, self.V = W, M, V
        xi, yi, ci = _coords()
        me = _dev_index(xi, yi, ci)
        self.me = me.astype(I32).reshape(1)
        self.rel_idx = jnp.stack([_dev_index(xi ^ (k & 1), yi ^ (k >> 1), ci) for k in range(4)]).astype(I32)
        self.lands = [{n: lax.dynamic_update_index_in_dim(lax.empty((N_DEV,) + W[n].shape[1:], BF16),
                                                          W[n][li].astype(BF16), me, 0) for n in _SHARDED}
                      for li in range(DEPTH)]
        (cw,) = all_gather([W["conv_w"]], "ag_conv_w")
        self.cw = cw.transpose(1, 2, 0, 3).reshape(DEPTH, 3, 4, FF_BLK).transpose(0, 2, 1, 3)
        self.ag, self.fwd, self.rs, self.pending, self.small = {}, {}, {}, {}, {}
        self.stacks = {n: [lax.empty((DEPTH,) + W[n].shape[1:], F32) for _ in range(4)] for n in _SHARDED}
        self.late = tuple(n for n in _SHARDED if n not in self.FIRST)
        self.ag[0] = ag_start([self.lands[0][n] for n in self.FIRST], cw, "ag_start0")

    def tokens(self):
        return [self.ag[0][3]]

    def prefetch(self, li, after):
        send, recv, lands, _ = self.ag[li]
        self.fwd[li] = ag_forward(send, recv, lands, after, f"ag_forward{li}")
        if li == 0:
            self.ag["0b"] = ag_start([self.lands[0][n] for n in self.late], self.fwd[0][2][0], "ag_start0b")

    def weights(self, li, after):
        send, recv, _, _ = self.ag.pop(li)
        fsend, frecv, lands = self.fwd.pop(li)
        lands = ag_finish(send, recv, fsend, frecv, lands, after, f"ag_finish{li}")
        if li == 0:
            return dict(zip(self.FIRST, lands)), self.cw[li], (self.ag["0b"][3],)
        tokens = ()
        if li + 1 < DEPTH:
            self.ag[li + 1] = ag_start([self.lands[li + 1][n] for n in _SHARDED], lands[0], f"ag_start{li + 1}")
            tokens = (self.ag[li + 1][3],)
        return dict(zip(_SHARDED, lands)), self.cw[li], tokens

    def rest(self, li, G, mid, after):
        if li != 0:
            return G, ()
        send, recv, lands, _ = self.ag.pop("0b")
        fsend, frecv, lands = ag_forward(send, recv, lands, mid, "ag_forward0b")
        lands = ag_finish(send, recv, fsend, frecv, lands, after, "ag_finish0b")
        self.ag[1] = ag_start([self.lands[1][n] for n in _SHARDED], lands[0], "ag_start1")
        return {**G, **dict(zip(self.late, lands))}, (self.ag[1][3],)

    def grads(self, li, group, gw):
        self.pending.setdefault(li, {}).update(gw)
        if li != 0 and group != len(self.GROUPS) - 1:
            return None
        gw = self.pending.pop(li)
        tag = f"{li}_{group}" if li == 0 else f"{li}"
        send, recv, glist, lands, token = d2d_start(list(gw.values()), f"d2d_start{tag}")
        self.d2d = (tag, tuple(gw), send, recv, glist, lands)
        return token

    def flush(self, li, group, after):
        if li != 0 and group != len(self.GROUPS) - 1:
            return None
        tag, names, send, recv, glist, lands = self.d2d
        glist, recv1 = d2d_finish(send, recv, glist, lands, after, f"d2d_finish{tag}")
        sums = [pair_add(self.rel_idx, g, r1, f"pair_add_{n}{li}") for n, g, r1 in zip(names, glist, recv1)]
        send, recv, psums, lands, token = rs_start([s_[1] for s_ in sums], f"rs_start{tag}")
        self.rs.setdefault(li, []).append((tag, names, send, recv, psums, lands, [s_[0] for s_ in sums]))
        if li == 0 and group == 1 and "early" in self.small:
            self.replicated_forward("early", token)
        return token

    def update(self, layers, after):
        for li in layers:
            for tag, names, send, recv, psums, lands, owns in self.rs.pop(li):
                recv2 = rs_finish(send, recv, psums, lands, after, f"rs_finish{tag}")
                for n, own, r2 in zip(names, owns, recv2):
                    self.stacks[n] = adamw_shard(own, r2, self.W[n], self.M[n], self.V[n], li, self.stacks[n],
                                                 f"adamw_{n}{li}")
                    after = self.stacks[n][0]
        return after

    def replicated_start(self, tag, pack, after):
        land = lax.dynamic_update_index_in_dim(lax.empty((N_DEV,) + pack.shape, F32), pack, self.me[0], 0)
        self.small[tag] = ag_start([land], after, f"ag_start_small_{tag}")
        return self.small[tag][3]

    def replicated_early(self, small, after):
        return self.replicated_start("early", _pack(small, _EARLY), after)

    def replicated_forward(self, tag, after):
        send, recv, lands, _ = self.small[tag]
        self.small[tag] = (send, recv) + ag_forward(send, recv, lands, after, f"ag_forward_small_{tag}")

    def replicated_finish(self, tag, after):
        send, recv, fsend, frecv, lands = self.small.pop(tag)
        return ag_finish(send, recv, fsend, frecv, lands, after, f"ag_finish_small_{tag}")[0]


def _local_step(x2, tgt, pb, W, ex):
    depth = W["rpb"].shape[0]
    vec = lambda t: t.reshape(1, -1)
    ln1_g, ln1_b, ln2_g, ln2_b = W["ln1_g"], W["ln1_b"], W["ln2_g"], W["ln2_b"]
    b_in, rpb, pool_scale = W["b_in"], W["rpb"], W["pool_scale"]
    cb_full = W["conv_b"].reshape(depth, 4, 1, FF_BLK)
    pool_w_b = W["pool_w"].astype(BF16)

    h, hb = ln_fwd(x2, vec(W["ln_in_g"]), vec(W["ln_in_b"]), "ln_in", after=ex.tokens())
    ex.prefetch(0, hb)
    saved = []
    for li in range(depth):
        G, cw, tokens = ex.weights(li, hb)
        e_tab, e_rev = _bias_tables(rpb[li])
        bias = vec(b_in[li])
        proj = proj_fwd(hb, G["w_in"], bias, li, 0, N_DEV, BF16, f"proj{li}", after=tokens)
        u = proj_fwd(hb, G["w_in"], bias, li, 3, 1, F32, f"proj_u{li}")
        att = attn_fwd(proj, e_tab, f"attn{li}")
        pm, pw = pool_fwd(u, pool_w_b[li], vec(pool_scale[li]), f"pool{li}")
        G, tokens = ex.rest(li, G, att, pw)
        mg, ya, yp = merge_fwd(att, pw, G["w_attn_out"], G["w_pool_out"], proj, li, f"merge{li}", after=tokens)
        if li + 1 < depth:
            ex.prefetch(li + 1, mg)
        z1, h1, h1b = mix_ln_fwd(mg, G["w_mix_out"], h, vec(ln1_g[li]), vec(ln1_b[li]), li, f"mix_ln{li}")
        up = up_fwd(h1b, G["w_up"], li, f"up{li}")
        t = ffn_act_fwd(up, cw, cb_full[li], f"ffn_act{li}")
        z2, h2, h2b, pg, pp = down_ple_ln_fwd(t, G["w_down"], h1b, G["w_ple_gate"], pb[li], G["w_ple_proj"], h1,
                                              vec(ln2_g[li]), vec(ln2_b[li]), li, f"down_ln{li}")
        saved.append(dict(hb=hb, proj=proj, att=att, pm=pm, pw=pw, mg=mg, ya=ya, yp=yp, z1=z1, h1b=h1b, up=up, t=t,
                          z2=z2, pg=pg, pp=pp, e_rev=e_rev, G=G, cw=cw))
        h, hb = h2, h2b

    dh, loss_part = loss_bwd(h, tgt, "loss")
    small = {n: [None] * depth for n in ("b_in", "rpb", "pool_w", "pool_scale", "ln1_g", "ln1_b", "conv_b", "ln2_g",
                                         "ln2_b", "conv_w")}
    token = ()
    tok = lambda t: () if t is None else (t,)
    for li in reversed(range(depth)):
        sv = saved[li]
        G, cw = sv["G"], sv["cw"]
        dz2, dz2b, dpg, dpp, dg2, db2 = ln2_ple_bwd(dh, sv["z2"], vec(ln2_g[li]), sv["pg"], sv["pp"], f"ln2_bwd{li}",
                                                    after=token)
        gw = {}
        gw["w_ple_gate"] = wgrad_rows(sv["h1b"], dpg, f"dw_pg{li}")
        gw["w_ple_proj"] = wgrad_cols(pb[li], dpp, f"dw_pp{li}")
        gw["w_down"] = wgrad_down(sv["t"], dz2b, f"dw_down{li}").reshape(N_DEV, FF_SHARD, D)
        dhv, dhg, dcw, dcb = ffn_act_bwd(dz2b, G["w_down"], sv["up"], cw, cb_full[li], li, f"ffn_bwd{li}")
        gw["w_up"] = wgrad_up(sv["h1b"], dhv, dhg, f"dw_up{li}")
        token = tok(ex.grads(li, 0, gw))
        dz1, dz1b, dg1, db1 = dh1_ln1_bwd(dz2, dpg, G["w_ple_gate"], dhv, dhg, G["w_up"], sv["z1"], vec(ln1_g[li]), li,
                                          f"ln1_bwd{li}", after=token)
        token = tok(ex.flush(li, 0, dz1b))
        gw = {"w_mix_out": wgrad_rows(sv["mg"], dz1b, f"dw_mix{li}", after=token)}
        dya, dyp, dga, dgb = merge_bwd(dz1b, G["w_mix_out"], sv["proj"], sv["ya"], sv["yp"], li, f"merge_bwd{li}")
        gw["w_attn_out"] = wgrad_cols(sv["att"], dya, f"dw_ao{li}")
        gw["w_pool_out"] = wgrad_cols(sv["pw"], dyp, f"dw_po{li}")
        token = tok(ex.grads(li, 1, gw))
        da = attn_out_bwd(dya, G["w_attn_out"], li, f"da{li}", after=token)
        du, dpool_w, dpool_sc = pool_bwd(dyp, G["w_pool_out"], sv["pm"], pool_w_b[li], vec(pool_scale[li]), li,
                                         f"pool_bwd{li}")
        token = tok(ex.flush(li, 1, du))
        dq, dk, dv, drpb = attn_bwd(sv["proj"], da, sv["e_rev"], f"attn_bwd{li}", after=token)
        dproj = jnp.concatenate([dq, dk, dv, du, dga, dgb], axis=1)
        dw_in, db_in = wgrad_cols(sv["hb"], dproj, f"dw_in{li}", with_colsum=True)
        token = tok(ex.grads(li, 2, {"w_in": dw_in}))
        dh = dh0_bwd(dz1, dproj, G["w_in"], li, f"dh0{li}", after=token)
        small["b_in"][li] = db_in.reshape(N_PROJ)
        small["rpb"][li] = drpb.reshape(N_HEADS, KROWS, GRID_W)[:, :2 * KH - 1, :2 * KW - 1]
        small["pool_w"][li] = dpool_w
        small["pool_scale"][li] = dpool_sc.reshape(D_POOL)
        small["ln1_g"][li], small["ln1_b"][li] = dg1.reshape(D), db1.reshape(D)
        small["ln2_g"][li], small["ln2_b"][li] = dg2.reshape(D), db2.reshape(D)
        small["conv_b"][li] = dcb.reshape(D_FF)
        small["conv_w"][li] = dcw.transpose(1, 0, 2).reshape(3, D_FF)
        token = tok(ex.flush(li, 2, dh))
        if li == 1:
            token = token + tok(ex.replicated_early(small, dh))
    dx, dg_in, db_in0 = ln_bwd(dh, x2, vec(W["ln_in_g"]), "ln_in_bwd", after=token)
    parts = {n: jnp.stack(v_) for n, v_ in small.items()}
    parts["ln_in_g"], parts["ln_in_b"] = dg_in.reshape(D), db_in0.reshape(D)
    return loss_part, dx, parts
```

```python
import numpy as np
import jax
import jax.numpy as jnp
from jax import lax
from jax.experimental import pallas as pl
from jax.experimental.pallas import tpu as pltpu

F32 = jnp.float32
BF16 = jnp.bfloat16
I32 = jnp.int32

D = 1024
DEPTH = 4
GRID_W = 64
N_HEADS = 8
HEAD_DIM = 64
D_ATTN = 512
KH = 8
KW = 16
POOL_WINDOWS = (2, 4, 8, 16)
D_POOL = 512
PGD = 128
D_FF = 2816
PLE_DIM = 256
N_PROJ = 4096
ALPHA = (2 * DEPTH) ** 0.25
LN_EPS = 1e-5
NEG_INF = -1e30
ATT_SCALE = HEAD_DIM ** -0.5
ADAM_LR = 0.001
ADAM_B1 = 0.9
ADAM_B2 = 0.999
ADAM_EPS = 1e-08
ADAM_WD = 0.01
ADAM_STEP = 10

N_DEV = 8
AXES = ("x", "y", "c")
FF_BLK = D_FF // 4
FF_SHARD = D_FF // N_DEV
QROWS = 8
KROWS = 16
QB = QROWS * GRID_W
KB = KROWS * GRID_W
V7X_VMEM_LIMIT = 56 * 2 ** 20
MESH = pl.DeviceIdType.MESH
ANY = pl.BlockSpec(memory_space=pl.ANY)


def _cparams(n_grid):
    return pltpu.CompilerParams(dimension_semantics=("arbitrary",) * n_grid, vmem_limit_bytes=V7X_VMEM_LIMIT)


def _nn(a, b):
    return lax.dot_general(a, b, (((1,), (0,)), ((), ())), preferred_element_type=F32)


def _nt(a, b):
    return lax.dot_general(a, b, (((1,), (1,)), ((), ())), preferred_element_type=F32)


def _tn(a, b):
    return lax.dot_general(a, b, (((0,), (0,)), ((), ())), preferred_element_type=F32)


def _sigmoid(x):
    return 1.0 / (1.0 + jnp.exp(-x))


def _ln_fwd(z, g, b):
    mu = jnp.mean(z, axis=-1, keepdims=True)
    xc = z - mu
    var = jnp.mean(xc * xc, axis=-1, keepdims=True)
    return xc * lax.rsqrt(var + LN_EPS) * g + b


def _ln_bwd(dh, z, g):
    mu = jnp.mean(z, axis=-1, keepdims=True)
    xc = z - mu
    var = jnp.mean(xc * xc, axis=-1, keepdims=True)
    rstd = lax.rsqrt(var + LN_EPS)
    xhat = xc * rstd
    dxh = dh * g
    m1 = jnp.mean(dxh, axis=-1, keepdims=True)
    m2 = jnp.mean(dxh * xhat, axis=-1, keepdims=True)
    return rstd * (dxh - m1 - xhat * m2), dh * xhat


def _colsum(x):
    return jnp.sum(x, axis=0, keepdims=True)


def _lane_cat(ref):
    return jnp.concatenate([ref[j] for j in range(ref.shape[0])], axis=1)


def _row_cat(ref):
    n, r, c = ref.shape
    return ref[...].reshape(n * r, c)


def _shards(n, r, c, li, j_of=None):
    del li
    if j_of is None:
        return pl.BlockSpec((n, r, c), lambda *_: (0, 0, 0))
    return pl.BlockSpec((n, r, c), lambda *g: (j_of(*g), 0, 0))


def _shard(r, c, li, j_of):
    del li
    return pl.BlockSpec((None, r, c), lambda *g: (j_of(*g), 0, 0))


def ln_fwd(x, g, b, name, after=()):
    s = x.shape[0]
    tm = 512
    na = len(after)

    def body(x_ref, g_ref, b_ref, *rest):
        h_ref, hb_ref = rest[na:]
        h = _ln_fwd(x_ref[...], g_ref[...], b_ref[...])
        h_ref[...] = h
        hb_ref[...] = h.astype(BF16)

    row = pl.BlockSpec((tm, D), lambda i: (i, 0))
    vec = pl.BlockSpec((1, D), lambda i: (0, 0))
    return pl.pallas_call(
        body, name=name, grid=(s // tm,), in_specs=[row, vec, vec] + [ANY] * na, out_specs=[row, row],
        out_shape=[jax.ShapeDtypeStruct((s, D), F32), jax.ShapeDtypeStruct((s, D), BF16)],
        compiler_params=_cparams(1))(x, g, b, *after)


def proj_fwd(hb, win, bias, li, name, after=()):
    s = hb.shape[0]
    bn = N_PROJ // N_DEV
    tm = 1024
    pool_shard = (3 * D_ATTN) // bn

    def body(a_ref, w_ref, b_ref, *rest):
        o_ref, u_ref = rest[-2:]
        acc = _nn(a_ref[...], w_ref[...]) + b_ref[...]
        o_ref[...] = acc.astype(BF16)

        @pl.when(pl.program_id(1) == pool_shard)
        def _():
            u_ref[...] = acc

    return pl.pallas_call(
        body, name=name, grid=(s // tm, N_DEV),
        in_specs=[pl.BlockSpec((tm, D), lambda i, j: (i, 0)),
                  _shard(D, bn, li, lambda i, j: j),
                  pl.BlockSpec((1, bn), lambda i, j: (0, j))] + [ANY] * len(after),
        out_specs=[pl.BlockSpec((tm, bn), lambda i, j: (i, j)), pl.BlockSpec((tm, bn), lambda i, j: (i, 0))],
        out_shape=[jax.ShapeDtypeStruct((s, N_PROJ), BF16), jax.ShapeDtypeStruct((s, D_POOL), F32)],
        compiler_params=_cparams(2))(hb, win, bias, *after)


def _attn_types(b, nb):
    first, last = 0, (nb * QROWS - KROWS) * GRID_W
    mid = pl.multiple_of((QROWS * b - KH // 2) * GRID_W, 256)
    return ((b == 0, first), ((b > 0) & (b < nb - 1), mid), (b == nb - 1, last))


def _attn_row(btype, qr):
    lo, delta = ((max(qr - KH // 2, 0), 0), (qr, -(KH // 2)), (min(qr + KH // 2, KH), -KH))[btype]
    return lo, (qr - delta - (KH - 1)) % KROWS, lo - qr + delta + KH - 1


def _row_window(lo):
    pad = (lo % 2) * GRID_W
    return (lo // 2) * 128, KH * GRID_W + 2 * pad, pad


def _lanes(ref, start, width):
    start %= KB
    if start + width <= KB:
        return ref[:, start:start + width]
    return jnp.concatenate([ref[:, start:], ref[:, :start + width - KB]], axis=1)


def _row_logits(s_ref, e_ref, hh, rows, btype, qr):
    lo, shift, _ = _attn_row(btype, qr)
    a0, w, pad = _row_window(lo)
    e = e_ref.at[hh, shift % 2]
    sb = s_ref[rows, a0:a0 + w] + _lanes(e, a0 - (shift - shift % 2) * GRID_W, w)
    if pad:
        lane = lax.broadcasted_iota(I32, (1, w), 1)
        sb = jnp.where((lane >= pad) & (lane < w - pad), sb, NEG_INF)
    return sb, a0, w, pad


def _store_row(ref, rows, a0, w, val):
    if a0:
        ref[rows, 0:a0] = jnp.zeros((GRID_W, a0), ref.dtype)
    ref[rows, a0:a0 + w] = val.astype(ref.dtype)
    if a0 + w < KB:
        ref[rows, a0 + w:KB] = jnp.zeros((GRID_W, KB - a0 - w), ref.dtype)


def attn_fwd(proj, e_tab, li, name):
    s = proj.shape[0]
    nb = s // QB

    def body(q_ref, k_ref, v_ref, e_ref, o_ref, s_ref, p_ref):
        q = q_ref[...] * ATT_SCALE
        lane = lax.broadcasted_iota(I32, (1, 128), 1)

        def block(btype, k0):
            kwin = k_ref[pl.ds(k0, KB), :]
            vwin = v_ref[pl.ds(k0, KB), :]
            acc = jnp.zeros((QB, 128), F32)
            for hh in range(2):
                lm = (lane // HEAD_DIM) == hh
                qh = jnp.where(lm, q, jnp.zeros_like(q))
                vh = jnp.where(lm, vwin, jnp.zeros_like(vwin))
                s_ref[...] = _nt(qh, kwin)
                for qr in range(QROWS):
                    rows = slice(qr * GRID_W, (qr + 1) * GRID_W)
                    sb, a0, w, _ = _row_logits(s_ref, e_ref, hh, rows, btype, qr)
                    p = jnp.exp(sb - jnp.max(sb, axis=1, keepdims=True))
                    _store_row(p_ref, rows, a0, w, p * (1.0 / jnp.sum(p, axis=1, keepdims=True)))
                acc = acc + _nn(p_ref[...], vh)
            o_ref[...] = acc.astype(BF16)

        for btype, (cond, k0) in enumerate(_attn_types(pl.program_id(1), nb)):
            pl.when(cond)(lambda btype=btype, k0=k0: block(btype, k0))

    return pl.pallas_call(
        body, name=name, grid=(4, nb),
        in_specs=[pl.BlockSpec((QB, 128), lambda j, b: (b, j)),
                  pl.BlockSpec((s, 128), lambda j, b: (0, 4 + j)),
                  pl.BlockSpec((s, 128), lambda j, b: (0, 8 + j)),
                  pl.BlockSpec((None, 2, 2, GRID_W, KB), lambda j, b: (li, j, 0, 0, 0))],
        out_specs=pl.BlockSpec((QB, 128), lambda j, b: (b, j)),
        out_shape=jax.ShapeDtypeStruct((s, D_ATTN), BF16),
        scratch_shapes=[pltpu.VMEM((QB, KB), F32), pltpu.VMEM((QB, KB), BF16)],
        compiler_params=_cparams(2))(proj, proj, proj, e_tab)


_POOL_PAD = 8


def _pool_counts(s, w):
    t = lax.broadcasted_iota(I32, (s, 1), 0)
    return (jnp.minimum(t + w // 2, s) - jnp.maximum(t - w // 2, 0)).astype(F32)


def _window_sum(x, w, back_first):
    s = x.shape[0]
    z = jnp.zeros((_POOL_PAD, x.shape[1]), F32)
    xe = jnp.concatenate([z, x, z], axis=0)
    n = s + 2 * _POOL_PAD
    acc = xe + pltpu.roll(xe, 1 if back_first else n - 1, 0)
    k = 1
    while 2 * k < w:
        acc = pltpu.roll(acc, k, 0) + pltpu.roll(acc, n - k, 0)
        k *= 2
    return acc[_POOL_PAD:_POOL_PAD + s, :]


def pool_fwd(u, pool_w, pool_scale, name):
    s = u.shape[0]

    def body(u_ref, w_ref, sc_ref, pm_ref, pw_ref):
        for g, w in enumerate(POOL_WINDOWS):
            cols = slice(g * PGD, (g + 1) * PGD)
            ug = u_ref[:, cols]
            pm = (_window_sum(ug, w, True) / _pool_counts(s, w) - ug).astype(BF16)
            pm_ref[:, cols] = pm
            pw_ref[:, cols] = (_nn(pm, w_ref[g]) * sc_ref[:, cols]).astype(BF16)

    full = lambda shape: pl.BlockSpec(shape, lambda i: (0,) * len(shape))
    return pl.pallas_call(
        body, name=name, grid=(1,),
        in_specs=[full((s, D_POOL)), full((4, PGD, PGD)), full((1, D_POOL))],
        out_specs=[full((s, D_POOL)), full((s, D_POOL))],
        out_shape=[jax.ShapeDtypeStruct((s, D_POOL), BF16)] * 2,
        compiler_params=_cparams(1))(u, pool_w, pool_scale)


def merge_fwd(a, pw, wao, wpo, proj, li, name, after=()):
    s = a.shape[0]
    tm, tn = 512, 512
    nt = D // tn
    per = tn // 128

    def body(a_ref, pw_ref, wa_ref, wp_ref, ga_ref, gb_ref, *rest):
        mg_ref, ya_ref, yp_ref = rest[len(after):]
        ya = _nn(a_ref[...], _lane_cat(wa_ref))
        yp = _nn(pw_ref[...], _lane_cat(wp_ref))
        mg = _sigmoid(ga_ref[...].astype(F32)) * ya + _sigmoid(gb_ref[...].astype(F32)) * yp
        mg_ref[...] = mg.astype(BF16)
        ya_ref[...] = ya.astype(BF16)
        yp_ref[...] = yp.astype(BF16)

    act = pl.BlockSpec((tm, D_ATTN), lambda i, j: (i, 0))
    wsp = _shards(per, D_ATTN, 128, li, lambda i, j: j)
    out = pl.BlockSpec((tm, tn), lambda i, j: (i, j))
    ga0 = (3 * D_ATTN + D_POOL) // tn
    return pl.pallas_call(
        body, name=name, grid=(s // tm, nt),
        in_specs=[act, act, wsp, wsp,
                  pl.BlockSpec((tm, tn), lambda i, j: (i, ga0 + j)),
                  pl.BlockSpec((tm, tn), lambda i, j: (i, ga0 + nt + j))] + [ANY] * len(after),
        out_specs=[out, out, out],
        out_shape=[jax.ShapeDtypeStruct((s, D), BF16)] * 3,
        compiler_params=_cparams(2))(a, pw, wao, wpo, proj, proj, *after)


def mix_ln_fwd(mg, wmix, h0, g, b, li, name):
    s = mg.shape[0]
    tm = 256

    def body(mg_ref, w_ref, h0_ref, g_ref, b_ref, z_ref, h_ref, hb_ref):
        z = ALPHA * h0_ref[...] + _nn(mg_ref[...], _row_cat(w_ref))
        h = _ln_fwd(z, g_ref[...], b_ref[...])
        z_ref[...] = z
        h_ref[...] = h
        hb_ref[...] = h.astype(BF16)

    row = pl.BlockSpec((tm, D), lambda i: (i, 0))
    vec = pl.BlockSpec((1, D), lambda i: (0, 0))
    return pl.pallas_call(
        body, name=name, grid=(s // tm,),
        in_specs=[row, _shards(N_DEV, D // N_DEV, D, li), row, vec, vec],
        out_specs=[row, row, row],
        out_shape=[jax.ShapeDtypeStruct((s, D), F32), jax.ShapeDtypeStruct((s, D), F32),
                   jax.ShapeDtypeStruct((s, D), BF16)],
        compiler_params=_cparams(1))(mg, wmix, h0, g, b)


def up_fwd(hb, wup, li, name):
    s = hb.shape[0]
    tm = 1024

    def body(a_ref, w_ref, o_ref):
        o_ref[...] = _nt(a_ref[...], w_ref[...]).astype(BF16)

    return pl.pallas_call(
        body, name=name, grid=(s // tm, N_DEV),
        in_specs=[pl.BlockSpec((tm, D), lambda i, j: (i, 0)), _shard(FF_BLK, D, li, lambda i, j: j)],
        out_specs=pl.BlockSpec((None, tm, FF_BLK), lambda i, j: (j, i, 0)),
        out_shape=jax.ShapeDtypeStruct((N_DEV, s, FF_BLK), BF16),
        compiler_params=_cparams(2))(hb, wup)


_SQRT_HALF = 0.7071067811865476
_INV_SQRT_2PI = 0.3989422804014327


def _shift_rows(x, prev_row, next_row):
    n = x.shape[0]
    r = lax.broadcasted_iota(I32, (n, 1), 0)
    back = jnp.where(r == 0, prev_row, pltpu.roll(x, 1, 0))
    fwd = jnp.where(r == n - 1, next_row, pltpu.roll(x, n - 1, 0))
    return back, fwd


HALO = 16


def _halo_maps(tm, s):
    th = tm // HALO
    return (lambda i: jnp.maximum(i * th - 1, 0)), (lambda i: jnp.minimum((i + 1) * th, s // HALO - 1))


def _slab_specs(tm, s, blk_of):
    before, after = _halo_maps(tm, s)
    main = pl.BlockSpec((None, tm, FF_BLK), lambda c, i: (blk_of(c), i, 0))
    prev = pl.BlockSpec((None, HALO, FF_BLK), lambda c, i: (blk_of(c), before(i), 0))
    nxt = pl.BlockSpec((None, HALO, FF_BLK), lambda c, i: (blk_of(c), after(i), 0))
    return main, prev, nxt


def ffn_act_fwd(up, conv_w, conv_b, name):
    s = up.shape[1]
    tm = 512
    nt = s // tm
    hv_main, _, _ = _slab_specs(tm, s, lambda c: c)
    hg_main, hg_prev, hg_next = _slab_specs(tm, s, lambda c: 4 + c)

    def body(hv_ref, hg_ref, hp_ref, hn_ref, cw_ref, cb_ref, t_ref):
        i = pl.program_id(1)
        hg = hg_ref[...].astype(F32)
        prow = jnp.where(i == 0, 0.0, hp_ref[...].astype(F32)[HALO - 1:HALO, :])
        nrow = jnp.where(i == nt - 1, 0.0, hn_ref[...].astype(F32)[0:1, :])
        back, fwd = _shift_rows(hg, prow, nrow)
        c = back * cw_ref[0:1, :] + hg * cw_ref[1:2, :] + fwd * cw_ref[2:3, :] + cb_ref[...]
        act = 0.5 * c * (1.0 + lax.erf(c * _SQRT_HALF))
        t_ref[...] = (act * hv_ref[...].astype(F32)).astype(BF16)

    return pl.pallas_call(
        body, name=name, grid=(4, nt),
        in_specs=[hv_main, hg_main, hg_prev, hg_next,
                  pl.BlockSpec((None, 3, FF_BLK), lambda c, i: (c, 0, 0)),
                  pl.BlockSpec((None, 1, FF_BLK), lambda c, i: (c, 0, 0))],
        out_specs=pl.BlockSpec((None, tm, FF_BLK), lambda c, i: (c, i, 0)),
        out_shape=jax.ShapeDtypeStruct((4, s, FF_BLK), BF16),
        compiler_params=_cparams(2))(up, up, up, up, conv_w, conv_b)


def down_ple_ln_fwd(t, wdown, hb, wpg, pb, wpp, h1, g, b, li, name):
    s = hb.shape[0]
    tm = 256

    def body(t_ref, wd_ref, hb_ref, wpg_ref, p_ref, wpp_ref, h1_ref, g_ref, b_ref,
             z_ref, h_ref, hbo_ref, pg_ref, pp_ref):
        wd = _row_cat(wd_ref)
        ffn = _nn(t_ref[0], wd[0:FF_BLK, :])
        for c in range(1, 4):
            ffn = ffn + _nn(t_ref[c], wd[c * FF_BLK:(c + 1) * FF_BLK, :])
        pg = _nn(hb_ref[...], _row_cat(wpg_ref))
        pp = _nn(p_ref[...], _lane_cat(wpp_ref))
        z = ALPHA * h1_ref[...] + ffn + _sigmoid(pg) * pp
        h = _ln_fwd(z, g_ref[...], b_ref[...])
        z_ref[...] = z
        h_ref[...] = h
        hbo_ref[...] = h.astype(BF16)
        pg_ref[...] = pg.astype(BF16)
        pp_ref[...] = pp.astype(BF16)

    row = pl.BlockSpec((tm, D), lambda i: (i, 0))
    vec = pl.BlockSpec((1, D), lambda i: (0, 0))
    return pl.pallas_call(
        body, name=name, grid=(s // tm,),
        in_specs=[pl.BlockSpec((4, tm, FF_BLK), lambda i: (0, i, 0)),
                  _shards(N_DEV, FF_SHARD, D, li),
                  row, _shards(N_DEV, D // N_DEV, D, li),
                  pl.BlockSpec((tm, PLE_DIM), lambda i: (i, 0)),
                  _shards(N_DEV, PLE_DIM, 128, li),
                  row, vec, vec],
        out_specs=[row] * 5,
        out_shape=[jax.ShapeDtypeStruct((s, D), F32), jax.ShapeDtypeStruct((s, D), F32),
                   jax.ShapeDtypeStruct((s, D), BF16), jax.ShapeDtypeStruct((s, D), BF16),
                   jax.ShapeDtypeStruct((s, D), BF16)],
        compiler_params=_cparams(1))(t, wdown, hb, wpg, pb, wpp, h1, g, b)


def loss_bwd(h, target, name):
    s = h.shape[0]
    tm = 512

    def body(h_ref, t_ref, dh_ref, l_ref):
        @pl.when(pl.program_id(0) == 0)
        def _():
            l_ref[...] = jnp.zeros_like(l_ref)
        e = h_ref[...] - t_ref[...]
        dh_ref[...] = e * (1.0 / D)
        l_ref[...] += 0.5 * jnp.sum(jnp.mean(e * e, axis=-1, keepdims=True), axis=0, keepdims=True)

    row = pl.BlockSpec((tm, D), lambda i: (i, 0))
    return pl.pallas_call(
        body, name=name, grid=(s // tm,), in_specs=[row, row],
        out_specs=[row, pl.BlockSpec((1, 1), lambda i: (0, 0))],
        out_shape=[jax.ShapeDtypeStruct((s, D), F32), jax.ShapeDtypeStruct((1, 1), F32)],
        compiler_params=_cparams(1))(h, target)


def ln_bwd(dh, z, g, name, after=()):
    s = dh.shape[0]
    tm = 512
    na = len(after)

    def body(dh_ref, z_ref, g_ref, *rest):
        dz_ref, dg_ref, db_ref = rest[na:]

        @pl.when(pl.program_id(0) == 0)
        def _():
            dg_ref[...] = jnp.zeros_like(dg_ref)
            db_ref[...] = jnp.zeros_like(db_ref)
        dh = dh_ref[...]
        dz, dgx = _ln_bwd(dh, z_ref[...], g_ref[...])
        dz_ref[...] = dz
        dg_ref[...] += _colsum(dgx)
        db_ref[...] += _colsum(dh)

    row = pl.BlockSpec((tm, D), lambda i: (i, 0))
    vec = pl.BlockSpec((1, D), lambda i: (0, 0))
    return pl.pallas_call(
        body, name=name, grid=(s // tm,), in_specs=[row, row, vec] + [ANY] * na, out_specs=[row, vec, vec],
        out_shape=[jax.ShapeDtypeStruct((s, D), F32), jax.ShapeDtypeStruct((1, D), F32),
                   jax.ShapeDtypeStruct((1, D), F32)],
        compiler_params=_cparams(1))(dh, z, g, *after)


def ln2_ple_bwd(dh, z, g, pg, pp, name, after=()):
    s = dh.shape[0]
    tm = 512
    na = len(after)

    def body(dh_ref, z_ref, g_ref, pg_ref, pp_ref, *rest):
        dz_ref, dzb_ref, dpg_ref, dpp_ref, dg_ref, db_ref = rest[na:]

        @pl.when(pl.program_id(0) == 0)
        def _():
            dg_ref[...] = jnp.zeros_like(dg_ref)
            db_ref[...] = jnp.zeros_like(db_ref)
        dh = dh_ref[...]
        dz, dgx = _ln_bwd(dh, z_ref[...], g_ref[...])
        sg = _sigmoid(pg_ref[...].astype(F32))
        dz_ref[...] = dz
        dzb_ref[...] = dz.astype(BF16)
        dpg_ref[...] = (dz * pp_ref[...].astype(F32) * sg * (1.0 - sg)).astype(BF16)
        dpp_ref[...] = (dz * sg).astype(BF16)
        dg_ref[...] += _colsum(dgx)
        db_ref[...] += _colsum(dh)

    row = pl.BlockSpec((tm, D), lambda i: (i, 0))
    vec = pl.BlockSpec((1, D), lambda i: (0, 0))
    return pl.pallas_call(
        body, name=name, grid=(s // tm,), in_specs=[row, row, vec, row, row] + [ANY] * na,
        out_specs=[row, row, row, row, vec, vec],
        out_shape=[jax.ShapeDtypeStruct((s, D), F32)] + [jax.ShapeDtypeStruct((s, D), BF16)] * 3
        + [jax.ShapeDtypeStruct((1, D), F32)] * 2,
        compiler_params=_cparams(1))(dh, z, g, pg, pp, *after)


def wgrad_rows(a, dy, name, after=()):
    s, k = a.shape
    n = dy.shape[1]
    kb = k // N_DEV

    def body(a_ref, dy_ref, *rest):
        rest[-1][...] = _tn(a_ref[...], dy_ref[...]).astype(BF16)

    return pl.pallas_call(
        body, name=name, grid=(N_DEV,),
        in_specs=[pl.BlockSpec((s, kb), lambda j: (0, j)), pl.BlockSpec((s, n), lambda j: (0, 0))] + [ANY] * len(after),
        out_specs=pl.BlockSpec((None, kb, n), lambda j: (j, 0, 0)),
        out_shape=jax.ShapeDtypeStruct((N_DEV, kb, n), BF16),
        compiler_params=_cparams(1))(a, dy, *after)


def wgrad_cols(a, dy, name, with_colsum=False):
    s, k = a.shape
    n = dy.shape[1]
    nb = n // N_DEV

    def body(a_ref, dy_ref, o_ref, *cs_ref):
        dy = dy_ref[...]
        o_ref[...] = _tn(a_ref[...], dy).astype(BF16)
        if with_colsum:
            cs_ref[0][...] = _colsum(dy.astype(F32))

    out_specs = [pl.BlockSpec((None, k, nb), lambda j: (j, 0, 0))]
    out_shape = [jax.ShapeDtypeStruct((N_DEV, k, nb), BF16)]
    if with_colsum:
        out_specs.append(pl.BlockSpec((1, nb), lambda j: (0, j)))
        out_shape.append(jax.ShapeDtypeStruct((1, n), F32))
    res = pl.pallas_call(
        body, name=name, grid=(N_DEV,),
        in_specs=[pl.BlockSpec((s, k), lambda j: (0, 0)), pl.BlockSpec((s, nb), lambda j: (0, j))],
        out_specs=out_specs, out_shape=out_shape,
        compiler_params=_cparams(1))(a, dy)
    return res if with_colsum else res[0]


def wgrad_down(t, dy, name):
    _, s, k = t.shape
    n = dy.shape[1]

    def body(a_ref, dy_ref, o_ref):
        o_ref[...] = _tn(a_ref[...], dy_ref[...]).astype(BF16)

    return pl.pallas_call(
        body, name=name, grid=(4,),
        in_specs=[pl.BlockSpec((None, s, k), lambda j: (j, 0, 0)), pl.BlockSpec((s, n), lambda j: (0, 0))],
        out_specs=pl.BlockSpec((None, k, n), lambda j: (j, 0, 0)),
        out_shape=jax.ShapeDtypeStruct((4, k, n), BF16),
        compiler_params=_cparams(1))(t, dy)


def wgrad_up(a, dhv, dhg, name):
    s, k = a.shape

    def body(a_ref, dv_ref, dg_ref, o_ref):
        j = pl.program_id(0)

        @pl.when(j < 4)
        def _():
            o_ref[...] = _tn(dv_ref[...], a_ref[...]).astype(BF16)

        @pl.when(j >= 4)
        def _():
            o_ref[...] = _tn(dg_ref[...], a_ref[...]).astype(BF16)

    return pl.pallas_call(
        body, name=name, grid=(N_DEV,),
        in_specs=[pl.BlockSpec((s, k), lambda j: (0, 0)),
                  pl.BlockSpec((None, s, FF_BLK), lambda j: (jnp.minimum(j, 3), 0, 0)),
                  pl.BlockSpec((None, s, FF_BLK), lambda j: (jnp.maximum(j - 4, 0), 0, 0))],
        out_specs=pl.BlockSpec((None, FF_BLK, k), lambda j: (j, 0, 0)),
        out_shape=jax.ShapeDtypeStruct((N_DEV, FF_BLK, k), BF16),
        compiler_params=_cparams(1))(a, dhv, dhg)


def ffn_act_bwd(dzb, wdown, up, conv_w, conv_b, li, name):
    s = up.shape[1]
    tm = 512
    nt = s // tm
    before, after = _halo_maps(tm, s)
    hv_main, hv_prev, hv_next = _slab_specs(tm, s, lambda c: c)
    hg_main, hg_prev, hg_next = _slab_specs(tm, s, lambda c: 4 + c)

    def dc_of(dz, wd, hv, hg, back, fwd, cw_ref, cb_ref):
        dt = _nt(dz, wd)
        c = back * cw_ref[0:1, :] + hg * cw_ref[1:2, :] + fwd * cw_ref[2:3, :] + cb_ref[...]
        cdf = 0.5 * (1.0 + lax.erf(c * _SQRT_HALF))
        pdf = jnp.exp(-0.5 * c * c) * _INV_SQRT_2PI
        return dt, c * cdf, dt * hv * (cdf + c * pdf)

    def body(dz_ref, dzp_ref, dzn_ref, wd_ref, hv_ref, hvp_ref, hvn_ref, hg_ref, hgp_ref, hgn_ref, cw_ref, cb_ref,
             dhv_ref, dhg_ref, dcw_ref, dcb_ref):
        i = pl.program_id(1)

        @pl.when(i == 0)
        def _():
            dcw_ref[...] = jnp.zeros_like(dcw_ref)
            dcb_ref[...] = jnp.zeros_like(dcb_ref)

        wd = _row_cat(wd_ref)
        hg = hg_ref[...].astype(F32)
        hgp = hgp_ref[...].astype(F32)
        hgn = hgn_ref[...].astype(F32)
        first, last = i == 0, i == nt - 1
        e = HALO - 1
        back, fwd = _shift_rows(hg, jnp.where(first, 0.0, hgp[e:e + 1, :]), jnp.where(last, 0.0, hgn[0:1, :]))
        dt, act, dc = dc_of(dz_ref[...], wd, hv_ref[...].astype(F32), hg, back, fwd, cw_ref, cb_ref)
        dhv_ref[...] = (dt * act).astype(BF16)
        bp, fp = _shift_rows(hgp, hgp[0:1, :], hg[0:1, :])
        _, _, dcp = dc_of(dzp_ref[...], wd, hvp_ref[...].astype(F32), hgp, bp, fp, cw_ref, cb_ref)
        bn, fn = _shift_rows(hgn, hg[tm - 1:tm, :], hgn[e:e + 1, :])
        _, _, dcn = dc_of(dzn_ref[...], wd, hvn_ref[...].astype(F32), hgn, bn, fn, cw_ref, cb_ref)
        dc_back, dc_fwd = _shift_rows(dc, jnp.where(first, 0.0, dcp[e:e + 1, :]), jnp.where(last, 0.0, dcn[0:1, :]))
        dhg_ref[...] = (dc_fwd * cw_ref[0:1, :] + dc * cw_ref[1:2, :] + dc_back * cw_ref[2:3, :]).astype(BF16)
        dcw_ref[0:1, :] += _colsum(dc * back)
        dcw_ref[1:2, :] += _colsum(dc * hg)
        dcw_ref[2:3, :] += _colsum(dc * fwd)
        dcb_ref[...] += _colsum(dc)

    out_slab = pl.BlockSpec((None, tm, FF_BLK), lambda c, i: (c, i, 0))
    cw_spec = pl.BlockSpec((None, 3, FF_BLK), lambda c, i: (c, 0, 0))
    cb_spec = pl.BlockSpec((None, 1, FF_BLK), lambda c, i: (c, 0, 0))
    return pl.pallas_call(
        body, name=name, grid=(4, nt),
        in_specs=[pl.BlockSpec((tm, D), lambda c, i: (i, 0)),
                  pl.BlockSpec((HALO, D), lambda c, i: (before(i), 0)),
                  pl.BlockSpec((HALO, D), lambda c, i: (after(i), 0)),
                  _shards(2, FF_SHARD, D, li, lambda c, i: c),
                  hv_main, hv_prev, hv_next, hg_main, hg_prev, hg_next, cw_spec, cb_spec],
        out_specs=[out_slab, out_slab, cw_spec, cb_spec],
        out_shape=[jax.ShapeDtypeStruct((4, s, FF_BLK), BF16), jax.ShapeDtypeStruct((4, s, FF_BLK), BF16),
                   jax.ShapeDtypeStruct((4, 3, FF_BLK), F32), jax.ShapeDtypeStruct((4, 1, FF_BLK), F32)],
        compiler_params=_cparams(2))(dzb, dzb, dzb, wdown, up, up, up, up, up, up, conv_w, conv_b)


def dh1_ln1_bwd(dz2, dpg, wpg, dhv, dhg, wup, z1, g1, li, name, after=()):
    s = dz2.shape[0]
    tm = 256
    na = len(after)

    def body(dz2_ref, dpg_ref, wpg_ref, dhv_ref, dhg_ref, wup_ref, z1_ref, g_ref, *rest):
        dz_ref, dzb_ref, dg_ref, db_ref = rest[na:]

        @pl.when(pl.program_id(0) == 0)
        def _():
            dg_ref[...] = jnp.zeros_like(dg_ref)
            db_ref[...] = jnp.zeros_like(db_ref)
        dh = ALPHA * dz2_ref[...] + _nt(dpg_ref[...], _row_cat(wpg_ref))
        for c in range(4):
            dh = dh + _nn(dhv_ref[c], wup_ref[c]) + _nn(dhg_ref[c], wup_ref[4 + c])
        dz, dgx = _ln_bwd(dh, z1_ref[...], g_ref[...])
        dz_ref[...] = dz
        dzb_ref[...] = dz.astype(BF16)
        dg_ref[...] += _colsum(dgx)
        db_ref[...] += _colsum(dh)

    row = pl.BlockSpec((tm, D), lambda i: (i, 0))
    vec = pl.BlockSpec((1, D), lambda i: (0, 0))
    slab = pl.BlockSpec((4, tm, FF_BLK), lambda i: (0, i, 0))
    return pl.pallas_call(
        body, name=name, grid=(s // tm,),
        in_specs=[row, row, _shards(N_DEV, D // N_DEV, D, li), slab, slab, _shards(N_DEV, FF_BLK, D, li), row, vec]
        + [ANY] * na,
        out_specs=[row, row, vec, vec],
        out_shape=[jax.ShapeDtypeStruct((s, D), F32), jax.ShapeDtypeStruct((s, D), BF16),
                   jax.ShapeDtypeStruct((1, D), F32), jax.ShapeDtypeStruct((1, D), F32)],
        compiler_params=_cparams(1))(dz2, dpg, wpg, dhv, dhg, wup, z1, g1, *after)


def merge_bwd(dz1b, wmix, proj, ya, yp, li, name):
    s = dz1b.shape[0]
    tm, tn = 512, 512
    nt = D // tn
    per = tn // (D // N_DEV)
    ga0 = (3 * D_ATTN + D_POOL) // tn

    def body(dz_ref, w_ref, ga_ref, gb_ref, ya_ref, yp_ref, dya_ref, dyp_ref, dga_ref, dgb_ref):
        dm = _nt(dz_ref[...], _row_cat(w_ref))
        sa = _sigmoid(ga_ref[...].astype(F32))
        sb = _sigmoid(gb_ref[...].astype(F32))
        dya_ref[...] = (dm * sa).astype(BF16)
        dyp_ref[...] = (dm * sb).astype(BF16)
        dga_ref[...] = (dm * ya_ref[...].astype(F32) * sa * (1.0 - sa)).astype(BF16)
        dgb_ref[...] = (dm * yp_ref[...].astype(F32) * sb * (1.0 - sb)).astype(BF16)

    tile = pl.BlockSpec((tm, tn), lambda i, j: (i, j))
    return pl.pallas_call(
        body, name=name, grid=(s // tm, nt),
        in_specs=[pl.BlockSpec((tm, D), lambda i, j: (i, 0)),
                  _shards(per, D // N_DEV, D, li, lambda i, j: j),
                  pl.BlockSpec((tm, tn), lambda i, j: (i, ga0 + j)),
                  pl.BlockSpec((tm, tn), lambda i, j: (i, ga0 + nt + j)),
                  tile, tile],
        out_specs=[tile] * 4,
        out_shape=[jax.ShapeDtypeStruct((s, D), BF16)] * 4,
        compiler_params=_cparams(2))(dz1b, wmix, proj, proj, ya, yp)


def attn_out_bwd(dya, wao, li, name, after=()):
    s = dya.shape[0]
    tm = 512

    def body(d_ref, w_ref, *rest):
        rest[-1][...] = _nt(d_ref[...], _lane_cat(w_ref)).astype(BF16)

    return pl.pallas_call(
        body, name=name, grid=(s // tm,),
        in_specs=[pl.BlockSpec((tm, D), lambda i: (i, 0)), _shards(N_DEV, D_ATTN, 128, li)] + [ANY] * len(after),
        out_specs=pl.BlockSpec((tm, D_ATTN), lambda i: (i, 0)),
        out_shape=jax.ShapeDtypeStruct((s, D_ATTN), BF16),
        compiler_params=_cparams(1))(dya, wao, *after)


def pool_bwd(dyp, wpo, pm, pool_w, pool_scale, li, name):
    s = dyp.shape[0]

    def body(dyp_ref, wpo_ref, pm_ref, w_ref, sc_ref, du_ref, dw_ref, dsc_ref):
        wpo = _lane_cat(wpo_ref)
        dyp = dyp_ref[...]
        for g, w in enumerate(POOL_WINDOWS):
            cols = slice(g * PGD, (g + 1) * PGD)
            dpw = _nt(dyp, wpo[g * PGD:(g + 1) * PGD, :])
            pmg = pm_ref[:, cols]
            dsc_ref[:, cols] = _colsum(dpw * _nn(pmg, w_ref[g]))
            dpmw = (dpw * sc_ref[:, cols]).astype(BF16)
            dw_ref[g] = _tn(pmg, dpmw)
            dpm = _nt(dpmw, w_ref[g])
            du_ref[:, cols] = (_window_sum(dpm / _pool_counts(s, w), w, False) - dpm).astype(BF16)

    full = lambda shape: pl.BlockSpec(shape, lambda i: (0,) * len(shape))
    return pl.pallas_call(
        body, name=name, grid=(1,),
        in_specs=[full((s, D)), _shards(N_DEV, D_POOL, 128, li), full((s, D_POOL)), full((4, PGD, PGD)),
                  full((1, D_POOL))],
        out_specs=[full((s, D_POOL)), full((4, PGD, PGD)), full((1, D_POOL))],
        out_shape=[jax.ShapeDtypeStruct((s, D_POOL), BF16), jax.ShapeDtypeStruct((4, PGD, PGD), F32),
                   jax.ShapeDtypeStruct((1, D_POOL), F32)],
        compiler_params=_cparams(1))(dyp, wpo, pm, pool_w, pool_scale)


def attn_bwd(proj, da, e_rev, li, name, after=()):
    s = proj.shape[0]
    nb = s // QB
    skew = GRID_W + (GRID_W - KW)

    def body(q_ref, k_ref, v_ref, do_ref, e_ref, *rest):
        dq_ref, dk_ref, dv_ref, g_ref, s_ref, dp_ref, ds_ref, p_ref, dkt_acc, dvt_acc = rest[len(after):]
        b = pl.program_id(1)

        @pl.when(b == 0)
        def _():
            dkt_acc[...] = jnp.zeros_like(dkt_acc)
            dvt_acc[...] = jnp.zeros_like(dvt_acc)
            g_ref[...] = jnp.zeros_like(g_ref)

        ri = lax.broadcasted_iota(I32, (QB, QB), 0)
        ci = lax.broadcasted_iota(I32, (QB, QB), 1)
        rev = jnp.where(ri + ci == QB - 1, 1.0, 0.0).astype(BF16)
        q = _nn(rev, q_ref[...]).astype(BF16) * ATT_SCALE
        do = _nn(rev, do_ref[...]).astype(BF16)
        lane = lax.broadcasted_iota(I32, (1, 128), 1)

        def block(btype, k0):
            kwin = k_ref[pl.ds(k0, KB), :]
            vwin = v_ref[pl.ds(k0, KB), :]
            dq = jnp.zeros((QB, 128), F32)
            for hh in range(2):
                lm = (lane // HEAD_DIM) == hh
                qh = jnp.where(lm, q, jnp.zeros_like(q))
                doh = jnp.where(lm, do, jnp.zeros_like(do))
                kh = jnp.where(lm, kwin, jnp.zeros_like(kwin))
                s_ref[...] = _nt(qh, kwin)
                dp_ref[...] = _nt(doh, vwin)
                g = jnp.zeros((1, KB), F32)
                for ib in range(QROWS):
                    qr = QROWS - 1 - ib
                    rows = slice(ib * GRID_W, (ib + 1) * GRID_W)
                    sb, a0, w, pad = _row_logits(s_ref, e_ref, hh, rows, btype, qr)
                    p = jnp.exp(sb - jnp.max(sb, axis=1, keepdims=True))
                    p = p * (1.0 / jnp.sum(p, axis=1, keepdims=True))
                    dp = dp_ref[rows, a0:a0 + w]
                    ds = p * (dp - jnp.sum(p * dp, axis=1, keepdims=True))
                    _store_row(ds_ref, rows, a0, w, ds)
                    _store_row(p_ref, rows, a0, w, p)
                    t = jnp.sum(pltpu.roll(ds, w - skew, 1, stride=1, stride_axis=0), axis=0, keepdims=True)
                    t = t[:, :KH * GRID_W] if pad else pltpu.roll(t, GRID_W, 1)
                    i0 = _attn_row(btype, qr)[2]
                    g = g + pltpu.roll(jnp.concatenate([t, jnp.zeros_like(t)], axis=1), i0 * GRID_W, 1)
                g_ref[hh] += g
                dsb = ds_ref[...]
                dq = dq + _nn(dsb, kh) * ATT_SCALE
                dkt_acc[:, pl.ds(k0, KB)] += _tn(qh, dsb)
                dvt_acc[:, pl.ds(k0, KB)] += _tn(doh, p_ref[...])
            dq_ref[...] = _nn(rev, dq.astype(BF16)).astype(BF16)

        for btype, (cond, k0) in enumerate(_attn_types(b, nb)):
            pl.when(cond)(lambda btype=btype, k0=k0: block(btype, k0))

        @pl.when(b == nb - 1)
        def _():
            dk_ref[...] = dkt_acc[...].T.astype(BF16)
            dv_ref[...] = dvt_acc[...].T.astype(BF16)

    col = pl.BlockSpec((s, 128), lambda j, b: (0, j))
    return pl.pallas_call(
        body, name=name, grid=(4, nb),
        in_specs=[pl.BlockSpec((QB, 128), lambda j, b: (b, j)),
                  pl.BlockSpec((s, 128), lambda j, b: (0, 4 + j)),
                  pl.BlockSpec((s, 128), lambda j, b: (0, 8 + j)),
                  pl.BlockSpec((QB, 128), lambda j, b: (b, j)),
                  pl.BlockSpec((None, 2, 2, GRID_W, KB), lambda j, b: (li, j, 0, 0, 0))] + [ANY] * len(after),
        out_specs=[pl.BlockSpec((QB, 128), lambda j, b: (b, j)), col, col,
                   pl.BlockSpec((2, 1, KB), lambda j, b: (j, 0, 0))],
        out_shape=[jax.ShapeDtypeStruct((s, D_ATTN), BF16)] * 3 + [jax.ShapeDtypeStruct((N_HEADS, 1, KB), F32)],
        scratch_shapes=[pltpu.VMEM((QB, KB), F32), pltpu.VMEM((QB, KB), F32), pltpu.VMEM((QB, KB), BF16),
                        pltpu.VMEM((QB, KB), BF16), pltpu.VMEM((128, s), F32), pltpu.VMEM((128, s), F32)],
        compiler_params=_cparams(2))(proj, proj, proj, da, e_rev, *after)


def dh0_bwd(dz1, dproj, win, li, name, after=()):
    s = dz1.shape[0]
    tm = 256
    bn = N_PROJ // N_DEV

    def body(dz_ref, dp_ref, w_ref, *rest):
        acc = ALPHA * dz_ref[...]
        for j in range(N_DEV):
            acc = acc + _nt(dp_ref[:, j * bn:(j + 1) * bn], w_ref[j])
        rest[-1][...] = acc

    row = pl.BlockSpec((tm, D), lambda i: (i, 0))
    return pl.pallas_call(
        body, name=name, grid=(s // tm,),
        in_specs=[row, pl.BlockSpec((tm, N_PROJ), lambda i: (i, 0)), _shards(N_DEV, D, bn, li)] + [ANY] * len(after),
        out_specs=row, out_shape=jax.ShapeDtypeStruct((s, D), F32),
        compiler_params=_cparams(1))(dz1, dproj, win, *after)


def _coords():
    return lax.axis_index("x"), lax.axis_index("y"), lax.axis_index("c")


def _dev_index(px, py, pc):
    return 4 * px + 2 * py + pc


HBM = pl.BlockSpec(memory_space=pltpu.HBM)
SEM = pl.BlockSpec(memory_space=pltpu.SEMAPHORE)
_EFFECT = pltpu.SideEffectType.DATAFLOW_SIDE_EFFECTING
_TOKEN = jax.ShapeDtypeStruct((8, 128), F32)


def _in_hbm(a):
    return pltpu.with_memory_space_constraint(a, pltpu.HBM)


def _hbm_like(a):
    return pltpu.HBM(a.shape, a.dtype)


def _peers(x, y, c):
    return [(x, y, 1 - c), (1 - x, y, c), (x, 1 - y, c), (1 - x, 1 - y, c)]


def ag_start(lands, after, name):
    n = len(lands)

    def body(*refs):
        land = refs[:n]
        send_sem, recv_sem, token = refs[n + 1], refs[n + 2], refs[-1]
        x, y, c = _coords()
        me = _dev_index(x, y, c)
        for k, peer in enumerate(_peers(x, y, c)):
            for a in range(n):
                pltpu.make_async_remote_copy(src_ref=land[a].at[me], dst_ref=land[a].at[me], send_sem=send_sem.at[k],
                                             recv_sem=recv_sem.at[k], device_id=peer, device_id_type=MESH).start()
        token[...] = jnp.zeros_like(token)

    res = pl.pallas_call(
        body, name=name,
        out_shape=(pltpu.SemaphoreType.DMA((4,)), pltpu.SemaphoreType.DMA((4,)), *[_hbm_like(l) for l in lands], _TOKEN),
        in_specs=[HBM] * n + [ANY], out_specs=(SEM, SEM, *[HBM] * n, pl.BlockSpec(memory_space=pltpu.VMEM)),
        input_output_aliases={a: 2 + a for a in range(n)},
        compiler_params=pltpu.CompilerParams(has_side_effects=_EFFECT),
    )(*[_in_hbm(l) for l in lands], after)
    return res[0], res[1], list(res[2:2 + n]), res[-1]


def ag_forward(send_sem, recv_sem, lands, after, name):
    n = len(lands)

    def body(*refs):
        send_sem, recv_sem = refs[0], refs[1]
        land = refs[2:2 + n]
        fsend, frecv = refs[3 + n], refs[4 + n]
        x, y, c = _coords()
        peers = _peers(x, y, c)
        for k in range(1, 4):
            blk = _dev_index(*peers[k])
            for a in range(n):
                pltpu.make_async_remote_copy(src_ref=land[a].at[blk], dst_ref=land[a].at[blk], send_sem=send_sem.at[k],
                                             recv_sem=recv_sem.at[k], device_id=peers[k], device_id_type=MESH).wait_recv()
        for k in range(1, 4):
            blk = _dev_index(*peers[k])
            for a in range(n):
                pltpu.make_async_remote_copy(src_ref=land[a].at[blk], dst_ref=land[a].at[blk], send_sem=fsend.at[k - 1],
                                             recv_sem=frecv.at[k - 1], device_id=peers[0], device_id_type=MESH).start()

    res = pl.pallas_call(
        body, name=name,
        out_shape=(pltpu.SemaphoreType.DMA((3,)), pltpu.SemaphoreType.DMA((3,)), *[_hbm_like(l) for l in lands]),
        in_specs=[SEM, SEM, *[HBM] * n, ANY], out_specs=(SEM, SEM, *[HBM] * n),
        input_output_aliases={2 + a: 2 + a for a in range(n)},
        compiler_params=pltpu.CompilerParams(has_side_effects=_EFFECT),
    )(send_sem, recv_sem, *lands, after)
    return res[0], res[1], list(res[2:])


def ag_finish(send_sem, recv_sem, fsend, frecv, lands, after, name):
    n = len(lands)

    def body(*refs):
        send_sem, recv_sem, fsend, frecv = refs[:4]
        land = refs[4:4 + n]
        x, y, c = _coords()
        me = _dev_index(x, y, c)
        peers = _peers(x, y, c)
        for k in range(4):
            for a in range(n):
                pltpu.make_async_remote_copy(src_ref=land[a].at[me], dst_ref=land[a].at[me], send_sem=send_sem.at[k],
                                             recv_sem=recv_sem.at[k], device_id=peers[k], device_id_type=MESH).wait_send()
        sib = _dev_index(*peers[0])
        for a in range(n):
            pltpu.make_async_remote_copy(src_ref=land[a].at[sib], dst_ref=land[a].at[sib], send_sem=send_sem.at[0],
                                         recv_sem=recv_sem.at[0], device_id=peers[0], device_id_type=MESH).wait_recv()
        for k in range(1, 4):
            mine = _dev_index(*peers[k])
            theirs = _dev_index(peers[k][0], peers[k][1], 1 - c)
            for a in range(n):
                pltpu.make_async_remote_copy(src_ref=land[a].at[mine], dst_ref=land[a].at[theirs], send_sem=fsend.at[k - 1],
                                             recv_sem=frecv.at[k - 1], device_id=peers[0], device_id_type=MESH).wait()

    res = pl.pallas_call(
        body, name=name, out_shape=tuple(_hbm_like(l) for l in lands),
        in_specs=[SEM] * 4 + [HBM] * n + [ANY], out_specs=tuple([HBM] * n),
        input_output_aliases={4 + a: a for a in range(n)},
        compiler_params=pltpu.CompilerParams(has_side_effects=_EFFECT),
    )(send_sem, recv_sem, fsend, frecv, *lands, after)
    return list(res)


def rs_start(psums, name):
    n = len(psums)
    lands = [lax.empty(p.shape, p.dtype) for p in psums]

    def body(*refs):
        src, land = refs[:n], refs[n:2 * n]
        send_sem, recv_sem, token = refs[2 * n], refs[2 * n + 1], refs[-1]
        peers = _peers(*_coords())
        for k in range(3):
            for a in range(n):
                pltpu.make_async_remote_copy(src_ref=src[a].at[k], dst_ref=land[a].at[k], send_sem=send_sem.at[k],
                                             recv_sem=recv_sem.at[k], device_id=peers[k + 1], device_id_type=MESH).start()
        token[...] = jnp.zeros_like(token)

    res = pl.pallas_call(
        body, name=name,
        out_shape=(pltpu.SemaphoreType.DMA((3,)), pltpu.SemaphoreType.DMA((3,)), *[_hbm_like(p) for p in psums],
                   *[_hbm_like(l) for l in lands], _TOKEN),
        in_specs=[HBM] * (2 * n), out_specs=(SEM, SEM, *[HBM] * (2 * n), pl.BlockSpec(memory_space=pltpu.VMEM)),
        input_output_aliases={a: 2 + a for a in range(2 * n)},
        compiler_params=pltpu.CompilerParams(has_side_effects=_EFFECT),
    )(*[_in_hbm(p) for p in psums], *[_in_hbm(l) for l in lands])
    return res[0], res[1], list(res[2:2 + n]), list(res[2 + n:2 + 2 * n]), res[-1]


def rs_finish(send_sem, recv_sem, psums, lands, after, name):
    n = len(psums)

    def body(*refs):
        send_sem, recv_sem = refs[0], refs[1]
        src, land = refs[2:2 + n], refs[2 + n:2 + 2 * n]
        peers = _peers(*_coords())
        for k in range(3):
            for a in range(n):
                pltpu.make_async_remote_copy(src_ref=src[a].at[k], dst_ref=land[a].at[k], send_sem=send_sem.at[k],
                                             recv_sem=recv_sem.at[k], device_id=peers[k + 1], device_id_type=MESH).wait()

    res = pl.pallas_call(
        body, name=name, out_shape=tuple(_hbm_like(l) for l in lands),
        in_specs=[SEM, SEM] + [HBM] * (2 * n) + [ANY], out_specs=tuple([HBM] * n),
        input_output_aliases={2 + n + a: a for a in range(n)},
        compiler_params=pltpu.CompilerParams(has_side_effects=_EFFECT),
    )(send_sem, recv_sem, *psums, *lands, after)
    return list(res)


def d2d_start(grads, name):
    n = len(grads)
    lands = [lax.empty((4,) + g.shape[1:], g.dtype) for g in grads]

    def body(*refs):
        src, land = refs[:n], refs[n:2 * n]
        send_sem, recv_sem, token = refs[2 * n], refs[2 * n + 1], refs[-1]
        x, y, c = _coords()
        for a in range(n):
            for k in range(4):
                blk = _dev_index(x ^ (k & 1), y ^ (k >> 1), 1 - c)
                pltpu.make_async_remote_copy(src_ref=src[a].at[blk], dst_ref=land[a].at[k], send_sem=send_sem.at[0],
                                             recv_sem=recv_sem.at[0], device_id=(x, y, 1 - c), device_id_type=MESH).start()
        token[...] = jnp.zeros_like(token)

    res = pl.pallas_call(
        body, name=name,
        out_shape=(pltpu.SemaphoreType.DMA((1,)), pltpu.SemaphoreType.DMA((1,)), *[_hbm_like(g) for g in grads],
                   *[_hbm_like(l) for l in lands], _TOKEN),
        in_specs=[HBM] * (2 * n), out_specs=(SEM, SEM, *[HBM] * (2 * n), pl.BlockSpec(memory_space=pltpu.VMEM)),
        input_output_aliases={a: 2 + a for a in range(2 * n)},
        compiler_params=pltpu.CompilerParams(has_side_effects=_EFFECT),
    )(*[_in_hbm(g) for g in grads], *[_in_hbm(l) for l in lands])
    return res[0], res[1], list(res[2:2 + n]), list(res[2 + n:2 + 2 * n]), res[-1]


def d2d_finish(send_sem, recv_sem, grads, lands, after, name):
    n = len(grads)

    def body(*refs):
        send_sem, recv_sem = refs[0], refs[1]
        src, land = refs[2:2 + n], refs[2 + n:2 + 2 * n]
        x, y, c = _coords()
        for a in range(n):
            for k in range(4):
                blk = _dev_index(x ^ (k & 1), y ^ (k >> 1), 1 - c)
                pltpu.make_async_remote_copy(src_ref=src[a].at[blk], dst_ref=land[a].at[k], send_sem=send_sem.at[0],
                                             recv_sem=recv_sem.at[0], device_id=(x, y, 1 - c), device_id_type=MESH).wait()

    res = pl.pallas_call(
        body, name=name, out_shape=tuple(_hbm_like(t) for t in list(grads) + list(lands)),
        in_specs=[SEM, SEM] + [HBM] * (2 * n) + [ANY], out_specs=tuple([HBM] * (2 * n)),
        input_output_aliases={2 + a: a for a in range(2 * n)},
        compiler_params=pltpu.CompilerParams(has_side_effects=_EFFECT),
    )(send_sem, recv_sem, *grads, *lands, after)
    return list(res[:n]), list(res[n:])


def pair_add(blk_idx, g, recv, name):
    _, r, c = g.shape
    tr = _row_tile(r)

    def body(idx_ref, g0, g1, g2, g3, r_ref, own_ref, oth_ref):
        own_ref[...] = g0[...].astype(F32) + r_ref[0].astype(F32)
        for k, gk in enumerate((g1, g2, g3)):
            oth_ref[k] = (gk[...].astype(F32) + r_ref[k + 1].astype(F32)).astype(BF16)

    def blk(k):
        return pl.BlockSpec((None, tr, c), lambda t, idx: (idx[k], t, 0))

    grid_spec = pltpu.PrefetchScalarGridSpec(
        num_scalar_prefetch=1, grid=(r // tr,),
        in_specs=[blk(0), blk(1), blk(2), blk(3), pl.BlockSpec((4, tr, c), lambda t, idx: (0, t, 0))],
        out_specs=[pl.BlockSpec((tr, c), lambda t, idx: (t, 0)), pl.BlockSpec((3, tr, c), lambda t, idx: (0, t, 0))])
    return pl.pallas_call(
        body, name=name, grid_spec=grid_spec,
        out_shape=[jax.ShapeDtypeStruct((r, c), F32), jax.ShapeDtypeStruct((3, r, c), BF16)],
        compiler_params=_cparams(1))(blk_idx, g, g, g, g, recv)


def _row_tile(r):
    return next(t for t in (512, 352, 256, 128) if r % t == 0)


def _adamw(w, g, m, v):
    m = ADAM_B1 * m + (1.0 - ADAM_B1) * g
    v = ADAM_B2 * v + (1.0 - ADAM_B2) * (g * g)
    m_hat = m / (1.0 - ADAM_B1 ** ADAM_STEP)
    v_hat = v / (1.0 - ADAM_B2 ** ADAM_STEP)
    delta = -ADAM_LR * (m_hat / (jnp.sqrt(v_hat) + ADAM_EPS) + ADAM_WD * w)
    return delta, m, v


def adamw_shard(own, recv, w, m, v, li, prev, name):
    r, c = own.shape
    tr = _row_tile(r)

    def body(own_ref, recv_ref, w_ref, m_ref, v_ref, p0, p1, p2, p3, g_ref, d_ref, nm_ref, nv_ref):
        g = own_ref[...] + recv_ref[0].astype(F32) + recv_ref[1].astype(F32) + recv_ref[2].astype(F32)
        delta, nm, nv = _adamw(w_ref[...], g, m_ref[...], v_ref[...])
        g_ref[...] = g
        d_ref[...] = delta
        nm_ref[...] = nm
        nv_ref[...] = nv

    lay = pl.BlockSpec((None, tr, c), lambda t: (li, t, 0))
    stack = jax.ShapeDtypeStruct((DEPTH, r, c), F32)
    return pl.pallas_call(
        body, name=name, grid=(r // tr,),
        in_specs=[pl.BlockSpec((tr, c), lambda t: (t, 0)), pl.BlockSpec((3, tr, c), lambda t: (0, t, 0)),
                  lay, lay, lay, ANY, ANY, ANY, ANY],
        out_specs=[lay] * 4, out_shape=[stack] * 4,
        input_output_aliases={5: 0, 6: 1, 7: 2, 8: 3},
        compiler_params=_cparams(1))(own, recv, w, m, v, *prev)


def adamw_replicated(gathered, w, m, v, name):
    _, r, c = gathered.shape
    tr = next(t for t in (96, 88, 64, _PACK_TILE) if r % t == 0)

    def body(gs_ref, w_ref, m_ref, v_ref, g_ref, d_ref, nm_ref, nv_ref):
        g = gs_ref[0]
        for d in range(1, N_DEV):
            g = g + gs_ref[d]
        delta, nm, nv = _adamw(w_ref[...], g, m_ref[...], v_ref[...])
        g_ref[...] = g
        d_ref[...] = delta
        nm_ref[...] = nm
        nv_ref[...] = nv

    row = pl.BlockSpec((tr, c), lambda t: (t, 0))
    return pl.pallas_call(
        body, name=name, grid=(r // tr,),
        in_specs=[pl.BlockSpec((N_DEV, tr, c), lambda t: (0, t, 0)), row, row, row],
        out_specs=[row] * 4, out_shape=[jax.ShapeDtypeStruct((r, c), F32)] * 4,
        compiler_params=_cparams(1))(gathered, w, m, v)


def adamw_plain(g, w, m, v, name):
    def body(g_ref, w_ref, m_ref, v_ref, d_ref, nm_ref, nv_ref):
        delta, nm, nv = _adamw(w_ref[...], g_ref[...], m_ref[...], v_ref[...])
        d_ref[...] = delta
        nm_ref[...] = nm
        nv_ref[...] = nv

    return pl.pallas_call(body, name=name, out_shape=[jax.ShapeDtypeStruct(w.shape, F32)] * 3)(g, w, m, v)


_PACK_LAYER = (("b_in", (N_PROJ,)), ("rpb", (N_HEADS, 2 * KH - 1, 2 * KW - 1)), ("pool_w", (4, PGD, PGD)),
               ("pool_scale", (D_POOL,)), ("ln1_g", (D,)), ("ln1_b", (D,)), ("conv_b", (D_FF,)), ("ln2_g", (D,)),
               ("ln2_b", (D,)), ("conv_w", (3, D_FF)))
_PACK_INPUT = (("ln_in_g", (D,)), ("ln_in_b", (D,)))
_PACK_LANES = 1024
_PACK_TILE = 8
_EARLY = tuple(range(1, DEPTH))


def _pack_items(layers):
    items = [(n, (len(layers),) + s) for n, s in _PACK_LAYER]
    return items + ([(n, s) for n, s in _PACK_INPUT] if 0 in layers else [])


def _pack(parts, layers):
    flats = [(parts[name] if (name, shape) in _PACK_INPUT else jnp.stack([parts[name][li] for li in layers]))
             .reshape(-1).astype(F32) for name, shape in _pack_items(layers)]
    used = sum(f.shape[0] for f in flats)
    tile = _PACK_TILE * _PACK_LANES
    total = -(-used // tile) * tile
    return jnp.concatenate(flats + [jnp.zeros((total - used,), F32)]).reshape(total // _PACK_LANES, _PACK_LANES)


def _unpack(packed, layers):
    flat, out, off = packed.reshape(-1), {}, 0
    for name, shape in _pack_items(layers):
        n = int(np.prod(shape))
        out[name] = flat[off:off + n].reshape(shape)
        off += n
    return out


def _bias_tables(rpb):
    qc = np.arange(GRID_W)[:, None]
    kc = np.arange(GRID_W)[None, :]
    start = np.clip(qc - KW // 2, 0, GRID_W - KW)
    valid = (kc >= start) & (kc < start + KW)
    col = np.clip(kc - qc, -(KW - 1), KW - 1) + KW - 1
    onehot = (col.reshape(-1)[None, :] == np.arange(2 * KW - 1)[:, None]).astype(np.float32)
    depth = rpb.shape[0]
    rows = jnp.pad(rpb, ((0, 0), (0, 0), (0, 1), (0, 0)))
    tab = jnp.einsum("lhij,jm->lhim", rows, jnp.asarray(onehot), precision=lax.Precision.HIGHEST)
    tab = tab.reshape(depth, N_HEADS, KROWS, GRID_W, GRID_W).transpose(0, 1, 3, 2, 4)
    ok = valid[:, None, :] & (np.arange(KROWS) < 2 * KH - 1)[None, :, None]
    tab = jnp.where(jnp.asarray(ok), tab, NEG_INF).reshape(depth, N_HEADS, GRID_W, KB)
    tab = jnp.stack([tab, jnp.roll(tab, GRID_W, axis=-1)], axis=2)
    return tab, tab[:, :, :, ::-1, :]


_SHARDED = ("w_in", "w_attn_out", "w_pool_out", "w_mix_out", "w_up", "w_down", "w_ple_gate", "w_ple_proj")
_NAMES = ("ln_in_g", "ln_in_b", "w_in", "b_in", "rpb", "w_attn_out", "pool_w", "pool_scale", "w_pool_out", "w_mix_out",
          "ln1_g", "ln1_b", "w_up", "conv_w", "conv_b", "w_down", "w_ple_gate", "w_ple_proj", "ln2_g", "ln2_b")


def kernel(x, p, ln_in_g, ln_in_b, w_in, b_in, rpb, w_attn_out, pool_w, pool_scale, w_pool_out, w_mix_out, ln1_g, ln1_b, w_up, conv_w, conv_b, w_down, w_ple_gate, w_ple_proj, ln2_g, ln2_b, loss_target, m_ln_in_g, m_ln_in_b, m_w_in, m_b_in, m_rpb, m_w_attn_out, m_pool_w, m_pool_scale, m_w_pool_out, m_w_mix_out, m_ln1_g, m_ln1_b, m_w_up, m_conv_w, m_conv_b, m_w_down, m_w_ple_gate, m_w_ple_proj, m_ln2_g, m_ln2_b, v_ln_in_g, v_ln_in_b, v_w_in, v_b_in, v_rpb, v_w_attn_out, v_pool_w, v_pool_scale, v_w_pool_out, v_w_mix_out, v_ln1_g, v_ln1_b, v_w_up, v_conv_w, v_conv_b, v_w_down, v_w_ple_gate, v_w_ple_proj, v_ln2_g, v_ln2_b):
    a = dict(locals())
    W = {n: a[n] for n in _NAMES}
    M = {n: a["m_" + n] for n in _NAMES}
    V = {n: a["v_" + n] for n in _NAMES}
    xi, yi, ci = _coords()
    me = _dev_index(xi, yi, ci)
    x2, tgt = x[0], loss_target[0]
    pb = p[:, 0].astype(BF16)

    flip = lambda d: {**d, "w_up": d["w_up"].transpose(0, 2, 1)}
    ex = _Exchange(flip(W), flip(M), flip(V))
    loss_part, dx, parts = _local_step(x2, tgt, pb, W, ex)
    loss = lax.psum(loss_part[0, 0], AXES)

    started = ex.replicated_start("late", _pack(parts, (0,)), dx)
    done = ex.update(range(DEPTH - 1, 0, -1), started)
    ex.replicated_forward("late", done)
    done = ex.update((0,), done)
    stacks = {**ex.stacks, "w_up": [t.transpose(0, 2, 1) for t in ex.stacks["w_up"]]}
    zero_cw = jnp.zeros((DEPTH, 3, D_FF), F32)
    halves = []
    for layers, tag in (((0,), "late"), (_EARLY, "early")):
        gath = ex.replicated_finish(tag, done)
        packs = [_pack({**src, "conv_w": zero_cw}, layers) for src in (W, M, V)]
        halves.append([_unpack(o, layers) for o in adamw_replicated(gath, *packs, f"adamw_replicated_{tag}")])
    outs = [{**{n: jnp.concatenate([lo[n], hi[n]]) for n, _ in _PACK_LAYER}, **{n: lo[n] for n, _ in _PACK_INPUT}}
            for lo, hi in zip(*halves)]
    g_cw = lax.dynamic_slice_in_dim(outs[0]["conv_w"], me * FF_SHARD, FF_SHARD, axis=2)
    flat = lambda t: t.reshape(DEPTH * 3, FF_SHARD)
    cw_out = [o.reshape(DEPTH, 3, FF_SHARD) for o in
              adamw_plain(flat(g_cw), flat(conv_w), flat(m_conv_w), flat(v_conv_w), "adamw_conv_w")]
    res = []
    for k in range(4):
        d = {n: stacks[n][k] for n in _SHARDED}
        d.update({n: outs[k][n] for n in outs[k] if n != "conv_w"})
        d["conv_w"] = g_cw if k == 0 else cw_out[k - 1]
        res.append(d)
    return (loss, dx[None], *[res[k][n] for k in range(4) for n in _NAMES])


class _Exchange:
    GROUPS = (("w_ple_gate", "w_ple_proj", "w_down", "w_up"), ("w_mix_out", "w_attn_out", "w_pool_out"), ("w_in",))
    FIRST = ("w_in",)

    def __init__(self, W, M, V):
        self.W, self.M, self.V = W, M, V
        xi, yi, ci = _coords()
        me = _dev_index(xi, yi, ci)
        self.me = me.astype(I32).reshape(1)
        self.rel_idx = jnp.stack([_dev_index(xi ^ (k & 1), yi ^ (k >> 1), ci) for k in range(4)]).astype(I32)
        self.lands = [{n: lax.dynamic_update_index_in_dim(lax.empty((N_DEV,) + W[n].shape[1:], BF16),
                                                          W[n][li].astype(BF16), me, 0) for n in _SHARDED}
                      for li in range(DEPTH)]
        cw_land = lax.dynamic_update_index_in_dim(lax.empty((N_DEV,) + W["conv_w"].shape, F32), W["conv_w"], me, 0)
        self.ag, self.fwd, self.rs, self.pending, self.small = {}, {}, {}, {}, {}
        self.stacks = {n: [lax.empty((DEPTH,) + W[n].shape[1:], F32) for _ in range(4)] for n in _SHARDED}
        self.late = tuple(n for n in _SHARDED if n not in self.FIRST)
        self.ag[0] = ag_start([self.lands[0][n] for n in self.FIRST] + [cw_land], W["conv_w"], "ag_start0")

    def tokens(self):
        return [self.ag[0][3]]

    def prefetch(self, li, after):
        send, recv, lands, _ = self.ag[li]
        self.fwd[li] = ag_forward(send, recv, lands, after, f"ag_forward{li}")
        if li == 0:
            self.ag["0b"] = ag_start([self.lands[0][n] for n in self.late], self.fwd[0][2][0], "ag_start0b")

    def weights(self, li, after):
        send, recv, _, _ = self.ag.pop(li)
        fsend, frecv, lands = self.fwd.pop(li)
        lands = ag_finish(send, recv, fsend, frecv, lands, after, f"ag_finish{li}")
        if li == 0:
            self.cw = lands[-1].transpose(1, 2, 0, 3).reshape(DEPTH, 3, 4, FF_BLK).transpose(0, 2, 1, 3)
            return dict(zip(self.FIRST, lands)), self.cw[li], (self.ag["0b"][3],)
        tokens = ()
        if li + 1 < DEPTH:
            self.ag[li + 1] = ag_start([self.lands[li + 1][n] for n in _SHARDED], lands[0], f"ag_start{li + 1}")
            tokens = (self.ag[li + 1][3],)
        return dict(zip(_SHARDED, lands)), self.cw[li], tokens

    def rest(self, li, G, mid, after):
        if li != 0:
            return G, ()
        send, recv, lands, _ = self.ag.pop("0b")
        fsend, frecv, lands = ag_forward(send, recv, lands, mid, "ag_forward0b")
        lands = ag_finish(send, recv, fsend, frecv, lands, after, "ag_finish0b")
        self.ag[1] = ag_start([self.lands[1][n] for n in _SHARDED], lands[0], "ag_start1")
        return {**G, **dict(zip(self.late, lands))}, (self.ag[1][3],)

    def grads(self, li, group, gw):
        self.pending.setdefault(li, {}).update(gw)
        if li != 0 and group != len(self.GROUPS) - 1:
            return None
        gw = self.pending.pop(li)
        tag = f"{li}_{group}" if li == 0 else f"{li}"
        send, recv, glist, lands, token = d2d_start(list(gw.values()), f"d2d_start{tag}")
        self.d2d = (tag, tuple(gw), send, recv, glist, lands)
        return token

    def flush(self, li, group, after):
        if li != 0 and group != len(self.GROUPS) - 1:
            return None
        tag, names, send, recv, glist, lands = self.d2d
        glist, recv1 = d2d_finish(send, recv, glist, lands, after, f"d2d_finish{tag}")
        sums = [pair_add(self.rel_idx, g, r1, f"pair_add_{n}{li}") for n, g, r1 in zip(names, glist, recv1)]
        send, recv, psums, lands, token = rs_start([s_[1] for s_ in sums], f"rs_start{tag}")
        self.rs.setdefault(li, []).append((tag, names, send, recv, psums, lands, [s_[0] for s_ in sums]))
        if li == 0 and group == 1 and "early" in self.small:
            self.replicated_forward("early", token)
        return token

    def update(self, layers, after):
        for li in layers:
            for tag, names, send, recv, psums, lands, owns in self.rs.pop(li):
                recv2 = rs_finish(send, recv, psums, lands, after, f"rs_finish{tag}")
                for n, own, r2 in zip(names, owns, recv2):
                    self.stacks[n] = adamw_shard(own, r2, self.W[n], self.M[n], self.V[n], li, self.stacks[n],
                                                 f"adamw_{n}{li}")
                    after = self.stacks[n][0]
        return after

    def replicated_start(self, tag, pack, after):
        land = lax.dynamic_update_index_in_dim(lax.empty((N_DEV,) + pack.shape, F32), pack, self.me[0], 0)
        self.small[tag] = ag_start([land], after, f"ag_start_small_{tag}")
        return self.small[tag][3]

    def replicated_early(self, small, after):
        return self.replicated_start("early", _pack(small, _EARLY), after)

    def replicated_forward(self, tag, after):
        send, recv, lands, _ = self.small[tag]
        self.small[tag] = (send, recv) + ag_forward(send, recv, lands, after, f"ag_forward_small_{tag}")

    def replicated_finish(self, tag, after):
        send, recv, fsend, frecv, lands = self.small.pop(tag)
        return ag_finish(send, recv, fsend, frecv, lands, after, f"ag_finish_small_{tag}")[0]


def _local_step(x2, tgt, pb, W, ex):
    depth = W["rpb"].shape[0]
    vec = lambda t: t.reshape(1, -1)
    ln1_g, ln1_b, ln2_g, ln2_b = W["ln1_g"], W["ln1_b"], W["ln2_g"], W["ln2_b"]
    b_in, rpb, pool_scale = W["b_in"], W["rpb"], W["pool_scale"]
    cb_full = W["conv_b"].reshape(depth, 4, 1, FF_BLK)
    pool_w_b = W["pool_w"].astype(BF16)
    e_tab, e_rev = _bias_tables(rpb)

    h, hb = ln_fwd(x2, vec(W["ln_in_g"]), vec(W["ln_in_b"]), "ln_in", after=ex.tokens())
    ex.prefetch(0, hb)
    saved = []
    for li in range(depth):
        G, cw, tokens = ex.weights(li, hb)
        bias = vec(b_in[li])
        proj, u = proj_fwd(hb, G["w_in"], bias, li, f"proj{li}", after=tokens)
        att = attn_fwd(proj, e_tab, li, f"attn{li}")
        pm, pw = pool_fwd(u, pool_w_b[li], vec(pool_scale[li]), f"pool{li}")
        G, tokens = ex.rest(li, G, att, pw)
        mg, ya, yp = merge_fwd(att, pw, G["w_attn_out"], G["w_pool_out"], proj, li, f"merge{li}", after=tokens)
        if li + 1 < depth:
            ex.prefetch(li + 1, mg)
        z1, h1, h1b = mix_ln_fwd(mg, G["w_mix_out"], h, vec(ln1_g[li]), vec(ln1_b[li]), li, f"mix_ln{li}")
        up = up_fwd(h1b, G["w_up"], li, f"up{li}")
        t = ffn_act_fwd(up, cw, cb_full[li], f"ffn_act{li}")
        z2, h2, h2b, pg, pp = down_ple_ln_fwd(t, G["w_down"], h1b, G["w_ple_gate"], pb[li], G["w_ple_proj"], h1,
                                              vec(ln2_g[li]), vec(ln2_b[li]), li, f"down_ln{li}")
        saved.append(dict(hb=hb, proj=proj, att=att, pm=pm, pw=pw, mg=mg, ya=ya, yp=yp, z1=z1, h1b=h1b, up=up, t=t,
                          z2=z2, pg=pg, pp=pp, G=G, cw=cw))
        h, hb = h2, h2b

    dh, loss_part = loss_bwd(h, tgt, "loss")
    small = {n: [None] * depth for n in ("b_in", "rpb", "pool_w", "pool_scale", "ln1_g", "ln1_b", "conv_b", "ln2_g",
                                         "ln2_b", "conv_w")}
    token = ()
    tok = lambda t: () if t is None else (t,)
    for li in reversed(range(depth)):
        sv = saved[li]
        G, cw = sv["G"], sv["cw"]
        dz2, dz2b, dpg, dpp, dg2, db2 = ln2_ple_bwd(dh, sv["z2"], vec(ln2_g[li]), sv["pg"], sv["pp"], f"ln2_bwd{li}",
                                                    after=token)
        gw = {}
        gw["w_ple_gate"] = wgrad_rows(sv["h1b"], dpg, f"dw_pg{li}")
        gw["w_ple_proj"] = wgrad_cols(pb[li], dpp, f"dw_pp{li}")
        gw["w_down"] = wgrad_down(sv["t"], dz2b, f"dw_down{li}").reshape(N_DEV, FF_SHARD, D)
        dhv, dhg, dcw, dcb = ffn_act_bwd(dz2b, G["w_down"], sv["up"], cw, cb_full[li], li, f"ffn_bwd{li}")
        gw["w_up"] = wgrad_up(sv["h1b"], dhv, dhg, f"dw_up{li}")
        token = tok(ex.grads(li, 0, gw))
        dz1, dz1b, dg1, db1 = dh1_ln1_bwd(dz2, dpg, G["w_ple_gate"], dhv, dhg, G["w_up"], sv["z1"], vec(ln1_g[li]), li,
                                          f"ln1_bwd{li}", after=token)
        token = tok(ex.flush(li, 0, dz1b))
        gw = {"w_mix_out": wgrad_rows(sv["mg"], dz1b, f"dw_mix{li}", after=token)}
        dya, dyp, dga, dgb = merge_bwd(dz1b, G["w_mix_out"], sv["proj"], sv["ya"], sv["yp"], li, f"merge_bwd{li}")
        gw["w_attn_out"] = wgrad_cols(sv["att"], dya, f"dw_ao{li}")
        gw["w_pool_out"] = wgrad_cols(sv["pw"], dyp, f"dw_po{li}")
        token = tok(ex.grads(li, 1, gw))
        da = attn_out_bwd(dya, G["w_attn_out"], li, f"da{li}", after=token)
        du, dpool_w, dpool_sc = pool_bwd(dyp, G["w_pool_out"], sv["pm"], pool_w_b[li], vec(pool_scale[li]), li,
                                         f"pool_bwd{li}")
        token = tok(ex.flush(li, 1, du))
        dq, dk, dv, drpb = attn_bwd(sv["proj"], da, e_rev, li, f"attn_bwd{li}", after=token)
        dproj = jnp.concatenate([dq, dk, dv, du, dga, dgb], axis=1)
        dw_in, db_in = wgrad_cols(sv["hb"], dproj, f"dw_in{li}", with_colsum=True)
        token = tok(ex.grads(li, 2, {"w_in": dw_in}))
        dh = dh0_bwd(dz1, dproj, G["w_in"], li, f"dh0{li}", after=token)
        small["b_in"][li] = db_in.reshape(N_PROJ)
        small["rpb"][li] = drpb.reshape(N_HEADS, KROWS, GRID_W)[:, :2 * KH - 1, :2 * KW - 1]
        small["pool_w"][li] = dpool_w
        small["pool_scale"][li] = dpool_sc.reshape(D_POOL)
        small["ln1_g"][li], small["ln1_b"][li] = dg1.reshape(D), db1.reshape(D)
        small["ln2_g"][li], small["ln2_b"][li] = dg2.reshape(D), db2.reshape(D)
        small["conv_b"][li] = dcb.reshape(D_FF)
        small["conv_w"][li] = dcw.transpose(1, 0, 2).reshape(3, D_FF)
        token = tok(ex.flush(li, 2, dh))
        if li == 1:
            token = token + tok(ex.replicated_early(small, dh))
    dx, dg_in, db_in0 = ln_bwd(dh, x2, vec(W["ln_in_g"]), "ln_in_bwd", after=token)
    parts = {n: jnp.stack(v_) for n, v_ in small.items()}
    parts["ln_in_g"], parts["ln_in_b"] = dg_in.reshape(D), db_in0.reshape(D)
    return loss_part, dx, parts
```

```python
import numpy as np
import jax
import jax.numpy as jnp
from jax import lax
from jax.experimental import pallas as pl
from jax.experimental.pallas import tpu as pltpu

F32 = jnp.float32
BF16 = jnp.bfloat16
I32 = jnp.int32

D = 1024
DEPTH = 4
GRID_W = 64
N_HEADS = 8
HEAD_DIM = 64
D_ATTN = 512
KH = 8
KW = 16
POOL_WINDOWS = (2, 4, 8, 16)
D_POOL = 512
PGD = 128
D_FF = 2816
PLE_DIM = 256
N_PROJ = 4096
ALPHA = (2 * DEPTH) ** 0.25
LN_EPS = 1e-5
NEG_INF = -1e30
ATT_SCALE = HEAD_DIM ** -0.5
ADAM_LR = 0.001
ADAM_B1 = 0.9
ADAM_B2 = 0.999
ADAM_EPS = 1e-08
ADAM_WD = 0.01
ADAM_STEP = 10

N_DEV = 8
AXES = ("x", "y", "c")
FF_BLK = D_FF // 4
FF_SHARD = D_FF // N_DEV
QROWS = 8
KROWS = 16
QB = QROWS * GRID_W
KB = KROWS * GRID_W
V7X_VMEM_LIMIT = 56 * 2 ** 20
MESH = pl.DeviceIdType.MESH
ANY = pl.BlockSpec(memory_space=pl.ANY)


def _cparams(n_grid):
    return pltpu.CompilerParams(dimension_semantics=("arbitrary",) * n_grid, vmem_limit_bytes=V7X_VMEM_LIMIT)


def _nn(a, b):
    return lax.dot_general(a, b, (((1,), (0,)), ((), ())), preferred_element_type=F32)


def _nt(a, b):
    return lax.dot_general(a, b, (((1,), (1,)), ((), ())), preferred_element_type=F32)


def _tn(a, b):
    return lax.dot_general(a, b, (((0,), (0,)), ((), ())), preferred_element_type=F32)


def _sigmoid(x):
    return 1.0 / (1.0 + jnp.exp(-x))


def _ln_fwd(z, g, b):
    mu = jnp.mean(z, axis=-1, keepdims=True)
    xc = z - mu
    var = jnp.mean(xc * xc, axis=-1, keepdims=True)
    return xc * lax.rsqrt(var + LN_EPS) * g + b


def _ln_bwd(dh, z, g):
    mu = jnp.mean(z, axis=-1, keepdims=True)
    xc = z - mu
    var = jnp.mean(xc * xc, axis=-1, keepdims=True)
    rstd = lax.rsqrt(var + LN_EPS)
    xhat = xc * rstd
    dxh = dh * g
    m1 = jnp.mean(dxh, axis=-1, keepdims=True)
    m2 = jnp.mean(dxh * xhat, axis=-1, keepdims=True)
    return rstd * (dxh - m1 - xhat * m2), dh * xhat


def _colsum(x):
    return jnp.sum(x, axis=0, keepdims=True)


def _lane_cat(ref):
    return jnp.concatenate([ref[j] for j in range(ref.shape[0])], axis=1)


def _row_cat(ref):
    n, r, c = ref.shape
    return ref[...].reshape(n * r, c)


def _shards(n, r, c, li, j_of=None):
    del li
    if j_of is None:
        return pl.BlockSpec((n, r, c), lambda *_: (0, 0, 0))
    return pl.BlockSpec((n, r, c), lambda *g: (j_of(*g), 0, 0))


def _shard(r, c, li, j_of):
    del li
    return pl.BlockSpec((None, r, c), lambda *g: (j_of(*g), 0, 0))


def ln_fwd(x, g, b, name, after=()):
    s = x.shape[0]
    tm = 512
    na = len(after)

    def body(x_ref, g_ref, b_ref, *rest):
        h_ref, hb_ref = rest[na:]
        h = _ln_fwd(x_ref[...], g_ref[...], b_ref[...])
        h_ref[...] = h
        hb_ref[...] = h.astype(BF16)

    row = pl.BlockSpec((tm, D), lambda i: (i, 0))
    vec = pl.BlockSpec((1, D), lambda i: (0, 0))
    return pl.pallas_call(
        body, name=name, grid=(s // tm,), in_specs=[row, vec, vec] + [ANY] * na, out_specs=[row, row],
        out_shape=[jax.ShapeDtypeStruct((s, D), F32), jax.ShapeDtypeStruct((s, D), BF16)],
        compiler_params=_cparams(1))(x, g, b, *after)


def proj_fwd(hb, win, bias, li, name, after=()):
    s = hb.shape[0]
    bn = N_PROJ // N_DEV
    tm = 1024
    pool_shard = (3 * D_ATTN) // bn

    def body(a_ref, w_ref, b_ref, *rest):
        o_ref, u_ref = rest[-2:]
        acc = _nn(a_ref[...], w_ref[...]) + b_ref[...]
        o_ref[...] = acc.astype(BF16)

        @pl.when(pl.program_id(1) == pool_shard)
        def _():
            u_ref[...] = acc

    return pl.pallas_call(
        body, name=name, grid=(s // tm, N_DEV),
        in_specs=[pl.BlockSpec((tm, D), lambda i, j: (i, 0)),
                  _shard(D, bn, li, lambda i, j: j),
                  pl.BlockSpec((1, bn), lambda i, j: (0, j))] + [ANY] * len(after),
        out_specs=[pl.BlockSpec((tm, bn), lambda i, j: (i, j)), pl.BlockSpec((tm, bn), lambda i, j: (i, 0))],
        out_shape=[jax.ShapeDtypeStruct((s, N_PROJ), BF16), jax.ShapeDtypeStruct((s, D_POOL), F32)],
        compiler_params=_cparams(2))(hb, win, bias, *after)


def _attn_types(b, nb):
    first, last = 0, (nb * QROWS - KROWS) * GRID_W
    mid = pl.multiple_of((QROWS * b - KH // 2) * GRID_W, 256)
    return ((b == 0, first), ((b > 0) & (b < nb - 1), mid), (b == nb - 1, last))


def _attn_row(btype, qr):
    lo, delta = ((max(qr - KH // 2, 0), 0), (qr, -(KH // 2)), (min(qr + KH // 2, KH), -KH))[btype]
    return lo, (qr - delta - (KH - 1)) % KROWS, lo - qr + delta + KH - 1


def _row_window(lo):
    pad = (lo % 2) * GRID_W
    return (lo // 2) * 128, KH * GRID_W + 2 * pad, pad


def _lanes(ref, start, width):
    start %= KB
    if start + width <= KB:
        return ref[:, start:start + width]
    return jnp.concatenate([ref[:, start:], ref[:, :start + width - KB]], axis=1)


def _row_logits(s_ref, e_ref, hh, rows, btype, qr):
    lo, shift, _ = _attn_row(btype, qr)
    a0, w, pad = _row_window(lo)
    e = e_ref.at[hh, shift % 2]
    sb = s_ref[rows, a0:a0 + w] + _lanes(e, a0 - (shift - shift % 2) * GRID_W, w)
    if pad:
        lane = lax.broadcasted_iota(I32, (1, w), 1)
        sb = jnp.where((lane >= pad) & (lane < w - pad), sb, NEG_INF)
    return sb, a0, w, pad


def _store_row(ref, rows, a0, w, val):
    if a0:
        ref[rows, 0:a0] = jnp.zeros((GRID_W, a0), ref.dtype)
    ref[rows, a0:a0 + w] = val.astype(ref.dtype)
    if a0 + w < KB:
        ref[rows, a0 + w:KB] = jnp.zeros((GRID_W, KB - a0 - w), ref.dtype)


def attn_fwd(proj, e_tab, li, name):
    s = proj.shape[0]
    nb = s // QB

    def body(q_ref, k_ref, v_ref, e_ref, o_ref, s_ref, p_ref):
        q = q_ref[...] * ATT_SCALE
        lane = lax.broadcasted_iota(I32, (1, 128), 1)

        def block(btype, k0):
            kwin = k_ref[pl.ds(k0, KB), :]
            vwin = v_ref[pl.ds(k0, KB), :]
            acc = jnp.zeros((QB, 128), F32)
            for hh in range(2):
                lm = (lane // HEAD_DIM) == hh
                qh = jnp.where(lm, q, jnp.zeros_like(q))
                vh = jnp.where(lm, vwin, jnp.zeros_like(vwin))
                s_ref[...] = _nt(qh, kwin)
                for qr in range(QROWS):
                    rows = slice(qr * GRID_W, (qr + 1) * GRID_W)
                    sb, a0, w, _ = _row_logits(s_ref, e_ref, hh, rows, btype, qr)
                    p = jnp.exp(sb - jnp.max(sb, axis=1, keepdims=True))
                    _store_row(p_ref, rows, a0, w, p * (1.0 / jnp.sum(p, axis=1, keepdims=True)))
                acc = acc + _nn(p_ref[...], vh)
            o_ref[...] = acc.astype(BF16)

        for btype, (cond, k0) in enumerate(_attn_types(pl.program_id(1), nb)):
            pl.when(cond)(lambda btype=btype, k0=k0: block(btype, k0))

    return pl.pallas_call(
        body, name=name, grid=(4, nb),
        in_specs=[pl.BlockSpec((QB, 128), lambda j, b: (b, j)),
                  pl.BlockSpec((s, 128), lambda j, b: (0, 4 + j)),
                  pl.BlockSpec((s, 128), lambda j, b: (0, 8 + j)),
                  pl.BlockSpec((None, 2, 2, GRID_W, KB), lambda j, b: (li, j, 0, 0, 0))],
        out_specs=pl.BlockSpec((QB, 128), lambda j, b: (b, j)),
        out_shape=jax.ShapeDtypeStruct((s, D_ATTN), BF16),
        scratch_shapes=[pltpu.VMEM((QB, KB), F32), pltpu.VMEM((QB, KB), BF16)],
        compiler_params=_cparams(2))(proj, proj, proj, e_tab)


_POOL_PAD = 8


def _pool_counts(s, w):
    t = lax.broadcasted_iota(I32, (s, 1), 0)
    return (jnp.minimum(t + w // 2, s) - jnp.maximum(t - w // 2, 0)).astype(F32)


def _window_sum(x, w, back_first):
    s = x.shape[0]
    z = jnp.zeros((_POOL_PAD, x.shape[1]), F32)
    xe = jnp.concatenate([z, x, z], axis=0)
    n = s + 2 * _POOL_PAD
    acc = xe + pltpu.roll(xe, 1 if back_first else n - 1, 0)
    k = 1
    while 2 * k < w:
        acc = pltpu.roll(acc, k, 0) + pltpu.roll(acc, n - k, 0)
        k *= 2
    return acc[_POOL_PAD:_POOL_PAD + s, :]


def pool_fwd(u, pool_w, pool_scale, name):
    s = u.shape[0]

    def body(u_ref, w_ref, sc_ref, pm_ref, pw_ref):
        for g, w in enumerate(POOL_WINDOWS):
            cols = slice(g * PGD, (g + 1) * PGD)
            ug = u_ref[:, cols]
            pm = (_window_sum(ug, w, True) / _pool_counts(s, w) - ug).astype(BF16)
            pm_ref[:, cols] = pm
            pw_ref[:, cols] = (_nn(pm, w_ref[g]) * sc_ref[:, cols]).astype(BF16)

    full = lambda shape: pl.BlockSpec(shape, lambda i: (0,) * len(shape))
    return pl.pallas_call(
        body, name=name, grid=(1,),
        in_specs=[full((s, D_POOL)), full((4, PGD, PGD)), full((1, D_POOL))],
        out_specs=[full((s, D_POOL)), full((s, D_POOL))],
        out_shape=[jax.ShapeDtypeStruct((s, D_POOL), BF16)] * 2,
        compiler_params=_cparams(1))(u, pool_w, pool_scale)


def merge_fwd(a, pw, wao, wpo, proj, li, name, after=()):
    s = a.shape[0]
    tm, tn = 512, 512
    nt = D // tn
    per = tn // 128

    def body(a_ref, pw_ref, wa_ref, wp_ref, ga_ref, gb_ref, *rest):
        mg_ref, ya_ref, yp_ref = rest[len(after):]
        ya = _nn(a_ref[...], _lane_cat(wa_ref))
        yp = _nn(pw_ref[...], _lane_cat(wp_ref))
        mg = _sigmoid(ga_ref[...].astype(F32)) * ya + _sigmoid(gb_ref[...].astype(F32)) * yp
        mg_ref[...] = mg.astype(BF16)
        ya_ref[...] = ya.astype(BF16)
        yp_ref[...] = yp.astype(BF16)

    act = pl.BlockSpec((tm, D_ATTN), lambda i, j: (i, 0))
    wsp = _shards(per, D_ATTN, 128, li, lambda i, j: j)
    out = pl.BlockSpec((tm, tn), lambda i, j: (i, j))
    ga0 = (3 * D_ATTN + D_POOL) // tn
    return pl.pallas_call(
        body, name=name, grid=(s // tm, nt),
        in_specs=[act, act, wsp, wsp,
                  pl.BlockSpec((tm, tn), lambda i, j: (i, ga0 + j)),
                  pl.BlockSpec((tm, tn), lambda i, j: (i, ga0 + nt + j))] + [ANY] * len(after),
        out_specs=[out, out, out],
        out_shape=[jax.ShapeDtypeStruct((s, D), BF16)] * 3,
        compiler_params=_cparams(2))(a, pw, wao, wpo, proj, proj, *after)


def mix_ln_fwd(mg, wmix, h0, g, b, li, name):
    s = mg.shape[0]
    tm = 256

    def body(mg_ref, w_ref, h0_ref, g_ref, b_ref, z_ref, h_ref, hb_ref):
        z = ALPHA * h0_ref[...] + _nn(mg_ref[...], _row_cat(w_ref))
        h = _ln_fwd(z, g_ref[...], b_ref[...])
        z_ref[...] = z
        h_ref[...] = h
        hb_ref[...] = h.astype(BF16)

    row = pl.BlockSpec((tm, D), lambda i: (i, 0))
    vec = pl.BlockSpec((1, D), lambda i: (0, 0))
    return pl.pallas_call(
        body, name=name, grid=(s // tm,),
        in_specs=[row, _shards(N_DEV, D // N_DEV, D, li), row, vec, vec],
        out_specs=[row, row, row],
        out_shape=[jax.ShapeDtypeStruct((s, D), F32), jax.ShapeDtypeStruct((s, D), F32),
                   jax.ShapeDtypeStruct((s, D), BF16)],
        compiler_params=_cparams(1))(mg, wmix, h0, g, b)


def up_fwd(hb, wup, li, name):
    s = hb.shape[0]
    tm = 1024

    def body(a_ref, w_ref, o_ref):
        o_ref[...] = _nt(a_ref[...], w_ref[...]).astype(BF16)

    return pl.pallas_call(
        body, name=name, grid=(s // tm, N_DEV),
        in_specs=[pl.BlockSpec((tm, D), lambda i, j: (i, 0)), _shard(FF_BLK, D, li, lambda i, j: j)],
        out_specs=pl.BlockSpec((None, tm, FF_BLK), lambda i, j: (j, i, 0)),
        out_shape=jax.ShapeDtypeStruct((N_DEV, s, FF_BLK), BF16),
        compiler_params=_cparams(2))(hb, wup)


_SQRT_HALF = 0.7071067811865476
_INV_SQRT_2PI = 0.3989422804014327


def _shift_rows(x, prev_row, next_row):
    n = x.shape[0]
    r = lax.broadcasted_iota(I32, (n, 1), 0)
    back = jnp.where(r == 0, prev_row, pltpu.roll(x, 1, 0))
    fwd = jnp.where(r == n - 1, next_row, pltpu.roll(x, n - 1, 0))
    return back, fwd


HALO = 16


def _halo_maps(tm, s):
    th = tm // HALO
    return (lambda i: jnp.maximum(i * th - 1, 0)), (lambda i: jnp.minimum((i + 1) * th, s // HALO - 1))


def _slab_specs(tm, s, blk_of):
    before, after = _halo_maps(tm, s)
    main = pl.BlockSpec((None, tm, FF_BLK), lambda c, i: (blk_of(c), i, 0))
    prev = pl.BlockSpec((None, HALO, FF_BLK), lambda c, i: (blk_of(c), before(i), 0))
    nxt = pl.BlockSpec((None, HALO, FF_BLK), lambda c, i: (blk_of(c), after(i), 0))
    return main, prev, nxt


def ffn_act_fwd(up, conv_w, conv_b, name):
    s = up.shape[1]
    tm = 512
    nt = s // tm
    hv_main, _, _ = _slab_specs(tm, s, lambda c: c)
    hg_main, hg_prev, hg_next = _slab_specs(tm, s, lambda c: 4 + c)

    def body(hv_ref, hg_ref, hp_ref, hn_ref, cw_ref, cb_ref, t_ref):
        i = pl.program_id(1)
        hg = hg_ref[...].astype(F32)
        prow = jnp.where(i == 0, 0.0, hp_ref[...].astype(F32)[HALO - 1:HALO, :])
        nrow = jnp.where(i == nt - 1, 0.0, hn_ref[...].astype(F32)[0:1, :])
        back, fwd = _shift_rows(hg, prow, nrow)
        c = back * cw_ref[0:1, :] + hg * cw_ref[1:2, :] + fwd * cw_ref[2:3, :] + cb_ref[...]
        act = 0.5 * c * (1.0 + lax.erf(c * _SQRT_HALF))
        t_ref[...] = (act * hv_ref[...].astype(F32)).astype(BF16)

    return pl.pallas_call(
        body, name=name, grid=(4, nt),
        in_specs=[hv_main, hg_main, hg_prev, hg_next,
                  pl.BlockSpec((None, 3, FF_BLK), lambda c, i: (c, 0, 0)),
                  pl.BlockSpec((None, 1, FF_BLK), lambda c, i: (c, 0, 0))],
        out_specs=pl.BlockSpec((None, tm, FF_BLK), lambda c, i: (c, i, 0)),
        out_shape=jax.ShapeDtypeStruct((4, s, FF_BLK), BF16),
        compiler_params=_cparams(2))(up, up, up, up, conv_w, conv_b)


def down_ple_ln_fwd(t, wdown, hb, wpg, pb, wpp, h1, g, b, li, name):
    s = hb.shape[0]
    tm = 256

    def body(t_ref, wd_ref, hb_ref, wpg_ref, p_ref, wpp_ref, h1_ref, g_ref, b_ref,
             z_ref, h_ref, hbo_ref, pg_ref, pp_ref):
        wd = _row_cat(wd_ref)
        ffn = _nn(t_ref[0], wd[0:FF_BLK, :])
        for c in range(1, 4):
            ffn = ffn + _nn(t_ref[c], wd[c * FF_BLK:(c + 1) * FF_BLK, :])
        pg = _nn(hb_ref[...], _row_cat(wpg_ref))
        pp = _nn(p_ref[...], _lane_cat(wpp_ref))
        z = ALPHA * h1_ref[...] + ffn + _sigmoid(pg) * pp
        h = _ln_fwd(z, g_ref[...], b_ref[...])
        z_ref[...] = z
        h_ref[...] = h
        hbo_ref[...] = h.astype(BF16)
        pg_ref[...] = pg.astype(BF16)
        pp_ref[...] = pp.astype(BF16)

    row = pl.BlockSpec((tm, D), lambda i: (i, 0))
    vec = pl.BlockSpec((1, D), lambda i: (0, 0))
    return pl.pallas_call(
        body, name=name, grid=(s // tm,),
        in_specs=[pl.BlockSpec((4, tm, FF_BLK), lambda i: (0, i, 0)),
                  _shards(N_DEV, FF_SHARD, D, li),
                  row, _shards(N_DEV, D // N_DEV, D, li),
                  pl.BlockSpec((tm, PLE_DIM), lambda i: (i, 0)),
                  _shards(N_DEV, PLE_DIM, 128, li),
                  row, vec, vec],
        out_specs=[row] * 5,
        out_shape=[jax.ShapeDtypeStruct((s, D), F32), jax.ShapeDtypeStruct((s, D), F32),
                   jax.ShapeDtypeStruct((s, D), BF16), jax.ShapeDtypeStruct((s, D), BF16),
                   jax.ShapeDtypeStruct((s, D), BF16)],
        compiler_params=_cparams(1))(t, wdown, hb, wpg, pb, wpp, h1, g, b)


def loss_bwd(h, target, name):
    s = h.shape[0]
    tm = 512

    def body(h_ref, t_ref, dh_ref, l_ref):
        @pl.when(pl.program_id(0) == 0)
        def _():
            l_ref[...] = jnp.zeros_like(l_ref)
        e = h_ref[...] - t_ref[...]
        dh_ref[...] = e * (1.0 / D)
        l_ref[...] += 0.5 * jnp.sum(jnp.mean(e * e, axis=-1, keepdims=True), axis=0, keepdims=True)

    row = pl.BlockSpec((tm, D), lambda i: (i, 0))
    return pl.pallas_call(
        body, name=name, grid=(s // tm,), in_specs=[row, row],
        out_specs=[row, pl.BlockSpec((1, 1), lambda i: (0, 0))],
        out_shape=[jax.ShapeDtypeStruct((s, D), F32), jax.ShapeDtypeStruct((1, 1), F32)],
        compiler_params=_cparams(1))(h, target)


def ln_bwd(dh, z, g, name, after=()):
    s = dh.shape[0]
    tm = 512
    na = len(after)

    def body(dh_ref, z_ref, g_ref, *rest):
        dz_ref, dg_ref, db_ref = rest[na:]

        @pl.when(pl.program_id(0) == 0)
        def _():
            dg_ref[...] = jnp.zeros_like(dg_ref)
            db_ref[...] = jnp.zeros_like(db_ref)
        dh = dh_ref[...]
        dz, dgx = _ln_bwd(dh, z_ref[...], g_ref[...])
        dz_ref[...] = dz
        dg_ref[...] += _colsum(dgx)
        db_ref[...] += _colsum(dh)

    row = pl.BlockSpec((tm, D), lambda i: (i, 0))
    vec = pl.BlockSpec((1, D), lambda i: (0, 0))
    return pl.pallas_call(
        body, name=name, grid=(s // tm,), in_specs=[row, row, vec] + [ANY] * na, out_specs=[row, vec, vec],
        out_shape=[jax.ShapeDtypeStruct((s, D), F32), jax.ShapeDtypeStruct((1, D), F32),
                   jax.ShapeDtypeStruct((1, D), F32)],
        compiler_params=_cparams(1))(dh, z, g, *after)


def ln2_ple_bwd(dh, z, g, pg, pp, name, after=()):
    s = dh.shape[0]
    tm = 512
    na = len(after)

    def body(dh_ref, z_ref, g_ref, pg_ref, pp_ref, *rest):
        dz_ref, dzb_ref, dpg_ref, dpp_ref, dg_ref, db_ref = rest[na:]

        @pl.when(pl.program_id(0) == 0)
        def _():
            dg_ref[...] = jnp.zeros_like(dg_ref)
            db_ref[...] = jnp.zeros_like(db_ref)
        dh = dh_ref[...]
        dz, dgx = _ln_bwd(dh, z_ref[...], g_ref[...])
        sg = _sigmoid(pg_ref[...].astype(F32))
        dz_ref[...] = dz
        dzb_ref[...] = dz.astype(BF16)
        dpg_ref[...] = (dz * pp_ref[...].astype(F32) * sg * (1.0 - sg)).astype(BF16)
        dpp_ref[...] = (dz * sg).astype(BF16)
        dg_ref[...] += _colsum(dgx)
        db_ref[...] += _colsum(dh)

    row = pl.BlockSpec((tm, D), lambda i: (i, 0))
    vec = pl.BlockSpec((1, D), lambda i: (0, 0))
    return pl.pallas_call(
        body, name=name, grid=(s // tm,), in_specs=[row, row, vec, row, row] + [ANY] * na,
        out_specs=[row, row, row, row, vec, vec],
        out_shape=[jax.ShapeDtypeStruct((s, D), F32)] + [jax.ShapeDtypeStruct((s, D), BF16)] * 3
        + [jax.ShapeDtypeStruct((1, D), F32)] * 2,
        compiler_params=_cparams(1))(dh, z, g, pg, pp, *after)


def wgrad_rows(a, dy, name, after=()):
    s, k = a.shape
    n = dy.shape[1]
    kb = k // N_DEV

    def body(a_ref, dy_ref, *rest):
        rest[-1][...] = _tn(a_ref[...], dy_ref[...]).astype(BF16)

    return pl.pallas_call(
        body, name=name, grid=(N_DEV,),
        in_specs=[pl.BlockSpec((s, kb), lambda j: (0, j)), pl.BlockSpec((s, n), lambda j: (0, 0))] + [ANY] * len(after),
        out_specs=pl.BlockSpec((None, kb, n), lambda j: (j, 0, 0)),
        out_shape=jax.ShapeDtypeStruct((N_DEV, kb, n), BF16),
        compiler_params=_cparams(1))(a, dy, *after)


def wgrad_cols(a, dy, name):
    s, k = a.shape
    n = dy.shape[1]
    nb = n // N_DEV

    def body(a_ref, dy_ref, o_ref):
        o_ref[...] = _tn(a_ref[...], dy_ref[...]).astype(BF16)

    return pl.pallas_call(
        body, name=name, grid=(N_DEV,),
        in_specs=[pl.BlockSpec((s, k), lambda j: (0, 0)), pl.BlockSpec((s, nb), lambda j: (0, j))],
        out_specs=pl.BlockSpec((None, k, nb), lambda j: (j, 0, 0)),
        out_shape=jax.ShapeDtypeStruct((N_DEV, k, nb), BF16),
        compiler_params=_cparams(1))(a, dy)


def wgrad_in(a, pieces, name):
    s, k = a.shape
    bn = N_PROJ // N_DEV
    n_narrow = 4

    def body(a_ref, *refs):
        dy_refs, (o_ref, cs_ref) = refs[:6], refs[6:]
        j = pl.program_id(0)

        def emit(dy_ref):
            dy = dy_ref[...]
            o_ref[...] = _tn(a_ref[...], dy).astype(BF16)
            cs_ref[...] = _colsum(dy.astype(F32))

        for idx in range(n_narrow):
            pl.when(j == idx)(lambda idx=idx: emit(dy_refs[idx]))
        pl.when((j >= n_narrow) & (j < n_narrow + 2))(lambda: emit(dy_refs[4]))
        pl.when(j >= n_narrow + 2)(lambda: emit(dy_refs[5]))

    narrow = pl.BlockSpec((s, bn), lambda j: (0, 0))
    return pl.pallas_call(
        body, name=name, grid=(N_DEV,),
        in_specs=[pl.BlockSpec((s, k), lambda j: (0, 0))] + [narrow] * n_narrow
        + [pl.BlockSpec((s, bn), lambda j: (0, jnp.clip(j - n_narrow, 0, 1))),
           pl.BlockSpec((s, bn), lambda j: (0, jnp.clip(j - n_narrow - 2, 0, 1)))],
        out_specs=[pl.BlockSpec((None, k, bn), lambda j: (j, 0, 0)), pl.BlockSpec((1, bn), lambda j: (0, j))],
        out_shape=[jax.ShapeDtypeStruct((N_DEV, k, bn), BF16), jax.ShapeDtypeStruct((1, N_PROJ), F32)],
        compiler_params=_cparams(1))(a, *pieces)


def wgrad_down(t, dy, name):
    _, s, k = t.shape
    n = dy.shape[1]

    def body(a_ref, dy_ref, o_ref):
        o_ref[...] = _tn(a_ref[...], dy_ref[...]).astype(BF16)

    return pl.pallas_call(
        body, name=name, grid=(4,),
        in_specs=[pl.BlockSpec((None, s, k), lambda j: (j, 0, 0)), pl.BlockSpec((s, n), lambda j: (0, 0))],
        out_specs=pl.BlockSpec((None, k, n), lambda j: (j, 0, 0)),
        out_shape=jax.ShapeDtypeStruct((4, k, n), BF16),
        compiler_params=_cparams(1))(t, dy)


def wgrad_up(a, dhv, dhg, name):
    s, k = a.shape

    def body(a_ref, dv_ref, dg_ref, o_ref):
        j = pl.program_id(0)

        @pl.when(j < 4)
        def _():
            o_ref[...] = _tn(dv_ref[...], a_ref[...]).astype(BF16)

        @pl.when(j >= 4)
        def _():
            o_ref[...] = _tn(dg_ref[...], a_ref[...]).astype(BF16)

    return pl.pallas_call(
        body, name=name, grid=(N_DEV,),
        in_specs=[pl.BlockSpec((s, k), lambda j: (0, 0)),
                  pl.BlockSpec((None, s, FF_BLK), lambda j: (jnp.minimum(j, 3), 0, 0)),
                  pl.BlockSpec((None, s, FF_BLK), lambda j: (jnp.maximum(j - 4, 0), 0, 0))],
        out_specs=pl.BlockSpec((None, FF_BLK, k), lambda j: (j, 0, 0)),
        out_shape=jax.ShapeDtypeStruct((N_DEV, FF_BLK, k), BF16),
        compiler_params=_cparams(1))(a, dhv, dhg)


def ffn_act_bwd(dzb, wdown, up, conv_w, conv_b, li, name):
    s = up.shape[1]
    tm = 512
    nt = s // tm
    before, after = _halo_maps(tm, s)
    hv_main, hv_prev, hv_next = _slab_specs(tm, s, lambda c: c)
    hg_main, hg_prev, hg_next = _slab_specs(tm, s, lambda c: 4 + c)

    def dc_of(dz, wd, hv, hg, back, fwd, cw_ref, cb_ref):
        dt = _nt(dz, wd)
        c = back * cw_ref[0:1, :] + hg * cw_ref[1:2, :] + fwd * cw_ref[2:3, :] + cb_ref[...]
        cdf = 0.5 * (1.0 + lax.erf(c * _SQRT_HALF))
        pdf = jnp.exp(-0.5 * c * c) * _INV_SQRT_2PI
        return dt, c * cdf, dt * hv * (cdf + c * pdf)

    def body(dz_ref, dzp_ref, dzn_ref, wd_ref, hv_ref, hvp_ref, hvn_ref, hg_ref, hgp_ref, hgn_ref, cw_ref, cb_ref,
             dhv_ref, dhg_ref, dcw_ref, dcb_ref):
        i = pl.program_id(1)

        @pl.when(i == 0)
        def _():
            dcw_ref[...] = jnp.zeros_like(dcw_ref)
            dcb_ref[...] = jnp.zeros_like(dcb_ref)

        wd = _row_cat(wd_ref)
        hg = hg_ref[...].astype(F32)
        hgp = hgp_ref[...].astype(F32)
        hgn = hgn_ref[...].astype(F32)
        first, last = i == 0, i == nt - 1
        e = HALO - 1
        back, fwd = _shift_rows(hg, jnp.where(first, 0.0, hgp[e:e + 1, :]), jnp.where(last, 0.0, hgn[0:1, :]))
        dt, act, dc = dc_of(dz_ref[...], wd, hv_ref[...].astype(F32), hg, back, fwd, cw_ref, cb_ref)
        dhv_ref[...] = (dt * act).astype(BF16)
        bp, fp = _shift_rows(hgp, hgp[0:1, :], hg[0:1, :])
        _, _, dcp = dc_of(dzp_ref[...], wd, hvp_ref[...].astype(F32), hgp, bp, fp, cw_ref, cb_ref)
        bn, fn = _shift_rows(hgn, hg[tm - 1:tm, :], hgn[e:e + 1, :])
        _, _, dcn = dc_of(dzn_ref[...], wd, hvn_ref[...].astype(F32), hgn, bn, fn, cw_ref, cb_ref)
        dc_back, dc_fwd = _shift_rows(dc, jnp.where(first, 0.0, dcp[e:e + 1, :]), jnp.where(last, 0.0, dcn[0:1, :]))
        dhg_ref[...] = (dc_fwd * cw_ref[0:1, :] + dc * cw_ref[1:2, :] + dc_back * cw_ref[2:3, :]).astype(BF16)
        dcw_ref[0:1, :] += _colsum(dc * back)
        dcw_ref[1:2, :] += _colsum(dc * hg)
        dcw_ref[2:3, :] += _colsum(dc * fwd)
        dcb_ref[...] += _colsum(dc)

    out_slab = pl.BlockSpec((None, tm, FF_BLK), lambda c, i: (c, i, 0))
    cw_spec = pl.BlockSpec((None, 3, FF_BLK), lambda c, i: (c, 0, 0))
    cb_spec = pl.BlockSpec((None, 1, FF_BLK), lambda c, i: (c, 0, 0))
    return pl.pallas_call(
        body, name=name, grid=(4, nt),
        in_specs=[pl.BlockSpec((tm, D), lambda c, i: (i, 0)),
                  pl.BlockSpec((HALO, D), lambda c, i: (before(i), 0)),
                  pl.BlockSpec((HALO, D), lambda c, i: (after(i), 0)),
                  _shards(2, FF_SHARD, D, li, lambda c, i: c),
                  hv_main, hv_prev, hv_next, hg_main, hg_prev, hg_next, cw_spec, cb_spec],
        out_specs=[out_slab, out_slab, cw_spec, cb_spec],
        out_shape=[jax.ShapeDtypeStruct((4, s, FF_BLK), BF16), jax.ShapeDtypeStruct((4, s, FF_BLK), BF16),
                   jax.ShapeDtypeStruct((4, 3, FF_BLK), F32), jax.ShapeDtypeStruct((4, 1, FF_BLK), F32)],
        compiler_params=_cparams(2))(dzb, dzb, dzb, wdown, up, up, up, up, up, up, conv_w, conv_b)


def dh1_ln1_bwd(dz2, dpg, wpg, dhv, dhg, wup, z1, g1, li, name, after=()):
    s = dz2.shape[0]
    tm = 256
    na = len(after)

    def body(dz2_ref, dpg_ref, wpg_ref, dhv_ref, dhg_ref, wup_ref, z1_ref, g_ref, *rest):
        dz_ref, dzb_ref, dg_ref, db_ref = rest[na:]

        @pl.when(pl.program_id(0) == 0)
        def _():
            dg_ref[...] = jnp.zeros_like(dg_ref)
            db_ref[...] = jnp.zeros_like(db_ref)
        dh = ALPHA * dz2_ref[...] + _nt(dpg_ref[...], _row_cat(wpg_ref))
        for c in range(4):
            dh = dh + _nn(dhv_ref[c], wup_ref[c]) + _nn(dhg_ref[c], wup_ref[4 + c])
        dz, dgx = _ln_bwd(dh, z1_ref[...], g_ref[...])
        dz_ref[...] = dz
        dzb_ref[...] = dz.astype(BF16)
        dg_ref[...] += _colsum(dgx)
        db_ref[...] += _colsum(dh)

    row = pl.BlockSpec((tm, D), lambda i: (i, 0))
    vec = pl.BlockSpec((1, D), lambda i: (0, 0))
    slab = pl.BlockSpec((4, tm, FF_BLK), lambda i: (0, i, 0))
    return pl.pallas_call(
        body, name=name, grid=(s // tm,),
        in_specs=[row, row, _shards(N_DEV, D // N_DEV, D, li), slab, slab, _shards(N_DEV, FF_BLK, D, li), row, vec]
        + [ANY] * na,
        out_specs=[row, row, vec, vec],
        out_shape=[jax.ShapeDtypeStruct((s, D), F32), jax.ShapeDtypeStruct((s, D), BF16),
                   jax.ShapeDtypeStruct((1, D), F32), jax.ShapeDtypeStruct((1, D), F32)],
        compiler_params=_cparams(1))(dz2, dpg, wpg, dhv, dhg, wup, z1, g1, *after)


def merge_bwd(dz1b, wmix, proj, ya, yp, li, name):
    s = dz1b.shape[0]
    tm, tn = 512, 512
    nt = D // tn
    per = tn // (D // N_DEV)
    ga0 = (3 * D_ATTN + D_POOL) // tn

    def body(dz_ref, w_ref, ga_ref, gb_ref, ya_ref, yp_ref, dya_ref, dyp_ref, dga_ref, dgb_ref):
        dm = _nt(dz_ref[...], _row_cat(w_ref))
        sa = _sigmoid(ga_ref[...].astype(F32))
        sb = _sigmoid(gb_ref[...].astype(F32))
        dya_ref[...] = (dm * sa).astype(BF16)
        dyp_ref[...] = (dm * sb).astype(BF16)
        dga_ref[...] = (dm * ya_ref[...].astype(F32) * sa * (1.0 - sa)).astype(BF16)
        dgb_ref[...] = (dm * yp_ref[...].astype(F32) * sb * (1.0 - sb)).astype(BF16)

    tile = pl.BlockSpec((tm, tn), lambda i, j: (i, j))
    return pl.pallas_call(
        body, name=name, grid=(s // tm, nt),
        in_specs=[pl.BlockSpec((tm, D), lambda i, j: (i, 0)),
                  _shards(per, D // N_DEV, D, li, lambda i, j: j),
                  pl.BlockSpec((tm, tn), lambda i, j: (i, ga0 + j)),
                  pl.BlockSpec((tm, tn), lambda i, j: (i, ga0 + nt + j)),
                  tile, tile],
        out_specs=[tile] * 4,
        out_shape=[jax.ShapeDtypeStruct((s, D), BF16)] * 4,
        compiler_params=_cparams(2))(dz1b, wmix, proj, proj, ya, yp)


def attn_out_bwd(dya, wao, li, name, after=()):
    s = dya.shape[0]
    tm = 512

    def body(d_ref, w_ref, *rest):
        rest[-1][...] = _nt(d_ref[...], _lane_cat(w_ref)).astype(BF16)

    return pl.pallas_call(
        body, name=name, grid=(s // tm,),
        in_specs=[pl.BlockSpec((tm, D), lambda i: (i, 0)), _shards(N_DEV, D_ATTN, 128, li)] + [ANY] * len(after),
        out_specs=pl.BlockSpec((tm, D_ATTN), lambda i: (i, 0)),
        out_shape=jax.ShapeDtypeStruct((s, D_ATTN), BF16),
        compiler_params=_cparams(1))(dya, wao, *after)


def pool_bwd(dyp, wpo, pm, pool_w, pool_scale, li, name):
    s = dyp.shape[0]

    def body(dyp_ref, wpo_ref, pm_ref, w_ref, sc_ref, du_ref, dw_ref, dsc_ref):
        wpo = _lane_cat(wpo_ref)
        dyp = dyp_ref[...]
        for g, w in enumerate(POOL_WINDOWS):
            cols = slice(g * PGD, (g + 1) * PGD)
            dpw = _nt(dyp, wpo[g * PGD:(g + 1) * PGD, :])
            pmg = pm_ref[:, cols]
            dsc_ref[:, cols] = _colsum(dpw * _nn(pmg, w_ref[g]))
            dpmw = (dpw * sc_ref[:, cols]).astype(BF16)
            dw_ref[g] = _tn(pmg, dpmw)
            dpm = _nt(dpmw, w_ref[g])
            du_ref[:, cols] = (_window_sum(dpm / _pool_counts(s, w), w, False) - dpm).astype(BF16)

    full = lambda shape: pl.BlockSpec(shape, lambda i: (0,) * len(shape))
    return pl.pallas_call(
        body, name=name, grid=(1,),
        in_specs=[full((s, D)), _shards(N_DEV, D_POOL, 128, li), full((s, D_POOL)), full((4, PGD, PGD)),
                  full((1, D_POOL))],
        out_specs=[full((s, D_POOL)), full((4, PGD, PGD)), full((1, D_POOL))],
        out_shape=[jax.ShapeDtypeStruct((s, D_POOL), BF16), jax.ShapeDtypeStruct((4, PGD, PGD), F32),
                   jax.ShapeDtypeStruct((1, D_POOL), F32)],
        compiler_params=_cparams(1))(dyp, wpo, pm, pool_w, pool_scale)


def attn_bwd(proj, da, e_rev, li, name, after=()):
    s = proj.shape[0]
    nb = s // QB
    skew = GRID_W + (GRID_W - KW)

    def body(q_ref, k_ref, v_ref, do_ref, e_ref, *rest):
        dq_ref, dk_ref, dv_ref, g_ref, s_ref, dp_ref, ds_ref, p_ref, dkt_acc, dvt_acc = rest[len(after):]
        b = pl.program_id(1)

        @pl.when(b == 0)
        def _():
            dkt_acc[...] = jnp.zeros_like(dkt_acc)
            dvt_acc[...] = jnp.zeros_like(dvt_acc)
            g_ref[...] = jnp.zeros_like(g_ref)

        ri = lax.broadcasted_iota(I32, (QB, QB), 0)
        ci = lax.broadcasted_iota(I32, (QB, QB), 1)
        rev = jnp.where(ri + ci == QB - 1, 1.0, 0.0).astype(BF16)
        q = _nn(rev, q_ref[...]).astype(BF16) * ATT_SCALE
        do = _nn(rev, do_ref[...]).astype(BF16)
        lane = lax.broadcasted_iota(I32, (1, 128), 1)

        def block(btype, k0):
            kwin = k_ref[pl.ds(k0, KB), :]
            vwin = v_ref[pl.ds(k0, KB), :]
            dq = jnp.zeros((QB, 128), F32)
            for hh in range(2):
                lm = (lane // HEAD_DIM) == hh
                qh = jnp.where(lm, q, jnp.zeros_like(q))
                doh = jnp.where(lm, do, jnp.zeros_like(do))
                kh = jnp.where(lm, kwin, jnp.zeros_like(kwin))
                s_ref[...] = _nt(qh, kwin)
                dp_ref[...] = _nt(doh, vwin)
                g = jnp.zeros((1, KB), F32)
                for ib in range(QROWS):
                    qr = QROWS - 1 - ib
                    rows = slice(ib * GRID_W, (ib + 1) * GRID_W)
                    sb, a0, w, pad = _row_logits(s_ref, e_ref, hh, rows, btype, qr)
                    p = jnp.exp(sb - jnp.max(sb, axis=1, keepdims=True))
                    p = p * (1.0 / jnp.sum(p, axis=1, keepdims=True))
                    dp = dp_ref[rows, a0:a0 + w]
                    ds = p * (dp - jnp.sum(p * dp, axis=1, keepdims=True))
                    _store_row(ds_ref, rows, a0, w, ds)
                    _store_row(p_ref, rows, a0, w, p)
                    t = jnp.sum(pltpu.roll(ds, w - skew, 1, stride=1, stride_axis=0), axis=0, keepdims=True)
                    t = t[:, :KH * GRID_W] if pad else pltpu.roll(t, GRID_W, 1)
                    i0 = _attn_row(btype, qr)[2]
                    g = g + pltpu.roll(jnp.concatenate([t, jnp.zeros_like(t)], axis=1), i0 * GRID_W, 1)
                g_ref[hh] += g
                dsb = ds_ref[...]
                dq = dq + _nn(dsb, kh) * ATT_SCALE
                dkt_acc[:, pl.ds(k0, KB)] += _tn(qh, dsb)
                dvt_acc[:, pl.ds(k0, KB)] += _tn(doh, p_ref[...])
            dq_ref[...] = _nn(rev, dq.astype(BF16)).astype(BF16)

        for btype, (cond, k0) in enumerate(_attn_types(b, nb)):
            pl.when(cond)(lambda btype=btype, k0=k0: block(btype, k0))

        @pl.when(b == nb - 1)
        def _():
            dk_ref[...] = dkt_acc[...].T.astype(BF16)
            dv_ref[...] = dvt_acc[...].T.astype(BF16)

    col = pl.BlockSpec((s, 128), lambda j, b: (0, j))
    return pl.pallas_call(
        body, name=name, grid=(4, nb),
        in_specs=[pl.BlockSpec((QB, 128), lambda j, b: (b, j)),
                  pl.BlockSpec((s, 128), lambda j, b: (0, 4 + j)),
                  pl.BlockSpec((s, 128), lambda j, b: (0, 8 + j)),
                  pl.BlockSpec((QB, 128), lambda j, b: (b, j)),
                  pl.BlockSpec((None, 2, 2, GRID_W, KB), lambda j, b: (li, j, 0, 0, 0))] + [ANY] * len(after),
        out_specs=[pl.BlockSpec((QB, 128), lambda j, b: (b, j)), col, col,
                   pl.BlockSpec((2, 1, KB), lambda j, b: (j, 0, 0))],
        out_shape=[jax.ShapeDtypeStruct((s, D_ATTN), BF16)] * 3 + [jax.ShapeDtypeStruct((N_HEADS, 1, KB), F32)],
        scratch_shapes=[pltpu.VMEM((QB, KB), F32), pltpu.VMEM((QB, KB), F32), pltpu.VMEM((QB, KB), BF16),
                        pltpu.VMEM((QB, KB), BF16), pltpu.VMEM((128, s), F32), pltpu.VMEM((128, s), F32)],
        compiler_params=_cparams(2))(proj, proj, proj, da, e_rev, *after)


def dh0_bwd(dz1, pieces, win, li, name, after=()):
    s = dz1.shape[0]
    tm = 256
    bn = N_PROJ // N_DEV

    def body(dz_ref, q_ref, k_ref, v_ref, u_ref, ga_ref, gb_ref, w_ref, *rest):
        acc = ALPHA * dz_ref[...]
        for j, ref in enumerate((q_ref, k_ref, v_ref, u_ref)):
            acc = acc + _nt(ref[...], w_ref[j])
        for j, ref in ((4, ga_ref), (6, gb_ref)):
            acc = acc + _nt(ref[:, 0:bn], w_ref[j]) + _nt(ref[:, bn:2 * bn], w_ref[j + 1])
        rest[-1][...] = acc

    row = pl.BlockSpec((tm, D), lambda i: (i, 0))
    narrow = pl.BlockSpec((tm, bn), lambda i: (i, 0))
    return pl.pallas_call(
        body, name=name, grid=(s // tm,),
        in_specs=[row] + [narrow] * 4 + [row, row, _shards(N_DEV, D, bn, li)] + [ANY] * len(after),
        out_specs=row, out_shape=jax.ShapeDtypeStruct((s, D), F32),
        compiler_params=_cparams(1))(dz1, *pieces, win, *after)


def _coords():
    return lax.axis_index("x"), lax.axis_index("y"), lax.axis_index("c")


def _dev_index(px, py, pc):
    return 4 * px + 2 * py + pc


HBM = pl.BlockSpec(memory_space=pltpu.HBM)
SEM = pl.BlockSpec(memory_space=pltpu.SEMAPHORE)
_EFFECT = pltpu.SideEffectType.DATAFLOW_SIDE_EFFECTING
_TOKEN = jax.ShapeDtypeStruct((8, 128), F32)


def _in_hbm(a):
    return pltpu.with_memory_space_constraint(a, pltpu.HBM)


def _hbm_like(a):
    return pltpu.HBM(a.shape, a.dtype)


def _peers(x, y, c):
    return [(x, y, 1 - c), (1 - x, y, c), (x, 1 - y, c), (1 - x, 1 - y, c)]


def ag_start(lands, after, name):
    n = len(lands)

    def body(*refs):
        land = refs[:n]
        send_sem, recv_sem, token = refs[n + 1], refs[n + 2], refs[-1]
        x, y, c = _coords()
        me = _dev_index(x, y, c)
        for k, peer in enumerate(_peers(x, y, c)):
            for a in range(n):
                pltpu.make_async_remote_copy(src_ref=land[a].at[me], dst_ref=land[a].at[me], send_sem=send_sem.at[k],
                                             recv_sem=recv_sem.at[k], device_id=peer, device_id_type=MESH).start()
        token[...] = jnp.zeros_like(token)

    res = pl.pallas_call(
        body, name=name,
        out_shape=(pltpu.SemaphoreType.DMA((4,)), pltpu.SemaphoreType.DMA((4,)), *[_hbm_like(l) for l in lands], _TOKEN),
        in_specs=[HBM] * n + [ANY], out_specs=(SEM, SEM, *[HBM] * n, pl.BlockSpec(memory_space=pltpu.VMEM)),
        input_output_aliases={a: 2 + a for a in range(n)},
        compiler_params=pltpu.CompilerParams(has_side_effects=_EFFECT),
    )(*[_in_hbm(l) for l in lands], after)
    return res[0], res[1], list(res[2:2 + n]), res[-1]


def ag_forward(send_sem, recv_sem, lands, after, name):
    n = len(lands)

    def body(*refs):
        send_sem, recv_sem = refs[0], refs[1]
        land = refs[2:2 + n]
        fsend, frecv = refs[3 + n], refs[4 + n]
        x, y, c = _coords()
        peers = _peers(x, y, c)
        for k in range(1, 4):
            blk = _dev_index(*peers[k])
            for a in range(n):
                pltpu.make_async_remote_copy(src_ref=land[a].at[blk], dst_ref=land[a].at[blk], send_sem=send_sem.at[k],
                                             recv_sem=recv_sem.at[k], device_id=peers[k], device_id_type=MESH).wait_recv()
        for k in range(1, 4):
            blk = _dev_index(*peers[k])
            for a in range(n):
                pltpu.make_async_remote_copy(src_ref=land[a].at[blk], dst_ref=land[a].at[blk], send_sem=fsend.at[k - 1],
                                             recv_sem=frecv.at[k - 1], device_id=peers[0], device_id_type=MESH).start()

    res = pl.pallas_call(
        body, name=name,
        out_shape=(pltpu.SemaphoreType.DMA((3,)), pltpu.SemaphoreType.DMA((3,)), *[_hbm_like(l) for l in lands]),
        in_specs=[SEM, SEM, *[HBM] * n, ANY], out_specs=(SEM, SEM, *[HBM] * n),
        input_output_aliases={2 + a: 2 + a for a in range(n)},
        compiler_params=pltpu.CompilerParams(has_side_effects=_EFFECT),
    )(send_sem, recv_sem, *lands, after)
    return res[0], res[1], list(res[2:])


def ag_finish(send_sem, recv_sem, fsend, frecv, lands, after, name):
    n = len(lands)

    def body(*refs):
        send_sem, recv_sem, fsend, frecv = refs[:4]
        land = refs[4:4 + n]
        x, y, c = _coords()
        me = _dev_index(x, y, c)
        peers = _peers(x, y, c)
        for k in range(4):
            for a in range(n):
                pltpu.make_async_remote_copy(src_ref=land[a].at[me], dst_ref=land[a].at[me], send_sem=send_sem.at[k],
                                             recv_sem=recv_sem.at[k], device_id=peers[k], device_id_type=MESH).wait_send()
        sib = _dev_index(*peers[0])
        for a in range(n):
            pltpu.make_async_remote_copy(src_ref=land[a].at[sib], dst_ref=land[a].at[sib], send_sem=send_sem.at[0],
                                         recv_sem=recv_sem.at[0], device_id=peers[0], device_id_type=MESH).wait_recv()
        for k in range(1, 4):
            mine = _dev_index(*peers[k])
            theirs = _dev_index(peers[k][0], peers[k][1], 1 - c)
            for a in range(n):
                pltpu.make_async_remote_copy(src_ref=land[a].at[mine], dst_ref=land[a].at[theirs], send_sem=fsend.at[k - 1],
                                             recv_sem=frecv.at[k - 1], device_id=peers[0], device_id_type=MESH).wait()

    res = pl.pallas_call(
        body, name=name, out_shape=tuple(_hbm_like(l) for l in lands),
        in_specs=[SEM] * 4 + [HBM] * n + [ANY], out_specs=tuple([HBM] * n),
        input_output_aliases={4 + a: a for a in range(n)},
        compiler_params=pltpu.CompilerParams(has_side_effects=_EFFECT),
    )(send_sem, recv_sem, fsend, frecv, *lands, after)
    return list(res)


def rs_start(psums, name):
    n = len(psums)
    lands = [lax.empty(p.shape, p.dtype) for p in psums]

    def body(*refs):
        src, land = refs[:n], refs[n:2 * n]
        send_sem, recv_sem, token = refs[2 * n], refs[2 * n + 1], refs[-1]
        peers = _peers(*_coords())
        for k in range(3):
            for a in range(n):
                pltpu.make_async_remote_copy(src_ref=src[a].at[k], dst_ref=land[a].at[k], send_sem=send_sem.at[k],
                                             recv_sem=recv_sem.at[k], device_id=peers[k + 1], device_id_type=MESH).start()
        token[...] = jnp.zeros_like(token)

    res = pl.pallas_call(
        body, name=name,
        out_shape=(pltpu.SemaphoreType.DMA((3,)), pltpu.SemaphoreType.DMA((3,)), *[_hbm_like(p) for p in psums],
                   *[_hbm_like(l) for l in lands], _TOKEN),
        in_specs=[HBM] * (2 * n), out_specs=(SEM, SEM, *[HBM] * (2 * n), pl.BlockSpec(memory_space=pltpu.VMEM)),
        input_output_aliases={a: 2 + a for a in range(2 * n)},
        compiler_params=pltpu.CompilerParams(has_side_effects=_EFFECT),
    )(*[_in_hbm(p) for p in psums], *[_in_hbm(l) for l in lands])
    return res[0], res[1], list(res[2:2 + n]), list(res[2 + n:2 + 2 * n]), res[-1]


def rs_finish(send_sem, recv_sem, psums, lands, after, name):
    n = len(psums)

    def body(*refs):
        send_sem, recv_sem = refs[0], refs[1]
        src, land = refs[2:2 + n], refs[2 + n:2 + 2 * n]
        peers = _peers(*_coords())
        for k in range(3):
            for a in range(n):
                pltpu.make_async_remote_copy(src_ref=src[a].at[k], dst_ref=land[a].at[k], send_sem=send_sem.at[k],
                                             recv_sem=recv_sem.at[k], device_id=peers[k + 1], device_id_type=MESH).wait()

    res = pl.pallas_call(
        body, name=name, out_shape=tuple(_hbm_like(l) for l in lands),
        in_specs=[SEM, SEM] + [HBM] * (2 * n) + [ANY], out_specs=tuple([HBM] * n),
        input_output_aliases={2 + n + a: a for a in range(n)},
        compiler_params=pltpu.CompilerParams(has_side_effects=_EFFECT),
    )(send_sem, recv_sem, *psums, *lands, after)
    return list(res)


def d2d_start(grads, name):
    n = len(grads)
    lands = [lax.empty((4,) + g.shape[1:], g.dtype) for g in grads]

    def body(*refs):
        src, land = refs[:n], refs[n:2 * n]
        send_sem, recv_sem, token = refs[2 * n], refs[2 * n + 1], refs[-1]
        x, y, c = _coords()
        for a in range(n):
            for k in range(4):
                blk = _dev_index(x ^ (k & 1), y ^ (k >> 1), 1 - c)
                pltpu.make_async_remote_copy(src_ref=src[a].at[blk], dst_ref=land[a].at[k], send_sem=send_sem.at[0],
                                             recv_sem=recv_sem.at[0], device_id=(x, y, 1 - c), device_id_type=MESH).start()
        token[...] = jnp.zeros_like(token)

    res = pl.pallas_call(
        body, name=name,
        out_shape=(pltpu.SemaphoreType.DMA((1,)), pltpu.SemaphoreType.DMA((1,)), *[_hbm_like(g) for g in grads],
                   *[_hbm_like(l) for l in lands], _TOKEN),
        in_specs=[HBM] * (2 * n), out_specs=(SEM, SEM, *[HBM] * (2 * n), pl.BlockSpec(memory_space=pltpu.VMEM)),
        input_output_aliases={a: 2 + a for a in range(2 * n)},
        compiler_params=pltpu.CompilerParams(has_side_effects=_EFFECT),
    )(*[_in_hbm(g) for g in grads], *[_in_hbm(l) for l in lands])
    return res[0], res[1], list(res[2:2 + n]), list(res[2 + n:2 + 2 * n]), res[-1]


def d2d_finish(send_sem, recv_sem, grads, lands, after, name):
    n = len(grads)

    def body(*refs):
        send_sem, recv_sem = refs[0], refs[1]
        src, land = refs[2:2 + n], refs[2 + n:2 + 2 * n]
        x, y, c = _coords()
        for a in range(n):
            for k in range(4):
                blk = _dev_index(x ^ (k & 1), y ^ (k >> 1), 1 - c)
                pltpu.make_async_remote_copy(src_ref=src[a].at[blk], dst_ref=land[a].at[k], send_sem=send_sem.at[0],
                                             recv_sem=recv_sem.at[0], device_id=(x, y, 1 - c), device_id_type=MESH).wait()

    res = pl.pallas_call(
        body, name=name, out_shape=tuple(_hbm_like(t) for t in list(grads) + list(lands)),
        in_specs=[SEM, SEM] + [HBM] * (2 * n) + [ANY], out_specs=tuple([HBM] * (2 * n)),
        input_output_aliases={2 + a: a for a in range(2 * n)},
        compiler_params=pltpu.CompilerParams(has_side_effects=_EFFECT),
    )(send_sem, recv_sem, *grads, *lands, after)
    return list(res[:n]), list(res[n:])


def pair_add(blk_idx, g, recv, name):
    _, r, c = g.shape
    tr = _row_tile(r)

    def body(idx_ref, g0, g1, g2, g3, r_ref, own_ref, oth_ref):
        own_ref[...] = g0[...].astype(F32) + r_ref[0].astype(F32)
        for k, gk in enumerate((g1, g2, g3)):
            oth_ref[k] = (gk[...].astype(F32) + r_ref[k + 1].astype(F32)).astype(BF16)

    def blk(k):
        return pl.BlockSpec((None, tr, c), lambda t, idx: (idx[k], t, 0))

    grid_spec = pltpu.PrefetchScalarGridSpec(
        num_scalar_prefetch=1, grid=(r // tr,),
        in_specs=[blk(0), blk(1), blk(2), blk(3), pl.BlockSpec((4, tr, c), lambda t, idx: (0, t, 0))],
        out_specs=[pl.BlockSpec((tr, c), lambda t, idx: (t, 0)), pl.BlockSpec((3, tr, c), lambda t, idx: (0, t, 0))])
    return pl.pallas_call(
        body, name=name, grid_spec=grid_spec,
        out_shape=[jax.ShapeDtypeStruct((r, c), F32), jax.ShapeDtypeStruct((3, r, c), BF16)],
        compiler_params=_cparams(1))(blk_idx, g, g, g, g, recv)


def _row_tile(r):
    return next(t for t in (512, 352, 256, 128) if r % t == 0)


def _adamw(w, g, m, v):
    m = ADAM_B1 * m + (1.0 - ADAM_B1) * g
    v = ADAM_B2 * v + (1.0 - ADAM_B2) * (g * g)
    m_hat = m / (1.0 - ADAM_B1 ** ADAM_STEP)
    v_hat = v / (1.0 - ADAM_B2 ** ADAM_STEP)
    delta = -ADAM_LR * (m_hat / (jnp.sqrt(v_hat) + ADAM_EPS) + ADAM_WD * w)
    return delta, m, v


def adamw_shard(own, recv, w, m, v, li, prev, name):
    r, c = own.shape
    tr = _row_tile(r)

    def body(own_ref, recv_ref, w_ref, m_ref, v_ref, p0, p1, p2, p3, g_ref, d_ref, nm_ref, nv_ref):
        g = own_ref[...] + recv_ref[0].astype(F32) + recv_ref[1].astype(F32) + recv_ref[2].astype(F32)
        delta, nm, nv = _adamw(w_ref[...], g, m_ref[...], v_ref[...])
        g_ref[...] = g
        d_ref[...] = delta
        nm_ref[...] = nm
        nv_ref[...] = nv

    lay = pl.BlockSpec((None, tr, c), lambda t: (li, t, 0))
    stack = jax.ShapeDtypeStruct((DEPTH, r, c), F32)
    return pl.pallas_call(
        body, name=name, grid=(r // tr,),
        in_specs=[pl.BlockSpec((tr, c), lambda t: (t, 0)), pl.BlockSpec((3, tr, c), lambda t: (0, t, 0)),
                  lay, lay, lay, ANY, ANY, ANY, ANY],
        out_specs=[lay] * 4, out_shape=[stack] * 4,
        input_output_aliases={5: 0, 6: 1, 7: 2, 8: 3},
        compiler_params=_cparams(1))(own, recv, w, m, v, *prev)


def sum_partials(gathered, name):
    _, r, c = gathered.shape
    tr = next(t for t in (96, 88, 64, _PACK_TILE) if r % t == 0)

    def body(gs_ref, g_ref):
        g = gs_ref[0]
        for d in range(1, N_DEV):
            g = g + gs_ref[d]
        g_ref[...] = g

    return pl.pallas_call(
        body, name=name, grid=(r // tr,),
        in_specs=[pl.BlockSpec((N_DEV, tr, c), lambda t: (0, t, 0))],
        out_specs=pl.BlockSpec((tr, c), lambda t: (t, 0)), out_shape=jax.ShapeDtypeStruct((r, c), F32),
        compiler_params=_cparams(1))(gathered)


def adamw_plain(g, w, m, v, name):
    def body(g_ref, w_ref, m_ref, v_ref, d_ref, nm_ref, nv_ref):
        delta, nm, nv = _adamw(w_ref[...], g_ref[...], m_ref[...], v_ref[...])
        d_ref[...] = delta
        nm_ref[...] = nm
        nv_ref[...] = nv

    return pl.pallas_call(body, name=name, out_shape=[jax.ShapeDtypeStruct(w.shape, F32)] * 3)(g, w, m, v)


_PACK_LAYER = (("b_in", (N_PROJ,)), ("rpb", (N_HEADS, 2 * KH - 1, 2 * KW - 1)), ("pool_w", (4, PGD, PGD)),
               ("pool_scale", (D_POOL,)), ("ln1_g", (D,)), ("ln1_b", (D,)), ("conv_b", (D_FF,)), ("ln2_g", (D,)),
               ("ln2_b", (D,)), ("conv_w", (3, D_FF)))
_PACK_INPUT = (("ln_in_g", (D,)), ("ln_in_b", (D,)))
_PACK_LANES = 1024
_PACK_TILE = 8
_EARLY = tuple(range(1, DEPTH))


def _pack_items(layers):
    items = [(n, (len(layers),) + s) for n, s in _PACK_LAYER]
    return items + ([(n, s) for n, s in _PACK_INPUT] if 0 in layers else [])


def _pack(parts, layers):
    flats = [(parts[name] if (name, shape) in _PACK_INPUT else jnp.stack([parts[name][li] for li in layers]))
             .reshape(-1).astype(F32) for name, shape in _pack_items(layers)]
    used = sum(f.shape[0] for f in flats)
    tile = _PACK_TILE * _PACK_LANES
    total = -(-used // tile) * tile
    return jnp.concatenate(flats + [jnp.zeros((total - used,), F32)]).reshape(total // _PACK_LANES, _PACK_LANES)


def _unpack(packed, layers):
    flat, out, off = packed.reshape(-1), {}, 0
    for name, shape in _pack_items(layers):
        n = int(np.prod(shape))
        out[name] = flat[off:off + n].reshape(shape)
        off += n
    return out


def _bias_tables(rpb):
    qc = np.arange(GRID_W)[:, None]
    kc = np.arange(GRID_W)[None, :]
    start = np.clip(qc - KW // 2, 0, GRID_W - KW)
    valid = (kc >= start) & (kc < start + KW)
    col = np.clip(kc - qc, -(KW - 1), KW - 1) + KW - 1
    onehot = (col.reshape(-1)[None, :] == np.arange(2 * KW - 1)[:, None]).astype(np.float32)
    depth = rpb.shape[0]
    rows = jnp.pad(rpb, ((0, 0), (0, 0), (0, 1), (0, 0)))
    tab = jnp.einsum("lhij,jm->lhim", rows, jnp.asarray(onehot), precision=lax.Precision.HIGHEST)
    tab = tab.reshape(depth, N_HEADS, KROWS, GRID_W, GRID_W).transpose(0, 1, 3, 2, 4)
    ok = valid[:, None, :] & (np.arange(KROWS) < 2 * KH - 1)[None, :, None]
    tab = jnp.where(jnp.asarray(ok), tab, NEG_INF).reshape(depth, N_HEADS, GRID_W, KB)
    tab = jnp.stack([tab, jnp.roll(tab, GRID_W, axis=-1)], axis=2)
    return tab, tab[:, :, :, ::-1, :]


_SHARDED = ("w_in", "w_attn_out", "w_pool_out", "w_mix_out", "w_up", "w_down", "w_ple_gate", "w_ple_proj")
_NAMES = ("ln_in_g", "ln_in_b", "w_in", "b_in", "rpb", "w_attn_out", "pool_w", "pool_scale", "w_pool_out", "w_mix_out",
          "ln1_g", "ln1_b", "w_up", "conv_w", "conv_b", "w_down", "w_ple_gate", "w_ple_proj", "ln2_g", "ln2_b")


def kernel(x, p, ln_in_g, ln_in_b, w_in, b_in, rpb, w_attn_out, pool_w, pool_scale, w_pool_out, w_mix_out, ln1_g, ln1_b, w_up, conv_w, conv_b, w_down, w_ple_gate, w_ple_proj, ln2_g, ln2_b, loss_target, m_ln_in_g, m_ln_in_b, m_w_in, m_b_in, m_rpb, m_w_attn_out, m_pool_w, m_pool_scale, m_w_pool_out, m_w_mix_out, m_ln1_g, m_ln1_b, m_w_up, m_conv_w, m_conv_b, m_w_down, m_w_ple_gate, m_w_ple_proj, m_ln2_g, m_ln2_b, v_ln_in_g, v_ln_in_b, v_w_in, v_b_in, v_rpb, v_w_attn_out, v_pool_w, v_pool_scale, v_w_pool_out, v_w_mix_out, v_ln1_g, v_ln1_b, v_w_up, v_conv_w, v_conv_b, v_w_down, v_w_ple_gate, v_w_ple_proj, v_ln2_g, v_ln2_b):
    a = dict(locals())
    W = {n: a[n] for n in _NAMES}
    M = {n: a["m_" + n] for n in _NAMES}
    V = {n: a["v_" + n] for n in _NAMES}
    xi, yi, ci = _coords()
    me = _dev_index(xi, yi, ci)
    x2, tgt = x[0], loss_target[0]
    pb = p[:, 0].astype(BF16)

    flip = lambda d: {**d, "w_up": d["w_up"].transpose(0, 2, 1)}
    ex = _Exchange(flip(W), flip(M), flip(V))
    loss_part, dx, parts = _local_step(x2, tgt, pb, W, ex)
    loss = lax.psum(loss_part[0, 0], AXES)

    started = ex.replicated_start("late", _pack(parts, (0,)), dx)
    done = ex.update(range(DEPTH - 1, 0, -1), started)
    ex.replicated_forward("late", done)
    done = ex.update((0,), done)
    stacks = {**ex.stacks, "w_up": [t.transpose(0, 2, 1) for t in ex.stacks["w_up"]]}
    lo, hi = [_unpack(sum_partials(ex.replicated_finish(tag, done), f"sum_replicated_{tag}"), layers)
              for layers, tag in (((0,), "late"), (_EARLY, "early"))]
    grads = {**{n: jnp.concatenate([lo[n], hi[n]]) for n, _ in _PACK_LAYER}, **{n: lo[n] for n, _ in _PACK_INPUT}}
    grads["conv_w"] = lax.dynamic_slice_in_dim(grads["conv_w"], me * FF_SHARD, FF_SHARD, axis=2)
    res = [{n: stacks[n][k] for n in _SHARDED} for k in range(4)]
    for n, g in grads.items():
        two_d = lambda t: t.reshape(-1, t.shape[-1])
        outs = adamw_plain(two_d(g), two_d(W[n]), two_d(M[n]), two_d(V[n]), f"adamw_{n}")
        for d, o in zip(res, [g] + [o.reshape(W[n].shape) for o in outs]):
            d[n] = o
    return (loss, dx[None], *[res[k][n] for k in range(4) for n in _NAMES])


class _Exchange:
    GROUPS = (("w_ple_gate", "w_ple_proj", "w_down", "w_up"), ("w_mix_out", "w_attn_out", "w_pool_out"), ("w_in",))
    FIRST = ("w_in",)

    def __init__(self, W, M, V):
        self.W, self.M, self.V = W, M, V
        xi, yi, ci = _coords()
        me = _dev_index(xi, yi, ci)
        self.me = me.astype(I32).reshape(1)
        self.rel_idx = jnp.stack([_dev_index(xi ^ (k & 1), yi ^ (k >> 1), ci) for k in range(4)]).astype(I32)
        self.lands = [{n: lax.dynamic_update_index_in_dim(lax.empty((N_DEV,) + W[n].shape[1:], BF16),
                                                          W[n][li].astype(BF16), me, 0) for n in _SHARDED}
                      for li in range(DEPTH)]
        cw_land = lax.dynamic_update_index_in_dim(lax.empty((N_DEV,) + W["conv_w"].shape, F32), W["conv_w"], me, 0)
        self.ag, self.fwd, self.rs, self.pending, self.small = {}, {}, {}, {}, {}
        self.stacks = {n: [lax.empty((DEPTH,) + W[n].shape[1:], F32) for _ in range(4)] for n in _SHARDED}
        self.late = tuple(n for n in _SHARDED if n not in self.FIRST)
        self.ag[0] = ag_start([self.lands[0][n] for n in self.FIRST] + [cw_land], W["conv_w"], "ag_start0")

    def tokens(self):
        return [self.ag[0][3]]

    def prefetch(self, li, after):
        send, recv, lands, _ = self.ag[li]
        self.fwd[li] = ag_forward(send, recv, lands, after, f"ag_forward{li}")
        if li == 0:
            self.ag["0b"] = ag_start([self.lands[0][n] for n in self.late], self.fwd[0][2][0], "ag_start0b")

    def weights(self, li, after):
        send, recv, _, _ = self.ag.pop(li)
        fsend, frecv, lands = self.fwd.pop(li)
        lands = ag_finish(send, recv, fsend, frecv, lands, after, f"ag_finish{li}")
        if li == 0:
            self.cw = lands[-1].transpose(1, 2, 0, 3).reshape(DEPTH, 3, 4, FF_BLK).transpose(0, 2, 1, 3)
            return dict(zip(self.FIRST, lands)), self.cw[li], (self.ag["0b"][3],)
        tokens = ()
        if li + 1 < DEPTH:
            self.ag[li + 1] = ag_start([self.lands[li + 1][n] for n in _SHARDED], lands[0], f"ag_start{li + 1}")
            tokens = (self.ag[li + 1][3],)
        return dict(zip(_SHARDED, lands)), self.cw[li], tokens

    def rest(self, li, G, mid, after):
        if li != 0:
            return G, ()
        send, recv, lands, _ = self.ag.pop("0b")
        fsend, frecv, lands = ag_forward(send, recv, lands, mid, "ag_forward0b")
        lands = ag_finish(send, recv, fsend, frecv, lands, after, "ag_finish0b")
        self.ag[1] = ag_start([self.lands[1][n] for n in _SHARDED], lands[0], "ag_start1")
        return {**G, **dict(zip(self.late, lands))}, (self.ag[1][3],)

    def grads(self, li, group, gw):
        self.pending.setdefault(li, {}).update(gw)
        if li != 0 and group != len(self.GROUPS) - 1:
            return None
        gw = self.pending.pop(li)
        tag = f"{li}_{group}" if li == 0 else f"{li}"
        send, recv, glist, lands, token = d2d_start(list(gw.values()), f"d2d_start{tag}")
        self.d2d = (tag, tuple(gw), send, recv, glist, lands)
        return token

    def flush(self, li, group, after):
        if li != 0 and group != len(self.GROUPS) - 1:
            return None
        tag, names, send, recv, glist, lands = self.d2d
        glist, recv1 = d2d_finish(send, recv, glist, lands, after, f"d2d_finish{tag}")
        sums = [pair_add(self.rel_idx, g, r1, f"pair_add_{n}{li}") for n, g, r1 in zip(names, glist, recv1)]
        send, recv, psums, lands, token = rs_start([s_[1] for s_ in sums], f"rs_start{tag}")
        self.rs.setdefault(li, []).append((tag, names, send, recv, psums, lands, [s_[0] for s_ in sums]))
        if li == 0 and group == 1 and "early" in self.small:
            self.replicated_forward("early", token)
        return token

    def update(self, layers, after):
        for li in layers:
            for tag, names, send, recv, psums, lands, owns in self.rs.pop(li):
                recv2 = rs_finish(send, recv, psums, lands, after, f"rs_finish{tag}")
                for n, own, r2 in zip(names, owns, recv2):
                    self.stacks[n] = adamw_shard(own, r2, self.W[n], self.M[n], self.V[n], li, self.stacks[n],
                                                 f"adamw_{n}{li}")
                    after = self.stacks[n][0]
        return after

    def replicated_start(self, tag, pack, after):
        land = lax.dynamic_update_index_in_dim(lax.empty((N_DEV,) + pack.shape, F32), pack, self.me[0], 0)
        self.small[tag] = ag_start([land], after, f"ag_start_small_{tag}")
        return self.small[tag][3]

    def replicated_early(self, small, after):
        return self.replicated_start("early", _pack(small, _EARLY), after)

    def replicated_forward(self, tag, after):
        send, recv, lands, _ = self.small[tag]
        self.small[tag] = (send, recv) + ag_forward(send, recv, lands, after, f"ag_forward_small_{tag}")

    def replicated_finish(self, tag, after):
        send, recv, fsend, frecv, lands = self.small.pop(tag)
        return ag_finish(send, recv, fsend, frecv, lands, after, f"ag_finish_small_{tag}")[0]


def _local_step(x2, tgt, pb, W, ex):
    depth = W["rpb"].shape[0]
    vec = lambda t: t.reshape(1, -1)
    ln1_g, ln1_b, ln2_g, ln2_b = W["ln1_g"], W["ln1_b"], W["ln2_g"], W["ln2_b"]
    b_in, rpb, pool_scale = W["b_in"], W["rpb"], W["pool_scale"]
    cb_full = W["conv_b"].reshape(depth, 4, 1, FF_BLK)
    pool_w_b = W["pool_w"].astype(BF16)
    e_tab, e_rev = _bias_tables(rpb)

    h, hb = ln_fwd(x2, vec(W["ln_in_g"]), vec(W["ln_in_b"]), "ln_in", after=ex.tokens())
    ex.prefetch(0, hb)
    saved = []
    for li in range(depth):
        G, cw, tokens = ex.weights(li, hb)
        bias = vec(b_in[li])
        proj, u = proj_fwd(hb, G["w_in"], bias, li, f"proj{li}", after=tokens)
        att = attn_fwd(proj, e_tab, li, f"attn{li}")
        pm, pw = pool_fwd(u, pool_w_b[li], vec(pool_scale[li]), f"pool{li}")
        G, tokens = ex.rest(li, G, att, pw)
        mg, ya, yp = merge_fwd(att, pw, G["w_attn_out"], G["w_pool_out"], proj, li, f"merge{li}", after=tokens)
        if li + 1 < depth:
            ex.prefetch(li + 1, mg)
        z1, h1, h1b = mix_ln_fwd(mg, G["w_mix_out"], h, vec(ln1_g[li]), vec(ln1_b[li]), li, f"mix_ln{li}")
        up = up_fwd(h1b, G["w_up"], li, f"up{li}")
        t = ffn_act_fwd(up, cw, cb_full[li], f"ffn_act{li}")
        z2, h2, h2b, pg, pp = down_ple_ln_fwd(t, G["w_down"], h1b, G["w_ple_gate"], pb[li], G["w_ple_proj"], h1,
                                              vec(ln2_g[li]), vec(ln2_b[li]), li, f"down_ln{li}")
        saved.append(dict(hb=hb, proj=proj, att=att, pm=pm, pw=pw, mg=mg, ya=ya, yp=yp, z1=z1, h1b=h1b, up=up, t=t,
                          z2=z2, pg=pg, pp=pp, G=G, cw=cw))
        h, hb = h2, h2b

    dh, loss_part = loss_bwd(h, tgt, "loss")
    small = {n: [None] * depth for n in ("b_in", "rpb", "pool_w", "pool_scale", "ln1_g", "ln1_b", "conv_b", "ln2_g",
                                         "ln2_b", "conv_w")}
    token = ()
    tok = lambda t: () if t is None else (t,)
    for li in reversed(range(depth)):
        sv = saved[li]
        G, cw = sv["G"], sv["cw"]
        dz2, dz2b, dpg, dpp, dg2, db2 = ln2_ple_bwd(dh, sv["z2"], vec(ln2_g[li]), sv["pg"], sv["pp"], f"ln2_bwd{li}",
                                                    after=token)
        gw = {}
        gw["w_ple_gate"] = wgrad_rows(sv["h1b"], dpg, f"dw_pg{li}")
        gw["w_ple_proj"] = wgrad_cols(pb[li], dpp, f"dw_pp{li}")
        gw["w_down"] = wgrad_down(sv["t"], dz2b, f"dw_down{li}").reshape(N_DEV, FF_SHARD, D)
        dhv, dhg, dcw, dcb = ffn_act_bwd(dz2b, G["w_down"], sv["up"], cw, cb_full[li], li, f"ffn_bwd{li}")
        gw["w_up"] = wgrad_up(sv["h1b"], dhv, dhg, f"dw_up{li}")
        token = tok(ex.grads(li, 0, gw))
        dz1, dz1b, dg1, db1 = dh1_ln1_bwd(dz2, dpg, G["w_ple_gate"], dhv, dhg, G["w_up"], sv["z1"], vec(ln1_g[li]), li,
                                          f"ln1_bwd{li}", after=token)
        token = tok(ex.flush(li, 0, dz1b))
        gw = {"w_mix_out": wgrad_rows(sv["mg"], dz1b, f"dw_mix{li}", after=token)}
        dya, dyp, dga, dgb = merge_bwd(dz1b, G["w_mix_out"], sv["proj"], sv["ya"], sv["yp"], li, f"merge_bwd{li}")
        gw["w_attn_out"] = wgrad_cols(sv["att"], dya, f"dw_ao{li}")
        gw["w_pool_out"] = wgrad_cols(sv["pw"], dyp, f"dw_po{li}")
        token = tok(ex.grads(li, 1, gw))
        da = attn_out_bwd(dya, G["w_attn_out"], li, f"da{li}", after=token)
        du, dpool_w, dpool_sc = pool_bwd(dyp, G["w_pool_out"], sv["pm"], pool_w_b[li], vec(pool_scale[li]), li,
                                         f"pool_bwd{li}")
        token = tok(ex.flush(li, 1, du))
        dq, dk, dv, drpb = attn_bwd(sv["proj"], da, e_rev, li, f"attn_bwd{li}", after=token)
        dproj = [dq, dk, dv, du, dga, dgb]
        dw_in, db_in = wgrad_in(sv["hb"], dproj, f"dw_in{li}")
        token = tok(ex.grads(li, 2, {"w_in": dw_in}))
        dh = dh0_bwd(dz1, dproj, G["w_in"], li, f"dh0{li}", after=token)
        small["b_in"][li] = db_in.reshape(N_PROJ)
        small["rpb"][li] = drpb.reshape(N_HEADS, KROWS, GRID_W)[:, :2 * KH - 1, :2 * KW - 1]
        small["pool_w"][li] = dpool_w
        small["pool_scale"][li] = dpool_sc.reshape(D_POOL)
        small["ln1_g"][li], small["ln1_b"][li] = dg1.reshape(D), db1.reshape(D)
        small["ln2_g"][li], small["ln2_b"][li] = dg2.reshape(D), db2.reshape(D)
        small["conv_b"][li] = dcb.reshape(D_FF)
        small["conv_w"][li] = dcw.transpose(1, 0, 2).reshape(3, D_FF)
        token = tok(ex.flush(li, 2, dh))
        if li == 1:
            token = token + tok(ex.replicated_early(small, dh))
    dx, dg_in, db_in0 = ln_bwd(dh, x2, vec(W["ln_in_g"]), "ln_in_bwd", after=token)
    parts = {n: jnp.stack(v_) for n, v_ in small.items()}
    parts["ln_in_g"], parts["ln_in_b"] = dg_in.reshape(D), db_in0.reshape(D)
    return loss_part, dx, parts
```

```python
import numpy as np
import jax
import jax.numpy as jnp
from jax import lax
from jax.experimental import pallas as pl
from jax.experimental.pallas import tpu as pltpu

F32 = jnp.float32
BF16 = jnp.bfloat16
I32 = jnp.int32

D = 1024
DEPTH = 4
GRID_W = 64
N_HEADS = 8
HEAD_DIM = 64
D_ATTN = 512
KH = 8
KW = 16
POOL_WINDOWS = (2, 4, 8, 16)
D_POOL = 512
PGD = 128
D_FF = 2816
PLE_DIM = 256
N_PROJ = 4096
ALPHA = (2 * DEPTH) ** 0.25
LN_EPS = 1e-5
NEG_INF = -1e30
ATT_SCALE = HEAD_DIM ** -0.5
ADAM_LR = 0.001
ADAM_B1 = 0.9
ADAM_B2 = 0.999
ADAM_EPS = 1e-08
ADAM_WD = 0.01
ADAM_STEP = 10

N_DEV = 8
AXES = ("x", "y", "c")
FF_BLK = D_FF // 4
FF_SHARD = D_FF // N_DEV
QROWS = 8
KROWS = 16
QB = QROWS * GRID_W
KB = KROWS * GRID_W
V7X_VMEM_LIMIT = 56 * 2 ** 20
MESH = pl.DeviceIdType.MESH
ANY = pl.BlockSpec(memory_space=pl.ANY)


def _cparams(n_grid):
    return pltpu.CompilerParams(dimension_semantics=("arbitrary",) * n_grid, vmem_limit_bytes=V7X_VMEM_LIMIT)


def _nn(a, b):
    return lax.dot_general(a, b, (((1,), (0,)), ((), ())), preferred_element_type=F32)


def _nt(a, b):
    return lax.dot_general(a, b, (((1,), (1,)), ((), ())), preferred_element_type=F32)


def _tn(a, b):
    return lax.dot_general(a, b, (((0,), (0,)), ((), ())), preferred_element_type=F32)


def _sigmoid(x):
    return 1.0 / (1.0 + jnp.exp(-x))


def _ln_fwd(z, g, b):
    mu = jnp.mean(z, axis=-1, keepdims=True)
    xc = z - mu
    var = jnp.mean(xc * xc, axis=-1, keepdims=True)
    return xc * lax.rsqrt(var + LN_EPS) * g + b


def _ln_bwd(dh, z, g):
    mu = jnp.mean(z, axis=-1, keepdims=True)
    xc = z - mu
    var = jnp.mean(xc * xc, axis=-1, keepdims=True)
    rstd = lax.rsqrt(var + LN_EPS)
    xhat = xc * rstd
    dxh = dh * g
    m1 = jnp.mean(dxh, axis=-1, keepdims=True)
    m2 = jnp.mean(dxh * xhat, axis=-1, keepdims=True)
    return rstd * (dxh - m1 - xhat * m2), dh * xhat


def _colsum(x):
    return jnp.sum(x, axis=0, keepdims=True)


def _lane_cat(ref):
    return jnp.concatenate([ref[j] for j in range(ref.shape[0])], axis=1)


def _row_cat(ref):
    n, r, c = ref.shape
    return ref[...].reshape(n * r, c)


def _shards(n, r, c, li, j_of=None):
    del li
    if j_of is None:
        return pl.BlockSpec((n, r, c), lambda *_: (0, 0, 0))
    return pl.BlockSpec((n, r, c), lambda *g: (j_of(*g), 0, 0))


def _shard(r, c, li, j_of):
    del li
    return pl.BlockSpec((None, r, c), lambda *g: (j_of(*g), 0, 0))


def ln_fwd(x, g, b, name, after=()):
    s = x.shape[0]
    tm = 512
    na = len(after)

    def body(x_ref, g_ref, b_ref, *rest):
        h_ref, hb_ref = rest[na:]
        h = _ln_fwd(x_ref[...], g_ref[...], b_ref[...])
        h_ref[...] = h
        hb_ref[...] = h.astype(BF16)

    row = pl.BlockSpec((tm, D), lambda i: (i, 0))
    vec = pl.BlockSpec((1, D), lambda i: (0, 0))
    return pl.pallas_call(
        body, name=name, grid=(s // tm,), in_specs=[row, vec, vec] + [ANY] * na, out_specs=[row, row],
        out_shape=[jax.ShapeDtypeStruct((s, D), F32), jax.ShapeDtypeStruct((s, D), BF16)],
        compiler_params=_cparams(1))(x, g, b, *after)


def proj_fwd(hb, win, bias, li, name, after=()):
    s = hb.shape[0]
    bn = N_PROJ // N_DEV
    tm = 1024
    pool_shard = (3 * D_ATTN) // bn

    def body(a_ref, w_ref, b_ref, *rest):
        o_ref, u_ref = rest[-2:]
        acc = _nn(a_ref[...], w_ref[...]) + b_ref[...]
        o_ref[...] = acc.astype(BF16)

        @pl.when(pl.program_id(1) == pool_shard)
        def _():
            u_ref[...] = acc

    return pl.pallas_call(
        body, name=name, grid=(s // tm, N_DEV),
        in_specs=[pl.BlockSpec((tm, D), lambda i, j: (i, 0)),
                  _shard(D, bn, li, lambda i, j: j),
                  pl.BlockSpec((1, bn), lambda i, j: (0, j))] + [ANY] * len(after),
        out_specs=[pl.BlockSpec((tm, bn), lambda i, j: (i, j)), pl.BlockSpec((tm, bn), lambda i, j: (i, 0))],
        out_shape=[jax.ShapeDtypeStruct((s, N_PROJ), BF16), jax.ShapeDtypeStruct((s, D_POOL), F32)],
        compiler_params=_cparams(2))(hb, win, bias, *after)


def _attn_types(b, nb):
    first, last = 0, (nb * QROWS - KROWS) * GRID_W
    mid = pl.multiple_of((QROWS * b - KH // 2) * GRID_W, 256)
    return ((b == 0, first), ((b > 0) & (b < nb - 1), mid), (b == nb - 1, last))


def _attn_row(btype, qr):
    lo, delta = ((max(qr - KH // 2, 0), 0), (qr, -(KH // 2)), (min(qr + KH // 2, KH), -KH))[btype]
    return lo, (qr - delta - (KH - 1)) % KROWS, lo - qr + delta + KH - 1


def _row_window(lo):
    pad = (lo % 2) * GRID_W
    return (lo // 2) * 128, KH * GRID_W + 2 * pad, pad


def _lanes(ref, start, width):
    start %= KB
    if start + width <= KB:
        return ref[:, start:start + width]
    return jnp.concatenate([ref[:, start:], ref[:, :start + width - KB]], axis=1)


HALF = QROWS // 2


def _half_window(btype, half):
    spans = [_row_window(_attn_row(btype, qr)[0]) for qr in range(half * HALF, (half + 1) * HALF)]
    h0 = min(a0 for a0, _, _ in spans) // 256 * 256
    h1 = -(-max(a0 + w for a0, w, _ in spans) // 256) * 256
    return h0, h1 - h0


def _token_at(k0, h0):
    return k0 + h0 if isinstance(k0, int) else pl.multiple_of(k0 + h0, 256)


def _row_logits(s_ref, e_ref, hh, rows, btype, qr, h0):
    lo, shift, _ = _attn_row(btype, qr)
    a0, w, pad = _row_window(lo)
    e = e_ref.at[hh, shift % 2]
    sb = s_ref[rows, a0 - h0:a0 - h0 + w] + _lanes(e, a0 - (shift - shift % 2) * GRID_W, w)
    if pad:
        lane = lax.broadcasted_iota(I32, (1, w), 1)
        sb = jnp.where((lane >= pad) & (lane < w - pad), sb, NEG_INF)
    return sb, a0 - h0, w, pad


def _store_row(ref, rows, a0, w, val, width):
    if a0:
        ref[rows, 0:a0] = jnp.zeros((GRID_W, a0), ref.dtype)
    ref[rows, a0:a0 + w] = val.astype(ref.dtype)
    if a0 + w < width:
        ref[rows, a0 + w:width] = jnp.zeros((GRID_W, width - a0 - w), ref.dtype)


def attn_fwd(proj, e_tab, li, name):
    s = proj.shape[0]
    nb = s // QB

    def body(q_ref, k_ref, v_ref, e_ref, o_ref, s_ref, p_ref):
        q = q_ref[...] * ATT_SCALE
        lane = lax.broadcasted_iota(I32, (1, 128), 1)

        def block(btype, k0):
            for half in range(2):
                h0, hw = _half_window(btype, half)
                hrows = slice(half * HALF * GRID_W, (half + 1) * HALF * GRID_W)
                kwin = k_ref[pl.ds(_token_at(k0, h0), hw), :]
                vwin = v_ref[pl.ds(_token_at(k0, h0), hw), :]
                acc = jnp.zeros((HALF * GRID_W, 128), F32)
                for hh in range(2):
                    lm = (lane // HEAD_DIM) == hh
                    qh = jnp.where(lm, q[hrows], jnp.zeros_like(q[hrows]))
                    vh = jnp.where(lm, vwin, jnp.zeros_like(vwin))
                    s_ref[:, 0:hw] = _nt(qh, kwin)
                    for r in range(HALF):
                        rows = slice(r * GRID_W, (r + 1) * GRID_W)
                        sb, a0, w, _ = _row_logits(s_ref, e_ref, hh, rows, btype, half * HALF + r, h0)
                        p = jnp.exp(sb - jnp.max(sb, axis=1, keepdims=True))
                        _store_row(p_ref, rows, a0, w, p * (1.0 / jnp.sum(p, axis=1, keepdims=True)), hw)
                    acc = acc + _nn(p_ref[:, 0:hw], vh)
                o_ref[hrows, :] = acc.astype(BF16)

        for btype, (cond, k0) in enumerate(_attn_types(pl.program_id(1), nb)):
            pl.when(cond)(lambda btype=btype, k0=k0: block(btype, k0))

    half_shape = (HALF * GRID_W, (HALF + KH - 1 + 1) * GRID_W)
    return pl.pallas_call(
        body, name=name, grid=(4, nb),
        in_specs=[pl.BlockSpec((QB, 128), lambda j, b: (b, j)),
                  pl.BlockSpec((s, 128), lambda j, b: (0, 4 + j)),
                  pl.BlockSpec((s, 128), lambda j, b: (0, 8 + j)),
                  pl.BlockSpec((None, 2, 2, GRID_W, KB), lambda j, b: (li, j, 0, 0, 0))],
        out_specs=pl.BlockSpec((QB, 128), lambda j, b: (b, j)),
        out_shape=jax.ShapeDtypeStruct((s, D_ATTN), BF16),
        scratch_shapes=[pltpu.VMEM(half_shape, F32), pltpu.VMEM(half_shape, BF16)],
        compiler_params=_cparams(2))(proj, proj, proj, e_tab)


_POOL_PAD = 8


def _pool_counts(s, w):
    t = lax.broadcasted_iota(I32, (s, 1), 0)
    return (jnp.minimum(t + w // 2, s) - jnp.maximum(t - w // 2, 0)).astype(F32)


def _window_sum(x, w, back_first):
    s = x.shape[0]
    z = jnp.zeros((_POOL_PAD, x.shape[1]), F32)
    xe = jnp.concatenate([z, x, z], axis=0)
    n = s + 2 * _POOL_PAD
    acc = xe + pltpu.roll(xe, 1 if back_first else n - 1, 0)
    k = 1
    while 2 * k < w:
        acc = pltpu.roll(acc, k, 0) + pltpu.roll(acc, n - k, 0)
        k *= 2
    return acc[_POOL_PAD:_POOL_PAD + s, :]


def pool_fwd(u, pool_w, pool_scale, name):
    s = u.shape[0]

    def body(u_ref, w_ref, sc_ref, pm_ref, pw_ref):
        for g, w in enumerate(POOL_WINDOWS):
            cols = slice(g * PGD, (g + 1) * PGD)
            ug = u_ref[:, cols]
            pm = (_window_sum(ug, w, True) / _pool_counts(s, w) - ug).astype(BF16)
            pm_ref[:, cols] = pm
            pw_ref[:, cols] = (_nn(pm, w_ref[g]) * sc_ref[:, cols]).astype(BF16)

    full = lambda shape: pl.BlockSpec(shape, lambda i: (0,) * len(shape))
    return pl.pallas_call(
        body, name=name, grid=(1,),
        in_specs=[full((s, D_POOL)), full((4, PGD, PGD)), full((1, D_POOL))],
        out_specs=[full((s, D_POOL)), full((s, D_POOL))],
        out_shape=[jax.ShapeDtypeStruct((s, D_POOL), BF16)] * 2,
        compiler_params=_cparams(1))(u, pool_w, pool_scale)


def merge_fwd(a, pw, wao, wpo, proj, li, name, after=()):
    s = a.shape[0]
    tm, tn = 512, 512
    nt = D // tn
    per = tn // 128

    def body(a_ref, pw_ref, wa_ref, wp_ref, ga_ref, gb_ref, *rest):
        mg_ref, ya_ref, yp_ref = rest[len(after):]
        ya = _nn(a_ref[...], _lane_cat(wa_ref))
        yp = _nn(pw_ref[...], _lane_cat(wp_ref))
        mg = _sigmoid(ga_ref[...].astype(F32)) * ya + _sigmoid(gb_ref[...].astype(F32)) * yp
        mg_ref[...] = mg.astype(BF16)
        ya_ref[...] = ya.astype(BF16)
        yp_ref[...] = yp.astype(BF16)

    act = pl.BlockSpec((tm, D_ATTN), lambda i, j: (i, 0))
    wsp = _shards(per, D_ATTN, 128, li, lambda i, j: j)
    out = pl.BlockSpec((tm, tn), lambda i, j: (i, j))
    ga0 = (3 * D_ATTN + D_POOL) // tn
    return pl.pallas_call(
        body, name=name, grid=(s // tm, nt),
        in_specs=[act, act, wsp, wsp,
                  pl.BlockSpec((tm, tn), lambda i, j: (i, ga0 + j)),
                  pl.BlockSpec((tm, tn), lambda i, j: (i, ga0 + nt + j))] + [ANY] * len(after),
        out_specs=[out, out, out],
        out_shape=[jax.ShapeDtypeStruct((s, D), BF16)] * 3,
        compiler_params=_cparams(2))(a, pw, wao, wpo, proj, proj, *after)


def mix_ln_fwd(mg, wmix, h0, g, b, li, name):
    s = mg.shape[0]
    tm = 256

    def body(mg_ref, w_ref, h0_ref, g_ref, b_ref, z_ref, h_ref, hb_ref):
        z = ALPHA * h0_ref[...] + _nn(mg_ref[...], _row_cat(w_ref))
        h = _ln_fwd(z, g_ref[...], b_ref[...])
        z_ref[...] = z
        h_ref[...] = h
        hb_ref[...] = h.astype(BF16)

    row = pl.BlockSpec((tm, D), lambda i: (i, 0))
    vec = pl.BlockSpec((1, D), lambda i: (0, 0))
    return pl.pallas_call(
        body, name=name, grid=(s // tm,),
        in_specs=[row, _shards(N_DEV, D // N_DEV, D, li), row, vec, vec],
        out_specs=[row, row, row],
        out_shape=[jax.ShapeDtypeStruct((s, D), F32), jax.ShapeDtypeStruct((s, D), F32),
                   jax.ShapeDtypeStruct((s, D), BF16)],
        compiler_params=_cparams(1))(mg, wmix, h0, g, b)


def up_fwd(hb, wup, li, name):
    s = hb.shape[0]
    tm = 1024

    def body(a_ref, w_ref, o_ref):
        o_ref[...] = _nt(a_ref[...], w_ref[...]).astype(BF16)

    return pl.pallas_call(
        body, name=name, grid=(s // tm, N_DEV),
        in_specs=[pl.BlockSpec((tm, D), lambda i, j: (i, 0)), _shard(FF_BLK, D, li, lambda i, j: j)],
        out_specs=pl.BlockSpec((None, tm, FF_BLK), lambda i, j: (j, i, 0)),
        out_shape=jax.ShapeDtypeStruct((N_DEV, s, FF_BLK), BF16),
        compiler_params=_cparams(2))(hb, wup)


_SQRT_HALF = 0.7071067811865476
_INV_SQRT_2PI = 0.3989422804014327


def _shift_rows(x, prev_row, next_row):
    n = x.shape[0]
    r = lax.broadcasted_iota(I32, (n, 1), 0)
    back = jnp.where(r == 0, prev_row, pltpu.roll(x, 1, 0))
    fwd = jnp.where(r == n - 1, next_row, pltpu.roll(x, n - 1, 0))
    return back, fwd


HALO = 16


def _halo_maps(tm, s):
    th = tm // HALO
    return (lambda i: jnp.maximum(i * th - 1, 0)), (lambda i: jnp.minimum((i + 1) * th, s // HALO - 1))


def _slab_specs(tm, s, blk_of):
    before, after = _halo_maps(tm, s)
    main = pl.BlockSpec((None, tm, FF_BLK), lambda c, i: (blk_of(c), i, 0))
    prev = pl.BlockSpec((None, HALO, FF_BLK), lambda c, i: (blk_of(c), before(i), 0))
    nxt = pl.BlockSpec((None, HALO, FF_BLK), lambda c, i: (blk_of(c), after(i), 0))
    return main, prev, nxt


def ffn_act_fwd(up, conv_w, conv_b, name):
    s = up.shape[1]
    tm = 512
    nt = s // tm
    hv_main, _, _ = _slab_specs(tm, s, lambda c: c)
    hg_main, hg_prev, hg_next = _slab_specs(tm, s, lambda c: 4 + c)

    def body(hv_ref, hg_ref, hp_ref, hn_ref, cw_ref, cb_ref, t_ref):
        i = pl.program_id(1)
        hg = hg_ref[...].astype(F32)
        prow = jnp.where(i == 0, 0.0, hp_ref[...].astype(F32)[HALO - 1:HALO, :])
        nrow = jnp.where(i == nt - 1, 0.0, hn_ref[...].astype(F32)[0:1, :])
        back, fwd = _shift_rows(hg, prow, nrow)
        c = back * cw_ref[0:1, :] + hg * cw_ref[1:2, :] + fwd * cw_ref[2:3, :] + cb_ref[...]
        act = 0.5 * c * (1.0 + lax.erf(c * _SQRT_HALF))
        t_ref[...] = (act * hv_ref[...].astype(F32)).astype(BF16)

    return pl.pallas_call(
        body, name=name, grid=(4, nt),
        in_specs=[hv_main, hg_main, hg_prev, hg_next,
                  pl.BlockSpec((None, 3, FF_BLK), lambda c, i: (c, 0, 0)),
                  pl.BlockSpec((None, 1, FF_BLK), lambda c, i: (c, 0, 0))],
        out_specs=pl.BlockSpec((None, tm, FF_BLK), lambda c, i: (c, i, 0)),
        out_shape=jax.ShapeDtypeStruct((4, s, FF_BLK), BF16),
        compiler_params=_cparams(2))(up, up, up, up, conv_w, conv_b)


def down_ple_ln_fwd(t, wdown, hb, wpg, pb, wpp, h1, g, b, li, name):
    s = hb.shape[0]
    tm = 256

    def body(t_ref, wd_ref, hb_ref, wpg_ref, p_ref, wpp_ref, h1_ref, g_ref, b_ref,
             z_ref, h_ref, hbo_ref, pg_ref, pp_ref):
        wd = _row_cat(wd_ref)
        ffn = _nn(t_ref[0], wd[0:FF_BLK, :])
        for c in range(1, 4):
            ffn = ffn + _nn(t_ref[c], wd[c * FF_BLK:(c + 1) * FF_BLK, :])
        pg = _nn(hb_ref[...], _row_cat(wpg_ref))
        pp = _nn(p_ref[...], _lane_cat(wpp_ref))
        z = ALPHA * h1_ref[...] + ffn + _sigmoid(pg) * pp
        h = _ln_fwd(z, g_ref[...], b_ref[...])
        z_ref[...] = z
        h_ref[...] = h
        hbo_ref[...] = h.astype(BF16)
        pg_ref[...] = pg.astype(BF16)
        pp_ref[...] = pp.astype(BF16)

    row = pl.BlockSpec((tm, D), lambda i: (i, 0))
    vec = pl.BlockSpec((1, D), lambda i: (0, 0))
    return pl.pallas_call(
        body, name=name, grid=(s // tm,),
        in_specs=[pl.BlockSpec((4, tm, FF_BLK), lambda i: (0, i, 0)),
                  _shards(N_DEV, FF_SHARD, D, li),
                  row, _shards(N_DEV, D // N_DEV, D, li),
                  pl.BlockSpec((tm, PLE_DIM), lambda i: (i, 0)),
                  _shards(N_DEV, PLE_DIM, 128, li),
                  row, vec, vec],
        out_specs=[row] * 5,
        out_shape=[jax.ShapeDtypeStruct((s, D), F32), jax.ShapeDtypeStruct((s, D), F32),
                   jax.ShapeDtypeStruct((s, D), BF16), jax.ShapeDtypeStruct((s, D), BF16),
                   jax.ShapeDtypeStruct((s, D), BF16)],
        compiler_params=_cparams(1))(t, wdown, hb, wpg, pb, wpp, h1, g, b)


def loss_bwd(h, target, name):
    s = h.shape[0]
    tm = 512

    def body(h_ref, t_ref, dh_ref, l_ref):
        @pl.when(pl.program_id(0) == 0)
        def _():
            l_ref[...] = jnp.zeros_like(l_ref)
        e = h_ref[...] - t_ref[...]
        dh_ref[...] = e * (1.0 / D)
        l_ref[...] += 0.5 * jnp.sum(jnp.mean(e * e, axis=-1, keepdims=True), axis=0, keepdims=True)

    row = pl.BlockSpec((tm, D), lambda i: (i, 0))
    return pl.pallas_call(
        body, name=name, grid=(s // tm,), in_specs=[row, row],
        out_specs=[row, pl.BlockSpec((1, 1), lambda i: (0, 0))],
        out_shape=[jax.ShapeDtypeStruct((s, D), F32), jax.ShapeDtypeStruct((1, 1), F32)],
        compiler_params=_cparams(1))(h, target)


def ln_bwd(dh, z, g, name, after=()):
    s = dh.shape[0]
    tm = 512
    na = len(after)

    def body(dh_ref, z_ref, g_ref, *rest):
        dz_ref, dg_ref, db_ref = rest[na:]

        @pl.when(pl.program_id(0) == 0)
        def _():
            dg_ref[...] = jnp.zeros_like(dg_ref)
            db_ref[...] = jnp.zeros_like(db_ref)
        dh = dh_ref[...]
        dz, dgx = _ln_bwd(dh, z_ref[...], g_ref[...])
        dz_ref[...] = dz
        dg_ref[...] += _colsum(dgx)
        db_ref[...] += _colsum(dh)

    row = pl.BlockSpec((tm, D), lambda i: (i, 0))
    vec = pl.BlockSpec((1, D), lambda i: (0, 0))
    return pl.pallas_call(
        body, name=name, grid=(s // tm,), in_specs=[row, row, vec] + [ANY] * na, out_specs=[row, vec, vec],
        out_shape=[jax.ShapeDtypeStruct((s, D), F32), jax.ShapeDtypeStruct((1, D), F32),
                   jax.ShapeDtypeStruct((1, D), F32)],
        compiler_params=_cparams(1))(dh, z, g, *after)


def ln2_ple_bwd(dh, z, g, pg, pp, name, after=()):
    s = dh.shape[0]
    tm = 512
    na = len(after)

    def body(dh_ref, z_ref, g_ref, pg_ref, pp_ref, *rest):
        dz_ref, dzb_ref, dpg_ref, dpp_ref, dg_ref, db_ref = rest[na:]

        @pl.when(pl.program_id(0) == 0)
        def _():
            dg_ref[...] = jnp.zeros_like(dg_ref)
            db_ref[...] = jnp.zeros_like(db_ref)
        dh = dh_ref[...]
        dz, dgx = _ln_bwd(dh, z_ref[...], g_ref[...])
        sg = _sigmoid(pg_ref[...].astype(F32))
        dz_ref[...] = dz
        dzb_ref[...] = dz.astype(BF16)
        dpg_ref[...] = (dz * pp_ref[...].astype(F32) * sg * (1.0 - sg)).astype(BF16)
        dpp_ref[...] = (dz * sg).astype(BF16)
        dg_ref[...] += _colsum(dgx)
        db_ref[...] += _colsum(dh)

    row = pl.BlockSpec((tm, D), lambda i: (i, 0))
    vec = pl.BlockSpec((1, D), lambda i: (0, 0))
    return pl.pallas_call(
        body, name=name, grid=(s // tm,), in_specs=[row, row, vec, row, row] + [ANY] * na,
        out_specs=[row, row, row, row, vec, vec],
        out_shape=[jax.ShapeDtypeStruct((s, D), F32)] + [jax.ShapeDtypeStruct((s, D), BF16)] * 3
        + [jax.ShapeDtypeStruct((1, D), F32)] * 2,
        compiler_params=_cparams(1))(dh, z, g, pg, pp, *after)


def wgrad_rows(a, dy, name, after=()):
    s, k = a.shape
    n = dy.shape[1]
    kb = k // N_DEV

    def body(a_ref, dy_ref, *rest):
        rest[-1][...] = _tn(a_ref[...], dy_ref[...]).astype(BF16)

    return pl.pallas_call(
        body, name=name, grid=(N_DEV,),
        in_specs=[pl.BlockSpec((s, kb), lambda j: (0, j)), pl.BlockSpec((s, n), lambda j: (0, 0))] + [ANY] * len(after),
        out_specs=pl.BlockSpec((None, kb, n), lambda j: (j, 0, 0)),
        out_shape=jax.ShapeDtypeStruct((N_DEV, kb, n), BF16),
        compiler_params=_cparams(1))(a, dy, *after)


def wgrad_cols(a, dy, name):
    s, k = a.shape
    n = dy.shape[1]
    nb = n // N_DEV

    def body(a_ref, dy_ref, o_ref):
        o_ref[...] = _tn(a_ref[...], dy_ref[...]).astype(BF16)

    return pl.pallas_call(
        body, name=name, grid=(N_DEV,),
        in_specs=[pl.BlockSpec((s, k), lambda j: (0, 0)), pl.BlockSpec((s, nb), lambda j: (0, j))],
        out_specs=pl.BlockSpec((None, k, nb), lambda j: (j, 0, 0)),
        out_shape=jax.ShapeDtypeStruct((N_DEV, k, nb), BF16),
        compiler_params=_cparams(1))(a, dy)


def wgrad_in(a, pieces, name):
    s, k = a.shape
    bn = N_PROJ // N_DEV
    n_narrow = 4

    def body(a_ref, *refs):
        dy_refs, (o_ref, cs_ref) = refs[:6], refs[6:]
        j = pl.program_id(0)

        def emit(dy_ref):
            dy = dy_ref[...]
            o_ref[...] = _tn(a_ref[...], dy).astype(BF16)
            cs_ref[...] = _colsum(dy.astype(F32))

        for idx in range(n_narrow):
            pl.when(j == idx)(lambda idx=idx: emit(dy_refs[idx]))
        pl.when((j >= n_narrow) & (j < n_narrow + 2))(lambda: emit(dy_refs[4]))
        pl.when(j >= n_narrow + 2)(lambda: emit(dy_refs[5]))

    narrow = pl.BlockSpec((s, bn), lambda j: (0, 0))
    return pl.pallas_call(
        body, name=name, grid=(N_DEV,),
        in_specs=[pl.BlockSpec((s, k), lambda j: (0, 0))] + [narrow] * n_narrow
        + [pl.BlockSpec((s, bn), lambda j: (0, jnp.clip(j - n_narrow, 0, 1))),
           pl.BlockSpec((s, bn), lambda j: (0, jnp.clip(j - n_narrow - 2, 0, 1)))],
        out_specs=[pl.BlockSpec((None, k, bn), lambda j: (j, 0, 0)), pl.BlockSpec((1, bn), lambda j: (0, j))],
        out_shape=[jax.ShapeDtypeStruct((N_DEV, k, bn), BF16), jax.ShapeDtypeStruct((1, N_PROJ), F32)],
        compiler_params=_cparams(1))(a, *pieces)


def wgrad_down(t, dy, name):
    _, s, k = t.shape
    n = dy.shape[1]

    def body(a_ref, dy_ref, o_ref):
        o_ref[...] = _tn(a_ref[...], dy_ref[...]).astype(BF16)

    return pl.pallas_call(
        body, name=name, grid=(4,),
        in_specs=[pl.BlockSpec((None, s, k), lambda j: (j, 0, 0)), pl.BlockSpec((s, n), lambda j: (0, 0))],
        out_specs=pl.BlockSpec((None, k, n), lambda j: (j, 0, 0)),
        out_shape=jax.ShapeDtypeStruct((4, k, n), BF16),
        compiler_params=_cparams(1))(t, dy)


def wgrad_up(a, dhv, dhg, name):
    s, k = a.shape

    def body(a_ref, dv_ref, dg_ref, o_ref):
        j = pl.program_id(0)

        @pl.when(j < 4)
        def _():
            o_ref[...] = _tn(dv_ref[...], a_ref[...]).astype(BF16)

        @pl.when(j >= 4)
        def _():
            o_ref[...] = _tn(dg_ref[...], a_ref[...]).astype(BF16)

    return pl.pallas_call(
        body, name=name, grid=(N_DEV,),
        in_specs=[pl.BlockSpec((s, k), lambda j: (0, 0)),
                  pl.BlockSpec((None, s, FF_BLK), lambda j: (jnp.minimum(j, 3), 0, 0)),
                  pl.BlockSpec((None, s, FF_BLK), lambda j: (jnp.maximum(j - 4, 0), 0, 0))],
        out_specs=pl.BlockSpec((None, FF_BLK, k), lambda j: (j, 0, 0)),
        out_shape=jax.ShapeDtypeStruct((N_DEV, FF_BLK, k), BF16),
        compiler_params=_cparams(1))(a, dhv, dhg)


def ffn_act_bwd(dzb, wdown, up, conv_w, conv_b, li, name):
    s = up.shape[1]
    tm = 512
    nt = s // tm
    before, after = _halo_maps(tm, s)
    hv_main, hv_prev, hv_next = _slab_specs(tm, s, lambda c: c)
    hg_main, hg_prev, hg_next = _slab_specs(tm, s, lambda c: 4 + c)

    def dc_of(dz, wd, hv, hg, back, fwd, cw_ref, cb_ref):
        dt = _nt(dz, wd)
        c = back * cw_ref[0:1, :] + hg * cw_ref[1:2, :] + fwd * cw_ref[2:3, :] + cb_ref[...]
        cdf = 0.5 * (1.0 + lax.erf(c * _SQRT_HALF))
        pdf = jnp.exp(-0.5 * c * c) * _INV_SQRT_2PI
        return dt, c * cdf, dt * hv * (cdf + c * pdf)

    def body(dz_ref, dzp_ref, dzn_ref, wd_ref, hv_ref, hvp_ref, hvn_ref, hg_ref, hgp_ref, hgn_ref, cw_ref, cb_ref,
             dhv_ref, dhg_ref, dcw_ref, dcb_ref):
        i = pl.program_id(1)

        @pl.when(i == 0)
        def _():
            dcw_ref[...] = jnp.zeros_like(dcw_ref)
            dcb_ref[...] = jnp.zeros_like(dcb_ref)

        wd = _row_cat(wd_ref)
        hg = hg_ref[...].astype(F32)
        hgp = hgp_ref[...].astype(F32)
        hgn = hgn_ref[...].astype(F32)
        first, last = i == 0, i == nt - 1
        e = HALO - 1
        back, fwd = _shift_rows(hg, jnp.where(first, 0.0, hgp[e:e + 1, :]), jnp.where(last, 0.0, hgn[0:1, :]))
        dt, act, dc = dc_of(dz_ref[...], wd, hv_ref[...].astype(F32), hg, back, fwd, cw_ref, cb_ref)
        dhv_ref[...] = (dt * act).astype(BF16)
        bp, fp = _shift_rows(hgp, hgp[0:1, :], hg[0:1, :])
        _, _, dcp = dc_of(dzp_ref[...], wd, hvp_ref[...].astype(F32), hgp, bp, fp, cw_ref, cb_ref)
        bn, fn = _shift_rows(hgn, hg[tm - 1:tm, :], hgn[e:e + 1, :])
        _, _, dcn = dc_of(dzn_ref[...], wd, hvn_ref[...].astype(F32), hgn, bn, fn, cw_ref, cb_ref)
        dc_back, dc_fwd = _shift_rows(dc, jnp.where(first, 0.0, dcp[e:e + 1, :]), jnp.where(last, 0.0, dcn[0:1, :]))
        dhg_ref[...] = (dc_fwd * cw_ref[0:1, :] + dc * cw_ref[1:2, :] + dc_back * cw_ref[2:3, :]).astype(BF16)
        dcw_ref[0:1, :] += _colsum(dc * back)
        dcw_ref[1:2, :] += _colsum(dc * hg)
        dcw_ref[2:3, :] += _colsum(dc * fwd)
        dcb_ref[...] += _colsum(dc)

    out_slab = pl.BlockSpec((None, tm, FF_BLK), lambda c, i: (c, i, 0))
    cw_spec = pl.BlockSpec((None, 3, FF_BLK), lambda c, i: (c, 0, 0))
    cb_spec = pl.BlockSpec((None, 1, FF_BLK), lambda c, i: (c, 0, 0))
    return pl.pallas_call(
        body, name=name, grid=(4, nt),
        in_specs=[pl.BlockSpec((tm, D), lambda c, i: (i, 0)),
                  pl.BlockSpec((HALO, D), lambda c, i: (before(i), 0)),
                  pl.BlockSpec((HALO, D), lambda c, i: (after(i), 0)),
                  _shards(2, FF_SHARD, D, li, lambda c, i: c),
                  hv_main, hv_prev, hv_next, hg_main, hg_prev, hg_next, cw_spec, cb_spec],
        out_specs=[out_slab, out_slab, cw_spec, cb_spec],
        out_shape=[jax.ShapeDtypeStruct((4, s, FF_BLK), BF16), jax.ShapeDtypeStruct((4, s, FF_BLK), BF16),
                   jax.ShapeDtypeStruct((4, 3, FF_BLK), F32), jax.ShapeDtypeStruct((4, 1, FF_BLK), F32)],
        compiler_params=_cparams(2))(dzb, dzb, dzb, wdown, up, up, up, up, up, up, conv_w, conv_b)


def dh1_ln1_bwd(dz2, dpg, wpg, dhv, dhg, wup, z1, g1, li, name, after=()):
    s = dz2.shape[0]
    tm = 256
    na = len(after)

    def body(dz2_ref, dpg_ref, wpg_ref, dhv_ref, dhg_ref, wup_ref, z1_ref, g_ref, *rest):
        dz_ref, dzb_ref, dg_ref, db_ref = rest[na:]

        @pl.when(pl.program_id(0) == 0)
        def _():
            dg_ref[...] = jnp.zeros_like(dg_ref)
            db_ref[...] = jnp.zeros_like(db_ref)
        dh = ALPHA * dz2_ref[...] + _nt(dpg_ref[...], _row_cat(wpg_ref))
        for c in range(4):
            dh = dh + _nn(dhv_ref[c], wup_ref[c]) + _nn(dhg_ref[c], wup_ref[4 + c])
        dz, dgx = _ln_bwd(dh, z1_ref[...], g_ref[...])
        dz_ref[...] = dz
        dzb_ref[...] = dz.astype(BF16)
        dg_ref[...] += _colsum(dgx)
        db_ref[...] += _colsum(dh)

    row = pl.BlockSpec((tm, D), lambda i: (i, 0))
    vec = pl.BlockSpec((1, D), lambda i: (0, 0))
    slab = pl.BlockSpec((4, tm, FF_BLK), lambda i: (0, i, 0))
    return pl.pallas_call(
        body, name=name, grid=(s // tm,),
        in_specs=[row, row, _shards(N_DEV, D // N_DEV, D, li), slab, slab, _shards(N_DEV, FF_BLK, D, li), row, vec]
        + [ANY] * na,
        out_specs=[row, row, vec, vec],
        out_shape=[jax.ShapeDtypeStruct((s, D), F32), jax.ShapeDtypeStruct((s, D), BF16),
                   jax.ShapeDtypeStruct((1, D), F32), jax.ShapeDtypeStruct((1, D), F32)],
        compiler_params=_cparams(1))(dz2, dpg, wpg, dhv, dhg, wup, z1, g1, *after)


def merge_bwd(dz1b, wmix, proj, ya, yp, li, name):
    s = dz1b.shape[0]
    tm, tn = 512, 512
    nt = D // tn
    per = tn // (D // N_DEV)
    ga0 = (3 * D_ATTN + D_POOL) // tn

    def body(dz_ref, w_ref, ga_ref, gb_ref, ya_ref, yp_ref, dya_ref, dyp_ref, dga_ref, dgb_ref):
        dm = _nt(dz_ref[...], _row_cat(w_ref))
        sa = _sigmoid(ga_ref[...].astype(F32))
        sb = _sigmoid(gb_ref[...].astype(F32))
        dya_ref[...] = (dm * sa).astype(BF16)
        dyp_ref[...] = (dm * sb).astype(BF16)
        dga_ref[...] = (dm * ya_ref[...].astype(F32) * sa * (1.0 - sa)).astype(BF16)
        dgb_ref[...] = (dm * yp_ref[...].astype(F32) * sb * (1.0 - sb)).astype(BF16)

    tile = pl.BlockSpec((tm, tn), lambda i, j: (i, j))
    return pl.pallas_call(
        body, name=name, grid=(s // tm, nt),
        in_specs=[pl.BlockSpec((tm, D), lambda i, j: (i, 0)),
                  _shards(per, D // N_DEV, D, li, lambda i, j: j),
                  pl.BlockSpec((tm, tn), lambda i, j: (i, ga0 + j)),
                  pl.BlockSpec((tm, tn), lambda i, j: (i, ga0 + nt + j)),
                  tile, tile],
        out_specs=[tile] * 4,
        out_shape=[jax.ShapeDtypeStruct((s, D), BF16)] * 4,
        compiler_params=_cparams(2))(dz1b, wmix, proj, proj, ya, yp)


def attn_out_bwd(dya, wao, li, name, after=()):
    s = dya.shape[0]
    tm = 512

    def body(d_ref, w_ref, *rest):
        rest[-1][...] = _nt(d_ref[...], _lane_cat(w_ref)).astype(BF16)

    return pl.pallas_call(
        body, name=name, grid=(s // tm,),
        in_specs=[pl.BlockSpec((tm, D), lambda i: (i, 0)), _shards(N_DEV, D_ATTN, 128, li)] + [ANY] * len(after),
        out_specs=pl.BlockSpec((tm, D_ATTN), lambda i: (i, 0)),
        out_shape=jax.ShapeDtypeStruct((s, D_ATTN), BF16),
        compiler_params=_cparams(1))(dya, wao, *after)


def pool_bwd(dyp, wpo, pm, pool_w, pool_scale, li, name):
    s = dyp.shape[0]

    def body(dyp_ref, wpo_ref, pm_ref, w_ref, sc_ref, du_ref, dw_ref, dsc_ref):
        wpo = _lane_cat(wpo_ref)
        dyp = dyp_ref[...]
        for g, w in enumerate(POOL_WINDOWS):
            cols = slice(g * PGD, (g + 1) * PGD)
            dpw = _nt(dyp, wpo[g * PGD:(g + 1) * PGD, :])
            pmg = pm_ref[:, cols]
            dsc_ref[:, cols] = _colsum(dpw * _nn(pmg, w_ref[g]))
            dpmw = (dpw * sc_ref[:, cols]).astype(BF16)
            dw_ref[g] = _tn(pmg, dpmw)
            dpm = _nt(dpmw, w_ref[g])
            du_ref[:, cols] = (_window_sum(dpm / _pool_counts(s, w), w, False) - dpm).astype(BF16)

    full = lambda shape: pl.BlockSpec(shape, lambda i: (0,) * len(shape))
    return pl.pallas_call(
        body, name=name, grid=(1,),
        in_specs=[full((s, D)), _shards(N_DEV, D_POOL, 128, li), full((s, D_POOL)), full((4, PGD, PGD)),
                  full((1, D_POOL))],
        out_specs=[full((s, D_POOL)), full((4, PGD, PGD)), full((1, D_POOL))],
        out_shape=[jax.ShapeDtypeStruct((s, D_POOL), BF16), jax.ShapeDtypeStruct((4, PGD, PGD), F32),
                   jax.ShapeDtypeStruct((1, D_POOL), F32)],
        compiler_params=_cparams(1))(dyp, wpo, pm, pool_w, pool_scale)


def attn_bwd(proj, da, e_rev, li, name, after=()):
    s = proj.shape[0]
    nb = s // QB
    skew = GRID_W + (GRID_W - KW)
    half_shape = (HALF * GRID_W, (HALF + KH) * GRID_W)

    def body(q_ref, k_ref, v_ref, do_ref, e_ref, *rest):
        dq_ref, dk_ref, dv_ref, g_ref, s_ref, dp_ref, ds_ref, p_ref, dkt_acc, dvt_acc = rest[len(after):]
        b = pl.program_id(1)

        @pl.when(b == 0)
        def _():
            dkt_acc[...] = jnp.zeros_like(dkt_acc)
            dvt_acc[...] = jnp.zeros_like(dvt_acc)
            g_ref[...] = jnp.zeros_like(g_ref)

        ri = lax.broadcasted_iota(I32, (QB, QB), 0)
        ci = lax.broadcasted_iota(I32, (QB, QB), 1)
        rev = jnp.where(ri + ci == QB - 1, 1.0, 0.0).astype(BF16)
        q = _nn(rev, q_ref[...]).astype(BF16) * ATT_SCALE
        do = _nn(rev, do_ref[...]).astype(BF16)
        lane = lax.broadcasted_iota(I32, (1, 128), 1)

        def block(btype, k0):
            dqs = []
            for rhalf in range(2):
                h0, hw = _half_window(btype, 1 - rhalf)
                hrows = slice(rhalf * HALF * GRID_W, (rhalf + 1) * HALF * GRID_W)
                kwin = k_ref[pl.ds(_token_at(k0, h0), hw), :]
                vwin = v_ref[pl.ds(_token_at(k0, h0), hw), :]
                dq = jnp.zeros((HALF * GRID_W, 128), F32)
                for hh in range(2):
                    lm = (lane // HEAD_DIM) == hh
                    qh = jnp.where(lm, q[hrows], jnp.zeros_like(q[hrows]))
                    doh = jnp.where(lm, do[hrows], jnp.zeros_like(do[hrows]))
                    kh = jnp.where(lm, kwin, jnp.zeros_like(kwin))
                    s_ref[:, 0:hw] = _nt(qh, kwin)
                    dp_ref[:, 0:hw] = _nt(doh, vwin)
                    g = jnp.zeros((1, KB), F32)
                    for r in range(HALF):
                        qr = QROWS - 1 - (rhalf * HALF + r)
                        rows = slice(r * GRID_W, (r + 1) * GRID_W)
                        sb, a0, w, pad = _row_logits(s_ref, e_ref, hh, rows, btype, qr, h0)
                        p = jnp.exp(sb - jnp.max(sb, axis=1, keepdims=True))
                        p = p * (1.0 / jnp.sum(p, axis=1, keepdims=True))
                        dp = dp_ref[rows, a0:a0 + w]
                        ds = p * (dp - jnp.sum(p * dp, axis=1, keepdims=True))
                        _store_row(ds_ref, rows, a0, w, ds, hw)
                        _store_row(p_ref, rows, a0, w, p, hw)
                        t = jnp.sum(pltpu.roll(ds, w - skew, 1, stride=1, stride_axis=0), axis=0, keepdims=True)
                        t = t[:, :KH * GRID_W] if pad else pltpu.roll(t, GRID_W, 1)
                        i0 = _attn_row(btype, qr)[2]
                        g = g + pltpu.roll(jnp.concatenate([t, jnp.zeros_like(t)], axis=1), i0 * GRID_W, 1)
                    g_ref[hh] += g
                    dsb = ds_ref[:, 0:hw]
                    dq = dq + _nn(dsb, kh) * ATT_SCALE
                    dkt_acc[:, pl.ds(_token_at(k0, h0), hw)] += _tn(qh, dsb)
                    dvt_acc[:, pl.ds(_token_at(k0, h0), hw)] += _tn(doh, p_ref[:, 0:hw])
                dqs.append(dq.astype(BF16))
            dq_ref[...] = _nn(rev, jnp.concatenate(dqs, axis=0)).astype(BF16)

        for btype, (cond, k0) in enumerate(_attn_types(b, nb)):
            pl.when(cond)(lambda btype=btype, k0=k0: block(btype, k0))

        @pl.when(b == nb - 1)
        def _():
            dk_ref[...] = dkt_acc[...].T.astype(BF16)
            dv_ref[...] = dvt_acc[...].T.astype(BF16)

    col = pl.BlockSpec((s, 128), lambda j, b: (0, j))
    return pl.pallas_call(
        body, name=name, grid=(4, nb),
        in_specs=[pl.BlockSpec((QB, 128), lambda j, b: (b, j)),
                  pl.BlockSpec((s, 128), lambda j, b: (0, 4 + j)),
                  pl.BlockSpec((s, 128), lambda j, b: (0, 8 + j)),
                  pl.BlockSpec((QB, 128), lambda j, b: (b, j)),
                  pl.BlockSpec((None, 2, 2, GRID_W, KB), lambda j, b: (li, j, 0, 0, 0))] + [ANY] * len(after),
        out_specs=[pl.BlockSpec((QB, 128), lambda j, b: (b, j)), col, col,
                   pl.BlockSpec((2, 1, KB), lambda j, b: (j, 0, 0))],
        out_shape=[jax.ShapeDtypeStruct((s, D_ATTN), BF16)] * 3 + [jax.ShapeDtypeStruct((N_HEADS, 1, KB), F32)],
        scratch_shapes=[pltpu.VMEM(half_shape, F32), pltpu.VMEM(half_shape, F32), pltpu.VMEM(half_shape, BF16),
                        pltpu.VMEM(half_shape, BF16), pltpu.VMEM((128, s), F32), pltpu.VMEM((128, s), F32)],
        compiler_params=_cparams(2))(proj, proj, proj, da, e_rev, *after)


def dh0_bwd(dz1, pieces, win, li, name, after=()):
    s = dz1.shape[0]
    tm = 256
    bn = N_PROJ // N_DEV

    def body(dz_ref, q_ref, k_ref, v_ref, u_ref, ga_ref, gb_ref, w_ref, *rest):
        acc = ALPHA * dz_ref[...]
        for j, ref in enumerate((q_ref, k_ref, v_ref, u_ref)):
            acc = acc + _nt(ref[...], w_ref[j])
        for j, ref in ((4, ga_ref), (6, gb_ref)):
            acc = acc + _nt(ref[:, 0:bn], w_ref[j]) + _nt(ref[:, bn:2 * bn], w_ref[j + 1])
        rest[-1][...] = acc

    row = pl.BlockSpec((tm, D), lambda i: (i, 0))
    narrow = pl.BlockSpec((tm, bn), lambda i: (i, 0))
    return pl.pallas_call(
        body, name=name, grid=(s // tm,),
        in_specs=[row] + [narrow] * 4 + [row, row, _shards(N_DEV, D, bn, li)] + [ANY] * len(after),
        out_specs=row, out_shape=jax.ShapeDtypeStruct((s, D), F32),
        compiler_params=_cparams(1))(dz1, *pieces, win, *after)


def _coords():
    return lax.axis_index("x"), lax.axis_index("y"), lax.axis_index("c")


def _dev_index(px, py, pc):
    return 4 * px + 2 * py + pc


HBM = pl.BlockSpec(memory_space=pltpu.HBM)
SEM = pl.BlockSpec(memory_space=pltpu.SEMAPHORE)
_EFFECT = pltpu.SideEffectType.DATAFLOW_SIDE_EFFECTING
_TOKEN = jax.ShapeDtypeStruct((8, 128), F32)


def _in_hbm(a):
    return pltpu.with_memory_space_constraint(a, pltpu.HBM)


def _hbm_like(a):
    return pltpu.HBM(a.shape, a.dtype)


def _peers(x, y, c):
    return [(x, y, 1 - c), (1 - x, y, c), (x, 1 - y, c), (1 - x, 1 - y, c)]


def ag_start(lands, after, name):
    n = len(lands)

    def body(*refs):
        land = refs[:n]
        send_sem, recv_sem, token = refs[n + 1], refs[n + 2], refs[-1]
        x, y, c = _coords()
        me = _dev_index(x, y, c)
        for k, peer in enumerate(_peers(x, y, c)):
            for a in range(n):
                pltpu.make_async_remote_copy(src_ref=land[a].at[me], dst_ref=land[a].at[me], send_sem=send_sem.at[k],
                                             recv_sem=recv_sem.at[k], device_id=peer, device_id_type=MESH).start()
        token[...] = jnp.zeros_like(token)

    res = pl.pallas_call(
        body, name=name,
        out_shape=(pltpu.SemaphoreType.DMA((4,)), pltpu.SemaphoreType.DMA((4,)), *[_hbm_like(l) for l in lands], _TOKEN),
        in_specs=[HBM] * n + [ANY], out_specs=(SEM, SEM, *[HBM] * n, pl.BlockSpec(memory_space=pltpu.VMEM)),
        input_output_aliases={a: 2 + a for a in range(n)},
        compiler_params=pltpu.CompilerParams(has_side_effects=_EFFECT),
    )(*[_in_hbm(l) for l in lands], after)
    return res[0], res[1], list(res[2:2 + n]), res[-1]


def ag_forward(send_sem, recv_sem, lands, after, name):
    n = len(lands)

    def body(*refs):
        send_sem, recv_sem = refs[0], refs[1]
        land = refs[2:2 + n]
        fsend, frecv = refs[3 + n], refs[4 + n]
        x, y, c = _coords()
        peers = _peers(x, y, c)
        for k in range(1, 4):
            blk = _dev_index(*peers[k])
            for a in range(n):
                pltpu.make_async_remote_copy(src_ref=land[a].at[blk], dst_ref=land[a].at[blk], send_sem=send_sem.at[k],
                                             recv_sem=recv_sem.at[k], device_id=peers[k], device_id_type=MESH).wait_recv()
        for k in range(1, 4):
            blk = _dev_index(*peers[k])
            for a in range(n):
                pltpu.make_async_remote_copy(src_ref=land[a].at[blk], dst_ref=land[a].at[blk], send_sem=fsend.at[k - 1],
                                             recv_sem=frecv.at[k - 1], device_id=peers[0], device_id_type=MESH).start()

    res = pl.pallas_call(
        body, name=name,
        out_shape=(pltpu.SemaphoreType.DMA((3,)), pltpu.SemaphoreType.DMA((3,)), *[_hbm_like(l) for l in lands]),
        in_specs=[SEM, SEM, *[HBM] * n, ANY], out_specs=(SEM, SEM, *[HBM] * n),
        input_output_aliases={2 + a: 2 + a for a in range(n)},
        compiler_params=pltpu.CompilerParams(has_side_effects=_EFFECT),
    )(send_sem, recv_sem, *lands, after)
    return res[0], res[1], list(res[2:])


def ag_finish(send_sem, recv_sem, fsend, frecv, lands, after, name):
    n = len(lands)

    def body(*refs):
        send_sem, recv_sem, fsend, frecv = refs[:4]
        land = refs[4:4 + n]
        x, y, c = _coords()
        me = _dev_index(x, y, c)
        peers = _peers(x, y, c)
        for k in range(4):
            for a in range(n):
                pltpu.make_async_remote_copy(src_ref=land[a].at[me], dst_ref=land[a].at[me], send_sem=send_sem.at[k],
                                             recv_sem=recv_sem.at[k], device_id=peers[k], device_id_type=MESH).wait_send()
        sib = _dev_index(*peers[0])
        for a in range(n):
            pltpu.make_async_remote_copy(src_ref=land[a].at[sib], dst_ref=land[a].at[sib], send_sem=send_sem.at[0],
                                         recv_sem=recv_sem.at[0], device_id=peers[0], device_id_type=MESH).wait_recv()
        for k in range(1, 4):
            mine = _dev_index(*peers[k])
            theirs = _dev_index(peers[k][0], peers[k][1], 1 - c)
            for a in range(n):
                pltpu.make_async_remote_copy(src_ref=land[a].at[mine], dst_ref=land[a].at[theirs], send_sem=fsend.at[k - 1],
                                             recv_sem=frecv.at[k - 1], device_id=peers[0], device_id_type=MESH).wait()

    res = pl.pallas_call(
        body, name=name, out_shape=tuple(_hbm_like(l) for l in lands),
        in_specs=[SEM] * 4 + [HBM] * n + [ANY], out_specs=tuple([HBM] * n),
        input_output_aliases={4 + a: a for a in range(n)},
        compiler_params=pltpu.CompilerParams(has_side_effects=_EFFECT),
    )(send_sem, recv_sem, fsend, frecv, *lands, after)
    return list(res)


def rs_start(psums, name):
    n = len(psums)
    lands = [lax.empty(p.shape, p.dtype) for p in psums]

    def body(*refs):
        src, land = refs[:n], refs[n:2 * n]
        send_sem, recv_sem, token = refs[2 * n], refs[2 * n + 1], refs[-1]
        peers = _peers(*_coords())
        for k in range(3):
            for a in range(n):
                pltpu.make_async_remote_copy(src_ref=src[a].at[k], dst_ref=land[a].at[k], send_sem=send_sem.at[k],
                                             recv_sem=recv_sem.at[k], device_id=peers[k + 1], device_id_type=MESH).start()
        token[...] = jnp.zeros_like(token)

    res = pl.pallas_call(
        body, name=name,
        out_shape=(pltpu.SemaphoreType.DMA((3,)), pltpu.SemaphoreType.DMA((3,)), *[_hbm_like(p) for p in psums],
                   *[_hbm_like(l) for l in lands], _TOKEN),
        in_specs=[HBM] * (2 * n), out_specs=(SEM, SEM, *[HBM] * (2 * n), pl.BlockSpec(memory_space=pltpu.VMEM)),
        input_output_aliases={a: 2 + a for a in range(2 * n)},
        compiler_params=pltpu.CompilerParams(has_side_effects=_EFFECT),
    )(*[_in_hbm(p) for p in psums], *[_in_hbm(l) for l in lands])
    return res[0], res[1], list(res[2:2 + n]), list(res[2 + n:2 + 2 * n]), res[-1]


def rs_finish(send_sem, recv_sem, psums, lands, after, name):
    n = len(psums)

    def body(*refs):
        send_sem, recv_sem = refs[0], refs[1]
        src, land = refs[2:2 + n], refs[2 + n:2 + 2 * n]
        peers = _peers(*_coords())
        for k in range(3):
            for a in range(n):
                pltpu.make_async_remote_copy(src_ref=src[a].at[k], dst_ref=land[a].at[k], send_sem=send_sem.at[k],
                                             recv_sem=recv_sem.at[k], device_id=peers[k + 1], device_id_type=MESH).wait()

    res = pl.pallas_call(
        body, name=name, out_shape=tuple(_hbm_like(l) for l in lands),
        in_specs=[SEM, SEM] + [HBM] * (2 * n) + [ANY], out_specs=tuple([HBM] * n),
        input_output_aliases={2 + n + a: a for a in range(n)},
        compiler_params=pltpu.CompilerParams(has_side_effects=_EFFECT),
    )(send_sem, recv_sem, *psums, *lands, after)
    return list(res)


def d2d_start(grads, name):
    n = len(grads)
    lands = [lax.empty((4,) + g.shape[1:], g.dtype) for g in grads]

    def body(*refs):
        src, land = refs[:n], refs[n:2 * n]
        send_sem, recv_sem, token = refs[2 * n], refs[2 * n + 1], refs[-1]
        x, y, c = _coords()
        for a in range(n):
            for k in range(4):
                blk = _dev_index(x ^ (k & 1), y ^ (k >> 1), 1 - c)
                pltpu.make_async_remote_copy(src_ref=src[a].at[blk], dst_ref=land[a].at[k], send_sem=send_sem.at[0],
                                             recv_sem=recv_sem.at[0], device_id=(x, y, 1 - c), device_id_type=MESH).start()
        token[...] = jnp.zeros_like(token)

    res = pl.pallas_call(
        body, name=name,
        out_shape=(pltpu.SemaphoreType.DMA((1,)), pltpu.SemaphoreType.DMA((1,)), *[_hbm_like(g) for g in grads],
                   *[_hbm_like(l) for l in lands], _TOKEN),
        in_specs=[HBM] * (2 * n), out_specs=(SEM, SEM, *[HBM] * (2 * n), pl.BlockSpec(memory_space=pltpu.VMEM)),
        input_output_aliases={a: 2 + a for a in range(2 * n)},
        compiler_params=pltpu.CompilerParams(has_side_effects=_EFFECT),
    )(*[_in_hbm(g) for g in grads], *[_in_hbm(l) for l in lands])
    return res[0], res[1], list(res[2:2 + n]), list(res[2 + n:2 + 2 * n]), res[-1]


def d2d_finish(send_sem, recv_sem, grads, lands, after, name):
    n = len(grads)

    def body(*refs):
        send_sem, recv_sem = refs[0], refs[1]
        src, land = refs[2:2 + n], refs[2 + n:2 + 2 * n]
        x, y, c = _coords()
        for a in range(n):
            for k in range(4):
                blk = _dev_index(x ^ (k & 1), y ^ (k >> 1), 1 - c)
                pltpu.make_async_remote_copy(src_ref=src[a].at[blk], dst_ref=land[a].at[k], send_sem=send_sem.at[0],
                                             recv_sem=recv_sem.at[0], device_id=(x, y, 1 - c), device_id_type=MESH).wait()

    res = pl.pallas_call(
        body, name=name, out_shape=tuple(_hbm_like(t) for t in list(grads) + list(lands)),
        in_specs=[SEM, SEM] + [HBM] * (2 * n) + [ANY], out_specs=tuple([HBM] * (2 * n)),
        input_output_aliases={2 + a: a for a in range(2 * n)},
        compiler_params=pltpu.CompilerParams(has_side_effects=_EFFECT),
    )(send_sem, recv_sem, *grads, *lands, after)
    return list(res[:n]), list(res[n:])


def pair_add(blk_idx, g, recv, name):
    _, r, c = g.shape
    tr = _row_tile(r)

    def body(idx_ref, g0, g1, g2, g3, r_ref, own_ref, oth_ref):
        own_ref[...] = g0[...].astype(F32) + r_ref[0].astype(F32)
        for k, gk in enumerate((g1, g2, g3)):
            oth_ref[k] = (gk[...].astype(F32) + r_ref[k + 1].astype(F32)).astype(BF16)

    def blk(k):
        return pl.BlockSpec((None, tr, c), lambda t, idx: (idx[k], t, 0))

    grid_spec = pltpu.PrefetchScalarGridSpec(
        num_scalar_prefetch=1, grid=(r // tr,),
        in_specs=[blk(0), blk(1), blk(2), blk(3), pl.BlockSpec((4, tr, c), lambda t, idx: (0, t, 0))],
        out_specs=[pl.BlockSpec((tr, c), lambda t, idx: (t, 0)), pl.BlockSpec((3, tr, c), lambda t, idx: (0, t, 0))])
    return pl.pallas_call(
        body, name=name, grid_spec=grid_spec,
        out_shape=[jax.ShapeDtypeStruct((r, c), F32), jax.ShapeDtypeStruct((3, r, c), BF16)],
        compiler_params=_cparams(1))(blk_idx, g, g, g, g, recv)


def _row_tile(r):
    return next(t for t in (512, 352, 256, 128) if r % t == 0)


def _adamw(w, g, m, v):
    m = ADAM_B1 * m + (1.0 - ADAM_B1) * g
    v = ADAM_B2 * v + (1.0 - ADAM_B2) * (g * g)
    m_hat = m / (1.0 - ADAM_B1 ** ADAM_STEP)
    v_hat = v / (1.0 - ADAM_B2 ** ADAM_STEP)
    delta = -ADAM_LR * (m_hat / (jnp.sqrt(v_hat) + ADAM_EPS) + ADAM_WD * w)
    return delta, m, v


def adamw_shard(own, recv, w, m, v, li, prev, name):
    r, c = own.shape
    tr = _row_tile(r)

    def body(own_ref, recv_ref, w_ref, m_ref, v_ref, p0, p1, p2, p3, g_ref, d_ref, nm_ref, nv_ref):
        g = own_ref[...] + recv_ref[0].astype(F32) + recv_ref[1].astype(F32) + recv_ref[2].astype(F32)
        delta, nm, nv = _adamw(w_ref[...], g, m_ref[...], v_ref[...])
        g_ref[...] = g
        d_ref[...] = delta
        nm_ref[...] = nm
        nv_ref[...] = nv

    lay = pl.BlockSpec((None, tr, c), lambda t: (li, t, 0))
    stack = jax.ShapeDtypeStruct((DEPTH, r, c), F32)
    return pl.pallas_call(
        body, name=name, grid=(r // tr,),
        in_specs=[pl.BlockSpec((tr, c), lambda t: (t, 0)), pl.BlockSpec((3, tr, c), lambda t: (0, t, 0)),
                  lay, lay, lay, ANY, ANY, ANY, ANY],
        out_specs=[lay] * 4, out_shape=[stack] * 4,
        input_output_aliases={5: 0, 6: 1, 7: 2, 8: 3},
        compiler_params=_cparams(1))(own, recv, w, m, v, *prev)


def sum_partials(gathered, name):
    _, r, c = gathered.shape
    tr = next(t for t in (96, 88, 64, _PACK_TILE) if r % t == 0)

    def body(gs_ref, g_ref):
        g = gs_ref[0]
        for d in range(1, N_DEV):
            g = g + gs_ref[d]
        g_ref[...] = g

    return pl.pallas_call(
        body, name=name, grid=(r // tr,),
        in_specs=[pl.BlockSpec((N_DEV, tr, c), lambda t: (0, t, 0))],
        out_specs=pl.BlockSpec((tr, c), lambda t: (t, 0)), out_shape=jax.ShapeDtypeStruct((r, c), F32),
        compiler_params=_cparams(1))(gathered)


def adamw_plain(g, w, m, v, name):
    def body(g_ref, w_ref, m_ref, v_ref, d_ref, nm_ref, nv_ref):
        delta, nm, nv = _adamw(w_ref[...], g_ref[...], m_ref[...], v_ref[...])
        d_ref[...] = delta
        nm_ref[...] = nm
        nv_ref[...] = nv

    return pl.pallas_call(body, name=name, out_shape=[jax.ShapeDtypeStruct(w.shape, F32)] * 3)(g, w, m, v)


_PACK_LAYER = (("b_in", (N_PROJ,)), ("rpb", (N_HEADS, 2 * KH - 1, 2 * KW - 1)), ("pool_w", (4, PGD, PGD)),
               ("pool_scale", (D_POOL,)), ("ln1_g", (D,)), ("ln1_b", (D,)), ("conv_b", (D_FF,)), ("ln2_g", (D,)),
               ("ln2_b", (D,)), ("conv_w", (3, D_FF)))
_PACK_INPUT = (("ln_in_g", (D,)), ("ln_in_b", (D,)))
_PACK_LANES = 1024
_PACK_TILE = 8
_EARLY = tuple(range(1, DEPTH))


def _pack_items(layers):
    items = [(n, (len(layers),) + s) for n, s in _PACK_LAYER]
    return items + ([(n, s) for n, s in _PACK_INPUT] if 0 in layers else [])


def _pack(parts, layers):
    flats = [(parts[name] if (name, shape) in _PACK_INPUT else jnp.stack([parts[name][li] for li in layers]))
             .reshape(-1).astype(F32) for name, shape in _pack_items(layers)]
    used = sum(f.shape[0] for f in flats)
    tile = _PACK_TILE * _PACK_LANES
    total = -(-used // tile) * tile
    return jnp.concatenate(flats + [jnp.zeros((total - used,), F32)]).reshape(total // _PACK_LANES, _PACK_LANES)


def _unpack(packed, layers):
    flat, out, off = packed.reshape(-1), {}, 0
    for name, shape in _pack_items(layers):
        n = int(np.prod(shape))
        out[name] = flat[off:off + n].reshape(shape)
        off += n
    return out


def _bias_tables(rpb):
    qc = np.arange(GRID_W)[:, None]
    kc = np.arange(GRID_W)[None, :]
    start = np.clip(qc - KW // 2, 0, GRID_W - KW)
    valid = (kc >= start) & (kc < start + KW)
    col = np.clip(kc - qc, -(KW - 1), KW - 1) + KW - 1
    onehot = (col.reshape(-1)[None, :] == np.arange(2 * KW - 1)[:, None]).astype(np.float32)
    depth = rpb.shape[0]
    rows = jnp.pad(rpb, ((0, 0), (0, 0), (0, 1), (0, 0)))
    tab = jnp.einsum("lhij,jm->lhim", rows, jnp.asarray(onehot), precision=lax.Precision.HIGHEST)
    tab = tab.reshape(depth, N_HEADS, KROWS, GRID_W, GRID_W).transpose(0, 1, 3, 2, 4)
    ok = valid[:, None, :] & (np.arange(KROWS) < 2 * KH - 1)[None, :, None]
    tab = jnp.where(jnp.asarray(ok), tab, NEG_INF).reshape(depth, N_HEADS, GRID_W, KB)
    tab = jnp.stack([tab, jnp.roll(tab, GRID_W, axis=-1)], axis=2)
    return tab, tab[:, :, :, ::-1, :]


_SHARDED = ("w_in", "w_attn_out", "w_pool_out", "w_mix_out", "w_up", "w_down", "w_ple_gate", "w_ple_proj")
_NAMES = ("ln_in_g", "ln_in_b", "w_in", "b_in", "rpb", "w_attn_out", "pool_w", "pool_scale", "w_pool_out", "w_mix_out",
          "ln1_g", "ln1_b", "w_up", "conv_w", "conv_b", "w_down", "w_ple_gate", "w_ple_proj", "ln2_g", "ln2_b")


def kernel(x, p, ln_in_g, ln_in_b, w_in, b_in, rpb, w_attn_out, pool_w, pool_scale, w_pool_out, w_mix_out, ln1_g, ln1_b, w_up, conv_w, conv_b, w_down, w_ple_gate, w_ple_proj, ln2_g, ln2_b, loss_target, m_ln_in_g, m_ln_in_b, m_w_in, m_b_in, m_rpb, m_w_attn_out, m_pool_w, m_pool_scale, m_w_pool_out, m_w_mix_out, m_ln1_g, m_ln1_b, m_w_up, m_conv_w, m_conv_b, m_w_down, m_w_ple_gate, m_w_ple_proj, m_ln2_g, m_ln2_b, v_ln_in_g, v_ln_in_b, v_w_in, v_b_in, v_rpb, v_w_attn_out, v_pool_w, v_pool_scale, v_w_pool_out, v_w_mix_out, v_ln1_g, v_ln1_b, v_w_up, v_conv_w, v_conv_b, v_w_down, v_w_ple_gate, v_w_ple_proj, v_ln2_g, v_ln2_b):
    a = dict(locals())
    W = {n: a[n] for n in _NAMES}
    M = {n: a["m_" + n] for n in _NAMES}
    V = {n: a["v_" + n] for n in _NAMES}
    xi, yi, ci = _coords()
    me = _dev_index(xi, yi, ci)
    x2, tgt = x[0], loss_target[0]
    pb = p[:, 0].astype(BF16)

    flip = lambda d: {**d, "w_up": d["w_up"].transpose(0, 2, 1)}
    ex = _Exchange(flip(W), flip(M), flip(V))
    loss_part, dx, parts = _local_step(x2, tgt, pb, W, ex)
    loss = lax.psum(loss_part[0, 0], AXES)

    started = ex.replicated_start("late", _pack(parts, (0,)), dx)
    done = ex.update(range(DEPTH - 1, 0, -1), started)
    ex.replicated_forward("late", done)
    done = ex.update((0,), done)
    stacks = {**ex.stacks, "w_up": [t.transpose(0, 2, 1) for t in ex.stacks["w_up"]]}
    lo, hi = [_unpack(sum_partials(ex.replicated_finish(tag, done), f"sum_replicated_{tag}"), layers)
              for layers, tag in (((0,), "late"), (_EARLY, "early"))]
    grads = {**{n: jnp.concatenate([lo[n], hi[n]]) for n, _ in _PACK_LAYER}, **{n: lo[n] for n, _ in _PACK_INPUT}}
    grads["conv_w"] = lax.dynamic_slice_in_dim(grads["conv_w"], me * FF_SHARD, FF_SHARD, axis=2)
    res = [{n: stacks[n][k] for n in _SHARDED} for k in range(4)]
    for n, g in grads.items():
        two_d = lambda t: t.reshape(-1, t.shape[-1])
        outs = adamw_plain(two_d(g), two_d(W[n]), two_d(M[n]), two_d(V[n]), f"adamw_{n}")
        for d, o in zip(res, [g] + [o.reshape(W[n].shape) for o in outs]):
            d[n] = o
    return (loss, dx[None], *[res[k][n] for k in range(4) for n in _NAMES])


class _Exchange:
    GROUPS = (("w_ple_gate", "w_ple_proj", "w_down", "w_up"), ("w_mix_out", "w_attn_out", "w_pool_out"), ("w_in",))
    FIRST = ("w_in",)

    def __init__(self, W, M, V):
        self.W, self.M, self.V = W, M, V
        xi, yi, ci = _coords()
        me = _dev_index(xi, yi, ci)
        self.me = me.astype(I32).reshape(1)
        self.rel_idx = jnp.stack([_dev_index(xi ^ (k & 1), yi ^ (k >> 1), ci) for k in range(4)]).astype(I32)
        self.lands = [{n: lax.dynamic_update_index_in_dim(lax.empty((N_DEV,) + W[n].shape[1:], BF16),
                                                          W[n][li].astype(BF16), me, 0) for n in _SHARDED}
                      for li in range(DEPTH)]
        cw_land = lax.dynamic_update_index_in_dim(lax.empty((N_DEV,) + W["conv_w"].shape, F32), W["conv_w"], me, 0)
        self.ag, self.fwd, self.rs, self.pending, self.small = {}, {}, {}, {}, {}
        self.stacks = {n: [lax.empty((DEPTH,) + W[n].shape[1:], F32) for _ in range(4)] for n in _SHARDED}
        self.late = tuple(n for n in _SHARDED if n not in self.FIRST)
        self.ag[0] = ag_start([self.lands[0][n] for n in self.FIRST] + [cw_land], W["conv_w"], "ag_start0")

    def tokens(self):
        return [self.ag[0][3]]

    def prefetch(self, li, after):
        send, recv, lands, _ = self.ag[li]
        self.fwd[li] = ag_forward(send, recv, lands, after, f"ag_forward{li}")
        if li == 0:
            self.ag["0b"] = ag_start([self.lands[0][n] for n in self.late], self.fwd[0][2][0], "ag_start0b")

    def weights(self, li, after):
        send, recv, _, _ = self.ag.pop(li)
        fsend, frecv, lands = self.fwd.pop(li)
        lands = ag_finish(send, recv, fsend, frecv, lands, after, f"ag_finish{li}")
        if li == 0:
            self.cw = lands[-1].transpose(1, 2, 0, 3).reshape(DEPTH, 3, 4, FF_BLK).transpose(0, 2, 1, 3)
            return dict(zip(self.FIRST, lands)), self.cw[li], (self.ag["0b"][3],)
        tokens = ()
        if li + 1 < DEPTH:
            self.ag[li + 1] = ag_start([self.lands[li + 1][n] for n in _SHARDED], lands[0], f"ag_start{li + 1}")
            tokens = (self.ag[li + 1][3],)
        return dict(zip(_SHARDED, lands)), self.cw[li], tokens

    def rest(self, li, G, mid, after):
        if li != 0:
            return G, ()
        send, recv, lands, _ = self.ag.pop("0b")
        fsend, frecv, lands = ag_forward(send, recv, lands, mid, "ag_forward0b")
        lands = ag_finish(send, recv, fsend, frecv, lands, after, "ag_finish0b")
        self.ag[1] = ag_start([self.lands[1][n] for n in _SHARDED], lands[0], "ag_start1")
        return {**G, **dict(zip(self.late, lands))}, (self.ag[1][3],)

    def grads(self, li, group, gw):
        self.pending.setdefault(li, {}).update(gw)
        if li != 0 and group != len(self.GROUPS) - 1:
            return None
        gw = self.pending.pop(li)
        tag = f"{li}_{group}" if li == 0 else f"{li}"
        send, recv, glist, lands, token = d2d_start(list(gw.values()), f"d2d_start{tag}")
        self.d2d = (tag, tuple(gw), send, recv, glist, lands)
        return token

    def flush(self, li, group, after):
        if li != 0 and group != len(self.GROUPS) - 1:
            return None
        tag, names, send, recv, glist, lands = self.d2d
        glist, recv1 = d2d_finish(send, recv, glist, lands, after, f"d2d_finish{tag}")
        sums = [pair_add(self.rel_idx, g, r1, f"pair_add_{n}{li}") for n, g, r1 in zip(names, glist, recv1)]
        send, recv, psums, lands, token = rs_start([s_[1] for s_ in sums], f"rs_start{tag}")
        self.rs.setdefault(li, []).append((tag, names, send, recv, psums, lands, [s_[0] for s_ in sums]))
        if li == 0 and group == 1 and "early" in self.small:
            self.replicated_forward("early", token)
        return token

    def update(self, layers, after):
        for li in layers:
            for tag, names, send, recv, psums, lands, owns in self.rs.pop(li):
                recv2 = rs_finish(send, recv, psums, lands, after, f"rs_finish{tag}")
                for n, own, r2 in zip(names, owns, recv2):
                    self.stacks[n] = adamw_shard(own, r2, self.W[n], self.M[n], self.V[n], li, self.stacks[n],
                                                 f"adamw_{n}{li}")
                    after = self.stacks[n][0]
        return after

    def replicated_start(self, tag, pack, after):
        land = lax.dynamic_update_index_in_dim(lax.empty((N_DEV,) + pack.shape, F32), pack, self.me[0], 0)
        self.small[tag] = ag_start([land], after, f"ag_start_small_{tag}")
        return self.small[tag][3]

    def replicated_early(self, small, after):
        return self.replicated_start("early", _pack(small, _EARLY), after)

    def replicated_forward(self, tag, after):
        send, recv, lands, _ = self.small[tag]
        self.small[tag] = (send, recv) + ag_forward(send, recv, lands, after, f"ag_forward_small_{tag}")

    def replicated_finish(self, tag, after):
        send, recv, fsend, frecv, lands = self.small.pop(tag)
        return ag_finish(send, recv, fsend, frecv, lands, after, f"ag_finish_small_{tag}")[0]


def _local_step(x2, tgt, pb, W, ex):
    depth = W["rpb"].shape[0]
    vec = lambda t: t.reshape(1, -1)
    ln1_g, ln1_b, ln2_g, ln2_b = W["ln1_g"], W["ln1_b"], W["ln2_g"], W["ln2_b"]
    b_in, rpb, pool_scale = W["b_in"], W["rpb"], W["pool_scale"]
    cb_full = W["conv_b"].reshape(depth, 4, 1, FF_BLK)
    pool_w_b = W["pool_w"].astype(BF16)
    e_tab, e_rev = _bias_tables(rpb)

    h, hb = ln_fwd(x2, vec(W["ln_in_g"]), vec(W["ln_in_b"]), "ln_in", after=ex.tokens())
    ex.prefetch(0, hb)
    saved = []
    for li in range(depth):
        G, cw, tokens = ex.weights(li, hb)
        bias = vec(b_in[li])
        proj, u = proj_fwd(hb, G["w_in"], bias, li, f"proj{li}", after=tokens)
        att = attn_fwd(proj, e_tab, li, f"attn{li}")
        pm, pw = pool_fwd(u, pool_w_b[li], vec(pool_scale[li]), f"pool{li}")
        G, tokens = ex.rest(li, G, att, pw)
        mg, ya, yp = merge_fwd(att, pw, G["w_attn_out"], G["w_pool_out"], proj, li, f"merge{li}", after=tokens)
        if li + 1 < depth:
            ex.prefetch(li + 1, mg)
        z1, h1, h1b = mix_ln_fwd(mg, G["w_mix_out"], h, vec(ln1_g[li]), vec(ln1_b[li]), li, f"mix_ln{li}")
        up = up_fwd(h1b, G["w_up"], li, f"up{li}")
        t = ffn_act_fwd(up, cw, cb_full[li], f"ffn_act{li}")
        z2, h2, h2b, pg, pp = down_ple_ln_fwd(t, G["w_down"], h1b, G["w_ple_gate"], pb[li], G["w_ple_proj"], h1,
                                              vec(ln2_g[li]), vec(ln2_b[li]), li, f"down_ln{li}")
        saved.append(dict(hb=hb, proj=proj, att=att, pm=pm, pw=pw, mg=mg, ya=ya, yp=yp, z1=z1, h1b=h1b, up=up, t=t,
                          z2=z2, pg=pg, pp=pp, G=G, cw=cw))
        h, hb = h2, h2b

    dh, loss_part = loss_bwd(h, tgt, "loss")
    small = {n: [None] * depth for n in ("b_in", "rpb", "pool_w", "pool_scale", "ln1_g", "ln1_b", "conv_b", "ln2_g",
                                         "ln2_b", "conv_w")}
    token = ()
    tok = lambda t: () if t is None else (t,)
    for li in reversed(range(depth)):
        sv = saved[li]
        G, cw = sv["G"], sv["cw"]
        dz2, dz2b, dpg, dpp, dg2, db2 = ln2_ple_bwd(dh, sv["z2"], vec(ln2_g[li]), sv["pg"], sv["pp"], f"ln2_bwd{li}",
                                                    after=token)
        gw = {}
        gw["w_ple_gate"] = wgrad_rows(sv["h1b"], dpg, f"dw_pg{li}")
        gw["w_ple_proj"] = wgrad_cols(pb[li], dpp, f"dw_pp{li}")
        gw["w_down"] = wgrad_down(sv["t"], dz2b, f"dw_down{li}").reshape(N_DEV, FF_SHARD, D)
        dhv, dhg, dcw, dcb = ffn_act_bwd(dz2b, G["w_down"], sv["up"], cw, cb_full[li], li, f"ffn_bwd{li}")
        gw["w_up"] = wgrad_up(sv["h1b"], dhv, dhg, f"dw_up{li}")
        token = tok(ex.grads(li, 0, gw))
        dz1, dz1b, dg1, db1 = dh1_ln1_bwd(dz2, dpg, G["w_ple_gate"], dhv, dhg, G["w_up"], sv["z1"], vec(ln1_g[li]), li,
                                          f"ln1_bwd{li}", after=token)
        token = tok(ex.flush(li, 0, dz1b))
        gw = {"w_mix_out": wgrad_rows(sv["mg"], dz1b, f"dw_mix{li}", after=token)}
        dya, dyp, dga, dgb = merge_bwd(dz1b, G["w_mix_out"], sv["proj"], sv["ya"], sv["yp"], li, f"merge_bwd{li}")
        gw["w_attn_out"] = wgrad_cols(sv["att"], dya, f"dw_ao{li}")
        gw["w_pool_out"] = wgrad_cols(sv["pw"], dyp, f"dw_po{li}")
        token = tok(ex.grads(li, 1, gw))
        da = attn_out_bwd(dya, G["w_attn_out"], li, f"da{li}", after=token)
        du, dpool_w, dpool_sc = pool_bwd(dyp, G["w_pool_out"], sv["pm"], pool_w_b[li], vec(pool_scale[li]), li,
                                         f"pool_bwd{li}")
        token = tok(ex.flush(li, 1, du))
        dq, dk, dv, drpb = attn_bwd(sv["proj"], da, e_rev, li, f"attn_bwd{li}", after=token)
        dproj = [dq, dk, dv, du, dga, dgb]
        dw_in, db_in = wgrad_in(sv["hb"], dproj, f"dw_in{li}")
        token = tok(ex.grads(li, 2, {"w_in": dw_in}))
        dh = dh0_bwd(dz1, dproj, G["w_in"], li, f"dh0{li}", after=token)
        small["b_in"][li] = db_in.reshape(N_PROJ)
        small["rpb"][li] = drpb.reshape(N_HEADS, KROWS, GRID_W)[:, :2 * KH - 1, :2 * KW - 1]
        small["pool_w"][li] = dpool_w
        small["pool_scale"][li] = dpool_sc.reshape(D_POOL)
        small["ln1_g"][li], small["ln1_b"][li] = dg1.reshape(D), db1.reshape(D)
        small["ln2_g"][li], small["ln2_b"][li] = dg2.reshape(D), db2.reshape(D)
        small["conv_b"][li] = dcb.reshape(D_FF)
        small["conv_w"][li] = dcw.transpose(1, 0, 2).reshape(3, D_FF)
        token = tok(ex.flush(li, 2, dh))
        if li == 1:
            token = token + tok(ex.replicated_early(small, dh))
    dx, dg_in, db_in0 = ln_bwd(dh, x2, vec(W["ln_in_g"]), "ln_in_bwd", after=token)
    parts = {n: jnp.stack(v_) for n, v_ in small.items()}
    parts["ln_in_g"], parts["ln_in_b"] = dg_in.reshape(D), db_in0.reshape(D)
    return loss_part, dx, parts
```

```python
import numpy as np
import jax
import jax.numpy as jnp
from jax import lax
from jax.experimental import pallas as pl
from jax.experimental.pallas import tpu as pltpu

F32 = jnp.float32
BF16 = jnp.bfloat16
I32 = jnp.int32

D = 1024
DEPTH = 4
GRID_W = 64
N_HEADS = 8
HEAD_DIM = 64
D_ATTN = 512
KH = 8
KW = 16
POOL_WINDOWS = (2, 4, 8, 16)
D_POOL = 512
PGD = 128
D_FF = 2816
PLE_DIM = 256
N_PROJ = 4096
ALPHA = (2 * DEPTH) ** 0.25
LN_EPS = 1e-5
NEG_INF = -1e30
ATT_SCALE = HEAD_DIM ** -0.5
ADAM_LR = 0.001
ADAM_B1 = 0.9
ADAM_B2 = 0.999
ADAM_EPS = 1e-08
ADAM_WD = 0.01
ADAM_STEP = 10

N_DEV = 8
AXES = ("x", "y", "c")
FF_BLK = D_FF // 4
FF_SHARD = D_FF // N_DEV
QROWS = 8
KROWS = 16
QB = QROWS * GRID_W
KB = KROWS * GRID_W
V7X_VMEM_LIMIT = 56 * 2 ** 20
MESH = pl.DeviceIdType.MESH
ANY = pl.BlockSpec(memory_space=pl.ANY)


def _cparams(n_grid):
    return pltpu.CompilerParams(dimension_semantics=("arbitrary",) * n_grid, vmem_limit_bytes=V7X_VMEM_LIMIT)


def _nn(a, b):
    return lax.dot_general(a, b, (((1,), (0,)), ((), ())), preferred_element_type=F32)


def _nt(a, b):
    return lax.dot_general(a, b, (((1,), (1,)), ((), ())), preferred_element_type=F32)


def _tn(a, b):
    return lax.dot_general(a, b, (((0,), (0,)), ((), ())), preferred_element_type=F32)


def _sigmoid(x):
    return 1.0 / (1.0 + jnp.exp(-x))


def _ln_fwd(z, g, b):
    mu = jnp.mean(z, axis=-1, keepdims=True)
    xc = z - mu
    var = jnp.mean(xc * xc, axis=-1, keepdims=True)
    return xc * lax.rsqrt(var + LN_EPS) * g + b


def _ln_bwd(dh, z, g):
    mu = jnp.mean(z, axis=-1, keepdims=True)
    xc = z - mu
    var = jnp.mean(xc * xc, axis=-1, keepdims=True)
    rstd = lax.rsqrt(var + LN_EPS)
    xhat = xc * rstd
    dxh = dh * g
    m1 = jnp.mean(dxh, axis=-1, keepdims=True)
    m2 = jnp.mean(dxh * xhat, axis=-1, keepdims=True)
    return rstd * (dxh - m1 - xhat * m2), dh * xhat


def _colsum(x):
    return jnp.sum(x, axis=0, keepdims=True)


def _lane_cat(ref):
    return jnp.concatenate([ref[j] for j in range(ref.shape[0])], axis=1)


def _row_cat(ref):
    n, r, c = ref.shape
    return ref[...].reshape(n * r, c)


def _shards(n, r, c, li, j_of=None):
    del li
    if j_of is None:
        return pl.BlockSpec((n, r, c), lambda *_: (0, 0, 0))
    return pl.BlockSpec((n, r, c), lambda *g: (j_of(*g), 0, 0))


def _shard(r, c, li, j_of):
    del li
    return pl.BlockSpec((None, r, c), lambda *g: (j_of(*g), 0, 0))


def ln_fwd(x, g, b, name, after=()):
    s = x.shape[0]
    tm = 512
    na = len(after)

    def body(x_ref, g_ref, b_ref, *rest):
        h_ref, hb_ref = rest[na:]
        h = _ln_fwd(x_ref[...], g_ref[...], b_ref[...])
        h_ref[...] = h
        hb_ref[...] = h.astype(BF16)

    row = pl.BlockSpec((tm, D), lambda i: (i, 0))
    vec = pl.BlockSpec((1, D), lambda i: (0, 0))
    return pl.pallas_call(
        body, name=name, grid=(s // tm,), in_specs=[row, vec, vec] + [ANY] * na, out_specs=[row, row],
        out_shape=[jax.ShapeDtypeStruct((s, D), F32), jax.ShapeDtypeStruct((s, D), BF16)],
        compiler_params=_cparams(1))(x, g, b, *after)


def proj_fwd(hb, win, bias, li, name, after=()):
    s = hb.shape[0]
    bn = N_PROJ // N_DEV
    tm = 1024
    pool_shard = (3 * D_ATTN) // bn

    def body(a_ref, w_ref, b_ref, *rest):
        o_ref, u_ref = rest[-2:]
        acc = _nn(a_ref[...], w_ref[...]) + b_ref[...]
        o_ref[...] = acc.astype(BF16)

        @pl.when(pl.program_id(1) == pool_shard)
        def _():
            u_ref[...] = acc

    return pl.pallas_call(
        body, name=name, grid=(s // tm, N_DEV),
        in_specs=[pl.BlockSpec((tm, D), lambda i, j: (i, 0)),
                  _shard(D, bn, li, lambda i, j: j),
                  pl.BlockSpec((1, bn), lambda i, j: (0, j))] + [ANY] * len(after),
        out_specs=[pl.BlockSpec((tm, bn), lambda i, j: (i, j)), pl.BlockSpec((tm, bn), lambda i, j: (i, 0))],
        out_shape=[jax.ShapeDtypeStruct((s, N_PROJ), BF16), jax.ShapeDtypeStruct((s, D_POOL), F32)],
        compiler_params=_cparams(2))(hb, win, bias, *after)


def _attn_types(b, nb):
    first, last = 0, (nb * QROWS - KROWS) * GRID_W
    mid = pl.multiple_of((QROWS * b - KH // 2) * GRID_W, 256)
    return ((b == 0, first), ((b > 0) & (b < nb - 1), mid), (b == nb - 1, last))


def _attn_row(btype, qr):
    lo, delta = ((max(qr - KH // 2, 0), 0), (qr, -(KH // 2)), (min(qr + KH // 2, KH), -KH))[btype]
    return lo, (qr - delta - (KH - 1)) % KROWS, lo - qr + delta + KH - 1


def _row_window(lo):
    pad = (lo % 2) * GRID_W
    return (lo // 2) * 128, KH * GRID_W + 2 * pad, pad


def _lanes(ref, start, width):
    start %= KB
    if start + width <= KB:
        return ref[:, start:start + width]
    return jnp.concatenate([ref[:, start:], ref[:, :start + width - KB]], axis=1)


HALF = QROWS // 2


def _rows_window(btype, qr0, n):
    spans = [_row_window(_attn_row(btype, qr)[0]) for qr in range(qr0, qr0 + n)]
    h0 = min(a0 for a0, _, _ in spans) // 256 * 256
    h1 = -(-max(a0 + w for a0, w, _ in spans) // 256) * 256
    return h0, h1 - h0


def _token_at(k0, h0):
    return k0 + h0 if isinstance(k0, int) else pl.multiple_of(k0 + h0, 256)


def _row_logits(s_ref, e_ref, hh, rows, btype, qr, h0):
    lo, shift, _ = _attn_row(btype, qr)
    a0, w, pad = _row_window(lo)
    e = e_ref.at[hh, shift % 2]
    sb = s_ref[rows, a0 - h0:a0 - h0 + w] + _lanes(e, a0 - (shift - shift % 2) * GRID_W, w)
    if pad:
        lane = lax.broadcasted_iota(I32, (1, w), 1)
        sb = jnp.where((lane >= pad) & (lane < w - pad), sb, NEG_INF)
    return sb, a0 - h0, w, pad


def _store_row(ref, rows, a0, w, val, width):
    if a0:
        ref[rows, 0:a0] = jnp.zeros((GRID_W, a0), ref.dtype)
    ref[rows, a0:a0 + w] = val.astype(ref.dtype)
    if a0 + w < width:
        ref[rows, a0 + w:width] = jnp.zeros((GRID_W, width - a0 - w), ref.dtype)


def attn_fwd(proj, e_tab, li, name):
    s = proj.shape[0]
    nb = s // QB

    def body(q_ref, k_ref, v_ref, e_ref, o_ref, s_ref, p_ref):
        q = q_ref[...] * ATT_SCALE
        lane = lax.broadcasted_iota(I32, (1, 128), 1)

        def block(btype, k0):
            for half in range(2):
                h0, hw = _rows_window(btype, half * HALF, HALF)
                hrows = slice(half * HALF * GRID_W, (half + 1) * HALF * GRID_W)
                kwin = k_ref[pl.ds(_token_at(k0, h0), hw), :]
                vwin = v_ref[pl.ds(_token_at(k0, h0), hw), :]
                acc = jnp.zeros((HALF * GRID_W, 128), F32)
                for hh in range(2):
                    lm = (lane // HEAD_DIM) == hh
                    qh = jnp.where(lm, q[hrows], jnp.zeros_like(q[hrows]))
                    vh = jnp.where(lm, vwin, jnp.zeros_like(vwin))
                    s_ref[:, 0:hw] = _nt(qh, kwin)
                    for r in range(HALF):
                        rows = slice(r * GRID_W, (r + 1) * GRID_W)
                        sb, a0, w, _ = _row_logits(s_ref, e_ref, hh, rows, btype, half * HALF + r, h0)
                        p = jnp.exp(sb - jnp.max(sb, axis=1, keepdims=True))
                        _store_row(p_ref, rows, a0, w, p * (1.0 / jnp.sum(p, axis=1, keepdims=True)), hw)
                    acc = acc + _nn(p_ref[:, 0:hw], vh)
                o_ref[hrows, :] = acc.astype(BF16)

        for btype, (cond, k0) in enumerate(_attn_types(pl.program_id(1), nb)):
            pl.when(cond)(lambda btype=btype, k0=k0: block(btype, k0))

    half_shape = (HALF * GRID_W, (HALF + KH - 1 + 1) * GRID_W)
    return pl.pallas_call(
        body, name=name, grid=(4, nb),
        in_specs=[pl.BlockSpec((QB, 128), lambda j, b: (b, j)),
                  pl.BlockSpec((s, 128), lambda j, b: (0, 4 + j)),
                  pl.BlockSpec((s, 128), lambda j, b: (0, 8 + j)),
                  pl.BlockSpec((None, 2, 2, GRID_W, KB), lambda j, b: (li, j, 0, 0, 0))],
        out_specs=pl.BlockSpec((QB, 128), lambda j, b: (b, j)),
        out_shape=jax.ShapeDtypeStruct((s, D_ATTN), BF16),
        scratch_shapes=[pltpu.VMEM(half_shape, F32), pltpu.VMEM(half_shape, BF16)],
        compiler_params=_cparams(2))(proj, proj, proj, e_tab)


_POOL_PAD = 8


def _pool_counts(s, w):
    t = lax.broadcasted_iota(I32, (s, 1), 0)
    return (jnp.minimum(t + w // 2, s) - jnp.maximum(t - w // 2, 0)).astype(F32)


def _window_sum(x, w, back_first):
    s = x.shape[0]
    z = jnp.zeros((_POOL_PAD, x.shape[1]), F32)
    xe = jnp.concatenate([z, x, z], axis=0)
    n = s + 2 * _POOL_PAD
    acc = xe + pltpu.roll(xe, 1 if back_first else n - 1, 0)
    k = 1
    while 2 * k < w:
        acc = pltpu.roll(acc, k, 0) + pltpu.roll(acc, n - k, 0)
        k *= 2
    return acc[_POOL_PAD:_POOL_PAD + s, :]


def pool_fwd(u, pool_w, pool_scale, name):
    s = u.shape[0]

    def body(u_ref, w_ref, sc_ref, pm_ref, pw_ref):
        for g, w in enumerate(POOL_WINDOWS):
            cols = slice(g * PGD, (g + 1) * PGD)
            ug = u_ref[:, cols]
            pm = (_window_sum(ug, w, True) / _pool_counts(s, w) - ug).astype(BF16)
            pm_ref[:, cols] = pm
            pw_ref[:, cols] = (_nn(pm, w_ref[g]) * sc_ref[:, cols]).astype(BF16)

    full = lambda shape: pl.BlockSpec(shape, lambda i: (0,) * len(shape))
    return pl.pallas_call(
        body, name=name, grid=(1,),
        in_specs=[full((s, D_POOL)), full((4, PGD, PGD)), full((1, D_POOL))],
        out_specs=[full((s, D_POOL)), full((s, D_POOL))],
        out_shape=[jax.ShapeDtypeStruct((s, D_POOL), BF16)] * 2,
        compiler_params=_cparams(1))(u, pool_w, pool_scale)


def merge_fwd(a, pw, wao, wpo, proj, li, name, after=()):
    s = a.shape[0]
    tm, tn = 512, 512
    nt = D // tn
    per = tn // 128

    def body(a_ref, pw_ref, wa_ref, wp_ref, ga_ref, gb_ref, *rest):
        mg_ref, ya_ref, yp_ref = rest[len(after):]
        ya = _nn(a_ref[...], _lane_cat(wa_ref))
        yp = _nn(pw_ref[...], _lane_cat(wp_ref))
        mg = _sigmoid(ga_ref[...].astype(F32)) * ya + _sigmoid(gb_ref[...].astype(F32)) * yp
        mg_ref[...] = mg.astype(BF16)
        ya_ref[...] = ya.astype(BF16)
        yp_ref[...] = yp.astype(BF16)

    act = pl.BlockSpec((tm, D_ATTN), lambda i, j: (i, 0))
    wsp = _shards(per, D_ATTN, 128, li, lambda i, j: j)
    out = pl.BlockSpec((tm, tn), lambda i, j: (i, j))
    ga0 = (3 * D_ATTN + D_POOL) // tn
    return pl.pallas_call(
        body, name=name, grid=(s // tm, nt),
        in_specs=[act, act, wsp, wsp,
                  pl.BlockSpec((tm, tn), lambda i, j: (i, ga0 + j)),
                  pl.BlockSpec((tm, tn), lambda i, j: (i, ga0 + nt + j))] + [ANY] * len(after),
        out_specs=[out, out, out],
        out_shape=[jax.ShapeDtypeStruct((s, D), BF16)] * 3,
        compiler_params=_cparams(2))(a, pw, wao, wpo, proj, proj, *after)


def mix_ln_fwd(mg, wmix, h0, g, b, li, name):
    s = mg.shape[0]
    tm = 256

    def body(mg_ref, w_ref, h0_ref, g_ref, b_ref, z_ref, h_ref, hb_ref):
        z = ALPHA * h0_ref[...] + _nn(mg_ref[...], _row_cat(w_ref))
        h = _ln_fwd(z, g_ref[...], b_ref[...])
        z_ref[...] = z
        h_ref[...] = h
        hb_ref[...] = h.astype(BF16)

    row = pl.BlockSpec((tm, D), lambda i: (i, 0))
    vec = pl.BlockSpec((1, D), lambda i: (0, 0))
    return pl.pallas_call(
        body, name=name, grid=(s // tm,),
        in_specs=[row, _shards(N_DEV, D // N_DEV, D, li), row, vec, vec],
        out_specs=[row, row, row],
        out_shape=[jax.ShapeDtypeStruct((s, D), F32), jax.ShapeDtypeStruct((s, D), F32),
                   jax.ShapeDtypeStruct((s, D), BF16)],
        compiler_params=_cparams(1))(mg, wmix, h0, g, b)


def up_fwd(hb, wup, li, name):
    s = hb.shape[0]
    tm = 1024

    def body(a_ref, w_ref, o_ref):
        o_ref[...] = _nt(a_ref[...], w_ref[...]).astype(BF16)

    return pl.pallas_call(
        body, name=name, grid=(s // tm, N_DEV),
        in_specs=[pl.BlockSpec((tm, D), lambda i, j: (i, 0)), _shard(FF_BLK, D, li, lambda i, j: j)],
        out_specs=pl.BlockSpec((None, tm, FF_BLK), lambda i, j: (j, i, 0)),
        out_shape=jax.ShapeDtypeStruct((N_DEV, s, FF_BLK), BF16),
        compiler_params=_cparams(2))(hb, wup)


_SQRT_HALF = 0.7071067811865476
_INV_SQRT_2PI = 0.3989422804014327


def _shift_rows(x, prev_row, next_row):
    n = x.shape[0]
    r = lax.broadcasted_iota(I32, (n, 1), 0)
    back = jnp.where(r == 0, prev_row, pltpu.roll(x, 1, 0))
    fwd = jnp.where(r == n - 1, next_row, pltpu.roll(x, n - 1, 0))
    return back, fwd


HALO = 16


def _halo_maps(tm, s):
    th = tm // HALO
    return (lambda i: jnp.maximum(i * th - 1, 0)), (lambda i: jnp.minimum((i + 1) * th, s // HALO - 1))


def _slab_specs(tm, s, blk_of):
    before, after = _halo_maps(tm, s)
    main = pl.BlockSpec((None, tm, FF_BLK), lambda c, i: (blk_of(c), i, 0))
    prev = pl.BlockSpec((None, HALO, FF_BLK), lambda c, i: (blk_of(c), before(i), 0))
    nxt = pl.BlockSpec((None, HALO, FF_BLK), lambda c, i: (blk_of(c), after(i), 0))
    return main, prev, nxt


def ffn_act_fwd(up, conv_w, conv_b, name):
    s = up.shape[1]
    tm = 512
    nt = s // tm
    hv_main, _, _ = _slab_specs(tm, s, lambda c: c)
    hg_main, hg_prev, hg_next = _slab_specs(tm, s, lambda c: 4 + c)

    def body(hv_ref, hg_ref, hp_ref, hn_ref, cw_ref, cb_ref, t_ref):
        i = pl.program_id(1)
        hg = hg_ref[...].astype(F32)
        prow = jnp.where(i == 0, 0.0, hp_ref[...].astype(F32)[HALO - 1:HALO, :])
        nrow = jnp.where(i == nt - 1, 0.0, hn_ref[...].astype(F32)[0:1, :])
        back, fwd = _shift_rows(hg, prow, nrow)
        c = back * cw_ref[0:1, :] + hg * cw_ref[1:2, :] + fwd * cw_ref[2:3, :] + cb_ref[...]
        act = 0.5 * c * (1.0 + lax.erf(c * _SQRT_HALF))
        t_ref[...] = (act * hv_ref[...].astype(F32)).astype(BF16)

    return pl.pallas_call(
        body, name=name, grid=(4, nt),
        in_specs=[hv_main, hg_main, hg_prev, hg_next,
                  pl.BlockSpec((None, 3, FF_BLK), lambda c, i: (c, 0, 0)),
                  pl.BlockSpec((None, 1, FF_BLK), lambda c, i: (c, 0, 0))],
        out_specs=pl.BlockSpec((None, tm, FF_BLK), lambda c, i: (c, i, 0)),
        out_shape=jax.ShapeDtypeStruct((4, s, FF_BLK), BF16),
        compiler_params=_cparams(2))(up, up, up, up, conv_w, conv_b)


def down_ple_ln_fwd(t, wdown, hb, wpg, pb, wpp, h1, g, b, li, name):
    s = hb.shape[0]
    tm = 256

    def body(t_ref, wd_ref, hb_ref, wpg_ref, p_ref, wpp_ref, h1_ref, g_ref, b_ref,
             z_ref, h_ref, hbo_ref, pg_ref, pp_ref):
        wd = _row_cat(wd_ref)
        ffn = _nn(t_ref[0], wd[0:FF_BLK, :])
        for c in range(1, 4):
            ffn = ffn + _nn(t_ref[c], wd[c * FF_BLK:(c + 1) * FF_BLK, :])
        pg = _nn(hb_ref[...], _row_cat(wpg_ref))
        pp = _nn(p_ref[...], _lane_cat(wpp_ref))
        z = ALPHA * h1_ref[...] + ffn + _sigmoid(pg) * pp
        h = _ln_fwd(z, g_ref[...], b_ref[...])
        z_ref[...] = z
        h_ref[...] = h
        hbo_ref[...] = h.astype(BF16)
        pg_ref[...] = pg.astype(BF16)
        pp_ref[...] = pp.astype(BF16)

    row = pl.BlockSpec((tm, D), lambda i: (i, 0))
    vec = pl.BlockSpec((1, D), lambda i: (0, 0))
    return pl.pallas_call(
        body, name=name, grid=(s // tm,),
        in_specs=[pl.BlockSpec((4, tm, FF_BLK), lambda i: (0, i, 0)),
                  _shards(N_DEV, FF_SHARD, D, li),
                  row, _shards(N_DEV, D // N_DEV, D, li),
                  pl.BlockSpec((tm, PLE_DIM), lambda i: (i, 0)),
                  _shards(N_DEV, PLE_DIM, 128, li),
                  row, vec, vec],
        out_specs=[row] * 5,
        out_shape=[jax.ShapeDtypeStruct((s, D), F32), jax.ShapeDtypeStruct((s, D), F32),
                   jax.ShapeDtypeStruct((s, D), BF16), jax.ShapeDtypeStruct((s, D), BF16),
                   jax.ShapeDtypeStruct((s, D), BF16)],
        compiler_params=_cparams(1))(t, wdown, hb, wpg, pb, wpp, h1, g, b)


def loss_bwd(h, target, name):
    s = h.shape[0]
    tm = 512

    def body(h_ref, t_ref, dh_ref, l_ref):
        @pl.when(pl.program_id(0) == 0)
        def _():
            l_ref[...] = jnp.zeros_like(l_ref)
        e = h_ref[...] - t_ref[...]
        dh_ref[...] = e * (1.0 / D)
        l_ref[...] += 0.5 * jnp.sum(jnp.mean(e * e, axis=-1, keepdims=True), axis=0, keepdims=True)

    row = pl.BlockSpec((tm, D), lambda i: (i, 0))
    return pl.pallas_call(
        body, name=name, grid=(s // tm,), in_specs=[row, row],
        out_specs=[row, pl.BlockSpec((1, 1), lambda i: (0, 0))],
        out_shape=[jax.ShapeDtypeStruct((s, D), F32), jax.ShapeDtypeStruct((1, 1), F32)],
        compiler_params=_cparams(1))(h, target)


def ln_bwd(dh, z, g, name, after=()):
    s = dh.shape[0]
    tm = 512
    na = len(after)

    def body(dh_ref, z_ref, g_ref, *rest):
        dz_ref, dg_ref, db_ref = rest[na:]

        @pl.when(pl.program_id(0) == 0)
        def _():
            dg_ref[...] = jnp.zeros_like(dg_ref)
            db_ref[...] = jnp.zeros_like(db_ref)
        dh = dh_ref[...]
        dz, dgx = _ln_bwd(dh, z_ref[...], g_ref[...])
        dz_ref[...] = dz
        dg_ref[...] += _colsum(dgx)
        db_ref[...] += _colsum(dh)

    row = pl.BlockSpec((tm, D), lambda i: (i, 0))
    vec = pl.BlockSpec((1, D), lambda i: (0, 0))
    return pl.pallas_call(
        body, name=name, grid=(s // tm,), in_specs=[row, row, vec] + [ANY] * na, out_specs=[row, vec, vec],
        out_shape=[jax.ShapeDtypeStruct((s, D), F32), jax.ShapeDtypeStruct((1, D), F32),
                   jax.ShapeDtypeStruct((1, D), F32)],
        compiler_params=_cparams(1))(dh, z, g, *after)


def ln2_ple_bwd(dh, z, g, pg, pp, name, after=()):
    s = dh.shape[0]
    tm = 512
    na = len(after)

    def body(dh_ref, z_ref, g_ref, pg_ref, pp_ref, *rest):
        dz_ref, dzb_ref, dpg_ref, dpp_ref, dg_ref, db_ref = rest[na:]

        @pl.when(pl.program_id(0) == 0)
        def _():
            dg_ref[...] = jnp.zeros_like(dg_ref)
            db_ref[...] = jnp.zeros_like(db_ref)
        dh = dh_ref[...]
        dz, dgx = _ln_bwd(dh, z_ref[...], g_ref[...])
        sg = _sigmoid(pg_ref[...].astype(F32))
        dz_ref[...] = dz
        dzb_ref[...] = dz.astype(BF16)
        dpg_ref[...] = (dz * pp_ref[...].astype(F32) * sg * (1.0 - sg)).astype(BF16)
        dpp_ref[...] = (dz * sg).astype(BF16)
        dg_ref[...] += _colsum(dgx)
        db_ref[...] += _colsum(dh)

    row = pl.BlockSpec((tm, D), lambda i: (i, 0))
    vec = pl.BlockSpec((1, D), lambda i: (0, 0))
    return pl.pallas_call(
        body, name=name, grid=(s // tm,), in_specs=[row, row, vec, row, row] + [ANY] * na,
        out_specs=[row, row, row, row, vec, vec],
        out_shape=[jax.ShapeDtypeStruct((s, D), F32)] + [jax.ShapeDtypeStruct((s, D), BF16)] * 3
        + [jax.ShapeDtypeStruct((1, D), F32)] * 2,
        compiler_params=_cparams(1))(dh, z, g, pg, pp, *after)


def wgrad_rows(a, dy, name, after=()):
    s, k = a.shape
    n = dy.shape[1]
    kb = k // N_DEV

    def body(a_ref, dy_ref, *rest):
        rest[-1][...] = _tn(a_ref[...], dy_ref[...]).astype(BF16)

    return pl.pallas_call(
        body, name=name, grid=(N_DEV,),
        in_specs=[pl.BlockSpec((s, kb), lambda j: (0, j)), pl.BlockSpec((s, n), lambda j: (0, 0))] + [ANY] * len(after),
        out_specs=pl.BlockSpec((None, kb, n), lambda j: (j, 0, 0)),
        out_shape=jax.ShapeDtypeStruct((N_DEV, kb, n), BF16),
        compiler_params=_cparams(1))(a, dy, *after)


def wgrad_pair(rows_a, rows_dy, cols, name):
    specs, args, outs, shapes, kinds = [], [], [], [], []
    if rows_a is not None:
        s, k = rows_a.shape
        n = rows_dy.shape[1]
        specs += [pl.BlockSpec((s, k // N_DEV), lambda j: (0, j)), pl.BlockSpec((s, n), lambda j: (0, 0))]
        args += [rows_a, rows_dy]
        outs.append(pl.BlockSpec((None, k // N_DEV, n), lambda j: (j, 0, 0)))
        shapes.append(jax.ShapeDtypeStruct((N_DEV, k // N_DEV, n), BF16))
    for a, dy in cols:
        s, k = a.shape
        n = dy.shape[1]
        specs += [pl.BlockSpec((s, k), lambda j: (0, 0)), pl.BlockSpec((s, n // N_DEV), lambda j: (0, j))]
        args += [a, dy]
        outs.append(pl.BlockSpec((None, k, n // N_DEV), lambda j: (j, 0, 0)))
        shapes.append(jax.ShapeDtypeStruct((N_DEV, k, n // N_DEV), BF16))
    n_pairs = len(shapes)

    def body(*refs):
        for i in range(n_pairs):
            refs[2 * n_pairs + i][...] = _tn(refs[2 * i][...], refs[2 * i + 1][...]).astype(BF16)

    return pl.pallas_call(body, name=name, grid=(N_DEV,), in_specs=specs, out_specs=outs, out_shape=shapes,
                          compiler_params=_cparams(1))(*args)


def wgrad_in(a, pieces, name):
    s, k = a.shape
    bn = N_PROJ // N_DEV
    n_narrow = 4

    def body(a_ref, *refs):
        dy_refs, (o_ref, cs_ref) = refs[:6], refs[6:]
        j = pl.program_id(0)

        def emit(dy_ref):
            dy = dy_ref[...]
            o_ref[...] = _tn(a_ref[...], dy).astype(BF16)
            cs_ref[...] = _colsum(dy.astype(F32))

        for idx in range(n_narrow):
            pl.when(j == idx)(lambda idx=idx: emit(dy_refs[idx]))
        pl.when((j >= n_narrow) & (j < n_narrow + 2))(lambda: emit(dy_refs[4]))
        pl.when(j >= n_narrow + 2)(lambda: emit(dy_refs[5]))

    narrow = pl.BlockSpec((s, bn), lambda j: (0, 0))
    return pl.pallas_call(
        body, name=name, grid=(N_DEV,),
        in_specs=[pl.BlockSpec((s, k), lambda j: (0, 0))] + [narrow] * n_narrow
        + [pl.BlockSpec((s, bn), lambda j: (0, jnp.clip(j - n_narrow, 0, 1))),
           pl.BlockSpec((s, bn), lambda j: (0, jnp.clip(j - n_narrow - 2, 0, 1)))],
        out_specs=[pl.BlockSpec((None, k, bn), lambda j: (j, 0, 0)), pl.BlockSpec((1, bn), lambda j: (0, j))],
        out_shape=[jax.ShapeDtypeStruct((N_DEV, k, bn), BF16), jax.ShapeDtypeStruct((1, N_PROJ), F32)],
        compiler_params=_cparams(1))(a, *pieces)


def wgrad_down(t, dy, name):
    _, s, k = t.shape
    n = dy.shape[1]

    def body(a_ref, dy_ref, o_ref):
        o_ref[...] = _tn(a_ref[...], dy_ref[...]).astype(BF16)

    return pl.pallas_call(
        body, name=name, grid=(4,),
        in_specs=[pl.BlockSpec((None, s, k), lambda j: (j, 0, 0)), pl.BlockSpec((s, n), lambda j: (0, 0))],
        out_specs=pl.BlockSpec((None, k, n), lambda j: (j, 0, 0)),
        out_shape=jax.ShapeDtypeStruct((4, k, n), BF16),
        compiler_params=_cparams(1))(t, dy)


def wgrad_up(a, dhv, dhg, name):
    s, k = a.shape

    def body(a_ref, dv_ref, dg_ref, o_ref):
        j = pl.program_id(0)

        @pl.when(j < 4)
        def _():
            o_ref[...] = _tn(dv_ref[...], a_ref[...]).astype(BF16)

        @pl.when(j >= 4)
        def _():
            o_ref[...] = _tn(dg_ref[...], a_ref[...]).astype(BF16)

    return pl.pallas_call(
        body, name=name, grid=(N_DEV,),
        in_specs=[pl.BlockSpec((s, k), lambda j: (0, 0)),
                  pl.BlockSpec((None, s, FF_BLK), lambda j: (jnp.minimum(j, 3), 0, 0)),
                  pl.BlockSpec((None, s, FF_BLK), lambda j: (jnp.maximum(j - 4, 0), 0, 0))],
        out_specs=pl.BlockSpec((None, FF_BLK, k), lambda j: (j, 0, 0)),
        out_shape=jax.ShapeDtypeStruct((N_DEV, FF_BLK, k), BF16),
        compiler_params=_cparams(1))(a, dhv, dhg)


def ffn_act_bwd(dzb, wdown, up, conv_w, conv_b, li, name):
    s = up.shape[1]
    tm = 512
    nt = s // tm
    before, after = _halo_maps(tm, s)
    hv_main, hv_prev, hv_next = _slab_specs(tm, s, lambda c: c)
    hg_main, hg_prev, hg_next = _slab_specs(tm, s, lambda c: 4 + c)

    def dc_of(dz, wd, hv, hg, back, fwd, cw_ref, cb_ref):
        dt = _nt(dz, wd)
        c = back * cw_ref[0:1, :] + hg * cw_ref[1:2, :] + fwd * cw_ref[2:3, :] + cb_ref[...]
        cdf = 0.5 * (1.0 + lax.erf(c * _SQRT_HALF))
        pdf = jnp.exp(-0.5 * c * c) * _INV_SQRT_2PI
        return dt, c * cdf, dt * hv * (cdf + c * pdf)

    def body(dz_ref, dzp_ref, dzn_ref, wd_ref, hv_ref, hvp_ref, hvn_ref, hg_ref, hgp_ref, hgn_ref, cw_ref, cb_ref,
             dhv_ref, dhg_ref, dcw_ref, dcb_ref):
        i = pl.program_id(1)

        @pl.when(i == 0)
        def _():
            dcw_ref[...] = jnp.zeros_like(dcw_ref)
            dcb_ref[...] = jnp.zeros_like(dcb_ref)

        wd = _row_cat(wd_ref)
        hg = hg_ref[...].astype(F32)
        hgp = hgp_ref[...].astype(F32)
        hgn = hgn_ref[...].astype(F32)
        first, last = i == 0, i == nt - 1
        e = HALO - 1
        back, fwd = _shift_rows(hg, jnp.where(first, 0.0, hgp[e:e + 1, :]), jnp.where(last, 0.0, hgn[0:1, :]))
        dt, act, dc = dc_of(dz_ref[...], wd, hv_ref[...].astype(F32), hg, back, fwd, cw_ref, cb_ref)
        dhv_ref[...] = (dt * act).astype(BF16)
        bp, fp = _shift_rows(hgp, hgp[0:1, :], hg[0:1, :])
        _, _, dcp = dc_of(dzp_ref[...], wd, hvp_ref[...].astype(F32), hgp, bp, fp, cw_ref, cb_ref)
        bn, fn = _shift_rows(hgn, hg[tm - 1:tm, :], hgn[e:e + 1, :])
        _, _, dcn = dc_of(dzn_ref[...], wd, hvn_ref[...].astype(F32), hgn, bn, fn, cw_ref, cb_ref)
        dc_back, dc_fwd = _shift_rows(dc, jnp.where(first, 0.0, dcp[e:e + 1, :]), jnp.where(last, 0.0, dcn[0:1, :]))
        dhg_ref[...] = (dc_fwd * cw_ref[0:1, :] + dc * cw_ref[1:2, :] + dc_back * cw_ref[2:3, :]).astype(BF16)
        dcw_ref[0:1, :] += _colsum(dc * back)
        dcw_ref[1:2, :] += _colsum(dc * hg)
        dcw_ref[2:3, :] += _colsum(dc * fwd)
        dcb_ref[...] += _colsum(dc)

    out_slab = pl.BlockSpec((None, tm, FF_BLK), lambda c, i: (c, i, 0))
    cw_spec = pl.BlockSpec((None, 3, FF_BLK), lambda c, i: (c, 0, 0))
    cb_spec = pl.BlockSpec((None, 1, FF_BLK), lambda c, i: (c, 0, 0))
    return pl.pallas_call(
        body, name=name, grid=(4, nt),
        in_specs=[pl.BlockSpec((tm, D), lambda c, i: (i, 0)),
                  pl.BlockSpec((HALO, D), lambda c, i: (before(i), 0)),
                  pl.BlockSpec((HALO, D), lambda c, i: (after(i), 0)),
                  _shards(2, FF_SHARD, D, li, lambda c, i: c),
                  hv_main, hv_prev, hv_next, hg_main, hg_prev, hg_next, cw_spec, cb_spec],
        out_specs=[out_slab, out_slab, cw_spec, cb_spec],
        out_shape=[jax.ShapeDtypeStruct((4, s, FF_BLK), BF16), jax.ShapeDtypeStruct((4, s, FF_BLK), BF16),
                   jax.ShapeDtypeStruct((4, 3, FF_BLK), F32), jax.ShapeDtypeStruct((4, 1, FF_BLK), F32)],
        compiler_params=_cparams(2))(dzb, dzb, dzb, wdown, up, up, up, up, up, up, conv_w, conv_b)


def dh1_ln1_bwd(dz2, dpg, wpg, dhv, dhg, wup, z1, g1, li, name, after=()):
    s = dz2.shape[0]
    tm = 256
    na = len(after)

    def body(dz2_ref, dpg_ref, wpg_ref, dhv_ref, dhg_ref, wup_ref, z1_ref, g_ref, *rest):
        dz_ref, dzb_ref, dg_ref, db_ref = rest[na:]

        @pl.when(pl.program_id(0) == 0)
        def _():
            dg_ref[...] = jnp.zeros_like(dg_ref)
            db_ref[...] = jnp.zeros_like(db_ref)
        dh = ALPHA * dz2_ref[...] + _nt(dpg_ref[...], _row_cat(wpg_ref))
        for c in range(4):
            dh = dh + _nn(dhv_ref[c], wup_ref[c]) + _nn(dhg_ref[c], wup_ref[4 + c])
        dz, dgx = _ln_bwd(dh, z1_ref[...], g_ref[...])
        dz_ref[...] = dz
        dzb_ref[...] = dz.astype(BF16)
        dg_ref[...] += _colsum(dgx)
        db_ref[...] += _colsum(dh)

    row = pl.BlockSpec((tm, D), lambda i: (i, 0))
    vec = pl.BlockSpec((1, D), lambda i: (0, 0))
    slab = pl.BlockSpec((4, tm, FF_BLK), lambda i: (0, i, 0))
    return pl.pallas_call(
        body, name=name, grid=(s // tm,),
        in_specs=[row, row, _shards(N_DEV, D // N_DEV, D, li), slab, slab, _shards(N_DEV, FF_BLK, D, li), row, vec]
        + [ANY] * na,
        out_specs=[row, row, vec, vec],
        out_shape=[jax.ShapeDtypeStruct((s, D), F32), jax.ShapeDtypeStruct((s, D), BF16),
                   jax.ShapeDtypeStruct((1, D), F32), jax.ShapeDtypeStruct((1, D), F32)],
        compiler_params=_cparams(1))(dz2, dpg, wpg, dhv, dhg, wup, z1, g1, *after)


def merge_bwd(dz1b, wmix, proj, ya, yp, li, name):
    s = dz1b.shape[0]
    tm, tn = 512, 512
    nt = D // tn
    per = tn // (D // N_DEV)
    ga0 = (3 * D_ATTN + D_POOL) // tn

    def body(dz_ref, w_ref, ga_ref, gb_ref, ya_ref, yp_ref, dya_ref, dyp_ref, dga_ref, dgb_ref):
        dm = _nt(dz_ref[...], _row_cat(w_ref))
        sa = _sigmoid(ga_ref[...].astype(F32))
        sb = _sigmoid(gb_ref[...].astype(F32))
        dya_ref[...] = (dm * sa).astype(BF16)
        dyp_ref[...] = (dm * sb).astype(BF16)
        dga_ref[...] = (dm * ya_ref[...].astype(F32) * sa * (1.0 - sa)).astype(BF16)
        dgb_ref[...] = (dm * yp_ref[...].astype(F32) * sb * (1.0 - sb)).astype(BF16)

    tile = pl.BlockSpec((tm, tn), lambda i, j: (i, j))
    return pl.pallas_call(
        body, name=name, grid=(s // tm, nt),
        in_specs=[pl.BlockSpec((tm, D), lambda i, j: (i, 0)),
                  _shards(per, D // N_DEV, D, li, lambda i, j: j),
                  pl.BlockSpec((tm, tn), lambda i, j: (i, ga0 + j)),
                  pl.BlockSpec((tm, tn), lambda i, j: (i, ga0 + nt + j)),
                  tile, tile],
        out_specs=[tile] * 4,
        out_shape=[jax.ShapeDtypeStruct((s, D), BF16)] * 4,
        compiler_params=_cparams(2))(dz1b, wmix, proj, proj, ya, yp)


def attn_out_bwd(dya, wao, li, name, after=()):
    s = dya.shape[0]
    tm = 512

    def body(d_ref, w_ref, *rest):
        rest[-1][...] = _nt(d_ref[...], _lane_cat(w_ref)).astype(BF16)

    return pl.pallas_call(
        body, name=name, grid=(s // tm,),
        in_specs=[pl.BlockSpec((tm, D), lambda i: (i, 0)), _shards(N_DEV, D_ATTN, 128, li)] + [ANY] * len(after),
        out_specs=pl.BlockSpec((tm, D_ATTN), lambda i: (i, 0)),
        out_shape=jax.ShapeDtypeStruct((s, D_ATTN), BF16),
        compiler_params=_cparams(1))(dya, wao, *after)


def pool_bwd(dyp, wpo, pm, pool_w, pool_scale, li, name):
    s = dyp.shape[0]

    def body(dyp_ref, wpo_ref, pm_ref, w_ref, sc_ref, du_ref, dw_ref, dsc_ref):
        wpo = _lane_cat(wpo_ref)
        dyp = dyp_ref[...]
        for g, w in enumerate(POOL_WINDOWS):
            cols = slice(g * PGD, (g + 1) * PGD)
            dpw = _nt(dyp, wpo[g * PGD:(g + 1) * PGD, :])
            pmg = pm_ref[:, cols]
            dsc_ref[:, cols] = _colsum(dpw * _nn(pmg, w_ref[g]))
            dpmw = (dpw * sc_ref[:, cols]).astype(BF16)
            dw_ref[g] = _tn(pmg, dpmw)
            dpm = _nt(dpmw, w_ref[g])
            du_ref[:, cols] = (_window_sum(dpm / _pool_counts(s, w), w, False) - dpm).astype(BF16)

    full = lambda shape: pl.BlockSpec(shape, lambda i: (0,) * len(shape))
    return pl.pallas_call(
        body, name=name, grid=(1,),
        in_specs=[full((s, D)), _shards(N_DEV, D_POOL, 128, li), full((s, D_POOL)), full((4, PGD, PGD)),
                  full((1, D_POOL))],
        out_specs=[full((s, D_POOL)), full((4, PGD, PGD)), full((1, D_POOL))],
        out_shape=[jax.ShapeDtypeStruct((s, D_POOL), BF16), jax.ShapeDtypeStruct((4, PGD, PGD), F32),
                   jax.ShapeDtypeStruct((1, D_POOL), F32)],
        compiler_params=_cparams(1))(dyp, wpo, pm, pool_w, pool_scale)


def attn_bwd(proj, da, e_rev, li, name, after=()):
    s = proj.shape[0]
    nb = s // QB
    skew = GRID_W + (GRID_W - KW)
    group = QROWS
    half_shape = (group * GRID_W, min(group + KH, KROWS) * GRID_W)

    def body(q_ref, k_ref, v_ref, do_ref, e_ref, *rest):
        dq_ref, dk_ref, dv_ref, g_ref, s_ref, dp_ref, ds_ref, p_ref, dkt_acc, dvt_acc = rest[len(after):]
        b = pl.program_id(1)

        @pl.when(b == 0)
        def _():
            dkt_acc[...] = jnp.zeros_like(dkt_acc)
            dvt_acc[...] = jnp.zeros_like(dvt_acc)
            g_ref[...] = jnp.zeros_like(g_ref)

        ri = lax.broadcasted_iota(I32, (QB, QB), 0)
        ci = lax.broadcasted_iota(I32, (QB, QB), 1)
        rev = jnp.where(ri + ci == QB - 1, 1.0, 0.0).astype(BF16)
        q = _nn(rev, q_ref[...]).astype(BF16) * ATT_SCALE
        do = _nn(rev, do_ref[...]).astype(BF16)
        lane = lax.broadcasted_iota(I32, (1, 128), 1)

        def block(btype, k0):
            dqs = []
            for g0 in range(0, QROWS, group):
                h0, hw = _rows_window(btype, QROWS - g0 - group, group)
                hrows = slice(g0 * GRID_W, (g0 + group) * GRID_W)
                kwin = k_ref[pl.ds(_token_at(k0, h0), hw), :]
                vwin = v_ref[pl.ds(_token_at(k0, h0), hw), :]
                dq = jnp.zeros((group * GRID_W, 128), F32)
                for hh in range(2):
                    lm = (lane // HEAD_DIM) == hh
                    qh = jnp.where(lm, q[hrows], jnp.zeros_like(q[hrows]))
                    doh = jnp.where(lm, do[hrows], jnp.zeros_like(do[hrows]))
                    kh = jnp.where(lm, kwin, jnp.zeros_like(kwin))
                    s_ref[:, 0:hw] = _nt(qh, kwin)
                    dp_ref[:, 0:hw] = _nt(doh, vwin)
                    g = jnp.zeros((1, KB), F32)
                    for r in range(group):
                        qr = QROWS - 1 - (g0 + r)
                        rows = slice(r * GRID_W, (r + 1) * GRID_W)
                        sb, a0, w, pad = _row_logits(s_ref, e_ref, hh, rows, btype, qr, h0)
                        p = jnp.exp(sb - jnp.max(sb, axis=1, keepdims=True))
                        p = p * (1.0 / jnp.sum(p, axis=1, keepdims=True))
                        dp = dp_ref[rows, a0:a0 + w]
                        ds = p * (dp - jnp.sum(p * dp, axis=1, keepdims=True))
                        _store_row(ds_ref, rows, a0, w, ds, hw)
                        _store_row(p_ref, rows, a0, w, p, hw)
                        t = jnp.sum(pltpu.roll(ds, w - skew, 1, stride=1, stride_axis=0), axis=0, keepdims=True)
                        t = t[:, :KH * GRID_W] if pad else pltpu.roll(t, GRID_W, 1)
                        i0 = _attn_row(btype, qr)[2]
                        g = g + pltpu.roll(jnp.concatenate([t, jnp.zeros_like(t)], axis=1), i0 * GRID_W, 1)
                    g_ref[hh] += g
                    dsb = ds_ref[:, 0:hw]
                    dq = dq + _nn(dsb, kh) * ATT_SCALE
                    dkt_acc[:, pl.ds(_token_at(k0, h0), hw)] += _tn(qh, dsb)
                    dvt_acc[:, pl.ds(_token_at(k0, h0), hw)] += _tn(doh, p_ref[:, 0:hw])
                dqs.append(dq.astype(BF16))
            dq_ref[...] = _nn(rev, jnp.concatenate(dqs, axis=0)).astype(BF16)

        for btype, (cond, k0) in enumerate(_attn_types(b, nb)):
            pl.when(cond)(lambda btype=btype, k0=k0: block(btype, k0))

        @pl.when(b == nb - 1)
        def _():
            dk_ref[...] = dkt_acc[...].T.astype(BF16)
            dv_ref[...] = dvt_acc[...].T.astype(BF16)

    col = pl.BlockSpec((s, 128), lambda j, b: (0, j))
    return pl.pallas_call(
        body, name=name, grid=(4, nb),
        in_specs=[pl.BlockSpec((QB, 128), lambda j, b: (b, j)),
                  pl.BlockSpec((s, 128), lambda j, b: (0, 4 + j)),
                  pl.BlockSpec((s, 128), lambda j, b: (0, 8 + j)),
                  pl.BlockSpec((QB, 128), lambda j, b: (b, j)),
                  pl.BlockSpec((None, 2, 2, GRID_W, KB), lambda j, b: (li, j, 0, 0, 0))] + [ANY] * len(after),
        out_specs=[pl.BlockSpec((QB, 128), lambda j, b: (b, j)), col, col,
                   pl.BlockSpec((2, 1, KB), lambda j, b: (j, 0, 0))],
        out_shape=[jax.ShapeDtypeStruct((s, D_ATTN), BF16)] * 3 + [jax.ShapeDtypeStruct((N_HEADS, 1, KB), F32)],
        scratch_shapes=[pltpu.VMEM(half_shape, F32), pltpu.VMEM(half_shape, F32), pltpu.VMEM(half_shape, BF16),
                        pltpu.VMEM(half_shape, BF16), pltpu.VMEM((128, s), F32), pltpu.VMEM((128, s), F32)],
        compiler_params=_cparams(2))(proj, proj, proj, da, e_rev, *after)


def dh0_bwd(dz1, pieces, win, li, name, after=()):
    s = dz1.shape[0]
    tm = 256
    bn = N_PROJ // N_DEV

    def body(dz_ref, q_ref, k_ref, v_ref, u_ref, ga_ref, gb_ref, w_ref, *rest):
        acc = ALPHA * dz_ref[...]
        for j, ref in enumerate((q_ref, k_ref, v_ref, u_ref)):
            acc = acc + _nt(ref[...], w_ref[j])
        for j, ref in ((4, ga_ref), (6, gb_ref)):
            acc = acc + _nt(ref[:, 0:bn], w_ref[j]) + _nt(ref[:, bn:2 * bn], w_ref[j + 1])
        rest[-1][...] = acc

    row = pl.BlockSpec((tm, D), lambda i: (i, 0))
    narrow = pl.BlockSpec((tm, bn), lambda i: (i, 0))
    return pl.pallas_call(
        body, name=name, grid=(s // tm,),
        in_specs=[row] + [narrow] * 4 + [row, row, _shards(N_DEV, D, bn, li)] + [ANY] * len(after),
        out_specs=row, out_shape=jax.ShapeDtypeStruct((s, D), F32),
        compiler_params=_cparams(1))(dz1, *pieces, win, *after)


def _coords():
    return lax.axis_index("x"), lax.axis_index("y"), lax.axis_index("c")


def _dev_index(px, py, pc):
    return 4 * px + 2 * py + pc


HBM = pl.BlockSpec(memory_space=pltpu.HBM)
SEM = pl.BlockSpec(memory_space=pltpu.SEMAPHORE)
_EFFECT = pltpu.SideEffectType.DATAFLOW_SIDE_EFFECTING
_TOKEN = jax.ShapeDtypeStruct((8, 128), F32)


def _in_hbm(a):
    return pltpu.with_memory_space_constraint(a, pltpu.HBM)


def _hbm_like(a):
    return pltpu.HBM(a.shape, a.dtype)


def _peers(x, y, c):
    return [(x, y, 1 - c), (1 - x, y, c), (x, 1 - y, c), (1 - x, 1 - y, c)]


def ag_start(lands, after, name):
    n = len(lands)

    def body(*refs):
        land = refs[:n]
        send_sem, recv_sem, token = refs[n + 1], refs[n + 2], refs[-1]
        x, y, c = _coords()
        me = _dev_index(x, y, c)
        for k, peer in enumerate(_peers(x, y, c)):
            for a in range(n):
                pltpu.make_async_remote_copy(src_ref=land[a].at[me], dst_ref=land[a].at[me], send_sem=send_sem.at[k],
                                             recv_sem=recv_sem.at[k], device_id=peer, device_id_type=MESH).start()
        token[...] = jnp.zeros_like(token)

    res = pl.pallas_call(
        body, name=name,
        out_shape=(pltpu.SemaphoreType.DMA((4,)), pltpu.SemaphoreType.DMA((4,)), *[_hbm_like(l) for l in lands], _TOKEN),
        in_specs=[HBM] * n + [ANY], out_specs=(SEM, SEM, *[HBM] * n, pl.BlockSpec(memory_space=pltpu.VMEM)),
        input_output_aliases={a: 2 + a for a in range(n)},
        compiler_params=pltpu.CompilerParams(has_side_effects=_EFFECT),
    )(*[_in_hbm(l) for l in lands], after)
    return res[0], res[1], list(res[2:2 + n]), res[-1]


def ag_forward(send_sem, recv_sem, lands, after, name):
    n = len(lands)

    def body(*refs):
        send_sem, recv_sem = refs[0], refs[1]
        land = refs[2:2 + n]
        fsend, frecv = refs[3 + n], refs[4 + n]
        x, y, c = _coords()
        peers = _peers(x, y, c)
        for k in range(1, 4):
            blk = _dev_index(*peers[k])
            for a in range(n):
                pltpu.make_async_remote_copy(src_ref=land[a].at[blk], dst_ref=land[a].at[blk], send_sem=send_sem.at[k],
                                             recv_sem=recv_sem.at[k], device_id=peers[k], device_id_type=MESH).wait_recv()
        for k in range(1, 4):
            blk = _dev_index(*peers[k])
            for a in range(n):
                pltpu.make_async_remote_copy(src_ref=land[a].at[blk], dst_ref=land[a].at[blk], send_sem=fsend.at[k - 1],
                                             recv_sem=frecv.at[k - 1], device_id=peers[0], device_id_type=MESH).start()

    res = pl.pallas_call(
        body, name=name,
        out_shape=(pltpu.SemaphoreType.DMA((3,)), pltpu.SemaphoreType.DMA((3,)), *[_hbm_like(l) for l in lands]),
        in_specs=[SEM, SEM, *[HBM] * n, ANY], out_specs=(SEM, SEM, *[HBM] * n),
        input_output_aliases={2 + a: 2 + a for a in range(n)},
        compiler_params=pltpu.CompilerParams(has_side_effects=_EFFECT),
    )(send_sem, recv_sem, *lands, after)
    return res[0], res[1], list(res[2:])


def ag_finish(send_sem, recv_sem, fsend, frecv, lands, after, name):
    n = len(lands)

    def body(*refs):
        send_sem, recv_sem, fsend, frecv = refs[:4]
        land = refs[4:4 + n]
        x, y, c = _coords()
        me = _dev_index(x, y, c)
        peers = _peers(x, y, c)
        for k in range(4):
            for a in range(n):
                pltpu.make_async_remote_copy(src_ref=land[a].at[me], dst_ref=land[a].at[me], send_sem=send_sem.at[k],
                                             recv_sem=recv_sem.at[k], device_id=peers[k], device_id_type=MESH).wait_send()
        sib = _dev_index(*peers[0])
        for a in range(n):
            pltpu.make_async_remote_copy(src_ref=land[a].at[sib], dst_ref=land[a].at[sib], send_sem=send_sem.at[0],
                                         recv_sem=recv_sem.at[0], device_id=peers[0], device_id_type=MESH).wait_recv()
        for k in range(1, 4):
            mine = _dev_index(*peers[k])
            theirs = _dev_index(peers[k][0], peers[k][1], 1 - c)
            for a in range(n):
                pltpu.make_async_remote_copy(src_ref=land[a].at[mine], dst_ref=land[a].at[theirs], send_sem=fsend.at[k - 1],
                                             recv_sem=frecv.at[k - 1], device_id=peers[0], device_id_type=MESH).wait()

    res = pl.pallas_call(
        body, name=name, out_shape=tuple(_hbm_like(l) for l in lands),
        in_specs=[SEM] * 4 + [HBM] * n + [ANY], out_specs=tuple([HBM] * n),
        input_output_aliases={4 + a: a for a in range(n)},
        compiler_params=pltpu.CompilerParams(has_side_effects=_EFFECT),
    )(send_sem, recv_sem, fsend, frecv, *lands, after)
    return list(res)


def rs_start(psums, name):
    n = len(psums)
    lands = [lax.empty(p.shape, p.dtype) for p in psums]

    def body(*refs):
        src, land = refs[:n], refs[n:2 * n]
        send_sem, recv_sem, token = refs[2 * n], refs[2 * n + 1], refs[-1]
        peers = _peers(*_coords())
        for k in range(3):
            for a in range(n):
                pltpu.make_async_remote_copy(src_ref=src[a].at[k], dst_ref=land[a].at[k], send_sem=send_sem.at[k],
                                             recv_sem=recv_sem.at[k], device_id=peers[k + 1], device_id_type=MESH).start()
        token[...] = jnp.zeros_like(token)

    res = pl.pallas_call(
        body, name=name,
        out_shape=(pltpu.SemaphoreType.DMA((3,)), pltpu.SemaphoreType.DMA((3,)), *[_hbm_like(p) for p in psums],
                   *[_hbm_like(l) for l in lands], _TOKEN),
        in_specs=[HBM] * (2 * n), out_specs=(SEM, SEM, *[HBM] * (2 * n), pl.BlockSpec(memory_space=pltpu.VMEM)),
        input_output_aliases={a: 2 + a for a in range(2 * n)},
        compiler_params=pltpu.CompilerParams(has_side_effects=_EFFECT),
    )(*[_in_hbm(p) for p in psums], *[_in_hbm(l) for l in lands])
    return res[0], res[1], list(res[2:2 + n]), list(res[2 + n:2 + 2 * n]), res[-1]


def rs_finish(send_sem, recv_sem, psums, lands, after, name):
    n = len(psums)

    def body(*refs):
        send_sem, recv_sem = refs[0], refs[1]
        src, land = refs[2:2 + n], refs[2 + n:2 + 2 * n]
        peers = _peers(*_coords())
        for k in range(3):
            for a in range(n):
                pltpu.make_async_remote_copy(src_ref=src[a].at[k], dst_ref=land[a].at[k], send_sem=send_sem.at[k],
                                             recv_sem=recv_sem.at[k], device_id=peers[k + 1], device_id_type=MESH).wait()

    res = pl.pallas_call(
        body, name=name, out_shape=tuple(_hbm_like(l) for l in lands),
        in_specs=[SEM, SEM] + [HBM] * (2 * n) + [ANY], out_specs=tuple([HBM] * n),
        input_output_aliases={2 + n + a: a for a in range(n)},
        compiler_params=pltpu.CompilerParams(has_side_effects=_EFFECT),
    )(send_sem, recv_sem, *psums, *lands, after)
    return list(res)


def d2d_start(grads, name):
    n = len(grads)
    lands = [lax.empty((4,) + g.shape[1:], g.dtype) for g in grads]

    def body(*refs):
        src, land = refs[:n], refs[n:2 * n]
        send_sem, recv_sem, token = refs[2 * n], refs[2 * n + 1], refs[-1]
        x, y, c = _coords()
        for a in range(n):
            for k in range(4):
                blk = _dev_index(x ^ (k & 1), y ^ (k >> 1), 1 - c)
                pltpu.make_async_remote_copy(src_ref=src[a].at[blk], dst_ref=land[a].at[k], send_sem=send_sem.at[0],
                                             recv_sem=recv_sem.at[0], device_id=(x, y, 1 - c), device_id_type=MESH).start()
        token[...] = jnp.zeros_like(token)

    res = pl.pallas_call(
        body, name=name,
        out_shape=(pltpu.SemaphoreType.DMA((1,)), pltpu.SemaphoreType.DMA((1,)), *[_hbm_like(g) for g in grads],
                   *[_hbm_like(l) for l in lands], _TOKEN),
        in_specs=[HBM] * (2 * n), out_specs=(SEM, SEM, *[HBM] * (2 * n), pl.BlockSpec(memory_space=pltpu.VMEM)),
        input_output_aliases={a: 2 + a for a in range(2 * n)},
        compiler_params=pltpu.CompilerParams(has_side_effects=_EFFECT),
    )(*[_in_hbm(g) for g in grads], *[_in_hbm(l) for l in lands])
    return res[0], res[1], list(res[2:2 + n]), list(res[2 + n:2 + 2 * n]), res[-1]


def d2d_finish(send_sem, recv_sem, grads, lands, after, name):
    n = len(grads)

    def body(*refs):
        send_sem, recv_sem = refs[0], refs[1]
        src, land = refs[2:2 + n], refs[2 + n:2 + 2 * n]
        x, y, c = _coords()
        for a in range(n):
            for k in range(4):
                blk = _dev_index(x ^ (k & 1), y ^ (k >> 1), 1 - c)
                pltpu.make_async_remote_copy(src_ref=src[a].at[blk], dst_ref=land[a].at[k], send_sem=send_sem.at[0],
                                             recv_sem=recv_sem.at[0], device_id=(x, y, 1 - c), device_id_type=MESH).wait()

    res = pl.pallas_call(
        body, name=name, out_shape=tuple(_hbm_like(t) for t in list(grads) + list(lands)),
        in_specs=[SEM, SEM] + [HBM] * (2 * n) + [ANY], out_specs=tuple([HBM] * (2 * n)),
        input_output_aliases={2 + a: a for a in range(2 * n)},
        compiler_params=pltpu.CompilerParams(has_side_effects=_EFFECT),
    )(send_sem, recv_sem, *grads, *lands, after)
    return list(res[:n]), list(res[n:])


def pair_add(blk_idx, g, recv, name):
    _, r, c = g.shape
    tr = _row_tile(r)

    def body(idx_ref, g0, g1, g2, g3, r_ref, own_ref, oth_ref):
        own_ref[...] = g0[...].astype(F32) + r_ref[0].astype(F32)
        for k, gk in enumerate((g1, g2, g3)):
            oth_ref[k] = (gk[...].astype(F32) + r_ref[k + 1].astype(F32)).astype(BF16)

    def blk(k):
        return pl.BlockSpec((None, tr, c), lambda t, idx: (idx[k], t, 0))

    grid_spec = pltpu.PrefetchScalarGridSpec(
        num_scalar_prefetch=1, grid=(r // tr,),
        in_specs=[blk(0), blk(1), blk(2), blk(3), pl.BlockSpec((4, tr, c), lambda t, idx: (0, t, 0))],
        out_specs=[pl.BlockSpec((tr, c), lambda t, idx: (t, 0)), pl.BlockSpec((3, tr, c), lambda t, idx: (0, t, 0))])
    return pl.pallas_call(
        body, name=name, grid_spec=grid_spec,
        out_shape=[jax.ShapeDtypeStruct((r, c), F32), jax.ShapeDtypeStruct((3, r, c), BF16)],
        compiler_params=_cparams(1))(blk_idx, g, g, g, g, recv)


def _row_tile(r):
    return next(t for t in (512, 352, 256, 128) if r % t == 0)


def _adamw(w, g, m, v):
    m = ADAM_B1 * m + (1.0 - ADAM_B1) * g
    v = ADAM_B2 * v + (1.0 - ADAM_B2) * (g * g)
    m_hat = m / (1.0 - ADAM_B1 ** ADAM_STEP)
    v_hat = v / (1.0 - ADAM_B2 ** ADAM_STEP)
    delta = -ADAM_LR * (m_hat / (jnp.sqrt(v_hat) + ADAM_EPS) + ADAM_WD * w)
    return delta, m, v


def adamw_shard(own, recv, w, m, v, li, prev, name):
    r, c = own.shape
    tr = _row_tile(r)

    def body(own_ref, recv_ref, w_ref, m_ref, v_ref, p0, p1, p2, p3, g_ref, d_ref, nm_ref, nv_ref):
        g = own_ref[...] + recv_ref[0].astype(F32) + recv_ref[1].astype(F32) + recv_ref[2].astype(F32)
        delta, nm, nv = _adamw(w_ref[...], g, m_ref[...], v_ref[...])
        g_ref[...] = g
        d_ref[...] = delta
        nm_ref[...] = nm
        nv_ref[...] = nv

    lay = pl.BlockSpec((None, tr, c), lambda t: (li, t, 0))
    stack = jax.ShapeDtypeStruct((DEPTH, r, c), F32)
    return pl.pallas_call(
        body, name=name, grid=(r // tr,),
        in_specs=[pl.BlockSpec((tr, c), lambda t: (t, 0)), pl.BlockSpec((3, tr, c), lambda t: (0, t, 0)),
                  lay, lay, lay, ANY, ANY, ANY, ANY],
        out_specs=[lay] * 4, out_shape=[stack] * 4,
        input_output_aliases={5: 0, 6: 1, 7: 2, 8: 3},
        compiler_params=_cparams(1))(own, recv, w, m, v, *prev)


def sum_partials(gathered, name):
    _, r, c = gathered.shape
    tr = next(t for t in (96, 88, 64, _PACK_TILE) if r % t == 0)

    def body(gs_ref, g_ref):
        g = gs_ref[0]
        for d in range(1, N_DEV):
            g = g + gs_ref[d]
        g_ref[...] = g

    return pl.pallas_call(
        body, name=name, grid=(r // tr,),
        in_specs=[pl.BlockSpec((N_DEV, tr, c), lambda t: (0, t, 0))],
        out_specs=pl.BlockSpec((tr, c), lambda t: (t, 0)), out_shape=jax.ShapeDtypeStruct((r, c), F32),
        compiler_params=_cparams(1))(gathered)


def adamw_plain(g, w, m, v, name):
    def body(g_ref, w_ref, m_ref, v_ref, d_ref, nm_ref, nv_ref):
        delta, nm, nv = _adamw(w_ref[...], g_ref[...], m_ref[...], v_ref[...])
        d_ref[...] = delta
        nm_ref[...] = nm
        nv_ref[...] = nv

    return pl.pallas_call(body, name=name, out_shape=[jax.ShapeDtypeStruct(w.shape, F32)] * 3)(g, w, m, v)


_PACK_LAYER = (("b_in", (N_PROJ,)), ("rpb", (N_HEADS, 2 * KH - 1, 2 * KW - 1)), ("pool_w", (4, PGD, PGD)),
               ("pool_scale", (D_POOL,)), ("ln1_g", (D,)), ("ln1_b", (D,)), ("conv_b", (D_FF,)), ("ln2_g", (D,)),
               ("ln2_b", (D,)), ("conv_w", (3, D_FF)))
_PACK_INPUT = (("ln_in_g", (D,)), ("ln_in_b", (D,)))
_PACK_LANES = 1024
_PACK_TILE = 8
_EARLY = tuple(range(1, DEPTH))


def _pack_items(layers):
    items = [(n, (len(layers),) + s) for n, s in _PACK_LAYER]
    return items + ([(n, s) for n, s in _PACK_INPUT] if 0 in layers else [])


def _pack(parts, layers):
    flats = [(parts[name] if (name, shape) in _PACK_INPUT else jnp.stack([parts[name][li] for li in layers]))
             .reshape(-1).astype(F32) for name, shape in _pack_items(layers)]
    used = sum(f.shape[0] for f in flats)
    tile = _PACK_TILE * _PACK_LANES
    total = -(-used // tile) * tile
    return jnp.concatenate(flats + [jnp.zeros((total - used,), F32)]).reshape(total // _PACK_LANES, _PACK_LANES)


def _unpack(packed, layers):
    flat, out, off = packed.reshape(-1), {}, 0
    for name, shape in _pack_items(layers):
        n = int(np.prod(shape))
        out[name] = flat[off:off + n].reshape(shape)
        off += n
    return out


def _bias_tables(rpb):
    qc = np.arange(GRID_W)[:, None]
    kc = np.arange(GRID_W)[None, :]
    start = np.clip(qc - KW // 2, 0, GRID_W - KW)
    valid = (kc >= start) & (kc < start + KW)
    col = np.clip(kc - qc, -(KW - 1), KW - 1) + KW - 1
    onehot = (col.reshape(-1)[None, :] == np.arange(2 * KW - 1)[:, None]).astype(np.float32)
    depth = rpb.shape[0]
    rows = jnp.pad(rpb, ((0, 0), (0, 0), (0, 1), (0, 0)))
    tab = jnp.einsum("lhij,jm->lhim", rows, jnp.asarray(onehot), precision=lax.Precision.HIGHEST)
    tab = tab.reshape(depth, N_HEADS, KROWS, GRID_W, GRID_W).transpose(0, 1, 3, 2, 4)
    ok = valid[:, None, :] & (np.arange(KROWS) < 2 * KH - 1)[None, :, None]
    tab = jnp.where(jnp.asarray(ok), tab, NEG_INF).reshape(depth, N_HEADS, GRID_W, KB)
    tab = jnp.stack([tab, jnp.roll(tab, GRID_W, axis=-1)], axis=2)
    return tab, tab[:, :, :, ::-1, :]


_SHARDED = ("w_in", "w_attn_out", "w_pool_out", "w_mix_out", "w_up", "w_down", "w_ple_gate", "w_ple_proj")
_NAMES = ("ln_in_g", "ln_in_b", "w_in", "b_in", "rpb", "w_attn_out", "pool_w", "pool_scale", "w_pool_out", "w_mix_out",
          "ln1_g", "ln1_b", "w_up", "conv_w", "conv_b", "w_down", "w_ple_gate", "w_ple_proj", "ln2_g", "ln2_b")


def kernel(x, p, ln_in_g, ln_in_b, w_in, b_in, rpb, w_attn_out, pool_w, pool_scale, w_pool_out, w_mix_out, ln1_g, ln1_b, w_up, conv_w, conv_b, w_down, w_ple_gate, w_ple_proj, ln2_g, ln2_b, loss_target, m_ln_in_g, m_ln_in_b, m_w_in, m_b_in, m_rpb, m_w_attn_out, m_pool_w, m_pool_scale, m_w_pool_out, m_w_mix_out, m_ln1_g, m_ln1_b, m_w_up, m_conv_w, m_conv_b, m_w_down, m_w_ple_gate, m_w_ple_proj, m_ln2_g, m_ln2_b, v_ln_in_g, v_ln_in_b, v_w_in, v_b_in, v_rpb, v_w_attn_out, v_pool_w, v_pool_scale, v_w_pool_out, v_w_mix_out, v_ln1_g, v_ln1_b, v_w_up, v_conv_w, v_conv_b, v_w_down, v_w_ple_gate, v_w_ple_proj, v_ln2_g, v_ln2_b):
    a = dict(locals())
    W = {n: a[n] for n in _NAMES}
    M = {n: a["m_" + n] for n in _NAMES}
    V = {n: a["v_" + n] for n in _NAMES}
    xi, yi, ci = _coords()
    me = _dev_index(xi, yi, ci)
    x2, tgt = x[0], loss_target[0]
    pb = p[:, 0].astype(BF16)

    flip = lambda d: {**d, "w_up": d["w_up"].transpose(0, 2, 1)}
    ex = _Exchange(flip(W), flip(M), flip(V))
    loss_part, dx, parts = _local_step(x2, tgt, pb, W, ex)
    loss = lax.psum(loss_part[0, 0], AXES)

    started = ex.replicated_start("late", _pack(parts, (0,)), dx)
    done = ex.update(range(DEPTH - 1, 0, -1), started)
    ex.replicated_forward("late", done)
    done = ex.update((0,), done)
    stacks = {**ex.stacks, "w_up": [t.transpose(0, 2, 1) for t in ex.stacks["w_up"]]}
    lo, hi = [_unpack(sum_partials(ex.replicated_finish(tag, done), f"sum_replicated_{tag}"), layers)
              for layers, tag in (((0,), "late"), (_EARLY, "early"))]
    grads = {**{n: jnp.concatenate([lo[n], hi[n]]) for n, _ in _PACK_LAYER}, **{n: lo[n] for n, _ in _PACK_INPUT}}
    grads["conv_w"] = lax.dynamic_slice_in_dim(grads["conv_w"], me * FF_SHARD, FF_SHARD, axis=2)
    res = [{n: stacks[n][k] for n in _SHARDED} for k in range(4)]
    for n, g in grads.items():
        two_d = lambda t: t.reshape(-1, t.shape[-1])
        outs = adamw_plain(two_d(g), two_d(W[n]), two_d(M[n]), two_d(V[n]), f"adamw_{n}")
        for d, o in zip(res, [g] + [o.reshape(W[n].shape) for o in outs]):
            d[n] = o
    return (loss, dx[None], *[res[k][n] for k in range(4) for n in _NAMES])


class _Exchange:
    GROUPS = (("w_ple_gate", "w_ple_proj", "w_down", "w_up"), ("w_mix_out", "w_attn_out", "w_pool_out"), ("w_in",))
    FIRST = ("w_in",)

    def __init__(self, W, M, V):
        self.W, self.M, self.V = W, M, V
        xi, yi, ci = _coords()
        me = _dev_index(xi, yi, ci)
        self.me = me.astype(I32).reshape(1)
        self.rel_idx = jnp.stack([_dev_index(xi ^ (k & 1), yi ^ (k >> 1), ci) for k in range(4)]).astype(I32)
        self.lands = [{n: lax.dynamic_update_index_in_dim(lax.empty((N_DEV,) + W[n].shape[1:], BF16),
                                                          W[n][li].astype(BF16), me, 0) for n in _SHARDED}
                      for li in range(DEPTH)]
        cw_land = lax.dynamic_update_index_in_dim(lax.empty((N_DEV,) + W["conv_w"].shape, F32), W["conv_w"], me, 0)
        self.ag, self.fwd, self.rs, self.pending, self.small = {}, {}, {}, {}, {}
        self.stacks = {n: [lax.empty((DEPTH,) + W[n].shape[1:], F32) for _ in range(4)] for n in _SHARDED}
        self.late = tuple(n for n in _SHARDED if n not in self.FIRST)
        self.ag[0] = ag_start([self.lands[0][n] for n in self.FIRST] + [cw_land], W["conv_w"], "ag_start0")

    def tokens(self):
        return [self.ag[0][3]]

    def prefetch(self, li, after):
        send, recv, lands, _ = self.ag[li]
        self.fwd[li] = ag_forward(send, recv, lands, after, f"ag_forward{li}")
        if li == 0:
            self.ag["0b"] = ag_start([self.lands[0][n] for n in self.late], self.fwd[0][2][0], "ag_start0b")

    def weights(self, li, after):
        send, recv, _, _ = self.ag.pop(li)
        fsend, frecv, lands = self.fwd.pop(li)
        lands = ag_finish(send, recv, fsend, frecv, lands, after, f"ag_finish{li}")
        if li == 0:
            self.cw = lands[-1].transpose(1, 2, 0, 3).reshape(DEPTH, 3, 4, FF_BLK).transpose(0, 2, 1, 3)
            return dict(zip(self.FIRST, lands)), self.cw[li], (self.ag["0b"][3],)
        tokens = ()
        if li + 1 < DEPTH:
            self.ag[li + 1] = ag_start([self.lands[li + 1][n] for n in _SHARDED], lands[0], f"ag_start{li + 1}")
            tokens = (self.ag[li + 1][3],)
        return dict(zip(_SHARDED, lands)), self.cw[li], tokens

    def rest(self, li, G, mid, after):
        if li != 0:
            return G, ()
        send, recv, lands, _ = self.ag.pop("0b")
        fsend, frecv, lands = ag_forward(send, recv, lands, mid, "ag_forward0b")
        lands = ag_finish(send, recv, fsend, frecv, lands, after, "ag_finish0b")
        self.ag[1] = ag_start([self.lands[1][n] for n in _SHARDED], lands[0], "ag_start1")
        return {**G, **dict(zip(self.late, lands))}, (self.ag[1][3],)

    def grads(self, li, group, gw):
        self.pending.setdefault(li, {}).update(gw)
        if li != 0 and group != len(self.GROUPS) - 1:
            return None
        gw = self.pending.pop(li)
        tag = f"{li}_{group}" if li == 0 else f"{li}"
        send, recv, glist, lands, token = d2d_start(list(gw.values()), f"d2d_start{tag}")
        self.d2d = (tag, tuple(gw), send, recv, glist, lands)
        return token

    def flush(self, li, group, after):
        if li != 0 and group != len(self.GROUPS) - 1:
            return None
        tag, names, send, recv, glist, lands = self.d2d
        glist, recv1 = d2d_finish(send, recv, glist, lands, after, f"d2d_finish{tag}")
        sums = [pair_add(self.rel_idx, g, r1, f"pair_add_{n}{li}") for n, g, r1 in zip(names, glist, recv1)]
        send, recv, psums, lands, token = rs_start([s_[1] for s_ in sums], f"rs_start{tag}")
        self.rs.setdefault(li, []).append((tag, names, send, recv, psums, lands, [s_[0] for s_ in sums]))
        if li == 0 and group == 1 and "early" in self.small:
            self.replicated_forward("early", token)
        return token

    def update(self, layers, after):
        for li in layers:
            for tag, names, send, recv, psums, lands, owns in self.rs.pop(li):
                recv2 = rs_finish(send, recv, psums, lands, after, f"rs_finish{tag}")
                for n, own, r2 in zip(names, owns, recv2):
                    self.stacks[n] = adamw_shard(own, r2, self.W[n], self.M[n], self.V[n], li, self.stacks[n],
                                                 f"adamw_{n}{li}")
                    after = self.stacks[n][0]
        return after

    def replicated_start(self, tag, pack, after):
        land = lax.dynamic_update_index_in_dim(lax.empty((N_DEV,) + pack.shape, F32), pack, self.me[0], 0)
        self.small[tag] = ag_start([land], after, f"ag_start_small_{tag}")
        return self.small[tag][3]

    def replicated_early(self, small, after):
        return self.replicated_start("early", _pack(small, _EARLY), after)

    def replicated_forward(self, tag, after):
        send, recv, lands, _ = self.small[tag]
        self.small[tag] = (send, recv) + ag_forward(send, recv, lands, after, f"ag_forward_small_{tag}")

    def replicated_finish(self, tag, after):
        send, recv, fsend, frecv, lands = self.small.pop(tag)
        return ag_finish(send, recv, fsend, frecv, lands, after, f"ag_finish_small_{tag}")[0]


def _local_step(x2, tgt, pb, W, ex):
    depth = W["rpb"].shape[0]
    vec = lambda t: t.reshape(1, -1)
    ln1_g, ln1_b, ln2_g, ln2_b = W["ln1_g"], W["ln1_b"], W["ln2_g"], W["ln2_b"]
    b_in, rpb, pool_scale = W["b_in"], W["rpb"], W["pool_scale"]
    cb_full = W["conv_b"].reshape(depth, 4, 1, FF_BLK)
    pool_w_b = W["pool_w"].astype(BF16)
    e_tab, e_rev = _bias_tables(rpb)

    h, hb = ln_fwd(x2, vec(W["ln_in_g"]), vec(W["ln_in_b"]), "ln_in", after=ex.tokens())
    ex.prefetch(0, hb)
    saved = []
    for li in range(depth):
        G, cw, tokens = ex.weights(li, hb)
        bias = vec(b_in[li])
        proj, u = proj_fwd(hb, G["w_in"], bias, li, f"proj{li}", after=tokens)
        att = attn_fwd(proj, e_tab, li, f"attn{li}")
        pm, pw = pool_fwd(u, pool_w_b[li], vec(pool_scale[li]), f"pool{li}")
        G, tokens = ex.rest(li, G, att, pw)
        mg, ya, yp = merge_fwd(att, pw, G["w_attn_out"], G["w_pool_out"], proj, li, f"merge{li}", after=tokens)
        if li + 1 < depth:
            ex.prefetch(li + 1, mg)
        z1, h1, h1b = mix_ln_fwd(mg, G["w_mix_out"], h, vec(ln1_g[li]), vec(ln1_b[li]), li, f"mix_ln{li}")
        up = up_fwd(h1b, G["w_up"], li, f"up{li}")
        t = ffn_act_fwd(up, cw, cb_full[li], f"ffn_act{li}")
        z2, h2, h2b, pg, pp = down_ple_ln_fwd(t, G["w_down"], h1b, G["w_ple_gate"], pb[li], G["w_ple_proj"], h1,
                                              vec(ln2_g[li]), vec(ln2_b[li]), li, f"down_ln{li}")
        saved.append(dict(hb=hb, proj=proj, att=att, pm=pm, pw=pw, mg=mg, ya=ya, yp=yp, z1=z1, h1b=h1b, up=up, t=t,
                          z2=z2, pg=pg, pp=pp, G=G, cw=cw))
        h, hb = h2, h2b

    dh, loss_part = loss_bwd(h, tgt, "loss")
    small = {n: [None] * depth for n in ("b_in", "rpb", "pool_w", "pool_scale", "ln1_g", "ln1_b", "conv_b", "ln2_g",
                                         "ln2_b", "conv_w")}
    token = ()
    tok = lambda t: () if t is None else (t,)
    for li in reversed(range(depth)):
        sv = saved[li]
        G, cw = sv["G"], sv["cw"]
        dz2, dz2b, dpg, dpp, dg2, db2 = ln2_ple_bwd(dh, sv["z2"], vec(ln2_g[li]), sv["pg"], sv["pp"], f"ln2_bwd{li}",
                                                    after=token)
        gw = {}
        gw["w_ple_gate"], gw["w_ple_proj"] = wgrad_pair(sv["h1b"], dpg, [(pb[li], dpp)], f"dw_ple{li}")
        gw["w_down"] = wgrad_down(sv["t"], dz2b, f"dw_down{li}").reshape(N_DEV, FF_SHARD, D)
        dhv, dhg, dcw, dcb = ffn_act_bwd(dz2b, G["w_down"], sv["up"], cw, cb_full[li], li, f"ffn_bwd{li}")
        gw["w_up"] = wgrad_up(sv["h1b"], dhv, dhg, f"dw_up{li}")
        token = tok(ex.grads(li, 0, gw))
        dz1, dz1b, dg1, db1 = dh1_ln1_bwd(dz2, dpg, G["w_ple_gate"], dhv, dhg, G["w_up"], sv["z1"], vec(ln1_g[li]), li,
                                          f"ln1_bwd{li}", after=token)
        token = tok(ex.flush(li, 0, dz1b))
        gw = {"w_mix_out": wgrad_rows(sv["mg"], dz1b, f"dw_mix{li}", after=token)}
        dya, dyp, dga, dgb = merge_bwd(dz1b, G["w_mix_out"], sv["proj"], sv["ya"], sv["yp"], li, f"merge_bwd{li}")
        gw["w_attn_out"], gw["w_pool_out"] = wgrad_pair(None, None, [(sv["att"], dya), (sv["pw"], dyp)], f"dw_out{li}")
        token = tok(ex.grads(li, 1, gw))
        da = attn_out_bwd(dya, G["w_attn_out"], li, f"da{li}", after=token)
        du, dpool_w, dpool_sc = pool_bwd(dyp, G["w_pool_out"], sv["pm"], pool_w_b[li], vec(pool_scale[li]), li,
                                         f"pool_bwd{li}")
        token = tok(ex.flush(li, 1, du))
        dq, dk, dv, drpb = attn_bwd(sv["proj"], da, e_rev, li, f"attn_bwd{li}", after=token)
        dproj = [dq, dk, dv, du, dga, dgb]
        dw_in, db_in = wgrad_in(sv["hb"], dproj, f"dw_in{li}")
        token = tok(ex.grads(li, 2, {"w_in": dw_in}))
        dh = dh0_bwd(dz1, dproj, G["w_in"], li, f"dh0{li}", after=token)
        small["b_in"][li] = db_in.reshape(N_PROJ)
        small["rpb"][li] = drpb.reshape(N_HEADS, KROWS, GRID_W)[:, :2 * KH - 1, :2 * KW - 1]
        small["pool_w"][li] = dpool_w
        small["pool_scale"][li] = dpool_sc.reshape(D_POOL)
        small["ln1_g"][li], small["ln1_b"][li] = dg1.reshape(D), db1.reshape(D)
        small["ln2_g"][li], small["ln2_b"][li] = dg2.reshape(D), db2.reshape(D)
        small["conv_b"][li] = dcb.reshape(D_FF)
        small["conv_w"][li] = dcw.transpose(1, 0, 2).reshape(3, D_FF)
        token = tok(ex.flush(li, 2, dh))
        if li == 1:
            token = token + tok(ex.replicated_early(small, dh))
    dx, dg_in, db_in0 = ln_bwd(dh, x2, vec(W["ln_in_g"]), "ln_in_bwd", after=token)
    parts = {n: jnp.stack(v_) for n, v_ in small.items()}
    parts["ln_in_g"], parts["ln_in_b"] = dg_in.reshape(D), db_in0.reshape(D)
    return loss_part, dx, parts
```

```python
import numpy as np
import jax
import jax.numpy as jnp
from jax import lax
from jax.experimental import pallas as pl
from jax.experimental.pallas import tpu as pltpu

F32 = jnp.float32
BF16 = jnp.bfloat16
I32 = jnp.int32

D = 1024
DEPTH = 4
GRID_W = 64
N_HEADS = 8
HEAD_DIM = 64
D_ATTN = 512
KH = 8
KW = 16
POOL_WINDOWS = (2, 4, 8, 16)
D_POOL = 512
PGD = 128
D_FF = 2816
PLE_DIM = 256
N_PROJ = 4096
ALPHA = (2 * DEPTH) ** 0.25
LN_EPS = 1e-5
NEG_INF = -1e30
ATT_SCALE = HEAD_DIM ** -0.5
ADAM_LR = 0.001
ADAM_B1 = 0.9
ADAM_B2 = 0.999
ADAM_EPS = 1e-08
ADAM_WD = 0.01
ADAM_STEP = 10

N_DEV = 8
AXES = ("x", "y", "c")
FF_BLK = D_FF // 4
FF_SHARD = D_FF // N_DEV
QROWS = 8
KROWS = 16
QB = QROWS * GRID_W
KB = KROWS * GRID_W
V7X_VMEM_LIMIT = 56 * 2 ** 20
MESH = pl.DeviceIdType.MESH
ANY = pl.BlockSpec(memory_space=pl.ANY)


def _cparams(n_grid):
    return pltpu.CompilerParams(dimension_semantics=("arbitrary",) * n_grid, vmem_limit_bytes=V7X_VMEM_LIMIT)


def _nn(a, b):
    return lax.dot_general(a, b, (((1,), (0,)), ((), ())), preferred_element_type=F32)


def _nt(a, b):
    return lax.dot_general(a, b, (((1,), (1,)), ((), ())), preferred_element_type=F32)


def _tn(a, b):
    return lax.dot_general(a, b, (((0,), (0,)), ((), ())), preferred_element_type=F32)


def _sigmoid(x):
    return 1.0 / (1.0 + jnp.exp(-x))


def _ln_fwd(z, g, b):
    mu = jnp.mean(z, axis=-1, keepdims=True)
    xc = z - mu
    var = jnp.mean(xc * xc, axis=-1, keepdims=True)
    return xc * lax.rsqrt(var + LN_EPS) * g + b


def _ln_bwd(dh, z, g):
    mu = jnp.mean(z, axis=-1, keepdims=True)
    xc = z - mu
    var = jnp.mean(xc * xc, axis=-1, keepdims=True)
    rstd = lax.rsqrt(var + LN_EPS)
    xhat = xc * rstd
    dxh = dh * g
    m1 = jnp.mean(dxh, axis=-1, keepdims=True)
    m2 = jnp.mean(dxh * xhat, axis=-1, keepdims=True)
    return rstd * (dxh - m1 - xhat * m2), dh * xhat


def _colsum(x):
    return jnp.sum(x, axis=0, keepdims=True)


def _lane_cat(ref):
    return jnp.concatenate([ref[j] for j in range(ref.shape[0])], axis=1)


def _row_cat(ref):
    n, r, c = ref.shape
    return ref[...].reshape(n * r, c)


def _shards(n, r, c, li, j_of=None):
    del li
    if j_of is None:
        return pl.BlockSpec((n, r, c), lambda *_: (0, 0, 0))
    return pl.BlockSpec((n, r, c), lambda *g: (j_of(*g), 0, 0))


def _shard(r, c, li, j_of):
    del li
    return pl.BlockSpec((None, r, c), lambda *g: (j_of(*g), 0, 0))


def ln_fwd(x, g, b, name, after=()):
    s = x.shape[0]
    tm = 512
    na = len(after)

    def body(x_ref, g_ref, b_ref, *rest):
        h_ref, hb_ref = rest[na:]
        h = _ln_fwd(x_ref[...], g_ref[...], b_ref[...])
        h_ref[...] = h
        hb_ref[...] = h.astype(BF16)

    row = pl.BlockSpec((tm, D), lambda i: (i, 0))
    vec = pl.BlockSpec((1, D), lambda i: (0, 0))
    return pl.pallas_call(
        body, name=name, grid=(s // tm,), in_specs=[row, vec, vec] + [ANY] * na, out_specs=[row, row],
        out_shape=[jax.ShapeDtypeStruct((s, D), F32), jax.ShapeDtypeStruct((s, D), BF16)],
        compiler_params=_cparams(1))(x, g, b, *after)


def proj_fwd(hb, win, bias, li, name, after=()):
    s = hb.shape[0]
    bn = N_PROJ // N_DEV
    tm = 1024
    pool_shard = (3 * D_ATTN) // bn

    def body(a_ref, w_ref, b_ref, *rest):
        o_ref, u_ref = rest[-2:]
        acc = _nn(a_ref[...], w_ref[...]) + b_ref[...]
        o_ref[...] = acc.astype(BF16)

        @pl.when(pl.program_id(1) == pool_shard)
        def _():
            u_ref[...] = acc

    return pl.pallas_call(
        body, name=name, grid=(s // tm, N_DEV),
        in_specs=[pl.BlockSpec((tm, D), lambda i, j: (i, 0)),
                  _shard(D, bn, li, lambda i, j: j),
                  pl.BlockSpec((1, bn), lambda i, j: (0, j))] + [ANY] * len(after),
        out_specs=[pl.BlockSpec((tm, bn), lambda i, j: (i, j)), pl.BlockSpec((tm, bn), lambda i, j: (i, 0))],
        out_shape=[jax.ShapeDtypeStruct((s, N_PROJ), BF16), jax.ShapeDtypeStruct((s, D_POOL), F32)],
        compiler_params=_cparams(2))(hb, win, bias, *after)


def _attn_types(b, nb):
    first, last = 0, (nb * QROWS - KROWS) * GRID_W
    mid = pl.multiple_of((QROWS * b - KH // 2) * GRID_W, 256)
    return ((b == 0, first), ((b > 0) & (b < nb - 1), mid), (b == nb - 1, last))


def _attn_row(btype, qr):
    lo, delta = ((max(qr - KH // 2, 0), 0), (qr, -(KH // 2)), (min(qr + KH // 2, KH), -KH))[btype]
    return lo, (qr - delta - (KH - 1)) % KROWS, lo - qr + delta + KH - 1


def _row_window(lo):
    pad = (lo % 2) * GRID_W
    return (lo // 2) * 128, KH * GRID_W + 2 * pad, pad


def _lanes(ref, start, width):
    start %= KB
    if start + width <= KB:
        return ref[:, start:start + width]
    return jnp.concatenate([ref[:, start:], ref[:, :start + width - KB]], axis=1)


HALF = QROWS // 2


def _rows_window(btype, qr0, n):
    spans = [_row_window(_attn_row(btype, qr)[0]) for qr in range(qr0, qr0 + n)]
    h0 = min(a0 for a0, _, _ in spans) // 256 * 256
    h1 = -(-max(a0 + w for a0, w, _ in spans) // 256) * 256
    return h0, h1 - h0


def _token_at(k0, h0):
    return k0 + h0 if isinstance(k0, int) else pl.multiple_of(k0 + h0, 256)


def _row_logits(s_ref, e_ref, hh, rows, btype, qr, h0):
    lo, shift, _ = _attn_row(btype, qr)
    a0, w, pad = _row_window(lo)
    e = e_ref.at[hh, shift % 2]
    sb = s_ref[rows, a0 - h0:a0 - h0 + w] + _lanes(e, a0 - (shift - shift % 2) * GRID_W, w)
    if pad:
        lane = lax.broadcasted_iota(I32, (1, w), 1)
        sb = jnp.where((lane >= pad) & (lane < w - pad), sb, NEG_INF)
    return sb, a0 - h0, w, pad


def _store_row(ref, rows, a0, w, val, width):
    if a0:
        ref[rows, 0:a0] = jnp.zeros((GRID_W, a0), ref.dtype)
    ref[rows, a0:a0 + w] = val.astype(ref.dtype)
    if a0 + w < width:
        ref[rows, a0 + w:width] = jnp.zeros((GRID_W, width - a0 - w), ref.dtype)


def attn_fwd(proj, e_tab, li, name):
    s = proj.shape[0]
    nb = s // QB

    def body(q_ref, k_ref, v_ref, e_ref, o_ref, s_ref, p_ref):
        q = q_ref[...] * ATT_SCALE
        lane = lax.broadcasted_iota(I32, (1, 128), 1)

        def block(btype, k0):
            for half in range(2):
                h0, hw = _rows_window(btype, half * HALF, HALF)
                hrows = slice(half * HALF * GRID_W, (half + 1) * HALF * GRID_W)
                kwin = k_ref[pl.ds(_token_at(k0, h0), hw), :]
                vwin = v_ref[pl.ds(_token_at(k0, h0), hw), :]
                acc = jnp.zeros((HALF * GRID_W, 128), F32)
                for hh in range(2):
                    lm = (lane // HEAD_DIM) == hh
                    qh = jnp.where(lm, q[hrows], jnp.zeros_like(q[hrows]))
                    vh = jnp.where(lm, vwin, jnp.zeros_like(vwin))
                    s_ref[:, 0:hw] = _nt(qh, kwin)
                    for r in range(HALF):
                        rows = slice(r * GRID_W, (r + 1) * GRID_W)
                        sb, a0, w, _ = _row_logits(s_ref, e_ref, hh, rows, btype, half * HALF + r, h0)
                        p = jnp.exp(sb - jnp.max(sb, axis=1, keepdims=True))
                        _store_row(p_ref, rows, a0, w, p * (1.0 / jnp.sum(p, axis=1, keepdims=True)), hw)
                    acc = acc + _nn(p_ref[:, 0:hw], vh)
                o_ref[hrows, :] = acc.astype(BF16)

        for btype, (cond, k0) in enumerate(_attn_types(pl.program_id(1), nb)):
            pl.when(cond)(lambda btype=btype, k0=k0: block(btype, k0))

    half_shape = (HALF * GRID_W, (HALF + KH - 1 + 1) * GRID_W)
    return pl.pallas_call(
        body, name=name, grid=(4, nb),
        in_specs=[pl.BlockSpec((QB, 128), lambda j, b: (b, j)),
                  pl.BlockSpec((s, 128), lambda j, b: (0, 4 + j)),
                  pl.BlockSpec((s, 128), lambda j, b: (0, 8 + j)),
                  pl.BlockSpec((None, 2, 2, GRID_W, KB), lambda j, b: (li, j, 0, 0, 0))],
        out_specs=pl.BlockSpec((QB, 128), lambda j, b: (b, j)),
        out_shape=jax.ShapeDtypeStruct((s, D_ATTN), BF16),
        scratch_shapes=[pltpu.VMEM(half_shape, F32), pltpu.VMEM(half_shape, BF16)],
        compiler_params=_cparams(2))(proj, proj, proj, e_tab)


_POOL_PAD = 8


def _pool_counts(s, w):
    t = lax.broadcasted_iota(I32, (s, 1), 0)
    return (jnp.minimum(t + w // 2, s) - jnp.maximum(t - w // 2, 0)).astype(F32)


def _window_sum(x, w, back_first):
    s = x.shape[0]
    z = jnp.zeros((_POOL_PAD, x.shape[1]), F32)
    xe = jnp.concatenate([z, x, z], axis=0)
    n = s + 2 * _POOL_PAD
    acc = xe + pltpu.roll(xe, 1 if back_first else n - 1, 0)
    k = 1
    while 2 * k < w:
        acc = pltpu.roll(acc, k, 0) + pltpu.roll(acc, n - k, 0)
        k *= 2
    return acc[_POOL_PAD:_POOL_PAD + s, :]


def pool_fwd(u, pool_w, pool_scale, name):
    s = u.shape[0]

    def body(u_ref, w_ref, sc_ref, pm_ref, pw_ref):
        for g, w in enumerate(POOL_WINDOWS):
            cols = slice(g * PGD, (g + 1) * PGD)
            ug = u_ref[:, cols]
            pm = (_window_sum(ug, w, True) / _pool_counts(s, w) - ug).astype(BF16)
            pm_ref[:, cols] = pm
            pw_ref[:, cols] = (_nn(pm, w_ref[g]) * sc_ref[:, cols]).astype(BF16)

    full = lambda shape: pl.BlockSpec(shape, lambda i: (0,) * len(shape))
    return pl.pallas_call(
        body, name=name, grid=(1,),
        in_specs=[full((s, D_POOL)), full((4, PGD, PGD)), full((1, D_POOL))],
        out_specs=[full((s, D_POOL)), full((s, D_POOL))],
        out_shape=[jax.ShapeDtypeStruct((s, D_POOL), BF16)] * 2,
        compiler_params=_cparams(1))(u, pool_w, pool_scale)


def merge_fwd(a, pw, wao, wpo, proj, li, name, after=()):
    s = a.shape[0]
    tm, tn = 512, 512
    nt = D // tn
    per = tn // 128

    def body(a_ref, pw_ref, wa_ref, wp_ref, ga_ref, gb_ref, *rest):
        mg_ref, ya_ref, yp_ref = rest[len(after):]
        ya = _nn(a_ref[...], _lane_cat(wa_ref))
        yp = _nn(pw_ref[...], _lane_cat(wp_ref))
        mg = _sigmoid(ga_ref[...].astype(F32)) * ya + _sigmoid(gb_ref[...].astype(F32)) * yp
        mg_ref[...] = mg.astype(BF16)
        ya_ref[...] = ya.astype(BF16)
        yp_ref[...] = yp.astype(BF16)

    act = pl.BlockSpec((tm, D_ATTN), lambda i, j: (i, 0))
    wsp = _shards(per, D_ATTN, 128, li, lambda i, j: j)
    out = pl.BlockSpec((tm, tn), lambda i, j: (i, j))
    ga0 = (3 * D_ATTN + D_POOL) // tn
    return pl.pallas_call(
        body, name=name, grid=(s // tm, nt),
        in_specs=[act, act, wsp, wsp,
                  pl.BlockSpec((tm, tn), lambda i, j: (i, ga0 + j)),
                  pl.BlockSpec((tm, tn), lambda i, j: (i, ga0 + nt + j))] + [ANY] * len(after),
        out_specs=[out, out, out],
        out_shape=[jax.ShapeDtypeStruct((s, D), BF16)] * 3,
        compiler_params=_cparams(2))(a, pw, wao, wpo, proj, proj, *after)


def mix_ln_fwd(mg, wmix, h0, g, b, li, name):
    s = mg.shape[0]
    tm = 256

    def body(mg_ref, w_ref, h0_ref, g_ref, b_ref, z_ref, h_ref, hb_ref):
        z = ALPHA * h0_ref[...] + _nn(mg_ref[...], _row_cat(w_ref))
        h = _ln_fwd(z, g_ref[...], b_ref[...])
        z_ref[...] = z
        h_ref[...] = h
        hb_ref[...] = h.astype(BF16)

    row = pl.BlockSpec((tm, D), lambda i: (i, 0))
    vec = pl.BlockSpec((1, D), lambda i: (0, 0))
    return pl.pallas_call(
        body, name=name, grid=(s // tm,),
        in_specs=[row, _shards(N_DEV, D // N_DEV, D, li), row, vec, vec],
        out_specs=[row, row, row],
        out_shape=[jax.ShapeDtypeStruct((s, D), F32), jax.ShapeDtypeStruct((s, D), F32),
                   jax.ShapeDtypeStruct((s, D), BF16)],
        compiler_params=_cparams(1))(mg, wmix, h0, g, b)


def up_fwd(hb, wup, li, name):
    s = hb.shape[0]
    tm = 1024

    def body(a_ref, w_ref, o_ref):
        o_ref[...] = _nt(a_ref[...], w_ref[...]).astype(BF16)

    return pl.pallas_call(
        body, name=name, grid=(s // tm, N_DEV),
        in_specs=[pl.BlockSpec((tm, D), lambda i, j: (i, 0)), _shard(FF_BLK, D, li, lambda i, j: j)],
        out_specs=pl.BlockSpec((None, tm, FF_BLK), lambda i, j: (j, i, 0)),
        out_shape=jax.ShapeDtypeStruct((N_DEV, s, FF_BLK), BF16),
        compiler_params=_cparams(2))(hb, wup)


_SQRT_HALF = 0.7071067811865476
_INV_SQRT_2PI = 0.3989422804014327


def _shift_rows(x, prev_row, next_row):
    n = x.shape[0]
    r = lax.broadcasted_iota(I32, (n, 1), 0)
    back = jnp.where(r == 0, prev_row, pltpu.roll(x, 1, 0))
    fwd = jnp.where(r == n - 1, next_row, pltpu.roll(x, n - 1, 0))
    return back, fwd


HALO = 16


def _halo_maps(tm, s):
    th = tm // HALO
    return (lambda i: jnp.maximum(i * th - 1, 0)), (lambda i: jnp.minimum((i + 1) * th, s // HALO - 1))


def _slab_specs(tm, s, blk_of):
    before, after = _halo_maps(tm, s)
    main = pl.BlockSpec((None, tm, FF_BLK), lambda c, i: (blk_of(c), i, 0))
    prev = pl.BlockSpec((None, HALO, FF_BLK), lambda c, i: (blk_of(c), before(i), 0))
    nxt = pl.BlockSpec((None, HALO, FF_BLK), lambda c, i: (blk_of(c), after(i), 0))
    return main, prev, nxt


def ffn_act_fwd(up, conv_w, conv_b, name):
    s = up.shape[1]
    tm = 512
    nt = s // tm
    hv_main, _, _ = _slab_specs(tm, s, lambda c: c)
    hg_main, hg_prev, hg_next = _slab_specs(tm, s, lambda c: 4 + c)

    def body(hv_ref, hg_ref, hp_ref, hn_ref, cw_ref, cb_ref, t_ref):
        i = pl.program_id(1)
        hg = hg_ref[...].astype(F32)
        prow = jnp.where(i == 0, 0.0, hp_ref[...].astype(F32)[HALO - 1:HALO, :])
        nrow = jnp.where(i == nt - 1, 0.0, hn_ref[...].astype(F32)[0:1, :])
        back, fwd = _shift_rows(hg, prow, nrow)
        c = back * cw_ref[0:1, :] + hg * cw_ref[1:2, :] + fwd * cw_ref[2:3, :] + cb_ref[...]
        act = 0.5 * c * (1.0 + lax.erf(c * _SQRT_HALF))
        t_ref[...] = (act * hv_ref[...].astype(F32)).astype(BF16)

    return pl.pallas_call(
        body, name=name, grid=(4, nt),
        in_specs=[hv_main, hg_main, hg_prev, hg_next,
                  pl.BlockSpec((None, 3, FF_BLK), lambda c, i: (c, 0, 0)),
                  pl.BlockSpec((None, 1, FF_BLK), lambda c, i: (c, 0, 0))],
        out_specs=pl.BlockSpec((None, tm, FF_BLK), lambda c, i: (c, i, 0)),
        out_shape=jax.ShapeDtypeStruct((4, s, FF_BLK), BF16),
        compiler_params=_cparams(2))(up, up, up, up, conv_w, conv_b)


def down_ple_ln_fwd(t, wdown, hb, wpg, pb, wpp, h1, g, b, li, name):
    s = hb.shape[0]
    tm = 256

    def body(t_ref, wd_ref, hb_ref, wpg_ref, p_ref, wpp_ref, h1_ref, g_ref, b_ref,
             z_ref, h_ref, hbo_ref, pg_ref, pp_ref):
        wd = _row_cat(wd_ref)
        ffn = _nn(t_ref[0], wd[0:FF_BLK, :])
        for c in range(1, 4):
            ffn = ffn + _nn(t_ref[c], wd[c * FF_BLK:(c + 1) * FF_BLK, :])
        pg = _nn(hb_ref[...], _row_cat(wpg_ref))
        pp = _nn(p_ref[...], _lane_cat(wpp_ref))
        z = ALPHA * h1_ref[...] + ffn + _sigmoid(pg) * pp
        h = _ln_fwd(z, g_ref[...], b_ref[...])
        z_ref[...] = z
        h_ref[...] = h
        hbo_ref[...] = h.astype(BF16)
        pg_ref[...] = pg.astype(BF16)
        pp_ref[...] = pp.astype(BF16)

    row = pl.BlockSpec((tm, D), lambda i: (i, 0))
    vec = pl.BlockSpec((1, D), lambda i: (0, 0))
    return pl.pallas_call(
        body, name=name, grid=(s // tm,),
        in_specs=[pl.BlockSpec((4, tm, FF_BLK), lambda i: (0, i, 0)),
                  _shards(N_DEV, FF_SHARD, D, li),
                  row, _shards(N_DEV, D // N_DEV, D, li),
                  pl.BlockSpec((tm, PLE_DIM), lambda i: (i, 0)),
                  _shards(N_DEV, PLE_DIM, 128, li),
                  row, vec, vec],
        out_specs=[row] * 5,
        out_shape=[jax.ShapeDtypeStruct((s, D), F32), jax.ShapeDtypeStruct((s, D), F32),
                   jax.ShapeDtypeStruct((s, D), BF16), jax.ShapeDtypeStruct((s, D), BF16),
                   jax.ShapeDtypeStruct((s, D), BF16)],
        compiler_params=_cparams(1))(t, wdown, hb, wpg, pb, wpp, h1, g, b)


def loss_bwd(h, target, name):
    s = h.shape[0]
    tm = 512

    def body(h_ref, t_ref, dh_ref, l_ref):
        @pl.when(pl.program_id(0) == 0)
        def _():
            l_ref[...] = jnp.zeros_like(l_ref)
        e = h_ref[...] - t_ref[...]
        dh_ref[...] = e * (1.0 / D)
        l_ref[...] += 0.5 * jnp.sum(jnp.mean(e * e, axis=-1, keepdims=True), axis=0, keepdims=True)

    row = pl.BlockSpec((tm, D), lambda i: (i, 0))
    return pl.pallas_call(
        body, name=name, grid=(s // tm,), in_specs=[row, row],
        out_specs=[row, pl.BlockSpec((1, 1), lambda i: (0, 0))],
        out_shape=[jax.ShapeDtypeStruct((s, D), F32), jax.ShapeDtypeStruct((1, 1), F32)],
        compiler_params=_cparams(1))(h, target)


def ln_bwd(dh, z, g, name, after=()):
    s = dh.shape[0]
    tm = 512
    na = len(after)

    def body(dh_ref, z_ref, g_ref, *rest):
        dz_ref, dg_ref, db_ref = rest[na:]

        @pl.when(pl.program_id(0) == 0)
        def _():
            dg_ref[...] = jnp.zeros_like(dg_ref)
            db_ref[...] = jnp.zeros_like(db_ref)
        dh = dh_ref[...]
        dz, dgx = _ln_bwd(dh, z_ref[...], g_ref[...])
        dz_ref[...] = dz
        dg_ref[...] += _colsum(dgx)
        db_ref[...] += _colsum(dh)

    row = pl.BlockSpec((tm, D), lambda i: (i, 0))
    vec = pl.BlockSpec((1, D), lambda i: (0, 0))
    return pl.pallas_call(
        body, name=name, grid=(s // tm,), in_specs=[row, row, vec] + [ANY] * na, out_specs=[row, vec, vec],
        out_shape=[jax.ShapeDtypeStruct((s, D), F32), jax.ShapeDtypeStruct((1, D), F32),
                   jax.ShapeDtypeStruct((1, D), F32)],
        compiler_params=_cparams(1))(dh, z, g, *after)


def ln2_ple_bwd(dh, z, g, pg, pp, name, after=()):
    s = dh.shape[0]
    tm = 512
    na = len(after)

    def body(dh_ref, z_ref, g_ref, pg_ref, pp_ref, *rest):
        dz_ref, dzb_ref, dpg_ref, dpp_ref, dg_ref, db_ref = rest[na:]

        @pl.when(pl.program_id(0) == 0)
        def _():
            dg_ref[...] = jnp.zeros_like(dg_ref)
            db_ref[...] = jnp.zeros_like(db_ref)
        dh = dh_ref[...]
        dz, dgx = _ln_bwd(dh, z_ref[...], g_ref[...])
        sg = _sigmoid(pg_ref[...].astype(F32))
        dz_ref[...] = dz
        dzb_ref[...] = dz.astype(BF16)
        dpg_ref[...] = (dz * pp_ref[...].astype(F32) * sg * (1.0 - sg)).astype(BF16)
        dpp_ref[...] = (dz * sg).astype(BF16)
        dg_ref[...] += _colsum(dgx)
        db_ref[...] += _colsum(dh)

    row = pl.BlockSpec((tm, D), lambda i: (i, 0))
    vec = pl.BlockSpec((1, D), lambda i: (0, 0))
    return pl.pallas_call(
        body, name=name, grid=(s // tm,), in_specs=[row, row, vec, row, row] + [ANY] * na,
        out_specs=[row, row, row, row, vec, vec],
        out_shape=[jax.ShapeDtypeStruct((s, D), F32)] + [jax.ShapeDtypeStruct((s, D), BF16)] * 3
        + [jax.ShapeDtypeStruct((1, D), F32)] * 2,
        compiler_params=_cparams(1))(dh, z, g, pg, pp, *after)


def wgrad_pair(rows_a, rows_dy, cols, name):
    specs, args, outs, shapes, kinds = [], [], [], [], []
    if rows_a is not None:
        s, k = rows_a.shape
        n = rows_dy.shape[1]
        specs += [pl.BlockSpec((s, k // N_DEV), lambda j: (0, j)), pl.BlockSpec((s, n), lambda j: (0, 0))]
        args += [rows_a, rows_dy]
        outs.append(pl.BlockSpec((None, k // N_DEV, n), lambda j: (j, 0, 0)))
        shapes.append(jax.ShapeDtypeStruct((N_DEV, k // N_DEV, n), BF16))
    for a, dy in cols:
        s, k = a.shape
        n = dy.shape[1]
        specs += [pl.BlockSpec((s, k), lambda j: (0, 0)), pl.BlockSpec((s, n // N_DEV), lambda j: (0, j))]
        args += [a, dy]
        outs.append(pl.BlockSpec((None, k, n // N_DEV), lambda j: (j, 0, 0)))
        shapes.append(jax.ShapeDtypeStruct((N_DEV, k, n // N_DEV), BF16))
    n_pairs = len(shapes)

    def body(*refs):
        for i in range(n_pairs):
            refs[2 * n_pairs + i][...] = _tn(refs[2 * i][...], refs[2 * i + 1][...]).astype(BF16)

    return pl.pallas_call(body, name=name, grid=(N_DEV,), in_specs=specs, out_specs=outs, out_shape=shapes,
                          compiler_params=_cparams(1))(*args)


def wgrad_in(a, pieces, name):
    s, k = a.shape
    bn = N_PROJ // N_DEV
    n_narrow = 4

    def body(a_ref, *refs):
        dy_refs, (o_ref, cs_ref) = refs[:6], refs[6:]
        j = pl.program_id(0)

        def emit(dy_ref):
            dy = dy_ref[...]
            o_ref[...] = _tn(a_ref[...], dy).astype(BF16)
            cs_ref[...] = _colsum(dy.astype(F32))

        for idx in range(n_narrow):
            pl.when(j == idx)(lambda idx=idx: emit(dy_refs[idx]))
        pl.when((j >= n_narrow) & (j < n_narrow + 2))(lambda: emit(dy_refs[4]))
        pl.when(j >= n_narrow + 2)(lambda: emit(dy_refs[5]))

    narrow = pl.BlockSpec((s, bn), lambda j: (0, 0))
    return pl.pallas_call(
        body, name=name, grid=(N_DEV,),
        in_specs=[pl.BlockSpec((s, k), lambda j: (0, 0))] + [narrow] * n_narrow
        + [pl.BlockSpec((s, bn), lambda j: (0, jnp.clip(j - n_narrow, 0, 1))),
           pl.BlockSpec((s, bn), lambda j: (0, jnp.clip(j - n_narrow - 2, 0, 1)))],
        out_specs=[pl.BlockSpec((None, k, bn), lambda j: (j, 0, 0)), pl.BlockSpec((1, bn), lambda j: (0, j))],
        out_shape=[jax.ShapeDtypeStruct((N_DEV, k, bn), BF16), jax.ShapeDtypeStruct((1, N_PROJ), F32)],
        compiler_params=_cparams(1))(a, *pieces)


def wgrad_down(t, dy, name):
    _, s, k = t.shape
    n = dy.shape[1]

    def body(a_ref, dy_ref, o_ref):
        o_ref[...] = _tn(a_ref[...], dy_ref[...]).astype(BF16)

    return pl.pallas_call(
        body, name=name, grid=(4,),
        in_specs=[pl.BlockSpec((None, s, k), lambda j: (j, 0, 0)), pl.BlockSpec((s, n), lambda j: (0, 0))],
        out_specs=pl.BlockSpec((None, k, n), lambda j: (j, 0, 0)),
        out_shape=jax.ShapeDtypeStruct((4, k, n), BF16),
        compiler_params=_cparams(1))(t, dy)


def wgrad_up(a, dhv, dhg, name):
    s, k = a.shape

    def body(a_ref, dv_ref, dg_ref, o_ref):
        j = pl.program_id(0)

        @pl.when(j < 4)
        def _():
            o_ref[...] = _tn(dv_ref[...], a_ref[...]).astype(BF16)

        @pl.when(j >= 4)
        def _():
            o_ref[...] = _tn(dg_ref[...], a_ref[...]).astype(BF16)

    return pl.pallas_call(
        body, name=name, grid=(N_DEV,),
        in_specs=[pl.BlockSpec((s, k), lambda j: (0, 0)),
                  pl.BlockSpec((None, s, FF_BLK), lambda j: (jnp.minimum(j, 3), 0, 0)),
                  pl.BlockSpec((None, s, FF_BLK), lambda j: (jnp.maximum(j - 4, 0), 0, 0))],
        out_specs=pl.BlockSpec((None, FF_BLK, k), lambda j: (j, 0, 0)),
        out_shape=jax.ShapeDtypeStruct((N_DEV, FF_BLK, k), BF16),
        compiler_params=_cparams(1))(a, dhv, dhg)


def ffn_act_bwd(dzb, wdown, up, conv_w, conv_b, li, name):
    s = up.shape[1]
    tm = 512
    nt = s // tm
    before, after = _halo_maps(tm, s)
    hv_main, hv_prev, hv_next = _slab_specs(tm, s, lambda c: c)
    hg_main, hg_prev, hg_next = _slab_specs(tm, s, lambda c: 4 + c)

    def dc_of(dz, wd, hv, hg, back, fwd, cw_ref, cb_ref):
        dt = _nt(dz, wd)
        c = back * cw_ref[0:1, :] + hg * cw_ref[1:2, :] + fwd * cw_ref[2:3, :] + cb_ref[...]
        cdf = 0.5 * (1.0 + lax.erf(c * _SQRT_HALF))
        pdf = jnp.exp(-0.5 * c * c) * _INV_SQRT_2PI
        return dt, c * cdf, dt * hv * (cdf + c * pdf)

    def body(dz_ref, dzp_ref, dzn_ref, wd_ref, hv_ref, hvp_ref, hvn_ref, hg_ref, hgp_ref, hgn_ref, cw_ref, cb_ref,
             dhv_ref, dhg_ref, dcw_ref, dcb_ref):
        i = pl.program_id(1)

        @pl.when(i == 0)
        def _():
            dcw_ref[...] = jnp.zeros_like(dcw_ref)
            dcb_ref[...] = jnp.zeros_like(dcb_ref)

        wd = _row_cat(wd_ref)
        hg = hg_ref[...].astype(F32)
        hgp = hgp_ref[...].astype(F32)
        hgn = hgn_ref[...].astype(F32)
        first, last = i == 0, i == nt - 1
        e = HALO - 1
        back, fwd = _shift_rows(hg, jnp.where(first, 0.0, hgp[e:e + 1, :]), jnp.where(last, 0.0, hgn[0:1, :]))
        dt, act, dc = dc_of(dz_ref[...], wd, hv_ref[...].astype(F32), hg, back, fwd, cw_ref, cb_ref)
        dhv_ref[...] = (dt * act).astype(BF16)
        bp, fp = _shift_rows(hgp, hgp[0:1, :], hg[0:1, :])
        _, _, dcp = dc_of(dzp_ref[...], wd, hvp_ref[...].astype(F32), hgp, bp, fp, cw_ref, cb_ref)
        bn, fn = _shift_rows(hgn, hg[tm - 1:tm, :], hgn[e:e + 1, :])
        _, _, dcn = dc_of(dzn_ref[...], wd, hvn_ref[...].astype(F32), hgn, bn, fn, cw_ref, cb_ref)
        dc_back, dc_fwd = _shift_rows(dc, jnp.where(first, 0.0, dcp[e:e + 1, :]), jnp.where(last, 0.0, dcn[0:1, :]))
        dhg_ref[...] = (dc_fwd * cw_ref[0:1, :] + dc * cw_ref[1:2, :] + dc_back * cw_ref[2:3, :]).astype(BF16)
        dcw_ref[0:1, :] += _colsum(dc * back)
        dcw_ref[1:2, :] += _colsum(dc * hg)
        dcw_ref[2:3, :] += _colsum(dc * fwd)
        dcb_ref[...] += _colsum(dc)

    out_slab = pl.BlockSpec((None, tm, FF_BLK), lambda c, i: (c, i, 0))
    cw_spec = pl.BlockSpec((None, 3, FF_BLK), lambda c, i: (c, 0, 0))
    cb_spec = pl.BlockSpec((None, 1, FF_BLK), lambda c, i: (c, 0, 0))
    return pl.pallas_call(
        body, name=name, grid=(4, nt),
        in_specs=[pl.BlockSpec((tm, D), lambda c, i: (i, 0)),
                  pl.BlockSpec((HALO, D), lambda c, i: (before(i), 0)),
                  pl.BlockSpec((HALO, D), lambda c, i: (after(i), 0)),
                  _shards(2, FF_SHARD, D, li, lambda c, i: c),
                  hv_main, hv_prev, hv_next, hg_main, hg_prev, hg_next, cw_spec, cb_spec],
        out_specs=[out_slab, out_slab, cw_spec, cb_spec],
        out_shape=[jax.ShapeDtypeStruct((4, s, FF_BLK), BF16), jax.ShapeDtypeStruct((4, s, FF_BLK), BF16),
                   jax.ShapeDtypeStruct((4, 3, FF_BLK), F32), jax.ShapeDtypeStruct((4, 1, FF_BLK), F32)],
        compiler_params=_cparams(2))(dzb, dzb, dzb, wdown, up, up, up, up, up, up, conv_w, conv_b)


def dh1_ln1_bwd(dz2, dpg, wpg, dhv, dhg, wup, z1, g1, li, name, after=()):
    s = dz2.shape[0]
    tm = 256
    na = len(after)

    def body(dz2_ref, dpg_ref, wpg_ref, dhv_ref, dhg_ref, wup_ref, z1_ref, g_ref, *rest):
        dz_ref, dzb_ref, dg_ref, db_ref = rest[na:]

        @pl.when(pl.program_id(0) == 0)
        def _():
            dg_ref[...] = jnp.zeros_like(dg_ref)
            db_ref[...] = jnp.zeros_like(db_ref)
        dh = ALPHA * dz2_ref[...] + _nt(dpg_ref[...], _row_cat(wpg_ref))
        for c in range(4):
            dh = dh + _nn(dhv_ref[c], wup_ref[c]) + _nn(dhg_ref[c], wup_ref[4 + c])
        dz, dgx = _ln_bwd(dh, z1_ref[...], g_ref[...])
        dz_ref[...] = dz
        dzb_ref[...] = dz.astype(BF16)
        dg_ref[...] += _colsum(dgx)
        db_ref[...] += _colsum(dh)

    row = pl.BlockSpec((tm, D), lambda i: (i, 0))
    vec = pl.BlockSpec((1, D), lambda i: (0, 0))
    slab = pl.BlockSpec((4, tm, FF_BLK), lambda i: (0, i, 0))
    return pl.pallas_call(
        body, name=name, grid=(s // tm,),
        in_specs=[row, row, _shards(N_DEV, D // N_DEV, D, li), slab, slab, _shards(N_DEV, FF_BLK, D, li), row, vec]
        + [ANY] * na,
        out_specs=[row, row, vec, vec],
        out_shape=[jax.ShapeDtypeStruct((s, D), F32), jax.ShapeDtypeStruct((s, D), BF16),
                   jax.ShapeDtypeStruct((1, D), F32), jax.ShapeDtypeStruct((1, D), F32)],
        compiler_params=_cparams(1))(dz2, dpg, wpg, dhv, dhg, wup, z1, g1, *after)


def merge_bwd(dz1b, wmix, proj, ya, yp, li, name, after=()):
    s = dz1b.shape[0]
    tm, tn = 512, 512
    nt = D // tn
    per = tn // (D // N_DEV)
    ga0 = (3 * D_ATTN + D_POOL) // tn

    def body(dz_ref, w_ref, ga_ref, gb_ref, ya_ref, yp_ref, *rest):
        dya_ref, dyp_ref, dga_ref, dgb_ref = rest[len(after):]
        dm = _nt(dz_ref[...], _row_cat(w_ref))
        sa = _sigmoid(ga_ref[...].astype(F32))
        sb = _sigmoid(gb_ref[...].astype(F32))
        dya_ref[...] = (dm * sa).astype(BF16)
        dyp_ref[...] = (dm * sb).astype(BF16)
        dga_ref[...] = (dm * ya_ref[...].astype(F32) * sa * (1.0 - sa)).astype(BF16)
        dgb_ref[...] = (dm * yp_ref[...].astype(F32) * sb * (1.0 - sb)).astype(BF16)

    tile = pl.BlockSpec((tm, tn), lambda i, j: (i, j))
    return pl.pallas_call(
        body, name=name, grid=(s // tm, nt),
        in_specs=[pl.BlockSpec((tm, D), lambda i, j: (i, 0)),
                  _shards(per, D // N_DEV, D, li, lambda i, j: j),
                  pl.BlockSpec((tm, tn), lambda i, j: (i, ga0 + j)),
                  pl.BlockSpec((tm, tn), lambda i, j: (i, ga0 + nt + j)),
                  tile, tile] + [ANY] * len(after),
        out_specs=[tile] * 4,
        out_shape=[jax.ShapeDtypeStruct((s, D), BF16)] * 4,
        compiler_params=_cparams(2))(dz1b, wmix, proj, proj, ya, yp, *after)


def attn_out_bwd(dya, wao, li, name, after=()):
    s = dya.shape[0]
    tm = 512

    def body(d_ref, w_ref, *rest):
        rest[-1][...] = _nt(d_ref[...], _lane_cat(w_ref)).astype(BF16)

    return pl.pallas_call(
        body, name=name, grid=(s // tm,),
        in_specs=[pl.BlockSpec((tm, D), lambda i: (i, 0)), _shards(N_DEV, D_ATTN, 128, li)] + [ANY] * len(after),
        out_specs=pl.BlockSpec((tm, D_ATTN), lambda i: (i, 0)),
        out_shape=jax.ShapeDtypeStruct((s, D_ATTN), BF16),
        compiler_params=_cparams(1))(dya, wao, *after)


def pool_bwd(dyp, wpo, pm, pool_w, pool_scale, li, name):
    s = dyp.shape[0]

    def body(dyp_ref, wpo_ref, pm_ref, w_ref, sc_ref, du_ref, dw_ref, dsc_ref):
        wpo = _lane_cat(wpo_ref)
        dyp = dyp_ref[...]
        for g, w in enumerate(POOL_WINDOWS):
            cols = slice(g * PGD, (g + 1) * PGD)
            dpw = _nt(dyp, wpo[g * PGD:(g + 1) * PGD, :])
            pmg = pm_ref[:, cols]
            dsc_ref[:, cols] = _colsum(dpw * _nn(pmg, w_ref[g]))
            dpmw = (dpw * sc_ref[:, cols]).astype(BF16)
            dw_ref[g] = _tn(pmg, dpmw)
            dpm = _nt(dpmw, w_ref[g])
            du_ref[:, cols] = (_window_sum(dpm / _pool_counts(s, w), w, False) - dpm).astype(BF16)

    full = lambda shape: pl.BlockSpec(shape, lambda i: (0,) * len(shape))
    return pl.pallas_call(
        body, name=name, grid=(1,),
        in_specs=[full((s, D)), _shards(N_DEV, D_POOL, 128, li), full((s, D_POOL)), full((4, PGD, PGD)),
                  full((1, D_POOL))],
        out_specs=[full((s, D_POOL)), full((4, PGD, PGD)), full((1, D_POOL))],
        out_shape=[jax.ShapeDtypeStruct((s, D_POOL), BF16), jax.ShapeDtypeStruct((4, PGD, PGD), F32),
                   jax.ShapeDtypeStruct((1, D_POOL), F32)],
        compiler_params=_cparams(1))(dyp, wpo, pm, pool_w, pool_scale)


def attn_bwd(proj, da, e_rev, li, name, after=()):
    s = proj.shape[0]
    nb = s // QB
    skew = GRID_W + (GRID_W - KW)
    group = QROWS
    half_shape = (group * GRID_W, min(group + KH, KROWS) * GRID_W)

    def body(q_ref, k_ref, v_ref, do_ref, e_ref, *rest):
        dq_ref, dk_ref, dv_ref, g_ref, s_ref, dp_ref, ds_ref, p_ref, dkt_acc, dvt_acc = rest[len(after):]
        b = pl.program_id(1)

        @pl.when(b == 0)
        def _():
            dkt_acc[...] = jnp.zeros_like(dkt_acc)
            dvt_acc[...] = jnp.zeros_like(dvt_acc)
            g_ref[...] = jnp.zeros_like(g_ref)

        ri = lax.broadcasted_iota(I32, (QB, QB), 0)
        ci = lax.broadcasted_iota(I32, (QB, QB), 1)
        rev = jnp.where(ri + ci == QB - 1, 1.0, 0.0).astype(BF16)
        q = _nn(rev, q_ref[...]).astype(BF16) * ATT_SCALE
        do = _nn(rev, do_ref[...]).astype(BF16)
        lane = lax.broadcasted_iota(I32, (1, 128), 1)

        def block(btype, k0):
            dqs = []
            for g0 in range(0, QROWS, group):
                h0, hw = _rows_window(btype, QROWS - g0 - group, group)
                hrows = slice(g0 * GRID_W, (g0 + group) * GRID_W)
                kwin = k_ref[pl.ds(_token_at(k0, h0), hw), :]
                vwin = v_ref[pl.ds(_token_at(k0, h0), hw), :]
                dq = jnp.zeros((group * GRID_W, 128), F32)
                for hh in range(2):
                    lm = (lane // HEAD_DIM) == hh
                    qh = jnp.where(lm, q[hrows], jnp.zeros_like(q[hrows]))
                    doh = jnp.where(lm, do[hrows], jnp.zeros_like(do[hrows]))
                    kh = jnp.where(lm, kwin, jnp.zeros_like(kwin))
                    s_ref[:, 0:hw] = _nt(qh, kwin)
                    dp_ref[:, 0:hw] = _nt(doh, vwin)
                    g = jnp.zeros((1, KB), F32)
                    for r in range(group):
                        qr = QROWS - 1 - (g0 + r)
                        rows = slice(r * GRID_W, (r + 1) * GRID_W)
                        sb, a0, w, pad = _row_logits(s_ref, e_ref, hh, rows, btype, qr, h0)
                        p = jnp.exp(sb - jnp.max(sb, axis=1, keepdims=True))
                        p = p * (1.0 / jnp.sum(p, axis=1, keepdims=True))
                        dp = dp_ref[rows, a0:a0 + w]
                        ds = p * (dp - jnp.sum(p * dp, axis=1, keepdims=True))
                        _store_row(ds_ref, rows, a0, w, ds, hw)
                        _store_row(p_ref, rows, a0, w, p, hw)
                        t = jnp.sum(pltpu.roll(ds, w - skew, 1, stride=1, stride_axis=0), axis=0, keepdims=True)
                        t = t[:, :KH * GRID_W] if pad else pltpu.roll(t, GRID_W, 1)
                        i0 = _attn_row(btype, qr)[2]
                        g = g + pltpu.roll(jnp.concatenate([t, jnp.zeros_like(t)], axis=1), i0 * GRID_W, 1)
                    g_ref[hh] += g
                    dsb = ds_ref[:, 0:hw]
                    dq = dq + _nn(dsb, kh) * ATT_SCALE
                    dkt_acc[:, pl.ds(_token_at(k0, h0), hw)] += _tn(qh, dsb)
                    dvt_acc[:, pl.ds(_token_at(k0, h0), hw)] += _tn(doh, p_ref[:, 0:hw])
                dqs.append(dq.astype(BF16))
            dq_ref[...] = _nn(rev, jnp.concatenate(dqs, axis=0)).astype(BF16)

        for btype, (cond, k0) in enumerate(_attn_types(b, nb)):
            pl.when(cond)(lambda btype=btype, k0=k0: block(btype, k0))

        @pl.when(b == nb - 1)
        def _():
            dk_ref[...] = dkt_acc[...].T.astype(BF16)
            dv_ref[...] = dvt_acc[...].T.astype(BF16)

    col = pl.BlockSpec((s, 128), lambda j, b: (0, j))
    return pl.pallas_call(
        body, name=name, grid=(4, nb),
        in_specs=[pl.BlockSpec((QB, 128), lambda j, b: (b, j)),
                  pl.BlockSpec((s, 128), lambda j, b: (0, 4 + j)),
                  pl.BlockSpec((s, 128), lambda j, b: (0, 8 + j)),
                  pl.BlockSpec((QB, 128), lambda j, b: (b, j)),
                  pl.BlockSpec((None, 2, 2, GRID_W, KB), lambda j, b: (li, j, 0, 0, 0))] + [ANY] * len(after),
        out_specs=[pl.BlockSpec((QB, 128), lambda j, b: (b, j)), col, col,
                   pl.BlockSpec((2, 1, KB), lambda j, b: (j, 0, 0))],
        out_shape=[jax.ShapeDtypeStruct((s, D_ATTN), BF16)] * 3 + [jax.ShapeDtypeStruct((N_HEADS, 1, KB), F32)],
        scratch_shapes=[pltpu.VMEM(half_shape, F32), pltpu.VMEM(half_shape, F32), pltpu.VMEM(half_shape, BF16),
                        pltpu.VMEM(half_shape, BF16), pltpu.VMEM((128, s), F32), pltpu.VMEM((128, s), F32)],
        compiler_params=_cparams(2))(proj, proj, proj, da, e_rev, *after)


def dh0_bwd(dz1, pieces, win, li, name, after=()):
    s = dz1.shape[0]
    tm = 256
    bn = N_PROJ // N_DEV

    def body(dz_ref, q_ref, k_ref, v_ref, u_ref, ga_ref, gb_ref, w_ref, *rest):
        acc = ALPHA * dz_ref[...]
        for j, ref in enumerate((q_ref, k_ref, v_ref, u_ref)):
            acc = acc + _nt(ref[...], w_ref[j])
        for j, ref in ((4, ga_ref), (6, gb_ref)):
            acc = acc + _nt(ref[:, 0:bn], w_ref[j]) + _nt(ref[:, bn:2 * bn], w_ref[j + 1])
        rest[-1][...] = acc

    row = pl.BlockSpec((tm, D), lambda i: (i, 0))
    narrow = pl.BlockSpec((tm, bn), lambda i: (i, 0))
    return pl.pallas_call(
        body, name=name, grid=(s // tm,),
        in_specs=[row] + [narrow] * 4 + [row, row, _shards(N_DEV, D, bn, li)] + [ANY] * len(after),
        out_specs=row, out_shape=jax.ShapeDtypeStruct((s, D), F32),
        compiler_params=_cparams(1))(dz1, *pieces, win, *after)


def _coords():
    return lax.axis_index("x"), lax.axis_index("y"), lax.axis_index("c")


def _dev_index(px, py, pc):
    return 4 * px + 2 * py + pc


HBM = pl.BlockSpec(memory_space=pltpu.HBM)
SEM = pl.BlockSpec(memory_space=pltpu.SEMAPHORE)
_EFFECT = pltpu.SideEffectType.DATAFLOW_SIDE_EFFECTING
_TOKEN = jax.ShapeDtypeStruct((8, 128), F32)


def _in_hbm(a):
    return pltpu.with_memory_space_constraint(a, pltpu.HBM)


def _hbm_like(a):
    return pltpu.HBM(a.shape, a.dtype)


def _peers(x, y, c):
    return [(x, y, 1 - c), (1 - x, y, c), (x, 1 - y, c), (1 - x, 1 - y, c)]


def ag_start(lands, after, name):
    n = len(lands)

    def body(*refs):
        land = refs[:n]
        send_sem, recv_sem, token = refs[n + 1], refs[n + 2], refs[-1]
        x, y, c = _coords()
        me = _dev_index(x, y, c)
        for k, peer in enumerate(_peers(x, y, c)):
            for a in range(n):
                pltpu.make_async_remote_copy(src_ref=land[a].at[me], dst_ref=land[a].at[me], send_sem=send_sem.at[k],
                                             recv_sem=recv_sem.at[k], device_id=peer, device_id_type=MESH).start()
        token[...] = jnp.zeros_like(token)

    res = pl.pallas_call(
        body, name=name,
        out_shape=(pltpu.SemaphoreType.DMA((4,)), pltpu.SemaphoreType.DMA((4,)), *[_hbm_like(l) for l in lands], _TOKEN),
        in_specs=[HBM] * n + [ANY], out_specs=(SEM, SEM, *[HBM] * n, pl.BlockSpec(memory_space=pltpu.VMEM)),
        input_output_aliases={a: 2 + a for a in range(n)},
        compiler_params=pltpu.CompilerParams(has_side_effects=_EFFECT),
    )(*[_in_hbm(l) for l in lands], after)
    return res[0], res[1], list(res[2:2 + n]), res[-1]


def ag_forward(send_sem, recv_sem, lands, after, name):
    n = len(lands)

    def body(*refs):
        send_sem, recv_sem = refs[0], refs[1]
        land = refs[2:2 + n]
        fsend, frecv = refs[3 + n], refs[4 + n]
        x, y, c = _coords()
        peers = _peers(x, y, c)
        for k in range(1, 4):
            blk = _dev_index(*peers[k])
            for a in range(n):
                pltpu.make_async_remote_copy(src_ref=land[a].at[blk], dst_ref=land[a].at[blk], send_sem=send_sem.at[k],
                                             recv_sem=recv_sem.at[k], device_id=peers[k], device_id_type=MESH).wait_recv()
        for k in range(1, 4):
            blk = _dev_index(*peers[k])
            for a in range(n):
                pltpu.make_async_remote_copy(src_ref=land[a].at[blk], dst_ref=land[a].at[blk], send_sem=fsend.at[k - 1],
                                             recv_sem=frecv.at[k - 1], device_id=peers[0], device_id_type=MESH).start()

    res = pl.pallas_call(
        body, name=name,
        out_shape=(pltpu.SemaphoreType.DMA((3,)), pltpu.SemaphoreType.DMA((3,)), *[_hbm_like(l) for l in lands]),
        in_specs=[SEM, SEM, *[HBM] * n, ANY], out_specs=(SEM, SEM, *[HBM] * n),
        input_output_aliases={2 + a: 2 + a for a in range(n)},
        compiler_params=pltpu.CompilerParams(has_side_effects=_EFFECT),
    )(send_sem, recv_sem, *lands, after)
    return res[0], res[1], list(res[2:])


def ag_finish(send_sem, recv_sem, fsend, frecv, lands, after, name):
    n = len(lands)

    def body(*refs):
        send_sem, recv_sem, fsend, frecv = refs[:4]
        land = refs[4:4 + n]
        x, y, c = _coords()
        me = _dev_index(x, y, c)
        peers = _peers(x, y, c)
        for k in range(4):
            for a in range(n):
                pltpu.make_async_remote_copy(src_ref=land[a].at[me], dst_ref=land[a].at[me], send_sem=send_sem.at[k],
                                             recv_sem=recv_sem.at[k], device_id=peers[k], device_id_type=MESH).wait_send()
        sib = _dev_index(*peers[0])
        for a in range(n):
            pltpu.make_async_remote_copy(src_ref=land[a].at[sib], dst_ref=land[a].at[sib], send_sem=send_sem.at[0],
                                         recv_sem=recv_sem.at[0], device_id=peers[0], device_id_type=MESH).wait_recv()
        for k in range(1, 4):
            mine = _dev_index(*peers[k])
            theirs = _dev_index(peers[k][0], peers[k][1], 1 - c)
            for a in range(n):
                pltpu.make_async_remote_copy(src_ref=land[a].at[mine], dst_ref=land[a].at[theirs], send_sem=fsend.at[k - 1],
                                             recv_sem=frecv.at[k - 1], device_id=peers[0], device_id_type=MESH).wait()

    res = pl.pallas_call(
        body, name=name, out_shape=tuple(_hbm_like(l) for l in lands),
        in_specs=[SEM] * 4 + [HBM] * n + [ANY], out_specs=tuple([HBM] * n),
        input_output_aliases={4 + a: a for a in range(n)},
        compiler_params=pltpu.CompilerParams(has_side_effects=_EFFECT),
    )(send_sem, recv_sem, fsend, frecv, *lands, after)
    return list(res)


def rs_start(psums, name):
    n = len(psums)
    lands = [lax.empty(p.shape, p.dtype) for p in psums]

    def body(*refs):
        src, land = refs[:n], refs[n:2 * n]
        send_sem, recv_sem, token = refs[2 * n], refs[2 * n + 1], refs[-1]
        peers = _peers(*_coords())
        for k in range(3):
            for a in range(n):
                pltpu.make_async_remote_copy(src_ref=src[a].at[k], dst_ref=land[a].at[k], send_sem=send_sem.at[k],
                                             recv_sem=recv_sem.at[k], device_id=peers[k + 1], device_id_type=MESH).start()
        token[...] = jnp.zeros_like(token)

    res = pl.pallas_call(
        body, name=name,
        out_shape=(pltpu.SemaphoreType.DMA((3,)), pltpu.SemaphoreType.DMA((3,)), *[_hbm_like(p) for p in psums],
                   *[_hbm_like(l) for l in lands], _TOKEN),
        in_specs=[HBM] * (2 * n), out_specs=(SEM, SEM, *[HBM] * (2 * n), pl.BlockSpec(memory_space=pltpu.VMEM)),
        input_output_aliases={a: 2 + a for a in range(2 * n)},
        compiler_params=pltpu.CompilerParams(has_side_effects=_EFFECT),
    )(*[_in_hbm(p) for p in psums], *[_in_hbm(l) for l in lands])
    return res[0], res[1], list(res[2:2 + n]), list(res[2 + n:2 + 2 * n]), res[-1]


def rs_finish(send_sem, recv_sem, psums, lands, after, name):
    n = len(psums)

    def body(*refs):
        send_sem, recv_sem = refs[0], refs[1]
        src, land = refs[2:2 + n], refs[2 + n:2 + 2 * n]
        peers = _peers(*_coords())
        for k in range(3):
            for a in range(n):
                pltpu.make_async_remote_copy(src_ref=src[a].at[k], dst_ref=land[a].at[k], send_sem=send_sem.at[k],
                                             recv_sem=recv_sem.at[k], device_id=peers[k + 1], device_id_type=MESH).wait()

    res = pl.pallas_call(
        body, name=name, out_shape=tuple(_hbm_like(l) for l in lands),
        in_specs=[SEM, SEM] + [HBM] * (2 * n) + [ANY], out_specs=tuple([HBM] * n),
        input_output_aliases={2 + n + a: a for a in range(n)},
        compiler_params=pltpu.CompilerParams(has_side_effects=_EFFECT),
    )(send_sem, recv_sem, *psums, *lands, after)
    return list(res)


def d2d_start(grads, name):
    n = len(grads)
    lands = [lax.empty((4,) + g.shape[1:], g.dtype) for g in grads]

    def body(*refs):
        src, land = refs[:n], refs[n:2 * n]
        send_sem, recv_sem, token = refs[2 * n], refs[2 * n + 1], refs[-1]
        x, y, c = _coords()
        for a in range(n):
            for k in range(4):
                blk = _dev_index(x ^ (k & 1), y ^ (k >> 1), 1 - c)
                pltpu.make_async_remote_copy(src_ref=src[a].at[blk], dst_ref=land[a].at[k], send_sem=send_sem.at[0],
                                             recv_sem=recv_sem.at[0], device_id=(x, y, 1 - c), device_id_type=MESH).start()
        token[...] = jnp.zeros_like(token)

    res = pl.pallas_call(
        body, name=name,
        out_shape=(pltpu.SemaphoreType.DMA((1,)), pltpu.SemaphoreType.DMA((1,)), *[_hbm_like(g) for g in grads],
                   *[_hbm_like(l) for l in lands], _TOKEN),
        in_specs=[HBM] * (2 * n), out_specs=(SEM, SEM, *[HBM] * (2 * n), pl.BlockSpec(memory_space=pltpu.VMEM)),
        input_output_aliases={a: 2 + a for a in range(2 * n)},
        compiler_params=pltpu.CompilerParams(has_side_effects=_EFFECT),
    )(*[_in_hbm(g) for g in grads], *[_in_hbm(l) for l in lands])
    return res[0], res[1], list(res[2:2 + n]), list(res[2 + n:2 + 2 * n]), res[-1]


def d2d_finish(send_sem, recv_sem, grads, lands, after, name):
    n = len(grads)

    def body(*refs):
        send_sem, recv_sem = refs[0], refs[1]
        src, land = refs[2:2 + n], refs[2 + n:2 + 2 * n]
        x, y, c = _coords()
        for a in range(n):
            for k in range(4):
                blk = _dev_index(x ^ (k & 1), y ^ (k >> 1), 1 - c)
                pltpu.make_async_remote_copy(src_ref=src[a].at[blk], dst_ref=land[a].at[k], send_sem=send_sem.at[0],
                                             recv_sem=recv_sem.at[0], device_id=(x, y, 1 - c), device_id_type=MESH).wait()

    res = pl.pallas_call(
        body, name=name, out_shape=tuple(_hbm_like(t) for t in list(grads) + list(lands)),
        in_specs=[SEM, SEM] + [HBM] * (2 * n) + [ANY], out_specs=tuple([HBM] * (2 * n)),
        input_output_aliases={2 + a: a for a in range(2 * n)},
        compiler_params=pltpu.CompilerParams(has_side_effects=_EFFECT),
    )(send_sem, recv_sem, *grads, *lands, after)
    return list(res[:n]), list(res[n:])


def pair_add(blk_idx, g, recv, name):
    _, r, c = g.shape
    tr = _row_tile(r)

    def body(idx_ref, g0, g1, g2, g3, r_ref, own_ref, oth_ref):
        own_ref[...] = g0[...].astype(F32) + r_ref[0].astype(F32)
        for k, gk in enumerate((g1, g2, g3)):
            oth_ref[k] = (gk[...].astype(F32) + r_ref[k + 1].astype(F32)).astype(BF16)

    def blk(k):
        return pl.BlockSpec((None, tr, c), lambda t, idx: (idx[k], t, 0))

    grid_spec = pltpu.PrefetchScalarGridSpec(
        num_scalar_prefetch=1, grid=(r // tr,),
        in_specs=[blk(0), blk(1), blk(2), blk(3), pl.BlockSpec((4, tr, c), lambda t, idx: (0, t, 0))],
        out_specs=[pl.BlockSpec((tr, c), lambda t, idx: (t, 0)), pl.BlockSpec((3, tr, c), lambda t, idx: (0, t, 0))])
    return pl.pallas_call(
        body, name=name, grid_spec=grid_spec,
        out_shape=[jax.ShapeDtypeStruct((r, c), F32), jax.ShapeDtypeStruct((3, r, c), BF16)],
        compiler_params=_cparams(1))(blk_idx, g, g, g, g, recv)


def _row_tile(r):
    return next(t for t in (512, 352, 256, 128) if r % t == 0)


def _adamw(w, g, m, v):
    m = ADAM_B1 * m + (1.0 - ADAM_B1) * g
    v = ADAM_B2 * v + (1.0 - ADAM_B2) * (g * g)
    m_hat = m / (1.0 - ADAM_B1 ** ADAM_STEP)
    v_hat = v / (1.0 - ADAM_B2 ** ADAM_STEP)
    delta = -ADAM_LR * (m_hat / (jnp.sqrt(v_hat) + ADAM_EPS) + ADAM_WD * w)
    return delta, m, v


def adamw_shard(own, recv, w, m, v, li, prev, name):
    r, c = own.shape
    tr = _row_tile(r)

    def body(own_ref, recv_ref, w_ref, m_ref, v_ref, p0, p1, p2, p3, g_ref, d_ref, nm_ref, nv_ref):
        g = own_ref[...] + recv_ref[0].astype(F32) + recv_ref[1].astype(F32) + recv_ref[2].astype(F32)
        delta, nm, nv = _adamw(w_ref[...], g, m_ref[...], v_ref[...])
        g_ref[...] = g
        d_ref[...] = delta
        nm_ref[...] = nm
        nv_ref[...] = nv

    lay = pl.BlockSpec((None, tr, c), lambda t: (li, t, 0))
    stack = jax.ShapeDtypeStruct((DEPTH, r, c), F32)
    return pl.pallas_call(
        body, name=name, grid=(r // tr,),
        in_specs=[pl.BlockSpec((tr, c), lambda t: (t, 0)), pl.BlockSpec((3, tr, c), lambda t: (0, t, 0)),
                  lay, lay, lay, ANY, ANY, ANY, ANY],
        out_specs=[lay] * 4, out_shape=[stack] * 4,
        input_output_aliases={5: 0, 6: 1, 7: 2, 8: 3},
        compiler_params=_cparams(1))(own, recv, w, m, v, *prev)


def sum_partials(gathered, name):
    _, r, c = gathered.shape
    tr = next(t for t in (96, 88, 64, _PACK_TILE) if r % t == 0)

    def body(gs_ref, g_ref):
        g = gs_ref[0]
        for d in range(1, N_DEV):
            g = g + gs_ref[d]
        g_ref[...] = g

    return pl.pallas_call(
        body, name=name, grid=(r // tr,),
        in_specs=[pl.BlockSpec((N_DEV, tr, c), lambda t: (0, t, 0))],
        out_specs=pl.BlockSpec((tr, c), lambda t: (t, 0)), out_shape=jax.ShapeDtypeStruct((r, c), F32),
        compiler_params=_cparams(1))(gathered)


def adamw_plain(g, w, m, v, name):
    def body(g_ref, w_ref, m_ref, v_ref, d_ref, nm_ref, nv_ref):
        delta, nm, nv = _adamw(w_ref[...], g_ref[...], m_ref[...], v_ref[...])
        d_ref[...] = delta
        nm_ref[...] = nm
        nv_ref[...] = nv

    return pl.pallas_call(body, name=name, out_shape=[jax.ShapeDtypeStruct(w.shape, F32)] * 3)(g, w, m, v)


_PACK_LAYER = (("b_in", (N_PROJ,)), ("rpb", (N_HEADS, 2 * KH - 1, 2 * KW - 1)), ("pool_w", (4, PGD, PGD)),
               ("pool_scale", (D_POOL,)), ("ln1_g", (D,)), ("ln1_b", (D,)), ("conv_b", (D_FF,)), ("ln2_g", (D,)),
               ("ln2_b", (D,)), ("conv_w", (3, D_FF)))
_PACK_INPUT = (("ln_in_g", (D,)), ("ln_in_b", (D,)))
_PACK_LANES = 1024
_PACK_TILE = 8
_EARLY = tuple(range(1, DEPTH))


def _pack_items(layers):
    items = [(n, (len(layers),) + s) for n, s in _PACK_LAYER]
    return items + ([(n, s) for n, s in _PACK_INPUT] if 0 in layers else [])


def _pack(parts, layers):
    flats = [(parts[name] if (name, shape) in _PACK_INPUT else jnp.stack([parts[name][li] for li in layers]))
             .reshape(-1).astype(F32) for name, shape in _pack_items(layers)]
    used = sum(f.shape[0] for f in flats)
    tile = _PACK_TILE * _PACK_LANES
    total = -(-used // tile) * tile
    return jnp.concatenate(flats + [jnp.zeros((total - used,), F32)]).reshape(total // _PACK_LANES, _PACK_LANES)


def _unpack(packed, layers):
    flat, out, off = packed.reshape(-1), {}, 0
    for name, shape in _pack_items(layers):
        n = int(np.prod(shape))
        out[name] = flat[off:off + n].reshape(shape)
        off += n
    return out


def _bias_tables(rpb):
    qc = np.arange(GRID_W)[:, None]
    kc = np.arange(GRID_W)[None, :]
    start = np.clip(qc - KW // 2, 0, GRID_W - KW)
    valid = (kc >= start) & (kc < start + KW)
    col = np.clip(kc - qc, -(KW - 1), KW - 1) + KW - 1
    onehot = (col.reshape(-1)[None, :] == np.arange(2 * KW - 1)[:, None]).astype(np.float32)
    depth = rpb.shape[0]
    rows = jnp.pad(rpb, ((0, 0), (0, 0), (0, 1), (0, 0)))
    tab = jnp.einsum("lhij,jm->lhim", rows, jnp.asarray(onehot), precision=lax.Precision.HIGHEST)
    tab = tab.reshape(depth, N_HEADS, KROWS, GRID_W, GRID_W).transpose(0, 1, 3, 2, 4)
    ok = valid[:, None, :] & (np.arange(KROWS) < 2 * KH - 1)[None, :, None]
    tab = jnp.where(jnp.asarray(ok), tab, NEG_INF).reshape(depth, N_HEADS, GRID_W, KB)
    tab = jnp.stack([tab, jnp.roll(tab, GRID_W, axis=-1)], axis=2)
    return tab, tab[:, :, :, ::-1, :]


_SHARDED = ("w_in", "w_attn_out", "w_pool_out", "w_mix_out", "w_up", "w_down", "w_ple_gate", "w_ple_proj")
_NAMES = ("ln_in_g", "ln_in_b", "w_in", "b_in", "rpb", "w_attn_out", "pool_w", "pool_scale", "w_pool_out", "w_mix_out",
          "ln1_g", "ln1_b", "w_up", "conv_w", "conv_b", "w_down", "w_ple_gate", "w_ple_proj", "ln2_g", "ln2_b")


def kernel(x, p, ln_in_g, ln_in_b, w_in, b_in, rpb, w_attn_out, pool_w, pool_scale, w_pool_out, w_mix_out, ln1_g, ln1_b, w_up, conv_w, conv_b, w_down, w_ple_gate, w_ple_proj, ln2_g, ln2_b, loss_target, m_ln_in_g, m_ln_in_b, m_w_in, m_b_in, m_rpb, m_w_attn_out, m_pool_w, m_pool_scale, m_w_pool_out, m_w_mix_out, m_ln1_g, m_ln1_b, m_w_up, m_conv_w, m_conv_b, m_w_down, m_w_ple_gate, m_w_ple_proj, m_ln2_g, m_ln2_b, v_ln_in_g, v_ln_in_b, v_w_in, v_b_in, v_rpb, v_w_attn_out, v_pool_w, v_pool_scale, v_w_pool_out, v_w_mix_out, v_ln1_g, v_ln1_b, v_w_up, v_conv_w, v_conv_b, v_w_down, v_w_ple_gate, v_w_ple_proj, v_ln2_g, v_ln2_b):
    a = dict(locals())
    W = {n: a[n] for n in _NAMES}
    M = {n: a["m_" + n] for n in _NAMES}
    V = {n: a["v_" + n] for n in _NAMES}
    xi, yi, ci = _coords()
    me = _dev_index(xi, yi, ci)
    x2, tgt = x[0], loss_target[0]
    pb = p[:, 0].astype(BF16)

    flip = lambda d: {**d, "w_up": d["w_up"].transpose(0, 2, 1)}
    ex = _Exchange(flip(W), flip(M), flip(V))
    loss_part, dx, parts = _local_step(x2, tgt, pb, W, ex)
    loss = lax.psum(loss_part[0, 0], AXES)

    started = ex.replicated_start("late", _pack(parts, (0,)), dx)
    done = ex.update(range(DEPTH - 1, 0, -1), started)
    ex.replicated_forward("late", done)
    done = ex.update((0,), done)
    stacks = {**ex.stacks, "w_up": [t.transpose(0, 2, 1) for t in ex.stacks["w_up"]]}
    lo, hi = [_unpack(sum_partials(ex.replicated_finish(tag, done), f"sum_replicated_{tag}"), layers)
              for layers, tag in (((0,), "late"), (_EARLY, "early"))]
    grads = {**{n: jnp.concatenate([lo[n], hi[n]]) for n, _ in _PACK_LAYER}, **{n: lo[n] for n, _ in _PACK_INPUT}}
    grads["conv_w"] = lax.dynamic_slice_in_dim(grads["conv_w"], me * FF_SHARD, FF_SHARD, axis=2)
    res = [{n: stacks[n][k] for n in _SHARDED} for k in range(4)]
    for n, g in grads.items():
        two_d = lambda t: t.reshape(-1, t.shape[-1])
        outs = adamw_plain(two_d(g), two_d(W[n]), two_d(M[n]), two_d(V[n]), f"adamw_{n}")
        for d, o in zip(res, [g] + [o.reshape(W[n].shape) for o in outs]):
            d[n] = o
    return (loss, dx[None], *[res[k][n] for k in range(4) for n in _NAMES])


class _Exchange:
    GROUPS = (("w_ple_gate", "w_ple_proj", "w_down", "w_up"), ("w_mix_out", "w_attn_out", "w_pool_out"), ("w_in",))
    FIRST = ("w_in",)

    def __init__(self, W, M, V):
        self.W, self.M, self.V = W, M, V
        xi, yi, ci = _coords()
        me = _dev_index(xi, yi, ci)
        self.me = me.astype(I32).reshape(1)
        self.rel_idx = jnp.stack([_dev_index(xi ^ (k & 1), yi ^ (k >> 1), ci) for k in range(4)]).astype(I32)
        self.lands = [{n: lax.dynamic_update_index_in_dim(lax.empty((N_DEV,) + W[n].shape[1:], BF16),
                                                          W[n][li].astype(BF16), me, 0) for n in _SHARDED}
                      for li in range(DEPTH)]
        cw_land = lax.dynamic_update_index_in_dim(lax.empty((N_DEV,) + W["conv_w"].shape, F32), W["conv_w"], me, 0)
        self.ag, self.fwd, self.rs, self.pending, self.small = {}, {}, {}, {}, {}
        self.stacks = {n: [lax.empty((DEPTH,) + W[n].shape[1:], F32) for _ in range(4)] for n in _SHARDED}
        self.late = tuple(n for n in _SHARDED if n not in self.FIRST)
        self.ag[0] = ag_start([self.lands[0][n] for n in self.FIRST] + [cw_land], W["conv_w"], "ag_start0")

    def tokens(self):
        return [self.ag[0][3]]

    def prefetch(self, li, after):
        send, recv, lands, _ = self.ag[li]
        self.fwd[li] = ag_forward(send, recv, lands, after, f"ag_forward{li}")
        if li == 0:
            self.ag["0b"] = ag_start([self.lands[0][n] for n in self.late], self.fwd[0][2][0], "ag_start0b")

    def weights(self, li, after):
        send, recv, _, _ = self.ag.pop(li)
        fsend, frecv, lands = self.fwd.pop(li)
        lands = ag_finish(send, recv, fsend, frecv, lands, after, f"ag_finish{li}")
        if li == 0:
            self.cw = lands[-1].transpose(1, 2, 0, 3).reshape(DEPTH, 3, 4, FF_BLK).transpose(0, 2, 1, 3)
            return dict(zip(self.FIRST, lands)), self.cw[li], (self.ag["0b"][3],)
        tokens = ()
        if li + 1 < DEPTH:
            self.ag[li + 1] = ag_start([self.lands[li + 1][n] for n in _SHARDED], lands[0], f"ag_start{li + 1}")
            tokens = (self.ag[li + 1][3],)
        return dict(zip(_SHARDED, lands)), self.cw[li], tokens

    def rest(self, li, G, mid, after):
        if li != 0:
            return G, ()
        send, recv, lands, _ = self.ag.pop("0b")
        fsend, frecv, lands = ag_forward(send, recv, lands, mid, "ag_forward0b")
        lands = ag_finish(send, recv, fsend, frecv, lands, after, "ag_finish0b")
        self.ag[1] = ag_start([self.lands[1][n] for n in _SHARDED], lands[0], "ag_start1")
        return {**G, **dict(zip(self.late, lands))}, (self.ag[1][3],)

    def grads(self, li, group, gw):
        self.pending.setdefault(li, {}).update(gw)
        if li != 0 and group != len(self.GROUPS) - 1:
            return None
        gw = self.pending.pop(li)
        tag = f"{li}_{group}" if li == 0 else f"{li}"
        send, recv, glist, lands, token = d2d_start(list(gw.values()), f"d2d_start{tag}")
        self.d2d = (tag, tuple(gw), send, recv, glist, lands)
        return token

    def flush(self, li, group, after):
        if li != 0 and group != len(self.GROUPS) - 1:
            return None
        tag, names, send, recv, glist, lands = self.d2d
        glist, recv1 = d2d_finish(send, recv, glist, lands, after, f"d2d_finish{tag}")
        sums = [pair_add(self.rel_idx, g, r1, f"pair_add_{n}{li}") for n, g, r1 in zip(names, glist, recv1)]
        send, recv, psums, lands, token = rs_start([s_[1] for s_ in sums], f"rs_start{tag}")
        self.rs.setdefault(li, []).append((tag, names, send, recv, psums, lands, [s_[0] for s_ in sums]))
        if li == 0 and group == 1 and "early" in self.small:
            self.replicated_forward("early", token)
        return token

    def update(self, layers, after):
        for li in layers:
            for tag, names, send, recv, psums, lands, owns in self.rs.pop(li):
                recv2 = rs_finish(send, recv, psums, lands, after, f"rs_finish{tag}")
                for n, own, r2 in zip(names, owns, recv2):
                    self.stacks[n] = adamw_shard(own, r2, self.W[n], self.M[n], self.V[n], li, self.stacks[n],
                                                 f"adamw_{n}{li}")
                    after = self.stacks[n][0]
        return after

    def replicated_start(self, tag, pack, after):
        land = lax.dynamic_update_index_in_dim(lax.empty((N_DEV,) + pack.shape, F32), pack, self.me[0], 0)
        self.small[tag] = ag_start([land], after, f"ag_start_small_{tag}")
        return self.small[tag][3]

    def replicated_early(self, small, after):
        return self.replicated_start("early", _pack(small, _EARLY), after)

    def replicated_forward(self, tag, after):
        send, recv, lands, _ = self.small[tag]
        self.small[tag] = (send, recv) + ag_forward(send, recv, lands, after, f"ag_forward_small_{tag}")

    def replicated_finish(self, tag, after):
        send, recv, fsend, frecv, lands = self.small.pop(tag)
        return ag_finish(send, recv, fsend, frecv, lands, after, f"ag_finish_small_{tag}")[0]


def _local_step(x2, tgt, pb, W, ex):
    depth = W["rpb"].shape[0]
    vec = lambda t: t.reshape(1, -1)
    ln1_g, ln1_b, ln2_g, ln2_b = W["ln1_g"], W["ln1_b"], W["ln2_g"], W["ln2_b"]
    b_in, rpb, pool_scale = W["b_in"], W["rpb"], W["pool_scale"]
    cb_full = W["conv_b"].reshape(depth, 4, 1, FF_BLK)
    pool_w_b = W["pool_w"].astype(BF16)
    e_tab, e_rev = _bias_tables(rpb)

    h, hb = ln_fwd(x2, vec(W["ln_in_g"]), vec(W["ln_in_b"]), "ln_in", after=ex.tokens())
    ex.prefetch(0, hb)
    saved = []
    for li in range(depth):
        G, cw, tokens = ex.weights(li, hb)
        bias = vec(b_in[li])
        proj, u = proj_fwd(hb, G["w_in"], bias, li, f"proj{li}", after=tokens)
        att = attn_fwd(proj, e_tab, li, f"attn{li}")
        pm, pw = pool_fwd(u, pool_w_b[li], vec(pool_scale[li]), f"pool{li}")
        G, tokens = ex.rest(li, G, att, pw)
        mg, ya, yp = merge_fwd(att, pw, G["w_attn_out"], G["w_pool_out"], proj, li, f"merge{li}", after=tokens)
        if li + 1 < depth:
            ex.prefetch(li + 1, mg)
        z1, h1, h1b = mix_ln_fwd(mg, G["w_mix_out"], h, vec(ln1_g[li]), vec(ln1_b[li]), li, f"mix_ln{li}")
        up = up_fwd(h1b, G["w_up"], li, f"up{li}")
        t = ffn_act_fwd(up, cw, cb_full[li], f"ffn_act{li}")
        z2, h2, h2b, pg, pp = down_ple_ln_fwd(t, G["w_down"], h1b, G["w_ple_gate"], pb[li], G["w_ple_proj"], h1,
                                              vec(ln2_g[li]), vec(ln2_b[li]), li, f"down_ln{li}")
        saved.append(dict(hb=hb, proj=proj, att=att, pm=pm, pw=pw, mg=mg, ya=ya, yp=yp, z1=z1, h1b=h1b, up=up, t=t,
                          z2=z2, pg=pg, pp=pp, G=G, cw=cw))
        h, hb = h2, h2b

    dh, loss_part = loss_bwd(h, tgt, "loss")
    small = {n: [None] * depth for n in ("b_in", "rpb", "pool_w", "pool_scale", "ln1_g", "ln1_b", "conv_b", "ln2_g",
                                         "ln2_b", "conv_w")}
    token = ()
    tok = lambda t: () if t is None else (t,)
    for li in reversed(range(depth)):
        sv = saved[li]
        G, cw = sv["G"], sv["cw"]
        dz2, dz2b, dpg, dpp, dg2, db2 = ln2_ple_bwd(dh, sv["z2"], vec(ln2_g[li]), sv["pg"], sv["pp"], f"ln2_bwd{li}",
                                                    after=token)
        gw = {}
        gw["w_ple_gate"], gw["w_ple_proj"] = wgrad_pair(sv["h1b"], dpg, [(pb[li], dpp)], f"dw_ple{li}")
        gw["w_down"] = wgrad_down(sv["t"], dz2b, f"dw_down{li}").reshape(N_DEV, FF_SHARD, D)
        dhv, dhg, dcw, dcb = ffn_act_bwd(dz2b, G["w_down"], sv["up"], cw, cb_full[li], li, f"ffn_bwd{li}")
        gw["w_up"] = wgrad_up(sv["h1b"], dhv, dhg, f"dw_up{li}")
        token = tok(ex.grads(li, 0, gw))
        dz1, dz1b, dg1, db1 = dh1_ln1_bwd(dz2, dpg, G["w_ple_gate"], dhv, dhg, G["w_up"], sv["z1"], vec(ln1_g[li]), li,
                                          f"ln1_bwd{li}", after=token)
        token = tok(ex.flush(li, 0, dz1b))
        dya, dyp, dga, dgb = merge_bwd(dz1b, G["w_mix_out"], sv["proj"], sv["ya"], sv["yp"], li, f"merge_bwd{li}",
                                       after=token)
        gw = dict(zip(("w_mix_out", "w_attn_out", "w_pool_out"),
                      wgrad_pair(sv["mg"], dz1b, [(sv["att"], dya), (sv["pw"], dyp)], f"dw_out{li}")))
        token = tok(ex.grads(li, 1, gw))
        da = attn_out_bwd(dya, G["w_attn_out"], li, f"da{li}", after=token)
        du, dpool_w, dpool_sc = pool_bwd(dyp, G["w_pool_out"], sv["pm"], pool_w_b[li], vec(pool_scale[li]), li,
                                         f"pool_bwd{li}")
        token = tok(ex.flush(li, 1, du))
        dq, dk, dv, drpb = attn_bwd(sv["proj"], da, e_rev, li, f"attn_bwd{li}", after=token)
        dproj = [dq, dk, dv, du, dga, dgb]
        dw_in, db_in = wgrad_in(sv["hb"], dproj, f"dw_in{li}")
        token = tok(ex.grads(li, 2, {"w_in": dw_in}))
        dh = dh0_bwd(dz1, dproj, G["w_in"], li, f"dh0{li}", after=token)
        small["b_in"][li] = db_in.reshape(N_PROJ)
        small["rpb"][li] = drpb.reshape(N_HEADS, KROWS, GRID_W)[:, :2 * KH - 1, :2 * KW - 1]
        small["pool_w"][li] = dpool_w
        small["pool_scale"][li] = dpool_sc.reshape(D_POOL)
        small["ln1_g"][li], small["ln1_b"][li] = dg1.reshape(D), db1.reshape(D)
        small["ln2_g"][li], small["ln2_b"][li] = dg2.reshape(D), db2.reshape(D)
        small["conv_b"][li] = dcb.reshape(D_FF)
        small["conv_w"][li] = dcw.transpose(1, 0, 2).reshape(3, D_FF)
        token = tok(ex.flush(li, 2, dh))
        if li == 1:
            token = token + tok(ex.replicated_early(small, dh))
    dx, dg_in, db_in0 = ln_bwd(dh, x2, vec(W["ln_in_g"]), "ln_in_bwd", after=token)
    parts = {n: jnp.stack(v_) for n, v_ in small.items()}
    parts["ln_in_g"], parts["ln_in_b"] = dg_in.reshape(D), db_in0.reshape(D)
    return loss_part, dx, parts
```

```python
import numpy as np
import jax
import jax.numpy as jnp
from jax import lax
from jax.experimental import pallas as pl
from jax.experimental.pallas import tpu as pltpu

F32 = jnp.float32
BF16 = jnp.bfloat16
I32 = jnp.int32

D = 1024
DEPTH = 4
GRID_W = 64
N_HEADS = 8
HEAD_DIM = 64
D_ATTN = 512
KH = 8
KW = 16
POOL_WINDOWS = (2, 4, 8, 16)
D_POOL = 512
PGD = 128
D_FF = 2816
PLE_DIM = 256
N_PROJ = 4096
ALPHA = (2 * DEPTH) ** 0.25
LN_EPS = 1e-5
NEG_INF = -1e30
ATT_SCALE = HEAD_DIM ** -0.5
ADAM_LR = 0.001
ADAM_B1 = 0.9
ADAM_B2 = 0.999
ADAM_EPS = 1e-08
ADAM_WD = 0.01
ADAM_STEP = 10

N_DEV = 8
AXES = ("x", "y", "c")
FF_BLK = D_FF // 4
FF_SHARD = D_FF // N_DEV
QROWS = 8
KROWS = 16
QB = QROWS * GRID_W
KB = KROWS * GRID_W
V7X_VMEM_LIMIT = 56 * 2 ** 20
MESH = pl.DeviceIdType.MESH
ANY = pl.BlockSpec(memory_space=pl.ANY)


def _cparams(n_grid):
    return pltpu.CompilerParams(dimension_semantics=("arbitrary",) * n_grid, vmem_limit_bytes=V7X_VMEM_LIMIT)


def _nn(a, b):
    return lax.dot_general(a, b, (((1,), (0,)), ((), ())), preferred_element_type=F32)


def _nt(a, b):
    return lax.dot_general(a, b, (((1,), (1,)), ((), ())), preferred_element_type=F32)


def _tn(a, b):
    return lax.dot_general(a, b, (((0,), (0,)), ((), ())), preferred_element_type=F32)


def _sigmoid(x):
    return 1.0 / (1.0 + jnp.exp(-x))


def _ln_fwd(z, g, b):
    mu = jnp.mean(z, axis=-1, keepdims=True)
    xc = z - mu
    var = jnp.mean(xc * xc, axis=-1, keepdims=True)
    return xc * lax.rsqrt(var + LN_EPS) * g + b


def _ln_bwd(dh, z, g):
    mu = jnp.mean(z, axis=-1, keepdims=True)
    xc = z - mu
    var = jnp.mean(xc * xc, axis=-1, keepdims=True)
    rstd = lax.rsqrt(var + LN_EPS)
    xhat = xc * rstd
    dxh = dh * g
    m1 = jnp.mean(dxh, axis=-1, keepdims=True)
    m2 = jnp.mean(dxh * xhat, axis=-1, keepdims=True)
    return rstd * (dxh - m1 - xhat * m2), dh * xhat


def _colsum(x):
    return jnp.sum(x, axis=0, keepdims=True)


def _lane_cat(ref):
    return jnp.concatenate([ref[j] for j in range(ref.shape[0])], axis=1)


def _row_cat(ref):
    n, r, c = ref.shape
    return ref[...].reshape(n * r, c)


def _shards(n, r, c, li, j_of=None):
    del li
    if j_of is None:
        return pl.BlockSpec((n, r, c), lambda *_: (0, 0, 0))
    return pl.BlockSpec((n, r, c), lambda *g: (j_of(*g), 0, 0))


def _shard(r, c, li, j_of):
    del li
    return pl.BlockSpec((None, r, c), lambda *g: (j_of(*g), 0, 0))


def ln_fwd(x, g, b, name, after=()):
    s = x.shape[0]
    tm = 512
    na = len(after)

    def body(x_ref, g_ref, b_ref, *rest):
        h_ref, hb_ref = rest[na:]
        h = _ln_fwd(x_ref[...], g_ref[...], b_ref[...])
        h_ref[...] = h
        hb_ref[...] = h.astype(BF16)

    row = pl.BlockSpec((tm, D), lambda i: (i, 0))
    vec = pl.BlockSpec((1, D), lambda i: (0, 0))
    return pl.pallas_call(
        body, name=name, grid=(s // tm,), in_specs=[row, vec, vec] + [ANY] * na, out_specs=[row, row],
        out_shape=[jax.ShapeDtypeStruct((s, D), F32), jax.ShapeDtypeStruct((s, D), BF16)],
        compiler_params=_cparams(1))(x, g, b, *after)


def proj_fwd(hb, win, bias, li, name, after=()):
    s = hb.shape[0]
    bn = N_PROJ // N_DEV
    tm = 1024
    pool_shard = (3 * D_ATTN) // bn

    def body(a_ref, w_ref, b_ref, *rest):
        o_ref, u_ref = rest[-2:]
        acc = _nn(a_ref[...], w_ref[...]) + b_ref[...]
        o_ref[...] = acc.astype(BF16)

        @pl.when(pl.program_id(1) == pool_shard)
        def _():
            u_ref[...] = acc

    return pl.pallas_call(
        body, name=name, grid=(s // tm, N_DEV),
        in_specs=[pl.BlockSpec((tm, D), lambda i, j: (i, 0)),
                  _shard(D, bn, li, lambda i, j: j),
                  pl.BlockSpec((1, bn), lambda i, j: (0, j))] + [ANY] * len(after),
        out_specs=[pl.BlockSpec((tm, bn), lambda i, j: (i, j)), pl.BlockSpec((tm, bn), lambda i, j: (i, 0))],
        out_shape=[jax.ShapeDtypeStruct((s, N_PROJ), BF16), jax.ShapeDtypeStruct((s, D_POOL), F32)],
        compiler_params=_cparams(2))(hb, win, bias, *after)


def _attn_types(b, nb):
    first, last = 0, (nb * QROWS - KROWS) * GRID_W
    mid = pl.multiple_of((QROWS * b - KH // 2) * GRID_W, 256)
    return ((b == 0, first), ((b > 0) & (b < nb - 1), mid), (b == nb - 1, last))


def _attn_row(btype, qr):
    lo, delta = ((max(qr - KH // 2, 0), 0), (qr, -(KH // 2)), (min(qr + KH // 2, KH), -KH))[btype]
    return lo, (qr - delta - (KH - 1)) % KROWS, lo - qr + delta + KH - 1


def _row_window(lo):
    pad = (lo % 2) * GRID_W
    return (lo // 2) * 128, KH * GRID_W + 2 * pad, pad


def _lanes(ref, start, width):
    start %= KB
    if start + width <= KB:
        return ref[:, start:start + width]
    return jnp.concatenate([ref[:, start:], ref[:, :start + width - KB]], axis=1)


HALF = QROWS // 2


def _rows_window(btype, qr0, n):
    spans = [_row_window(_attn_row(btype, qr)[0]) for qr in range(qr0, qr0 + n)]
    h0 = min(a0 for a0, _, _ in spans) // 256 * 256
    h1 = -(-max(a0 + w for a0, w, _ in spans) // 256) * 256
    return h0, h1 - h0


def _token_at(k0, h0):
    return k0 + h0 if isinstance(k0, int) else pl.multiple_of(k0 + h0, 256)


def _row_logits(s_ref, e_ref, hh, rows, btype, qr, h0):
    lo, shift, _ = _attn_row(btype, qr)
    a0, w, pad = _row_window(lo)
    e = e_ref.at[hh, shift % 2]
    sb = s_ref[rows, a0 - h0:a0 - h0 + w] + _lanes(e, a0 - (shift - shift % 2) * GRID_W, w)
    if pad:
        lane = lax.broadcasted_iota(I32, (1, w), 1)
        sb = jnp.where((lane >= pad) & (lane < w - pad), sb, NEG_INF)
    return sb, a0 - h0, w, pad


def _store_row(ref, rows, a0, w, val, width):
    if a0:
        ref[rows, 0:a0] = jnp.zeros((GRID_W, a0), ref.dtype)
    ref[rows, a0:a0 + w] = val.astype(ref.dtype)
    if a0 + w < width:
        ref[rows, a0 + w:width] = jnp.zeros((GRID_W, width - a0 - w), ref.dtype)


def attn_fwd(proj, e_tab, li, name):
    s = proj.shape[0]
    nb = s // QB

    def body(q_ref, k_ref, v_ref, e_ref, o_ref, s_ref, p_ref):
        q = q_ref[...] * ATT_SCALE
        lane = lax.broadcasted_iota(I32, (1, 128), 1)

        def block(btype, k0):
            for half in range(2):
                h0, hw = _rows_window(btype, half * HALF, HALF)
                hrows = slice(half * HALF * GRID_W, (half + 1) * HALF * GRID_W)
                kwin = k_ref[pl.ds(_token_at(k0, h0), hw), :]
                vwin = v_ref[pl.ds(_token_at(k0, h0), hw), :]
                acc = jnp.zeros((HALF * GRID_W, 128), F32)
                for hh in range(2):
                    lm = (lane // HEAD_DIM) == hh
                    qh = jnp.where(lm, q[hrows], jnp.zeros_like(q[hrows]))
                    vh = jnp.where(lm, vwin, jnp.zeros_like(vwin))
                    s_ref[:, 0:hw] = _nt(qh, kwin)
                    for r in range(HALF):
                        rows = slice(r * GRID_W, (r + 1) * GRID_W)
                        sb, a0, w, _ = _row_logits(s_ref, e_ref, hh, rows, btype, half * HALF + r, h0)
                        p = jnp.exp(sb - jnp.max(sb, axis=1, keepdims=True))
                        _store_row(p_ref, rows, a0, w, p * (1.0 / jnp.sum(p, axis=1, keepdims=True)), hw)
                    acc = acc + _nn(p_ref[:, 0:hw], vh)
                o_ref[hrows, :] = acc.astype(BF16)

        for btype, (cond, k0) in enumerate(_attn_types(pl.program_id(1), nb)):
            pl.when(cond)(lambda btype=btype, k0=k0: block(btype, k0))

    half_shape = (HALF * GRID_W, (HALF + KH - 1 + 1) * GRID_W)
    return pl.pallas_call(
        body, name=name, grid=(4, nb),
        in_specs=[pl.BlockSpec((QB, 128), lambda j, b: (b, j)),
                  pl.BlockSpec((s, 128), lambda j, b: (0, 4 + j)),
                  pl.BlockSpec((s, 128), lambda j, b: (0, 8 + j)),
                  pl.BlockSpec((None, 2, 2, GRID_W, KB), lambda j, b: (li, j, 0, 0, 0))],
        out_specs=pl.BlockSpec((QB, 128), lambda j, b: (b, j)),
        out_shape=jax.ShapeDtypeStruct((s, D_ATTN), BF16),
        scratch_shapes=[pltpu.VMEM(half_shape, F32), pltpu.VMEM(half_shape, BF16)],
        compiler_params=_cparams(2))(proj, proj, proj, e_tab)


_POOL_PAD = 8


def _pool_counts(s, w):
    t = lax.broadcasted_iota(I32, (s, 1), 0)
    return (jnp.minimum(t + w // 2, s) - jnp.maximum(t - w // 2, 0)).astype(F32)


def _window_sum(x, w, back_first):
    s = x.shape[0]
    z = jnp.zeros((_POOL_PAD, x.shape[1]), F32)
    xe = jnp.concatenate([z, x, z], axis=0)
    n = s + 2 * _POOL_PAD
    acc = xe + pltpu.roll(xe, 1 if back_first else n - 1, 0)
    k = 1
    while 2 * k < w:
        acc = pltpu.roll(acc, k, 0) + pltpu.roll(acc, n - k, 0)
        k *= 2
    return acc[_POOL_PAD:_POOL_PAD + s, :]


def pool_fwd(u, pool_w, pool_scale, name):
    s = u.shape[0]

    def body(u_ref, w_ref, sc_ref, pm_ref, pw_ref):
        for g, w in enumerate(POOL_WINDOWS):
            cols = slice(g * PGD, (g + 1) * PGD)
            ug = u_ref[:, cols]
            pm = (_window_sum(ug, w, True) / _pool_counts(s, w) - ug).astype(BF16)
            pm_ref[:, cols] = pm
            pw_ref[:, cols] = (_nn(pm, w_ref[g]) * sc_ref[:, cols]).astype(BF16)

    full = lambda shape: pl.BlockSpec(shape, lambda i: (0,) * len(shape))
    return pl.pallas_call(
        body, name=name, grid=(1,),
        in_specs=[full((s, D_POOL)), full((4, PGD, PGD)), full((1, D_POOL))],
        out_specs=[full((s, D_POOL)), full((s, D_POOL))],
        out_shape=[jax.ShapeDtypeStruct((s, D_POOL), BF16)] * 2,
        compiler_params=_cparams(1))(u, pool_w, pool_scale)


def merge_fwd(a, pw, wao, wpo, proj, li, name, after=()):
    s = a.shape[0]
    tm, tn = 512, 512
    nt = D // tn
    per = tn // 128

    def body(a_ref, pw_ref, wa_ref, wp_ref, ga_ref, gb_ref, *rest):
        mg_ref, ya_ref, yp_ref = rest[len(after):]
        ya = _nn(a_ref[...], _lane_cat(wa_ref))
        yp = _nn(pw_ref[...], _lane_cat(wp_ref))
        mg = _sigmoid(ga_ref[...].astype(F32)) * ya + _sigmoid(gb_ref[...].astype(F32)) * yp
        mg_ref[...] = mg.astype(BF16)
        ya_ref[...] = ya.astype(BF16)
        yp_ref[...] = yp.astype(BF16)

    act = pl.BlockSpec((tm, D_ATTN), lambda i, j: (i, 0))
    wsp = _shards(per, D_ATTN, 128, li, lambda i, j: j)
    out = pl.BlockSpec((tm, tn), lambda i, j: (i, j))
    ga0 = (3 * D_ATTN + D_POOL) // tn
    return pl.pallas_call(
        body, name=name, grid=(s // tm, nt),
        in_specs=[act, act, wsp, wsp,
                  pl.BlockSpec((tm, tn), lambda i, j: (i, ga0 + j)),
                  pl.BlockSpec((tm, tn), lambda i, j: (i, ga0 + nt + j))] + [ANY] * len(after),
        out_specs=[out, out, out],
        out_shape=[jax.ShapeDtypeStruct((s, D), BF16)] * 3,
        compiler_params=_cparams(2))(a, pw, wao, wpo, proj, proj, *after)


def mix_ln_fwd(mg, wmix, h0, g, b, li, name):
    s = mg.shape[0]
    tm = 256

    def body(mg_ref, w_ref, h0_ref, g_ref, b_ref, z_ref, h_ref, hb_ref):
        z = ALPHA * h0_ref[...] + _nn(mg_ref[...], _row_cat(w_ref))
        h = _ln_fwd(z, g_ref[...], b_ref[...])
        z_ref[...] = z
        h_ref[...] = h
        hb_ref[...] = h.astype(BF16)

    row = pl.BlockSpec((tm, D), lambda i: (i, 0))
    vec = pl.BlockSpec((1, D), lambda i: (0, 0))
    return pl.pallas_call(
        body, name=name, grid=(s // tm,),
        in_specs=[row, _shards(N_DEV, D // N_DEV, D, li), row, vec, vec],
        out_specs=[row, row, row],
        out_shape=[jax.ShapeDtypeStruct((s, D), F32), jax.ShapeDtypeStruct((s, D), F32),
                   jax.ShapeDtypeStruct((s, D), BF16)],
        compiler_params=_cparams(1))(mg, wmix, h0, g, b)


def up_fwd(hb, wup, li, name):
    s = hb.shape[0]
    tm = 1024

    def body(a_ref, w_ref, o_ref):
        o_ref[...] = _nt(a_ref[...], w_ref[...]).astype(BF16)

    return pl.pallas_call(
        body, name=name, grid=(s // tm, N_DEV),
        in_specs=[pl.BlockSpec((tm, D), lambda i, j: (i, 0)), _shard(FF_BLK, D, li, lambda i, j: j)],
        out_specs=pl.BlockSpec((None, tm, FF_BLK), lambda i, j: (j, i, 0)),
        out_shape=jax.ShapeDtypeStruct((N_DEV, s, FF_BLK), BF16),
        compiler_params=_cparams(2))(hb, wup)


_SQRT_HALF = 0.7071067811865476
_INV_SQRT_2PI = 0.3989422804014327


def _shift_rows(x, prev_row, next_row):
    n = x.shape[0]
    r = lax.broadcasted_iota(I32, (n, 1), 0)
    back = jnp.where(r == 0, prev_row, pltpu.roll(x, 1, 0))
    fwd = jnp.where(r == n - 1, next_row, pltpu.roll(x, n - 1, 0))
    return back, fwd


HALO = 16


def _halo_maps(tm, s):
    th = tm // HALO
    return (lambda i: jnp.maximum(i * th - 1, 0)), (lambda i: jnp.minimum((i + 1) * th, s // HALO - 1))


def _slab_specs(tm, s, blk_of):
    before, after = _halo_maps(tm, s)
    main = pl.BlockSpec((None, tm, FF_BLK), lambda c, i: (blk_of(c), i, 0))
    prev = pl.BlockSpec((None, HALO, FF_BLK), lambda c, i: (blk_of(c), before(i), 0))
    nxt = pl.BlockSpec((None, HALO, FF_BLK), lambda c, i: (blk_of(c), after(i), 0))
    return main, prev, nxt


def ffn_act_fwd(up, conv_w, conv_b, name):
    s = up.shape[1]
    tm = 512
    nt = s // tm
    hv_main, _, _ = _slab_specs(tm, s, lambda c: c)
    hg_main, hg_prev, hg_next = _slab_specs(tm, s, lambda c: 4 + c)

    def body(hv_ref, hg_ref, hp_ref, hn_ref, cw_ref, cb_ref, t_ref):
        i = pl.program_id(1)
        hg = hg_ref[...].astype(F32)
        prow = jnp.where(i == 0, 0.0, hp_ref[...].astype(F32)[HALO - 1:HALO, :])
        nrow = jnp.where(i == nt - 1, 0.0, hn_ref[...].astype(F32)[0:1, :])
        back, fwd = _shift_rows(hg, prow, nrow)
        c = back * cw_ref[0:1, :] + hg * cw_ref[1:2, :] + fwd * cw_ref[2:3, :] + cb_ref[...]
        act = 0.5 * c * (1.0 + lax.erf(c * _SQRT_HALF))
        t_ref[...] = (act * hv_ref[...].astype(F32)).astype(BF16)

    return pl.pallas_call(
        body, name=name, grid=(4, nt),
        in_specs=[hv_main, hg_main, hg_prev, hg_next,
                  pl.BlockSpec((None, 3, FF_BLK), lambda c, i: (c, 0, 0)),
                  pl.BlockSpec((None, 1, FF_BLK), lambda c, i: (c, 0, 0))],
        out_specs=pl.BlockSpec((None, tm, FF_BLK), lambda c, i: (c, i, 0)),
        out_shape=jax.ShapeDtypeStruct((4, s, FF_BLK), BF16),
        compiler_params=_cparams(2))(up, up, up, up, conv_w, conv_b)


def down_ple_ln_fwd(t, wdown, hb, wpg, pb, wpp, h1, g, b, li, name):
    s = hb.shape[0]
    tm = 256

    def body(t_ref, wd_ref, hb_ref, wpg_ref, p_ref, wpp_ref, h1_ref, g_ref, b_ref,
             z_ref, h_ref, hbo_ref, pg_ref, pp_ref):
        wd = _row_cat(wd_ref)
        ffn = _nn(t_ref[0], wd[0:FF_BLK, :])
        for c in range(1, 4):
            ffn = ffn + _nn(t_ref[c], wd[c * FF_BLK:(c + 1) * FF_BLK, :])
        pg = _nn(hb_ref[...], _row_cat(wpg_ref))
        pp = _nn(p_ref[...], _lane_cat(wpp_ref))
        z = ALPHA * h1_ref[...] + ffn + _sigmoid(pg) * pp
        h = _ln_fwd(z, g_ref[...], b_ref[...])
        z_ref[...] = z
        h_ref[...] = h
        hbo_ref[...] = h.astype(BF16)
        pg_ref[...] = pg.astype(BF16)
        pp_ref[...] = pp.astype(BF16)

    row = pl.BlockSpec((tm, D), lambda i: (i, 0))
    vec = pl.BlockSpec((1, D), lambda i: (0, 0))
    return pl.pallas_call(
        body, name=name, grid=(s // tm,),
        in_specs=[pl.BlockSpec((4, tm, FF_BLK), lambda i: (0, i, 0)),
                  _shards(N_DEV, FF_SHARD, D, li),
                  row, _shards(N_DEV, D // N_DEV, D, li),
                  pl.BlockSpec((tm, PLE_DIM), lambda i: (i, 0)),
                  _shards(N_DEV, PLE_DIM, 128, li),
                  row, vec, vec],
        out_specs=[row] * 5,
        out_shape=[jax.ShapeDtypeStruct((s, D), F32), jax.ShapeDtypeStruct((s, D), F32),
                   jax.ShapeDtypeStruct((s, D), BF16), jax.ShapeDtypeStruct((s, D), BF16),
                   jax.ShapeDtypeStruct((s, D), BF16)],
        compiler_params=_cparams(1))(t, wdown, hb, wpg, pb, wpp, h1, g, b)


def loss_bwd(h, target, name):
    s = h.shape[0]
    tm = 512

    def body(h_ref, t_ref, dh_ref, l_ref):
        @pl.when(pl.program_id(0) == 0)
        def _():
            l_ref[...] = jnp.zeros_like(l_ref)
        e = h_ref[...] - t_ref[...]
        dh_ref[...] = e * (1.0 / D)
        l_ref[...] += 0.5 * jnp.sum(jnp.mean(e * e, axis=-1, keepdims=True), axis=0, keepdims=True)

    row = pl.BlockSpec((tm, D), lambda i: (i, 0))
    return pl.pallas_call(
        body, name=name, grid=(s // tm,), in_specs=[row, row],
        out_specs=[row, pl.BlockSpec((1, 1), lambda i: (0, 0))],
        out_shape=[jax.ShapeDtypeStruct((s, D), F32), jax.ShapeDtypeStruct((1, 1), F32)],
        compiler_params=_cparams(1))(h, target)


def ln_bwd(dh, z, g, name, after=()):
    s = dh.shape[0]
    tm = 512
    na = len(after)

    def body(dh_ref, z_ref, g_ref, *rest):
        dz_ref, dg_ref, db_ref = rest[na:]

        @pl.when(pl.program_id(0) == 0)
        def _():
            dg_ref[...] = jnp.zeros_like(dg_ref)
            db_ref[...] = jnp.zeros_like(db_ref)
        dh = dh_ref[...]
        dz, dgx = _ln_bwd(dh, z_ref[...], g_ref[...])
        dz_ref[...] = dz
        dg_ref[...] += _colsum(dgx)
        db_ref[...] += _colsum(dh)

    row = pl.BlockSpec((tm, D), lambda i: (i, 0))
    vec = pl.BlockSpec((1, D), lambda i: (0, 0))
    return pl.pallas_call(
        body, name=name, grid=(s // tm,), in_specs=[row, row, vec] + [ANY] * na, out_specs=[row, vec, vec],
        out_shape=[jax.ShapeDtypeStruct((s, D), F32), jax.ShapeDtypeStruct((1, D), F32),
                   jax.ShapeDtypeStruct((1, D), F32)],
        compiler_params=_cparams(1))(dh, z, g, *after)


def ln2_ple_bwd(dh, z, g, pg, pp, name, after=()):
    s = dh.shape[0]
    tm = 512
    na = len(after)

    def body(dh_ref, z_ref, g_ref, pg_ref, pp_ref, *rest):
        dz_ref, dzb_ref, dpg_ref, dpp_ref, dg_ref, db_ref = rest[na:]

        @pl.when(pl.program_id(0) == 0)
        def _():
            dg_ref[...] = jnp.zeros_like(dg_ref)
            db_ref[...] = jnp.zeros_like(db_ref)
        dh = dh_ref[...]
        dz, dgx = _ln_bwd(dh, z_ref[...], g_ref[...])
        sg = _sigmoid(pg_ref[...].astype(F32))
        dz_ref[...] = dz
        dzb_ref[...] = dz.astype(BF16)
        dpg_ref[...] = (dz * pp_ref[...].astype(F32) * sg * (1.0 - sg)).astype(BF16)
        dpp_ref[...] = (dz * sg).astype(BF16)
        dg_ref[...] += _colsum(dgx)
        db_ref[...] += _colsum(dh)

    row = pl.BlockSpec((tm, D), lambda i: (i, 0))
    vec = pl.BlockSpec((1, D), lambda i: (0, 0))
    return pl.pallas_call(
        body, name=name, grid=(s // tm,), in_specs=[row, row, vec, row, row] + [ANY] * na,
        out_specs=[row, row, row, row, vec, vec],
        out_shape=[jax.ShapeDtypeStruct((s, D), F32)] + [jax.ShapeDtypeStruct((s, D), BF16)] * 3
        + [jax.ShapeDtypeStruct((1, D), F32)] * 2,
        compiler_params=_cparams(1))(dh, z, g, pg, pp, *after)


def wgrad_pair(rows_a, rows_dy, cols, name):
    specs, args, outs, shapes, kinds = [], [], [], [], []
    if rows_a is not None:
        s, k = rows_a.shape
        n = rows_dy.shape[1]
        specs += [pl.BlockSpec((s, k // N_DEV), lambda j: (0, j)), pl.BlockSpec((s, n), lambda j: (0, 0))]
        args += [rows_a, rows_dy]
        outs.append(pl.BlockSpec((None, k // N_DEV, n), lambda j: (j, 0, 0)))
        shapes.append(jax.ShapeDtypeStruct((N_DEV, k // N_DEV, n), BF16))
    for a, dy in cols:
        s, k = a.shape
        n = dy.shape[1]
        specs += [pl.BlockSpec((s, k), lambda j: (0, 0)), pl.BlockSpec((s, n // N_DEV), lambda j: (0, j))]
        args += [a, dy]
        outs.append(pl.BlockSpec((None, k, n // N_DEV), lambda j: (j, 0, 0)))
        shapes.append(jax.ShapeDtypeStruct((N_DEV, k, n // N_DEV), BF16))
    n_pairs = len(shapes)

    def body(*refs):
        for i in range(n_pairs):
            refs[2 * n_pairs + i][...] = _tn(refs[2 * i][...], refs[2 * i + 1][...]).astype(BF16)

    return pl.pallas_call(body, name=name, grid=(N_DEV,), in_specs=specs, out_specs=outs, out_shape=shapes,
                          compiler_params=_cparams(1))(*args)


def wgrad_in(a, pieces, name):
    s, k = a.shape
    bn = N_PROJ // N_DEV
    n_narrow = 4

    def body(a_ref, *refs):
        dy_refs, (o_ref, cs_ref) = refs[:6], refs[6:]
        j = pl.program_id(0)

        def emit(dy_ref):
            dy = dy_ref[...]
            o_ref[...] = _tn(a_ref[...], dy).astype(BF16)
            cs_ref[...] = _colsum(dy.astype(F32))

        for idx in range(n_narrow):
            pl.when(j == idx)(lambda idx=idx: emit(dy_refs[idx]))
        pl.when((j >= n_narrow) & (j < n_narrow + 2))(lambda: emit(dy_refs[4]))
        pl.when(j >= n_narrow + 2)(lambda: emit(dy_refs[5]))

    narrow = pl.BlockSpec((s, bn), lambda j: (0, 0))
    return pl.pallas_call(
        body, name=name, grid=(N_DEV,),
        in_specs=[pl.BlockSpec((s, k), lambda j: (0, 0))] + [narrow] * n_narrow
        + [pl.BlockSpec((s, bn), lambda j: (0, jnp.clip(j - n_narrow, 0, 1))),
           pl.BlockSpec((s, bn), lambda j: (0, jnp.clip(j - n_narrow - 2, 0, 1)))],
        out_specs=[pl.BlockSpec((None, k, bn), lambda j: (j, 0, 0)), pl.BlockSpec((1, bn), lambda j: (0, j))],
        out_shape=[jax.ShapeDtypeStruct((N_DEV, k, bn), BF16), jax.ShapeDtypeStruct((1, N_PROJ), F32)],
        compiler_params=_cparams(1))(a, *pieces)


def wgrad_down(t, dy, name):
    _, s, k = t.shape
    n = dy.shape[1]

    def body(a_ref, dy_ref, o_ref):
        o_ref[...] = _tn(a_ref[...], dy_ref[...]).astype(BF16)

    return pl.pallas_call(
        body, name=name, grid=(4,),
        in_specs=[pl.BlockSpec((None, s, k), lambda j: (j, 0, 0)), pl.BlockSpec((s, n), lambda j: (0, 0))],
        out_specs=pl.BlockSpec((None, k, n), lambda j: (j, 0, 0)),
        out_shape=jax.ShapeDtypeStruct((4, k, n), BF16),
        compiler_params=_cparams(1))(t, dy)


def wgrad_up(a, dhv, dhg, name):
    s, k = a.shape

    def body(a_ref, dv_ref, dg_ref, o_ref):
        j = pl.program_id(0)

        @pl.when(j < 4)
        def _():
            o_ref[...] = _tn(dv_ref[...], a_ref[...]).astype(BF16)

        @pl.when(j >= 4)
        def _():
            o_ref[...] = _tn(dg_ref[...], a_ref[...]).astype(BF16)

    return pl.pallas_call(
        body, name=name, grid=(N_DEV,),
        in_specs=[pl.BlockSpec((s, k), lambda j: (0, 0)),
                  pl.BlockSpec((None, s, FF_BLK), lambda j: (jnp.minimum(j, 3), 0, 0)),
                  pl.BlockSpec((None, s, FF_BLK), lambda j: (jnp.maximum(j - 4, 0), 0, 0))],
        out_specs=pl.BlockSpec((None, FF_BLK, k), lambda j: (j, 0, 0)),
        out_shape=jax.ShapeDtypeStruct((N_DEV, FF_BLK, k), BF16),
        compiler_params=_cparams(1))(a, dhv, dhg)


def ffn_act_bwd(dzb, wdown, up, conv_w, conv_b, li, name):
    s = up.shape[1]
    tm = 512
    nt = s // tm
    before, after = _halo_maps(tm, s)
    hv_main, hv_prev, hv_next = _slab_specs(tm, s, lambda c: c)
    hg_main, hg_prev, hg_next = _slab_specs(tm, s, lambda c: 4 + c)

    def dc_of(dz, wd, hv, hg, back, fwd, cw_ref, cb_ref):
        dt = _nt(dz, wd)
        c = back * cw_ref[0:1, :] + hg * cw_ref[1:2, :] + fwd * cw_ref[2:3, :] + cb_ref[...]
        cdf = 0.5 * (1.0 + lax.erf(c * _SQRT_HALF))
        pdf = jnp.exp(-0.5 * c * c) * _INV_SQRT_2PI
        return dt, c * cdf, dt * hv * (cdf + c * pdf)

    def body(dz_ref, dzp_ref, dzn_ref, wd_ref, hv_ref, hvp_ref, hvn_ref, hg_ref, hgp_ref, hgn_ref, cw_ref, cb_ref,
             dhv_ref, dhg_ref, dcw_ref, dcb_ref):
        i = pl.program_id(1)

        @pl.when(i == 0)
        def _():
            dcw_ref[...] = jnp.zeros_like(dcw_ref)
            dcb_ref[...] = jnp.zeros_like(dcb_ref)

        wd = _row_cat(wd_ref)
        hg = hg_ref[...].astype(F32)
        hgp = hgp_ref[...].astype(F32)
        hgn = hgn_ref[...].astype(F32)
        first, last = i == 0, i == nt - 1
        e = HALO - 1
        back, fwd = _shift_rows(hg, jnp.where(first, 0.0, hgp[e:e + 1, :]), jnp.where(last, 0.0, hgn[0:1, :]))
        dt, act, dc = dc_of(dz_ref[...], wd, hv_ref[...].astype(F32), hg, back, fwd, cw_ref, cb_ref)
        dhv_ref[...] = (dt * act).astype(BF16)
        bp, fp = _shift_rows(hgp, hgp[0:1, :], hg[0:1, :])
        _, _, dcp = dc_of(dzp_ref[...], wd, hvp_ref[...].astype(F32), hgp, bp, fp, cw_ref, cb_ref)
        bn, fn = _shift_rows(hgn, hg[tm - 1:tm, :], hgn[e:e + 1, :])
        _, _, dcn = dc_of(dzn_ref[...], wd, hvn_ref[...].astype(F32), hgn, bn, fn, cw_ref, cb_ref)
        dc_back, dc_fwd = _shift_rows(dc, jnp.where(first, 0.0, dcp[e:e + 1, :]), jnp.where(last, 0.0, dcn[0:1, :]))
        dhg_ref[...] = (dc_fwd * cw_ref[0:1, :] + dc * cw_ref[1:2, :] + dc_back * cw_ref[2:3, :]).astype(BF16)
        dcw_ref[0:1, :] += _colsum(dc * back)
        dcw_ref[1:2, :] += _colsum(dc * hg)
        dcw_ref[2:3, :] += _colsum(dc * fwd)
        dcb_ref[...] += _colsum(dc)

    out_slab = pl.BlockSpec((None, tm, FF_BLK), lambda c, i: (c, i, 0))
    cw_spec = pl.BlockSpec((None, 3, FF_BLK), lambda c, i: (c, 0, 0))
    cb_spec = pl.BlockSpec((None, 1, FF_BLK), lambda c, i: (c, 0, 0))
    return pl.pallas_call(
        body, name=name, grid=(4, nt),
        in_specs=[pl.BlockSpec((tm, D), lambda c, i: (i, 0)),
                  pl.BlockSpec((HALO, D), lambda c, i: (before(i), 0)),
                  pl.BlockSpec((HALO, D), lambda c, i: (after(i), 0)),
                  _shards(2, FF_SHARD, D, li, lambda c, i: c),
                  hv_main, hv_prev, hv_next, hg_main, hg_prev, hg_next, cw_spec, cb_spec],
        out_specs=[out_slab, out_slab, cw_spec, cb_spec],
        out_shape=[jax.ShapeDtypeStruct((4, s, FF_BLK), BF16), jax.ShapeDtypeStruct((4, s, FF_BLK), BF16),
                   jax.ShapeDtypeStruct((4, 3, FF_BLK), F32), jax.ShapeDtypeStruct((4, 1, FF_BLK), F32)],
        compiler_params=_cparams(2))(dzb, dzb, dzb, wdown, up, up, up, up, up, up, conv_w, conv_b)


def dh1_ln1_bwd(dz2, dpg, wpg, dhv, dhg, wup, z1, g1, li, name, after=()):
    s = dz2.shape[0]
    tm = 256
    na = len(after)

    def body(dz2_ref, dpg_ref, wpg_ref, dhv_ref, dhg_ref, wup_ref, z1_ref, g_ref, *rest):
        dz_ref, dzb_ref, dg_ref, db_ref = rest[na:]

        @pl.when(pl.program_id(0) == 0)
        def _():
            dg_ref[...] = jnp.zeros_like(dg_ref)
            db_ref[...] = jnp.zeros_like(db_ref)
        dh = ALPHA * dz2_ref[...] + _nt(dpg_ref[...], _row_cat(wpg_ref))
        for c in range(4):
            dh = dh + _nn(dhv_ref[c], wup_ref[c]) + _nn(dhg_ref[c], wup_ref[4 + c])
        dz, dgx = _ln_bwd(dh, z1_ref[...], g_ref[...])
        dz_ref[...] = dz
        dzb_ref[...] = dz.astype(BF16)
        dg_ref[...] += _colsum(dgx)
        db_ref[...] += _colsum(dh)

    row = pl.BlockSpec((tm, D), lambda i: (i, 0))
    vec = pl.BlockSpec((1, D), lambda i: (0, 0))
    slab = pl.BlockSpec((4, tm, FF_BLK), lambda i: (0, i, 0))
    return pl.pallas_call(
        body, name=name, grid=(s // tm,),
        in_specs=[row, row, _shards(N_DEV, D // N_DEV, D, li), slab, slab, _shards(N_DEV, FF_BLK, D, li), row, vec]
        + [ANY] * na,
        out_specs=[row, row, vec, vec],
        out_shape=[jax.ShapeDtypeStruct((s, D), F32), jax.ShapeDtypeStruct((s, D), BF16),
                   jax.ShapeDtypeStruct((1, D), F32), jax.ShapeDtypeStruct((1, D), F32)],
        compiler_params=_cparams(1))(dz2, dpg, wpg, dhv, dhg, wup, z1, g1, *after)


def merge_bwd(dz1b, wmix, proj, ya, yp, li, name, after=()):
    s = dz1b.shape[0]
    tm, tn = 512, 512
    nt = D // tn
    per = tn // (D // N_DEV)
    ga0 = (3 * D_ATTN + D_POOL) // tn

    def body(dz_ref, w_ref, ga_ref, gb_ref, ya_ref, yp_ref, *rest):
        dya_ref, dyp_ref, dga_ref, dgb_ref = rest[len(after):]
        dm = _nt(dz_ref[...], _row_cat(w_ref))
        sa = _sigmoid(ga_ref[...].astype(F32))
        sb = _sigmoid(gb_ref[...].astype(F32))
        dya_ref[...] = (dm * sa).astype(BF16)
        dyp_ref[...] = (dm * sb).astype(BF16)
        dga_ref[...] = (dm * ya_ref[...].astype(F32) * sa * (1.0 - sa)).astype(BF16)
        dgb_ref[...] = (dm * yp_ref[...].astype(F32) * sb * (1.0 - sb)).astype(BF16)

    tile = pl.BlockSpec((tm, tn), lambda i, j: (i, j))
    return pl.pallas_call(
        body, name=name, grid=(s // tm, nt),
        in_specs=[pl.BlockSpec((tm, D), lambda i, j: (i, 0)),
                  _shards(per, D // N_DEV, D, li, lambda i, j: j),
                  pl.BlockSpec((tm, tn), lambda i, j: (i, ga0 + j)),
                  pl.BlockSpec((tm, tn), lambda i, j: (i, ga0 + nt + j)),
                  tile, tile] + [ANY] * len(after),
        out_specs=[tile] * 4,
        out_shape=[jax.ShapeDtypeStruct((s, D), BF16)] * 4,
        compiler_params=_cparams(2))(dz1b, wmix, proj, proj, ya, yp, *after)


def attn_out_bwd(dya, wao, li, name, after=()):
    s = dya.shape[0]
    tm = 512

    def body(d_ref, w_ref, *rest):
        rest[-1][...] = _nt(d_ref[...], _lane_cat(w_ref)).astype(BF16)

    return pl.pallas_call(
        body, name=name, grid=(s // tm,),
        in_specs=[pl.BlockSpec((tm, D), lambda i: (i, 0)), _shards(N_DEV, D_ATTN, 128, li)] + [ANY] * len(after),
        out_specs=pl.BlockSpec((tm, D_ATTN), lambda i: (i, 0)),
        out_shape=jax.ShapeDtypeStruct((s, D_ATTN), BF16),
        compiler_params=_cparams(1))(dya, wao, *after)


def pool_bwd(dyp, wpo, pm, pool_w, pool_scale, li, name):
    s = dyp.shape[0]

    def body(dyp_ref, wpo_ref, pm_ref, w_ref, sc_ref, du_ref, dw_ref, dsc_ref):
        wpo = _lane_cat(wpo_ref)
        dyp = dyp_ref[...]
        for g, w in enumerate(POOL_WINDOWS):
            cols = slice(g * PGD, (g + 1) * PGD)
            dpw = _nt(dyp, wpo[g * PGD:(g + 1) * PGD, :])
            pmg = pm_ref[:, cols]
            dsc_ref[:, cols] = _colsum(dpw * _nn(pmg, w_ref[g]))
            dpmw = (dpw * sc_ref[:, cols]).astype(BF16)
            dw_ref[g] = _tn(pmg, dpmw)
            dpm = _nt(dpmw, w_ref[g])
            du_ref[:, cols] = (_window_sum(dpm / _pool_counts(s, w), w, False) - dpm).astype(BF16)

    full = lambda shape: pl.BlockSpec(shape, lambda i: (0,) * len(shape))
    return pl.pallas_call(
        body, name=name, grid=(1,),
        in_specs=[full((s, D)), _shards(N_DEV, D_POOL, 128, li), full((s, D_POOL)), full((4, PGD, PGD)),
                  full((1, D_POOL))],
        out_specs=[full((s, D_POOL)), full((4, PGD, PGD)), full((1, D_POOL))],
        out_shape=[jax.ShapeDtypeStruct((s, D_POOL), BF16), jax.ShapeDtypeStruct((4, PGD, PGD), F32),
                   jax.ShapeDtypeStruct((1, D_POOL), F32)],
        compiler_params=_cparams(1))(dyp, wpo, pm, pool_w, pool_scale)


def attn_bwd(proj, da, e_rev, li, name, after=()):
    s = proj.shape[0]
    nb = s // QB
    skew = GRID_W + (GRID_W - KW)
    half_shape = (HALF * GRID_W, (HALF + KH) * GRID_W)

    def body(q_ref, k_ref, v_ref, do_ref, e_ref, *rest):
        dq_ref, dk_ref, dv_ref, g_ref, s_ref, dp_ref, ds_ref, p_ref, dkt_acc, dvt_acc = rest[len(after):]
        b = pl.program_id(1)

        @pl.when(b == 0)
        def _():
            dkt_acc[...] = jnp.zeros_like(dkt_acc)
            dvt_acc[...] = jnp.zeros_like(dvt_acc)
            g_ref[...] = jnp.zeros_like(g_ref)

        ri = lax.broadcasted_iota(I32, (QB, QB), 0)
        ci = lax.broadcasted_iota(I32, (QB, QB), 1)
        rev = jnp.where(ri + ci == QB - 1, 1.0, 0.0).astype(BF16)
        q = _nn(rev, q_ref[...]).astype(BF16) * ATT_SCALE
        do = _nn(rev, do_ref[...]).astype(BF16)
        lane = lax.broadcasted_iota(I32, (1, 128), 1)

        def block(btype, k0):
            b0, bw = _rows_window(btype, 0, QROWS)
            dqs = [jnp.zeros((HALF * GRID_W, 128), F32) for _ in range(2)]
            for hh in range(2):
                lm = (lane // HEAD_DIM) == hh
                qh = jnp.where(lm, q, jnp.zeros_like(q))
                doh = jnp.where(lm, do, jnp.zeros_like(do))
                g = jnp.zeros((1, KB), F32)
                for rh in range(2):
                    h0, hw = _rows_window(btype, QROWS - (rh + 1) * HALF, HALF)
                    hrows = slice(rh * HALF * GRID_W, (rh + 1) * HALF * GRID_W)
                    kwin = k_ref[pl.ds(_token_at(k0, h0), hw), :]
                    vwin = v_ref[pl.ds(_token_at(k0, h0), hw), :]
                    s_ref[:, 0:hw] = _nt(qh[hrows], kwin)
                    dp_ref[:, 0:hw] = _nt(doh[hrows], vwin)
                    for r in range(HALF):
                        ib = rh * HALF + r
                        qr = QROWS - 1 - ib
                        rows = slice(r * GRID_W, (r + 1) * GRID_W)
                        sb, a0, w, pad = _row_logits(s_ref, e_ref, hh, rows, btype, qr, h0)
                        p = jnp.exp(sb - jnp.max(sb, axis=1, keepdims=True))
                        p = p * (1.0 / jnp.sum(p, axis=1, keepdims=True))
                        dp = dp_ref[rows, a0:a0 + w]
                        ds = p * (dp - jnp.sum(p * dp, axis=1, keepdims=True))
                        brows = slice(ib * GRID_W, (ib + 1) * GRID_W)
                        _store_row(ds_ref, brows, a0 + h0 - b0, w, ds, bw)
                        _store_row(p_ref, brows, a0 + h0 - b0, w, p, bw)
                        t = jnp.sum(pltpu.roll(ds, w - skew, 1, stride=1, stride_axis=0), axis=0, keepdims=True)
                        t = t[:, :KH * GRID_W] if pad else pltpu.roll(t, GRID_W, 1)
                        i0 = _attn_row(btype, qr)[2]
                        g = g + pltpu.roll(jnp.concatenate([t, jnp.zeros_like(t)], axis=1), i0 * GRID_W, 1)
                    kh = jnp.where(lm, kwin, jnp.zeros_like(kwin))
                    dqs[rh] = dqs[rh] + _nn(ds_ref[hrows, h0 - b0:h0 - b0 + hw], kh) * ATT_SCALE
                g_ref[hh] += g
                dkt_acc[:, pl.ds(_token_at(k0, b0), bw)] += _tn(qh, ds_ref[:, 0:bw])
                dvt_acc[:, pl.ds(_token_at(k0, b0), bw)] += _tn(doh, p_ref[:, 0:bw])
            dq_ref[...] = _nn(rev, jnp.concatenate([d.astype(BF16) for d in dqs], axis=0)).astype(BF16)

        for btype, (cond, k0) in enumerate(_attn_types(b, nb)):
            pl.when(cond)(lambda btype=btype, k0=k0: block(btype, k0))

        @pl.when(b == nb - 1)
        def _():
            dk_ref[...] = dkt_acc[...].T.astype(BF16)
            dv_ref[...] = dvt_acc[...].T.astype(BF16)

    col = pl.BlockSpec((s, 128), lambda j, b: (0, j))
    return pl.pallas_call(
        body, name=name, grid=(4, nb),
        in_specs=[pl.BlockSpec((QB, 128), lambda j, b: (b, j)),
                  pl.BlockSpec((s, 128), lambda j, b: (0, 4 + j)),
                  pl.BlockSpec((s, 128), lambda j, b: (0, 8 + j)),
                  pl.BlockSpec((QB, 128), lambda j, b: (b, j)),
                  pl.BlockSpec((None, 2, 2, GRID_W, KB), lambda j, b: (li, j, 0, 0, 0))] + [ANY] * len(after),
        out_specs=[pl.BlockSpec((QB, 128), lambda j, b: (b, j)), col, col,
                   pl.BlockSpec((2, 1, KB), lambda j, b: (j, 0, 0))],
        out_shape=[jax.ShapeDtypeStruct((s, D_ATTN), BF16)] * 3 + [jax.ShapeDtypeStruct((N_HEADS, 1, KB), F32)],
        scratch_shapes=[pltpu.VMEM(half_shape, F32), pltpu.VMEM(half_shape, F32), pltpu.VMEM((QB, KB), BF16),
                        pltpu.VMEM((QB, KB), BF16), pltpu.VMEM((128, s), F32), pltpu.VMEM((128, s), F32)],
        compiler_params=_cparams(2))(proj, proj, proj, da, e_rev, *after)


def dh0_bwd(dz1, pieces, win, li, name, after=()):
    s = dz1.shape[0]
    tm = 256
    bn = N_PROJ // N_DEV

    def body(dz_ref, q_ref, k_ref, v_ref, u_ref, ga_ref, gb_ref, w_ref, *rest):
        acc = ALPHA * dz_ref[...]
        for j, ref in enumerate((q_ref, k_ref, v_ref, u_ref)):
            acc = acc + _nt(ref[...], w_ref[j])
        for j, ref in ((4, ga_ref), (6, gb_ref)):
            acc = acc + _nt(ref[:, 0:bn], w_ref[j]) + _nt(ref[:, bn:2 * bn], w_ref[j + 1])
        rest[-1][...] = acc

    row = pl.BlockSpec((tm, D), lambda i: (i, 0))
    narrow = pl.BlockSpec((tm, bn), lambda i: (i, 0))
    return pl.pallas_call(
        body, name=name, grid=(s // tm,),
        in_specs=[row] + [narrow] * 4 + [row, row, _shards(N_DEV, D, bn, li)] + [ANY] * len(after),
        out_specs=row, out_shape=jax.ShapeDtypeStruct((s, D), F32),
        compiler_params=_cparams(1))(dz1, *pieces, win, *after)


def _coords():
    return lax.axis_index("x"), lax.axis_index("y"), lax.axis_index("c")


def _dev_index(px, py, pc):
    return 4 * px + 2 * py + pc


HBM = pl.BlockSpec(memory_space=pltpu.HBM)
SEM = pl.BlockSpec(memory_space=pltpu.SEMAPHORE)
_EFFECT = pltpu.SideEffectType.DATAFLOW_SIDE_EFFECTING
_TOKEN = jax.ShapeDtypeStruct((8, 128), F32)


def _in_hbm(a):
    return pltpu.with_memory_space_constraint(a, pltpu.HBM)


def _hbm_like(a):
    return pltpu.HBM(a.shape, a.dtype)


def _peers(x, y, c):
    return [(x, y, 1 - c), (1 - x, y, c), (x, 1 - y, c), (1 - x, 1 - y, c)]


def ag_start(lands, after, name):
    n = len(lands)

    def body(*refs):
        land = refs[:n]
        send_sem, recv_sem, token = refs[n + 1], refs[n + 2], refs[-1]
        x, y, c = _coords()
        me = _dev_index(x, y, c)
        for k, peer in enumerate(_peers(x, y, c)):
            for a in range(n):
                pltpu.make_async_remote_copy(src_ref=land[a].at[me], dst_ref=land[a].at[me], send_sem=send_sem.at[k],
                                             recv_sem=recv_sem.at[k], device_id=peer, device_id_type=MESH).start()
        token[...] = jnp.zeros_like(token)

    res = pl.pallas_call(
        body, name=name,
        out_shape=(pltpu.SemaphoreType.DMA((4,)), pltpu.SemaphoreType.DMA((4,)), *[_hbm_like(l) for l in lands], _TOKEN),
        in_specs=[HBM] * n + [ANY], out_specs=(SEM, SEM, *[HBM] * n, pl.BlockSpec(memory_space=pltpu.VMEM)),
        input_output_aliases={a: 2 + a for a in range(n)},
        compiler_params=pltpu.CompilerParams(has_side_effects=_EFFECT),
    )(*[_in_hbm(l) for l in lands], after)
    return res[0], res[1], list(res[2:2 + n]), res[-1]


def ag_forward(send_sem, recv_sem, lands, after, name):
    n = len(lands)

    def body(*refs):
        send_sem, recv_sem = refs[0], refs[1]
        land = refs[2:2 + n]
        fsend, frecv = refs[3 + n], refs[4 + n]
        x, y, c = _coords()
        peers = _peers(x, y, c)
        for k in range(1, 4):
            blk = _dev_index(*peers[k])
            for a in range(n):
                pltpu.make_async_remote_copy(src_ref=land[a].at[blk], dst_ref=land[a].at[blk], send_sem=send_sem.at[k],
                                             recv_sem=recv_sem.at[k], device_id=peers[k], device_id_type=MESH).wait_recv()
        for k in range(1, 4):
            blk = _dev_index(*peers[k])
            for a in range(n):
                pltpu.make_async_remote_copy(src_ref=land[a].at[blk], dst_ref=land[a].at[blk], send_sem=fsend.at[k - 1],
                                             recv_sem=frecv.at[k - 1], device_id=peers[0], device_id_type=MESH).start()

    res = pl.pallas_call(
        body, name=name,
        out_shape=(pltpu.SemaphoreType.DMA((3,)), pltpu.SemaphoreType.DMA((3,)), *[_hbm_like(l) for l in lands]),
        in_specs=[SEM, SEM, *[HBM] * n, ANY], out_specs=(SEM, SEM, *[HBM] * n),
        input_output_aliases={2 + a: 2 + a for a in range(n)},
        compiler_params=pltpu.CompilerParams(has_side_effects=_EFFECT),
    )(send_sem, recv_sem, *lands, after)
    return res[0], res[1], list(res[2:])


def ag_finish(send_sem, recv_sem, fsend, frecv, lands, after, name):
    n = len(lands)

    def body(*refs):
        send_sem, recv_sem, fsend, frecv = refs[:4]
        land = refs[4:4 + n]
        x, y, c = _coords()
        me = _dev_index(x, y, c)
        peers = _peers(x, y, c)
        for k in range(4):
            for a in range(n):
                pltpu.make_async_remote_copy(src_ref=land[a].at[me], dst_ref=land[a].at[me], send_sem=send_sem.at[k],
                                             recv_sem=recv_sem.at[k], device_id=peers[k], device_id_type=MESH).wait_send()
        sib = _dev_index(*peers[0])
        for a in range(n):
            pltpu.make_async_remote_copy(src_ref=land[a].at[sib], dst_ref=land[a].at[sib], send_sem=send_sem.at[0],
                                         recv_sem=recv_sem.at[0], device_id=peers[0], device_id_type=MESH).wait_recv()
        for k in range(1, 4):
            mine = _dev_index(*peers[k])
            theirs = _dev_index(peers[k][0], peers[k][1], 1 - c)
            for a in range(n):
                pltpu.make_async_remote_copy(src_ref=land[a].at[mine], dst_ref=land[a].at[theirs], send_sem=fsend.at[k - 1],
                                             recv_sem=frecv.at[k - 1], device_id=peers[0], device_id_type=MESH).wait()

    res = pl.pallas_call(
        body, name=name, out_shape=tuple(_hbm_like(l) for l in lands),
        in_specs=[SEM] * 4 + [HBM] * n + [ANY], out_specs=tuple([HBM] * n),
        input_output_aliases={4 + a: a for a in range(n)},
        compiler_params=pltpu.CompilerParams(has_side_effects=_EFFECT),
    )(send_sem, recv_sem, fsend, frecv, *lands, after)
    return list(res)


def rs_start(psums, name):
    n = len(psums)
    lands = [lax.empty(p.shape, p.dtype) for p in psums]

    def body(*refs):
        src, land = refs[:n], refs[n:2 * n]
        send_sem, recv_sem, token = refs[2 * n], refs[2 * n + 1], refs[-1]
        peers = _peers(*_coords())
        for k in range(3):
            for a in range(n):
                pltpu.make_async_remote_copy(src_ref=src[a].at[k], dst_ref=land[a].at[k], send_sem=send_sem.at[k],
                                             recv_sem=recv_sem.at[k], device_id=peers[k + 1], device_id_type=MESH).start()
        token[...] = jnp.zeros_like(token)

    res = pl.pallas_call(
        body, name=name,
        out_shape=(pltpu.SemaphoreType.DMA((3,)), pltpu.SemaphoreType.DMA((3,)), *[_hbm_like(p) for p in psums],
                   *[_hbm_like(l) for l in lands], _TOKEN),
        in_specs=[HBM] * (2 * n), out_specs=(SEM, SEM, *[HBM] * (2 * n), pl.BlockSpec(memory_space=pltpu.VMEM)),
        input_output_aliases={a: 2 + a for a in range(2 * n)},
        compiler_params=pltpu.CompilerParams(has_side_effects=_EFFECT),
    )(*[_in_hbm(p) for p in psums], *[_in_hbm(l) for l in lands])
    return res[0], res[1], list(res[2:2 + n]), list(res[2 + n:2 + 2 * n]), res[-1]


def rs_finish(send_sem, recv_sem, psums, lands, after, name):
    n = len(psums)

    def body(*refs):
        send_sem, recv_sem = refs[0], refs[1]
        src, land = refs[2:2 + n], refs[2 + n:2 + 2 * n]
        peers = _peers(*_coords())
        for k in range(3):
            for a in range(n):
                pltpu.make_async_remote_copy(src_ref=src[a].at[k], dst_ref=land[a].at[k], send_sem=send_sem.at[k],
                                             recv_sem=recv_sem.at[k], device_id=peers[k + 1], device_id_type=MESH).wait()

    res = pl.pallas_call(
        body, name=name, out_shape=tuple(_hbm_like(l) for l in lands),
        in_specs=[SEM, SEM] + [HBM] * (2 * n) + [ANY], out_specs=tuple([HBM] * n),
        input_output_aliases={2 + n + a: a for a in range(n)},
        compiler_params=pltpu.CompilerParams(has_side_effects=_EFFECT),
    )(send_sem, recv_sem, *psums, *lands, after)
    return list(res)


def d2d_start(grads, name):
    n = len(grads)
    lands = [lax.empty((4,) + g.shape[1:], g.dtype) for g in grads]

    def body(*refs):
        src, land = refs[:n], refs[n:2 * n]
        send_sem, recv_sem, token = refs[2 * n], refs[2 * n + 1], refs[-1]
        x, y, c = _coords()
        for a in range(n):
            for k in range(4):
                blk = _dev_index(x ^ (k & 1), y ^ (k >> 1), 1 - c)
                pltpu.make_async_remote_copy(src_ref=src[a].at[blk], dst_ref=land[a].at[k], send_sem=send_sem.at[0],
                                             recv_sem=recv_sem.at[0], device_id=(x, y, 1 - c), device_id_type=MESH).start()
        token[...] = jnp.zeros_like(token)

    res = pl.pallas_call(
        body, name=name,
        out_shape=(pltpu.SemaphoreType.DMA((1,)), pltpu.SemaphoreType.DMA((1,)), *[_hbm_like(g) for g in grads],
                   *[_hbm_like(l) for l in lands], _TOKEN),
        in_specs=[HBM] * (2 * n), out_specs=(SEM, SEM, *[HBM] * (2 * n), pl.BlockSpec(memory_space=pltpu.VMEM)),
        input_output_aliases={a: 2 + a for a in range(2 * n)},
        compiler_params=pltpu.CompilerParams(has_side_effects=_EFFECT),
    )(*[_in_hbm(g) for g in grads], *[_in_hbm(l) for l in lands])
    return res[0], res[1], list(res[2:2 + n]), list(res[2 + n:2 + 2 * n]), res[-1]


def d2d_finish(send_sem, recv_sem, grads, lands, after, name):
    n = len(grads)

    def body(*refs):
        send_sem, recv_sem = refs[0], refs[1]
        src, land = refs[2:2 + n], refs[2 + n:2 + 2 * n]
        x, y, c = _coords()
        for a in range(n):
            for k in range(4):
                blk = _dev_index(x ^ (k & 1), y ^ (k >> 1), 1 - c)
                pltpu.make_async_remote_copy(src_ref=src[a].at[blk], dst_ref=land[a].at[k], send_sem=send_sem.at[0],
                                             recv_sem=recv_sem.at[0], device_id=(x, y, 1 - c), device_id_type=MESH).wait()

    res = pl.pallas_call(
        body, name=name, out_shape=tuple(_hbm_like(t) for t in list(grads) + list(lands)),
        in_specs=[SEM, SEM] + [HBM] * (2 * n) + [ANY], out_specs=tuple([HBM] * (2 * n)),
        input_output_aliases={2 + a: a for a in range(2 * n)},
        compiler_params=pltpu.CompilerParams(has_side_effects=_EFFECT),
    )(send_sem, recv_sem, *grads, *lands, after)
    return list(res[:n]), list(res[n:])


def pair_add(blk_idx, g, recv, name):
    _, r, c = g.shape
    tr = _row_tile(r)

    def body(idx_ref, g0, g1, g2, g3, r_ref, own_ref, oth_ref):
        own_ref[...] = g0[...].astype(F32) + r_ref[0].astype(F32)
        for k, gk in enumerate((g1, g2, g3)):
            oth_ref[k] = (gk[...].astype(F32) + r_ref[k + 1].astype(F32)).astype(BF16)

    def blk(k):
        return pl.BlockSpec((None, tr, c), lambda t, idx: (idx[k], t, 0))

    grid_spec = pltpu.PrefetchScalarGridSpec(
        num_scalar_prefetch=1, grid=(r // tr,),
        in_specs=[blk(0), blk(1), blk(2), blk(3), pl.BlockSpec((4, tr, c), lambda t, idx: (0, t, 0))],
        out_specs=[pl.BlockSpec((tr, c), lambda t, idx: (t, 0)), pl.BlockSpec((3, tr, c), lambda t, idx: (0, t, 0))])
    return pl.pallas_call(
        body, name=name, grid_spec=grid_spec,
        out_shape=[jax.ShapeDtypeStruct((r, c), F32), jax.ShapeDtypeStruct((3, r, c), BF16)],
        compiler_params=_cparams(1))(blk_idx, g, g, g, g, recv)


def _row_tile(r):
    return next(t for t in (512, 352, 256, 128) if r % t == 0)


def _adamw(w, g, m, v):
    m = ADAM_B1 * m + (1.0 - ADAM_B1) * g
    v = ADAM_B2 * v + (1.0 - ADAM_B2) * (g * g)
    m_hat = m / (1.0 - ADAM_B1 ** ADAM_STEP)
    v_hat = v / (1.0 - ADAM_B2 ** ADAM_STEP)
    delta = -ADAM_LR * (m_hat / (jnp.sqrt(v_hat) + ADAM_EPS) + ADAM_WD * w)
    return delta, m, v


def adamw_shard(own, recv, w, m, v, li, prev, name):
    r, c = own.shape
    tr = _row_tile(r)

    def body(own_ref, recv_ref, w_ref, m_ref, v_ref, p0, p1, p2, p3, g_ref, d_ref, nm_ref, nv_ref):
        g = own_ref[...] + recv_ref[0].astype(F32) + recv_ref[1].astype(F32) + recv_ref[2].astype(F32)
        delta, nm, nv = _adamw(w_ref[...], g, m_ref[...], v_ref[...])
        g_ref[...] = g
        d_ref[...] = delta
        nm_ref[...] = nm
        nv_ref[...] = nv

    lay = pl.BlockSpec((None, tr, c), lambda t: (li, t, 0))
    stack = jax.ShapeDtypeStruct((DEPTH, r, c), F32)
    return pl.pallas_call(
        body, name=name, grid=(r // tr,),
        in_specs=[pl.BlockSpec((tr, c), lambda t: (t, 0)), pl.BlockSpec((3, tr, c), lambda t: (0, t, 0)),
                  lay, lay, lay, ANY, ANY, ANY, ANY],
        out_specs=[lay] * 4, out_shape=[stack] * 4,
        input_output_aliases={5: 0, 6: 1, 7: 2, 8: 3},
        compiler_params=_cparams(1))(own, recv, w, m, v, *prev)


def sum_partials(gathered, name):
    _, r, c = gathered.shape
    tr = next(t for t in (96, 88, 64, _PACK_TILE) if r % t == 0)

    def body(gs_ref, g_ref):
        g = gs_ref[0]
        for d in range(1, N_DEV):
            g = g + gs_ref[d]
        g_ref[...] = g

    return pl.pallas_call(
        body, name=name, grid=(r // tr,),
        in_specs=[pl.BlockSpec((N_DEV, tr, c), lambda t: (0, t, 0))],
        out_specs=pl.BlockSpec((tr, c), lambda t: (t, 0)), out_shape=jax.ShapeDtypeStruct((r, c), F32),
        compiler_params=_cparams(1))(gathered)


def adamw_plain(g, w, m, v, name):
    def body(g_ref, w_ref, m_ref, v_ref, d_ref, nm_ref, nv_ref):
        delta, nm, nv = _adamw(w_ref[...], g_ref[...], m_ref[...], v_ref[...])
        d_ref[...] = delta
        nm_ref[...] = nm
        nv_ref[...] = nv

    return pl.pallas_call(body, name=name, out_shape=[jax.ShapeDtypeStruct(w.shape, F32)] * 3)(g, w, m, v)


_PACK_LAYER = (("b_in", (N_PROJ,)), ("rpb", (N_HEADS, 2 * KH - 1, 2 * KW - 1)), ("pool_w", (4, PGD, PGD)),
               ("pool_scale", (D_POOL,)), ("ln1_g", (D,)), ("ln1_b", (D,)), ("conv_b", (D_FF,)), ("ln2_g", (D,)),
               ("ln2_b", (D,)), ("conv_w", (3, D_FF)))
_PACK_INPUT = (("ln_in_g", (D,)), ("ln_in_b", (D,)))
_PACK_LANES = 1024
_PACK_TILE = 8
_EARLY = tuple(range(1, DEPTH))


def _pack_items(layers):
    items = [(n, (len(layers),) + s) for n, s in _PACK_LAYER]
    return items + ([(n, s) for n, s in _PACK_INPUT] if 0 in layers else [])


def _pack(parts, layers):
    flats = [(parts[name] if (name, shape) in _PACK_INPUT else jnp.stack([parts[name][li] for li in layers]))
             .reshape(-1).astype(F32) for name, shape in _pack_items(layers)]
    used = sum(f.shape[0] for f in flats)
    tile = _PACK_TILE * _PACK_LANES
    total = -(-used // tile) * tile
    return jnp.concatenate(flats + [jnp.zeros((total - used,), F32)]).reshape(total // _PACK_LANES, _PACK_LANES)


def _unpack(packed, layers):
    flat, out, off = packed.reshape(-1), {}, 0
    for name, shape in _pack_items(layers):
        n = int(np.prod(shape))
        out[name] = flat[off:off + n].reshape(shape)
        off += n
    return out


def _bias_tables(rpb):
    qc = np.arange(GRID_W)[:, None]
    kc = np.arange(GRID_W)[None, :]
    start = np.clip(qc - KW // 2, 0, GRID_W - KW)
    valid = (kc >= start) & (kc < start + KW)
    col = np.clip(kc - qc, -(KW - 1), KW - 1) + KW - 1
    onehot = (col.reshape(-1)[None, :] == np.arange(2 * KW - 1)[:, None]).astype(np.float32)
    depth = rpb.shape[0]
    rows = jnp.pad(rpb, ((0, 0), (0, 0), (0, 1), (0, 0)))
    tab = jnp.einsum("lhij,jm->lhim", rows, jnp.asarray(onehot), precision=lax.Precision.HIGHEST)
    tab = tab.reshape(depth, N_HEADS, KROWS, GRID_W, GRID_W).transpose(0, 1, 3, 2, 4)
    ok = valid[:, None, :] & (np.arange(KROWS) < 2 * KH - 1)[None, :, None]
    tab = jnp.where(jnp.asarray(ok), tab, NEG_INF).reshape(depth, N_HEADS, GRID_W, KB)
    tab = jnp.stack([tab, jnp.roll(tab, GRID_W, axis=-1)], axis=2)
    return tab, tab[:, :, :, ::-1, :]


_SHARDED = ("w_in", "w_attn_out", "w_pool_out", "w_mix_out", "w_up", "w_down", "w_ple_gate", "w_ple_proj")
_NAMES = ("ln_in_g", "ln_in_b", "w_in", "b_in", "rpb", "w_attn_out", "pool_w", "pool_scale", "w_pool_out", "w_mix_out",
          "ln1_g", "ln1_b", "w_up", "conv_w", "conv_b", "w_down", "w_ple_gate", "w_ple_proj", "ln2_g", "ln2_b")


def kernel(x, p, ln_in_g, ln_in_b, w_in, b_in, rpb, w_attn_out, pool_w, pool_scale, w_pool_out, w_mix_out, ln1_g, ln1_b, w_up, conv_w, conv_b, w_down, w_ple_gate, w_ple_proj, ln2_g, ln2_b, loss_target, m_ln_in_g, m_ln_in_b, m_w_in, m_b_in, m_rpb, m_w_attn_out, m_pool_w, m_pool_scale, m_w_pool_out, m_w_mix_out, m_ln1_g, m_ln1_b, m_w_up, m_conv_w, m_conv_b, m_w_down, m_w_ple_gate, m_w_ple_proj, m_ln2_g, m_ln2_b, v_ln_in_g, v_ln_in_b, v_w_in, v_b_in, v_rpb, v_w_attn_out, v_pool_w, v_pool_scale, v_w_pool_out, v_w_mix_out, v_ln1_g, v_ln1_b, v_w_up, v_conv_w, v_conv_b, v_w_down, v_w_ple_gate, v_w_ple_proj, v_ln2_g, v_ln2_b):
    a = dict(locals())
    W = {n: a[n] for n in _NAMES}
    M = {n: a["m_" + n] for n in _NAMES}
    V = {n: a["v_" + n] for n in _NAMES}
    xi, yi, ci = _coords()
    me = _dev_index(xi, yi, ci)
    x2, tgt = x[0], loss_target[0]
    pb = p[:, 0].astype(BF16)

    flip = lambda d: {**d, "w_up": d["w_up"].transpose(0, 2, 1)}
    ex = _Exchange(flip(W), flip(M), flip(V))
    loss_part, dx, parts = _local_step(x2, tgt, pb, W, ex)
    loss = lax.psum(loss_part[0, 0], AXES)

    started = ex.replicated_start("late", _pack(parts, (0,)), dx)
    done = ex.update(range(DEPTH - 1, 0, -1), started)
    ex.replicated_forward("late", done)
    done = ex.update((0,), done)
    stacks = {**ex.stacks, "w_up": [t.transpose(0, 2, 1) for t in ex.stacks["w_up"]]}
    lo, hi = [_unpack(sum_partials(ex.replicated_finish(tag, done), f"sum_replicated_{tag}"), layers)
              for layers, tag in (((0,), "late"), (_EARLY, "early"))]
    grads = {**{n: jnp.concatenate([lo[n], hi[n]]) for n, _ in _PACK_LAYER}, **{n: lo[n] for n, _ in _PACK_INPUT}}
    grads["conv_w"] = lax.dynamic_slice_in_dim(grads["conv_w"], me * FF_SHARD, FF_SHARD, axis=2)
    res = [{n: stacks[n][k] for n in _SHARDED} for k in range(4)]
    for n, g in grads.items():
        two_d = lambda t: t.reshape(-1, t.shape[-1])
        outs = adamw_plain(two_d(g), two_d(W[n]), two_d(M[n]), two_d(V[n]), f"adamw_{n}")
        for d, o in zip(res, [g] + [o.reshape(W[n].shape) for o in outs]):
            d[n] = o
    return (loss, dx[None], *[res[k][n] for k in range(4) for n in _NAMES])


class _Exchange:
    GROUPS = (("w_ple_gate", "w_ple_proj", "w_down", "w_up"), ("w_mix_out", "w_attn_out", "w_pool_out"), ("w_in",))
    FIRST = ("w_in",)

    def __init__(self, W, M, V):
        self.W, self.M, self.V = W, M, V
        xi, yi, ci = _coords()
        me = _dev_index(xi, yi, ci)
        self.me = me.astype(I32).reshape(1)
        self.rel_idx = jnp.stack([_dev_index(xi ^ (k & 1), yi ^ (k >> 1), ci) for k in range(4)]).astype(I32)
        self.lands = [{n: lax.dynamic_update_index_in_dim(lax.empty((N_DEV,) + W[n].shape[1:], BF16),
                                                          W[n][li].astype(BF16), me, 0) for n in _SHARDED}
                      for li in range(DEPTH)]
        cw_land = lax.dynamic_update_index_in_dim(lax.empty((N_DEV,) + W["conv_w"].shape, F32), W["conv_w"], me, 0)
        self.ag, self.fwd, self.rs, self.pending, self.small = {}, {}, {}, {}, {}
        self.stacks = {n: [lax.empty((DEPTH,) + W[n].shape[1:], F32) for _ in range(4)] for n in _SHARDED}
        self.late = tuple(n for n in _SHARDED if n not in self.FIRST)
        self.ag[0] = ag_start([self.lands[0][n] for n in self.FIRST] + [cw_land], W["conv_w"], "ag_start0")

    def tokens(self):
        return [self.ag[0][3]]

    def prefetch(self, li, after):
        send, recv, lands, _ = self.ag[li]
        self.fwd[li] = ag_forward(send, recv, lands, after, f"ag_forward{li}")
        if li == 0:
            self.ag["0b"] = ag_start([self.lands[0][n] for n in self.late], self.fwd[0][2][0], "ag_start0b")

    def weights(self, li, after):
        send, recv, _, _ = self.ag.pop(li)
        fsend, frecv, lands = self.fwd.pop(li)
        lands = ag_finish(send, recv, fsend, frecv, lands, after, f"ag_finish{li}")
        if li == 0:
            self.cw = lands[-1].transpose(1, 2, 0, 3).reshape(DEPTH, 3, 4, FF_BLK).transpose(0, 2, 1, 3)
            return dict(zip(self.FIRST, lands)), self.cw[li], (self.ag["0b"][3],)
        tokens = ()
        if li + 1 < DEPTH:
            self.ag[li + 1] = ag_start([self.lands[li + 1][n] for n in _SHARDED], lands[0], f"ag_start{li + 1}")
            tokens = (self.ag[li + 1][3],)
        return dict(zip(_SHARDED, lands)), self.cw[li], tokens

    def rest(self, li, G, mid, after):
        if li != 0:
            return G, ()
        send, recv, lands, _ = self.ag.pop("0b")
        fsend, frecv, lands = ag_forward(send, recv, lands, mid, "ag_forward0b")
        lands = ag_finish(send, recv, fsend, frecv, lands, after, "ag_finish0b")
        self.ag[1] = ag_start([self.lands[1][n] for n in _SHARDED], lands[0], "ag_start1")
        return {**G, **dict(zip(self.late, lands))}, (self.ag[1][3],)

    def grads(self, li, group, gw):
        self.pending.setdefault(li, {}).update(gw)
        if li != 0 and group != len(self.GROUPS) - 1:
            return None
        gw = self.pending.pop(li)
        tag = f"{li}_{group}" if li == 0 else f"{li}"
        send, recv, glist, lands, token = d2d_start(list(gw.values()), f"d2d_start{tag}")
        self.d2d = (tag, tuple(gw), send, recv, glist, lands)
        return token

    def flush(self, li, group, after):
        if li != 0 and group != len(self.GROUPS) - 1:
            return None
        tag, names, send, recv, glist, lands = self.d2d
        glist, recv1 = d2d_finish(send, recv, glist, lands, after, f"d2d_finish{tag}")
        sums = [pair_add(self.rel_idx, g, r1, f"pair_add_{n}{li}") for n, g, r1 in zip(names, glist, recv1)]
        send, recv, psums, lands, token = rs_start([s_[1] for s_ in sums], f"rs_start{tag}")
        self.rs.setdefault(li, []).append((tag, names, send, recv, psums, lands, [s_[0] for s_ in sums]))
        if li == 0 and group == 1 and "early" in self.small:
            self.replicated_forward("early", token)
        return token

    def update(self, layers, after):
        for li in layers:
            for tag, names, send, recv, psums, lands, owns in self.rs.pop(li):
                recv2 = rs_finish(send, recv, psums, lands, after, f"rs_finish{tag}")
                for n, own, r2 in zip(names, owns, recv2):
                    self.stacks[n] = adamw_shard(own, r2, self.W[n], self.M[n], self.V[n], li, self.stacks[n],
                                                 f"adamw_{n}{li}")
                    after = self.stacks[n][0]
        return after

    def replicated_start(self, tag, pack, after):
        land = lax.dynamic_update_index_in_dim(lax.empty((N_DEV,) + pack.shape, F32), pack, self.me[0], 0)
        self.small[tag] = ag_start([land], after, f"ag_start_small_{tag}")
        return self.small[tag][3]

    def replicated_early(self, small, after):
        return self.replicated_start("early", _pack(small, _EARLY), after)

    def replicated_forward(self, tag, after):
        send, recv, lands, _ = self.small[tag]
        self.small[tag] = (send, recv) + ag_forward(send, recv, lands, after, f"ag_forward_small_{tag}")

    def replicated_finish(self, tag, after):
        send, recv, fsend, frecv, lands = self.small.pop(tag)
        return ag_finish(send, recv, fsend, frecv, lands, after, f"ag_finish_small_{tag}")[0]


def _local_step(x2, tgt, pb, W, ex):
    depth = W["rpb"].shape[0]
    vec = lambda t: t.reshape(1, -1)
    ln1_g, ln1_b, ln2_g, ln2_b = W["ln1_g"], W["ln1_b"], W["ln2_g"], W["ln2_b"]
    b_in, rpb, pool_scale = W["b_in"], W["rpb"], W["pool_scale"]
    cb_full = W["conv_b"].reshape(depth, 4, 1, FF_BLK)
    pool_w_b = W["pool_w"].astype(BF16)
    e_tab, e_rev = _bias_tables(rpb)

    h, hb = ln_fwd(x2, vec(W["ln_in_g"]), vec(W["ln_in_b"]), "ln_in", after=ex.tokens())
    ex.prefetch(0, hb)
    saved = []
    for li in range(depth):
        G, cw, tokens = ex.weights(li, hb)
        bias = vec(b_in[li])
        proj, u = proj_fwd(hb, G["w_in"], bias, li, f"proj{li}", after=tokens)
        att = attn_fwd(proj, e_tab, li, f"attn{li}")
        pm, pw = pool_fwd(u, pool_w_b[li], vec(pool_scale[li]), f"pool{li}")
        G, tokens = ex.rest(li, G, att, pw)
        mg, ya, yp = merge_fwd(att, pw, G["w_attn_out"], G["w_pool_out"], proj, li, f"merge{li}", after=tokens)
        if li + 1 < depth:
            ex.prefetch(li + 1, mg)
        z1, h1, h1b = mix_ln_fwd(mg, G["w_mix_out"], h, vec(ln1_g[li]), vec(ln1_b[li]), li, f"mix_ln{li}")
        up = up_fwd(h1b, G["w_up"], li, f"up{li}")
        t = ffn_act_fwd(up, cw, cb_full[li], f"ffn_act{li}")
        z2, h2, h2b, pg, pp = down_ple_ln_fwd(t, G["w_down"], h1b, G["w_ple_gate"], pb[li], G["w_ple_proj"], h1,
                                              vec(ln2_g[li]), vec(ln2_b[li]), li, f"down_ln{li}")
        saved.append(dict(hb=hb, proj=proj, att=att, pm=pm, pw=pw, mg=mg, ya=ya, yp=yp, z1=z1, h1b=h1b, up=up, t=t,
                          z2=z2, pg=pg, pp=pp, G=G, cw=cw))
        h, hb = h2, h2b

    dh, loss_part = loss_bwd(h, tgt, "loss")
    small = {n: [None] * depth for n in ("b_in", "rpb", "pool_w", "pool_scale", "ln1_g", "ln1_b", "conv_b", "ln2_g",
                                         "ln2_b", "conv_w")}
    token = ()
    tok = lambda t: () if t is None else (t,)
    for li in reversed(range(depth)):
        sv = saved[li]
        G, cw = sv["G"], sv["cw"]
        dz2, dz2b, dpg, dpp, dg2, db2 = ln2_ple_bwd(dh, sv["z2"], vec(ln2_g[li]), sv["pg"], sv["pp"], f"ln2_bwd{li}",
                                                    after=token)
        gw = {}
        gw["w_ple_gate"], gw["w_ple_proj"] = wgrad_pair(sv["h1b"], dpg, [(pb[li], dpp)], f"dw_ple{li}")
        gw["w_down"] = wgrad_down(sv["t"], dz2b, f"dw_down{li}").reshape(N_DEV, FF_SHARD, D)
        dhv, dhg, dcw, dcb = ffn_act_bwd(dz2b, G["w_down"], sv["up"], cw, cb_full[li], li, f"ffn_bwd{li}")
        gw["w_up"] = wgrad_up(sv["h1b"], dhv, dhg, f"dw_up{li}")
        token = tok(ex.grads(li, 0, gw))
        dz1, dz1b, dg1, db1 = dh1_ln1_bwd(dz2, dpg, G["w_ple_gate"], dhv, dhg, G["w_up"], sv["z1"], vec(ln1_g[li]), li,
                                          f"ln1_bwd{li}", after=token)
        token = tok(ex.flush(li, 0, dz1b))
        dya, dyp, dga, dgb = merge_bwd(dz1b, G["w_mix_out"], sv["proj"], sv["ya"], sv["yp"], li, f"merge_bwd{li}",
                                       after=token)
        gw = dict(zip(("w_mix_out", "w_attn_out", "w_pool_out"),
                      wgrad_pair(sv["mg"], dz1b, [(sv["att"], dya), (sv["pw"], dyp)], f"dw_out{li}")))
        token = tok(ex.grads(li, 1, gw))
        da = attn_out_bwd(dya, G["w_attn_out"], li, f"da{li}", after=token)
        du, dpool_w, dpool_sc = pool_bwd(dyp, G["w_pool_out"], sv["pm"], pool_w_b[li], vec(pool_scale[li]), li,
                                         f"pool_bwd{li}")
        token = tok(ex.flush(li, 1, du))
        dq, dk, dv, drpb = attn_bwd(sv["proj"], da, e_rev, li, f"attn_bwd{li}", after=token)
        dproj = [dq, dk, dv, du, dga, dgb]
        dw_in, db_in = wgrad_in(sv["hb"], dproj, f"dw_in{li}")
        token = tok(ex.grads(li, 2, {"w_in": dw_in}))
        dh = dh0_bwd(dz1, dproj, G["w_in"], li, f"dh0{li}", after=token)
        small["b_in"][li] = db_in.reshape(N_PROJ)
        small["rpb"][li] = drpb.reshape(N_HEADS, KROWS, GRID_W)[:, :2 * KH - 1, :2 * KW - 1]
        small["pool_w"][li] = dpool_w
        small["pool_scale"][li] = dpool_sc.reshape(D_POOL)
        small["ln1_g"][li], small["ln1_b"][li] = dg1.reshape(D), db1.reshape(D)
        small["ln2_g"][li], small["ln2_b"][li] = dg2.reshape(D), db2.reshape(D)
        small["conv_b"][li] = dcb.reshape(D_FF)
        small["conv_w"][li] = dcw.transpose(1, 0, 2).reshape(3, D_FF)
        token = tok(ex.flush(li, 2, dh))
        if li == 1:
            token = token + tok(ex.replicated_early(small, dh))
    dx, dg_in, db_in0 = ln_bwd(dh, x2, vec(W["ln_in_g"]), "ln_in_bwd", after=token)
    parts = {n: jnp.stack(v_) for n, v_ in small.items()}
    parts["ln_in_g"], parts["ln_in_b"] = dg_in.reshape(D), db_in0.reshape(D)
    return loss_part, dx, parts
```

```python
import numpy as np
import jax
import jax.numpy as jnp
from jax import lax
from jax.experimental import pallas as pl
from jax.experimental.pallas import tpu as pltpu

F32 = jnp.float32
BF16 = jnp.bfloat16
I32 = jnp.int32

D = 1024
DEPTH = 4
GRID_W = 64
N_HEADS = 8
HEAD_DIM = 64
D_ATTN = 512
KH = 8
KW = 16
POOL_WINDOWS = (2, 4, 8, 16)
D_POOL = 512
PGD = 128
D_FF = 2816
PLE_DIM = 256
N_PROJ = 4096
ALPHA = (2 * DEPTH) ** 0.25
LN_EPS = 1e-5
NEG_INF = -1e30
ATT_SCALE = HEAD_DIM ** -0.5
ADAM_LR = 0.001
ADAM_B1 = 0.9
ADAM_B2 = 0.999
ADAM_EPS = 1e-08
ADAM_WD = 0.01
ADAM_STEP = 10

N_DEV = 8
AXES = ("x", "y", "c")
FF_BLK = D_FF // 4
FF_SHARD = D_FF // N_DEV
QROWS = 8
KROWS = 16
QB = QROWS * GRID_W
KB = KROWS * GRID_W
V7X_VMEM_LIMIT = 56 * 2 ** 20
MESH = pl.DeviceIdType.MESH
ANY = pl.BlockSpec(memory_space=pl.ANY)


def _cparams(n_grid):
    return pltpu.CompilerParams(dimension_semantics=("arbitrary",) * n_grid, vmem_limit_bytes=V7X_VMEM_LIMIT)


def _nn(a, b):
    return lax.dot_general(a, b, (((1,), (0,)), ((), ())), preferred_element_type=F32)


def _nt(a, b):
    return lax.dot_general(a, b, (((1,), (1,)), ((), ())), preferred_element_type=F32)


def _tn(a, b):
    return lax.dot_general(a, b, (((0,), (0,)), ((), ())), preferred_element_type=F32)


def _sigmoid(x):
    return 1.0 / (1.0 + jnp.exp(-x))


def _ln_fwd(z, g, b):
    mu = jnp.mean(z, axis=-1, keepdims=True)
    xc = z - mu
    var = jnp.mean(xc * xc, axis=-1, keepdims=True)
    return xc * lax.rsqrt(var + LN_EPS) * g + b


def _ln_bwd(dh, z, g):
    mu = jnp.mean(z, axis=-1, keepdims=True)
    xc = z - mu
    var = jnp.mean(xc * xc, axis=-1, keepdims=True)
    rstd = lax.rsqrt(var + LN_EPS)
    xhat = xc * rstd
    dxh = dh * g
    m1 = jnp.mean(dxh, axis=-1, keepdims=True)
    m2 = jnp.mean(dxh * xhat, axis=-1, keepdims=True)
    return rstd * (dxh - m1 - xhat * m2), dh * xhat


def _colsum(x):
    return jnp.sum(x, axis=0, keepdims=True)


def _lane_cat(ref):
    return jnp.concatenate([ref[j] for j in range(ref.shape[0])], axis=1)


def _row_cat(ref):
    n, r, c = ref.shape
    return ref[...].reshape(n * r, c)


def _shards(n, r, c, li, j_of=None):
    del li
    if j_of is None:
        return pl.BlockSpec((n, r, c), lambda *_: (0, 0, 0))
    return pl.BlockSpec((n, r, c), lambda *g: (j_of(*g), 0, 0))


def _shard(r, c, li, j_of):
    del li
    return pl.BlockSpec((None, r, c), lambda *g: (j_of(*g), 0, 0))


def ln_fwd(x, g, b, name, after=()):
    s = x.shape[0]
    tm = 512
    na = len(after)

    def body(x_ref, g_ref, b_ref, *rest):
        h_ref, hb_ref = rest[na:]
        h = _ln_fwd(x_ref[...], g_ref[...], b_ref[...])
        h_ref[...] = h
        hb_ref[...] = h.astype(BF16)

    row = pl.BlockSpec((tm, D), lambda i: (i, 0))
    vec = pl.BlockSpec((1, D), lambda i: (0, 0))
    return pl.pallas_call(
        body, name=name, grid=(s // tm,), in_specs=[row, vec, vec] + [ANY] * na, out_specs=[row, row],
        out_shape=[jax.ShapeDtypeStruct((s, D), F32), jax.ShapeDtypeStruct((s, D), BF16)],
        compiler_params=_cparams(1))(x, g, b, *after)


def proj_fwd(hb, win, bias, li, name, after=()):
    s = hb.shape[0]
    bn = N_PROJ // N_DEV
    tm = 1024
    pool_shard = (3 * D_ATTN) // bn

    def body(a_ref, w_ref, b_ref, *rest):
        o_ref, u_ref = rest[-2:]
        acc = _nn(a_ref[...], w_ref[...]) + b_ref[...]
        o_ref[...] = acc.astype(BF16)

        @pl.when(pl.program_id(1) == pool_shard)
        def _():
            u_ref[...] = acc

    return pl.pallas_call(
        body, name=name, grid=(s // tm, N_DEV),
        in_specs=[pl.BlockSpec((tm, D), lambda i, j: (i, 0)),
                  _shard(D, bn, li, lambda i, j: j),
                  pl.BlockSpec((1, bn), lambda i, j: (0, j))] + [ANY] * len(after),
        out_specs=[pl.BlockSpec((tm, bn), lambda i, j: (i, j)), pl.BlockSpec((tm, bn), lambda i, j: (i, 0))],
        out_shape=[jax.ShapeDtypeStruct((s, N_PROJ), BF16), jax.ShapeDtypeStruct((s, D_POOL), F32)],
        compiler_params=_cparams(2))(hb, win, bias, *after)


def _attn_types(b, nb):
    first, last = 0, (nb * QROWS - KROWS) * GRID_W
    mid = pl.multiple_of((QROWS * b - KH // 2) * GRID_W, 256)
    return ((b == 0, first), ((b > 0) & (b < nb - 1), mid), (b == nb - 1, last))


def _attn_row(btype, qr):
    lo, delta = ((max(qr - KH // 2, 0), 0), (qr, -(KH // 2)), (min(qr + KH // 2, KH), -KH))[btype]
    return lo, (qr - delta - (KH - 1)) % KROWS, lo - qr + delta + KH - 1


def _row_window(lo):
    pad = (lo % 2) * GRID_W
    return (lo // 2) * 128, KH * GRID_W + 2 * pad, pad


def _lanes(ref, start, width):
    start %= KB
    if start + width <= KB:
        return ref[:, start:start + width]
    return jnp.concatenate([ref[:, start:], ref[:, :start + width - KB]], axis=1)


HALF = QROWS // 2


def _rows_window(btype, qr0, n):
    spans = [_row_window(_attn_row(btype, qr)[0]) for qr in range(qr0, qr0 + n)]
    h0 = min(a0 for a0, _, _ in spans) // 256 * 256
    h1 = -(-max(a0 + w for a0, w, _ in spans) // 256) * 256
    return h0, h1 - h0


def _token_at(k0, h0):
    return k0 + h0 if isinstance(k0, int) else pl.multiple_of(k0 + h0, 256)


def _row_logits(s_ref, e_ref, hh, rows, btype, qr, h0):
    lo, shift, _ = _attn_row(btype, qr)
    a0, w, pad = _row_window(lo)
    e = e_ref.at[hh, shift % 2]
    sb = s_ref[rows, a0 - h0:a0 - h0 + w] + _lanes(e, a0 - (shift - shift % 2) * GRID_W, w)
    if pad:
        lane = lax.broadcasted_iota(I32, (1, w), 1)
        sb = jnp.where((lane >= pad) & (lane < w - pad), sb, NEG_INF)
    return sb, a0 - h0, w, pad


def _store_row(ref, rows, a0, w, val, width):
    if a0:
        ref[rows, 0:a0] = jnp.zeros((GRID_W, a0), ref.dtype)
    ref[rows, a0:a0 + w] = val.astype(ref.dtype)
    if a0 + w < width:
        ref[rows, a0 + w:width] = jnp.zeros((GRID_W, width - a0 - w), ref.dtype)


def attn_fwd(proj, e_tab, li, name):
    s = proj.shape[0]
    nb = s // QB

    def body(q_ref, k_ref, v_ref, e_ref, o_ref, s_ref, p_ref):
        q = q_ref[...] * ATT_SCALE
        lane = lax.broadcasted_iota(I32, (1, 128), 1)

        def block(btype, k0):
            for half in range(2):
                h0, hw = _rows_window(btype, half * HALF, HALF)
                hrows = slice(half * HALF * GRID_W, (half + 1) * HALF * GRID_W)
                kwin = k_ref[pl.ds(_token_at(k0, h0), hw), :]
                vwin = v_ref[pl.ds(_token_at(k0, h0), hw), :]
                acc = jnp.zeros((HALF * GRID_W, 128), F32)
                for hh in range(2):
                    lm = (lane // HEAD_DIM) == hh
                    qh = jnp.where(lm, q[hrows], jnp.zeros_like(q[hrows]))
                    vh = jnp.where(lm, vwin, jnp.zeros_like(vwin))
                    s_ref[:, 0:hw] = _nt(qh, kwin)
                    for r in range(HALF):
                        rows = slice(r * GRID_W, (r + 1) * GRID_W)
                        sb, a0, w, _ = _row_logits(s_ref, e_ref, hh, rows, btype, half * HALF + r, h0)
                        p = jnp.exp(sb - jnp.max(sb, axis=1, keepdims=True))
                        _store_row(p_ref, rows, a0, w, p * (1.0 / jnp.sum(p, axis=1, keepdims=True)), hw)
                    acc = acc + _nn(p_ref[:, 0:hw], vh)
                o_ref[hrows, :] = acc.astype(BF16)

        for btype, (cond, k0) in enumerate(_attn_types(pl.program_id(1), nb)):
            pl.when(cond)(lambda btype=btype, k0=k0: block(btype, k0))

    half_shape = (HALF * GRID_W, (HALF + KH - 1 + 1) * GRID_W)
    return pl.pallas_call(
        body, name=name, grid=(4, nb),
        in_specs=[pl.BlockSpec((QB, 128), lambda j, b: (b, j)),
                  pl.BlockSpec((s, 128), lambda j, b: (0, 4 + j)),
                  pl.BlockSpec((s, 128), lambda j, b: (0, 8 + j)),
                  pl.BlockSpec((None, 2, 2, GRID_W, KB), lambda j, b: (li, j, 0, 0, 0))],
        out_specs=pl.BlockSpec((QB, 128), lambda j, b: (b, j)),
        out_shape=jax.ShapeDtypeStruct((s, D_ATTN), BF16),
        scratch_shapes=[pltpu.VMEM(half_shape, F32), pltpu.VMEM(half_shape, BF16)],
        compiler_params=_cparams(2))(proj, proj, proj, e_tab)


_POOL_PAD = 8


def _pool_counts(s, w):
    t = lax.broadcasted_iota(I32, (s, 1), 0)
    return (jnp.minimum(t + w // 2, s) - jnp.maximum(t - w // 2, 0)).astype(F32)


def _window_sum(x, w, back_first):
    s = x.shape[0]
    z = jnp.zeros((_POOL_PAD, x.shape[1]), F32)
    xe = jnp.concatenate([z, x, z], axis=0)
    n = s + 2 * _POOL_PAD
    acc = xe + pltpu.roll(xe, 1 if back_first else n - 1, 0)
    k = 1
    while 2 * k < w:
        acc = pltpu.roll(acc, k, 0) + pltpu.roll(acc, n - k, 0)
        k *= 2
    return acc[_POOL_PAD:_POOL_PAD + s, :]


def pool_fwd(u, pool_w, pool_scale, name):
    s = u.shape[0]

    def body(u_ref, w_ref, sc_ref, pm_ref, pw_ref):
        for g, w in enumerate(POOL_WINDOWS):
            cols = slice(g * PGD, (g + 1) * PGD)
            ug = u_ref[:, cols]
            pm = (_window_sum(ug, w, True) / _pool_counts(s, w) - ug).astype(BF16)
            pm_ref[:, cols] = pm
            pw_ref[:, cols] = (_nn(pm, w_ref[g]) * sc_ref[:, cols]).astype(BF16)

    full = lambda shape: pl.BlockSpec(shape, lambda i: (0,) * len(shape))
    return pl.pallas_call(
        body, name=name, grid=(1,),
        in_specs=[full((s, D_POOL)), full((4, PGD, PGD)), full((1, D_POOL))],
        out_specs=[full((s, D_POOL)), full((s, D_POOL))],
        out_shape=[jax.ShapeDtypeStruct((s, D_POOL), BF16)] * 2,
        compiler_params=_cparams(1))(u, pool_w, pool_scale)


def merge_fwd(a, pw, wao, wpo, proj, li, name, after=()):
    s = a.shape[0]
    tm, tn = 512, 512
    nt = D // tn
    per = tn // 128

    def body(a_ref, pw_ref, wa_ref, wp_ref, ga_ref, gb_ref, *rest):
        mg_ref, ya_ref, yp_ref = rest[len(after):]
        ya = _nn(a_ref[...], _lane_cat(wa_ref))
        yp = _nn(pw_ref[...], _lane_cat(wp_ref))
        mg = _sigmoid(ga_ref[...].astype(F32)) * ya + _sigmoid(gb_ref[...].astype(F32)) * yp
        mg_ref[...] = mg.astype(BF16)
        ya_ref[...] = ya.astype(BF16)
        yp_ref[...] = yp.astype(BF16)

    act = pl.BlockSpec((tm, D_ATTN), lambda i, j: (i, 0))
    wsp = _shards(per, D_ATTN, 128, li, lambda i, j: j)
    out = pl.BlockSpec((tm, tn), lambda i, j: (i, j))
    ga0 = (3 * D_ATTN + D_POOL) // tn
    return pl.pallas_call(
        body, name=name, grid=(s // tm, nt),
        in_specs=[act, act, wsp, wsp,
                  pl.BlockSpec((tm, tn), lambda i, j: (i, ga0 + j)),
                  pl.BlockSpec((tm, tn), lambda i, j: (i, ga0 + nt + j))] + [ANY] * len(after),
        out_specs=[out, out, out],
        out_shape=[jax.ShapeDtypeStruct((s, D), BF16)] * 3,
        compiler_params=_cparams(2))(a, pw, wao, wpo, proj, proj, *after)


def mix_ln_fwd(mg, wmix, h0, g, b, li, name):
    s = mg.shape[0]
    tm = 256

    def body(mg_ref, w_ref, h0_ref, g_ref, b_ref, z_ref, h_ref, hb_ref):
        z = ALPHA * h0_ref[...] + _nn(mg_ref[...], _row_cat(w_ref))
        h = _ln_fwd(z, g_ref[...], b_ref[...])
        z_ref[...] = z
        h_ref[...] = h
        hb_ref[...] = h.astype(BF16)

    row = pl.BlockSpec((tm, D), lambda i: (i, 0))
    vec = pl.BlockSpec((1, D), lambda i: (0, 0))
    return pl.pallas_call(
        body, name=name, grid=(s // tm,),
        in_specs=[row, _shards(N_DEV, D // N_DEV, D, li), row, vec, vec],
        out_specs=[row, row, row],
        out_shape=[jax.ShapeDtypeStruct((s, D), F32), jax.ShapeDtypeStruct((s, D), F32),
                   jax.ShapeDtypeStruct((s, D), BF16)],
        compiler_params=_cparams(1))(mg, wmix, h0, g, b)


def up_fwd(hb, wup, li, name):
    s = hb.shape[0]
    tm = 1024

    def body(a_ref, w_ref, o_ref):
        o_ref[...] = _nt(a_ref[...], w_ref[...]).astype(BF16)

    return pl.pallas_call(
        body, name=name, grid=(s // tm, N_DEV),
        in_specs=[pl.BlockSpec((tm, D), lambda i, j: (i, 0)), _shard(FF_BLK, D, li, lambda i, j: j)],
        out_specs=pl.BlockSpec((None, tm, FF_BLK), lambda i, j: (j, i, 0)),
        out_shape=jax.ShapeDtypeStruct((N_DEV, s, FF_BLK), BF16),
        compiler_params=_cparams(2))(hb, wup)


_SQRT_HALF = 0.7071067811865476
_INV_SQRT_2PI = 0.3989422804014327


def _shift_rows(x, prev_row, next_row):
    n = x.shape[0]
    r = lax.broadcasted_iota(I32, (n, 1), 0)
    back = jnp.where(r == 0, prev_row, pltpu.roll(x, 1, 0))
    fwd = jnp.where(r == n - 1, next_row, pltpu.roll(x, n - 1, 0))
    return back, fwd


HALO = 16


def _halo_maps(tm, s):
    th = tm // HALO
    return (lambda i: jnp.maximum(i * th - 1, 0)), (lambda i: jnp.minimum((i + 1) * th, s // HALO - 1))


def _slab_specs(tm, s, blk_of):
    before, after = _halo_maps(tm, s)
    main = pl.BlockSpec((None, tm, FF_BLK), lambda c, i: (blk_of(c), i, 0))
    prev = pl.BlockSpec((None, HALO, FF_BLK), lambda c, i: (blk_of(c), before(i), 0))
    nxt = pl.BlockSpec((None, HALO, FF_BLK), lambda c, i: (blk_of(c), after(i), 0))
    return main, prev, nxt


def ffn_act_fwd(up, conv_w, conv_b, name):
    s = up.shape[1]
    tm = 512
    nt = s // tm
    hv_main, _, _ = _slab_specs(tm, s, lambda c: c)
    hg_main, hg_prev, hg_next = _slab_specs(tm, s, lambda c: 4 + c)

    def body(hv_ref, hg_ref, hp_ref, hn_ref, cw_ref, cb_ref, t_ref):
        i = pl.program_id(1)
        hg = hg_ref[...].astype(F32)
        prow = jnp.where(i == 0, 0.0, hp_ref[...].astype(F32)[HALO - 1:HALO, :])
        nrow = jnp.where(i == nt - 1, 0.0, hn_ref[...].astype(F32)[0:1, :])
        back, fwd = _shift_rows(hg, prow, nrow)
        c = back * cw_ref[0:1, :] + hg * cw_ref[1:2, :] + fwd * cw_ref[2:3, :] + cb_ref[...]
        act = 0.5 * c * (1.0 + lax.erf(c * _SQRT_HALF))
        t_ref[...] = (act * hv_ref[...].astype(F32)).astype(BF16)

    return pl.pallas_call(
        body, name=name, grid=(4, nt),
        in_specs=[hv_main, hg_main, hg_prev, hg_next,
                  pl.BlockSpec((None, 3, FF_BLK), lambda c, i: (c, 0, 0)),
                  pl.BlockSpec((None, 1, FF_BLK), lambda c, i: (c, 0, 0))],
        out_specs=pl.BlockSpec((None, tm, FF_BLK), lambda c, i: (c, i, 0)),
        out_shape=jax.ShapeDtypeStruct((4, s, FF_BLK), BF16),
        compiler_params=_cparams(2))(up, up, up, up, conv_w, conv_b)


def down_ple_ln_fwd(t, wdown, hb, wpg, pb, wpp, h1, g, b, li, name):
    s = hb.shape[0]
    tm = 256

    def body(t_ref, wd_ref, hb_ref, wpg_ref, p_ref, wpp_ref, h1_ref, g_ref, b_ref,
             z_ref, h_ref, hbo_ref, pg_ref, pp_ref):
        wd = _row_cat(wd_ref)
        ffn = _nn(t_ref[0], wd[0:FF_BLK, :])
        for c in range(1, 4):
            ffn = ffn + _nn(t_ref[c], wd[c * FF_BLK:(c + 1) * FF_BLK, :])
        pg = _nn(hb_ref[...], _row_cat(wpg_ref))
        pp = _nn(p_ref[...], _lane_cat(wpp_ref))
        z = ALPHA * h1_ref[...] + ffn + _sigmoid(pg) * pp
        h = _ln_fwd(z, g_ref[...], b_ref[...])
        z_ref[...] = z
        h_ref[...] = h
        hbo_ref[...] = h.astype(BF16)
        pg_ref[...] = pg.astype(BF16)
        pp_ref[...] = pp.astype(BF16)

    row = pl.BlockSpec((tm, D), lambda i: (i, 0))
    vec = pl.BlockSpec((1, D), lambda i: (0, 0))
    return pl.pallas_call(
        body, name=name, grid=(s // tm,),
        in_specs=[pl.BlockSpec((4, tm, FF_BLK), lambda i: (0, i, 0)),
                  _shards(N_DEV, FF_SHARD, D, li),
                  row, _shards(N_DEV, D // N_DEV, D, li),
                  pl.BlockSpec((tm, PLE_DIM), lambda i: (i, 0)),
                  _shards(N_DEV, PLE_DIM, 128, li),
                  row, vec, vec],
        out_specs=[row] * 5,
        out_shape=[jax.ShapeDtypeStruct((s, D), F32), jax.ShapeDtypeStruct((s, D), F32),
                   jax.ShapeDtypeStruct((s, D), BF16), jax.ShapeDtypeStruct((s, D), BF16),
                   jax.ShapeDtypeStruct((s, D), BF16)],
        compiler_params=_cparams(1))(t, wdown, hb, wpg, pb, wpp, h1, g, b)


def loss_bwd(h, target, name):
    s = h.shape[0]
    tm = 512

    def body(h_ref, t_ref, dh_ref, l_ref):
        @pl.when(pl.program_id(0) == 0)
        def _():
            l_ref[...] = jnp.zeros_like(l_ref)
        e = h_ref[...] - t_ref[...]
        dh_ref[...] = e * (1.0 / D)
        l_ref[...] += 0.5 * jnp.sum(jnp.mean(e * e, axis=-1, keepdims=True), axis=0, keepdims=True)

    row = pl.BlockSpec((tm, D), lambda i: (i, 0))
    return pl.pallas_call(
        body, name=name, grid=(s // tm,), in_specs=[row, row],
        out_specs=[row, pl.BlockSpec((1, 1), lambda i: (0, 0))],
        out_shape=[jax.ShapeDtypeStruct((s, D), F32), jax.ShapeDtypeStruct((1, 1), F32)],
        compiler_params=_cparams(1))(h, target)


def ln_bwd(dh, z, g, name, after=()):
    s = dh.shape[0]
    tm = 512
    na = len(after)

    def body(dh_ref, z_ref, g_ref, *rest):
        dz_ref, dg_ref, db_ref = rest[na:]

        @pl.when(pl.program_id(0) == 0)
        def _():
            dg_ref[...] = jnp.zeros_like(dg_ref)
            db_ref[...] = jnp.zeros_like(db_ref)
        dh = dh_ref[...]
        dz, dgx = _ln_bwd(dh, z_ref[...], g_ref[...])
        dz_ref[...] = dz
        dg_ref[...] += _colsum(dgx)
        db_ref[...] += _colsum(dh)

    row = pl.BlockSpec((tm, D), lambda i: (i, 0))
    vec = pl.BlockSpec((1, D), lambda i: (0, 0))
    return pl.pallas_call(
        body, name=name, grid=(s // tm,), in_specs=[row, row, vec] + [ANY] * na, out_specs=[row, vec, vec],
        out_shape=[jax.ShapeDtypeStruct((s, D), F32), jax.ShapeDtypeStruct((1, D), F32),
                   jax.ShapeDtypeStruct((1, D), F32)],
        compiler_params=_cparams(1))(dh, z, g, *after)


def ln2_ple_bwd(dh, z, g, pg, pp, name, after=()):
    s = dh.shape[0]
    tm = 512
    na = len(after)

    def body(dh_ref, z_ref, g_ref, pg_ref, pp_ref, *rest):
        dz_ref, dzb_ref, dpg_ref, dpp_ref, dg_ref, db_ref = rest[na:]

        @pl.when(pl.program_id(0) == 0)
        def _():
            dg_ref[...] = jnp.zeros_like(dg_ref)
            db_ref[...] = jnp.zeros_like(db_ref)
        dh = dh_ref[...]
        dz, dgx = _ln_bwd(dh, z_ref[...], g_ref[...])
        sg = _sigmoid(pg_ref[...].astype(F32))
        dz_ref[...] = dz
        dzb_ref[...] = dz.astype(BF16)
        dpg_ref[...] = (dz * pp_ref[...].astype(F32) * sg * (1.0 - sg)).astype(BF16)
        dpp_ref[...] = (dz * sg).astype(BF16)
        dg_ref[...] += _colsum(dgx)
        db_ref[...] += _colsum(dh)

    row = pl.BlockSpec((tm, D), lambda i: (i, 0))
    vec = pl.BlockSpec((1, D), lambda i: (0, 0))
    return pl.pallas_call(
        body, name=name, grid=(s // tm,), in_specs=[row, row, vec, row, row] + [ANY] * na,
        out_specs=[row, row, row, row, vec, vec],
        out_shape=[jax.ShapeDtypeStruct((s, D), F32)] + [jax.ShapeDtypeStruct((s, D), BF16)] * 3
        + [jax.ShapeDtypeStruct((1, D), F32)] * 2,
        compiler_params=_cparams(1))(dh, z, g, pg, pp, *after)


def wgrad_pair(rows_a, rows_dy, cols, name):
    specs, args, outs, shapes, kinds = [], [], [], [], []
    if rows_a is not None:
        s, k = rows_a.shape
        n = rows_dy.shape[1]
        specs += [pl.BlockSpec((s, k // N_DEV), lambda j: (0, j)), pl.BlockSpec((s, n), lambda j: (0, 0))]
        args += [rows_a, rows_dy]
        outs.append(pl.BlockSpec((None, k // N_DEV, n), lambda j: (j, 0, 0)))
        shapes.append(jax.ShapeDtypeStruct((N_DEV, k // N_DEV, n), BF16))
    for a, dy in cols:
        s, k = a.shape
        n = dy.shape[1]
        specs += [pl.BlockSpec((s, k), lambda j: (0, 0)), pl.BlockSpec((s, n // N_DEV), lambda j: (0, j))]
        args += [a, dy]
        outs.append(pl.BlockSpec((None, k, n // N_DEV), lambda j: (j, 0, 0)))
        shapes.append(jax.ShapeDtypeStruct((N_DEV, k, n // N_DEV), BF16))
    n_pairs = len(shapes)

    def body(*refs):
        for i in range(n_pairs):
            refs[2 * n_pairs + i][...] = _tn(refs[2 * i][...], refs[2 * i + 1][...]).astype(BF16)

    return pl.pallas_call(body, name=name, grid=(N_DEV,), in_specs=specs, out_specs=outs, out_shape=shapes,
                          compiler_params=_cparams(1))(*args)


def wgrad_in(a, pieces, name):
    s, k = a.shape
    bn = N_PROJ // N_DEV
    n_narrow = 4

    def body(a_ref, *refs):
        dy_refs, (o_ref, cs_ref) = refs[:6], refs[6:]
        j = pl.program_id(0)

        def emit(dy_ref):
            dy = dy_ref[...]
            o_ref[...] = _tn(a_ref[...], dy).astype(BF16)
            cs_ref[...] = _colsum(dy.astype(F32))

        for idx in range(n_narrow):
            pl.when(j == idx)(lambda idx=idx: emit(dy_refs[idx]))
        pl.when((j >= n_narrow) & (j < n_narrow + 2))(lambda: emit(dy_refs[4]))
        pl.when(j >= n_narrow + 2)(lambda: emit(dy_refs[5]))

    narrow = pl.BlockSpec((s, bn), lambda j: (0, 0))
    return pl.pallas_call(
        body, name=name, grid=(N_DEV,),
        in_specs=[pl.BlockSpec((s, k), lambda j: (0, 0))] + [narrow] * n_narrow
        + [pl.BlockSpec((s, bn), lambda j: (0, jnp.clip(j - n_narrow, 0, 1))),
           pl.BlockSpec((s, bn), lambda j: (0, jnp.clip(j - n_narrow - 2, 0, 1)))],
        out_specs=[pl.BlockSpec((None, k, bn), lambda j: (j, 0, 0)), pl.BlockSpec((1, bn), lambda j: (0, j))],
        out_shape=[jax.ShapeDtypeStruct((N_DEV, k, bn), BF16), jax.ShapeDtypeStruct((1, N_PROJ), F32)],
        compiler_params=_cparams(1))(a, *pieces)


def wgrad_down(t, dy, name):
    _, s, k = t.shape
    n = dy.shape[1]

    def body(a_ref, dy_ref, o_ref):
        o_ref[...] = _tn(a_ref[...], dy_ref[...]).astype(BF16)

    return pl.pallas_call(
        body, name=name, grid=(4,),
        in_specs=[pl.BlockSpec((None, s, k), lambda j: (j, 0, 0)), pl.BlockSpec((s, n), lambda j: (0, 0))],
        out_specs=pl.BlockSpec((None, k, n), lambda j: (j, 0, 0)),
        out_shape=jax.ShapeDtypeStruct((4, k, n), BF16),
        compiler_params=_cparams(1))(t, dy)


def wgrad_up(a, dhv, dhg, name):
    s, k = a.shape

    def body(a_ref, dv_ref, dg_ref, o_ref):
        j = pl.program_id(0)

        @pl.when(j < 4)
        def _():
            o_ref[...] = _tn(dv_ref[...], a_ref[...]).astype(BF16)

        @pl.when(j >= 4)
        def _():
            o_ref[...] = _tn(dg_ref[...], a_ref[...]).astype(BF16)

    return pl.pallas_call(
        body, name=name, grid=(N_DEV,),
        in_specs=[pl.BlockSpec((s, k), lambda j: (0, 0)),
                  pl.BlockSpec((None, s, FF_BLK), lambda j: (jnp.minimum(j, 3), 0, 0)),
                  pl.BlockSpec((None, s, FF_BLK), lambda j: (jnp.maximum(j - 4, 0), 0, 0))],
        out_specs=pl.BlockSpec((None, FF_BLK, k), lambda j: (j, 0, 0)),
        out_shape=jax.ShapeDtypeStruct((N_DEV, FF_BLK, k), BF16),
        compiler_params=_cparams(1))(a, dhv, dhg)


def ffn_act_bwd(dzb, wdown, up, conv_w, conv_b, li, name):
    s = up.shape[1]
    tm = 512
    nt = s // tm
    before, after = _halo_maps(tm, s)
    hv_main, hv_prev, hv_next = _slab_specs(tm, s, lambda c: c)
    hg_main, hg_prev, hg_next = _slab_specs(tm, s, lambda c: 4 + c)

    def dc_of(dz, wd, hv, hg, back, fwd, cw_ref, cb_ref):
        dt = _nt(dz, wd)
        c = back * cw_ref[0:1, :] + hg * cw_ref[1:2, :] + fwd * cw_ref[2:3, :] + cb_ref[...]
        cdf = 0.5 * (1.0 + lax.erf(c * _SQRT_HALF))
        pdf = jnp.exp(-0.5 * c * c) * _INV_SQRT_2PI
        return dt, c * cdf, dt * hv * (cdf + c * pdf)

    def body(dz_ref, dzp_ref, dzn_ref, wd_ref, hv_ref, hvp_ref, hvn_ref, hg_ref, hgp_ref, hgn_ref, cw_ref, cb_ref,
             dhv_ref, dhg_ref, dcw_ref, dcb_ref):
        i = pl.program_id(1)

        @pl.when(i == 0)
        def _():
            dcw_ref[...] = jnp.zeros_like(dcw_ref)
            dcb_ref[...] = jnp.zeros_like(dcb_ref)

        wd = _row_cat(wd_ref)
        hg = hg_ref[...].astype(F32)
        hgp = hgp_ref[...].astype(F32)
        hgn = hgn_ref[...].astype(F32)
        first, last = i == 0, i == nt - 1
        e = HALO - 1
        back, fwd = _shift_rows(hg, jnp.where(first, 0.0, hgp[e:e + 1, :]), jnp.where(last, 0.0, hgn[0:1, :]))
        dt, act, dc = dc_of(dz_ref[...], wd, hv_ref[...].astype(F32), hg, back, fwd, cw_ref, cb_ref)
        dhv_ref[...] = (dt * act).astype(BF16)
        bp, fp = _shift_rows(hgp, hgp[0:1, :], hg[0:1, :])
        _, _, dcp = dc_of(dzp_ref[...], wd, hvp_ref[...].astype(F32), hgp, bp, fp, cw_ref, cb_ref)
        bn, fn = _shift_rows(hgn, hg[tm - 1:tm, :], hgn[e:e + 1, :])
        _, _, dcn = dc_of(dzn_ref[...], wd, hvn_ref[...].astype(F32), hgn, bn, fn, cw_ref, cb_ref)
        dc_back, dc_fwd = _shift_rows(dc, jnp.where(first, 0.0, dcp[e:e + 1, :]), jnp.where(last, 0.0, dcn[0:1, :]))
        dhg_ref[...] = (dc_fwd * cw_ref[0:1, :] + dc * cw_ref[1:2, :] + dc_back * cw_ref[2:3, :]).astype(BF16)
        dcw_ref[0:1, :] += _colsum(dc * back)
        dcw_ref[1:2, :] += _colsum(dc * hg)
        dcw_ref[2:3, :] += _colsum(dc * fwd)
        dcb_ref[...] += _colsum(dc)

    out_slab = pl.BlockSpec((None, tm, FF_BLK), lambda c, i: (c, i, 0))
    cw_spec = pl.BlockSpec((None, 3, FF_BLK), lambda c, i: (c, 0, 0))
    cb_spec = pl.BlockSpec((None, 1, FF_BLK), lambda c, i: (c, 0, 0))
    return pl.pallas_call(
        body, name=name, grid=(4, nt),
        in_specs=[pl.BlockSpec((tm, D), lambda c, i: (i, 0)),
                  pl.BlockSpec((HALO, D), lambda c, i: (before(i), 0)),
                  pl.BlockSpec((HALO, D), lambda c, i: (after(i), 0)),
                  _shards(2, FF_SHARD, D, li, lambda c, i: c),
                  hv_main, hv_prev, hv_next, hg_main, hg_prev, hg_next, cw_spec, cb_spec],
        out_specs=[out_slab, out_slab, cw_spec, cb_spec],
        out_shape=[jax.ShapeDtypeStruct((4, s, FF_BLK), BF16), jax.ShapeDtypeStruct((4, s, FF_BLK), BF16),
                   jax.ShapeDtypeStruct((4, 3, FF_BLK), F32), jax.ShapeDtypeStruct((4, 1, FF_BLK), F32)],
        compiler_params=_cparams(2))(dzb, dzb, dzb, wdown, up, up, up, up, up, up, conv_w, conv_b)


def dh1_ln1_bwd(dz2, dpg, wpg, dhv, dhg, wup, z1, g1, li, name, after=()):
    s = dz2.shape[0]
    tm = 256
    na = len(after)

    def body(dz2_ref, dpg_ref, wpg_ref, dhv_ref, dhg_ref, wup_ref, z1_ref, g_ref, *rest):
        dz_ref, dzb_ref, dg_ref, db_ref = rest[na:]

        @pl.when(pl.program_id(0) == 0)
        def _():
            dg_ref[...] = jnp.zeros_like(dg_ref)
            db_ref[...] = jnp.zeros_like(db_ref)
        dh = ALPHA * dz2_ref[...] + _nt(dpg_ref[...], _row_cat(wpg_ref))
        for c in range(4):
            dh = dh + _nn(dhv_ref[c], wup_ref[c]) + _nn(dhg_ref[c], wup_ref[4 + c])
        dz, dgx = _ln_bwd(dh, z1_ref[...], g_ref[...])
        dz_ref[...] = dz
        dzb_ref[...] = dz.astype(BF16)
        dg_ref[...] += _colsum(dgx)
        db_ref[...] += _colsum(dh)

    row = pl.BlockSpec((tm, D), lambda i: (i, 0))
    vec = pl.BlockSpec((1, D), lambda i: (0, 0))
    slab = pl.BlockSpec((4, tm, FF_BLK), lambda i: (0, i, 0))
    return pl.pallas_call(
        body, name=name, grid=(s // tm,),
        in_specs=[row, row, _shards(N_DEV, D // N_DEV, D, li), slab, slab, _shards(N_DEV, FF_BLK, D, li), row, vec]
        + [ANY] * na,
        out_specs=[row, row, vec, vec],
        out_shape=[jax.ShapeDtypeStruct((s, D), F32), jax.ShapeDtypeStruct((s, D), BF16),
                   jax.ShapeDtypeStruct((1, D), F32), jax.ShapeDtypeStruct((1, D), F32)],
        compiler_params=_cparams(1))(dz2, dpg, wpg, dhv, dhg, wup, z1, g1, *after)


def merge_bwd(dz1b, wmix, proj, ya, yp, li, name, after=()):
    s = dz1b.shape[0]
    tm, tn = 512, 512
    nt = D // tn
    per = tn // (D // N_DEV)
    ga0 = (3 * D_ATTN + D_POOL) // tn

    def body(dz_ref, w_ref, ga_ref, gb_ref, ya_ref, yp_ref, *rest):
        dya_ref, dyp_ref, dga_ref, dgb_ref = rest[len(after):]
        dm = _nt(dz_ref[...], _row_cat(w_ref))
        sa = _sigmoid(ga_ref[...].astype(F32))
        sb = _sigmoid(gb_ref[...].astype(F32))
        dya_ref[...] = (dm * sa).astype(BF16)
        dyp_ref[...] = (dm * sb).astype(BF16)
        dga_ref[...] = (dm * ya_ref[...].astype(F32) * sa * (1.0 - sa)).astype(BF16)
        dgb_ref[...] = (dm * yp_ref[...].astype(F32) * sb * (1.0 - sb)).astype(BF16)

    tile = pl.BlockSpec((tm, tn), lambda i, j: (i, j))
    return pl.pallas_call(
        body, name=name, grid=(s // tm, nt),
        in_specs=[pl.BlockSpec((tm, D), lambda i, j: (i, 0)),
                  _shards(per, D // N_DEV, D, li, lambda i, j: j),
                  pl.BlockSpec((tm, tn), lambda i, j: (i, ga0 + j)),
                  pl.BlockSpec((tm, tn), lambda i, j: (i, ga0 + nt + j)),
                  tile, tile] + [ANY] * len(after),
        out_specs=[tile] * 4,
        out_shape=[jax.ShapeDtypeStruct((s, D), BF16)] * 4,
        compiler_params=_cparams(2))(dz1b, wmix, proj, proj, ya, yp, *after)


def attn_out_bwd(dya, wao, li, name, after=()):
    s = dya.shape[0]
    tm = 512

    def body(d_ref, w_ref, *rest):
        rest[-1][...] = _nt(d_ref[...], _lane_cat(w_ref)).astype(BF16)

    return pl.pallas_call(
        body, name=name, grid=(s // tm,),
        in_specs=[pl.BlockSpec((tm, D), lambda i: (i, 0)), _shards(N_DEV, D_ATTN, 128, li)] + [ANY] * len(after),
        out_specs=pl.BlockSpec((tm, D_ATTN), lambda i: (i, 0)),
        out_shape=jax.ShapeDtypeStruct((s, D_ATTN), BF16),
        compiler_params=_cparams(1))(dya, wao, *after)


def pool_bwd(dyp, wpo, pm, pool_w, pool_scale, li, name):
    s = dyp.shape[0]

    def body(dyp_ref, wpo_ref, pm_ref, w_ref, sc_ref, du_ref, dw_ref, dsc_ref):
        wpo = _lane_cat(wpo_ref)
        dyp = dyp_ref[...]
        for g, w in enumerate(POOL_WINDOWS):
            cols = slice(g * PGD, (g + 1) * PGD)
            dpw = _nt(dyp, wpo[g * PGD:(g + 1) * PGD, :])
            pmg = pm_ref[:, cols]
            dsc_ref[:, cols] = _colsum(dpw * _nn(pmg, w_ref[g]))
            dpmw = (dpw * sc_ref[:, cols]).astype(BF16)
            dw_ref[g] = _tn(pmg, dpmw)
            dpm = _nt(dpmw, w_ref[g])
            du_ref[:, cols] = (_window_sum(dpm / _pool_counts(s, w), w, False) - dpm).astype(BF16)

    full = lambda shape: pl.BlockSpec(shape, lambda i: (0,) * len(shape))
    return pl.pallas_call(
        body, name=name, grid=(1,),
        in_specs=[full((s, D)), _shards(N_DEV, D_POOL, 128, li), full((s, D_POOL)), full((4, PGD, PGD)),
                  full((1, D_POOL))],
        out_specs=[full((s, D_POOL)), full((4, PGD, PGD)), full((1, D_POOL))],
        out_shape=[jax.ShapeDtypeStruct((s, D_POOL), BF16), jax.ShapeDtypeStruct((4, PGD, PGD), F32),
                   jax.ShapeDtypeStruct((1, D_POOL), F32)],
        compiler_params=_cparams(1))(dyp, wpo, pm, pool_w, pool_scale)


def attn_bwd(proj, da, e_rev, li, name, after=()):
    s = proj.shape[0]
    nb = s // QB
    skew = GRID_W + (GRID_W - KW)
    group = QROWS
    half_shape = (group * GRID_W, min(group + KH, KROWS) * GRID_W)

    def body(q_ref, k_ref, v_ref, do_ref, e_ref, *rest):
        dq_ref, dk_ref, dv_ref, g_ref, s_ref, dp_ref, ds_ref, p_ref, dkt_acc, dvt_acc = rest[len(after):]
        b = pl.program_id(1)

        @pl.when(b == 0)
        def _():
            dkt_acc[...] = jnp.zeros_like(dkt_acc)
            dvt_acc[...] = jnp.zeros_like(dvt_acc)
            g_ref[...] = jnp.zeros_like(g_ref)

        ri = lax.broadcasted_iota(I32, (QB, QB), 0)
        ci = lax.broadcasted_iota(I32, (QB, QB), 1)
        rev = jnp.where(ri + ci == QB - 1, 1.0, 0.0).astype(BF16)
        q = _nn(rev, q_ref[...]).astype(BF16) * ATT_SCALE
        do = _nn(rev, do_ref[...]).astype(BF16)
        lane = lax.broadcasted_iota(I32, (1, 128), 1)

        def block(btype, k0):
            dqs = []
            for g0 in range(0, QROWS, group):
                h0, hw = _rows_window(btype, QROWS - g0 - group, group)
                hrows = slice(g0 * GRID_W, (g0 + group) * GRID_W)
                kwin = k_ref[pl.ds(_token_at(k0, h0), hw), :]
                vwin = v_ref[pl.ds(_token_at(k0, h0), hw), :]
                dq = jnp.zeros((group * GRID_W, 128), F32)
                for hh in range(2):
                    lm = (lane // HEAD_DIM) == hh
                    qh = jnp.where(lm, q[hrows], jnp.zeros_like(q[hrows]))
                    doh = jnp.where(lm, do[hrows], jnp.zeros_like(do[hrows]))
                    kh = jnp.where(lm, kwin, jnp.zeros_like(kwin))
                    s_ref[:, 0:hw] = _nt(qh, kwin)
                    dp_ref[:, 0:hw] = _nt(doh, vwin)
                    g = jnp.zeros((1, KB), F32)
                    for r in range(group):
                        qr = QROWS - 1 - (g0 + r)
                        rows = slice(r * GRID_W, (r + 1) * GRID_W)
                        sb, a0, w, pad = _row_logits(s_ref, e_ref, hh, rows, btype, qr, h0)
                        p = jnp.exp(sb - jnp.max(sb, axis=1, keepdims=True))
                        p = p * (1.0 / jnp.sum(p, axis=1, keepdims=True))
                        dp = dp_ref[rows, a0:a0 + w]
                        ds = p * (dp - jnp.sum(p * dp, axis=1, keepdims=True))
                        _store_row(ds_ref, rows, a0, w, ds, hw)
                        _store_row(p_ref, rows, a0, w, p, hw)
                        t = jnp.sum(pltpu.roll(ds, w - skew, 1, stride=1, stride_axis=0), axis=0, keepdims=True)
                        t = t[:, :KH * GRID_W] if pad else pltpu.roll(t, GRID_W, 1)
                        i0 = _attn_row(btype, qr)[2]
                        g = g + pltpu.roll(jnp.concatenate([t, jnp.zeros_like(t)], axis=1), i0 * GRID_W, 1)
                    g_ref[hh] += g
                    dsb = ds_ref[:, 0:hw]
                    dq = dq + _nn(dsb, kh) * ATT_SCALE
                    dkt_acc[:, pl.ds(_token_at(k0, h0), hw)] += _tn(qh, dsb)
                    dvt_acc[:, pl.ds(_token_at(k0, h0), hw)] += _tn(doh, p_ref[:, 0:hw])
                dqs.append(dq.astype(BF16))
            dq_ref[...] = _nn(rev, jnp.concatenate(dqs, axis=0)).astype(BF16)

        for btype, (cond, k0) in enumerate(_attn_types(b, nb)):
            pl.when(cond)(lambda btype=btype, k0=k0: block(btype, k0))

        @pl.when(b == nb - 1)
        def _():
            dk_ref[...] = dkt_acc[...].T.astype(BF16)
            dv_ref[...] = dvt_acc[...].T.astype(BF16)

    col = pl.BlockSpec((s, 128), lambda j, b: (0, j))
    return pl.pallas_call(
        body, name=name, grid=(4, nb),
        in_specs=[pl.BlockSpec((QB, 128), lambda j, b: (b, j)),
                  pl.BlockSpec((s, 128), lambda j, b: (0, 4 + j)),
                  pl.BlockSpec((s, 128), lambda j, b: (0, 8 + j)),
                  pl.BlockSpec((QB, 128), lambda j, b: (b, j)),
                  pl.BlockSpec((None, 2, 2, GRID_W, KB), lambda j, b: (li, j, 0, 0, 0))] + [ANY] * len(after),
        out_specs=[pl.BlockSpec((QB, 128), lambda j, b: (b, j)), col, col,
                   pl.BlockSpec((2, 1, KB), lambda j, b: (j, 0, 0))],
        out_shape=[jax.ShapeDtypeStruct((s, D_ATTN), BF16)] * 3 + [jax.ShapeDtypeStruct((N_HEADS, 1, KB), F32)],
        scratch_shapes=[pltpu.VMEM(half_shape, F32), pltpu.VMEM(half_shape, F32), pltpu.VMEM(half_shape, BF16),
                        pltpu.VMEM(half_shape, BF16), pltpu.VMEM((128, s), F32), pltpu.VMEM((128, s), F32)],
        compiler_params=_cparams(2))(proj, proj, proj, da, e_rev, *after)


def dh0_bwd(dz1, pieces, win, li, name, after=()):
    s = dz1.shape[0]
    tm = 256
    bn = N_PROJ // N_DEV

    def body(dz_ref, q_ref, k_ref, v_ref, u_ref, ga_ref, gb_ref, w_ref, *rest):
        acc = ALPHA * dz_ref[...]
        for j, ref in enumerate((q_ref, k_ref, v_ref, u_ref)):
            acc = acc + _nt(ref[...], w_ref[j])
        for j, ref in ((4, ga_ref), (6, gb_ref)):
            acc = acc + _nt(ref[:, 0:bn], w_ref[j]) + _nt(ref[:, bn:2 * bn], w_ref[j + 1])
        rest[-1][...] = acc

    row = pl.BlockSpec((tm, D), lambda i: (i, 0))
    narrow = pl.BlockSpec((tm, bn), lambda i: (i, 0))
    return pl.pallas_call(
        body, name=name, grid=(s // tm,),
        in_specs=[row] + [narrow] * 4 + [row, row, _shards(N_DEV, D, bn, li)] + [ANY] * len(after),
        out_specs=row, out_shape=jax.ShapeDtypeStruct((s, D), F32),
        compiler_params=_cparams(1))(dz1, *pieces, win, *after)


def _coords():
    return lax.axis_index("x"), lax.axis_index("y"), lax.axis_index("c")


def _dev_index(px, py, pc):
    return 4 * px + 2 * py + pc


HBM = pl.BlockSpec(memory_space=pltpu.HBM)
SEM = pl.BlockSpec(memory_space=pltpu.SEMAPHORE)
_EFFECT = pltpu.SideEffectType.DATAFLOW_SIDE_EFFECTING
_TOKEN = jax.ShapeDtypeStruct((8, 128), F32)


def _in_hbm(a):
    return pltpu.with_memory_space_constraint(a, pltpu.HBM)


def _hbm_like(a):
    return pltpu.HBM(a.shape, a.dtype)


def _peers(x, y, c):
    return [(x, y, 1 - c), (1 - x, y, c), (x, 1 - y, c), (1 - x, 1 - y, c)]


def ag_start(lands, after, name):
    n = len(lands)

    def body(*refs):
        land = refs[:n]
        send_sem, recv_sem, token = refs[n + 1], refs[n + 2], refs[-1]
        x, y, c = _coords()
        me = _dev_index(x, y, c)
        for k, peer in enumerate(_peers(x, y, c)):
            for a in range(n):
                pltpu.make_async_remote_copy(src_ref=land[a].at[me], dst_ref=land[a].at[me], send_sem=send_sem.at[k],
                                             recv_sem=recv_sem.at[k], device_id=peer, device_id_type=MESH).start()
        token[...] = jnp.zeros_like(token)

    res = pl.pallas_call(
        body, name=name,
        out_shape=(pltpu.SemaphoreType.DMA((4,)), pltpu.SemaphoreType.DMA((4,)), *[_hbm_like(l) for l in lands], _TOKEN),
        in_specs=[HBM] * n + [ANY], out_specs=(SEM, SEM, *[HBM] * n, pl.BlockSpec(memory_space=pltpu.VMEM)),
        input_output_aliases={a: 2 + a for a in range(n)},
        compiler_params=pltpu.CompilerParams(has_side_effects=_EFFECT),
    )(*[_in_hbm(l) for l in lands], after)
    return res[0], res[1], list(res[2:2 + n]), res[-1]


def ag_forward(send_sem, recv_sem, lands, after, name):
    n = len(lands)

    def body(*refs):
        send_sem, recv_sem = refs[0], refs[1]
        land = refs[2:2 + n]
        fsend, frecv = refs[3 + n], refs[4 + n]
        x, y, c = _coords()
        peers = _peers(x, y, c)
        for k in range(1, 4):
            blk = _dev_index(*peers[k])
            for a in range(n):
                pltpu.make_async_remote_copy(src_ref=land[a].at[blk], dst_ref=land[a].at[blk], send_sem=send_sem.at[k],
                                             recv_sem=recv_sem.at[k], device_id=peers[k], device_id_type=MESH).wait_recv()
        for k in range(1, 4):
            blk = _dev_index(*peers[k])
            for a in range(n):
                pltpu.make_async_remote_copy(src_ref=land[a].at[blk], dst_ref=land[a].at[blk], send_sem=fsend.at[k - 1],
                                             recv_sem=frecv.at[k - 1], device_id=peers[0], device_id_type=MESH).start()

    res = pl.pallas_call(
        body, name=name,
        out_shape=(pltpu.SemaphoreType.DMA((3,)), pltpu.SemaphoreType.DMA((3,)), *[_hbm_like(l) for l in lands]),
        in_specs=[SEM, SEM, *[HBM] * n, ANY], out_specs=(SEM, SEM, *[HBM] * n),
        input_output_aliases={2 + a: 2 + a for a in range(n)},
        compiler_params=pltpu.CompilerParams(has_side_effects=_EFFECT),
    )(send_sem, recv_sem, *lands, after)
    return res[0], res[1], list(res[2:])


def ag_finish(send_sem, recv_sem, fsend, frecv, lands, after, name):
    n = len(lands)

    def body(*refs):
        send_sem, recv_sem, fsend, frecv = refs[:4]
        land = refs[4:4 + n]
        x, y, c = _coords()
        me = _dev_index(x, y, c)
        peers = _peers(x, y, c)
        for k in range(4):
            for a in range(n):
                pltpu.make_async_remote_copy(src_ref=land[a].at[me], dst_ref=land[a].at[me], send_sem=send_sem.at[k],
                                             recv_sem=recv_sem.at[k], device_id=peers[k], device_id_type=MESH).wait_send()
        sib = _dev_index(*peers[0])
        for a in range(n):
            pltpu.make_async_remote_copy(src_ref=land[a].at[sib], dst_ref=land[a].at[sib], send_sem=send_sem.at[0],
                                         recv_sem=recv_sem.at[0], device_id=peers[0], device_id_type=MESH).wait_recv()
        for k in range(1, 4):
            mine = _dev_index(*peers[k])
            theirs = _dev_index(peers[k][0], peers[k][1], 1 - c)
            for a in range(n):
                pltpu.make_async_remote_copy(src_ref=land[a].at[mine], dst_ref=land[a].at[theirs], send_sem=fsend.at[k - 1],
                                             recv_sem=frecv.at[k - 1], device_id=peers[0], device_id_type=MESH).wait()

    res = pl.pallas_call(
        body, name=name, out_shape=tuple(_hbm_like(l) for l in lands),
        in_specs=[SEM] * 4 + [HBM] * n + [ANY], out_specs=tuple([HBM] * n),
        input_output_aliases={4 + a: a for a in range(n)},
        compiler_params=pltpu.CompilerParams(has_side_effects=_EFFECT),
    )(send_sem, recv_sem, fsend, frecv, *lands, after)
    return list(res)


def rs_start(psums, name):
    n = len(psums)
    lands = [lax.empty(p.shape, p.dtype) for p in psums]

    def body(*refs):
        src, land = refs[:n], refs[n:2 * n]
        send_sem, recv_sem, token = refs[2 * n], refs[2 * n + 1], refs[-1]
        peers = _peers(*_coords())
        for k in range(3):
            for a in range(n):
                pltpu.make_async_remote_copy(src_ref=src[a].at[k], dst_ref=land[a].at[k], send_sem=send_sem.at[k],
                                             recv_sem=recv_sem.at[k], device_id=peers[k + 1], device_id_type=MESH).start()
        token[...] = jnp.zeros_like(token)

    res = pl.pallas_call(
        body, name=name,
        out_shape=(pltpu.SemaphoreType.DMA((3,)), pltpu.SemaphoreType.DMA((3,)), *[_hbm_like(p) for p in psums],
                   *[_hbm_like(l) for l in lands], _TOKEN),
        in_specs=[HBM] * (2 * n), out_specs=(SEM, SEM, *[HBM] * (2 * n), pl.BlockSpec(memory_space=pltpu.VMEM)),
        input_output_aliases={a: 2 + a for a in range(2 * n)},
        compiler_params=pltpu.CompilerParams(has_side_effects=_EFFECT),
    )(*[_in_hbm(p) for p in psums], *[_in_hbm(l) for l in lands])
    return res[0], res[1], list(res[2:2 + n]), list(res[2 + n:2 + 2 * n]), res[-1]


def rs_finish(send_sem, recv_sem, psums, lands, after, name):
    n = len(psums)

    def body(*refs):
        send_sem, recv_sem = refs[0], refs[1]
        src, land = refs[2:2 + n], refs[2 + n:2 + 2 * n]
        peers = _peers(*_coords())
        for k in range(3):
            for a in range(n):
                pltpu.make_async_remote_copy(src_ref=src[a].at[k], dst_ref=land[a].at[k], send_sem=send_sem.at[k],
                                             recv_sem=recv_sem.at[k], device_id=peers[k + 1], device_id_type=MESH).wait()

    res = pl.pallas_call(
        body, name=name, out_shape=tuple(_hbm_like(l) for l in lands),
        in_specs=[SEM, SEM] + [HBM] * (2 * n) + [ANY], out_specs=tuple([HBM] * n),
        input_output_aliases={2 + n + a: a for a in range(n)},
        compiler_params=pltpu.CompilerParams(has_side_effects=_EFFECT),
    )(send_sem, recv_sem, *psums, *lands, after)
    return list(res)


def d2d_start(grads, name):
    n = len(grads)
    lands = [lax.empty((4,) + g.shape[1:], g.dtype) for g in grads]

    def body(*refs):
        src, land = refs[:n], refs[n:2 * n]
        send_sem, recv_sem, token = refs[2 * n], refs[2 * n + 1], refs[-1]
        x, y, c = _coords()
        for a in range(n):
            for k in range(4):
                blk = _dev_index(x ^ (k & 1), y ^ (k >> 1), 1 - c)
                pltpu.make_async_remote_copy(src_ref=src[a].at[blk], dst_ref=land[a].at[k], send_sem=send_sem.at[0],
                                             recv_sem=recv_sem.at[0], device_id=(x, y, 1 - c), device_id_type=MESH).start()
        token[...] = jnp.zeros_like(token)

    res = pl.pallas_call(
        body, name=name,
        out_shape=(pltpu.SemaphoreType.DMA((1,)), pltpu.SemaphoreType.DMA((1,)), *[_hbm_like(g) for g in grads],
                   *[_hbm_like(l) for l in lands], _TOKEN),
        in_specs=[HBM] * (2 * n), out_specs=(SEM, SEM, *[HBM] * (2 * n), pl.BlockSpec(memory_space=pltpu.VMEM)),
        input_output_aliases={a: 2 + a for a in range(2 * n)},
        compiler_params=pltpu.CompilerParams(has_side_effects=_EFFECT),
    )(*[_in_hbm(g) for g in grads], *[_in_hbm(l) for l in lands])
    return res[0], res[1], list(res[2:2 + n]), list(res[2 + n:2 + 2 * n]), res[-1]


def d2d_finish(send_sem, recv_sem, grads, lands, after, name):
    n = len(grads)

    def body(*refs):
        send_sem, recv_sem = refs[0], refs[1]
        src, land = refs[2:2 + n], refs[2 + n:2 + 2 * n]
        x, y, c = _coords()
        for a in range(n):
            for k in range(4):
                blk = _dev_index(x ^ (k & 1), y ^ (k >> 1), 1 - c)
                pltpu.make_async_remote_copy(src_ref=src[a].at[blk], dst_ref=land[a].at[k], send_sem=send_sem.at[0],
                                             recv_sem=recv_sem.at[0], device_id=(x, y, 1 - c), device_id_type=MESH).wait()

    res = pl.pallas_call(
        body, name=name, out_shape=tuple(_hbm_like(t) for t in list(grads) + list(lands)),
        in_specs=[SEM, SEM] + [HBM] * (2 * n) + [ANY], out_specs=tuple([HBM] * (2 * n)),
        input_output_aliases={2 + a: a for a in range(2 * n)},
        compiler_params=pltpu.CompilerParams(has_side_effects=_EFFECT),
    )(send_sem, recv_sem, *grads, *lands, after)
    return list(res[:n]), list(res[n:])


def pair_add(blk_idx, g, recv, name):
    _, r, c = g.shape
    tr = _row_tile(r)

    def body(idx_ref, g0, g1, g2, g3, r_ref, own_ref, oth_ref):
        own_ref[...] = g0[...].astype(F32) + r_ref[0].astype(F32)
        for k, gk in enumerate((g1, g2, g3)):
            oth_ref[k] = (gk[...].astype(F32) + r_ref[k + 1].astype(F32)).astype(BF16)

    def blk(k):
        return pl.BlockSpec((None, tr, c), lambda t, idx: (idx[k], t, 0))

    grid_spec = pltpu.PrefetchScalarGridSpec(
        num_scalar_prefetch=1, grid=(r // tr,),
        in_specs=[blk(0), blk(1), blk(2), blk(3), pl.BlockSpec((4, tr, c), lambda t, idx: (0, t, 0))],
        out_specs=[pl.BlockSpec((tr, c), lambda t, idx: (t, 0)), pl.BlockSpec((3, tr, c), lambda t, idx: (0, t, 0))])
    return pl.pallas_call(
        body, name=name, grid_spec=grid_spec,
        out_shape=[jax.ShapeDtypeStruct((r, c), F32), jax.ShapeDtypeStruct((3, r, c), BF16)],
        compiler_params=_cparams(1))(blk_idx, g, g, g, g, recv)


def _row_tile(r):
    return next(t for t in (512, 352, 256, 128) if r % t == 0)


def _adamw(w, g, m, v):
    m = ADAM_B1 * m + (1.0 - ADAM_B1) * g
    v = ADAM_B2 * v + (1.0 - ADAM_B2) * (g * g)
    m_hat = m / (1.0 - ADAM_B1 ** ADAM_STEP)
    v_hat = v / (1.0 - ADAM_B2 ** ADAM_STEP)
    delta = -ADAM_LR * (m_hat / (jnp.sqrt(v_hat) + ADAM_EPS) + ADAM_WD * w)
    return delta, m, v


def adamw_shard(own, recv, w, m, v, li, prev, name):
    r, c = own.shape
    tr = _row_tile(r)

    def body(own_ref, recv_ref, w_ref, m_ref, v_ref, p0, p1, p2, p3, g_ref, d_ref, nm_ref, nv_ref):
        g = own_ref[...] + recv_ref[0].astype(F32) + recv_ref[1].astype(F32) + recv_ref[2].astype(F32)
        delta, nm, nv = _adamw(w_ref[...], g, m_ref[...], v_ref[...])
        g_ref[...] = g
        d_ref[...] = delta
        nm_ref[...] = nm
        nv_ref[...] = nv

    lay = pl.BlockSpec((None, tr, c), lambda t: (li, t, 0))
    stack = jax.ShapeDtypeStruct((DEPTH, r, c), F32)
    return pl.pallas_call(
        body, name=name, grid=(r // tr,),
        in_specs=[pl.BlockSpec((tr, c), lambda t: (t, 0)), pl.BlockSpec((3, tr, c), lambda t: (0, t, 0)),
                  lay, lay, lay, ANY, ANY, ANY, ANY],
        out_specs=[lay] * 4, out_shape=[stack] * 4,
        input_output_aliases={5: 0, 6: 1, 7: 2, 8: 3},
        compiler_params=_cparams(1))(own, recv, w, m, v, *prev)


def sum_partials(gathered, name):
    _, r, c = gathered.shape
    tr = next(t for t in (96, 88, 64, _PACK_TILE) if r % t == 0)

    def body(gs_ref, g_ref):
        g = gs_ref[0]
        for d in range(1, N_DEV):
            g = g + gs_ref[d]
        g_ref[...] = g

    return pl.pallas_call(
        body, name=name, grid=(r // tr,),
        in_specs=[pl.BlockSpec((N_DEV, tr, c), lambda t: (0, t, 0))],
        out_specs=pl.BlockSpec((tr, c), lambda t: (t, 0)), out_shape=jax.ShapeDtypeStruct((r, c), F32),
        compiler_params=_cparams(1))(gathered)


def adamw_plain(g, w, m, v, name):
    def body(g_ref, w_ref, m_ref, v_ref, d_ref, nm_ref, nv_ref):
        delta, nm, nv = _adamw(w_ref[...], g_ref[...], m_ref[...], v_ref[...])
        d_ref[...] = delta
        nm_ref[...] = nm
        nv_ref[...] = nv

    return pl.pallas_call(body, name=name, out_shape=[jax.ShapeDtypeStruct(w.shape, F32)] * 3)(g, w, m, v)


_PACK_LAYER = (("b_in", (N_PROJ,)), ("rpb", (N_HEADS, 2 * KH - 1, 2 * KW - 1)), ("pool_w", (4, PGD, PGD)),
               ("pool_scale", (D_POOL,)), ("ln1_g", (D,)), ("ln1_b", (D,)), ("conv_b", (D_FF,)), ("ln2_g", (D,)),
               ("ln2_b", (D,)), ("conv_w", (3, D_FF)))
_PACK_INPUT = (("ln_in_g", (D,)), ("ln_in_b", (D,)))
_PACK_LANES = 1024
_PACK_TILE = 8
_EARLY = tuple(range(1, DEPTH))


def _pack_items(layers):
    items = [(n, (len(layers),) + s) for n, s in _PACK_LAYER]
    return items + ([(n, s) for n, s in _PACK_INPUT] if 0 in layers else [])


def _pack(parts, layers):
    flats = [(parts[name] if (name, shape) in _PACK_INPUT else jnp.stack([parts[name][li] for li in layers]))
             .reshape(-1).astype(F32) for name, shape in _pack_items(layers)]
    used = sum(f.shape[0] for f in flats)
    tile = _PACK_TILE * _PACK_LANES
    total = -(-used // tile) * tile
    return jnp.concatenate(flats + [jnp.zeros((total - used,), F32)]).reshape(total // _PACK_LANES, _PACK_LANES)


def _unpack(packed, layers):
    flat, out, off = packed.reshape(-1), {}, 0
    for name, shape in _pack_items(layers):
        n = int(np.prod(shape))
        out[name] = flat[off:off + n].reshape(shape)
        off += n
    return out


def _bias_tables(rpb):
    qc = np.arange(GRID_W)[:, None]
    kc = np.arange(GRID_W)[None, :]
    start = np.clip(qc - KW // 2, 0, GRID_W - KW)
    valid = (kc >= start) & (kc < start + KW)
    col = np.clip(kc - qc, -(KW - 1), KW - 1) + KW - 1
    depth = rpb.shape[0]
    zero = jnp.zeros((depth, N_HEADS, 1, 2 * KW - 1), F32)
    rows = jnp.stack([jnp.concatenate([rpb, zero], axis=2), jnp.concatenate([zero, rpb], axis=2)], axis=2)
    i = np.arange(KROWS)
    real_row = np.stack([i < 2 * KH - 1, i >= 1])

    def table(col, valid):
        onehot = (col.reshape(-1)[None, :] == np.arange(2 * KW - 1)[:, None]).astype(np.float32)
        tab = jnp.einsum("lhpij,jm->lhpim", rows, jnp.asarray(onehot), precision=lax.Precision.HIGHEST)
        tab = tab.reshape(depth, N_HEADS, 2, KROWS, GRID_W, GRID_W).transpose(0, 1, 2, 4, 3, 5)
        ok = valid[None, :, None, :] & real_row[:, None, :, None]
        return jnp.where(jnp.asarray(ok), tab, NEG_INF).reshape(depth, N_HEADS, 2, GRID_W, KB)

    return table(col, valid), table(col[::-1], valid[::-1])


_SHARDED = ("w_in", "w_attn_out", "w_pool_out", "w_mix_out", "w_up", "w_down", "w_ple_gate", "w_ple_proj")
_NAMES = ("ln_in_g", "ln_in_b", "w_in", "b_in", "rpb", "w_attn_out", "pool_w", "pool_scale", "w_pool_out", "w_mix_out",
          "ln1_g", "ln1_b", "w_up", "conv_w", "conv_b", "w_down", "w_ple_gate", "w_ple_proj", "ln2_g", "ln2_b")


def kernel(x, p, ln_in_g, ln_in_b, w_in, b_in, rpb, w_attn_out, pool_w, pool_scale, w_pool_out, w_mix_out, ln1_g, ln1_b, w_up, conv_w, conv_b, w_down, w_ple_gate, w_ple_proj, ln2_g, ln2_b, loss_target, m_ln_in_g, m_ln_in_b, m_w_in, m_b_in, m_rpb, m_w_attn_out, m_pool_w, m_pool_scale, m_w_pool_out, m_w_mix_out, m_ln1_g, m_ln1_b, m_w_up, m_conv_w, m_conv_b, m_w_down, m_w_ple_gate, m_w_ple_proj, m_ln2_g, m_ln2_b, v_ln_in_g, v_ln_in_b, v_w_in, v_b_in, v_rpb, v_w_attn_out, v_pool_w, v_pool_scale, v_w_pool_out, v_w_mix_out, v_ln1_g, v_ln1_b, v_w_up, v_conv_w, v_conv_b, v_w_down, v_w_ple_gate, v_w_ple_proj, v_ln2_g, v_ln2_b):
    a = dict(locals())
    W = {n: a[n] for n in _NAMES}
    M = {n: a["m_" + n] for n in _NAMES}
    V = {n: a["v_" + n] for n in _NAMES}
    xi, yi, ci = _coords()
    me = _dev_index(xi, yi, ci)
    x2, tgt = x[0], loss_target[0]
    pb = p[:, 0].astype(BF16)

    flip = lambda d: {**d, "w_up": d["w_up"].transpose(0, 2, 1)}
    ex = _Exchange(flip(W), flip(M), flip(V))
    loss_part, dx, parts = _local_step(x2, tgt, pb, W, ex)
    loss = lax.psum(loss_part[0, 0], AXES)

    started = ex.replicated_start("late", _pack(parts, (0,)), dx)
    done = ex.update(range(DEPTH - 1, 0, -1), started)
    ex.replicated_forward("late", done)
    done = ex.update((0,), done)
    stacks = {**ex.stacks, "w_up": [t.transpose(0, 2, 1) for t in ex.stacks["w_up"]]}
    lo, hi = [_unpack(sum_partials(ex.replicated_finish(tag, done), f"sum_replicated_{tag}"), layers)
              for layers, tag in (((0,), "late"), (_EARLY, "early"))]
    grads = {**{n: jnp.concatenate([lo[n], hi[n]]) for n, _ in _PACK_LAYER}, **{n: lo[n] for n, _ in _PACK_INPUT}}
    grads["conv_w"] = lax.dynamic_slice_in_dim(grads["conv_w"], me * FF_SHARD, FF_SHARD, axis=2)
    res = [{n: stacks[n][k] for n in _SHARDED} for k in range(4)]
    for n, g in grads.items():
        two_d = lambda t: t.reshape(-1, t.shape[-1])
        outs = adamw_plain(two_d(g), two_d(W[n]), two_d(M[n]), two_d(V[n]), f"adamw_{n}")
        for d, o in zip(res, [g] + [o.reshape(W[n].shape) for o in outs]):
            d[n] = o
    return (loss, dx[None], *[res[k][n] for k in range(4) for n in _NAMES])


class _Exchange:
    GROUPS = (("w_ple_gate", "w_ple_proj", "w_down", "w_up"), ("w_mix_out", "w_attn_out", "w_pool_out"), ("w_in",))
    FIRST = ("w_in",)

    def __init__(self, W, M, V):
        self.W, self.M, self.V = W, M, V
        xi, yi, ci = _coords()
        me = _dev_index(xi, yi, ci)
        self.me = me.astype(I32).reshape(1)
        self.rel_idx = jnp.stack([_dev_index(xi ^ (k & 1), yi ^ (k >> 1), ci) for k in range(4)]).astype(I32)
        self.lands = [{n: lax.dynamic_update_index_in_dim(lax.empty((N_DEV,) + W[n].shape[1:], BF16),
                                                          W[n][li].astype(BF16), me, 0) for n in _SHARDED}
                      for li in range(DEPTH)]
        cw_land = lax.dynamic_update_index_in_dim(lax.empty((N_DEV,) + W["conv_w"].shape, F32), W["conv_w"], me, 0)
        self.ag, self.fwd, self.rs, self.pending, self.small = {}, {}, {}, {}, {}
        self.stacks = {n: [lax.empty((DEPTH,) + W[n].shape[1:], F32) for _ in range(4)] for n in _SHARDED}
        self.late = tuple(n for n in _SHARDED if n not in self.FIRST)
        self.ag[0] = ag_start([self.lands[0][n] for n in self.FIRST] + [cw_land], W["conv_w"], "ag_start0")

    def tokens(self):
        return [self.ag[0][3]]

    def prefetch(self, li, after):
        send, recv, lands, _ = self.ag[li]
        self.fwd[li] = ag_forward(send, recv, lands, after, f"ag_forward{li}")
        if li == 0:
            self.ag["0b"] = ag_start([self.lands[0][n] for n in self.late], self.fwd[0][2][0], "ag_start0b")

    def weights(self, li, after):
        send, recv, _, _ = self.ag.pop(li)
        fsend, frecv, lands = self.fwd.pop(li)
        lands = ag_finish(send, recv, fsend, frecv, lands, after, f"ag_finish{li}")
        if li == 0:
            self.cw = lands[-1].transpose(1, 2, 0, 3).reshape(DEPTH, 3, 4, FF_BLK).transpose(0, 2, 1, 3)
            return dict(zip(self.FIRST, lands)), self.cw[li], (self.ag["0b"][3],)
        tokens = ()
        if li + 1 < DEPTH:
            self.ag[li + 1] = ag_start([self.lands[li + 1][n] for n in _SHARDED], lands[0], f"ag_start{li + 1}")
            tokens = (self.ag[li + 1][3],)
        return dict(zip(_SHARDED, lands)), self.cw[li], tokens

    def rest(self, li, G, mid, after):
        if li != 0:
            return G, ()
        send, recv, lands, _ = self.ag.pop("0b")
        fsend, frecv, lands = ag_forward(send, recv, lands, mid, "ag_forward0b")
        lands = ag_finish(send, recv, fsend, frecv, lands, after, "ag_finish0b")
        self.ag[1] = ag_start([self.lands[1][n] for n in _SHARDED], lands[0], "ag_start1")
        return {**G, **dict(zip(self.late, lands))}, (self.ag[1][3],)

    def grads(self, li, group, gw):
        self.pending.setdefault(li, {}).update(gw)
        if li != 0 and group != len(self.GROUPS) - 1:
            return None
        gw = self.pending.pop(li)
        tag = f"{li}_{group}" if li == 0 else f"{li}"
        send, recv, glist, lands, token = d2d_start(list(gw.values()), f"d2d_start{tag}")
        self.d2d = (tag, tuple(gw), send, recv, glist, lands)
        return token

    def flush(self, li, group, after):
        if li != 0 and group != len(self.GROUPS) - 1:
            return None
        tag, names, send, recv, glist, lands = self.d2d
        glist, recv1 = d2d_finish(send, recv, glist, lands, after, f"d2d_finish{tag}")
        sums = [pair_add(self.rel_idx, g, r1, f"pair_add_{n}{li}") for n, g, r1 in zip(names, glist, recv1)]
        send, recv, psums, lands, token = rs_start([s_[1] for s_ in sums], f"rs_start{tag}")
        self.rs.setdefault(li, []).append((tag, names, send, recv, psums, lands, [s_[0] for s_ in sums]))
        if li == 0 and group == 1 and "early" in self.small:
            self.replicated_forward("early", token)
        return token

    def update(self, layers, after):
        for li in layers:
            for tag, names, send, recv, psums, lands, owns in self.rs.pop(li):
                recv2 = rs_finish(send, recv, psums, lands, after, f"rs_finish{tag}")
                for n, own, r2 in zip(names, owns, recv2):
                    self.stacks[n] = adamw_shard(own, r2, self.W[n], self.M[n], self.V[n], li, self.stacks[n],
                                                 f"adamw_{n}{li}")
                    after = self.stacks[n][0]
        return after

    def replicated_start(self, tag, pack, after):
        land = lax.dynamic_update_index_in_dim(lax.empty((N_DEV,) + pack.shape, F32), pack, self.me[0], 0)
        self.small[tag] = ag_start([land], after, f"ag_start_small_{tag}")
        return self.small[tag][3]

    def replicated_early(self, small, after):
        return self.replicated_start("early", _pack(small, _EARLY), after)

    def replicated_forward(self, tag, after):
        send, recv, lands, _ = self.small[tag]
        self.small[tag] = (send, recv) + ag_forward(send, recv, lands, after, f"ag_forward_small_{tag}")

    def replicated_finish(self, tag, after):
        send, recv, fsend, frecv, lands = self.small.pop(tag)
        return ag_finish(send, recv, fsend, frecv, lands, after, f"ag_finish_small_{tag}")[0]


def _local_step(x2, tgt, pb, W, ex):
    depth = W["rpb"].shape[0]
    vec = lambda t: t.reshape(1, -1)
    ln1_g, ln1_b, ln2_g, ln2_b = W["ln1_g"], W["ln1_b"], W["ln2_g"], W["ln2_b"]
    b_in, rpb, pool_scale = W["b_in"], W["rpb"], W["pool_scale"]
    cb_full = W["conv_b"].reshape(depth, 4, 1, FF_BLK)
    pool_w_b = W["pool_w"].astype(BF16)
    e_tab, e_rev = _bias_tables(rpb)

    h, hb = ln_fwd(x2, vec(W["ln_in_g"]), vec(W["ln_in_b"]), "ln_in", after=ex.tokens())
    ex.prefetch(0, hb)
    saved = []
    for li in range(depth):
        G, cw, tokens = ex.weights(li, hb)
        bias = vec(b_in[li])
        proj, u = proj_fwd(hb, G["w_in"], bias, li, f"proj{li}", after=tokens)
        att = attn_fwd(proj, e_tab, li, f"attn{li}")
        pm, pw = pool_fwd(u, pool_w_b[li], vec(pool_scale[li]), f"pool{li}")
        G, tokens = ex.rest(li, G, att, pw)
        mg, ya, yp = merge_fwd(att, pw, G["w_attn_out"], G["w_pool_out"], proj, li, f"merge{li}", after=tokens)
        if li + 1 < depth:
            ex.prefetch(li + 1, mg)
        z1, h1, h1b = mix_ln_fwd(mg, G["w_mix_out"], h, vec(ln1_g[li]), vec(ln1_b[li]), li, f"mix_ln{li}")
        up = up_fwd(h1b, G["w_up"], li, f"up{li}")
        t = ffn_act_fwd(up, cw, cb_full[li], f"ffn_act{li}")
        z2, h2, h2b, pg, pp = down_ple_ln_fwd(t, G["w_down"], h1b, G["w_ple_gate"], pb[li], G["w_ple_proj"], h1,
                                              vec(ln2_g[li]), vec(ln2_b[li]), li, f"down_ln{li}")
        saved.append(dict(hb=hb, proj=proj, att=att, pm=pm, pw=pw, mg=mg, ya=ya, yp=yp, z1=z1, h1b=h1b, up=up, t=t,
                          z2=z2, pg=pg, pp=pp, G=G, cw=cw))
        h, hb = h2, h2b

    dh, loss_part = loss_bwd(h, tgt, "loss")
    small = {n: [None] * depth for n in ("b_in", "rpb", "pool_w", "pool_scale", "ln1_g", "ln1_b", "conv_b", "ln2_g",
                                         "ln2_b", "conv_w")}
    token = ()
    tok = lambda t: () if t is None else (t,)
    for li in reversed(range(depth)):
        sv = saved[li]
        G, cw = sv["G"], sv["cw"]
        dz2, dz2b, dpg, dpp, dg2, db2 = ln2_ple_bwd(dh, sv["z2"], vec(ln2_g[li]), sv["pg"], sv["pp"], f"ln2_bwd{li}",
                                                    after=token)
        gw = {}
        gw["w_ple_gate"], gw["w_ple_proj"] = wgrad_pair(sv["h1b"], dpg, [(pb[li], dpp)], f"dw_ple{li}")
        gw["w_down"] = wgrad_down(sv["t"], dz2b, f"dw_down{li}").reshape(N_DEV, FF_SHARD, D)
        dhv, dhg, dcw, dcb = ffn_act_bwd(dz2b, G["w_down"], sv["up"], cw, cb_full[li], li, f"ffn_bwd{li}")
        gw["w_up"] = wgrad_up(sv["h1b"], dhv, dhg, f"dw_up{li}")
        token = tok(ex.grads(li, 0, gw))
        dz1, dz1b, dg1, db1 = dh1_ln1_bwd(dz2, dpg, G["w_ple_gate"], dhv, dhg, G["w_up"], sv["z1"], vec(ln1_g[li]), li,
                                          f"ln1_bwd{li}", after=token)
        token = tok(ex.flush(li, 0, dz1b))
        dya, dyp, dga, dgb = merge_bwd(dz1b, G["w_mix_out"], sv["proj"], sv["ya"], sv["yp"], li, f"merge_bwd{li}",
                                       after=token)
        gw = dict(zip(("w_mix_out", "w_attn_out", "w_pool_out"),
                      wgrad_pair(sv["mg"], dz1b, [(sv["att"], dya), (sv["pw"], dyp)], f"dw_out{li}")))
        token = tok(ex.grads(li, 1, gw))
        da = attn_out_bwd(dya, G["w_attn_out"], li, f"da{li}", after=token)
        du, dpool_w, dpool_sc = pool_bwd(dyp, G["w_pool_out"], sv["pm"], pool_w_b[li], vec(pool_scale[li]), li,
                                         f"pool_bwd{li}")
        token = tok(ex.flush(li, 1, du))
        dq, dk, dv, drpb = attn_bwd(sv["proj"], da, e_rev, li, f"attn_bwd{li}", after=token)
        dproj = [dq, dk, dv, du, dga, dgb]
        dw_in, db_in = wgrad_in(sv["hb"], dproj, f"dw_in{li}")
        token = tok(ex.grads(li, 2, {"w_in": dw_in}))
        dh = dh0_bwd(dz1, dproj, G["w_in"], li, f"dh0{li}", after=token)
        small["b_in"][li] = db_in.reshape(N_PROJ)
        small["rpb"][li] = drpb.reshape(N_HEADS, KROWS, GRID_W)[:, :2 * KH - 1, :2 * KW - 1]
        small["pool_w"][li] = dpool_w
        small["pool_scale"][li] = dpool_sc.reshape(D_POOL)
        small["ln1_g"][li], small["ln1_b"][li] = dg1.reshape(D), db1.reshape(D)
        small["ln2_g"][li], small["ln2_b"][li] = dg2.reshape(D), db2.reshape(D)
        small["conv_b"][li] = dcb.reshape(D_FF)
        small["conv_w"][li] = dcw.transpose(1, 0, 2).reshape(3, D_FF)
        token = tok(ex.flush(li, 2, dh))
        if li == 1:
            token = token + tok(ex.replicated_early(small, dh))
    dx, dg_in, db_in0 = ln_bwd(dh, x2, vec(W["ln_in_g"]), "ln_in_bwd", after=token)
    parts = {n: jnp.stack(v_) for n, v_ in small.items()}
    parts["ln_in_g"], parts["ln_in_b"] = dg_in.reshape(D), db_in0.reshape(D)
    return loss_part, dx, parts
```

```python
import numpy as np
import jax
import jax.numpy as jnp
from jax import lax
from jax.experimental import pallas as pl
from jax.experimental.pallas import tpu as pltpu

F32 = jnp.float32
BF16 = jnp.bfloat16
I32 = jnp.int32

D = 1024
DEPTH = 4
GRID_W = 64
N_HEADS = 8
HEAD_DIM = 64
D_ATTN = 512
KH = 8
KW = 16
POOL_WINDOWS = (2, 4, 8, 16)
D_POOL = 512
PGD = 128
D_FF = 2816
PLE_DIM = 256
N_PROJ = 4096
ALPHA = (2 * DEPTH) ** 0.25
LN_EPS = 1e-5
NEG_INF = -1e30
ATT_SCALE = HEAD_DIM ** -0.5
ADAM_LR = 0.001
ADAM_B1 = 0.9
ADAM_B2 = 0.999
ADAM_EPS = 1e-08
ADAM_WD = 0.01
ADAM_STEP = 10

N_DEV = 8
AXES = ("x", "y", "c")
FF_BLK = D_FF // 4
FF_SHARD = D_FF // N_DEV
QROWS = 8
KROWS = 16
QB = QROWS * GRID_W
KB = KROWS * GRID_W
V7X_VMEM_LIMIT = 56 * 2 ** 20
MESH = pl.DeviceIdType.MESH
ANY = pl.BlockSpec(memory_space=pl.ANY)


def _cparams(n_grid):
    return pltpu.CompilerParams(dimension_semantics=("arbitrary",) * n_grid, vmem_limit_bytes=V7X_VMEM_LIMIT)


def _nn(a, b):
    return lax.dot_general(a, b, (((1,), (0,)), ((), ())), preferred_element_type=F32)


def _nt(a, b):
    return lax.dot_general(a, b, (((1,), (1,)), ((), ())), preferred_element_type=F32)


def _tn(a, b):
    return lax.dot_general(a, b, (((0,), (0,)), ((), ())), preferred_element_type=F32)


def _sigmoid(x):
    return 1.0 / (1.0 + jnp.exp(-x))


def _ln_fwd(z, g, b):
    mu = jnp.mean(z, axis=-1, keepdims=True)
    xc = z - mu
    var = jnp.mean(xc * xc, axis=-1, keepdims=True)
    return xc * lax.rsqrt(var + LN_EPS) * g + b


def _ln_bwd(dh, z, g):
    mu = jnp.mean(z, axis=-1, keepdims=True)
    xc = z - mu
    var = jnp.mean(xc * xc, axis=-1, keepdims=True)
    rstd = lax.rsqrt(var + LN_EPS)
    xhat = xc * rstd
    dxh = dh * g
    m1 = jnp.mean(dxh, axis=-1, keepdims=True)
    m2 = jnp.mean(dxh * xhat, axis=-1, keepdims=True)
    return rstd * (dxh - m1 - xhat * m2), dh * xhat


def _colsum(x):
    return jnp.sum(x, axis=0, keepdims=True)


def _lane_cat(ref):
    return jnp.concatenate([ref[j] for j in range(ref.shape[0])], axis=1)


def _row_cat(ref):
    n, r, c = ref.shape
    return ref[...].reshape(n * r, c)


def _shards(n, r, c, li, j_of=None):
    del li
    if j_of is None:
        return pl.BlockSpec((n, r, c), lambda *_: (0, 0, 0))
    return pl.BlockSpec((n, r, c), lambda *g: (j_of(*g), 0, 0))


def _shard(r, c, li, j_of):
    del li
    return pl.BlockSpec((None, r, c), lambda *g: (j_of(*g), 0, 0))


def ln_fwd(x, g, b, name, after=()):
    s = x.shape[0]
    tm = 512
    na = len(after)

    def body(x_ref, g_ref, b_ref, *rest):
        h_ref, hb_ref = rest[na:]
        h = _ln_fwd(x_ref[...], g_ref[...], b_ref[...])
        h_ref[...] = h
        hb_ref[...] = h.astype(BF16)

    row = pl.BlockSpec((tm, D), lambda i: (i, 0))
    vec = pl.BlockSpec((1, D), lambda i: (0, 0))
    return pl.pallas_call(
        body, name=name, grid=(s // tm,), in_specs=[row, vec, vec] + [ANY] * na, out_specs=[row, row],
        out_shape=[jax.ShapeDtypeStruct((s, D), F32), jax.ShapeDtypeStruct((s, D), BF16)],
        compiler_params=_cparams(1))(x, g, b, *after)


def proj_fwd(hb, win, bias, li, name, after=()):
    s = hb.shape[0]
    bn = N_PROJ // N_DEV
    tm = s
    pool_shard = (3 * D_ATTN) // bn

    def body(a_ref, w_ref, b_ref, *rest):
        o_ref, u_ref = rest[-2:]
        acc = _nn(a_ref[...], w_ref[...]) + b_ref[...]
        o_ref[...] = acc.astype(BF16)

        @pl.when(pl.program_id(1) == pool_shard)
        def _():
            u_ref[...] = acc

    return pl.pallas_call(
        body, name=name, grid=(s // tm, N_DEV),
        in_specs=[pl.BlockSpec((tm, D), lambda i, j: (i, 0)),
                  _shard(D, bn, li, lambda i, j: j),
                  pl.BlockSpec((1, bn), lambda i, j: (0, j))] + [ANY] * len(after),
        out_specs=[pl.BlockSpec((tm, bn), lambda i, j: (i, j)), pl.BlockSpec((tm, bn), lambda i, j: (i, 0))],
        out_shape=[jax.ShapeDtypeStruct((s, N_PROJ), BF16), jax.ShapeDtypeStruct((s, D_POOL), F32)],
        compiler_params=_cparams(2))(hb, win, bias, *after)


def _attn_types(b, nb):
    first, last = 0, (nb * QROWS - KROWS) * GRID_W
    mid = pl.multiple_of((QROWS * b - KH // 2) * GRID_W, 256)
    return ((b == 0, first), ((b > 0) & (b < nb - 1), mid), (b == nb - 1, last))


def _attn_row(btype, qr):
    lo, delta = ((max(qr - KH // 2, 0), 0), (qr, -(KH // 2)), (min(qr + KH // 2, KH), -KH))[btype]
    return lo, (qr - delta - (KH - 1)) % KROWS, lo - qr + delta + KH - 1


def _row_window(lo):
    pad = (lo % 2) * GRID_W
    return (lo // 2) * 128, KH * GRID_W + 2 * pad, pad


def _lanes(ref, start, width):
    start %= KB
    if start + width <= KB:
        return ref[:, start:start + width]
    return jnp.concatenate([ref[:, start:], ref[:, :start + width - KB]], axis=1)


HALF = QROWS // 2


def _rows_window(btype, qr0, n):
    spans = [_row_window(_attn_row(btype, qr)[0]) for qr in range(qr0, qr0 + n)]
    h0 = min(a0 for a0, _, _ in spans) // 256 * 256
    h1 = -(-max(a0 + w for a0, w, _ in spans) // 256) * 256
    return h0, h1 - h0


def _token_at(k0, h0):
    return k0 + h0 if isinstance(k0, int) else pl.multiple_of(k0 + h0, 256)


def _row_logits(s_ref, e_ref, hh, rows, btype, qr, h0):
    lo, shift, _ = _attn_row(btype, qr)
    a0, w, pad = _row_window(lo)
    e = e_ref.at[hh, shift % 2]
    sb = s_ref[rows, a0 - h0:a0 - h0 + w] + _lanes(e, a0 - (shift - shift % 2) * GRID_W, w)
    if pad:
        lane = lax.broadcasted_iota(I32, (1, w), 1)
        sb = jnp.where((lane >= pad) & (lane < w - pad), sb, NEG_INF)
    return sb, a0 - h0, w, pad


def _store_row(ref, rows, a0, w, val, width):
    if a0:
        ref[rows, 0:a0] = jnp.zeros((GRID_W, a0), ref.dtype)
    ref[rows, a0:a0 + w] = val.astype(ref.dtype)
    if a0 + w < width:
        ref[rows, a0 + w:width] = jnp.zeros((GRID_W, width - a0 - w), ref.dtype)


def attn_fwd(proj, e_tab, li, name):
    s = proj.shape[0]
    nb = s // QB

    def body(q_ref, k_ref, v_ref, e_ref, o_ref, s_ref, p_ref):
        q = q_ref[...] * ATT_SCALE
        lane = lax.broadcasted_iota(I32, (1, 128), 1)

        def block(btype, k0):
            for half in range(2):
                h0, hw = _rows_window(btype, half * HALF, HALF)
                hrows = slice(half * HALF * GRID_W, (half + 1) * HALF * GRID_W)
                kwin = k_ref[pl.ds(_token_at(k0, h0), hw), :]
                vwin = v_ref[pl.ds(_token_at(k0, h0), hw), :]
                acc = jnp.zeros((HALF * GRID_W, 128), F32)
                for hh in range(2):
                    lm = (lane // HEAD_DIM) == hh
                    qh = jnp.where(lm, q[hrows], jnp.zeros_like(q[hrows]))
                    vh = jnp.where(lm, vwin, jnp.zeros_like(vwin))
                    s_ref[:, 0:hw] = _nt(qh, kwin)
                    for r in range(HALF):
                        rows = slice(r * GRID_W, (r + 1) * GRID_W)
                        sb, a0, w, _ = _row_logits(s_ref, e_ref, hh, rows, btype, half * HALF + r, h0)
                        p = jnp.exp(sb - jnp.max(sb, axis=1, keepdims=True))
                        _store_row(p_ref, rows, a0, w, p * (1.0 / jnp.sum(p, axis=1, keepdims=True)), hw)
                    acc = acc + _nn(p_ref[:, 0:hw], vh)
                o_ref[hrows, :] = acc.astype(BF16)

        for btype, (cond, k0) in enumerate(_attn_types(pl.program_id(1), nb)):
            pl.when(cond)(lambda btype=btype, k0=k0: block(btype, k0))

    half_shape = (HALF * GRID_W, (HALF + KH - 1 + 1) * GRID_W)
    return pl.pallas_call(
        body, name=name, grid=(4, nb),
        in_specs=[pl.BlockSpec((QB, 128), lambda j, b: (b, j)),
                  pl.BlockSpec((s, 128), lambda j, b: (0, 4 + j)),
                  pl.BlockSpec((s, 128), lambda j, b: (0, 8 + j)),
                  pl.BlockSpec((None, 2, 2, GRID_W, KB), lambda j, b: (li, j, 0, 0, 0))],
        out_specs=pl.BlockSpec((QB, 128), lambda j, b: (b, j)),
        out_shape=jax.ShapeDtypeStruct((s, D_ATTN), BF16),
        scratch_shapes=[pltpu.VMEM(half_shape, F32), pltpu.VMEM(half_shape, BF16)],
        compiler_params=_cparams(2))(proj, proj, proj, e_tab)


_POOL_PAD = 8


def _pool_counts(s, w):
    t = lax.broadcasted_iota(I32, (s, 1), 0)
    return (jnp.minimum(t + w // 2, s) - jnp.maximum(t - w // 2, 0)).astype(F32)


def _window_sum(x, w, back_first):
    s = x.shape[0]
    z = jnp.zeros((_POOL_PAD, x.shape[1]), F32)
    xe = jnp.concatenate([z, x, z], axis=0)
    n = s + 2 * _POOL_PAD
    acc = xe + pltpu.roll(xe, 1 if back_first else n - 1, 0)
    k = 1
    while 2 * k < w:
        acc = pltpu.roll(acc, k, 0) + pltpu.roll(acc, n - k, 0)
        k *= 2
    return acc[_POOL_PAD:_POOL_PAD + s, :]


def pool_fwd(u, pool_w, pool_scale, name):
    s = u.shape[0]

    def body(u_ref, w_ref, sc_ref, pm_ref, pw_ref):
        for g, w in enumerate(POOL_WINDOWS):
            cols = slice(g * PGD, (g + 1) * PGD)
            ug = u_ref[:, cols]
            pm = (_window_sum(ug, w, True) / _pool_counts(s, w) - ug).astype(BF16)
            pm_ref[:, cols] = pm
            pw_ref[:, cols] = (_nn(pm, w_ref[g]) * sc_ref[:, cols]).astype(BF16)

    full = lambda shape: pl.BlockSpec(shape, lambda i: (0,) * len(shape))
    return pl.pallas_call(
        body, name=name, grid=(1,),
        in_specs=[full((s, D_POOL)), full((4, PGD, PGD)), full((1, D_POOL))],
        out_specs=[full((s, D_POOL)), full((s, D_POOL))],
        out_shape=[jax.ShapeDtypeStruct((s, D_POOL), BF16)] * 2,
        compiler_params=_cparams(1))(u, pool_w, pool_scale)


def merge_fwd(a, pw, wao, wpo, proj, li, name, after=()):
    s = a.shape[0]
    tm, tn = 512, 512
    nt = D // tn
    per = tn // 128

    def body(a_ref, pw_ref, wa_ref, wp_ref, ga_ref, gb_ref, *rest):
        mg_ref, ya_ref, yp_ref = rest[len(after):]
        ya = _nn(a_ref[...], _lane_cat(wa_ref))
        yp = _nn(pw_ref[...], _lane_cat(wp_ref))
        mg = _sigmoid(ga_ref[...].astype(F32)) * ya + _sigmoid(gb_ref[...].astype(F32)) * yp
        mg_ref[...] = mg.astype(BF16)
        ya_ref[...] = ya.astype(BF16)
        yp_ref[...] = yp.astype(BF16)

    act = pl.BlockSpec((tm, D_ATTN), lambda i, j: (i, 0))
    wsp = _shards(per, D_ATTN, 128, li, lambda i, j: j)
    out = pl.BlockSpec((tm, tn), lambda i, j: (i, j))
    ga0 = (3 * D_ATTN + D_POOL) // tn
    return pl.pallas_call(
        body, name=name, grid=(s // tm, nt),
        in_specs=[act, act, wsp, wsp,
                  pl.BlockSpec((tm, tn), lambda i, j: (i, ga0 + j)),
                  pl.BlockSpec((tm, tn), lambda i, j: (i, ga0 + nt + j))] + [ANY] * len(after),
        out_specs=[out, out, out],
        out_shape=[jax.ShapeDtypeStruct((s, D), BF16)] * 3,
        compiler_params=_cparams(2))(a, pw, wao, wpo, proj, proj, *after)


def mix_ln_fwd(mg, wmix, h0, g, b, li, name):
    s = mg.shape[0]
    tm = 512

    def body(mg_ref, w_ref, h0_ref, g_ref, b_ref, z_ref, h_ref, hb_ref):
        z = ALPHA * h0_ref[...] + _nn(mg_ref[...], _row_cat(w_ref))
        h = _ln_fwd(z, g_ref[...], b_ref[...])
        z_ref[...] = z
        h_ref[...] = h
        hb_ref[...] = h.astype(BF16)

    row = pl.BlockSpec((tm, D), lambda i: (i, 0))
    vec = pl.BlockSpec((1, D), lambda i: (0, 0))
    return pl.pallas_call(
        body, name=name, grid=(s // tm,),
        in_specs=[row, _shards(N_DEV, D // N_DEV, D, li), row, vec, vec],
        out_specs=[row, row, row],
        out_shape=[jax.ShapeDtypeStruct((s, D), F32), jax.ShapeDtypeStruct((s, D), F32),
                   jax.ShapeDtypeStruct((s, D), BF16)],
        compiler_params=_cparams(1))(mg, wmix, h0, g, b)


def up_fwd(hb, wup, li, name):
    s = hb.shape[0]
    tm = s

    def body(a_ref, w_ref, o_ref):
        o_ref[...] = _nt(a_ref[...], w_ref[...]).astype(BF16)

    return pl.pallas_call(
        body, name=name, grid=(s // tm, N_DEV),
        in_specs=[pl.BlockSpec((tm, D), lambda i, j: (i, 0)), _shard(FF_BLK, D, li, lambda i, j: j)],
        out_specs=pl.BlockSpec((None, tm, FF_BLK), lambda i, j: (j, i, 0)),
        out_shape=jax.ShapeDtypeStruct((N_DEV, s, FF_BLK), BF16),
        compiler_params=_cparams(2))(hb, wup)


_SQRT_HALF = 0.7071067811865476
_INV_SQRT_2PI = 0.3989422804014327


def _shift_rows(x, prev_row, next_row):
    n = x.shape[0]
    r = lax.broadcasted_iota(I32, (n, 1), 0)
    back = jnp.where(r == 0, prev_row, pltpu.roll(x, 1, 0))
    fwd = jnp.where(r == n - 1, next_row, pltpu.roll(x, n - 1, 0))
    return back, fwd


HALO = 16


def _halo_maps(tm, s):
    th = tm // HALO
    return (lambda i: jnp.maximum(i * th - 1, 0)), (lambda i: jnp.minimum((i + 1) * th, s // HALO - 1))


def _slab_specs(tm, s, blk_of):
    before, after = _halo_maps(tm, s)
    main = pl.BlockSpec((None, tm, FF_BLK), lambda c, i: (blk_of(c), i, 0))
    prev = pl.BlockSpec((None, HALO, FF_BLK), lambda c, i: (blk_of(c), before(i), 0))
    nxt = pl.BlockSpec((None, HALO, FF_BLK), lambda c, i: (blk_of(c), after(i), 0))
    return main, prev, nxt


def ffn_act_fwd(up, conv_w, conv_b, name):
    s = up.shape[1]
    tm = 512
    nt = s // tm
    hv_main, _, _ = _slab_specs(tm, s, lambda c: c)
    hg_main, hg_prev, hg_next = _slab_specs(tm, s, lambda c: 4 + c)

    def body(hv_ref, hg_ref, hp_ref, hn_ref, cw_ref, cb_ref, t_ref):
        i = pl.program_id(1)
        hg = hg_ref[...].astype(F32)
        prow = jnp.where(i == 0, 0.0, hp_ref[...].astype(F32)[HALO - 1:HALO, :])
        nrow = jnp.where(i == nt - 1, 0.0, hn_ref[...].astype(F32)[0:1, :])
        back, fwd = _shift_rows(hg, prow, nrow)
        c = back * cw_ref[0:1, :] + hg * cw_ref[1:2, :] + fwd * cw_ref[2:3, :] + cb_ref[...]
        act = 0.5 * c * (1.0 + lax.erf(c * _SQRT_HALF))
        t_ref[...] = (act * hv_ref[...].astype(F32)).astype(BF16)

    return pl.pallas_call(
        body, name=name, grid=(4, nt),
        in_specs=[hv_main, hg_main, hg_prev, hg_next,
                  pl.BlockSpec((None, 3, FF_BLK), lambda c, i: (c, 0, 0)),
                  pl.BlockSpec((None, 1, FF_BLK), lambda c, i: (c, 0, 0))],
        out_specs=pl.BlockSpec((None, tm, FF_BLK), lambda c, i: (c, i, 0)),
        out_shape=jax.ShapeDtypeStruct((4, s, FF_BLK), BF16),
        compiler_params=_cparams(2))(up, up, up, up, conv_w, conv_b)


def down_ple_ln_fwd(t, wdown, hb, wpg, pb, wpp, h1, g, b, li, name):
    s = hb.shape[0]
    tm = 256

    def body(t_ref, wd_ref, hb_ref, wpg_ref, p_ref, wpp_ref, h1_ref, g_ref, b_ref,
             z_ref, h_ref, hbo_ref, pg_ref, pp_ref):
        wd = _row_cat(wd_ref)
        ffn = _nn(t_ref[0], wd[0:FF_BLK, :])
        for c in range(1, 4):
            ffn = ffn + _nn(t_ref[c], wd[c * FF_BLK:(c + 1) * FF_BLK, :])
        pg = _nn(hb_ref[...], _row_cat(wpg_ref))
        pp = _nn(p_ref[...], _lane_cat(wpp_ref))
        z = ALPHA * h1_ref[...] + ffn + _sigmoid(pg) * pp
        h = _ln_fwd(z, g_ref[...], b_ref[...])
        z_ref[...] = z
        h_ref[...] = h
        hbo_ref[...] = h.astype(BF16)
        pg_ref[...] = pg.astype(BF16)
        pp_ref[...] = pp.astype(BF16)

    row = pl.BlockSpec((tm, D), lambda i: (i, 0))
    vec = pl.BlockSpec((1, D), lambda i: (0, 0))
    return pl.pallas_call(
        body, name=name, grid=(s // tm,),
        in_specs=[pl.BlockSpec((4, tm, FF_BLK), lambda i: (0, i, 0)),
                  _shards(N_DEV, FF_SHARD, D, li),
                  row, _shards(N_DEV, D // N_DEV, D, li),
                  pl.BlockSpec((tm, PLE_DIM), lambda i: (i, 0)),
                  _shards(N_DEV, PLE_DIM, 128, li),
                  row, vec, vec],
        out_specs=[row] * 5,
        out_shape=[jax.ShapeDtypeStruct((s, D), F32), jax.ShapeDtypeStruct((s, D), F32),
                   jax.ShapeDtypeStruct((s, D), BF16), jax.ShapeDtypeStruct((s, D), BF16),
                   jax.ShapeDtypeStruct((s, D), BF16)],
        compiler_params=_cparams(1))(t, wdown, hb, wpg, pb, wpp, h1, g, b)


def loss_bwd(h, target, name):
    s = h.shape[0]
    tm = 512

    def body(h_ref, t_ref, dh_ref, l_ref):
        @pl.when(pl.program_id(0) == 0)
        def _():
            l_ref[...] = jnp.zeros_like(l_ref)
        e = h_ref[...] - t_ref[...]
        dh_ref[...] = e * (1.0 / D)
        l_ref[...] += 0.5 * jnp.sum(jnp.mean(e * e, axis=-1, keepdims=True), axis=0, keepdims=True)

    row = pl.BlockSpec((tm, D), lambda i: (i, 0))
    return pl.pallas_call(
        body, name=name, grid=(s // tm,), in_specs=[row, row],
        out_specs=[row, pl.BlockSpec((1, 1), lambda i: (0, 0))],
        out_shape=[jax.ShapeDtypeStruct((s, D), F32), jax.ShapeDtypeStruct((1, 1), F32)],
        compiler_params=_cparams(1))(h, target)


def ln_bwd(dh, z, g, name, after=()):
    s = dh.shape[0]
    tm = 512
    na = len(after)

    def body(dh_ref, z_ref, g_ref, *rest):
        dz_ref, dg_ref, db_ref = rest[na:]

        @pl.when(pl.program_id(0) == 0)
        def _():
            dg_ref[...] = jnp.zeros_like(dg_ref)
            db_ref[...] = jnp.zeros_like(db_ref)
        dh = dh_ref[...]
        dz, dgx = _ln_bwd(dh, z_ref[...], g_ref[...])
        dz_ref[...] = dz
        dg_ref[...] += _colsum(dgx)
        db_ref[...] += _colsum(dh)

    row = pl.BlockSpec((tm, D), lambda i: (i, 0))
    vec = pl.BlockSpec((1, D), lambda i: (0, 0))
    return pl.pallas_call(
        body, name=name, grid=(s // tm,), in_specs=[row, row, vec] + [ANY] * na, out_specs=[row, vec, vec],
        out_shape=[jax.ShapeDtypeStruct((s, D), F32), jax.ShapeDtypeStruct((1, D), F32),
                   jax.ShapeDtypeStruct((1, D), F32)],
        compiler_params=_cparams(1))(dh, z, g, *after)


def ln2_ple_bwd(dh, z, g, pg, pp, name, after=()):
    s = dh.shape[0]
    tm = 512
    na = len(after)

    def body(dh_ref, z_ref, g_ref, pg_ref, pp_ref, *rest):
        dz_ref, dzb_ref, dpg_ref, dpp_ref, dg_ref, db_ref = rest[na:]

        @pl.when(pl.program_id(0) == 0)
        def _():
            dg_ref[...] = jnp.zeros_like(dg_ref)
            db_ref[...] = jnp.zeros_like(db_ref)
        dh = dh_ref[...]
        dz, dgx = _ln_bwd(dh, z_ref[...], g_ref[...])
        sg = _sigmoid(pg_ref[...].astype(F32))
        dz_ref[...] = dz
        dzb_ref[...] = dz.astype(BF16)
        dpg_ref[...] = (dz * pp_ref[...].astype(F32) * sg * (1.0 - sg)).astype(BF16)
        dpp_ref[...] = (dz * sg).astype(BF16)
        dg_ref[...] += _colsum(dgx)
        db_ref[...] += _colsum(dh)

    row = pl.BlockSpec((tm, D), lambda i: (i, 0))
    vec = pl.BlockSpec((1, D), lambda i: (0, 0))
    return pl.pallas_call(
        body, name=name, grid=(s // tm,), in_specs=[row, row, vec, row, row] + [ANY] * na,
        out_specs=[row, row, row, row, vec, vec],
        out_shape=[jax.ShapeDtypeStruct((s, D), F32)] + [jax.ShapeDtypeStruct((s, D), BF16)] * 3
        + [jax.ShapeDtypeStruct((1, D), F32)] * 2,
        compiler_params=_cparams(1))(dh, z, g, pg, pp, *after)


def wgrad_pair(rows_a, rows_dy, cols, name):
    specs, args, outs, shapes, kinds = [], [], [], [], []
    if rows_a is not None:
        s, k = rows_a.shape
        n = rows_dy.shape[1]
        specs += [pl.BlockSpec((s, k // N_DEV), lambda j: (0, j)), pl.BlockSpec((s, n), lambda j: (0, 0))]
        args += [rows_a, rows_dy]
        outs.append(pl.BlockSpec((None, k // N_DEV, n), lambda j: (j, 0, 0)))
        shapes.append(jax.ShapeDtypeStruct((N_DEV, k // N_DEV, n), BF16))
    for a, dy in cols:
        s, k = a.shape
        n = dy.shape[1]
        specs += [pl.BlockSpec((s, k), lambda j: (0, 0)), pl.BlockSpec((s, n // N_DEV), lambda j: (0, j))]
        args += [a, dy]
        outs.append(pl.BlockSpec((None, k, n // N_DEV), lambda j: (j, 0, 0)))
        shapes.append(jax.ShapeDtypeStruct((N_DEV, k, n // N_DEV), BF16))
    n_pairs = len(shapes)

    def body(*refs):
        for i in range(n_pairs):
            refs[2 * n_pairs + i][...] = _tn(refs[2 * i][...], refs[2 * i + 1][...]).astype(BF16)

    return pl.pallas_call(body, name=name, grid=(N_DEV,), in_specs=specs, out_specs=outs, out_shape=shapes,
                          compiler_params=_cparams(1))(*args)


def wgrad_in(a, pieces, name):
    s, k = a.shape
    bn = N_PROJ // N_DEV
    n_narrow = 4

    def body(a_ref, *refs):
        dy_refs, (o_ref, cs_ref) = refs[:6], refs[6:]
        j = pl.program_id(0)

        def emit(dy_ref):
            dy = dy_ref[...]
            o_ref[...] = _tn(a_ref[...], dy).astype(BF16)
            cs_ref[...] = _colsum(dy.astype(F32))

        for idx in range(n_narrow):
            pl.when(j == idx)(lambda idx=idx: emit(dy_refs[idx]))
        pl.when((j >= n_narrow) & (j < n_narrow + 2))(lambda: emit(dy_refs[4]))
        pl.when(j >= n_narrow + 2)(lambda: emit(dy_refs[5]))

    narrow = pl.BlockSpec((s, bn), lambda j: (0, 0))
    return pl.pallas_call(
        body, name=name, grid=(N_DEV,),
        in_specs=[pl.BlockSpec((s, k), lambda j: (0, 0))] + [narrow] * n_narrow
        + [pl.BlockSpec((s, bn), lambda j: (0, jnp.clip(j - n_narrow, 0, 1))),
           pl.BlockSpec((s, bn), lambda j: (0, jnp.clip(j - n_narrow - 2, 0, 1)))],
        out_specs=[pl.BlockSpec((None, k, bn), lambda j: (j, 0, 0)), pl.BlockSpec((1, bn), lambda j: (0, j))],
        out_shape=[jax.ShapeDtypeStruct((N_DEV, k, bn), BF16), jax.ShapeDtypeStruct((1, N_PROJ), F32)],
        compiler_params=_cparams(1))(a, *pieces)


def wgrad_down(t, dy, name):
    _, s, k = t.shape
    n = dy.shape[1]

    def body(a_ref, dy_ref, o_ref):
        o_ref[...] = _tn(a_ref[...], dy_ref[...]).astype(BF16)

    return pl.pallas_call(
        body, name=name, grid=(4,),
        in_specs=[pl.BlockSpec((None, s, k), lambda j: (j, 0, 0)), pl.BlockSpec((s, n), lambda j: (0, 0))],
        out_specs=pl.BlockSpec((None, k, n), lambda j: (j, 0, 0)),
        out_shape=jax.ShapeDtypeStruct((4, k, n), BF16),
        compiler_params=_cparams(1))(t, dy)


def wgrad_up(a, dhv, dhg, name):
    s, k = a.shape

    def body(a_ref, dv_ref, dg_ref, o_ref):
        j = pl.program_id(0)

        @pl.when(j < 4)
        def _():
            o_ref[...] = _tn(dv_ref[...], a_ref[...]).astype(BF16)

        @pl.when(j >= 4)
        def _():
            o_ref[...] = _tn(dg_ref[...], a_ref[...]).astype(BF16)

    return pl.pallas_call(
        body, name=name, grid=(N_DEV,),
        in_specs=[pl.BlockSpec((s, k), lambda j: (0, 0)),
                  pl.BlockSpec((None, s, FF_BLK), lambda j: (jnp.minimum(j, 3), 0, 0)),
                  pl.BlockSpec((None, s, FF_BLK), lambda j: (jnp.maximum(j - 4, 0), 0, 0))],
        out_specs=pl.BlockSpec((None, FF_BLK, k), lambda j: (j, 0, 0)),
        out_shape=jax.ShapeDtypeStruct((N_DEV, FF_BLK, k), BF16),
        compiler_params=_cparams(1))(a, dhv, dhg)


def ffn_act_bwd(dzb, wdown, up, conv_w, conv_b, li, name):
    s = up.shape[1]
    tm = 512
    nt = s // tm
    before, after = _halo_maps(tm, s)
    hv_main, hv_prev, hv_next = _slab_specs(tm, s, lambda c: c)
    hg_main, hg_prev, hg_next = _slab_specs(tm, s, lambda c: 4 + c)

    def dc_of(dz, wd, hv, hg, back, fwd, cw_ref, cb_ref):
        dt = _nt(dz, wd)
        c = back * cw_ref[0:1, :] + hg * cw_ref[1:2, :] + fwd * cw_ref[2:3, :] + cb_ref[...]
        cdf = 0.5 * (1.0 + lax.erf(c * _SQRT_HALF))
        pdf = jnp.exp(-0.5 * c * c) * _INV_SQRT_2PI
        return dt, c * cdf, dt * hv * (cdf + c * pdf)

    def body(dz_ref, dzp_ref, dzn_ref, wd_ref, hv_ref, hvp_ref, hvn_ref, hg_ref, hgp_ref, hgn_ref, cw_ref, cb_ref,
             dhv_ref, dhg_ref, dcw_ref, dcb_ref):
        i = pl.program_id(1)

        @pl.when(i == 0)
        def _():
            dcw_ref[...] = jnp.zeros_like(dcw_ref)
            dcb_ref[...] = jnp.zeros_like(dcb_ref)

        wd = _row_cat(wd_ref)
        hg = hg_ref[...].astype(F32)
        hgp = hgp_ref[...].astype(F32)
        hgn = hgn_ref[...].astype(F32)
        first, last = i == 0, i == nt - 1
        e = HALO - 1
        back, fwd = _shift_rows(hg, jnp.where(first, 0.0, hgp[e:e + 1, :]), jnp.where(last, 0.0, hgn[0:1, :]))
        dt, act, dc = dc_of(dz_ref[...], wd, hv_ref[...].astype(F32), hg, back, fwd, cw_ref, cb_ref)
        dhv_ref[...] = (dt * act).astype(BF16)
        bp, fp = _shift_rows(hgp, hgp[0:1, :], hg[0:1, :])
        _, _, dcp = dc_of(dzp_ref[...], wd, hvp_ref[...].astype(F32), hgp, bp, fp, cw_ref, cb_ref)
        bn, fn = _shift_rows(hgn, hg[tm - 1:tm, :], hgn[e:e + 1, :])
        _, _, dcn = dc_of(dzn_ref[...], wd, hvn_ref[...].astype(F32), hgn, bn, fn, cw_ref, cb_ref)
        dc_back, dc_fwd = _shift_rows(dc, jnp.where(first, 0.0, dcp[e:e + 1, :]), jnp.where(last, 0.0, dcn[0:1, :]))
        dhg_ref[...] = (dc_fwd * cw_ref[0:1, :] + dc * cw_ref[1:2, :] + dc_back * cw_ref[2:3, :]).astype(BF16)
        dcw_ref[0:1, :] += _colsum(dc * back)
        dcw_ref[1:2, :] += _colsum(dc * hg)
        dcw_ref[2:3, :] += _colsum(dc * fwd)
        dcb_ref[...] += _colsum(dc)

    out_slab = pl.BlockSpec((None, tm, FF_BLK), lambda c, i: (c, i, 0))
    cw_spec = pl.BlockSpec((None, 3, FF_BLK), lambda c, i: (c, 0, 0))
    cb_spec = pl.BlockSpec((None, 1, FF_BLK), lambda c, i: (c, 0, 0))
    return pl.pallas_call(
        body, name=name, grid=(4, nt),
        in_specs=[pl.BlockSpec((tm, D), lambda c, i: (i, 0)),
                  pl.BlockSpec((HALO, D), lambda c, i: (before(i), 0)),
                  pl.BlockSpec((HALO, D), lambda c, i: (after(i), 0)),
                  _shards(2, FF_SHARD, D, li, lambda c, i: c),
                  hv_main, hv_prev, hv_next, hg_main, hg_prev, hg_next, cw_spec, cb_spec],
        out_specs=[out_slab, out_slab, cw_spec, cb_spec],
        out_shape=[jax.ShapeDtypeStruct((4, s, FF_BLK), BF16), jax.ShapeDtypeStruct((4, s, FF_BLK), BF16),
                   jax.ShapeDtypeStruct((4, 3, FF_BLK), F32), jax.ShapeDtypeStruct((4, 1, FF_BLK), F32)],
        compiler_params=_cparams(2))(dzb, dzb, dzb, wdown, up, up, up, up, up, up, conv_w, conv_b)


def dh1_ln1_bwd(dz2, dpg, wpg, dhv, dhg, wup, z1, g1, li, name, after=()):
    s = dz2.shape[0]
    tm = 256
    na = len(after)

    def body(dz2_ref, dpg_ref, wpg_ref, dhv_ref, dhg_ref, wup_ref, z1_ref, g_ref, *rest):
        dz_ref, dzb_ref, dg_ref, db_ref = rest[na:]

        @pl.when(pl.program_id(0) == 0)
        def _():
            dg_ref[...] = jnp.zeros_like(dg_ref)
            db_ref[...] = jnp.zeros_like(db_ref)
        dh = ALPHA * dz2_ref[...] + _nt(dpg_ref[...], _row_cat(wpg_ref))
        for c in range(4):
            dh = dh + _nn(dhv_ref[c], wup_ref[c]) + _nn(dhg_ref[c], wup_ref[4 + c])
        dz, dgx = _ln_bwd(dh, z1_ref[...], g_ref[...])
        dz_ref[...] = dz
        dzb_ref[...] = dz.astype(BF16)
        dg_ref[...] += _colsum(dgx)
        db_ref[...] += _colsum(dh)

    row = pl.BlockSpec((tm, D), lambda i: (i, 0))
    vec = pl.BlockSpec((1, D), lambda i: (0, 0))
    slab = pl.BlockSpec((4, tm, FF_BLK), lambda i: (0, i, 0))
    return pl.pallas_call(
        body, name=name, grid=(s // tm,),
        in_specs=[row, row, _shards(N_DEV, D // N_DEV, D, li), slab, slab, _shards(N_DEV, FF_BLK, D, li), row, vec]
        + [ANY] * na,
        out_specs=[row, row, vec, vec],
        out_shape=[jax.ShapeDtypeStruct((s, D), F32), jax.ShapeDtypeStruct((s, D), BF16),
                   jax.ShapeDtypeStruct((1, D), F32), jax.ShapeDtypeStruct((1, D), F32)],
        compiler_params=_cparams(1))(dz2, dpg, wpg, dhv, dhg, wup, z1, g1, *after)


def merge_bwd(dz1b, wmix, proj, ya, yp, li, name, after=()):
    s = dz1b.shape[0]
    tm, tn = 512, 512
    nt = D // tn
    per = tn // (D // N_DEV)
    ga0 = (3 * D_ATTN + D_POOL) // tn

    def body(dz_ref, w_ref, ga_ref, gb_ref, ya_ref, yp_ref, *rest):
        dya_ref, dyp_ref, dga_ref, dgb_ref = rest[len(after):]
        dm = _nt(dz_ref[...], _row_cat(w_ref))
        sa = _sigmoid(ga_ref[...].astype(F32))
        sb = _sigmoid(gb_ref[...].astype(F32))
        dya_ref[...] = (dm * sa).astype(BF16)
        dyp_ref[...] = (dm * sb).astype(BF16)
        dga_ref[...] = (dm * ya_ref[...].astype(F32) * sa * (1.0 - sa)).astype(BF16)
        dgb_ref[...] = (dm * yp_ref[...].astype(F32) * sb * (1.0 - sb)).astype(BF16)

    tile = pl.BlockSpec((tm, tn), lambda i, j: (i, j))
    return pl.pallas_call(
        body, name=name, grid=(s // tm, nt),
        in_specs=[pl.BlockSpec((tm, D), lambda i, j: (i, 0)),
                  _shards(per, D // N_DEV, D, li, lambda i, j: j),
                  pl.BlockSpec((tm, tn), lambda i, j: (i, ga0 + j)),
                  pl.BlockSpec((tm, tn), lambda i, j: (i, ga0 + nt + j)),
                  tile, tile] + [ANY] * len(after),
        out_specs=[tile] * 4,
        out_shape=[jax.ShapeDtypeStruct((s, D), BF16)] * 4,
        compiler_params=_cparams(2))(dz1b, wmix, proj, proj, ya, yp, *after)


def attn_out_bwd(dya, wao, li, name, after=()):
    s = dya.shape[0]
    tm = 512

    def body(d_ref, w_ref, *rest):
        rest[-1][...] = _nt(d_ref[...], _lane_cat(w_ref)).astype(BF16)

    return pl.pallas_call(
        body, name=name, grid=(s // tm,),
        in_specs=[pl.BlockSpec((tm, D), lambda i: (i, 0)), _shards(N_DEV, D_ATTN, 128, li)] + [ANY] * len(after),
        out_specs=pl.BlockSpec((tm, D_ATTN), lambda i: (i, 0)),
        out_shape=jax.ShapeDtypeStruct((s, D_ATTN), BF16),
        compiler_params=_cparams(1))(dya, wao, *after)


def pool_bwd(dyp, wpo, pm, pool_w, pool_scale, li, name):
    s = dyp.shape[0]

    def body(dyp_ref, wpo_ref, pm_ref, w_ref, sc_ref, du_ref, dw_ref, dsc_ref):
        wpo = _lane_cat(wpo_ref)
        dyp = dyp_ref[...]
        for g, w in enumerate(POOL_WINDOWS):
            cols = slice(g * PGD, (g + 1) * PGD)
            dpw = _nt(dyp, wpo[g * PGD:(g + 1) * PGD, :])
            pmg = pm_ref[:, cols]
            dsc_ref[:, cols] = _colsum(dpw * _nn(pmg, w_ref[g]))
            dpmw = (dpw * sc_ref[:, cols]).astype(BF16)
            dw_ref[g] = _tn(pmg, dpmw)
            dpm = _nt(dpmw, w_ref[g])
            du_ref[:, cols] = (_window_sum(dpm / _pool_counts(s, w), w, False) - dpm).astype(BF16)

    full = lambda shape: pl.BlockSpec(shape, lambda i: (0,) * len(shape))
    return pl.pallas_call(
        body, name=name, grid=(1,),
        in_specs=[full((s, D)), _shards(N_DEV, D_POOL, 128, li), full((s, D_POOL)), full((4, PGD, PGD)),
                  full((1, D_POOL))],
        out_specs=[full((s, D_POOL)), full((4, PGD, PGD)), full((1, D_POOL))],
        out_shape=[jax.ShapeDtypeStruct((s, D_POOL), BF16), jax.ShapeDtypeStruct((4, PGD, PGD), F32),
                   jax.ShapeDtypeStruct((1, D_POOL), F32)],
        compiler_params=_cparams(1))(dyp, wpo, pm, pool_w, pool_scale)


def attn_bwd(proj, da, e_rev, li, name, after=()):
    s = proj.shape[0]
    nb = s // QB
    skew = GRID_W + (GRID_W - KW)
    group = QROWS
    half_shape = (group * GRID_W, min(group + KH, KROWS) * GRID_W)

    def body(q_ref, k_ref, v_ref, do_ref, e_ref, *rest):
        dq_ref, dk_ref, dv_ref, g_ref, s_ref, dp_ref, ds_ref, p_ref, dkt_acc, dvt_acc = rest[len(after):]
        b = pl.program_id(1)

        @pl.when(b == 0)
        def _():
            dkt_acc[...] = jnp.zeros_like(dkt_acc)
            dvt_acc[...] = jnp.zeros_like(dvt_acc)
            g_ref[...] = jnp.zeros_like(g_ref)

        ri = lax.broadcasted_iota(I32, (QB, QB), 0)
        ci = lax.broadcasted_iota(I32, (QB, QB), 1)
        rev = jnp.where(ri + ci == QB - 1, 1.0, 0.0).astype(BF16)
        q = _nn(rev, q_ref[...]).astype(BF16) * ATT_SCALE
        do = _nn(rev, do_ref[...]).astype(BF16)
        lane = lax.broadcasted_iota(I32, (1, 128), 1)

        def block(btype, k0):
            dqs = []
            for g0 in range(0, QROWS, group):
                h0, hw = _rows_window(btype, QROWS - g0 - group, group)
                hrows = slice(g0 * GRID_W, (g0 + group) * GRID_W)
                kwin = k_ref[pl.ds(_token_at(k0, h0), hw), :]
                vwin = v_ref[pl.ds(_token_at(k0, h0), hw), :]
                dq = jnp.zeros((group * GRID_W, 128), F32)
                for hh in range(2):
                    lm = (lane // HEAD_DIM) == hh
                    qh = jnp.where(lm, q[hrows], jnp.zeros_like(q[hrows]))
                    doh = jnp.where(lm, do[hrows], jnp.zeros_like(do[hrows]))
                    kh = jnp.where(lm, kwin, jnp.zeros_like(kwin))
                    s_ref[:, 0:hw] = _nt(qh, kwin)
                    dp_ref[:, 0:hw] = _nt(doh, vwin)
                    g = jnp.zeros((1, KB), F32)
                    for r in range(group):
                        qr = QROWS - 1 - (g0 + r)
                        rows = slice(r * GRID_W, (r + 1) * GRID_W)
                        sb, a0, w, pad = _row_logits(s_ref, e_ref, hh, rows, btype, qr, h0)
                        p = jnp.exp(sb - jnp.max(sb, axis=1, keepdims=True))
                        p = p * (1.0 / jnp.sum(p, axis=1, keepdims=True))
                        dp = dp_ref[rows, a0:a0 + w]
                        ds = p * (dp - jnp.sum(p * dp, axis=1, keepdims=True))
                        _store_row(ds_ref, rows, a0, w, ds, hw)
                        _store_row(p_ref, rows, a0, w, p, hw)
                        t = jnp.sum(pltpu.roll(ds, w - skew, 1, stride=1, stride_axis=0), axis=0, keepdims=True)
                        t = t[:, :KH * GRID_W] if pad else pltpu.roll(t, GRID_W, 1)
                        i0 = _attn_row(btype, qr)[2]
                        g = g + pltpu.roll(jnp.concatenate([t, jnp.zeros_like(t)], axis=1), i0 * GRID_W, 1)
                    g_ref[hh] += g
                    dsb = ds_ref[:, 0:hw]
                    dq = dq + _nn(dsb, kh) * ATT_SCALE
                    dkt_acc[:, pl.ds(_token_at(k0, h0), hw)] += _tn(qh, dsb)
                    dvt_acc[:, pl.ds(_token_at(k0, h0), hw)] += _tn(doh, p_ref[:, 0:hw])
                dqs.append(dq.astype(BF16))
            dq_ref[...] = _nn(rev, jnp.concatenate(dqs, axis=0)).astype(BF16)

        for btype, (cond, k0) in enumerate(_attn_types(b, nb)):
            pl.when(cond)(lambda btype=btype, k0=k0: block(btype, k0))

        @pl.when(b == nb - 1)
        def _():
            dk_ref[...] = dkt_acc[...].T.astype(BF16)
            dv_ref[...] = dvt_acc[...].T.astype(BF16)

    col = pl.BlockSpec((s, 128), lambda j, b: (0, j))
    return pl.pallas_call(
        body, name=name, grid=(4, nb),
        in_specs=[pl.BlockSpec((QB, 128), lambda j, b: (b, j)),
                  pl.BlockSpec((s, 128), lambda j, b: (0, 4 + j)),
                  pl.BlockSpec((s, 128), lambda j, b: (0, 8 + j)),
                  pl.BlockSpec((QB, 128), lambda j, b: (b, j)),
                  pl.BlockSpec((None, 2, 2, GRID_W, KB), lambda j, b: (li, j, 0, 0, 0))] + [ANY] * len(after),
        out_specs=[pl.BlockSpec((QB, 128), lambda j, b: (b, j)), col, col,
                   pl.BlockSpec((2, 1, KB), lambda j, b: (j, 0, 0))],
        out_shape=[jax.ShapeDtypeStruct((s, D_ATTN), BF16)] * 3 + [jax.ShapeDtypeStruct((N_HEADS, 1, KB), F32)],
        scratch_shapes=[pltpu.VMEM(half_shape, F32), pltpu.VMEM(half_shape, F32), pltpu.VMEM(half_shape, BF16),
                        pltpu.VMEM(half_shape, BF16), pltpu.VMEM((128, s), F32), pltpu.VMEM((128, s), F32)],
        compiler_params=_cparams(2))(proj, proj, proj, da, e_rev, *after)


def dh0_bwd(dz1, pieces, win, li, name, after=()):
    s = dz1.shape[0]
    tm = 512
    bn = N_PROJ // N_DEV

    def body(dz_ref, q_ref, k_ref, v_ref, u_ref, ga_ref, gb_ref, w_ref, *rest):
        acc = ALPHA * dz_ref[...]
        for j, ref in enumerate((q_ref, k_ref, v_ref, u_ref)):
            acc = acc + _nt(ref[...], w_ref[j])
        for j, ref in ((4, ga_ref), (6, gb_ref)):
            acc = acc + _nt(ref[:, 0:bn], w_ref[j]) + _nt(ref[:, bn:2 * bn], w_ref[j + 1])
        rest[-1][...] = acc

    row = pl.BlockSpec((tm, D), lambda i: (i, 0))
    narrow = pl.BlockSpec((tm, bn), lambda i: (i, 0))
    return pl.pallas_call(
        body, name=name, grid=(s // tm,),
        in_specs=[row] + [narrow] * 4 + [row, row, _shards(N_DEV, D, bn, li)] + [ANY] * len(after),
        out_specs=row, out_shape=jax.ShapeDtypeStruct((s, D), F32),
        compiler_params=_cparams(1))(dz1, *pieces, win, *after)


def _coords():
    return lax.axis_index("x"), lax.axis_index("y"), lax.axis_index("c")


def _dev_index(px, py, pc):
    return 4 * px + 2 * py + pc


HBM = pl.BlockSpec(memory_space=pltpu.HBM)
SEM = pl.BlockSpec(memory_space=pltpu.SEMAPHORE)
_EFFECT = pltpu.SideEffectType.DATAFLOW_SIDE_EFFECTING
_TOKEN = jax.ShapeDtypeStruct((8, 128), F32)


def _in_hbm(a):
    return pltpu.with_memory_space_constraint(a, pltpu.HBM)


def _hbm_like(a):
    return pltpu.HBM(a.shape, a.dtype)


def _peers(x, y, c):
    return [(x, y, 1 - c), (1 - x, y, c), (x, 1 - y, c), (1 - x, 1 - y, c)]


def ag_start(lands, after, name):
    n = len(lands)

    def body(*refs):
        land = refs[:n]
        send_sem, recv_sem, token = refs[n + 1], refs[n + 2], refs[-1]
        x, y, c = _coords()
        me = _dev_index(x, y, c)
        for k, peer in enumerate(_peers(x, y, c)):
            for a in range(n):
                pltpu.make_async_remote_copy(src_ref=land[a].at[me], dst_ref=land[a].at[me], send_sem=send_sem.at[k],
                                             recv_sem=recv_sem.at[k], device_id=peer, device_id_type=MESH).start()
        token[...] = jnp.zeros_like(token)

    res = pl.pallas_call(
        body, name=name,
        out_shape=(pltpu.SemaphoreType.DMA((4,)), pltpu.SemaphoreType.DMA((4,)), *[_hbm_like(l) for l in lands], _TOKEN),
        in_specs=[HBM] * n + [ANY], out_specs=(SEM, SEM, *[HBM] * n, pl.BlockSpec(memory_space=pltpu.VMEM)),
        input_output_aliases={a: 2 + a for a in range(n)},
        compiler_params=pltpu.CompilerParams(has_side_effects=_EFFECT),
    )(*[_in_hbm(l) for l in lands], after)
    return res[0], res[1], list(res[2:2 + n]), res[-1]


def ag_forward(send_sem, recv_sem, lands, after, name):
    n = len(lands)

    def body(*refs):
        send_sem, recv_sem = refs[0], refs[1]
        land = refs[2:2 + n]
        fsend, frecv = refs[3 + n], refs[4 + n]
        x, y, c = _coords()
        peers = _peers(x, y, c)
        for k in range(1, 4):
            blk = _dev_index(*peers[k])
            for a in range(n):
                pltpu.make_async_remote_copy(src_ref=land[a].at[blk], dst_ref=land[a].at[blk], send_sem=send_sem.at[k],
                                             recv_sem=recv_sem.at[k], device_id=peers[k], device_id_type=MESH).wait_recv()
        for k in range(1, 4):
            blk = _dev_index(*peers[k])
            for a in range(n):
                pltpu.make_async_remote_copy(src_ref=land[a].at[blk], dst_ref=land[a].at[blk], send_sem=fsend.at[k - 1],
                                             recv_sem=frecv.at[k - 1], device_id=peers[0], device_id_type=MESH).start()

    res = pl.pallas_call(
        body, name=name,
        out_shape=(pltpu.SemaphoreType.DMA((3,)), pltpu.SemaphoreType.DMA((3,)), *[_hbm_like(l) for l in lands]),
        in_specs=[SEM, SEM, *[HBM] * n, ANY], out_specs=(SEM, SEM, *[HBM] * n),
        input_output_aliases={2 + a: 2 + a for a in range(n)},
        compiler_params=pltpu.CompilerParams(has_side_effects=_EFFECT),
    )(send_sem, recv_sem, *lands, after)
    return res[0], res[1], list(res[2:])


def ag_finish(send_sem, recv_sem, fsend, frecv, lands, after, name):
    n = len(lands)

    def body(*refs):
        send_sem, recv_sem, fsend, frecv = refs[:4]
        land = refs[4:4 + n]
        x, y, c = _coords()
        me = _dev_index(x, y, c)
        peers = _peers(x, y, c)
        for k in range(4):
            for a in range(n):
                pltpu.make_async_remote_copy(src_ref=land[a].at[me], dst_ref=land[a].at[me], send_sem=send_sem.at[k],
                                             recv_sem=recv_sem.at[k], device_id=peers[k], device_id_type=MESH).wait_send()
        sib = _dev_index(*peers[0])
        for a in range(n):
            pltpu.make_async_remote_copy(src_ref=land[a].at[sib], dst_ref=land[a].at[sib], send_sem=send_sem.at[0],
                                         recv_sem=recv_sem.at[0], device_id=peers[0], device_id_type=MESH).wait_recv()
        for k in range(1, 4):
            mine = _dev_index(*peers[k])
            theirs = _dev_index(peers[k][0], peers[k][1], 1 - c)
            for a in range(n):
                pltpu.make_async_remote_copy(src_ref=land[a].at[mine], dst_ref=land[a].at[theirs], send_sem=fsend.at[k - 1],
                                             recv_sem=frecv.at[k - 1], device_id=peers[0], device_id_type=MESH).wait()

    res = pl.pallas_call(
        body, name=name, out_shape=tuple(_hbm_like(l) for l in lands),
        in_specs=[SEM] * 4 + [HBM] * n + [ANY], out_specs=tuple([HBM] * n),
        input_output_aliases={4 + a: a for a in range(n)},
        compiler_params=pltpu.CompilerParams(has_side_effects=_EFFECT),
    )(send_sem, recv_sem, fsend, frecv, *lands, after)
    return list(res)


def rs_start(psums, name):
    n = len(psums)
    lands = [lax.empty(p.shape, p.dtype) for p in psums]

    def body(*refs):
        src, land = refs[:n], refs[n:2 * n]
        send_sem, recv_sem, token = refs[2 * n], refs[2 * n + 1], refs[-1]
        peers = _peers(*_coords())
        for k in range(3):
            for a in range(n):
                pltpu.make_async_remote_copy(src_ref=src[a].at[k], dst_ref=land[a].at[k], send_sem=send_sem.at[k],
                                             recv_sem=recv_sem.at[k], device_id=peers[k + 1], device_id_type=MESH).start()
        token[...] = jnp.zeros_like(token)

    res = pl.pallas_call(
        body, name=name,
        out_shape=(pltpu.SemaphoreType.DMA((3,)), pltpu.SemaphoreType.DMA((3,)), *[_hbm_like(p) for p in psums],
                   *[_hbm_like(l) for l in lands], _TOKEN),
        in_specs=[HBM] * (2 * n), out_specs=(SEM, SEM, *[HBM] * (2 * n), pl.BlockSpec(memory_space=pltpu.VMEM)),
        input_output_aliases={a: 2 + a for a in range(2 * n)},
        compiler_params=pltpu.CompilerParams(has_side_effects=_EFFECT),
    )(*[_in_hbm(p) for p in psums], *[_in_hbm(l) for l in lands])
    return res[0], res[1], list(res[2:2 + n]), list(res[2 + n:2 + 2 * n]), res[-1]


def rs_finish(send_sem, recv_sem, psums, lands, after, name):
    n = len(psums)

    def body(*refs):
        send_sem, recv_sem = refs[0], refs[1]
        src, land = refs[2:2 + n], refs[2 + n:2 + 2 * n]
        peers = _peers(*_coords())
        for k in range(3):
            for a in range(n):
                pltpu.make_async_remote_copy(src_ref=src[a].at[k], dst_ref=land[a].at[k], send_sem=send_sem.at[k],
                                             recv_sem=recv_sem.at[k], device_id=peers[k + 1], device_id_type=MESH).wait()

    res = pl.pallas_call(
        body, name=name, out_shape=tuple(_hbm_like(l) for l in lands),
        in_specs=[SEM, SEM] + [HBM] * (2 * n) + [ANY], out_specs=tuple([HBM] * n),
        input_output_aliases={2 + n + a: a for a in range(n)},
        compiler_params=pltpu.CompilerParams(has_side_effects=_EFFECT),
    )(send_sem, recv_sem, *psums, *lands, after)
    return list(res)


def d2d_start(grads, name):
    n = len(grads)
    lands = [lax.empty((4,) + g.shape[1:], g.dtype) for g in grads]

    def body(*refs):
        src, land = refs[:n], refs[n:2 * n]
        send_sem, recv_sem, token = refs[2 * n], refs[2 * n + 1], refs[-1]
        x, y, c = _coords()
        for a in range(n):
            for k in range(4):
                blk = _dev_index(x ^ (k & 1), y ^ (k >> 1), 1 - c)
                pltpu.make_async_remote_copy(src_ref=src[a].at[blk], dst_ref=land[a].at[k], send_sem=send_sem.at[0],
                                             recv_sem=recv_sem.at[0], device_id=(x, y, 1 - c), device_id_type=MESH).start()
        token[...] = jnp.zeros_like(token)

    res = pl.pallas_call(
        body, name=name,
        out_shape=(pltpu.SemaphoreType.DMA((1,)), pltpu.SemaphoreType.DMA((1,)), *[_hbm_like(g) for g in grads],
                   *[_hbm_like(l) for l in lands], _TOKEN),
        in_specs=[HBM] * (2 * n), out_specs=(SEM, SEM, *[HBM] * (2 * n), pl.BlockSpec(memory_space=pltpu.VMEM)),
        input_output_aliases={a: 2 + a for a in range(2 * n)},
        compiler_params=pltpu.CompilerParams(has_side_effects=_EFFECT),
    )(*[_in_hbm(g) for g in grads], *[_in_hbm(l) for l in lands])
    return res[0], res[1], list(res[2:2 + n]), list(res[2 + n:2 + 2 * n]), res[-1]


def d2d_finish(send_sem, recv_sem, grads, lands, after, name):
    n = len(grads)

    def body(*refs):
        send_sem, recv_sem = refs[0], refs[1]
        src, land = refs[2:2 + n], refs[2 + n:2 + 2 * n]
        x, y, c = _coords()
        for a in range(n):
            for k in range(4):
                blk = _dev_index(x ^ (k & 1), y ^ (k >> 1), 1 - c)
                pltpu.make_async_remote_copy(src_ref=src[a].at[blk], dst_ref=land[a].at[k], send_sem=send_sem.at[0],
                                             recv_sem=recv_sem.at[0], device_id=(x, y, 1 - c), device_id_type=MESH).wait()

    res = pl.pallas_call(
        body, name=name, out_shape=tuple(_hbm_like(t) for t in list(grads) + list(lands)),
        in_specs=[SEM, SEM] + [HBM] * (2 * n) + [ANY], out_specs=tuple([HBM] * (2 * n)),
        input_output_aliases={2 + a: a for a in range(2 * n)},
        compiler_params=pltpu.CompilerParams(has_side_effects=_EFFECT),
    )(send_sem, recv_sem, *grads, *lands, after)
    return list(res[:n]), list(res[n:])


def pair_add(blk_idx, g, recv, name):
    _, r, c = g.shape
    tr = _row_tile(r)

    def body(idx_ref, g0, g1, g2, g3, r_ref, own_ref, oth_ref):
        own_ref[...] = g0[...].astype(F32) + r_ref[0].astype(F32)
        for k, gk in enumerate((g1, g2, g3)):
            oth_ref[k] = (gk[...].astype(F32) + r_ref[k + 1].astype(F32)).astype(BF16)

    def blk(k):
        return pl.BlockSpec((None, tr, c), lambda t, idx: (idx[k], t, 0))

    grid_spec = pltpu.PrefetchScalarGridSpec(
        num_scalar_prefetch=1, grid=(r // tr,),
        in_specs=[blk(0), blk(1), blk(2), blk(3), pl.BlockSpec((4, tr, c), lambda t, idx: (0, t, 0))],
        out_specs=[pl.BlockSpec((tr, c), lambda t, idx: (t, 0)), pl.BlockSpec((3, tr, c), lambda t, idx: (0, t, 0))])
    return pl.pallas_call(
        body, name=name, grid_spec=grid_spec,
        out_shape=[jax.ShapeDtypeStruct((r, c), F32), jax.ShapeDtypeStruct((3, r, c), BF16)],
        compiler_params=_cparams(1))(blk_idx, g, g, g, g, recv)


def _row_tile(r):
    return next(t for t in (512, 352, 256, 128) if r % t == 0)


def _adamw(w, g, m, v):
    m = ADAM_B1 * m + (1.0 - ADAM_B1) * g
    v = ADAM_B2 * v + (1.0 - ADAM_B2) * (g * g)
    m_hat = m / (1.0 - ADAM_B1 ** ADAM_STEP)
    v_hat = v / (1.0 - ADAM_B2 ** ADAM_STEP)
    delta = -ADAM_LR * (m_hat / (jnp.sqrt(v_hat) + ADAM_EPS) + ADAM_WD * w)
    return delta, m, v


def adamw_shard(own, recv, w, m, v, li, prev, name):
    r, c = own.shape
    tr = _row_tile(r)

    def body(own_ref, recv_ref, w_ref, m_ref, v_ref, p0, p1, p2, p3, g_ref, d_ref, nm_ref, nv_ref):
        g = own_ref[...] + recv_ref[0].astype(F32) + recv_ref[1].astype(F32) + recv_ref[2].astype(F32)
        delta, nm, nv = _adamw(w_ref[...], g, m_ref[...], v_ref[...])
        g_ref[...] = g
        d_ref[...] = delta
        nm_ref[...] = nm
        nv_ref[...] = nv

    lay = pl.BlockSpec((None, tr, c), lambda t: (li, t, 0))
    stack = jax.ShapeDtypeStruct((DEPTH, r, c), F32)
    return pl.pallas_call(
        body, name=name, grid=(r // tr,),
        in_specs=[pl.BlockSpec((tr, c), lambda t: (t, 0)), pl.BlockSpec((3, tr, c), lambda t: (0, t, 0)),
                  lay, lay, lay, ANY, ANY, ANY, ANY],
        out_specs=[lay] * 4, out_shape=[stack] * 4,
        input_output_aliases={5: 0, 6: 1, 7: 2, 8: 3},
        compiler_params=_cparams(1))(own, recv, w, m, v, *prev)


def sum_partials(gathered, name):
    _, r, c = gathered.shape
    tr = next(t for t in (96, 88, 64, _PACK_TILE) if r % t == 0)

    def body(gs_ref, g_ref):
        g = gs_ref[0]
        for d in range(1, N_DEV):
            g = g + gs_ref[d]
        g_ref[...] = g

    return pl.pallas_call(
        body, name=name, grid=(r // tr,),
        in_specs=[pl.BlockSpec((N_DEV, tr, c), lambda t: (0, t, 0))],
        out_specs=pl.BlockSpec((tr, c), lambda t: (t, 0)), out_shape=jax.ShapeDtypeStruct((r, c), F32),
        compiler_params=_cparams(1))(gathered)


def adamw_plain(g, w, m, v, name):
    def body(g_ref, w_ref, m_ref, v_ref, d_ref, nm_ref, nv_ref):
        delta, nm, nv = _adamw(w_ref[...], g_ref[...], m_ref[...], v_ref[...])
        d_ref[...] = delta
        nm_ref[...] = nm
        nv_ref[...] = nv

    return pl.pallas_call(body, name=name, out_shape=[jax.ShapeDtypeStruct(w.shape, F32)] * 3)(g, w, m, v)


_PACK_LAYER = (("b_in", (N_PROJ,)), ("rpb", (N_HEADS, 2 * KH - 1, 2 * KW - 1)), ("pool_w", (4, PGD, PGD)),
               ("pool_scale", (D_POOL,)), ("ln1_g", (D,)), ("ln1_b", (D,)), ("conv_b", (D_FF,)), ("ln2_g", (D,)),
               ("ln2_b", (D,)), ("conv_w", (3, D_FF)))
_PACK_INPUT = (("ln_in_g", (D,)), ("ln_in_b", (D,)))
_PACK_LANES = 1024
_PACK_TILE = 8
_EARLY = tuple(range(1, DEPTH))


def _pack_items(layers):
    items = [(n, (len(layers),) + s) for n, s in _PACK_LAYER]
    return items + ([(n, s) for n, s in _PACK_INPUT] if 0 in layers else [])


def _pack(parts, layers):
    flats = [(parts[name] if (name, shape) in _PACK_INPUT else jnp.stack([parts[name][li] for li in layers]))
             .reshape(-1).astype(F32) for name, shape in _pack_items(layers)]
    used = sum(f.shape[0] for f in flats)
    tile = _PACK_TILE * _PACK_LANES
    total = -(-used // tile) * tile
    return jnp.concatenate(flats + [jnp.zeros((total - used,), F32)]).reshape(total // _PACK_LANES, _PACK_LANES)


def _unpack(packed, layers):
    flat, out, off = packed.reshape(-1), {}, 0
    for name, shape in _pack_items(layers):
        n = int(np.prod(shape))
        out[name] = flat[off:off + n].reshape(shape)
        off += n
    return out


def _bias_tables(rpb):
    qc = np.arange(GRID_W)[:, None]
    kc = np.arange(GRID_W)[None, :]
    start = np.clip(qc - KW // 2, 0, GRID_W - KW)
    valid = (kc >= start) & (kc < start + KW)
    col = np.clip(kc - qc, -(KW - 1), KW - 1) + KW - 1
    onehot = (col.reshape(-1)[None, :] == np.arange(2 * KW - 1)[:, None]).astype(np.float32)
    depth = rpb.shape[0]
    rows = jnp.pad(rpb, ((0, 0), (0, 0), (0, 1), (0, 0)))
    tab = jnp.einsum("lhij,jm->lhim", rows, jnp.asarray(onehot), precision=lax.Precision.HIGHEST)
    tab = tab.reshape(depth, N_HEADS, KROWS, GRID_W, GRID_W).transpose(0, 1, 3, 2, 4)
    ok = valid[:, None, :] & (np.arange(KROWS) < 2 * KH - 1)[None, :, None]
    tab = jnp.where(jnp.asarray(ok), tab, NEG_INF).reshape(depth, N_HEADS, GRID_W, KB)
    tab = jnp.stack([tab, jnp.roll(tab, GRID_W, axis=-1)], axis=2)
    return tab, tab[:, :, :, ::-1, :]


_SHARDED = ("w_in", "w_attn_out", "w_pool_out", "w_mix_out", "w_up", "w_down", "w_ple_gate", "w_ple_proj")
_NAMES = ("ln_in_g", "ln_in_b", "w_in", "b_in", "rpb", "w_attn_out", "pool_w", "pool_scale", "w_pool_out", "w_mix_out",
          "ln1_g", "ln1_b", "w_up", "conv_w", "conv_b", "w_down", "w_ple_gate", "w_ple_proj", "ln2_g", "ln2_b")


def kernel(x, p, ln_in_g, ln_in_b, w_in, b_in, rpb, w_attn_out, pool_w, pool_scale, w_pool_out, w_mix_out, ln1_g, ln1_b, w_up, conv_w, conv_b, w_down, w_ple_gate, w_ple_proj, ln2_g, ln2_b, loss_target, m_ln_in_g, m_ln_in_b, m_w_in, m_b_in, m_rpb, m_w_attn_out, m_pool_w, m_pool_scale, m_w_pool_out, m_w_mix_out, m_ln1_g, m_ln1_b, m_w_up, m_conv_w, m_conv_b, m_w_down, m_w_ple_gate, m_w_ple_proj, m_ln2_g, m_ln2_b, v_ln_in_g, v_ln_in_b, v_w_in, v_b_in, v_rpb, v_w_attn_out, v_pool_w, v_pool_scale, v_w_pool_out, v_w_mix_out, v_ln1_g, v_ln1_b, v_w_up, v_conv_w, v_conv_b, v_w_down, v_w_ple_gate, v_w_ple_proj, v_ln2_g, v_ln2_b):
    a = dict(locals())
    W = {n: a[n] for n in _NAMES}
    M = {n: a["m_" + n] for n in _NAMES}
    V = {n: a["v_" + n] for n in _NAMES}
    xi, yi, ci = _coords()
    me = _dev_index(xi, yi, ci)
    x2, tgt = x[0], loss_target[0]
    pb = p[:, 0].astype(BF16)

    flip = lambda d: {**d, "w_up": d["w_up"].transpose(0, 2, 1)}
    ex = _Exchange(flip(W), flip(M), flip(V))
    loss_part, dx, parts = _local_step(x2, tgt, pb, W, ex)
    loss = lax.psum(loss_part[0, 0], AXES)

    started = ex.replicated_start("late", _pack(parts, (0,)), dx)
    done = ex.update(range(DEPTH - 1, 0, -1), started)
    ex.replicated_forward("late", done)
    done = ex.update((0,), done)
    stacks = {**ex.stacks, "w_up": [t.transpose(0, 2, 1) for t in ex.stacks["w_up"]]}
    lo, hi = [_unpack(sum_partials(ex.replicated_finish(tag, done), f"sum_replicated_{tag}"), layers)
              for layers, tag in (((0,), "late"), (_EARLY, "early"))]
    grads = {**{n: jnp.concatenate([lo[n], hi[n]]) for n, _ in _PACK_LAYER}, **{n: lo[n] for n, _ in _PACK_INPUT}}
    grads["conv_w"] = lax.dynamic_slice_in_dim(grads["conv_w"], me * FF_SHARD, FF_SHARD, axis=2)
    res = [{n: stacks[n][k] for n in _SHARDED} for k in range(4)]
    for n, g in grads.items():
        two_d = lambda t: t.reshape(-1, t.shape[-1])
        outs = adamw_plain(two_d(g), two_d(W[n]), two_d(M[n]), two_d(V[n]), f"adamw_{n}")
        for d, o in zip(res, [g] + [o.reshape(W[n].shape) for o in outs]):
            d[n] = o
    return (loss, dx[None], *[res[k][n] for k in range(4) for n in _NAMES])


class _Exchange:
    GROUPS = (("w_ple_gate", "w_ple_proj", "w_down", "w_up"), ("w_mix_out", "w_attn_out", "w_pool_out"), ("w_in",))
    FIRST = ("w_in",)

    def __init__(self, W, M, V):
        self.W, self.M, self.V = W, M, V
        xi, yi, ci = _coords()
        me = _dev_index(xi, yi, ci)
        self.me = me.astype(I32).reshape(1)
        self.rel_idx = jnp.stack([_dev_index(xi ^ (k & 1), yi ^ (k >> 1), ci) for k in range(4)]).astype(I32)
        self.lands = [{n: lax.dynamic_update_index_in_dim(lax.empty((N_DEV,) + W[n].shape[1:], BF16),
                                                          W[n][li].astype(BF16), me, 0) for n in _SHARDED}
                      for li in range(DEPTH)]
        cw_land = lax.dynamic_update_index_in_dim(lax.empty((N_DEV,) + W["conv_w"].shape, F32), W["conv_w"], me, 0)
        self.ag, self.fwd, self.rs, self.pending, self.small = {}, {}, {}, {}, {}
        self.stacks = {n: [lax.empty((DEPTH,) + W[n].shape[1:], F32) for _ in range(4)] for n in _SHARDED}
        self.late = tuple(n for n in _SHARDED if n not in self.FIRST)
        self.ag[0] = ag_start([self.lands[0][n] for n in self.FIRST] + [cw_land], W["conv_w"], "ag_start0")

    def tokens(self):
        return [self.ag[0][3]]

    def prefetch(self, li, after):
        send, recv, lands, _ = self.ag[li]
        self.fwd[li] = ag_forward(send, recv, lands, after, f"ag_forward{li}")
        if li == 0:
            self.ag["0b"] = ag_start([self.lands[0][n] for n in self.late], self.fwd[0][2][0], "ag_start0b")

    def weights(self, li, after):
        send, recv, _, _ = self.ag.pop(li)
        fsend, frecv, lands = self.fwd.pop(li)
        lands = ag_finish(send, recv, fsend, frecv, lands, after, f"ag_finish{li}")
        if li == 0:
            self.cw = lands[-1].transpose(1, 2, 0, 3).reshape(DEPTH, 3, 4, FF_BLK).transpose(0, 2, 1, 3)
            return dict(zip(self.FIRST, lands)), self.cw[li], (self.ag["0b"][3],)
        tokens = ()
        if li + 1 < DEPTH:
            self.ag[li + 1] = ag_start([self.lands[li + 1][n] for n in _SHARDED], lands[0], f"ag_start{li + 1}")
            tokens = (self.ag[li + 1][3],)
        return dict(zip(_SHARDED, lands)), self.cw[li], tokens

    def rest(self, li, G, mid, after):
        if li != 0:
            return G, ()
        send, recv, lands, _ = self.ag.pop("0b")
        fsend, frecv, lands = ag_forward(send, recv, lands, mid, "ag_forward0b")
        lands = ag_finish(send, recv, fsend, frecv, lands, after, "ag_finish0b")
        self.ag[1] = ag_start([self.lands[1][n] for n in _SHARDED], lands[0], "ag_start1")
        return {**G, **dict(zip(self.late, lands))}, (self.ag[1][3],)

    def grads(self, li, group, gw):
        self.pending.setdefault(li, {}).update(gw)
        if li != 0 and group != len(self.GROUPS) - 1:
            return None
        gw = self.pending.pop(li)
        tag = f"{li}_{group}" if li == 0 else f"{li}"
        send, recv, glist, lands, token = d2d_start(list(gw.values()), f"d2d_start{tag}")
        self.d2d = (tag, tuple(gw), send, recv, glist, lands)
        return token

    def flush(self, li, group, after):
        if li != 0 and group != len(self.GROUPS) - 1:
            return None
        tag, names, send, recv, glist, lands = self.d2d
        glist, recv1 = d2d_finish(send, recv, glist, lands, after, f"d2d_finish{tag}")
        sums = [pair_add(self.rel_idx, g, r1, f"pair_add_{n}{li}") for n, g, r1 in zip(names, glist, recv1)]
        send, recv, psums, lands, token = rs_start([s_[1] for s_ in sums], f"rs_start{tag}")
        self.rs.setdefault(li, []).append((tag, names, send, recv, psums, lands, [s_[0] for s_ in sums]))
        if li == 0 and group == 1 and "early" in self.small:
            self.replicated_forward("early", token)
        return token

    def update(self, layers, after):
        for li in layers:
            for tag, names, send, recv, psums, lands, owns in self.rs.pop(li):
                recv2 = rs_finish(send, recv, psums, lands, after, f"rs_finish{tag}")
                for n, own, r2 in zip(names, owns, recv2):
                    self.stacks[n] = adamw_shard(own, r2, self.W[n], self.M[n], self.V[n], li, self.stacks[n],
                                                 f"adamw_{n}{li}")
                    after = self.stacks[n][0]
        return after

    def replicated_start(self, tag, pack, after):
        land = lax.dynamic_update_index_in_dim(lax.empty((N_DEV,) + pack.shape, F32), pack, self.me[0], 0)
        self.small[tag] = ag_start([land], after, f"ag_start_small_{tag}")
        return self.small[tag][3]

    def replicated_early(self, small, after):
        return self.replicated_start("early", _pack(small, _EARLY), after)

    def replicated_forward(self, tag, after):
        send, recv, lands, _ = self.small[tag]
        self.small[tag] = (send, recv) + ag_forward(send, recv, lands, after, f"ag_forward_small_{tag}")

    def replicated_finish(self, tag, after):
        send, recv, fsend, frecv, lands = self.small.pop(tag)
        return ag_finish(send, recv, fsend, frecv, lands, after, f"ag_finish_small_{tag}")[0]


def _local_step(x2, tgt, pb, W, ex):
    depth = W["rpb"].shape[0]
    vec = lambda t: t.reshape(1, -1)
    ln1_g, ln1_b, ln2_g, ln2_b = W["ln1_g"], W["ln1_b"], W["ln2_g"], W["ln2_b"]
    b_in, rpb, pool_scale = W["b_in"], W["rpb"], W["pool_scale"]
    cb_full = W["conv_b"].reshape(depth, 4, 1, FF_BLK)
    pool_w_b = W["pool_w"].astype(BF16)
    e_tab, e_rev = _bias_tables(rpb)

    h, hb = ln_fwd(x2, vec(W["ln_in_g"]), vec(W["ln_in_b"]), "ln_in", after=ex.tokens())
    ex.prefetch(0, hb)
    saved = []
    for li in range(depth):
        G, cw, tokens = ex.weights(li, hb)
        bias = vec(b_in[li])
        proj, u = proj_fwd(hb, G["w_in"], bias, li, f"proj{li}", after=tokens)
        att = attn_fwd(proj, e_tab, li, f"attn{li}")
        pm, pw = pool_fwd(u, pool_w_b[li], vec(pool_scale[li]), f"pool{li}")
        G, tokens = ex.rest(li, G, att, pw)
        mg, ya, yp = merge_fwd(att, pw, G["w_attn_out"], G["w_pool_out"], proj, li, f"merge{li}", after=tokens)
        if li + 1 < depth:
            ex.prefetch(li + 1, mg)
        z1, h1, h1b = mix_ln_fwd(mg, G["w_mix_out"], h, vec(ln1_g[li]), vec(ln1_b[li]), li, f"mix_ln{li}")
        up = up_fwd(h1b, G["w_up"], li, f"up{li}")
        t = ffn_act_fwd(up, cw, cb_full[li], f"ffn_act{li}")
        z2, h2, h2b, pg, pp = down_ple_ln_fwd(t, G["w_down"], h1b, G["w_ple_gate"], pb[li], G["w_ple_proj"], h1,
                                              vec(ln2_g[li]), vec(ln2_b[li]), li, f"down_ln{li}")
        saved.append(dict(hb=hb, proj=proj, att=att, pm=pm, pw=pw, mg=mg, ya=ya, yp=yp, z1=z1, h1b=h1b, up=up, t=t,
                          z2=z2, pg=pg, pp=pp, G=G, cw=cw))
        h, hb = h2, h2b

    dh, loss_part = loss_bwd(h, tgt, "loss")
    small = {n: [None] * depth for n in ("b_in", "rpb", "pool_w", "pool_scale", "ln1_g", "ln1_b", "conv_b", "ln2_g",
                                         "ln2_b", "conv_w")}
    token = ()
    tok = lambda t: () if t is None else (t,)
    for li in reversed(range(depth)):
        sv = saved[li]
        G, cw = sv["G"], sv["cw"]
        dz2, dz2b, dpg, dpp, dg2, db2 = ln2_ple_bwd(dh, sv["z2"], vec(ln2_g[li]), sv["pg"], sv["pp"], f"ln2_bwd{li}",
                                                    after=token)
        gw = {}
        gw["w_ple_gate"], gw["w_ple_proj"] = wgrad_pair(sv["h1b"], dpg, [(pb[li], dpp)], f"dw_ple{li}")
        gw["w_down"] = wgrad_down(sv["t"], dz2b, f"dw_down{li}").reshape(N_DEV, FF_SHARD, D)
        dhv, dhg, dcw, dcb = ffn_act_bwd(dz2b, G["w_down"], sv["up"], cw, cb_full[li], li, f"ffn_bwd{li}")
        gw["w_up"] = wgrad_up(sv["h1b"], dhv, dhg, f"dw_up{li}")
        token = tok(ex.grads(li, 0, gw))
        dz1, dz1b, dg1, db1 = dh1_ln1_bwd(dz2, dpg, G["w_ple_gate"], dhv, dhg, G["w_up"], sv["z1"], vec(ln1_g[li]), li,
                                          f"ln1_bwd{li}", after=token)
        token = tok(ex.flush(li, 0, dz1b))
        dya, dyp, dga, dgb = merge_bwd(dz1b, G["w_mix_out"], sv["proj"], sv["ya"], sv["yp"], li, f"merge_bwd{li}",
                                       after=token)
        gw = dict(zip(("w_mix_out", "w_attn_out", "w_pool_out"),
                      wgrad_pair(sv["mg"], dz1b, [(sv["att"], dya), (sv["pw"], dyp)], f"dw_out{li}")))
        token = tok(ex.grads(li, 1, gw))
        da = attn_out_bwd(dya, G["w_attn_out"], li, f"da{li}", after=token)
        du, dpool_w, dpool_sc = pool_bwd(dyp, G["w_pool_out"], sv["pm"], pool_w_b[li], vec(pool_scale[li]), li,
                                         f"pool_bwd{li}")
        token = tok(ex.flush(li, 1, du))
        dq, dk, dv, drpb = attn_bwd(sv["proj"], da, e_rev, li, f"attn_bwd{li}", after=token)
        dproj = [dq, dk, dv, du, dga, dgb]
        dw_in, db_in = wgrad_in(sv["hb"], dproj, f"dw_in{li}")
        token = tok(ex.grads(li, 2, {"w_in": dw_in}))
        dh = dh0_bwd(dz1, dproj, G["w_in"], li, f"dh0{li}", after=token)
        small["b_in"][li] = db_in.reshape(N_PROJ)
        small["rpb"][li] = drpb.reshape(N_HEADS, KROWS, GRID_W)[:, :2 * KH - 1, :2 * KW - 1]
        small["pool_w"][li] = dpool_w
        small["pool_scale"][li] = dpool_sc.reshape(D_POOL)
        small["ln1_g"][li], small["ln1_b"][li] = dg1.reshape(D), db1.reshape(D)
        small["ln2_g"][li], small["ln2_b"][li] = dg2.reshape(D), db2.reshape(D)
        small["conv_b"][li] = dcb.reshape(D_FF)
        small["conv_w"][li] = dcw.transpose(1, 0, 2).reshape(3, D_FF)
        token = tok(ex.flush(li, 2, dh))
        if li == 1:
            token = token + tok(ex.replicated_early(small, dh))
    dx, dg_in, db_in0 = ln_bwd(dh, x2, vec(W["ln_in_g"]), "ln_in_bwd", after=token)
    parts = {n: jnp.stack(v_) for n, v_ in small.items()}
    parts["ln_in_g"], parts["ln_in_b"] = dg_in.reshape(D), db_in0.reshape(D)
    return loss_part, dx, parts
```

```python
import numpy as np
import jax
import jax.numpy as jnp
from jax import lax
from jax.experimental import pallas as pl
from jax.experimental.pallas import tpu as pltpu

F32 = jnp.float32
BF16 = jnp.bfloat16
I32 = jnp.int32

D = 1024
DEPTH = 4
GRID_W = 64
N_HEADS = 8
HEAD_DIM = 64
D_ATTN = 512
KH = 8
KW = 16
POOL_WINDOWS = (2, 4, 8, 16)
D_POOL = 512
PGD = 128
D_FF = 2816
PLE_DIM = 256
N_PROJ = 4096
ALPHA = (2 * DEPTH) ** 0.25
LN_EPS = 1e-5
NEG_INF = -1e30
ATT_SCALE = HEAD_DIM ** -0.5
ADAM_LR = 0.001
ADAM_B1 = 0.9
ADAM_B2 = 0.999
ADAM_EPS = 1e-08
ADAM_WD = 0.01
ADAM_STEP = 10

N_DEV = 8
AXES = ("x", "y", "c")
FF_BLK = D_FF // 4
FF_SHARD = D_FF // N_DEV
QROWS = 8
KROWS = 16
QB = QROWS * GRID_W
KB = KROWS * GRID_W
V7X_VMEM_LIMIT = 56 * 2 ** 20
MESH = pl.DeviceIdType.MESH
ANY = pl.BlockSpec(memory_space=pl.ANY)


def _cparams(n_grid):
    return pltpu.CompilerParams(dimension_semantics=("arbitrary",) * n_grid, vmem_limit_bytes=V7X_VMEM_LIMIT)


def _nn(a, b):
    return lax.dot_general(a, b, (((1,), (0,)), ((), ())), preferred_element_type=F32)


def _nt(a, b):
    return lax.dot_general(a, b, (((1,), (1,)), ((), ())), preferred_element_type=F32)


def _tn(a, b):
    return lax.dot_general(a, b, (((0,), (0,)), ((), ())), preferred_element_type=F32)


def _sigmoid(x):
    return 1.0 / (1.0 + jnp.exp(-x))


def _ln_fwd(z, g, b):
    mu = jnp.mean(z, axis=-1, keepdims=True)
    xc = z - mu
    var = jnp.mean(xc * xc, axis=-1, keepdims=True)
    return xc * lax.rsqrt(var + LN_EPS) * g + b


def _ln_bwd(dh, z, g):
    mu = jnp.mean(z, axis=-1, keepdims=True)
    xc = z - mu
    var = jnp.mean(xc * xc, axis=-1, keepdims=True)
    rstd = lax.rsqrt(var + LN_EPS)
    xhat = xc * rstd
    dxh = dh * g
    m1 = jnp.mean(dxh, axis=-1, keepdims=True)
    m2 = jnp.mean(dxh * xhat, axis=-1, keepdims=True)
    return rstd * (dxh - m1 - xhat * m2), dh * xhat


def _colsum(x):
    return jnp.sum(x, axis=0, keepdims=True)


def _lane_cat(ref):
    return jnp.concatenate([ref[j] for j in range(ref.shape[0])], axis=1)


def _row_cat(ref):
    n, r, c = ref.shape
    return ref[...].reshape(n * r, c)


def _shards(n, r, c, li, j_of=None):
    del li
    if j_of is None:
        return pl.BlockSpec((n, r, c), lambda *_: (0, 0, 0))
    return pl.BlockSpec((n, r, c), lambda *g: (j_of(*g), 0, 0))


def _shard(r, c, li, j_of):
    del li
    return pl.BlockSpec((None, r, c), lambda *g: (j_of(*g), 0, 0))


def ln_fwd(x, g, b, name, after=()):
    s = x.shape[0]
    tm = 512
    na = len(after)

    def body(x_ref, g_ref, b_ref, *rest):
        h_ref, hb_ref = rest[na:]
        h = _ln_fwd(x_ref[...], g_ref[...], b_ref[...])
        h_ref[...] = h
        hb_ref[...] = h.astype(BF16)

    row = pl.BlockSpec((tm, D), lambda i: (i, 0))
    vec = pl.BlockSpec((1, D), lambda i: (0, 0))
    return pl.pallas_call(
        body, name=name, grid=(s // tm,), in_specs=[row, vec, vec] + [ANY] * na, out_specs=[row, row],
        out_shape=[jax.ShapeDtypeStruct((s, D), F32), jax.ShapeDtypeStruct((s, D), BF16)],
        compiler_params=_cparams(1))(x, g, b, *after)


def proj_fwd(hb, win, bias, li, name, after=()):
    s = hb.shape[0]
    bn = N_PROJ // N_DEV
    tm = s
    pool_shard = (3 * D_ATTN) // bn

    def body(a_ref, w_ref, b_ref, *rest):
        o_ref, u_ref = rest[-2:]
        acc = _nn(a_ref[...], w_ref[...]) + b_ref[...]
        o_ref[...] = acc.astype(BF16)

        @pl.when(pl.program_id(1) == pool_shard)
        def _():
            u_ref[...] = acc

    return pl.pallas_call(
        body, name=name, grid=(s // tm, N_DEV),
        in_specs=[pl.BlockSpec((tm, D), lambda i, j: (i, 0)),
                  _shard(D, bn, li, lambda i, j: j),
                  pl.BlockSpec((1, bn), lambda i, j: (0, j))] + [ANY] * len(after),
        out_specs=[pl.BlockSpec((tm, bn), lambda i, j: (i, j)), pl.BlockSpec((tm, bn), lambda i, j: (i, 0))],
        out_shape=[jax.ShapeDtypeStruct((s, N_PROJ), BF16), jax.ShapeDtypeStruct((s, D_POOL), F32)],
        compiler_params=_cparams(2))(hb, win, bias, *after)


def _attn_types(b, nb):
    first, last = 0, (nb * QROWS - KROWS) * GRID_W
    mid = pl.multiple_of((QROWS * b - KH // 2) * GRID_W, 256)
    return ((b == 0, first), ((b > 0) & (b < nb - 1), mid), (b == nb - 1, last))


def _attn_row(btype, qr):
    lo, delta = ((max(qr - KH // 2, 0), 0), (qr, -(KH // 2)), (min(qr + KH // 2, KH), -KH))[btype]
    return lo, (qr - delta - (KH - 1)) % KROWS, lo - qr + delta + KH - 1


def _row_window(lo):
    pad = (lo % 2) * GRID_W
    return (lo // 2) * 128, KH * GRID_W + 2 * pad, pad


def _lanes(ref, start, width):
    start %= KB
    if start + width <= KB:
        return ref[:, start:start + width]
    return jnp.concatenate([ref[:, start:], ref[:, :start + width - KB]], axis=1)


HALF = QROWS // 2


def _rows_window(btype, qr0, n):
    spans = [_row_window(_attn_row(btype, qr)[0]) for qr in range(qr0, qr0 + n)]
    h0 = min(a0 for a0, _, _ in spans) // 256 * 256
    h1 = -(-max(a0 + w for a0, w, _ in spans) // 256) * 256
    return h0, h1 - h0


def _token_at(k0, h0):
    return k0 + h0 if isinstance(k0, int) else pl.multiple_of(k0 + h0, 256)


def _row_logits(s_ref, e_ref, hh, rows, btype, qr, h0):
    lo, shift, _ = _attn_row(btype, qr)
    a0, w, pad = _row_window(lo)
    e = e_ref.at[hh, shift % 2]
    sb = s_ref[rows, a0 - h0:a0 - h0 + w] + _lanes(e, a0 - (shift - shift % 2) * GRID_W, w)
    if pad:
        lane = lax.broadcasted_iota(I32, (1, w), 1)
        sb = jnp.where((lane >= pad) & (lane < w - pad), sb, NEG_INF)
    return sb, a0 - h0, w, pad


def _store_row(ref, rows, a0, w, val, width):
    if a0:
        ref[rows, 0:a0] = jnp.zeros((GRID_W, a0), ref.dtype)
    ref[rows, a0:a0 + w] = val.astype(ref.dtype)
    if a0 + w < width:
        ref[rows, a0 + w:width] = jnp.zeros((GRID_W, width - a0 - w), ref.dtype)


def attn_fwd(proj, e_tab, li, name):
    s = proj.shape[0]
    nb = s // QB

    def body(q_ref, k_ref, v_ref, e_ref, o_ref, s_ref, p_ref):
        q = q_ref[...] * ATT_SCALE
        lane = lax.broadcasted_iota(I32, (1, 128), 1)

        def block(btype, k0):
            for half in range(2):
                h0, hw = _rows_window(btype, half * HALF, HALF)
                hrows = slice(half * HALF * GRID_W, (half + 1) * HALF * GRID_W)
                kwin = k_ref[pl.ds(_token_at(k0, h0), hw), :]
                vwin = v_ref[pl.ds(_token_at(k0, h0), hw), :]
                acc = jnp.zeros((HALF * GRID_W, 128), F32)
                for hh in range(2):
                    lm = (lane // HEAD_DIM) == hh
                    qh = jnp.where(lm, q[hrows], jnp.zeros_like(q[hrows]))
                    vh = jnp.where(lm, vwin, jnp.zeros_like(vwin))
                    s_ref[:, 0:hw] = _nt(qh, kwin)
                    for r in range(HALF):
                        rows = slice(r * GRID_W, (r + 1) * GRID_W)
                        sb, a0, w, _ = _row_logits(s_ref, e_ref, hh, rows, btype, half * HALF + r, h0)
                        p = jnp.exp(sb - jnp.max(sb, axis=1, keepdims=True))
                        _store_row(p_ref, rows, a0, w, p * (1.0 / jnp.sum(p, axis=1, keepdims=True)), hw)
                    acc = acc + _nn(p_ref[:, 0:hw], vh)
                o_ref[hrows, :] = acc.astype(BF16)

        for btype, (cond, k0) in enumerate(_attn_types(pl.program_id(1), nb)):
            pl.when(cond)(lambda btype=btype, k0=k0: block(btype, k0))

    half_shape = (HALF * GRID_W, (HALF + KH - 1 + 1) * GRID_W)
    return pl.pallas_call(
        body, name=name, grid=(4, nb),
        in_specs=[pl.BlockSpec((QB, 128), lambda j, b: (b, j)),
                  pl.BlockSpec((s, 128), lambda j, b: (0, 4 + j)),
                  pl.BlockSpec((s, 128), lambda j, b: (0, 8 + j)),
                  pl.BlockSpec((None, 2, 2, GRID_W, KB), lambda j, b: (li, j, 0, 0, 0))],
        out_specs=pl.BlockSpec((QB, 128), lambda j, b: (b, j)),
        out_shape=jax.ShapeDtypeStruct((s, D_ATTN), BF16),
        scratch_shapes=[pltpu.VMEM(half_shape, F32), pltpu.VMEM(half_shape, BF16)],
        compiler_params=_cparams(2))(proj, proj, proj, e_tab)


_POOL_PAD = 8


def _pool_counts(s, w):
    t = lax.broadcasted_iota(I32, (s, 1), 0)
    return (jnp.minimum(t + w // 2, s) - jnp.maximum(t - w // 2, 0)).astype(F32)


def _window_sum(x, w, back_first):
    s = x.shape[0]
    z = jnp.zeros((_POOL_PAD, x.shape[1]), F32)
    xe = jnp.concatenate([z, x, z], axis=0)
    n = s + 2 * _POOL_PAD
    acc = xe + pltpu.roll(xe, 1 if back_first else n - 1, 0)
    k = 1
    while 2 * k < w:
        acc = pltpu.roll(acc, k, 0) + pltpu.roll(acc, n - k, 0)
        k *= 2
    return acc[_POOL_PAD:_POOL_PAD + s, :]


def pool_fwd(u, pool_w, pool_scale, name):
    s = u.shape[0]

    def body(u_ref, w_ref, sc_ref, pm_ref, pw_ref):
        for g, w in enumerate(POOL_WINDOWS):
            cols = slice(g * PGD, (g + 1) * PGD)
            ug = u_ref[:, cols]
            pm = (_window_sum(ug, w, True) / _pool_counts(s, w) - ug).astype(BF16)
            pm_ref[:, cols] = pm
            pw_ref[:, cols] = (_nn(pm, w_ref[g]) * sc_ref[:, cols]).astype(BF16)

    full = lambda shape: pl.BlockSpec(shape, lambda i: (0,) * len(shape))
    return pl.pallas_call(
        body, name=name, grid=(1,),
        in_specs=[full((s, D_POOL)), full((4, PGD, PGD)), full((1, D_POOL))],
        out_specs=[full((s, D_POOL)), full((s, D_POOL))],
        out_shape=[jax.ShapeDtypeStruct((s, D_POOL), BF16)] * 2,
        compiler_params=_cparams(1))(u, pool_w, pool_scale)


def merge_fwd(a, pw, wao, wpo, proj, li, name, after=()):
    s = a.shape[0]
    tm, tn = 1024, 512
    nt = D // tn
    per = tn // 128

    def body(a_ref, pw_ref, wa_ref, wp_ref, ga_ref, gb_ref, *rest):
        mg_ref, ya_ref, yp_ref = rest[len(after):]
        ya = _nn(a_ref[...], _lane_cat(wa_ref))
        yp = _nn(pw_ref[...], _lane_cat(wp_ref))
        mg = _sigmoid(ga_ref[...].astype(F32)) * ya + _sigmoid(gb_ref[...].astype(F32)) * yp
        mg_ref[...] = mg.astype(BF16)
        ya_ref[...] = ya.astype(BF16)
        yp_ref[...] = yp.astype(BF16)

    act = pl.BlockSpec((tm, D_ATTN), lambda i, j: (i, 0))
    wsp = _shards(per, D_ATTN, 128, li, lambda i, j: j)
    out = pl.BlockSpec((tm, tn), lambda i, j: (i, j))
    ga0 = (3 * D_ATTN + D_POOL) // tn
    return pl.pallas_call(
        body, name=name, grid=(s // tm, nt),
        in_specs=[act, act, wsp, wsp,
                  pl.BlockSpec((tm, tn), lambda i, j: (i, ga0 + j)),
                  pl.BlockSpec((tm, tn), lambda i, j: (i, ga0 + nt + j))] + [ANY] * len(after),
        out_specs=[out, out, out],
        out_shape=[jax.ShapeDtypeStruct((s, D), BF16)] * 3,
        compiler_params=_cparams(2))(a, pw, wao, wpo, proj, proj, *after)


def mix_ln_fwd(mg, wmix, h0, g, b, li, name):
    s = mg.shape[0]
    tm = 512

    def body(mg_ref, w_ref, h0_ref, g_ref, b_ref, z_ref, h_ref, hb_ref):
        z = ALPHA * h0_ref[...] + _nn(mg_ref[...], _row_cat(w_ref))
        h = _ln_fwd(z, g_ref[...], b_ref[...])
        z_ref[...] = z
        h_ref[...] = h
        hb_ref[...] = h.astype(BF16)

    row = pl.BlockSpec((tm, D), lambda i: (i, 0))
    vec = pl.BlockSpec((1, D), lambda i: (0, 0))
    return pl.pallas_call(
        body, name=name, grid=(s // tm,),
        in_specs=[row, _shards(N_DEV, D // N_DEV, D, li), row, vec, vec],
        out_specs=[row, row, row],
        out_shape=[jax.ShapeDtypeStruct((s, D), F32), jax.ShapeDtypeStruct((s, D), F32),
                   jax.ShapeDtypeStruct((s, D), BF16)],
        compiler_params=_cparams(1))(mg, wmix, h0, g, b)


def up_fwd(hb, wup, li, name):
    s = hb.shape[0]
    tm = s

    def body(a_ref, w_ref, o_ref):
        o_ref[...] = _nt(a_ref[...], w_ref[...]).astype(BF16)

    return pl.pallas_call(
        body, name=name, grid=(s // tm, N_DEV),
        in_specs=[pl.BlockSpec((tm, D), lambda i, j: (i, 0)), _shard(FF_BLK, D, li, lambda i, j: j)],
        out_specs=pl.BlockSpec((None, tm, FF_BLK), lambda i, j: (j, i, 0)),
        out_shape=jax.ShapeDtypeStruct((N_DEV, s, FF_BLK), BF16),
        compiler_params=_cparams(2))(hb, wup)


_SQRT_HALF = 0.7071067811865476
_INV_SQRT_2PI = 0.3989422804014327


def _shift_rows(x, prev_row, next_row):
    n = x.shape[0]
    r = lax.broadcasted_iota(I32, (n, 1), 0)
    back = jnp.where(r == 0, prev_row, pltpu.roll(x, 1, 0))
    fwd = jnp.where(r == n - 1, next_row, pltpu.roll(x, n - 1, 0))
    return back, fwd


HALO = 16


def _halo_maps(tm, s):
    th = tm // HALO
    return (lambda i: jnp.maximum(i * th - 1, 0)), (lambda i: jnp.minimum((i + 1) * th, s // HALO - 1))


def _slab_specs(tm, s, blk_of):
    before, after = _halo_maps(tm, s)
    main = pl.BlockSpec((None, tm, FF_BLK), lambda c, i: (blk_of(c), i, 0))
    prev = pl.BlockSpec((None, HALO, FF_BLK), lambda c, i: (blk_of(c), before(i), 0))
    nxt = pl.BlockSpec((None, HALO, FF_BLK), lambda c, i: (blk_of(c), after(i), 0))
    return main, prev, nxt


def ffn_act_fwd(up, conv_w, conv_b, name):
    s = up.shape[1]
    tm = 1024
    nt = s // tm
    hv_main, _, _ = _slab_specs(tm, s, lambda c: c)
    hg_main, hg_prev, hg_next = _slab_specs(tm, s, lambda c: 4 + c)

    def body(hv_ref, hg_ref, hp_ref, hn_ref, cw_ref, cb_ref, t_ref):
        i = pl.program_id(1)
        hg = hg_ref[...].astype(F32)
        prow = jnp.where(i == 0, 0.0, hp_ref[...].astype(F32)[HALO - 1:HALO, :])
        nrow = jnp.where(i == nt - 1, 0.0, hn_ref[...].astype(F32)[0:1, :])
        back, fwd = _shift_rows(hg, prow, nrow)
        c = back * cw_ref[0:1, :] + hg * cw_ref[1:2, :] + fwd * cw_ref[2:3, :] + cb_ref[...]
        act = 0.5 * c * (1.0 + lax.erf(c * _SQRT_HALF))
        t_ref[...] = (act * hv_ref[...].astype(F32)).astype(BF16)

    return pl.pallas_call(
        body, name=name, grid=(4, nt),
        in_specs=[hv_main, hg_main, hg_prev, hg_next,
                  pl.BlockSpec((None, 3, FF_BLK), lambda c, i: (c, 0, 0)),
                  pl.BlockSpec((None, 1, FF_BLK), lambda c, i: (c, 0, 0))],
        out_specs=pl.BlockSpec((None, tm, FF_BLK), lambda c, i: (c, i, 0)),
        out_shape=jax.ShapeDtypeStruct((4, s, FF_BLK), BF16),
        compiler_params=_cparams(2))(up, up, up, up, conv_w, conv_b)


def down_ple_ln_fwd(t, wdown, hb, wpg, pb, wpp, h1, g, b, li, name):
    s = hb.shape[0]
    tm = 256

    def body(t_ref, wd_ref, hb_ref, wpg_ref, p_ref, wpp_ref, h1_ref, g_ref, b_ref,
             z_ref, h_ref, hbo_ref, pg_ref, pp_ref):
        wd = _row_cat(wd_ref)
        ffn = _nn(t_ref[0], wd[0:FF_BLK, :])
        for c in range(1, 4):
            ffn = ffn + _nn(t_ref[c], wd[c * FF_BLK:(c + 1) * FF_BLK, :])
        pg = _nn(hb_ref[...], _row_cat(wpg_ref))
        pp = _nn(p_ref[...], _lane_cat(wpp_ref))
        z = ALPHA * h1_ref[...] + ffn + _sigmoid(pg) * pp
        h = _ln_fwd(z, g_ref[...], b_ref[...])
        z_ref[...] = z
        h_ref[...] = h
        hbo_ref[...] = h.astype(BF16)
        pg_ref[...] = pg.astype(BF16)
        pp_ref[...] = pp.astype(BF16)

    row = pl.BlockSpec((tm, D), lambda i: (i, 0))
    vec = pl.BlockSpec((1, D), lambda i: (0, 0))
    return pl.pallas_call(
        body, name=name, grid=(s // tm,),
        in_specs=[pl.BlockSpec((4, tm, FF_BLK), lambda i: (0, i, 0)),
                  _shards(N_DEV, FF_SHARD, D, li),
                  row, _shards(N_DEV, D // N_DEV, D, li),
                  pl.BlockSpec((tm, PLE_DIM), lambda i: (i, 0)),
                  _shards(N_DEV, PLE_DIM, 128, li),
                  row, vec, vec],
        out_specs=[row] * 5,
        out_shape=[jax.ShapeDtypeStruct((s, D), F32), jax.ShapeDtypeStruct((s, D), F32),
                   jax.ShapeDtypeStruct((s, D), BF16), jax.ShapeDtypeStruct((s, D), BF16),
                   jax.ShapeDtypeStruct((s, D), BF16)],
        compiler_params=_cparams(1))(t, wdown, hb, wpg, pb, wpp, h1, g, b)


def loss_bwd(h, target, name):
    s = h.shape[0]
    tm = 512

    def body(h_ref, t_ref, dh_ref, l_ref):
        @pl.when(pl.program_id(0) == 0)
        def _():
            l_ref[...] = jnp.zeros_like(l_ref)
        e = h_ref[...] - t_ref[...]
        dh_ref[...] = e * (1.0 / D)
        l_ref[...] += 0.5 * jnp.sum(jnp.mean(e * e, axis=-1, keepdims=True), axis=0, keepdims=True)

    row = pl.BlockSpec((tm, D), lambda i: (i, 0))
    return pl.pallas_call(
        body, name=name, grid=(s // tm,), in_specs=[row, row],
        out_specs=[row, pl.BlockSpec((1, 1), lambda i: (0, 0))],
        out_shape=[jax.ShapeDtypeStruct((s, D), F32), jax.ShapeDtypeStruct((1, 1), F32)],
        compiler_params=_cparams(1))(h, target)


def ln_bwd(dh, z, g, name, after=()):
    s = dh.shape[0]
    tm = 512
    na = len(after)

    def body(dh_ref, z_ref, g_ref, *rest):
        dz_ref, dg_ref, db_ref = rest[na:]

        @pl.when(pl.program_id(0) == 0)
        def _():
            dg_ref[...] = jnp.zeros_like(dg_ref)
            db_ref[...] = jnp.zeros_like(db_ref)
        dh = dh_ref[...]
        dz, dgx = _ln_bwd(dh, z_ref[...], g_ref[...])
        dz_ref[...] = dz
        dg_ref[...] += _colsum(dgx)
        db_ref[...] += _colsum(dh)

    row = pl.BlockSpec((tm, D), lambda i: (i, 0))
    vec = pl.BlockSpec((1, D), lambda i: (0, 0))
    return pl.pallas_call(
        body, name=name, grid=(s // tm,), in_specs=[row, row, vec] + [ANY] * na, out_specs=[row, vec, vec],
        out_shape=[jax.ShapeDtypeStruct((s, D), F32), jax.ShapeDtypeStruct((1, D), F32),
                   jax.ShapeDtypeStruct((1, D), F32)],
        compiler_params=_cparams(1))(dh, z, g, *after)


def ln2_ple_bwd(dh, z, g, pg, pp, name, after=()):
    s = dh.shape[0]
    tm = 512
    na = len(after)

    def body(dh_ref, z_ref, g_ref, pg_ref, pp_ref, *rest):
        dz_ref, dzb_ref, dpg_ref, dpp_ref, dg_ref, db_ref = rest[na:]

        @pl.when(pl.program_id(0) == 0)
        def _():
            dg_ref[...] = jnp.zeros_like(dg_ref)
            db_ref[...] = jnp.zeros_like(db_ref)
        dh = dh_ref[...]
        dz, dgx = _ln_bwd(dh, z_ref[...], g_ref[...])
        sg = _sigmoid(pg_ref[...].astype(F32))
        dz_ref[...] = dz
        dzb_ref[...] = dz.astype(BF16)
        dpg_ref[...] = (dz * pp_ref[...].astype(F32) * sg * (1.0 - sg)).astype(BF16)
        dpp_ref[...] = (dz * sg).astype(BF16)
        dg_ref[...] += _colsum(dgx)
        db_ref[...] += _colsum(dh)

    row = pl.BlockSpec((tm, D), lambda i: (i, 0))
    vec = pl.BlockSpec((1, D), lambda i: (0, 0))
    return pl.pallas_call(
        body, name=name, grid=(s // tm,), in_specs=[row, row, vec, row, row] + [ANY] * na,
        out_specs=[row, row, row, row, vec, vec],
        out_shape=[jax.ShapeDtypeStruct((s, D), F32)] + [jax.ShapeDtypeStruct((s, D), BF16)] * 3
        + [jax.ShapeDtypeStruct((1, D), F32)] * 2,
        compiler_params=_cparams(1))(dh, z, g, pg, pp, *after)


def wgrad_pair(rows_a, rows_dy, cols, name):
    specs, args, outs, shapes, kinds = [], [], [], [], []
    if rows_a is not None:
        s, k = rows_a.shape
        n = rows_dy.shape[1]
        specs += [pl.BlockSpec((s, k // N_DEV), lambda j: (0, j)), pl.BlockSpec((s, n), lambda j: (0, 0))]
        args += [rows_a, rows_dy]
        outs.append(pl.BlockSpec((None, k // N_DEV, n), lambda j: (j, 0, 0)))
        shapes.append(jax.ShapeDtypeStruct((N_DEV, k // N_DEV, n), BF16))
    for a, dy in cols:
        s, k = a.shape
        n = dy.shape[1]
        specs += [pl.BlockSpec((s, k), lambda j: (0, 0)), pl.BlockSpec((s, n // N_DEV), lambda j: (0, j))]
        args += [a, dy]
        outs.append(pl.BlockSpec((None, k, n // N_DEV), lambda j: (j, 0, 0)))
        shapes.append(jax.ShapeDtypeStruct((N_DEV, k, n // N_DEV), BF16))
    n_pairs = len(shapes)

    def body(*refs):
        for i in range(n_pairs):
            refs[2 * n_pairs + i][...] = _tn(refs[2 * i][...], refs[2 * i + 1][...]).astype(BF16)

    return pl.pallas_call(body, name=name, grid=(N_DEV,), in_specs=specs, out_specs=outs, out_shape=shapes,
                          compiler_params=_cparams(1))(*args)


def wgrad_in(a, pieces, name):
    s, k = a.shape
    bn = N_PROJ // N_DEV
    n_narrow = 4

    def body(a_ref, *refs):
        dy_refs, (o_ref, cs_ref) = refs[:6], refs[6:]
        j = pl.program_id(0)

        def emit(dy_ref):
            dy = dy_ref[...]
            o_ref[...] = _tn(a_ref[...], dy).astype(BF16)
            cs_ref[...] = _colsum(dy.astype(F32))

        for idx in range(n_narrow):
            pl.when(j == idx)(lambda idx=idx: emit(dy_refs[idx]))
        pl.when((j >= n_narrow) & (j < n_narrow + 2))(lambda: emit(dy_refs[4]))
        pl.when(j >= n_narrow + 2)(lambda: emit(dy_refs[5]))

    narrow = pl.BlockSpec((s, bn), lambda j: (0, 0))
    return pl.pallas_call(
        body, name=name, grid=(N_DEV,),
        in_specs=[pl.BlockSpec((s, k), lambda j: (0, 0))] + [narrow] * n_narrow
        + [pl.BlockSpec((s, bn), lambda j: (0, jnp.clip(j - n_narrow, 0, 1))),
           pl.BlockSpec((s, bn), lambda j: (0, jnp.clip(j - n_narrow - 2, 0, 1)))],
        out_specs=[pl.BlockSpec((None, k, bn), lambda j: (j, 0, 0)), pl.BlockSpec((1, bn), lambda j: (0, j))],
        out_shape=[jax.ShapeDtypeStruct((N_DEV, k, bn), BF16), jax.ShapeDtypeStruct((1, N_PROJ), F32)],
        compiler_params=_cparams(1))(a, *pieces)


def wgrad_down(t, dy, name):
    _, s, k = t.shape
    n = dy.shape[1]

    def body(a_ref, dy_ref, o_ref):
        o_ref[...] = _tn(a_ref[...], dy_ref[...]).astype(BF16)

    return pl.pallas_call(
        body, name=name, grid=(4,),
        in_specs=[pl.BlockSpec((None, s, k), lambda j: (j, 0, 0)), pl.BlockSpec((s, n), lambda j: (0, 0))],
        out_specs=pl.BlockSpec((None, k, n), lambda j: (j, 0, 0)),
        out_shape=jax.ShapeDtypeStruct((4, k, n), BF16),
        compiler_params=_cparams(1))(t, dy)


def wgrad_up(a, dhv, dhg, name):
    s, k = a.shape

    def body(a_ref, dv_ref, dg_ref, o_ref):
        j = pl.program_id(0)

        @pl.when(j < 4)
        def _():
            o_ref[...] = _tn(dv_ref[...], a_ref[...]).astype(BF16)

        @pl.when(j >= 4)
        def _():
            o_ref[...] = _tn(dg_ref[...], a_ref[...]).astype(BF16)

    return pl.pallas_call(
        body, name=name, grid=(N_DEV,),
        in_specs=[pl.BlockSpec((s, k), lambda j: (0, 0)),
                  pl.BlockSpec((None, s, FF_BLK), lambda j: (jnp.minimum(j, 3), 0, 0)),
                  pl.BlockSpec((None, s, FF_BLK), lambda j: (jnp.maximum(j - 4, 0), 0, 0))],
        out_specs=pl.BlockSpec((None, FF_BLK, k), lambda j: (j, 0, 0)),
        out_shape=jax.ShapeDtypeStruct((N_DEV, FF_BLK, k), BF16),
        compiler_params=_cparams(1))(a, dhv, dhg)


def ffn_act_bwd(dzb, wdown, up, conv_w, conv_b, li, name):
    s = up.shape[1]
    tm = 512
    nt = s // tm
    before, after = _halo_maps(tm, s)
    hv_main, hv_prev, hv_next = _slab_specs(tm, s, lambda c: c)
    hg_main, hg_prev, hg_next = _slab_specs(tm, s, lambda c: 4 + c)

    def dc_of(dz, wd, hv, hg, back, fwd, cw_ref, cb_ref):
        dt = _nt(dz, wd)
        c = back * cw_ref[0:1, :] + hg * cw_ref[1:2, :] + fwd * cw_ref[2:3, :] + cb_ref[...]
        cdf = 0.5 * (1.0 + lax.erf(c * _SQRT_HALF))
        pdf = jnp.exp(-0.5 * c * c) * _INV_SQRT_2PI
        return dt, c * cdf, dt * hv * (cdf + c * pdf)

    def body(dz_ref, dzp_ref, dzn_ref, wd_ref, hv_ref, hvp_ref, hvn_ref, hg_ref, hgp_ref, hgn_ref, cw_ref, cb_ref,
             dhv_ref, dhg_ref, dcw_ref, dcb_ref):
        i = pl.program_id(1)

        @pl.when(i == 0)
        def _():
            dcw_ref[...] = jnp.zeros_like(dcw_ref)
            dcb_ref[...] = jnp.zeros_like(dcb_ref)

        wd = _row_cat(wd_ref)
        hg = hg_ref[...].astype(F32)
        hgp = hgp_ref[...].astype(F32)
        hgn = hgn_ref[...].astype(F32)
        first, last = i == 0, i == nt - 1
        e = HALO - 1
        back, fwd = _shift_rows(hg, jnp.where(first, 0.0, hgp[e:e + 1, :]), jnp.where(last, 0.0, hgn[0:1, :]))
        dt, act, dc = dc_of(dz_ref[...], wd, hv_ref[...].astype(F32), hg, back, fwd, cw_ref, cb_ref)
        dhv_ref[...] = (dt * act).astype(BF16)
        bp, fp = _shift_rows(hgp, hgp[0:1, :], hg[0:1, :])
        _, _, dcp = dc_of(dzp_ref[...], wd, hvp_ref[...].astype(F32), hgp, bp, fp, cw_ref, cb_ref)
        bn, fn = _shift_rows(hgn, hg[tm - 1:tm, :], hgn[e:e + 1, :])
        _, _, dcn = dc_of(dzn_ref[...], wd, hvn_ref[...].astype(F32), hgn, bn, fn, cw_ref, cb_ref)
        dc_back, dc_fwd = _shift_rows(dc, jnp.where(first, 0.0, dcp[e:e + 1, :]), jnp.where(last, 0.0, dcn[0:1, :]))
        dhg_ref[...] = (dc_fwd * cw_ref[0:1, :] + dc * cw_ref[1:2, :] + dc_back * cw_ref[2:3, :]).astype(BF16)
        dcw_ref[0:1, :] += _colsum(dc * back)
        dcw_ref[1:2, :] += _colsum(dc * hg)
        dcw_ref[2:3, :] += _colsum(dc * fwd)
        dcb_ref[...] += _colsum(dc)

    out_slab = pl.BlockSpec((None, tm, FF_BLK), lambda c, i: (c, i, 0))
    cw_spec = pl.BlockSpec((None, 3, FF_BLK), lambda c, i: (c, 0, 0))
    cb_spec = pl.BlockSpec((None, 1, FF_BLK), lambda c, i: (c, 0, 0))
    return pl.pallas_call(
        body, name=name, grid=(4, nt),
        in_specs=[pl.BlockSpec((tm, D), lambda c, i: (i, 0)),
                  pl.BlockSpec((HALO, D), lambda c, i: (before(i), 0)),
                  pl.BlockSpec((HALO, D), lambda c, i: (after(i), 0)),
                  _shards(2, FF_SHARD, D, li, lambda c, i: c),
                  hv_main, hv_prev, hv_next, hg_main, hg_prev, hg_next, cw_spec, cb_spec],
        out_specs=[out_slab, out_slab, cw_spec, cb_spec],
        out_shape=[jax.ShapeDtypeStruct((4, s, FF_BLK), BF16), jax.ShapeDtypeStruct((4, s, FF_BLK), BF16),
                   jax.ShapeDtypeStruct((4, 3, FF_BLK), F32), jax.ShapeDtypeStruct((4, 1, FF_BLK), F32)],
        compiler_params=_cparams(2))(dzb, dzb, dzb, wdown, up, up, up, up, up, up, conv_w, conv_b)


def dh1_ln1_bwd(dz2, dpg, wpg, dhv, dhg, wup, z1, g1, li, name, after=()):
    s = dz2.shape[0]
    tm = 256
    na = len(after)

    def body(dz2_ref, dpg_ref, wpg_ref, dhv_ref, dhg_ref, wup_ref, z1_ref, g_ref, *rest):
        dz_ref, dzb_ref, dg_ref, db_ref = rest[na:]

        @pl.when(pl.program_id(0) == 0)
        def _():
            dg_ref[...] = jnp.zeros_like(dg_ref)
            db_ref[...] = jnp.zeros_like(db_ref)
        dh = ALPHA * dz2_ref[...] + _nt(dpg_ref[...], _row_cat(wpg_ref))
        for c in range(4):
            dh = dh + _nn(dhv_ref[c], wup_ref[c]) + _nn(dhg_ref[c], wup_ref[4 + c])
        dz, dgx = _ln_bwd(dh, z1_ref[...], g_ref[...])
        dz_ref[...] = dz
        dzb_ref[...] = dz.astype(BF16)
        dg_ref[...] += _colsum(dgx)
        db_ref[...] += _colsum(dh)

    row = pl.BlockSpec((tm, D), lambda i: (i, 0))
    vec = pl.BlockSpec((1, D), lambda i: (0, 0))
    slab = pl.BlockSpec((4, tm, FF_BLK), lambda i: (0, i, 0))
    return pl.pallas_call(
        body, name=name, grid=(s // tm,),
        in_specs=[row, row, _shards(N_DEV, D // N_DEV, D, li), slab, slab, _shards(N_DEV, FF_BLK, D, li), row, vec]
        + [ANY] * na,
        out_specs=[row, row, vec, vec],
        out_shape=[jax.ShapeDtypeStruct((s, D), F32), jax.ShapeDtypeStruct((s, D), BF16),
                   jax.ShapeDtypeStruct((1, D), F32), jax.ShapeDtypeStruct((1, D), F32)],
        compiler_params=_cparams(1))(dz2, dpg, wpg, dhv, dhg, wup, z1, g1, *after)


def merge_bwd(dz1b, wmix, proj, ya, yp, li, name, after=()):
    s = dz1b.shape[0]
    tm, tn = 1024, 512
    nt = D // tn
    per = tn // (D // N_DEV)
    ga0 = (3 * D_ATTN + D_POOL) // tn

    def body(dz_ref, w_ref, ga_ref, gb_ref, ya_ref, yp_ref, *rest):
        dya_ref, dyp_ref, dga_ref, dgb_ref = rest[len(after):]
        dm = _nt(dz_ref[...], _row_cat(w_ref))
        sa = _sigmoid(ga_ref[...].astype(F32))
        sb = _sigmoid(gb_ref[...].astype(F32))
        dya_ref[...] = (dm * sa).astype(BF16)
        dyp_ref[...] = (dm * sb).astype(BF16)
        dga_ref[...] = (dm * ya_ref[...].astype(F32) * sa * (1.0 - sa)).astype(BF16)
        dgb_ref[...] = (dm * yp_ref[...].astype(F32) * sb * (1.0 - sb)).astype(BF16)

    tile = pl.BlockSpec((tm, tn), lambda i, j: (i, j))
    return pl.pallas_call(
        body, name=name, grid=(s // tm, nt),
        in_specs=[pl.BlockSpec((tm, D), lambda i, j: (i, 0)),
                  _shards(per, D // N_DEV, D, li, lambda i, j: j),
                  pl.BlockSpec((tm, tn), lambda i, j: (i, ga0 + j)),
                  pl.BlockSpec((tm, tn), lambda i, j: (i, ga0 + nt + j)),
                  tile, tile] + [ANY] * len(after),
        out_specs=[tile] * 4,
        out_shape=[jax.ShapeDtypeStruct((s, D), BF16)] * 4,
        compiler_params=_cparams(2))(dz1b, wmix, proj, proj, ya, yp, *after)


def attn_out_bwd(dya, wao, li, name, after=()):
    s = dya.shape[0]
    tm = 1024

    def body(d_ref, w_ref, *rest):
        rest[-1][...] = _nt(d_ref[...], _lane_cat(w_ref)).astype(BF16)

    return pl.pallas_call(
        body, name=name, grid=(s // tm,),
        in_specs=[pl.BlockSpec((tm, D), lambda i: (i, 0)), _shards(N_DEV, D_ATTN, 128, li)] + [ANY] * len(after),
        out_specs=pl.BlockSpec((tm, D_ATTN), lambda i: (i, 0)),
        out_shape=jax.ShapeDtypeStruct((s, D_ATTN), BF16),
        compiler_params=_cparams(1))(dya, wao, *after)


def pool_bwd(dyp, wpo, pm, pool_w, pool_scale, li, name):
    s = dyp.shape[0]

    def body(dyp_ref, wpo_ref, pm_ref, w_ref, sc_ref, du_ref, dw_ref, dsc_ref):
        wpo = _lane_cat(wpo_ref)
        dyp = dyp_ref[...]
        for g, w in enumerate(POOL_WINDOWS):
            cols = slice(g * PGD, (g + 1) * PGD)
            dpw = _nt(dyp, wpo[g * PGD:(g + 1) * PGD, :])
            pmg = pm_ref[:, cols]
            dsc_ref[:, cols] = _colsum(dpw * _nn(pmg, w_ref[g]))
            dpmw = (dpw * sc_ref[:, cols]).astype(BF16)
            dw_ref[g] = _tn(pmg, dpmw)
            dpm = _nt(dpmw, w_ref[g])
            du_ref[:, cols] = (_window_sum(dpm / _pool_counts(s, w), w, False) - dpm).astype(BF16)

    full = lambda shape: pl.BlockSpec(shape, lambda i: (0,) * len(shape))
    return pl.pallas_call(
        body, name=name, grid=(1,),
        in_specs=[full((s, D)), _shards(N_DEV, D_POOL, 128, li), full((s, D_POOL)), full((4, PGD, PGD)),
                  full((1, D_POOL))],
        out_specs=[full((s, D_POOL)), full((4, PGD, PGD)), full((1, D_POOL))],
        out_shape=[jax.ShapeDtypeStruct((s, D_POOL), BF16), jax.ShapeDtypeStruct((4, PGD, PGD), F32),
                   jax.ShapeDtypeStruct((1, D_POOL), F32)],
        compiler_params=_cparams(1))(dyp, wpo, pm, pool_w, pool_scale)


def attn_bwd(proj, da, e_rev, li, name, after=()):
    s = proj.shape[0]
    nb = s // QB
    skew = GRID_W + (GRID_W - KW)
    group = QROWS
    half_shape = (group * GRID_W, min(group + KH, KROWS) * GRID_W)

    def body(q_ref, k_ref, v_ref, do_ref, e_ref, *rest):
        dq_ref, dk_ref, dv_ref, g_ref, s_ref, dp_ref, ds_ref, p_ref, dkt_acc, dvt_acc = rest[len(after):]
        b = pl.program_id(1)

        @pl.when(b == 0)
        def _():
            dkt_acc[...] = jnp.zeros_like(dkt_acc)
            dvt_acc[...] = jnp.zeros_like(dvt_acc)
            g_ref[...] = jnp.zeros_like(g_ref)

        ri = lax.broadcasted_iota(I32, (QB, QB), 0)
        ci = lax.broadcasted_iota(I32, (QB, QB), 1)
        rev = jnp.where(ri + ci == QB - 1, 1.0, 0.0).astype(BF16)
        q = _nn(rev, q_ref[...]).astype(BF16) * ATT_SCALE
        do = _nn(rev, do_ref[...]).astype(BF16)
        lane = lax.broadcasted_iota(I32, (1, 128), 1)

        def block(btype, k0):
            dqs = []
            for g0 in range(0, QROWS, group):
                h0, hw = _rows_window(btype, QROWS - g0 - group, group)
                hrows = slice(g0 * GRID_W, (g0 + group) * GRID_W)
                kwin = k_ref[pl.ds(_token_at(k0, h0), hw), :]
                vwin = v_ref[pl.ds(_token_at(k0, h0), hw), :]
                dq = jnp.zeros((group * GRID_W, 128), F32)
                for hh in range(2):
                    lm = (lane // HEAD_DIM) == hh
                    qh = jnp.where(lm, q[hrows], jnp.zeros_like(q[hrows]))
                    doh = jnp.where(lm, do[hrows], jnp.zeros_like(do[hrows]))
                    kh = jnp.where(lm, kwin, jnp.zeros_like(kwin))
                    s_ref[:, 0:hw] = _nt(qh, kwin)
                    dp_ref[:, 0:hw] = _nt(doh, vwin)
                    g = jnp.zeros((1, KB), F32)
                    for r in range(group):
                        qr = QROWS - 1 - (g0 + r)
                        rows = slice(r * GRID_W, (r + 1) * GRID_W)
                        sb, a0, w, pad = _row_logits(s_ref, e_ref, hh, rows, btype, qr, h0)
                        p = jnp.exp(sb - jnp.max(sb, axis=1, keepdims=True))
                        p = p * (1.0 / jnp.sum(p, axis=1, keepdims=True))
                        dp = dp_ref[rows, a0:a0 + w]
                        ds = p * (dp - jnp.sum(p * dp, axis=1, keepdims=True))
                        _store_row(ds_ref, rows, a0, w, ds, hw)
                        _store_row(p_ref, rows, a0, w, p, hw)
                        t = jnp.sum(pltpu.roll(ds, w - skew, 1, stride=1, stride_axis=0), axis=0, keepdims=True)
                        t = t[:, :KH * GRID_W] if pad else pltpu.roll(t, GRID_W, 1)
                        i0 = _attn_row(btype, qr)[2]
                        g = g + pltpu.roll(jnp.concatenate([t, jnp.zeros_like(t)], axis=1), i0 * GRID_W, 1)
                    g_ref[hh] += g
                    dsb = ds_ref[:, 0:hw]
                    dq = dq + _nn(dsb, kh) * ATT_SCALE
                    dkt_acc[:, pl.ds(_token_at(k0, h0), hw)] += _tn(qh, dsb)
                    dvt_acc[:, pl.ds(_token_at(k0, h0), hw)] += _tn(doh, p_ref[:, 0:hw])
                dqs.append(dq.astype(BF16))
            dq_ref[...] = _nn(rev, jnp.concatenate(dqs, axis=0)).astype(BF16)

        for btype, (cond, k0) in enumerate(_attn_types(b, nb)):
            pl.when(cond)(lambda btype=btype, k0=k0: block(btype, k0))

        @pl.when(b == nb - 1)
        def _():
            dk_ref[...] = dkt_acc[...].T.astype(BF16)
            dv_ref[...] = dvt_acc[...].T.astype(BF16)

    col = pl.BlockSpec((s, 128), lambda j, b: (0, j))
    return pl.pallas_call(
        body, name=name, grid=(4, nb),
        in_specs=[pl.BlockSpec((QB, 128), lambda j, b: (b, j)),
                  pl.BlockSpec((s, 128), lambda j, b: (0, 4 + j)),
                  pl.BlockSpec((s, 128), lambda j, b: (0, 8 + j)),
                  pl.BlockSpec((QB, 128), lambda j, b: (b, j)),
                  pl.BlockSpec((None, 2, 2, GRID_W, KB), lambda j, b: (li, j, 0, 0, 0))] + [ANY] * len(after),
        out_specs=[pl.BlockSpec((QB, 128), lambda j, b: (b, j)), col, col,
                   pl.BlockSpec((2, 1, KB), lambda j, b: (j, 0, 0))],
        out_shape=[jax.ShapeDtypeStruct((s, D_ATTN), BF16)] * 3 + [jax.ShapeDtypeStruct((N_HEADS, 1, KB), F32)],
        scratch_shapes=[pltpu.VMEM(half_shape, F32), pltpu.VMEM(half_shape, F32), pltpu.VMEM(half_shape, BF16),
                        pltpu.VMEM(half_shape, BF16), pltpu.VMEM((128, s), F32), pltpu.VMEM((128, s), F32)],
        compiler_params=_cparams(2))(proj, proj, proj, da, e_rev, *after)


def dh0_bwd(dz1, pieces, win, li, name, after=()):
    s = dz1.shape[0]
    tm = 512
    bn = N_PROJ // N_DEV

    def body(dz_ref, q_ref, k_ref, v_ref, u_ref, ga_ref, gb_ref, w_ref, *rest):
        acc = ALPHA * dz_ref[...]
        for j, ref in enumerate((q_ref, k_ref, v_ref, u_ref)):
            acc = acc + _nt(ref[...], w_ref[j])
        for j, ref in ((4, ga_ref), (6, gb_ref)):
            acc = acc + _nt(ref[:, 0:bn], w_ref[j]) + _nt(ref[:, bn:2 * bn], w_ref[j + 1])
        rest[-1][...] = acc

    row = pl.BlockSpec((tm, D), lambda i: (i, 0))
    narrow = pl.BlockSpec((tm, bn), lambda i: (i, 0))
    return pl.pallas_call(
        body, name=name, grid=(s // tm,),
        in_specs=[row] + [narrow] * 4 + [row, row, _shards(N_DEV, D, bn, li)] + [ANY] * len(after),
        out_specs=row, out_shape=jax.ShapeDtypeStruct((s, D), F32),
        compiler_params=_cparams(1))(dz1, *pieces, win, *after)


def _coords():
    return lax.axis_index("x"), lax.axis_index("y"), lax.axis_index("c")


def _dev_index(px, py, pc):
    return 4 * px + 2 * py + pc


HBM = pl.BlockSpec(memory_space=pltpu.HBM)
SEM = pl.BlockSpec(memory_space=pltpu.SEMAPHORE)
_EFFECT = pltpu.SideEffectType.DATAFLOW_SIDE_EFFECTING
_TOKEN = jax.ShapeDtypeStruct((8, 128), F32)


def _in_hbm(a):
    return pltpu.with_memory_space_constraint(a, pltpu.HBM)


def _hbm_like(a):
    return pltpu.HBM(a.shape, a.dtype)


def _peers(x, y, c):
    return [(x, y, 1 - c), (1 - x, y, c), (x, 1 - y, c), (1 - x, 1 - y, c)]


def ag_start(lands, after, name):
    n = len(lands)

    def body(*refs):
        land = refs[:n]
        send_sem, recv_sem, token = refs[n + 1], refs[n + 2], refs[-1]
        x, y, c = _coords()
        me = _dev_index(x, y, c)
        for k, peer in enumerate(_peers(x, y, c)):
            for a in range(n):
                pltpu.make_async_remote_copy(src_ref=land[a].at[me], dst_ref=land[a].at[me], send_sem=send_sem.at[k],
                                             recv_sem=recv_sem.at[k], device_id=peer, device_id_type=MESH).start()
        token[...] = jnp.zeros_like(token)

    res = pl.pallas_call(
        body, name=name,
        out_shape=(pltpu.SemaphoreType.DMA((4,)), pltpu.SemaphoreType.DMA((4,)), *[_hbm_like(l) for l in lands], _TOKEN),
        in_specs=[HBM] * n + [ANY], out_specs=(SEM, SEM, *[HBM] * n, pl.BlockSpec(memory_space=pltpu.VMEM)),
        input_output_aliases={a: 2 + a for a in range(n)},
        compiler_params=pltpu.CompilerParams(has_side_effects=_EFFECT),
    )(*[_in_hbm(l) for l in lands], after)
    return res[0], res[1], list(res[2:2 + n]), res[-1]


def ag_forward(send_sem, recv_sem, lands, after, name):
    n = len(lands)

    def body(*refs):
        send_sem, recv_sem = refs[0], refs[1]
        land = refs[2:2 + n]
        fsend, frecv = refs[3 + n], refs[4 + n]
        x, y, c = _coords()
        peers = _peers(x, y, c)
        for k in range(1, 4):
            blk = _dev_index(*peers[k])
            for a in range(n):
                pltpu.make_async_remote_copy(src_ref=land[a].at[blk], dst_ref=land[a].at[blk], send_sem=send_sem.at[k],
                                             recv_sem=recv_sem.at[k], device_id=peers[k], device_id_type=MESH).wait_recv()
        for k in range(1, 4):
            blk = _dev_index(*peers[k])
            for a in range(n):
                pltpu.make_async_remote_copy(src_ref=land[a].at[blk], dst_ref=land[a].at[blk], send_sem=fsend.at[k - 1],
                                             recv_sem=frecv.at[k - 1], device_id=peers[0], device_id_type=MESH).start()

    res = pl.pallas_call(
        body, name=name,
        out_shape=(pltpu.SemaphoreType.DMA((3,)), pltpu.SemaphoreType.DMA((3,)), *[_hbm_like(l) for l in lands]),
        in_specs=[SEM, SEM, *[HBM] * n, ANY], out_specs=(SEM, SEM, *[HBM] * n),
        input_output_aliases={2 + a: 2 + a for a in range(n)},
        compiler_params=pltpu.CompilerParams(has_side_effects=_EFFECT),
    )(send_sem, recv_sem, *lands, after)
    return res[0], res[1], list(res[2:])


def ag_finish(send_sem, recv_sem, fsend, frecv, lands, after, name):
    n = len(lands)

    def body(*refs):
        send_sem, recv_sem, fsend, frecv = refs[:4]
        land = refs[4:4 + n]
        x, y, c = _coords()
        me = _dev_index(x, y, c)
        peers = _peers(x, y, c)
        for k in range(4):
            for a in range(n):
                pltpu.make_async_remote_copy(src_ref=land[a].at[me], dst_ref=land[a].at[me], send_sem=send_sem.at[k],
                                             recv_sem=recv_sem.at[k], device_id=peers[k], device_id_type=MESH).wait_send()
        sib = _dev_index(*peers[0])
        for a in range(n):
            pltpu.make_async_remote_copy(src_ref=land[a].at[sib], dst_ref=land[a].at[sib], send_sem=send_sem.at[0],
                                         recv_sem=recv_sem.at[0], device_id=peers[0], device_id_type=MESH).wait_recv()
        for k in range(1, 4):
            mine = _dev_index(*peers[k])
            theirs = _dev_index(peers[k][0], peers[k][1], 1 - c)
            for a in range(n):
                pltpu.make_async_remote_copy(src_ref=land[a].at[mine], dst_ref=land[a].at[theirs], send_sem=fsend.at[k - 1],
                                             recv_sem=frecv.at[k - 1], device_id=peers[0], device_id_type=MESH).wait()

    res = pl.pallas_call(
        body, name=name, out_shape=tuple(_hbm_like(l) for l in lands),
        in_specs=[SEM] * 4 + [HBM] * n + [ANY], out_specs=tuple([HBM] * n),
        input_output_aliases={4 + a: a for a in range(n)},
        compiler_params=pltpu.CompilerParams(has_side_effects=_EFFECT),
    )(send_sem, recv_sem, fsend, frecv, *lands, after)
    return list(res)


def rs_start(psums, name):
    n = len(psums)
    lands = [lax.empty(p.shape, p.dtype) for p in psums]

    def body(*refs):
        src, land = refs[:n], refs[n:2 * n]
        send_sem, recv_sem, token = refs[2 * n], refs[2 * n + 1], refs[-1]
        peers = _peers(*_coords())
        for k in range(3):
            for a in range(n):
                pltpu.make_async_remote_copy(src_ref=src[a].at[k], dst_ref=land[a].at[k], send_sem=send_sem.at[k],
                                             recv_sem=recv_sem.at[k], device_id=peers[k + 1], device_id_type=MESH).start()
        token[...] = jnp.zeros_like(token)

    res = pl.pallas_call(
        body, name=name,
        out_shape=(pltpu.SemaphoreType.DMA((3,)), pltpu.SemaphoreType.DMA((3,)), *[_hbm_like(p) for p in psums],
                   *[_hbm_like(l) for l in lands], _TOKEN),
        in_specs=[HBM] * (2 * n), out_specs=(SEM, SEM, *[HBM] * (2 * n), pl.BlockSpec(memory_space=pltpu.VMEM)),
        input_output_aliases={a: 2 + a for a in range(2 * n)},
        compiler_params=pltpu.CompilerParams(has_side_effects=_EFFECT),
    )(*[_in_hbm(p) for p in psums], *[_in_hbm(l) for l in lands])
    return res[0], res[1], list(res[2:2 + n]), list(res[2 + n:2 + 2 * n]), res[-1]


def rs_finish(send_sem, recv_sem, psums, lands, after, name):
    n = len(psums)

    def body(*refs):
        send_sem, recv_sem = refs[0], refs[1]
        src, land = refs[2:2 + n], refs[2 + n:2 + 2 * n]
        peers = _peers(*_coords())
        for k in range(3):
            for a in range(n):
                pltpu.make_async_remote_copy(src_ref=src[a].at[k], dst_ref=land[a].at[k], send_sem=send_sem.at[k],
                                             recv_sem=recv_sem.at[k], device_id=peers[k + 1], device_id_type=MESH).wait()

    res = pl.pallas_call(
        body, name=name, out_shape=tuple(_hbm_like(l) for l in lands),
        in_specs=[SEM, SEM] + [HBM] * (2 * n) + [ANY], out_specs=tuple([HBM] * n),
        input_output_aliases={2 + n + a: a for a in range(n)},
        compiler_params=pltpu.CompilerParams(has_side_effects=_EFFECT),
    )(send_sem, recv_sem, *psums, *lands, after)
    return list(res)


def d2d_start(grads, name):
    n = len(grads)
    lands = [lax.empty((4,) + g.shape[1:], g.dtype) for g in grads]

    def body(*refs):
        src, land = refs[:n], refs[n:2 * n]
        send_sem, recv_sem, token = refs[2 * n], refs[2 * n + 1], refs[-1]
        x, y, c = _coords()
        for a in range(n):
            for k in range(4):
                blk = _dev_index(x ^ (k & 1), y ^ (k >> 1), 1 - c)
                pltpu.make_async_remote_copy(src_ref=src[a].at[blk], dst_ref=land[a].at[k], send_sem=send_sem.at[0],
                                             recv_sem=recv_sem.at[0], device_id=(x, y, 1 - c), device_id_type=MESH).start()
        token[...] = jnp.zeros_like(token)

    res = pl.pallas_call(
        body, name=name,
        out_shape=(pltpu.SemaphoreType.DMA((1,)), pltpu.SemaphoreType.DMA((1,)), *[_hbm_like(g) for g in grads],
                   *[_hbm_like(l) for l in lands], _TOKEN),
        in_specs=[HBM] * (2 * n), out_specs=(SEM, SEM, *[HBM] * (2 * n), pl.BlockSpec(memory_space=pltpu.VMEM)),
        input_output_aliases={a: 2 + a for a in range(2 * n)},
        compiler_params=pltpu.CompilerParams(has_side_effects=_EFFECT),
    )(*[_in_hbm(g) for g in grads], *[_in_hbm(l) for l in lands])
    return res[0], res[1], list(res[2:2 + n]), list(res[2 + n:2 + 2 * n]), res[-1]


def d2d_finish(send_sem, recv_sem, grads, lands, after, name):
    n = len(grads)

    def body(*refs):
        send_sem, recv_sem = refs[0], refs[1]
        src, land = refs[2:2 + n], refs[2 + n:2 + 2 * n]
        x, y, c = _coords()
        for a in range(n):
            for k in range(4):
                blk = _dev_index(x ^ (k & 1), y ^ (k >> 1), 1 - c)
                pltpu.make_async_remote_copy(src_ref=src[a].at[blk], dst_ref=land[a].at[k], send_sem=send_sem.at[0],
                                             recv_sem=recv_sem.at[0], device_id=(x, y, 1 - c), device_id_type=MESH).wait()

    res = pl.pallas_call(
        body, name=name, out_shape=tuple(_hbm_like(t) for t in list(grads) + list(lands)),
        in_specs=[SEM, SEM] + [HBM] * (2 * n) + [ANY], out_specs=tuple([HBM] * (2 * n)),
        input_output_aliases={2 + a: a for a in range(2 * n)},
        compiler_params=pltpu.CompilerParams(has_side_effects=_EFFECT),
    )(send_sem, recv_sem, *grads, *lands, after)
    return list(res[:n]), list(res[n:])


def pair_add(blk_idx, g, recv, name):
    _, r, c = g.shape
    tr = _row_tile(r)

    def body(idx_ref, g0, g1, g2, g3, r_ref, own_ref, oth_ref):
        own_ref[...] = g0[...].astype(F32) + r_ref[0].astype(F32)
        for k, gk in enumerate((g1, g2, g3)):
            oth_ref[k] = (gk[...].astype(F32) + r_ref[k + 1].astype(F32)).astype(BF16)

    def blk(k):
        return pl.BlockSpec((None, tr, c), lambda t, idx: (idx[k], t, 0))

    grid_spec = pltpu.PrefetchScalarGridSpec(
        num_scalar_prefetch=1, grid=(r // tr,),
        in_specs=[blk(0), blk(1), blk(2), blk(3), pl.BlockSpec((4, tr, c), lambda t, idx: (0, t, 0))],
        out_specs=[pl.BlockSpec((tr, c), lambda t, idx: (t, 0)), pl.BlockSpec((3, tr, c), lambda t, idx: (0, t, 0))])
    return pl.pallas_call(
        body, name=name, grid_spec=grid_spec,
        out_shape=[jax.ShapeDtypeStruct((r, c), F32), jax.ShapeDtypeStruct((3, r, c), BF16)],
        compiler_params=_cparams(1))(blk_idx, g, g, g, g, recv)


def _row_tile(r):
    return next(t for t in (512, 352, 256, 128) if r % t == 0)


def _adamw(w, g, m, v):
    m = ADAM_B1 * m + (1.0 - ADAM_B1) * g
    v = ADAM_B2 * v + (1.0 - ADAM_B2) * (g * g)
    m_hat = m / (1.0 - ADAM_B1 ** ADAM_STEP)
    v_hat = v / (1.0 - ADAM_B2 ** ADAM_STEP)
    delta = -ADAM_LR * (m_hat / (jnp.sqrt(v_hat) + ADAM_EPS) + ADAM_WD * w)
    return delta, m, v


def adamw_shard(own, recv, w, m, v, li, prev, name):
    r, c = own.shape
    tr = _row_tile(r)

    def body(own_ref, recv_ref, w_ref, m_ref, v_ref, p0, p1, p2, p3, g_ref, d_ref, nm_ref, nv_ref):
        g = own_ref[...] + recv_ref[0].astype(F32) + recv_ref[1].astype(F32) + recv_ref[2].astype(F32)
        delta, nm, nv = _adamw(w_ref[...], g, m_ref[...], v_ref[...])
        g_ref[...] = g
        d_ref[...] = delta
        nm_ref[...] = nm
        nv_ref[...] = nv

    lay = pl.BlockSpec((None, tr, c), lambda t: (li, t, 0))
    stack = jax.ShapeDtypeStruct((DEPTH, r, c), F32)
    return pl.pallas_call(
        body, name=name, grid=(r // tr,),
        in_specs=[pl.BlockSpec((tr, c), lambda t: (t, 0)), pl.BlockSpec((3, tr, c), lambda t: (0, t, 0)),
                  lay, lay, lay, ANY, ANY, ANY, ANY],
        out_specs=[lay] * 4, out_shape=[stack] * 4,
        input_output_aliases={5: 0, 6: 1, 7: 2, 8: 3},
        compiler_params=_cparams(1))(own, recv, w, m, v, *prev)


def sum_partials(gathered, name):
    _, r, c = gathered.shape
    tr = next(t for t in (96, 88, 64, _PACK_TILE) if r % t == 0)

    def body(gs_ref, g_ref):
        g = gs_ref[0]
        for d in range(1, N_DEV):
            g = g + gs_ref[d]
        g_ref[...] = g

    return pl.pallas_call(
        body, name=name, grid=(r // tr,),
        in_specs=[pl.BlockSpec((N_DEV, tr, c), lambda t: (0, t, 0))],
        out_specs=pl.BlockSpec((tr, c), lambda t: (t, 0)), out_shape=jax.ShapeDtypeStruct((r, c), F32),
        compiler_params=_cparams(1))(gathered)


def adamw_plain(g, w, m, v, name):
    def body(g_ref, w_ref, m_ref, v_ref, d_ref, nm_ref, nv_ref):
        delta, nm, nv = _adamw(w_ref[...], g_ref[...], m_ref[...], v_ref[...])
        d_ref[...] = delta
        nm_ref[...] = nm
        nv_ref[...] = nv

    return pl.pallas_call(body, name=name, out_shape=[jax.ShapeDtypeStruct(w.shape, F32)] * 3)(g, w, m, v)


_PACK_LAYER = (("b_in", (N_PROJ,)), ("rpb", (N_HEADS, 2 * KH - 1, 2 * KW - 1)), ("pool_w", (4, PGD, PGD)),
               ("pool_scale", (D_POOL,)), ("ln1_g", (D,)), ("ln1_b", (D,)), ("conv_b", (D_FF,)), ("ln2_g", (D,)),
               ("ln2_b", (D,)), ("conv_w", (3, D_FF)))
_PACK_INPUT = (("ln_in_g", (D,)), ("ln_in_b", (D,)))
_PACK_LANES = 1024
_PACK_TILE = 8
_EARLY = tuple(range(1, DEPTH))


def _pack_items(layers):
    items = [(n, (len(layers),) + s) for n, s in _PACK_LAYER]
    return items + ([(n, s) for n, s in _PACK_INPUT] if 0 in layers else [])


def _pack(parts, layers):
    flats = [(parts[name] if (name, shape) in _PACK_INPUT else jnp.stack([parts[name][li] for li in layers]))
             .reshape(-1).astype(F32) for name, shape in _pack_items(layers)]
    used = sum(f.shape[0] for f in flats)
    tile = _PACK_TILE * _PACK_LANES
    total = -(-used // tile) * tile
    return jnp.concatenate(flats + [jnp.zeros((total - used,), F32)]).reshape(total // _PACK_LANES, _PACK_LANES)


def _unpack(packed, layers):
    flat, out, off = packed.reshape(-1), {}, 0
    for name, shape in _pack_items(layers):
        n = int(np.prod(shape))
        out[name] = flat[off:off + n].reshape(shape)
        off += n
    return out


def _bias_tables(rpb):
    qc = np.arange(GRID_W)[:, None]
    kc = np.arange(GRID_W)[None, :]
    start = np.clip(qc - KW // 2, 0, GRID_W - KW)
    valid = (kc >= start) & (kc < start + KW)
    col = np.clip(kc - qc, -(KW - 1), KW - 1) + KW - 1
    onehot = (col.reshape(-1)[None, :] == np.arange(2 * KW - 1)[:, None]).astype(np.float32)
    depth = rpb.shape[0]
    rows = jnp.pad(rpb, ((0, 0), (0, 0), (0, 1), (0, 0)))
    tab = jnp.einsum("lhij,jm->lhim", rows, jnp.asarray(onehot), precision=lax.Precision.HIGHEST)
    tab = tab.reshape(depth, N_HEADS, KROWS, GRID_W, GRID_W).transpose(0, 1, 3, 2, 4)
    ok = valid[:, None, :] & (np.arange(KROWS) < 2 * KH - 1)[None, :, None]
    tab = jnp.where(jnp.asarray(ok), tab, NEG_INF).reshape(depth, N_HEADS, GRID_W, KB)
    tab = jnp.stack([tab, jnp.roll(tab, GRID_W, axis=-1)], axis=2)
    return tab, tab[:, :, :, ::-1, :]


_SHARDED = ("w_in", "w_attn_out", "w_pool_out", "w_mix_out", "w_up", "w_down", "w_ple_gate", "w_ple_proj")
_NAMES = ("ln_in_g", "ln_in_b", "w_in", "b_in", "rpb", "w_attn_out", "pool_w", "pool_scale", "w_pool_out", "w_mix_out",
          "ln1_g", "ln1_b", "w_up", "conv_w", "conv_b", "w_down", "w_ple_gate", "w_ple_proj", "ln2_g", "ln2_b")


def kernel(x, p, ln_in_g, ln_in_b, w_in, b_in, rpb, w_attn_out, pool_w, pool_scale, w_pool_out, w_mix_out, ln1_g, ln1_b, w_up, conv_w, conv_b, w_down, w_ple_gate, w_ple_proj, ln2_g, ln2_b, loss_target, m_ln_in_g, m_ln_in_b, m_w_in, m_b_in, m_rpb, m_w_attn_out, m_pool_w, m_pool_scale, m_w_pool_out, m_w_mix_out, m_ln1_g, m_ln1_b, m_w_up, m_conv_w, m_conv_b, m_w_down, m_w_ple_gate, m_w_ple_proj, m_ln2_g, m_ln2_b, v_ln_in_g, v_ln_in_b, v_w_in, v_b_in, v_rpb, v_w_attn_out, v_pool_w, v_pool_scale, v_w_pool_out, v_w_mix_out, v_ln1_g, v_ln1_b, v_w_up, v_conv_w, v_conv_b, v_w_down, v_w_ple_gate, v_w_ple_proj, v_ln2_g, v_ln2_b):
    a = dict(locals())
    W = {n: a[n] for n in _NAMES}
    M = {n: a["m_" + n] for n in _NAMES}
    V = {n: a["v_" + n] for n in _NAMES}
    xi, yi, ci = _coords()
    me = _dev_index(xi, yi, ci)
    x2, tgt = x[0], loss_target[0]
    pb = p[:, 0].astype(BF16)

    flip = lambda d: {**d, "w_up": d["w_up"].transpose(0, 2, 1)}
    ex = _Exchange(flip(W), flip(M), flip(V))
    loss_part, dx, parts = _local_step(x2, tgt, pb, W, ex)
    loss = lax.psum(loss_part[0, 0], AXES)

    started = ex.replicated_start("late", _pack(parts, (0,)), dx)
    done = ex.update(range(DEPTH - 1, 0, -1), started)
    ex.replicated_forward("late", done)
    done = ex.update((0,), done)
    stacks = {**ex.stacks, "w_up": [t.transpose(0, 2, 1) for t in ex.stacks["w_up"]]}
    lo, hi = [_unpack(sum_partials(ex.replicated_finish(tag, done), f"sum_replicated_{tag}"), layers)
              for layers, tag in (((0,), "late"), (_EARLY, "early"))]
    grads = {**{n: jnp.concatenate([lo[n], hi[n]]) for n, _ in _PACK_LAYER}, **{n: lo[n] for n, _ in _PACK_INPUT}}
    grads["conv_w"] = lax.dynamic_slice_in_dim(grads["conv_w"], me * FF_SHARD, FF_SHARD, axis=2)
    res = [{n: stacks[n][k] for n in _SHARDED} for k in range(4)]
    for n, g in grads.items():
        two_d = lambda t: t.reshape(-1, t.shape[-1])
        outs = adamw_plain(two_d(g), two_d(W[n]), two_d(M[n]), two_d(V[n]), f"adamw_{n}")
        for d, o in zip(res, [g] + [o.reshape(W[n].shape) for o in outs]):
            d[n] = o
    return (loss, dx[None], *[res[k][n] for k in range(4) for n in _NAMES])


class _Exchange:
    GROUPS = (("w_ple_gate", "w_ple_proj", "w_down", "w_up"), ("w_mix_out", "w_attn_out", "w_pool_out"), ("w_in",))
    FIRST = ("w_in",)

    def __init__(self, W, M, V):
        self.W, self.M, self.V = W, M, V
        xi, yi, ci = _coords()
        me = _dev_index(xi, yi, ci)
        self.me = me.astype(I32).reshape(1)
        self.rel_idx = jnp.stack([_dev_index(xi ^ (k & 1), yi ^ (k >> 1), ci) for k in range(4)]).astype(I32)
        self.lands = [{n: lax.dynamic_update_index_in_dim(lax.empty((N_DEV,) + W[n].shape[1:], BF16),
                                                          W[n][li].astype(BF16), me, 0) for n in _SHARDED}
                      for li in range(DEPTH)]
        cw_land = lax.dynamic_update_index_in_dim(lax.empty((N_DEV,) + W["conv_w"].shape, F32), W["conv_w"], me, 0)
        self.ag, self.fwd, self.rs, self.pending, self.small = {}, {}, {}, {}, {}
        self.stacks = {n: [lax.empty((DEPTH,) + W[n].shape[1:], F32) for _ in range(4)] for n in _SHARDED}
        self.late = tuple(n for n in _SHARDED if n not in self.FIRST)
        self.ag[0] = ag_start([self.lands[0][n] for n in self.FIRST] + [cw_land], W["conv_w"], "ag_start0")

    def tokens(self):
        return [self.ag[0][3]]

    def prefetch(self, li, after):
        send, recv, lands, _ = self.ag[li]
        self.fwd[li] = ag_forward(send, recv, lands, after, f"ag_forward{li}")
        if li == 0:
            self.ag["0b"] = ag_start([self.lands[0][n] for n in self.late], self.fwd[0][2][0], "ag_start0b")

    def weights(self, li, after):
        send, recv, _, _ = self.ag.pop(li)
        fsend, frecv, lands = self.fwd.pop(li)
        lands = ag_finish(send, recv, fsend, frecv, lands, after, f"ag_finish{li}")
        if li == 0:
            self.cw = lands[-1].transpose(1, 2, 0, 3).reshape(DEPTH, 3, 4, FF_BLK).transpose(0, 2, 1, 3)
            return dict(zip(self.FIRST, lands)), self.cw[li], (self.ag["0b"][3],)
        tokens = ()
        if li + 1 < DEPTH:
            self.ag[li + 1] = ag_start([self.lands[li + 1][n] for n in _SHARDED], lands[0], f"ag_start{li + 1}")
            tokens = (self.ag[li + 1][3],)
        return dict(zip(_SHARDED, lands)), self.cw[li], tokens

    def rest(self, li, G, mid, after):
        if li != 0:
            return G, ()
        send, recv, lands, _ = self.ag.pop("0b")
        fsend, frecv, lands = ag_forward(send, recv, lands, mid, "ag_forward0b")
        lands = ag_finish(send, recv, fsend, frecv, lands, after, "ag_finish0b")
        self.ag[1] = ag_start([self.lands[1][n] for n in _SHARDED], lands[0], "ag_start1")
        return {**G, **dict(zip(self.late, lands))}, (self.ag[1][3],)

    def grads(self, li, group, gw):
        self.pending.setdefault(li, {}).update(gw)
        if li != 0 and group != len(self.GROUPS) - 1:
            return None
        gw = self.pending.pop(li)
        tag = f"{li}_{group}" if li == 0 else f"{li}"
        send, recv, glist, lands, token = d2d_start(list(gw.values()), f"d2d_start{tag}")
        self.d2d = (tag, tuple(gw), send, recv, glist, lands)
        return token

    def flush(self, li, group, after):
        if li != 0 and group != len(self.GROUPS) - 1:
            return None
        tag, names, send, recv, glist, lands = self.d2d
        glist, recv1 = d2d_finish(send, recv, glist, lands, after, f"d2d_finish{tag}")
        sums = [pair_add(self.rel_idx, g, r1, f"pair_add_{n}{li}") for n, g, r1 in zip(names, glist, recv1)]
        send, recv, psums, lands, token = rs_start([s_[1] for s_ in sums], f"rs_start{tag}")
        self.rs.setdefault(li, []).append((tag, names, send, recv, psums, lands, [s_[0] for s_ in sums]))
        if li == 0 and group == 1 and "early" in self.small:
            self.replicated_forward("early", token)
        return token

    def update(self, layers, after):
        for li in layers:
            for tag, names, send, recv, psums, lands, owns in self.rs.pop(li):
                recv2 = rs_finish(send, recv, psums, lands, after, f"rs_finish{tag}")
                for n, own, r2 in zip(names, owns, recv2):
                    self.stacks[n] = adamw_shard(own, r2, self.W[n], self.M[n], self.V[n], li, self.stacks[n],
                                                 f"adamw_{n}{li}")
                    after = self.stacks[n][0]
        return after

    def replicated_start(self, tag, pack, after):
        land = lax.dynamic_update_index_in_dim(lax.empty((N_DEV,) + pack.shape, F32), pack, self.me[0], 0)
        self.small[tag] = ag_start([land], after, f"ag_start_small_{tag}")
        return self.small[tag][3]

    def replicated_early(self, small, after):
        return self.replicated_start("early", _pack(small, _EARLY), after)

    def replicated_forward(self, tag, after):
        send, recv, lands, _ = self.small[tag]
        self.small[tag] = (send, recv) + ag_forward(send, recv, lands, after, f"ag_forward_small_{tag}")

    def replicated_finish(self, tag, after):
        send, recv, fsend, frecv, lands = self.small.pop(tag)
        return ag_finish(send, recv, fsend, frecv, lands, after, f"ag_finish_small_{tag}")[0]


def _local_step(x2, tgt, pb, W, ex):
    depth = W["rpb"].shape[0]
    vec = lambda t: t.reshape(1, -1)
    ln1_g, ln1_b, ln2_g, ln2_b = W["ln1_g"], W["ln1_b"], W["ln2_g"], W["ln2_b"]
    b_in, rpb, pool_scale = W["b_in"], W["rpb"], W["pool_scale"]
    cb_full = W["conv_b"].reshape(depth, 4, 1, FF_BLK)
    pool_w_b = W["pool_w"].astype(BF16)
    e_tab, e_rev = _bias_tables(rpb)

    h, hb = ln_fwd(x2, vec(W["ln_in_g"]), vec(W["ln_in_b"]), "ln_in", after=ex.tokens())
    ex.prefetch(0, hb)
    saved = []
    for li in range(depth):
        G, cw, tokens = ex.weights(li, hb)
        bias = vec(b_in[li])
        proj, u = proj_fwd(hb, G["w_in"], bias, li, f"proj{li}", after=tokens)
        att = attn_fwd(proj, e_tab, li, f"attn{li}")
        pm, pw = pool_fwd(u, pool_w_b[li], vec(pool_scale[li]), f"pool{li}")
        G, tokens = ex.rest(li, G, att, pw)
        mg, ya, yp = merge_fwd(att, pw, G["w_attn_out"], G["w_pool_out"], proj, li, f"merge{li}", after=tokens)
        if li + 1 < depth:
            ex.prefetch(li + 1, mg)
        z1, h1, h1b = mix_ln_fwd(mg, G["w_mix_out"], h, vec(ln1_g[li]), vec(ln1_b[li]), li, f"mix_ln{li}")
        up = up_fwd(h1b, G["w_up"], li, f"up{li}")
        t = ffn_act_fwd(up, cw, cb_full[li], f"ffn_act{li}")
        z2, h2, h2b, pg, pp = down_ple_ln_fwd(t, G["w_down"], h1b, G["w_ple_gate"], pb[li], G["w_ple_proj"], h1,
                                              vec(ln2_g[li]), vec(ln2_b[li]), li, f"down_ln{li}")
        saved.append(dict(hb=hb, proj=proj, att=att, pm=pm, pw=pw, mg=mg, ya=ya, yp=yp, z1=z1, h1b=h1b, up=up, t=t,
                          z2=z2, pg=pg, pp=pp, G=G, cw=cw))
        h, hb = h2, h2b

    dh, loss_part = loss_bwd(h, tgt, "loss")
    small = {n: [None] * depth for n in ("b_in", "rpb", "pool_w", "pool_scale", "ln1_g", "ln1_b", "conv_b", "ln2_g",
                                         "ln2_b", "conv_w")}
    token = ()
    tok = lambda t: () if t is None else (t,)
    for li in reversed(range(depth)):
        sv = saved[li]
        G, cw = sv["G"], sv["cw"]
        dz2, dz2b, dpg, dpp, dg2, db2 = ln2_ple_bwd(dh, sv["z2"], vec(ln2_g[li]), sv["pg"], sv["pp"], f"ln2_bwd{li}",
                                                    after=token)
        gw = {}
        gw["w_ple_gate"], gw["w_ple_proj"] = wgrad_pair(sv["h1b"], dpg, [(pb[li], dpp)], f"dw_ple{li}")
        gw["w_down"] = wgrad_down(sv["t"], dz2b, f"dw_down{li}").reshape(N_DEV, FF_SHARD, D)
        dhv, dhg, dcw, dcb = ffn_act_bwd(dz2b, G["w_down"], sv["up"], cw, cb_full[li], li, f"ffn_bwd{li}")
        gw["w_up"] = wgrad_up(sv["h1b"], dhv, dhg, f"dw_up{li}")
        token = tok(ex.grads(li, 0, gw))
        dz1, dz1b, dg1, db1 = dh1_ln1_bwd(dz2, dpg, G["w_ple_gate"], dhv, dhg, G["w_up"], sv["z1"], vec(ln1_g[li]), li,
                                          f"ln1_bwd{li}", after=token)
        token = tok(ex.flush(li, 0, dz1b))
        dya, dyp, dga, dgb = merge_bwd(dz1b, G["w_mix_out"], sv["proj"], sv["ya"], sv["yp"], li, f"merge_bwd{li}",
                                       after=token)
        gw = dict(zip(("w_mix_out", "w_attn_out", "w_pool_out"),
                      wgrad_pair(sv["mg"], dz1b, [(sv["att"], dya), (sv["pw"], dyp)], f"dw_out{li}")))
        token = tok(ex.grads(li, 1, gw))
        da = attn_out_bwd(dya, G["w_attn_out"], li, f"da{li}", after=token)
        du, dpool_w, dpool_sc = pool_bwd(dyp, G["w_pool_out"], sv["pm"], pool_w_b[li], vec(pool_scale[li]), li,
                                         f"pool_bwd{li}")
        token = tok(ex.flush(li, 1, du))
        dq, dk, dv, drpb = attn_bwd(sv["proj"], da, e_rev, li, f"attn_bwd{li}", after=token)
        dproj = [dq, dk, dv, du, dga, dgb]
        dw_in, db_in = wgrad_in(sv["hb"], dproj, f"dw_in{li}")
        token = tok(ex.grads(li, 2, {"w_in": dw_in}))
        dh = dh0_bwd(dz1, dproj, G["w_in"], li, f"dh0{li}", after=token)
        small["b_in"][li] = db_in.reshape(N_PROJ)
        small["rpb"][li] = drpb.reshape(N_HEADS, KROWS, GRID_W)[:, :2 * KH - 1, :2 * KW - 1]
        small["pool_w"][li] = dpool_w
        small["pool_scale"][li] = dpool_sc.reshape(D_POOL)
        small["ln1_g"][li], small["ln1_b"][li] = dg1.reshape(D), db1.reshape(D)
        small["ln2_g"][li], small["ln2_b"][li] = dg2.reshape(D), db2.reshape(D)
        small["conv_b"][li] = dcb.reshape(D_FF)
        small["conv_w"][li] = dcw.transpose(1, 0, 2).reshape(3, D_FF)
        token = tok(ex.flush(li, 2, dh))
        if li == 1:
            token = token + tok(ex.replicated_early(small, dh))
    dx, dg_in, db_in0 = ln_bwd(dh, x2, vec(W["ln_in_g"]), "ln_in_bwd", after=token)
    parts = {n: jnp.stack(v_) for n, v_ in small.items()}
    parts["ln_in_g"], parts["ln_in_b"] = dg_in.reshape(D), db_in0.reshape(D)
    return loss_part, dx, parts
```
